```python
import jax, jax.numpy as jnp
from jax import lax
import numpy as np

D_MODEL = 1024
BATCH = 8
SEQ = 4096
DEPTH = 1

GRID_W = 64
CTX_LEN = 256
D_SSD = 1024
SSD_HEAD_DIM = 64
SSD_HEADS = D_SSD // SSD_HEAD_DIM
SSD_GROUPS = 2
SSD_HPG = SSD_HEADS // SSD_GROUPS
SSD_STATE = 128
SSD_CONV = 5
SSD_CHUNK = 128
SSD_GN = SSD_GROUPS * SSD_STATE
XBC_DIM = D_SSD + 2 * SSD_GN
D_GM = 1024
GM_GROUPS = 8
GM_GROUP_DIM = D_GM // GM_GROUPS
GM_CHUNK = 128
ROWS_PER_CHUNK = GM_CHUNK // GRID_W
D_FF = -(-8 * D_MODEL // (3 * 256)) * 256
D_PROJ = D_SSD + XBC_DIM + 2 * SSD_HEADS + 2 * D_GM + 2 * D_MODEL
ALPHA = (2 * DEPTH) ** 0.25
BETA = (8 * DEPTH) ** -0.25
LN_EPS = 1e-5

kernel_name = 'hybrid_ssd_gmlp_dit_block'


def layer_norm(x, g, b):
    xf = x.astype(jnp.float32)
    mu = jnp.mean(xf, axis=-1, keepdims=True)
    var = jnp.mean(jnp.square(xf - mu), axis=-1, keepdims=True)
    return ((xf - mu) * lax.rsqrt(var + LN_EPS) * g + b).astype(x.dtype)


def gated_rms_norm(y, z, g):
    h = (y * jax.nn.silu(z)).astype(jnp.float32)
    h = h * lax.rsqrt(jnp.mean(jnp.square(h), axis=-1, keepdims=True) + LN_EPS)
    return (h * g).astype(y.dtype)


def dwconv_centred(x, w, b):
    ch = x.shape[-1]
    pad = SSD_CONV // 2
    y = lax.conv_general_dilated(x, w[:, None, :], window_strides=(1,), padding=[(pad, pad)],
                                 dimension_numbers=('NWC', 'WIO', 'NWC'), feature_group_count=ch)
    return y + b


def split_proj(p):
    o1 = D_SSD
    o2 = o1 + XBC_DIM
    o3 = o2 + 2 * SSD_HEADS
    o4 = o3 + D_GM
    o5 = o4 + D_GM
    return p[..., :o1], p[..., o1:o2], p[..., o2:o3], p[..., o3:o4], p[..., o4:o5], p[..., o5:]


def ssd_scan(xs, dt, A, Bm, Cm, h0):
    bsz, L, _ = xs.shape
    nc = L // SSD_CHUNK
    x = xs.reshape(bsz, nc, SSD_CHUNK, SSD_GROUPS, SSD_HPG, SSD_HEAD_DIM)
    dt = dt.reshape(bsz, nc, SSD_CHUNK, SSD_GROUPS, SSD_HPG)
    Bm = Bm.reshape(bsz, nc, SSD_CHUNK, SSD_GROUPS, SSD_STATE)
    Cm = Cm.reshape(bsz, nc, SSD_CHUNK, SSD_GROUPS, SSD_STATE)
    acum = jnp.cumsum(dt * A.reshape(SSD_GROUPS, SSD_HPG), axis=2)
    seg = acum[:, :, :, None] - acum[:, :, None, :]
    tri = jnp.tril(jnp.ones((SSD_CHUNK, SSD_CHUNK), dtype=bool))[:, :, None, None]
    lmat = jnp.where(tri, jnp.exp(jnp.where(tri, seg, 0.0)), 0.0)
    cb = jnp.einsum('bcign,bcjgn->bcijg', Cm, Bm)
    wgt = cb[..., None] * lmat * dt[:, :, None]
    y_diag = jnp.einsum('bcijgh,bcjghp->bcighp', wgt, x)
    decay_to_end = jnp.exp(acum[:, :, -1:] - acum)
    states = jnp.einsum('bcjgn,bcjgh,bcjghp->bcghpn', Bm, decay_to_end * dt, x)
    chunk_decay = jnp.exp(acum[:, :, -1])

    def step(h, inp):
        dec, st = inp
        h_new = (dec[..., None, None] * h + st).astype(h.dtype)
        return h_new, h

    h_final, h_prev = lax.scan(step, h0.astype(states.dtype),
                               (jnp.moveaxis(chunk_decay, 1, 0), jnp.moveaxis(states, 1, 0)))
    h_prev = jnp.moveaxis(h_prev, 0, 1)
    y_off = jnp.einsum('bcign,bcghpn,bcigh->bcighp', Cm, h_prev, jnp.exp(acum))
    return (y_diag + y_off).reshape(bsz, L, D_SSD), h_final


def ssd_branch(z, xbc, dt_raw, conv_w, conv_b, dt_bias, a_log, d_skip, norm_g, h0_f, h0_b):
    bsz, L, _ = xbc.shape
    xbc = jax.nn.silu(dwconv_centred(xbc, conv_w, conv_b))
    xs, Bm, Cm = xbc[..., :D_SSD], xbc[..., D_SSD:D_SSD + SSD_GN], xbc[..., D_SSD + SSD_GN:]
    dt = jax.nn.softplus(dt_raw.reshape(bsz, L, 2, SSD_HEADS) + dt_bias)
    A = -jnp.exp(a_log)
    y_f, s_f = ssd_scan(xs, dt[:, :, 0], A[0], Bm, Cm, h0_f)
    y_b, s_b = ssd_scan(jnp.flip(xs, 1), jnp.flip(dt[:, :, 1], 1), A[1],
                        jnp.flip(Bm, 1), jnp.flip(Cm, 1), h0_b)
    skip = (xs.reshape(bsz, L, SSD_HEADS, SSD_HEAD_DIM) * (d_skip[0] + d_skip[1])[:, None]).reshape(bsz, L, D_SSD)
    y = y_f + jnp.flip(y_b, 1) + skip
    return gated_rms_norm(y, z, norm_g), s_f, s_b


def spatial_gating(u, v, n_chunks, g, b, w_s, b_s):
    bsz = u.shape[0]
    v = layer_norm(v, g, b).reshape(bsz, n_chunks, GM_CHUNK, GM_GROUPS, GM_GROUP_DIM)
    mixed = jnp.einsum('gpq,bnqgc->bnpgc', w_s, v) + b_s.T[:, :, None]
    return u * mixed.reshape(u.shape)


def branch_merge(y_ssd, u, v, gates, n_chunks, gm_g, gm_b, w_s, b_s, b_gate, w_ssd_proj, w_gm_proj, w_out):
    y_gm = spatial_gating(jax.nn.gelu(u), jax.nn.gelu(v), n_chunks, gm_g, gm_b, w_s, b_s)
    g = jax.nn.sigmoid(gates + b_gate)
    merged = g[..., :D_MODEL] * (y_ssd @ w_ssd_proj) + g[..., D_MODEL:] * (y_gm @ w_gm_proj)
    return merged @ w_out


def swiglu(h, w1, w3, w2):
    return (jax.nn.silu(h @ w1) * (h @ w3)) @ w2


def _fwd_setup_inputs(seed: int = 0) -> dict:
    key = jax.random.key(seed)
    ks = jax.random.split(key, 32)
    f32 = jnp.float32

    def nrm(k, shape, scale=1.0):
        return jax.random.normal(k, shape, f32) * scale

    dt0 = jnp.exp(jax.random.uniform(ks[8], (DEPTH, 2, SSD_HEADS), f32, np.log(1e-3), np.log(1e-1)))
    return {
        'x': nrm(ks[0], (BATCH, SEQ, D_MODEL)),
        'c': nrm(ks[1], (BATCH, D_MODEL)),
        'ctx': nrm(ks[2], (BATCH, CTX_LEN, D_MODEL)),
        'c_ctx': nrm(ks[3], (D_MODEL,)),
        'ln0_g': 1.0 + nrm(ks[4], (D_MODEL,), 0.01),
        'ln0_b': nrm(ks[5], (D_MODEL,), 0.01),
        'w_ada': nrm(ks[6], (DEPTH, D_MODEL, 6 * D_MODEL), 0.3 * D_MODEL ** -0.5),
        'b_ada': nrm(ks[7], (DEPTH, 6 * D_MODEL), 0.01),
        'w_in': nrm(ks[9], (DEPTH, D_MODEL, D_PROJ), D_MODEL ** -0.5),
        'conv_w': nrm(ks[10], (DEPTH, SSD_CONV, XBC_DIM), SSD_CONV ** -0.5),
        'conv_b': nrm(ks[11], (DEPTH, XBC_DIM), 0.01),
        'dt_bias': dt0 + jnp.log(-jnp.expm1(-dt0)),
        'a_log': jnp.log(jax.random.uniform(ks[12], (DEPTH, 2, SSD_HEADS), f32, 1.0, 16.0)),
        'd_skip': 1.0 + nrm(ks[13], (DEPTH, 2, SSD_HEADS), 0.01),
        'ssd_norm_g': 1.0 + nrm(ks[14], (DEPTH, D_SSD), 0.01),
        'gm_norm_g': 1.0 + nrm(ks[15], (DEPTH, D_GM), 0.01),
        'gm_norm_b': nrm(ks[16], (DEPTH, D_GM), 0.01),
        'w_spatial': nrm(ks[17], (DEPTH, GM_GROUPS, GM_CHUNK, GM_CHUNK), GM_CHUNK ** -0.5),
        'b_spatial': 1.0 + nrm(ks[18], (DEPTH, GM_GROUPS, GM_CHUNK), 0.01),
        'b_gate': nrm(ks[19], (DEPTH, 2 * D_MODEL), 0.01),
        'w_ssd_proj': nrm(ks[20], (DEPTH, D_SSD, D_MODEL), BETA * D_SSD ** -0.5),
        'w_gm_proj': nrm(ks[21], (DEPTH, D_GM, D_MODEL), BETA * D_GM ** -0.5),
        'w_out': nrm(ks[22], (DEPTH, D_MODEL, D_MODEL), BETA * D_MODEL ** -0.5),
        'ln1_g': 1.0 + nrm(ks[23], (DEPTH, D_MODEL), 0.01),
        'ln1_b': nrm(ks[24], (DEPTH, D_MODEL), 0.01),
        'w_ff1': nrm(ks[25], (DEPTH, D_MODEL, D_FF), BETA * D_MODEL ** -0.5),
        'w_ff3': nrm(ks[26], (DEPTH, D_MODEL, D_FF), BETA * D_MODEL ** -0.5),
        'w_ff2': nrm(ks[27], (DEPTH, D_FF, D_MODEL), BETA * D_FF ** -0.5),
        'ln2_g': 1.0 + nrm(ks[28], (DEPTH, D_MODEL), 0.01),
        'ln2_b': nrm(ks[29], (DEPTH, D_MODEL), 0.01),
    }


def _fwd_reference(x, c, ctx, c_ctx, ln0_g, ln0_b, w_ada, b_ada, w_in, conv_w, conv_b, dt_bias, a_log, d_skip,
              ssd_norm_g, gm_norm_g, gm_norm_b, w_spatial, b_spatial, b_gate, w_ssd_proj, w_gm_proj, w_out,
              ln1_g, ln1_b, w_ff1, w_ff3, w_ff2, ln2_g, ln2_b):
    x = layer_norm(x, ln0_g, ln0_b)
    ctx_h = layer_norm(ctx, ln0_g, ln0_b)
    rows = x.shape[1] // GRID_W
    n_lat_chunks = rows // ROWS_PER_CHUNK
    n_ctx_chunks = ctx.shape[1] // GM_CHUNK
    zero_state = jnp.zeros((ctx.shape[0], SSD_GROUPS, SSD_HPG, SSD_HEAD_DIM, SSD_STATE), x.dtype)
    for l in range(DEPTH):
        mod_x = (jax.nn.silu(c) @ w_ada[l] + b_ada[l])[:, None, :]
        mod_c = jax.nn.silu(c_ctx) @ w_ada[l] + b_ada[l]
        sh1x, sc1x, g1x, sh2x, sc2x, g2x = jnp.split(mod_x, 6, axis=-1)
        sh1c, sc1c, g1c, sh2c, sc2c, g2c = jnp.split(mod_c, 6, axis=-1)
        ssd_p = (conv_w[l], conv_b[l], dt_bias[l], a_log[l], d_skip[l], ssd_norm_g[l])
        mrg_p = (gm_norm_g[l], gm_norm_b[l], w_spatial[l], b_spatial[l], b_gate[l],
                 w_ssd_proj[l], w_gm_proj[l], w_out[l])
        zc, xbcc, dtc, uc, vc, gc = split_proj((ctx_h * (1.0 + sc1c) + sh1c) @ w_in[l])
        yc, s_f, s_b = ssd_branch(zc, xbcc, dtc, *ssd_p, zero_state, zero_state)
        zx, xbcx, dtx, ux, vx, gx = split_proj((x * (1.0 + sc1x) + sh1x) @ w_in[l])
        yx, _, _ = ssd_branch(zx, xbcx, dtx, *ssd_p, s_f, s_b)
        out_x = branch_merge(yx, ux, vx, gx, n_lat_chunks, *mrg_p)
        x = layer_norm(ALPHA * x + g1x * out_x, ln1_g[l], ln1_b[l])
        x = layer_norm(ALPHA * x + g2x * swiglu(x * (1.0 + sc2x) + sh2x, w_ff1[l], w_ff3[l], w_ff2[l]),
                       ln2_g[l], ln2_b[l])
        if l < DEPTH - 1:
            out_c = branch_merge(yc, uc, vc, gc, n_ctx_chunks, *mrg_p)
            ctx_h = layer_norm(ALPHA * ctx_h + g1c * out_c, ln1_g[l], ln1_b[l])
            ctx_h = layer_norm(ALPHA * ctx_h + g2c * swiglu(ctx_h * (1.0 + sc2c) + sh2c, w_ff1[l], w_ff3[l], w_ff2[l]),
                               ln2_g[l], ln2_b[l])
    return x


import jax as _jax
import jax.numpy as _jnp

TWIN_FORMAT = 'train_step'
FWD_PARAMS = ['x', 'c', 'ctx', 'c_ctx', 'ln0_g', 'ln0_b', 'w_ada', 'b_ada', 'w_in', 'conv_w', 'conv_b', 'dt_bias', 'a_log', 'd_skip', 'ssd_norm_g', 'gm_norm_g', 'gm_norm_b', 'w_spatial', 'b_spatial', 'b_gate', 'w_ssd_proj', 'w_gm_proj', 'w_out', 'ln1_g', 'ln1_b', 'w_ff1', 'w_ff3', 'w_ff2', 'ln2_g', 'ln2_b']
TWIN_WEIGHTS = ['c_ctx', 'ln0_g', 'ln0_b', 'w_ada', 'b_ada', 'w_in', 'conv_w', 'conv_b', 'dt_bias', 'a_log', 'd_skip', 'ssd_norm_g', 'gm_norm_g', 'gm_norm_b', 'w_spatial', 'b_spatial', 'b_gate', 'w_ssd_proj', 'w_gm_proj', 'w_out', 'ln1_g', 'ln1_b', 'w_ff1', 'w_ff3', 'w_ff2', 'ln2_g', 'ln2_b']
TWIN_DIFF_INPUT = 'x'
TWIN_INPUTS = ['x', 'c', 'ctx', 'c_ctx', 'ln0_g', 'ln0_b', 'w_ada', 'b_ada', 'w_in', 'conv_w', 'conv_b', 'dt_bias', 'a_log', 'd_skip', 'ssd_norm_g', 'gm_norm_g', 'gm_norm_b', 'w_spatial', 'b_spatial', 'b_gate', 'w_ssd_proj', 'w_gm_proj', 'w_out', 'ln1_g', 'ln1_b', 'w_ff1', 'w_ff3', 'w_ff2', 'ln2_g', 'ln2_b', 'loss_target', 'm_c_ctx', 'm_ln0_g', 'm_ln0_b', 'm_w_ada', 'm_b_ada', 'm_w_in', 'm_conv_w', 'm_conv_b', 'm_dt_bias', 'm_a_log', 'm_d_skip', 'm_ssd_norm_g', 'm_gm_norm_g', 'm_gm_norm_b', 'm_w_spatial', 'm_b_spatial', 'm_b_gate', 'm_w_ssd_proj', 'm_w_gm_proj', 'm_w_out', 'm_ln1_g', 'm_ln1_b', 'm_w_ff1', 'm_w_ff3', 'm_w_ff2', 'm_ln2_g', 'm_ln2_b', 'v_c_ctx', 'v_ln0_g', 'v_ln0_b', 'v_w_ada', 'v_b_ada', 'v_w_in', 'v_conv_w', 'v_conv_b', 'v_dt_bias', 'v_a_log', 'v_d_skip', 'v_ssd_norm_g', 'v_gm_norm_g', 'v_gm_norm_b', 'v_w_spatial', 'v_b_spatial', 'v_b_gate', 'v_w_ssd_proj', 'v_w_gm_proj', 'v_w_out', 'v_ln1_g', 'v_ln1_b', 'v_w_ff1', 'v_w_ff3', 'v_w_ff2', 'v_ln2_g', 'v_ln2_b']
TWIN_OUTPUTS = ['loss', 'grad_x', 'grad_c_ctx', 'grad_ln0_g', 'grad_ln0_b', 'grad_w_ada', 'grad_b_ada', 'grad_w_in', 'grad_conv_w', 'grad_conv_b', 'grad_dt_bias', 'grad_a_log', 'grad_d_skip', 'grad_ssd_norm_g', 'grad_gm_norm_g', 'grad_gm_norm_b', 'grad_w_spatial', 'grad_b_spatial', 'grad_b_gate', 'grad_w_ssd_proj', 'grad_w_gm_proj', 'grad_w_out', 'grad_ln1_g', 'grad_ln1_b', 'grad_w_ff1', 'grad_w_ff3', 'grad_w_ff2', 'grad_ln2_g', 'grad_ln2_b', 'delta_c_ctx', 'delta_ln0_g', 'delta_ln0_b', 'delta_w_ada', 'delta_b_ada', 'delta_w_in', 'delta_conv_w', 'delta_conv_b', 'delta_dt_bias', 'delta_a_log', 'delta_d_skip', 'delta_ssd_norm_g', 'delta_gm_norm_g', 'delta_gm_norm_b', 'delta_w_spatial', 'delta_b_spatial', 'delta_b_gate', 'delta_w_ssd_proj', 'delta_w_gm_proj', 'delta_w_out', 'delta_ln1_g', 'delta_ln1_b', 'delta_w_ff1', 'delta_w_ff3', 'delta_w_ff2', 'delta_ln2_g', 'delta_ln2_b', 'new_m_c_ctx', 'new_m_ln0_g', 'new_m_ln0_b', 'new_m_w_ada', 'new_m_b_ada', 'new_m_w_in', 'new_m_conv_w', 'new_m_conv_b', 'new_m_dt_bias', 'new_m_a_log', 'new_m_d_skip', 'new_m_ssd_norm_g', 'new_m_gm_norm_g', 'new_m_gm_norm_b', 'new_m_w_spatial', 'new_m_b_spatial', 'new_m_b_gate', 'new_m_w_ssd_proj', 'new_m_w_gm_proj', 'new_m_w_out', 'new_m_ln1_g', 'new_m_ln1_b', 'new_m_w_ff1', 'new_m_w_ff3', 'new_m_w_ff2', 'new_m_ln2_g', 'new_m_ln2_b', 'new_v_c_ctx', 'new_v_ln0_g', 'new_v_ln0_b', 'new_v_w_ada', 'new_v_b_ada', 'new_v_w_in', 'new_v_conv_w', 'new_v_conv_b', 'new_v_dt_bias', 'new_v_a_log', 'new_v_d_skip', 'new_v_ssd_norm_g', 'new_v_gm_norm_g', 'new_v_gm_norm_b', 'new_v_w_spatial', 'new_v_b_spatial', 'new_v_b_gate', 'new_v_w_ssd_proj', 'new_v_w_gm_proj', 'new_v_w_out', 'new_v_ln1_g', 'new_v_ln1_b', 'new_v_w_ff1', 'new_v_w_ff3', 'new_v_w_ff2', 'new_v_ln2_g', 'new_v_ln2_b']
TWIN_LEAF_KINDS = {'loss': 'loss', 'grad_x': 'grad_x', 'grad_c_ctx': 'grad_w', 'grad_ln0_g': 'grad_w', 'grad_ln0_b': 'grad_w', 'grad_w_ada': 'grad_w', 'grad_b_ada': 'grad_w', 'grad_w_in': 'grad_w', 'grad_conv_w': 'grad_w', 'grad_conv_b': 'grad_w', 'grad_dt_bias': 'grad_w', 'grad_a_log': 'grad_w', 'grad_d_skip': 'grad_w', 'grad_ssd_norm_g': 'grad_w', 'grad_gm_norm_g': 'grad_w', 'grad_gm_norm_b': 'grad_w', 'grad_w_spatial': 'grad_w', 'grad_b_spatial': 'grad_w', 'grad_b_gate': 'grad_w', 'grad_w_ssd_proj': 'grad_w', 'grad_w_gm_proj': 'grad_w', 'grad_w_out': 'grad_w', 'grad_ln1_g': 'grad_w', 'grad_ln1_b': 'grad_w', 'grad_w_ff1': 'grad_w', 'grad_w_ff3': 'grad_w', 'grad_w_ff2': 'grad_w', 'grad_ln2_g': 'grad_w', 'grad_ln2_b': 'grad_w', 'delta_c_ctx': 'delta_w', 'delta_ln0_g': 'delta_w', 'delta_ln0_b': 'delta_w', 'delta_w_ada': 'delta_w', 'delta_b_ada': 'delta_w', 'delta_w_in': 'delta_w', 'delta_conv_w': 'delta_w', 'delta_conv_b': 'delta_w', 'delta_dt_bias': 'delta_w', 'delta_a_log': 'delta_w', 'delta_d_skip': 'delta_w', 'delta_ssd_norm_g': 'delta_w', 'delta_gm_norm_g': 'delta_w', 'delta_gm_norm_b': 'delta_w', 'delta_w_spatial': 'delta_w', 'delta_b_spatial': 'delta_w', 'delta_b_gate': 'delta_w', 'delta_w_ssd_proj': 'delta_w', 'delta_w_gm_proj': 'delta_w', 'delta_w_out': 'delta_w', 'delta_ln1_g': 'delta_w', 'delta_ln1_b': 'delta_w', 'delta_w_ff1': 'delta_w', 'delta_w_ff3': 'delta_w', 'delta_w_ff2': 'delta_w', 'delta_ln2_g': 'delta_w', 'delta_ln2_b': 'delta_w', 'new_m_c_ctx': 'new_m', 'new_m_ln0_g': 'new_m', 'new_m_ln0_b': 'new_m', 'new_m_w_ada': 'new_m', 'new_m_b_ada': 'new_m', 'new_m_w_in': 'new_m', 'new_m_conv_w': 'new_m', 'new_m_conv_b': 'new_m', 'new_m_dt_bias': 'new_m', 'new_m_a_log': 'new_m', 'new_m_d_skip': 'new_m', 'new_m_ssd_norm_g': 'new_m', 'new_m_gm_norm_g': 'new_m', 'new_m_gm_norm_b': 'new_m', 'new_m_w_spatial': 'new_m', 'new_m_b_spatial': 'new_m', 'new_m_b_gate': 'new_m', 'new_m_w_ssd_proj': 'new_m', 'new_m_w_gm_proj': 'new_m', 'new_m_w_out': 'new_m', 'new_m_ln1_g': 'new_m', 'new_m_ln1_b': 'new_m', 'new_m_w_ff1': 'new_m', 'new_m_w_ff3': 'new_m', 'new_m_w_ff2': 'new_m', 'new_m_ln2_g': 'new_m', 'new_m_ln2_b': 'new_m', 'new_v_c_ctx': 'new_v', 'new_v_ln0_g': 'new_v', 'new_v_ln0_b': 'new_v', 'new_v_w_ada': 'new_v', 'new_v_b_ada': 'new_v', 'new_v_w_in': 'new_v', 'new_v_conv_w': 'new_v', 'new_v_conv_b': 'new_v', 'new_v_dt_bias': 'new_v', 'new_v_a_log': 'new_v', 'new_v_d_skip': 'new_v', 'new_v_ssd_norm_g': 'new_v', 'new_v_gm_norm_g': 'new_v', 'new_v_gm_norm_b': 'new_v', 'new_v_w_spatial': 'new_v', 'new_v_b_spatial': 'new_v', 'new_v_b_gate': 'new_v', 'new_v_w_ssd_proj': 'new_v', 'new_v_w_gm_proj': 'new_v', 'new_v_w_out': 'new_v', 'new_v_ln1_g': 'new_v', 'new_v_ln1_b': 'new_v', 'new_v_w_ff1': 'new_v', 'new_v_w_ff3': 'new_v', 'new_v_w_ff2': 'new_v', 'new_v_ln2_g': 'new_v', 'new_v_ln2_b': 'new_v'}


def _forward(args):
    return _fwd_reference(*[args[k] for k in FWD_PARAMS])


def _output_shape():
    out = _jax.eval_shape(lambda: _forward(_fwd_setup_inputs(0)))
    return out.shape, out.dtype

N_MICROBATCH = 1
ADAM_LR = 0.001
ADAM_B1 = 0.9
ADAM_B2 = 0.999
ADAM_EPS = 1e-08
ADAM_WD = 0.01
ADAM_STEP = 10
PER_EXAMPLE_BATCH_AXIS = {'x': 0, 'c': 0, 'ctx': 0, 'loss_target': 0}
SHARED_INPUTS = []
_WEIGHT_DTYPES = {'c_ctx': _jnp.float32, 'ln0_g': _jnp.float32, 'ln0_b': _jnp.float32, 'w_ada': _jnp.float32, 'b_ada': _jnp.float32, 'w_in': _jnp.float32, 'conv_w': _jnp.float32, 'conv_b': _jnp.float32, 'dt_bias': _jnp.float32, 'a_log': _jnp.float32, 'd_skip': _jnp.float32, 'ssd_norm_g': _jnp.float32, 'gm_norm_g': _jnp.float32, 'gm_norm_b': _jnp.float32, 'w_spatial': _jnp.float32, 'b_spatial': _jnp.float32, 'b_gate': _jnp.float32, 'w_ssd_proj': _jnp.float32, 'w_gm_proj': _jnp.float32, 'w_out': _jnp.float32, 'ln1_g': _jnp.float32, 'ln1_b': _jnp.float32, 'w_ff1': _jnp.float32, 'w_ff3': _jnp.float32, 'w_ff2': _jnp.float32, 'ln2_g': _jnp.float32, 'ln2_b': _jnp.float32}
MOMENT_SCALE = {'c_ctx': 1.930363e-04, 'ln0_g': 4.939150e-01, 'ln0_b': 2.463020e-01, 'w_ada': 1.241967e-02, 'b_ada': 1.936484e-02, 'w_in': 4.138184e-03, 'conv_w': 4.299778e-03, 'conv_b': 6.235364e-03, 'dt_bias': 7.530791e-03, 'a_log': 1.319079e-02, 'd_skip': 1.743852e-02, 'ssd_norm_g': 5.201156e-03, 'gm_norm_g': 3.974211e-03, 'gm_norm_b': 3.623455e-03, 'w_spatial': 3.578963e-03, 'b_spatial': 3.590985e-03, 'b_gate': 1.969701e-03, 'w_ssd_proj': 9.006961e-03, 'w_gm_proj': 8.669040e-03, 'w_out': 1.253016e-02, 'ln1_g': 4.940941e-01, 'ln1_b': 2.454362e-01, 'w_ff1': 3.606424e-03, 'w_ff3': 3.513613e-03, 'w_ff2': 5.837074e-03, 'ln2_g': 3.197156e+01, 'ln2_b': 4.479832e-01}


def _to_microbatches(a, axis):
    t = _jnp.moveaxis(a, axis, 0)
    t = t.reshape((N_MICROBATCH, t.shape[0] // N_MICROBATCH) + t.shape[1:])
    return _jnp.moveaxis(t, 1, axis + 1)


def setup_inputs(seed: int = 0) -> dict:
    inp = _fwd_setup_inputs(seed)
    key = _jax.random.fold_in(_jax.random.key(seed), 7919)
    shape, _ = _output_shape()
    out = dict(inp)
    out["loss_target"] = _jax.random.normal(_jax.random.fold_in(key, 0), shape, _jnp.float32)
    for i, name in enumerate(TWIN_WEIGHTS):
        w = inp[name].astype(_jnp.float32)
        if MOMENT_SCALE is None:
            s = _jnp.sqrt(_jnp.mean(_jnp.square(w)) + 1e-30)
        else:
            s = MOMENT_SCALE[name]
        km, kv = _jax.random.split(_jax.random.fold_in(key, i + 1))
        out[name] = w
        out["m_" + name] = s * _jax.random.normal(km, w.shape, _jnp.float32)
        out["v_" + name] = (s * s) * _jax.random.uniform(kv, w.shape, _jnp.float32, 0.5, 1.5)
    if N_MICROBATCH > 1:
        for name, axis in PER_EXAMPLE_BATCH_AXIS.items():
            out[name] = _to_microbatches(out[name], axis)
    return {'x': out['x'], 'c': out['c'], 'ctx': out['ctx'], 'c_ctx': out['c_ctx'], 'ln0_g': out['ln0_g'], 'ln0_b': out['ln0_b'], 'w_ada': out['w_ada'], 'b_ada': out['b_ada'], 'w_in': out['w_in'], 'conv_w': out['conv_w'], 'conv_b': out['conv_b'], 'dt_bias': out['dt_bias'], 'a_log': out['a_log'], 'd_skip': out['d_skip'], 'ssd_norm_g': out['ssd_norm_g'], 'gm_norm_g': out['gm_norm_g'], 'gm_norm_b': out['gm_norm_b'], 'w_spatial': out['w_spatial'], 'b_spatial': out['b_spatial'], 'b_gate': out['b_gate'], 'w_ssd_proj': out['w_ssd_proj'], 'w_gm_proj': out['w_gm_proj'], 'w_out': out['w_out'], 'ln1_g': out['ln1_g'], 'ln1_b': out['ln1_b'], 'w_ff1': out['w_ff1'], 'w_ff3': out['w_ff3'], 'w_ff2': out['w_ff2'], 'ln2_g': out['ln2_g'], 'ln2_b': out['ln2_b'], 'loss_target': out['loss_target'], 'm_c_ctx': out['m_c_ctx'], 'm_ln0_g': out['m_ln0_g'], 'm_ln0_b': out['m_ln0_b'], 'm_w_ada': out['m_w_ada'], 'm_b_ada': out['m_b_ada'], 'm_w_in': out['m_w_in'], 'm_conv_w': out['m_conv_w'], 'm_conv_b': out['m_conv_b'], 'm_dt_bias': out['m_dt_bias'], 'm_a_log': out['m_a_log'], 'm_d_skip': out['m_d_skip'], 'm_ssd_norm_g': out['m_ssd_norm_g'], 'm_gm_norm_g': out['m_gm_norm_g'], 'm_gm_norm_b': out['m_gm_norm_b'], 'm_w_spatial': out['m_w_spatial'], 'm_b_spatial': out['m_b_spatial'], 'm_b_gate': out['m_b_gate'], 'm_w_ssd_proj': out['m_w_ssd_proj'], 'm_w_gm_proj': out['m_w_gm_proj'], 'm_w_out': out['m_w_out'], 'm_ln1_g': out['m_ln1_g'], 'm_ln1_b': out['m_ln1_b'], 'm_w_ff1': out['m_w_ff1'], 'm_w_ff3': out['m_w_ff3'], 'm_w_ff2': out['m_w_ff2'], 'm_ln2_g': out['m_ln2_g'], 'm_ln2_b': out['m_ln2_b'], 'v_c_ctx': out['v_c_ctx'], 'v_ln0_g': out['v_ln0_g'], 'v_ln0_b': out['v_ln0_b'], 'v_w_ada': out['v_w_ada'], 'v_b_ada': out['v_b_ada'], 'v_w_in': out['v_w_in'], 'v_conv_w': out['v_conv_w'], 'v_conv_b': out['v_conv_b'], 'v_dt_bias': out['v_dt_bias'], 'v_a_log': out['v_a_log'], 'v_d_skip': out['v_d_skip'], 'v_ssd_norm_g': out['v_ssd_norm_g'], 'v_gm_norm_g': out['v_gm_norm_g'], 'v_gm_norm_b': out['v_gm_norm_b'], 'v_w_spatial': out['v_w_spatial'], 'v_b_spatial': out['v_b_spatial'], 'v_b_gate': out['v_b_gate'], 'v_w_ssd_proj': out['v_w_ssd_proj'], 'v_w_gm_proj': out['v_w_gm_proj'], 'v_w_out': out['v_w_out'], 'v_ln1_g': out['v_ln1_g'], 'v_ln1_b': out['v_ln1_b'], 'v_w_ff1': out['v_w_ff1'], 'v_w_ff3': out['v_w_ff3'], 'v_w_ff2': out['v_w_ff2'], 'v_ln2_g': out['v_ln2_g'], 'v_ln2_b': out['v_ln2_b']}


def _loss(weights, diff, rest, loss_target):
    with _jax.named_scope("forward"):
        args = {**rest, TWIN_DIFF_INPUT: diff, **{k: w.astype(_WEIGHT_DTYPES[k]) for k, w in weights.items()}}
        y = _forward(args)
    with _jax.named_scope("loss_head"):
        err = _jnp.square(y.astype(_jnp.float32) - loss_target)
        return 0.5 * _jnp.sum(_jnp.mean(err, axis=-1)) if err.ndim else 0.5 * err


def _adamw(w, g, m, v):
    m = ADAM_B1 * m + (1.0 - ADAM_B1) * g
    v = ADAM_B2 * v + (1.0 - ADAM_B2) * _jnp.square(g)
    m_hat = m / (1.0 - ADAM_B1 ** ADAM_STEP)
    v_hat = v / (1.0 - ADAM_B2 ** ADAM_STEP)
    delta = -ADAM_LR * (m_hat / (_jnp.sqrt(v_hat) + ADAM_EPS) + ADAM_WD * w)
    return delta, m, v


def reference(x, c, ctx, c_ctx, ln0_g, ln0_b, w_ada, b_ada, w_in, conv_w, conv_b, dt_bias, a_log, d_skip, ssd_norm_g, gm_norm_g, gm_norm_b, w_spatial, b_spatial, b_gate, w_ssd_proj, w_gm_proj, w_out, ln1_g, ln1_b, w_ff1, w_ff3, w_ff2, ln2_g, ln2_b, loss_target, m_c_ctx, m_ln0_g, m_ln0_b, m_w_ada, m_b_ada, m_w_in, m_conv_w, m_conv_b, m_dt_bias, m_a_log, m_d_skip, m_ssd_norm_g, m_gm_norm_g, m_gm_norm_b, m_w_spatial, m_b_spatial, m_b_gate, m_w_ssd_proj, m_w_gm_proj, m_w_out, m_ln1_g, m_ln1_b, m_w_ff1, m_w_ff3, m_w_ff2, m_ln2_g, m_ln2_b, v_c_ctx, v_ln0_g, v_ln0_b, v_w_ada, v_b_ada, v_w_in, v_conv_w, v_conv_b, v_dt_bias, v_a_log, v_d_skip, v_ssd_norm_g, v_gm_norm_g, v_gm_norm_b, v_w_spatial, v_b_spatial, v_b_gate, v_w_ssd_proj, v_w_gm_proj, v_w_out, v_ln1_g, v_ln1_b, v_w_ff1, v_w_ff3, v_w_ff2, v_ln2_g, v_ln2_b):
    given = dict(x=x, c=c, ctx=ctx, c_ctx=c_ctx, ln0_g=ln0_g, ln0_b=ln0_b, w_ada=w_ada, b_ada=b_ada, w_in=w_in, conv_w=conv_w, conv_b=conv_b, dt_bias=dt_bias, a_log=a_log, d_skip=d_skip, ssd_norm_g=ssd_norm_g, gm_norm_g=gm_norm_g, gm_norm_b=gm_norm_b, w_spatial=w_spatial, b_spatial=b_spatial, b_gate=b_gate, w_ssd_proj=w_ssd_proj, w_gm_proj=w_gm_proj, w_out=w_out, ln1_g=ln1_g, ln1_b=ln1_b, w_ff1=w_ff1, w_ff3=w_ff3, w_ff2=w_ff2, ln2_g=ln2_g, ln2_b=ln2_b, loss_target=loss_target, m_c_ctx=m_c_ctx, m_ln0_g=m_ln0_g, m_ln0_b=m_ln0_b, m_w_ada=m_w_ada, m_b_ada=m_b_ada, m_w_in=m_w_in, m_conv_w=m_conv_w, m_conv_b=m_conv_b, m_dt_bias=m_dt_bias, m_a_log=m_a_log, m_d_skip=m_d_skip, m_ssd_norm_g=m_ssd_norm_g, m_gm_norm_g=m_gm_norm_g, m_gm_norm_b=m_gm_norm_b, m_w_spatial=m_w_spatial, m_b_spatial=m_b_spatial, m_b_gate=m_b_gate, m_w_ssd_proj=m_w_ssd_proj, m_w_gm_proj=m_w_gm_proj, m_w_out=m_w_out, m_ln1_g=m_ln1_g, m_ln1_b=m_ln1_b, m_w_ff1=m_w_ff1, m_w_ff3=m_w_ff3, m_w_ff2=m_w_ff2, m_ln2_g=m_ln2_g, m_ln2_b=m_ln2_b, v_c_ctx=v_c_ctx, v_ln0_g=v_ln0_g, v_ln0_b=v_ln0_b, v_w_ada=v_w_ada, v_b_ada=v_b_ada, v_w_in=v_w_in, v_conv_w=v_conv_w, v_conv_b=v_conv_b, v_dt_bias=v_dt_bias, v_a_log=v_a_log, v_d_skip=v_d_skip, v_ssd_norm_g=v_ssd_norm_g, v_gm_norm_g=v_gm_norm_g, v_gm_norm_b=v_gm_norm_b, v_w_spatial=v_w_spatial, v_b_spatial=v_b_spatial, v_b_gate=v_b_gate, v_w_ssd_proj=v_w_ssd_proj, v_w_gm_proj=v_w_gm_proj, v_w_out=v_w_out, v_ln1_g=v_ln1_g, v_ln1_b=v_ln1_b, v_w_ff1=v_w_ff1, v_w_ff3=v_w_ff3, v_w_ff2=v_w_ff2, v_ln2_g=v_ln2_g, v_ln2_b=v_ln2_b)
    weights = {n: given[n] for n in TWIN_WEIGHTS}
    shared = {n: given[n] for n in SHARED_INPUTS}
    per_example = {n: given[n] for n in ['x', 'c', 'ctx']}
    grad_fn = _jax.value_and_grad(_loss, argnums=(0, 1))

    def one_microbatch(ex, loss_target):
        ex = dict(ex)
        diff = ex.pop(TWIN_DIFF_INPUT)
        return grad_fn(weights, diff, {**shared, **ex}, loss_target)

    if N_MICROBATCH == 1:
        loss, (grad_w, grad_x) = one_microbatch(per_example, given["loss_target"])
    else:
        def body(carry, xs):
            loss_sum, grad_sum = carry
            l_k, (gw_k, gx_k) = one_microbatch(xs[0], xs[1])
            with _jax.named_scope("update"):
                return (loss_sum + l_k, _jax.tree.map(_jnp.add, grad_sum, gw_k)), gx_k

        init = (_jnp.zeros((), _jnp.float32), _jax.tree.map(_jnp.zeros_like, weights))
        (loss, grad_w), grad_x = _jax.lax.scan(body, init, (per_example, given["loss_target"]))
    with _jax.named_scope("update"):
        delta_w, new_m, new_v = {}, {}, {}
        for n in TWIN_WEIGHTS:
            delta_w[n], new_m[n], new_v[n] = _adamw(weights[n], grad_w[n], given["m_" + n], given["v_" + n])
    return (loss, grad_x, *[grad_w[n] for n in TWIN_WEIGHTS], *[delta_w[n] for n in TWIN_WEIGHTS],
            *[new_m[n] for n in TWIN_WEIGHTS], *[new_v[n] for n in TWIN_WEIGHTS])
```

```python
import functools

import jax
import jax.numpy as jnp
from jax import lax
from jax.experimental import pallas as pl
from jax.experimental.pallas import tpu as pltpu

F32 = jnp.float32
BF16 = jnp.bfloat16
MESH = pl.DeviceIdType.MESH

VMEM_LIMIT_V7X = 56 * 1024 * 1024

D = 1024
LC = 256
Q = 128
NH = 16
D_FF = 2816
LN_EPS = 1e-5
ALPHA = 2.0 ** 0.25

PW = 7168
C_GATE, C_UV, C_Z, C_XBC, C_DT = 0, 2048, 4096, 5120, 6656
D_PROJ = 6688

ADAM_LR, ADAM_B1, ADAM_B2, ADAM_EPS, ADAM_WD, ADAM_STEP = 0.001, 0.9, 0.999, 1e-08, 0.01, 10


def _cp(*sem):
    return pltpu.CompilerParams(dimension_semantics=sem, vmem_limit_bytes=VMEM_LIMIT_V7X)


def _dot(a, b, ca, cb):
    return lax.dot_general(a.astype(BF16), b.astype(BF16), (((ca,), (cb,)), ((), ())),
                           preferred_element_type=F32)


@jax.custom_vjp
def mm(a, b):
    return _dot(a, b, 1, 0)


mm.defvjp(lambda a, b: (_dot(a, b, 1, 0), (a, b)),
          lambda r, g: (_dot(g, r[1], 1, 1), _dot(r[0], g, 0, 0)))


@jax.custom_vjp
def mm_nt(a, b):
    return _dot(a, b, 1, 1)


mm_nt.defvjp(lambda a, b: (_dot(a, b, 1, 1), (a, b)),
             lambda r, g: (_dot(g, r[1], 1, 0), _dot(g, r[0], 0, 0)))


@jax.custom_vjp
def mm_tn(a, b):
    return _dot(a, b, 0, 0)


mm_tn.defvjp(lambda a, b: (_dot(a, b, 0, 0), (a, b)),
             lambda r, g: (_dot(r[1], g, 1, 1), _dot(r[0], g, 1, 0)))


def _dot32(a, b):
    return lax.dot_general(a, b, (((1,), (0,)), ((), ())), precision=lax.Precision.HIGHEST,
                           preferred_element_type=F32)


def _cumsum_fn(rev):
    def tri(transpose):
        r = lax.broadcasted_iota(jnp.int32, (Q, Q), 0)
        c = lax.broadcasted_iota(jnp.int32, (Q, Q), 1)
        keep = (r >= c) if (rev == transpose) else (r <= c)
        return jnp.where(keep, 1.0, 0.0).astype(F32)

    @jax.custom_vjp
    def cums(a):
        return _dot32(tri(False), a)

    cums.defvjp(lambda a: (_dot32(tri(False), a), None), lambda _, g: (_dot32(tri(True), g),))
    return cums


def _cols(v, k):
    w = v.shape[1] // k
    return tuple(v[:, w * i:w * (i + 1)] for i in range(k))


def _splitter(k):
    @jax.custom_vjp
    def split(v):
        return _cols(v, k)

    @jax.custom_vjp
    def concat(ps):
        return jnp.concatenate(ps, axis=1)

    split.defvjp(lambda v: (_cols(v, k), None), lambda _, g: (jnp.concatenate(g, axis=1),))
    concat.defvjp(lambda ps: (jnp.concatenate(ps, axis=1), None), lambda _, g: (_cols(g, k),))
    return split, concat


split2, _ = _splitter(2)
split4, _ = _splitter(4)
split8, concat8 = _splitter(8)


def _ln(x, g, b):
    mu = jnp.mean(x, axis=-1, keepdims=True)
    xc = x - mu
    var = jnp.mean(xc * xc, axis=-1, keepdims=True)
    return xc * lax.rsqrt(var + LN_EPS) * g + b


def _silu(x):
    return x * jax.nn.sigmoid(x)


def _gelu(x):
    return 0.5 * x * (1.0 + jnp.tanh(0.7978845608028654 * (x + 0.044715 * (x * x * x))))


def _xspec(T, w, col, roff):
    return pl.BlockSpec((T, w), lambda i, col=col, roff=roff: (jnp.maximum(i + roff, 0), col))


def _pspec(p, sel):
    if sel is None:
        return pl.BlockSpec(p.shape, lambda i, n=p.ndim: (0,) * n)
    return pl.BlockSpec((1,) + p.shape[1:], lambda i, n=p.ndim: (sel(i),) + (0,) * (n - 1))


def _out_plumbing(outs, T, args, in_specs):
    shapes, specs, aliases = [], [], {}
    for k, o in enumerate(outs):
        if o[0] == 'new':
            _, rows, w, roff = o
            shapes.append(jax.ShapeDtypeStruct((rows, w), F32))
            specs.append(_xspec(T, w, 0, roff))
        elif o[0] == 'acc':
            shapes.append(jax.ShapeDtypeStruct(o[1], F32))
            specs.append(pl.BlockSpec(o[1], lambda i, n=len(o[1]): (0,) * n))
        elif o[0] == 'part':
            _, rows, wtot, w, col, roff = o
            shapes.append(jax.ShapeDtypeStruct((rows, wtot), F32))
            specs.append(_xspec(T, w, col, roff))
        else:
            _, arr, w, col, roff = o
            aliases[len(args)] = k
            args.append(arr)
            in_specs.append(pl.BlockSpec(memory_space=pl.ANY))
            shapes.append(jax.ShapeDtypeStruct(arr.shape, arr.dtype))
            specs.append(_xspec(T, w, col, roff))
    return shapes, specs, aliases


def stage_fwd(name, f, T, n, xs, ps, outs):
    nx, npar = len(xs), len(ps)
    args = [x[0] for x in xs] + [p[0] for p in ps]
    in_specs = [_xspec(T, w, col, roff) for (_, w, col, roff) in xs] + [_pspec(p, sel) for (p, sel) in ps]
    n_in = len(args)
    shapes, specs, aliases = _out_plumbing(outs, T, args, in_specs)
    n_all_in = len(args)

    def body(*refs):
        i = pl.program_id(0)
        xv = [r[...] for r in refs[:nx]]
        pv = [r[...] if ps[k][1] is None else r[0] for k, r in enumerate(refs[nx:n_in])]
        res = f(*xv, *pv)
        for k, o_ref in enumerate(refs[n_all_in:]):
            if outs[k][0] == 'acc':
                @pl.when(i == 0)
                def _(o_ref=o_ref, v=res[k]):
                    o_ref[...] = v

                @pl.when(i > 0)
                def _(o_ref=o_ref, v=res[k]):
                    o_ref[...] += v
            else:
                o_ref[...] = res[k].astype(o_ref.dtype)

    return pl.pallas_call(body, name=name, grid=(n,), in_specs=in_specs, out_specs=specs, out_shape=shapes,
                          input_output_aliases=aliases, compiler_params=_cp("arbitrary"))(*args)


def stage_bwd(name, f, T, n, xs, ps, cts, dxs, dps):
    nx, npar = len(xs), len(ps)
    args = [x[0] for x in xs] + [p[0] for p in ps]
    in_specs = [_xspec(T, w, col, roff) for (_, w, col, roff) in xs] + [_pspec(p, sel) for (p, sel) in ps]
    ct_arrs = [c for c in cts if isinstance(c, tuple)]
    for (a, w, col, roff) in ct_arrs:
        args.append(a)
        in_specs.append(_xspec(T, w, col, roff))
    n_in = len(args)
    outs, out_of = [], []
    for k, o in enumerate(dxs):
        if o is not None:
            outs.append(o)
            out_of.append(('x', k))
    for k, want in enumerate(dps):
        if want:
            p, sel = ps[k]
            outs.append(('acc', p.shape))
            out_of.append(('p', k))
    shapes, specs, aliases = _out_plumbing(outs, T, args, in_specs)
    for j, (kind, k) in enumerate(out_of):
        if kind == 'p' and ps[k][1] is not None:
            p, sel = ps[k]
            specs[j] = pl.BlockSpec((1,) + p.shape[1:], lambda i, n=p.ndim, sel=sel: (sel(i),) + (0,) * (n - 1))
    n_all_in = len(args)

    def body(*refs):
        i = pl.program_id(0)
        xv = [r[...] for r in refs[:nx]]
        pv = [r[...] if ps[k][1] is None else r[0] for k, r in enumerate(refs[nx:nx + npar])]
        res, vjp_fn = jax.vjp(f, *xv, *pv)
        ctv, q = [], nx + npar
        for k, c in enumerate(cts):
            if c is None:
                ctv.append(jnp.zeros_like(res[k]))
            elif isinstance(c, tuple):
                v = refs[q][...]
                if c[3] < 0:
                    v = v * (i + c[3] >= 0).astype(F32)
                ctv.append(v)
                q += 1
            else:
                ctv.append(jnp.full_like(res[k], c))
        grads = vjp_fn(tuple(ctv))
        for j, o_ref in enumerate(refs[n_all_in:]):
            kind, k = out_of[j]
            if kind == 'x':
                o_ref[...] = grads[k].astype(o_ref.dtype)
            else:
                g = grads[nx + k]
                sel = ps[k][1]
                if sel is None:
                    first = i == 0
                    tgt = o_ref
                else:
                    first = jnp.logical_or(i == 0, sel(i) != sel(jnp.maximum(i - 1, 0)))
                    tgt = o_ref.at[0]

                @pl.when(first)
                def _(tgt=tgt, g=g):
                    tgt[...] = g

                @pl.when(jnp.logical_not(first))
                def _(tgt=tgt, g=g):
                    tgt[...] += g

    return pl.pallas_call(body, name=name, grid=(n,), in_specs=in_specs, out_specs=specs, out_shape=shapes,
                          input_output_aliases=aliases, compiler_params=_cp("arbitrary"))(*args)


_CONTRACT = {'nn': (1, 0), 'nt': (1, 1), 'tn': (0, 0)}


def matmul(name, a, b, mode, tm, tn, tk, out_dtype=F32, add=None):
    if mode == 'nn':
        (M, K), (_, N) = a.shape, b.shape
    elif mode == 'nt':
        (M, K), (N, _) = a.shape, b.shape
    else:
        (K, M), (_, N) = a.shape, b.shape
    assert M % tm == 0 and N % tn == 0 and K % tk == 0, (name, M, N, K, tm, tn, tk)
    a_spec = (pl.BlockSpec((tk, tm), lambda j, i, k: (k, i)) if mode == 'tn'
              else pl.BlockSpec((tm, tk), lambda j, i, k: (i, k)))
    b_spec = (pl.BlockSpec((tn, tk), lambda j, i, k: (j, k)) if mode == 'nt'
              else pl.BlockSpec((tk, tn), lambda j, i, k: (k, j)))
    o_spec = pl.BlockSpec((tm, tn), lambda j, i, k: (i, j))
    return matmul_call(name, (N // tn, M // tm, K // tk), a, a_spec, b, b_spec, (M, N), o_spec, (tm, tn), mode,
                       out_dtype, add)


def matmul_call(name, grid, a, a_spec, b, b_spec, out_shape, o_spec, tile, mode, out_dtype=F32, add=None):
    tm, tn = tile
    nk = grid[2]
    ca, cb = _CONTRACT[mode]
    args, in_specs = [a, b], [a_spec, b_spec]
    if add is not None:
        args.append(add)
        in_specs.append(o_spec)

    def body(*refs):
        a_ref, b_ref = refs[0], refs[1]
        o_ref, acc = refs[-2], refs[-1]
        k = pl.program_id(2)
        p = _dot(a_ref[...], b_ref[...], ca, cb)

        @pl.when(k == 0)
        def _():
            acc[...] = p + refs[2][...] if add is not None else p

        @pl.when(k > 0)
        def _():
            acc[...] += p

        @pl.when(k == nk - 1)
        def _():
            o_ref[...] = acc[...].astype(out_dtype)

    return pl.pallas_call(body, name=name, grid=grid, in_specs=in_specs, out_specs=o_spec,
                          out_shape=jax.ShapeDtypeStruct(out_shape, out_dtype),
                          scratch_shapes=[pltpu.VMEM((tm, tn), F32)],
                          compiler_params=_cp("arbitrary", "arbitrary", "arbitrary"))(*args)


NS, WS = 4, 704


def ff_in_fwd(name, h, w3, tm):
    M = h.shape[0]
    return matmul_call(name, (NS, M // tm, 1), h, pl.BlockSpec((tm, D), lambda j, i, k: (i, 0)),
                       w3, pl.BlockSpec((None, D, WS), lambda j, i, k: (j, 0, 0)),
                       (NS, M, WS), pl.BlockSpec((None, tm, WS), lambda j, i, k: (j, i, 0)), (tm, WS), 'nn')


def ff_in_bwd_x(name, da3, w3, tm, add=None):
    M = da3.shape[1]
    return matmul_call(name, (1, M // tm, NS), da3, pl.BlockSpec((None, tm, WS), lambda j, i, k: (k, i, 0)),
                       w3, pl.BlockSpec((None, D, WS), lambda j, i, k: (k, 0, 0)),
                       (M, D), pl.BlockSpec((tm, D), lambda j, i, k: (i, 0)), (tm, D), 'nt', F32, add)


def ff_in_bwd_w(name, h, da3, tk):
    M = h.shape[0]
    return matmul_call(name, (NS, 1, M // tk), h, pl.BlockSpec((tk, D), lambda j, i, k: (k, 0)),
                       da3, pl.BlockSpec((None, tk, WS), lambda j, i, k: (j, k, 0)),
                       (NS, D, WS), pl.BlockSpec((None, D, WS), lambda j, i, k: (j, 0, 0)), (D, WS), 'tn', BF16)


def ff_out_fwd(name, act3, w2, tm):
    M = act3.shape[1]
    return matmul_call(name, (1, M // tm, NS), act3, pl.BlockSpec((None, tm, WS), lambda j, i, k: (k, i, 0)),
                       w2, pl.BlockSpec((WS, D), lambda j, i, k: (k, 0)),
                       (M, D), pl.BlockSpec((tm, D), lambda j, i, k: (i, 0)), (tm, D), 'nn')


def ff_out_bwd_x(name, dff, w2, tm):
    M = dff.shape[0]
    return matmul_call(name, (NS, M // tm, 1), dff, pl.BlockSpec((tm, D), lambda j, i, k: (i, 0)),
                       w2, pl.BlockSpec((WS, D), lambda j, i, k: (j, 0)),
                       (NS, M, WS), pl.BlockSpec((None, tm, WS), lambda j, i, k: (j, i, 0)), (tm, WS), 'nt')


def ff_out_bwd_w(name, act3, dff, tk):
    M = dff.shape[0]
    return matmul_call(name, (1, NS, M // tk), act3, pl.BlockSpec((None, tk, WS), lambda j, i, k: (i, k, 0)),
                       dff, pl.BlockSpec((tk, D), lambda j, i, k: (k, 0)),
                       (NS * WS, D), pl.BlockSpec((WS, D), lambda j, i, k: (i, 0)), (WS, D), 'tn', BF16)


def _shift_rows(x, d):
    n = x.shape[0]
    if d == 0:
        return x
    y = pltpu.roll(x, (-d) % n, 0)
    t = lax.broadcasted_iota(jnp.int32, x.shape, 0)
    ok = (t + d >= 0) & (t + d < n)
    return jnp.where(ok, y, 0.0)


def _conv_pre(x, w_ref, b_ref):
    acc = jnp.broadcast_to(b_ref[...], x.shape)
    for k in range(5):
        acc = acc + _shift_rows(x, k - 2) * w_ref[k:k + 1, :]
    return acc


def conv_fwd(name, proj, conv_w, conv_b, R):
    segs = ((0, LC), (LC, R))

    def body(x_ref, w_ref, b_ref, o_ref):
        for (s, e) in segs:
            pre = _conv_pre(x_ref[s:e, :], w_ref, b_ref)
            o_ref[s:e, :] = _silu(pre)

    return pl.pallas_call(
        body, name=name, grid=(12,),
        in_specs=[pl.BlockSpec((R, 128), lambda j: (0, C_XBC // 128 + j)),
                  pl.BlockSpec((8, 128), lambda j: (0, j)), pl.BlockSpec((1, 128), lambda j: (0, j))],
        out_specs=pl.BlockSpec((R, 128), lambda j: (0, j)),
        out_shape=jax.ShapeDtypeStruct((R, 1536), F32), compiler_params=_cp("arbitrary"))(proj, conv_w, conv_b)


def conv_bwd(name, proj, conv_w, conv_b, d_f, d_b, d_skip, dproj, R):
    segs = ((0, LC), (LC, R))

    def body(x_ref, w_ref, b_ref, df_ref, db_ref, ds_ref, _, dx_ref, dw_ref, dbias_ref):
        j = pl.program_id(0)
        has_skip = (j < 8).astype(F32)
        dw = [jnp.zeros((1, 128), F32) for _ in range(5)]
        dbias = jnp.zeros((1, 128), F32)
        for (s, e) in segs:
            x = x_ref[s:e, :]
            pre = _conv_pre(x, w_ref, b_ref)
            sig = jax.nn.sigmoid(pre)
            dy = df_ref[s:e, :] + db_ref[s:e, :]
            if s == LC:
                dy = dy + ds_ref[...] * has_skip
            dpre = dy * (sig * (1.0 + pre * (1.0 - sig)))
            dx = jnp.zeros_like(x)
            for k in range(5):
                dx = dx + _shift_rows(dpre, 2 - k) * w_ref[k:k + 1, :]
                dw[k] = dw[k] + jnp.sum(dpre * _shift_rows(x, k - 2), axis=0, keepdims=True)
            dbias = dbias + jnp.sum(dpre, axis=0, keepdims=True)
            dx_ref[s:e, :] = dx
        dw_ref[...] = jnp.zeros_like(dw_ref)
        for k in range(5):
            dw_ref[k:k + 1, :] = dw[k]
        dbias_ref[...] = dbias

    return pl.pallas_call(
        body, name=name, grid=(12,),
        in_specs=[pl.BlockSpec((R, 128), lambda j: (0, C_XBC // 128 + j)),
                  pl.BlockSpec((8, 128), lambda j: (0, j)), pl.BlockSpec((1, 128), lambda j: (0, j)),
                  pl.BlockSpec((R, 128), lambda j: (0, j)), pl.BlockSpec((R, 128), lambda j: (0, j)),
                  pl.BlockSpec((R - LC, 128), lambda j: (0, jnp.minimum(j, 7))),
                  pl.BlockSpec(memory_space=pl.ANY)],
        out_specs=[pl.BlockSpec((R, 128), lambda j: (0, C_XBC // 128 + j)),
                   pl.BlockSpec((8, 128), lambda j: (0, j)), pl.BlockSpec((1, 128), lambda j: (0, j))],
        out_shape=[jax.ShapeDtypeStruct(dproj.shape, F32), jax.ShapeDtypeStruct((8, 1536), F32),
                   jax.ShapeDtypeStruct((1, 1536), F32)],
        input_output_aliases={6: 0}, compiler_params=_cp("arbitrary"))(proj, conv_w, conv_b, d_f, d_b, d_skip, dproj)


def _ssd_chunk(rev, dirn):
    cums = _cumsum_fn(rev)

    def f(xs, Bs, Cs, dt, alog, Hs):
        lane = lax.broadcasted_iota(jnp.int32, (1, 128), 1)
        sub = lax.broadcasted_iota(jnp.int32, (Q, 1), 0)
        r = lax.broadcasted_iota(jnp.int32, (Q, Q), 0)
        c = lax.broadcasted_iota(jnp.int32, (Q, Q), 1)
        mask = (r <= c) if rev else (r >= c)
        left = lane < 64
        a = dt * (-jnp.exp(alog))
        s = cums(a)
        sT, dtT = s.T, dt.T
        last_row = (sub == (0 if rev else Q - 1)).astype(F32)
        s_last = jnp.sum(s * last_row, axis=0, keepdims=True)
        G = [mm_nt(Cs[g], Bs[g]) for g in range(2)]
        M, es, wc, ed = [], [], [], []
        for h in range(NH):
            l = 16 * dirn + h
            oh_l = (lane == l).astype(F32)
            oh_s = (sub == l).astype(F32)
            s_col = jnp.sum(s * oh_l, axis=1, keepdims=True)
            dt_col = jnp.sum(dt * oh_l, axis=1, keepdims=True)
            s_row = jnp.sum(sT * oh_s, axis=0, keepdims=True)
            dt_row = jnp.sum(dtT * oh_s, axis=0, keepdims=True)
            sl = jnp.sum(s_last * oh_l, axis=1, keepdims=True)
            seg = jnp.where(mask, s_col - s_row, 0.0)
            lm = jnp.where(mask, jnp.exp(seg), 0.0)
            M.append(G[h // 8] * lm * dt_row)
            es.append(jnp.exp(s_col))
            wc.append(jnp.exp(sl - s_col) * dt_col)
            ed.append(jnp.exp(sl))
        Ys, Hn = [], []
        for j in range(8):
            g = j // 4
            xa = jnp.where(left, xs[j], 0.0)
            xb = jnp.where(left, 0.0, xs[j])
            yd = mm(M[2 * j], xa) + mm(M[2 * j + 1], xb)
            yo = mm(Cs[g], Hs[j]) * jnp.where(left, es[2 * j], es[2 * j + 1])
            Ys.append(yd + yo)
            st = mm_tn(Bs[g], xs[j] * jnp.where(left, wc[2 * j], wc[2 * j + 1]))
            Hn.append(Hs[j] * jnp.where(left, ed[2 * j], ed[2 * j + 1]) + st)
        return Ys, Hn

    return f


def _chunk_of(t, n, rev):
    if not rev:
        return t
    return jnp.where(t < 2, 1 - t, n + 1 - t)


def ssd_fwd(name, xbc, dt, alog, n, rev, dirn):
    chunk = _ssd_chunk(rev, dirn)

    def body(x_ref, b_ref, c_ref, dt_ref, al_ref, y_ref, hs_ref, h_scr):
        @pl.when(pl.program_id(0) == 0)
        def _():
            h_scr[...] = jnp.zeros_like(h_scr)

        xs = [x_ref[:, 128 * j:128 * (j + 1)] for j in range(8)]
        Bs = [b_ref[:, 128 * g:128 * (g + 1)] for g in range(2)]
        Cs = [c_ref[:, 128 * g:128 * (g + 1)] for g in range(2)]
        Hs = [h_scr[:, 128 * j:128 * (j + 1)] for j in range(8)]
        hs_ref[0] = h_scr[...]
        Ys, Hn = chunk(xs, Bs, Cs, dt_ref[...], al_ref[...], Hs)
        for j in range(8):
            y_ref[:, 128 * j:128 * (j + 1)] = Ys[j]
            h_scr[:, 128 * j:128 * (j + 1)] = Hn[j]

    cm = lambda t: _chunk_of(t, n, rev)
    return pl.pallas_call(
        body, name=name, grid=(n,),
        in_specs=[pl.BlockSpec((Q, 1024), lambda t: (cm(t), 0)), pl.BlockSpec((Q, 256), lambda t: (cm(t), 4)),
                  pl.BlockSpec((Q, 256), lambda t: (cm(t), 5)), pl.BlockSpec((Q, 128), lambda t: (cm(t), 0)),
                  pl.BlockSpec((1, 128), lambda t: (0, 0))],
        out_specs=[pl.BlockSpec((Q, 1024), lambda t: (cm(t), 0)), pl.BlockSpec((1, Q, 1024), lambda t: (cm(t), 0, 0))],
        out_shape=[jax.ShapeDtypeStruct((n * Q, 1024), F32), jax.ShapeDtypeStruct((n, Q, 1024), F32)],
        scratch_shapes=[pltpu.VMEM((Q, 1024), F32)], compiler_params=_cp("arbitrary"))(xbc, xbc, xbc, dt, alog)


def ssd_bwd(name, xbc, dt, alog, hs, dy, n, rev, dirn):
    chunk = _ssd_chunk(rev, dirn)

    def body(x_ref, b_ref, c_ref, dt_ref, al_ref, hs_ref, dy_ref, dx_ref, ddt_ref, dal_ref, dh_scr):
        tt = pl.program_id(0)
        ch = _chunk_of(n - 1 - tt, n, rev)

        @pl.when(tt == 0)
        def _():
            dh_scr[...] = jnp.zeros_like(dh_scr)

        xs = [x_ref[:, 128 * j:128 * (j + 1)] for j in range(8)]
        Bs = [b_ref[:, 128 * g:128 * (g + 1)] for g in range(2)]
        Cs = [c_ref[:, 128 * g:128 * (g + 1)] for g in range(2)]
        Hs = [hs_ref[0, :, 128 * j:128 * (j + 1)] for j in range(8)]
        live = (ch >= 2).astype(F32)
        dYs = [dy_ref[:, 128 * j:128 * (j + 1)] * live for j in range(8)]
        dHn = [dh_scr[:, 128 * j:128 * (j + 1)] for j in range(8)]
        _, vjp_fn = jax.vjp(chunk, xs, Bs, Cs, dt_ref[...], al_ref[...], Hs)
        dxs, dBs, dCs, ddt, dal, dHs = vjp_fn((dYs, dHn))
        for j in range(8):
            dx_ref[:, 128 * j:128 * (j + 1)] = dxs[j]
            dh_scr[:, 128 * j:128 * (j + 1)] = dHs[j]
        for g in range(2):
            dx_ref[:, 1024 + 128 * g:1024 + 128 * (g + 1)] = dBs[g]
            dx_ref[:, 1280 + 128 * g:1280 + 128 * (g + 1)] = dCs[g]
        ddt_ref[...] = ddt

        @pl.when(tt == 0)
        def _():
            dal_ref[...] = dal

        @pl.when(tt > 0)
        def _():
            dal_ref[...] += dal

    cm = lambda t: _chunk_of(n - 1 - t, n, rev)
    return pl.pallas_call(
        body, name=name, grid=(n,),
        in_specs=[pl.BlockSpec((Q, 1024), lambda t: (cm(t), 0)), pl.BlockSpec((Q, 256), lambda t: (cm(t), 4)),
                  pl.BlockSpec((Q, 256), lambda t: (cm(t), 5)), pl.BlockSpec((Q, 128), lambda t: (cm(t), 0)),
                  pl.BlockSpec((1, 128), lambda t: (0, 0)), pl.BlockSpec((1, Q, 1024), lambda t: (cm(t), 0, 0)),
                  pl.BlockSpec((Q, 1024), lambda t: (jnp.maximum(cm(t) - 2, 0), 0))],
        out_specs=[pl.BlockSpec((Q, 1536), lambda t: (cm(t), 0)), pl.BlockSpec((Q, 128), lambda t: (cm(t), 0)),
                   pl.BlockSpec((1, 128), lambda t: (0, 0))],
        out_shape=[jax.ShapeDtypeStruct((n * Q, 1536), F32), jax.ShapeDtypeStruct((n * Q, 128), F32),
                   jax.ShapeDtypeStruct((1, 128), F32)],
        scratch_shapes=[pltpu.VMEM((Q, 1024), F32)], compiler_params=_cp("arbitrary"))(xbc, xbc, xbc, dt, alog, hs, dy)


def f_norm0(x, g0, b0, sc, sh):
    x0 = _ln(x, g0, b0)
    return x0, x0 * (1.0 + sc) + sh


def f_dt(raw, bias):
    z = split4(raw)[0] + bias
    dt = jnp.maximum(z, 0.0) + jnp.log1p(jnp.exp(-jnp.abs(z)))
    return dt, dt


def f_gated_norm(yf, yb, xs, z, dcol, g):
    h = (yf + yb + xs * dcol) * _silu(z)
    return (h * lax.rsqrt(jnp.mean(h * h, axis=-1, keepdims=True) + LN_EPS) * g,)


def f_gmlp(uv, gmg, gmb, *wb):
    ws, bs = wb[:8], wb[8:]
    u, v = split2(uv)
    vn = split8(_ln(_gelu(v), gmg, gmb))
    mixed = concat8(tuple(mm(ws[g], vn[g]) + bs[g] for g in range(8)))
    return (_gelu(u) * mixed,)


def f_merge(ps, pg, gates, bg):
    gs, gg = split2(jax.nn.sigmoid(gates + bg))
    return (gs * ps + gg * pg,)


def f_res1(x0, out, g1, lg, lb, sc, sh):
    x1 = _ln(ALPHA * x0 + g1 * out, lg, lb)
    return x1, x1 * (1.0 + sc) + sh


def f_swiglu(a1, a3):
    return (_silu(a1) * a3,)


def f_res2_loss(x1, ff, tgt, g2, lg, lb):
    x2 = _ln(ALPHA * x1 + g2 * ff, lg, lb)
    e = x2 - tgt
    return (0.5 * jnp.sum(jnp.mean(e * e, axis=-1, keepdims=True), axis=0, keepdims=True),)


def _row_tile(M):
    return 544 if M % 544 == 0 else (512 if M % 512 == 0 else M)


def core(x_all, tgt, mod_x, mod_c, W, S):
    R = x_all.shape[0]
    L = R - LC
    n = R // Q
    T = 256
    nt, ntl = R // T, L // T
    tmR, tmL = _row_tile(R), _row_tile(L)
    tkR = 256 if R % 512 else 512
    tkL = 512 if L % 512 == 0 else 256
    row = lambda v: v.reshape(1, -1)
    mx = [row(mod_x[k]) for k in range(6)]
    mc = [row(mod_c[k]) for k in range(6)]
    sel = lambda i: jnp.minimum(i, 1)
    sc1 = jnp.stack([mc[1], mx[1]])
    sh1 = jnp.stack([mc[0], mx[0]])
    ln0 = [(row(S['ln0_g']), None), (row(S['ln0_b']), None), (sc1, sel), (sh1, sel)]

    x0, xm = stage_fwd("norm0_fwd", f_norm0, T, nt, [(x_all, D, 0, 0)], ln0, [('new', R, D, 0), ('new', R, D, 0)])
    proj = matmul("proj_fwd", xm, W['w_in'], 'nn', tmR, 1024, 1024)
    conv_w8 = jnp.pad(S['conv_w'], ((0, 3), (0, 0)))
    conv_b = row(S['conv_b'])
    xbc = conv_fwd("conv_fwd", proj, conv_w8, conv_b, R)
    dt_bias = jnp.pad(S['dt_bias'].reshape(1, 32), ((0, 0), (0, 96)))
    alog = jnp.pad(S['a_log'].reshape(1, 32), ((0, 0), (0, 96)))
    x_dt = [(proj, 512, C_DT // 512, 0)]
    dt_f, dt_b = stage_fwd("dt_fwd", f_dt, T, nt, x_dt, [(dt_bias, None)], [('new', R, 128, 0), ('new', R, 128, 0)])
    y_f, hs_f = ssd_fwd("ssd_fwd_f", xbc, dt_f, alog, n, False, 0)
    y_b, hs_b = ssd_fwd("ssd_fwd_b", xbc, dt_b, alog, n, True, 1)
    dcol = jnp.repeat(S['d_skip'][0] + S['d_skip'][1], 64).reshape(1, D)
    x_gn = [(y_f, D, 0, 1), (y_b, D, 0, 1), (xbc, D, 0, 1), (proj, D, C_Z // D, 1)]
    p_gn = [(dcol, None), (row(S['ssd_norm_g']), None)]
    (yn,) = stage_fwd("gnorm_fwd", f_gated_norm, T, ntl, x_gn, p_gn, [('new', L, D, 0)])
    x_gm = [(proj, 2 * D, C_UV // (2 * D), LC // Q)]
    p_gm = ([(row(S['gm_norm_g']), None), (row(S['gm_norm_b']), None)]
            + [(S['w_spatial'][g], None) for g in range(8)] + [(S['b_spatial'][g].reshape(Q, 1), None) for g in range(8)])
    (y_gm,) = stage_fwd("gmlp_fwd", f_gmlp, Q, L // Q, x_gm, p_gm, [('new', L, D, 0)])
    p_ssd = matmul("pssd_fwd", yn, W['w_ssd_proj'], 'nn', tmL, 1024, 1024)
    p_g = matmul("pgm_fwd", y_gm, W['w_gm_proj'], 'nn', tmL, 1024, 1024)
    x_mg = [(p_ssd, D, 0, 0), (p_g, D, 0, 0), (proj, 2 * D, C_GATE // (2 * D), 1)]
    p_mg = [(row(S['b_gate']), None)]
    (merged,) = stage_fwd("merge_fwd", f_merge, T, ntl, x_mg, p_mg, [('new', L, D, 0)])
    out = matmul("out_fwd", merged, W['w_out'], 'nn', tmL, 1024, 1024)
    x_r1 = [(x0, D, 0, 1), (out, D, 0, 0)]
    p_r1 = [(mx[2], None), (row(S['ln1_g']), None), (row(S['ln1_b']), None), (mx[4], None), (mx[3], None)]
    x1, hm = stage_fwd("res1_fwd", f_res1, T, ntl, x_r1, p_r1, [('new', L, D, 0), ('new', L, D, 0)])
    a1 = ff_in_fwd("ff1_fwd", hm, W['w_ff1'], tmL).reshape(NS * L, WS)
    a3 = ff_in_fwd("ff3_fwd", hm, W['w_ff3'], tmL).reshape(NS * L, WS)
    x_sw = [(a1, WS, 0, 0), (a3, WS, 0, 0)]
    (act,) = stage_fwd("swiglu_fwd", f_swiglu, T, NS * L // T, x_sw, [], [('new', NS * L, WS, 0)])
    act = act.reshape(NS, L, WS)
    ff = ff_out_fwd("ff2_fwd", act, W['w_ff2'], tmL)
    x_r2 = [(x1, D, 0, 0), (ff, D, 0, 0), (tgt, D, 0, 0)]
    p_r2 = [(mx[5], None), (row(S['ln2_g']), None), (row(S['ln2_b']), None)]

    dx1_a, dff, dg2, dl2g, dl2b = stage_bwd(
        "res2_bwd", f_res2_loss, T, ntl, x_r2, p_r2, [1.0],
        [('new', L, D, 0), ('new', L, D, 0), None], [True, True, True])
    loss = stage_fwd("loss_fwd", f_res2_loss, T, ntl, x_r2, p_r2, [('acc', (1, 1))])[0]
    dact = ff_out_bwd_x("ff2_bwd_x", dff, W['w_ff2'], tmL).reshape(NS * L, WS)
    gw_ff2 = ff_out_bwd_w("ff2_bwd_w", act, dff, tkL)
    da1, da3 = stage_bwd("swiglu_bwd", f_swiglu, T, NS * L // T, x_sw, [], [(dact, WS, 0, 0)],
                         [('new', NS * L, WS, 0), ('new', NS * L, WS, 0)], [])
    da1, da3 = da1.reshape(NS, L, WS), da3.reshape(NS, L, WS)
    dhm = ff_in_bwd_x("ff1_bwd_x", da1, W['w_ff1'], tmL)
    dhm = ff_in_bwd_x("ff3_bwd_x", da3, W['w_ff3'], tmL, add=dhm)
    gw_ff1 = ff_in_bwd_w("ff1_bwd_w", hm, da1, tkL)
    gw_ff3 = ff_in_bwd_w("ff3_bwd_w", hm, da3, tkL)
    dx0_a, dout, dg1, dl1g, dl1b, dsc2, dsh2 = stage_bwd(
        "res1_bwd", f_res1, T, ntl, x_r1, p_r1, [(dx1_a, D, 0, 0), (dhm, D, 0, 0)],
        [('new', L, D, 0), ('new', L, D, 0)], [True] * 5)
    dmerged = matmul("out_bwd_x", dout, W['w_out'], 'nt', tmL, 1024, 1024)
    gw_out = matmul("out_bwd_w", merged, dout, 'tn', 1024, 1024, tkL, BF16)
    lt, lq = -(LC // T), -(LC // Q)
    x_mg_b = [(p_ssd, D, 0, lt), (p_g, D, 0, lt), (proj, 2 * D, C_GATE // (2 * D), 0)]
    dp_ssd, dp_g, dproj, dbg = stage_bwd(
        "merge_bwd", f_merge, T, nt, x_mg_b, p_mg, [(dmerged, D, 0, lt)],
        [('new', L, D, lt), ('new', L, D, lt), ('part', R, PW, 2 * D, C_GATE // (2 * D), 0)], [True])
    dyn = matmul("pssd_bwd_x", dp_ssd, W['w_ssd_proj'], 'nt', tmL, 1024, 1024)
    gw_ssd = matmul("pssd_bwd_w", yn, dp_ssd, 'tn', 1024, 1024, tkL, BF16)
    dy_gm = matmul("pgm_bwd_x", dp_g, W['w_gm_proj'], 'nt', tmL, 1024, 1024)
    gw_gm = matmul("pgm_bwd_w", y_gm, dp_g, 'tn', 1024, 1024, tkL, BF16)
    r_gm = stage_bwd("gmlp_bwd", f_gmlp, Q, n, [(proj, 2 * D, C_UV // (2 * D), 0)], p_gm, [(dy_gm, D, 0, lq)],
                     [('alias', dproj, 2 * D, C_UV // (2 * D), 0)], [True] * 18)
    dproj, dgmg, dgmb, dws, dbs = r_gm[0], r_gm[1], r_gm[2], r_gm[3:11], r_gm[11:19]
    x_gn_b = [(y_f, D, 0, 0), (y_b, D, 0, 0), (xbc, D, 0, 0), (proj, D, C_Z // D, 0)]
    dy, dskipx, dproj, ddcol, dng = stage_bwd(
        "gnorm_bwd", f_gated_norm, T, nt, x_gn_b, p_gn, [(dyn, D, 0, lt)],
        [('new', L, D, lt), None, ('new', L, D, lt), ('alias', dproj, D, C_Z // D, 0)], [True, True])
    dxbc_f, ddt_f, dal_f = ssd_bwd("ssd_bwd_f", xbc, dt_f, alog, hs_f, dy, n, False, 0)
    dxbc_b, ddt_b, dal_b = ssd_bwd("ssd_bwd_b", xbc, dt_b, alog, hs_b, dy, n, True, 1)
    dproj, ddtb = stage_bwd("dt_bwd", f_dt, T, nt, x_dt, [(dt_bias, None)],
                            [(ddt_f, 128, 0, 0), (ddt_b, 128, 0, 0)],
                            [('alias', dproj, 512, C_DT // 512, 0)], [True])
    dproj, dcw8, dcb = conv_bwd("conv_bwd", proj, conv_w8, conv_b, dxbc_f, dxbc_b, dskipx, dproj, R)
    dxm = matmul("proj_bwd_x", dproj, W['w_in'], 'nt', tmR, 1024, 1024)
    gw_in = matmul("proj_bwd_w", xm, dproj, 'tn', 1024, 1024, tkR, BF16)
    grad_x, dl0g, dl0b, dsc1, dsh1 = stage_bwd(
        "norm0_bwd", f_norm0, T, nt, [(x_all, D, 0, 0)], ln0, [(dx0_a, D, 0, -1), (dxm, D, 0, 0)],
        [('new', L, D, -1)], [True] * 4)

    zero = jnp.zeros((D,), F32)
    flat = lambda v: v.reshape(-1)
    small = {
        'loss': flat(loss), 'ln0_g': flat(dl0g), 'ln0_b': flat(dl0b),
        'dmod_x': jnp.concatenate([flat(dsh1[1]), flat(dsc1[1]), flat(dg1), flat(dsh2), flat(dsc2), flat(dg2)]),
        'dmod_c': jnp.concatenate([flat(dsh1[0]), flat(dsc1[0]), zero, zero, zero, zero]),
        'conv_w': flat(dcw8[:5]), 'conv_b': flat(dcb), 'dt_bias': flat(ddtb[:, :32]),
        'a_log': flat((dal_f + dal_b)[:, :32]),
        'd_skip': flat(jnp.tile(ddcol.reshape(1, NH, 64).sum(-1), (2, 1))),
        'ssd_norm_g': flat(dng), 'gm_norm_g': flat(dgmg), 'gm_norm_b': flat(dgmb),
        'w_spatial': flat(jnp.stack(dws)), 'b_spatial': flat(jnp.stack(dbs)), 'b_gate': flat(dbg),
        'ln1_g': flat(dl1g), 'ln1_b': flat(dl1b), 'ln2_g': flat(dl2g), 'ln2_b': flat(dl2b),
    }
    big = {'w_in': gw_in, 'w_ssd_proj': gw_ssd, 'w_gm_proj': gw_gm, 'w_out': gw_out,
           'w_ff1': gw_ff1, 'w_ff3': gw_ff3, 'w_ff2': gw_ff2}
    return grad_x, big, small


def _place():
    return lax.axis_index("x"), lax.axis_index("y"), lax.axis_index("c")


def allgather8(name, blk, hbm):
    space = pl.ANY if hbm else pltpu.VMEM

    def body(x_ref, out_ref, send_sems, recv_sems, local_sem):
        x, y, c = _place()
        me, sibling = (x, y, c), (x, y, 1 - c)
        chips = [(1 - x, y), (x, 1 - y), (1 - x, 1 - y)]

        def slot(px, py, pc):
            return out_ref.at[4 * px + 2 * py + pc]

        def copy(k, block, to, src=None):
            return pltpu.make_async_remote_copy(
                src_ref=slot(*block) if src is None else src, dst_ref=slot(*block),
                send_sem=send_sems.at[k], recv_sem=recv_sems.at[k], device_id=to, device_id_type=MESH)

        mine = pltpu.make_async_copy(x_ref, slot(*me), local_sem)
        mine.start()
        first = [copy(0, me, sibling, src=x_ref)]
        first += [copy(1 + j, me, (*chip, c), src=x_ref) for j, chip in enumerate(chips)]
        for cp in first:
            cp.start()
        passed = [copy(4 + j, (*chip, c), sibling) for j, chip in enumerate(chips)]
        for j, chip in enumerate(chips):
            copy(1 + j, (*chip, c), me).wait_recv()
            passed[j].start()
        copy(0, sibling, me).wait_recv()
        for j, chip in enumerate(chips):
            copy(4 + j, (*chip, 1 - c), me).wait_recv()
        for cp in first + passed:
            cp.wait_send()
        mine.wait()

    return pl.pallas_call(
        body, name=name, out_shape=jax.ShapeDtypeStruct((8,) + blk.shape, blk.dtype),
        in_specs=[pl.BlockSpec(memory_space=space)], out_specs=pl.BlockSpec(memory_space=space),
        scratch_shapes=[pltpu.SemaphoreType.DMA((7,)), pltpu.SemaphoreType.DMA((7,)), pltpu.SemaphoreType.DMA],
        compiler_params=pltpu.CompilerParams(vmem_limit_bytes=VMEM_LIMIT_V7X))(blk)


def gather_weights(name, blks):
    n = len(blks)

    def body(*refs):
        ins, outs = refs[:n], refs[n:2 * n]
        send_sems, recv_sems, local_sems = refs[2 * n:]
        x, y, c = _place()
        me, sibling = (x, y, c), (x, y, 1 - c)
        chips = [(1 - x, y), (x, 1 - y), (1 - x, 1 - y)]

        def copy(a, k, block, to, own=False):
            dst = outs[a].at[4 * block[0] + 2 * block[1] + block[2]]
            return pltpu.make_async_remote_copy(
                src_ref=ins[a] if own else dst, dst_ref=dst, send_sem=send_sems.at[7 * a + k],
                recv_sem=recv_sems.at[7 * a + k], device_id=to, device_id_type=MESH)

        mine = [pltpu.make_async_copy(ins[a], outs[a].at[4 * x + 2 * y + c], local_sems.at[a]) for a in range(n)]
        for cp in mine:
            cp.start()
        first = []
        for j, chip in enumerate(chips):
            first += [copy(a, 1 + j, me, (*chip, c), own=True) for a in range(n)]
        first += [copy(a, 0, me, sibling, own=True) for a in range(n)]
        for cp in first:
            cp.start()
        passed = []
        for a in range(n):
            for j, chip in enumerate(chips):
                copy(a, 1 + j, (*chip, c), me).wait_recv()
                passed.append(copy(a, 4 + j, (*chip, c), sibling))
                passed[-1].start()
        for a in range(n):
            copy(a, 0, sibling, me).wait_recv()
            for j, chip in enumerate(chips):
                copy(a, 4 + j, (*chip, 1 - c), me).wait_recv()
        for cp in first + passed:
            cp.wait_send()
        for cp in mine:
            cp.wait()

    any_spec = pl.BlockSpec(memory_space=pl.ANY)
    return pl.pallas_call(
        body, name=name, out_shape=[jax.ShapeDtypeStruct((8,) + b.shape, b.dtype) for b in blks],
        in_specs=[any_spec] * n, out_specs=[any_spec] * n,
        scratch_shapes=[pltpu.SemaphoreType.DMA((7 * n,)), pltpu.SemaphoreType.DMA((7 * n,)),
                        pltpu.SemaphoreType.DMA((n,))])(*blks)


def _exchange(name, srcs, out_shapes, ncp, plan):
    n = len(srcs)

    def body(*refs):
        ins, outs = refs[:n], refs[n:n + len(out_shapes)]
        send_sems, recv_sems = refs[n + len(out_shapes):]
        cps = [pltpu.make_async_remote_copy(src_ref=s, dst_ref=d, send_sem=send_sems.at[k], recv_sem=recv_sems.at[k],
                                            device_id=to, device_id_type=MESH)
               for k, (s, d, to) in enumerate(plan(_place(), ins, outs))]
        for cp in cps:
            cp.start()
        for cp in cps:
            cp.wait_recv()
        for cp in cps:
            cp.wait_send()

    any_spec = pl.BlockSpec(memory_space=pl.ANY)
    return pl.pallas_call(
        body, name=name, out_shape=out_shapes, in_specs=[any_spec] * n, out_specs=[any_spec] * len(out_shapes),
        scratch_shapes=[pltpu.SemaphoreType.DMA((ncp,)), pltpu.SemaphoreType.DMA((ncp,))])(*srcs)


def sibling_swap(name, gs):
    def plan(place, ins, outs):
        x, y, c = place
        return [(g.at[s, 1 - c], o.at[s], (x, y, 1 - c)) for g, o in zip(ins, outs) for s in range(4)]

    return _exchange(name, gs, [jax.ShapeDtypeStruct((4,) + g.shape[2:], g.dtype) for g in gs], 4 * len(gs), plan)


def chip_scatter(name, ss):
    def plan(place, ins, outs):
        x, y, c = place
        chips = [(1 - x, y), (x, 1 - y), (1 - x, 1 - y)]
        return [(s.at[2 * px + py], o.at[k], (px, py, c)) for k, (px, py) in enumerate(chips) for s, o in zip(ins, outs)]

    return _exchange(name, ss, [jax.ShapeDtypeStruct((3,) + s.shape[1:], s.dtype) for s in ss], 3 * len(ss), plan)


def sibling_pair(name, hs):
    n = len(hs)

    def body(*refs):
        ins, outs = refs[:n], refs[n:2 * n]
        send_sems, recv_sems = refs[2 * n:]
        x, y, c = _place()
        cps = [pltpu.make_async_remote_copy(src_ref=outs[a].at[c], dst_ref=outs[a].at[c], send_sem=send_sems.at[a],
                                            recv_sem=recv_sems.at[a], device_id=(x, y, 1 - c), device_id_type=MESH)
               for a in range(n)]
        for cp in cps:
            cp.start()
        for a in range(n):
            pltpu.make_async_remote_copy(src_ref=outs[a].at[1 - c], dst_ref=outs[a].at[1 - c], send_sem=send_sems.at[a],
                                         recv_sem=recv_sems.at[a], device_id=(x, y, 1 - c),
                                         device_id_type=MESH).wait_recv()
        for cp in cps:
            cp.wait_send()

    any_spec = pl.BlockSpec(memory_space=pl.ANY)
    return pl.pallas_call(
        body, name=name, out_shape=[jax.ShapeDtypeStruct(h.shape, h.dtype) for h in hs],
        in_specs=[any_spec] * n, out_specs=[any_spec] * n, input_output_aliases={a: a for a in range(n)},
        scratch_shapes=[pltpu.SemaphoreType.DMA((n,)), pltpu.SemaphoreType.DMA((n,))])(*hs)


def chip_sum(name, g, theirs):
    r, w = g.shape[2:]

    def body(_, a_ref, b_ref, o_ref):
        o_ref[...] = (a_ref[...].astype(F32) + b_ref[...].astype(F32)).astype(BF16)

    grid_spec = pltpu.PrefetchScalarGridSpec(
        num_scalar_prefetch=1, grid=(4,),
        in_specs=[pl.BlockSpec((None, None, r, w), lambda s, at: (s, at[0], 0, 0)),
                  pl.BlockSpec((None, r, w), lambda s, at: (s, 0, 0))],
        out_specs=pl.BlockSpec((None, r, w), lambda s, at: (s, 0, 0)))
    at = jnp.stack([lax.axis_index("c")]).astype(jnp.int32)
    return pl.pallas_call(body, name=name, grid_spec=grid_spec, out_shape=jax.ShapeDtypeStruct((4, r, w), BF16),
                          compiler_params=_cp("arbitrary"))(at, g, theirs)


def shard_sum(name, s4, others):
    r, w = s4.shape[1:]

    def body(_, a_ref, b0, b1, b2, o_ref):
        o_ref[...] = ((a_ref[...].astype(F32) + b0[...].astype(F32)) + b1[...].astype(F32)) + b2[...].astype(F32)

    oth = [pl.BlockSpec((None, r, w), lambda i, at, k=k: (k, 0, 0)) for k in range(3)]
    grid_spec = pltpu.PrefetchScalarGridSpec(
        num_scalar_prefetch=1, grid=(1,),
        in_specs=[pl.BlockSpec((None, r, w), lambda i, at: (at[0], 0, 0))] + oth,
        out_specs=pl.BlockSpec((None, r, w), lambda i, at: (at[1], 0, 0)))
    at = jnp.stack([2 * lax.axis_index("x") + lax.axis_index("y"), lax.axis_index("c")]).astype(jnp.int32)
    return pl.pallas_call(body, name=name, grid_spec=grid_spec, out_shape=jax.ShapeDtypeStruct((2, r, w), F32),
                          compiler_params=_cp("arbitrary"))(at, s4, others, others, others)


W_IN_RUNS = ((0, 2, 1296, 376), (376, 3, 0, 1672), (2048, 1, 920, 752), (2800, 2, 0, 1296), (4096, 0, 0, 1024),
             (5120, 0, 1024, 648), (5768, 1, 0, 920))


def w_in_to_padded(name, g4):
    T = 128

    def body(g_ref, o_ref):
        o_ref[:, D_PROJ:PW] = jnp.zeros((T, PW - D_PROJ), o_ref.dtype)
        for (a, s, j0, w) in W_IN_RUNS:
            o_ref[:, a:a + w] = g_ref[s, :, j0:j0 + w]

    return pl.pallas_call(body, name=name, grid=(D // T,), in_specs=[pl.BlockSpec((4, T, 1672), lambda i: (0, i, 0))],
                          out_specs=pl.BlockSpec((T, PW), lambda i: (i, 0)),
                          out_shape=jax.ShapeDtypeStruct((D, PW), g4.dtype), compiler_params=_cp("arbitrary"))(g4)


def w_in_from_padded(name, gp):
    T = 128

    def body(g_ref, o_ref):
        for (a, s, j0, w) in W_IN_RUNS:
            o_ref[s, :, j0:j0 + w] = g_ref[:, a:a + w]

    return pl.pallas_call(body, name=name, grid=(D // T,), in_specs=[pl.BlockSpec((T, PW), lambda i: (i, 0))],
                          out_specs=pl.BlockSpec((4, T, 1672), lambda i: (0, i, 0)),
                          out_shape=jax.ShapeDtypeStruct((4, D, 1672), gp.dtype), compiler_params=_cp("arbitrary"))(gp)


def sum_devices(name, g):
    def body(g_ref, o_ref):
        acc = g_ref[0]
        for k in range(1, 8):
            acc = acc + g_ref[k]
        o_ref[...] = acc

    return pl.pallas_call(body, name=name, out_shape=jax.ShapeDtypeStruct(g.shape[1:], F32),
                          compiler_params=pltpu.CompilerParams(vmem_limit_bytes=VMEM_LIMIT_V7X))(g)


def adamw(name, w, g, m, v, T):
    r, wd = w.shape
    c1 = 1.0 - ADAM_B1 ** ADAM_STEP
    c2 = 1.0 - ADAM_B2 ** ADAM_STEP

    def body(w_ref, g_ref, m_ref, v_ref, d_ref, mo_ref, vo_ref):
        gv = g_ref[...]
        mn = ADAM_B1 * m_ref[...] + (1.0 - ADAM_B1) * gv
        vn = ADAM_B2 * v_ref[...] + (1.0 - ADAM_B2) * (gv * gv)
        d_ref[...] = -ADAM_LR * ((mn / c1) / (jnp.sqrt(vn / c2) + ADAM_EPS) + ADAM_WD * w_ref[...])
        mo_ref[...] = mn
        vo_ref[...] = vn

    spec = pl.BlockSpec((T, wd), lambda i: (i, 0))
    return pl.pallas_call(body, name=name, grid=(r // T,), in_specs=[spec] * 4, out_specs=[spec] * 3,
                          out_shape=[jax.ShapeDtypeStruct((r, wd), F32)] * 3, compiler_params=_cp("arbitrary"))(w, g, m, v)


BIG = {'w_in': (1024, 1672), 'w_ssd_proj': (256, 1024), 'w_gm_proj': (256, 1024), 'w_out': (256, 1024),
       'w_ff1': (1024, 704), 'w_ff3': (1024, 704), 'w_ff2': (704, 1024)}


class Flat:
    def __init__(self, segs):
        self.off, o = {}, 0
        for name, size in segs:
            self.off[name] = (o, size)
            o += -(-size // 128) * 128
        self.rows = -(-o // 1024) * 8

    def pack(self, vals):
        parts = []
        for name, (o, size) in self.off.items():
            v = vals[name].reshape(-1).astype(F32)
            parts.append(jnp.pad(v, (0, -(-size // 128) * 128 - size)))
        buf = jnp.concatenate(parts)
        return jnp.pad(buf, (0, self.rows * 128 - buf.shape[0])).reshape(self.rows, 128)

    def get(self, buf, name, shape=None):
        o, size = self.off[name]
        v = buf.reshape(-1)[o:o + size]
        return v if shape is None else v.reshape(shape)


PARTIALS = Flat([('loss', 1), ('ln0_g', D), ('ln0_b', D), ('dmod_x', 6 * D), ('dmod_c', 6 * D), ('conv_w', 5 * 1536),
                 ('conv_b', 1536), ('dt_bias', 32), ('a_log', 32), ('d_skip', 32), ('ssd_norm_g', D),
                 ('gm_norm_g', D), ('gm_norm_b', D), ('w_spatial', 8 * Q * Q), ('b_spatial', 8 * Q), ('b_gate', 2 * D),
                 ('ln1_g', D), ('ln1_b', D), ('ln2_g', D), ('ln2_b', D)])

WEIGHTS = ('c_ctx', 'ln0_g', 'ln0_b', 'w_ada', 'b_ada', 'w_in', 'conv_w', 'conv_b', 'dt_bias', 'a_log', 'd_skip',
           'ssd_norm_g', 'gm_norm_g', 'gm_norm_b', 'w_spatial', 'b_spatial', 'b_gate', 'w_ssd_proj', 'w_gm_proj',
           'w_out', 'ln1_g', 'ln1_b', 'w_ff1', 'w_ff3', 'w_ff2', 'ln2_g', 'ln2_b')
BIG_NAMES = tuple(BIG)
SMALL_NAMES = tuple(n for n in WEIGHTS if n not in BIG_NAMES and n != 'w_ada')


def kernel(x, c, ctx, c_ctx, ln0_g, ln0_b, w_ada, b_ada, w_in, conv_w, conv_b, dt_bias, a_log, d_skip, ssd_norm_g, gm_norm_g, gm_norm_b, w_spatial, b_spatial, b_gate, w_ssd_proj, w_gm_proj, w_out, ln1_g, ln1_b, w_ff1, w_ff3, w_ff2, ln2_g, ln2_b, loss_target, m_c_ctx, m_ln0_g, m_ln0_b, m_w_ada, m_b_ada, m_w_in, m_conv_w, m_conv_b, m_dt_bias, m_a_log, m_d_skip, m_ssd_norm_g, m_gm_norm_g, m_gm_norm_b, m_w_spatial, m_b_spatial, m_b_gate, m_w_ssd_proj, m_w_gm_proj, m_w_out, m_ln1_g, m_ln1_b, m_w_ff1, m_w_ff3, m_w_ff2, m_ln2_g, m_ln2_b, v_c_ctx, v_ln0_g, v_ln0_b, v_w_ada, v_b_ada, v_w_in, v_conv_w, v_conv_b, v_dt_bias, v_a_log, v_d_skip, v_ssd_norm_g, v_gm_norm_g, v_gm_norm_b, v_w_spatial, v_b_spatial, v_b_gate, v_w_ssd_proj, v_w_gm_proj, v_w_out, v_ln1_g, v_ln1_b, v_w_ff1, v_w_ff3, v_w_ff2, v_ln2_g, v_ln2_b):
    wts = dict(c_ctx=c_ctx, ln0_g=ln0_g, ln0_b=ln0_b, w_ada=w_ada, b_ada=b_ada, w_in=w_in, conv_w=conv_w, conv_b=conv_b,
               dt_bias=dt_bias, a_log=a_log, d_skip=d_skip, ssd_norm_g=ssd_norm_g, gm_norm_g=gm_norm_g,
               gm_norm_b=gm_norm_b, w_spatial=w_spatial, b_spatial=b_spatial, b_gate=b_gate, w_ssd_proj=w_ssd_proj,
               w_gm_proj=w_gm_proj, w_out=w_out, ln1_g=ln1_g, ln1_b=ln1_b, w_ff1=w_ff1, w_ff3=w_ff3, w_ff2=w_ff2,
               ln2_g=ln2_g, ln2_b=ln2_b)
    ms = dict(zip(WEIGHTS, (m_c_ctx, m_ln0_g, m_ln0_b, m_w_ada, m_b_ada, m_w_in, m_conv_w, m_conv_b, m_dt_bias, m_a_log,
                            m_d_skip, m_ssd_norm_g, m_gm_norm_g, m_gm_norm_b, m_w_spatial, m_b_spatial, m_b_gate,
                            m_w_ssd_proj, m_w_gm_proj, m_w_out, m_ln1_g, m_ln1_b, m_w_ff1, m_w_ff3, m_w_ff2, m_ln2_g,
                            m_ln2_b)))
    vs = dict(zip(WEIGHTS, (v_c_ctx, v_ln0_g, v_ln0_b, v_w_ada, v_b_ada, v_w_in, v_conv_w, v_conv_b, v_dt_bias, v_a_log,
                            v_d_skip, v_ssd_norm_g, v_gm_norm_g, v_gm_norm_b, v_w_spatial, v_b_spatial, v_b_gate,
                            v_w_ssd_proj, v_w_gm_proj, v_w_out, v_ln1_g, v_ln1_b, v_w_ff1, v_w_ff3, v_w_ff2, v_ln2_g,
                            v_ln2_b)))
    px, py, pc = _place()
    shard = 2 * px + py
    dev = 2 * shard + pc
    take = lambda a, i, axis=0: lax.dynamic_index_in_dim(a, i, axis, keepdims=False)

    pre = jnp.concatenate([c, jnp.pad(conv_w[0], ((0, 0), (0, D - 384))), jnp.zeros((2, D), F32)], axis=0)
    pre = allgather8("gather_cond", pre, False)
    conv_w_full = pre[0::2, 1:6, :384].transpose(1, 0, 2).reshape(5, 1536)
    a16 = jnp.concatenate([_silu(pre[:, 0, :]), _silu(c_ctx)[None], jnp.zeros((7, D), F32)], axis=0)
    mod = matmul("ada_fwd", a16, w_ada[0], 'nn', 16, 512, 1024)
    mod = mod + lax.dynamic_slice_in_dim(b_ada[0], shard * 1536, 1536)[None]
    mod = allgather8("gather_mod", mod, False)
    mod = jnp.concatenate([mod[0], mod[2], mod[4], mod[6]], axis=1)
    mod_x = take(mod, dev).reshape(6, D)
    mod_c = mod[8].reshape(6, D)

    halves = [take(wts[n][0].reshape(2, BIG[n][0] // 2, BIG[n][1]), pc).astype(BF16) for n in BIG_NAMES]
    W = {}
    for n, blocks in zip(BIG_NAMES, gather_weights("gather_weights", halves)):
        r, w = BIG[n]
        W[n] = blocks.reshape(4, r, w) if w != D else blocks.reshape(4 * r, w)
    W['w_in'] = w_in_to_padded("w_in_layout", W['w_in'])

    S = dict(ln0_g=ln0_g, ln0_b=ln0_b, conv_w=conv_w_full, conv_b=conv_b[0], dt_bias=dt_bias[0], a_log=a_log[0],
             d_skip=d_skip[0], ssd_norm_g=ssd_norm_g[0], gm_norm_g=gm_norm_g[0], gm_norm_b=gm_norm_b[0],
             w_spatial=w_spatial[0], b_spatial=b_spatial[0], b_gate=b_gate[0], ln1_g=ln1_g[0], ln1_b=ln1_b[0],
             ln2_g=ln2_g[0], ln2_b=ln2_b[0])
    x_all = jnp.concatenate([ctx[0], x[0]], axis=0)
    grad_x, gbig, gsmall = core(x_all, loss_target[0], mod_x, mod_c, W, S)

    gbig['w_in'] = w_in_from_padded("w_in_grad_layout", gbig['w_in'])
    blocks = [gbig[n].reshape(4, 2, BIG[n][0] // 2, BIG[n][1]) for n in BIG_NAMES]
    theirs = sibling_swap("grads_to_sibling", blocks)
    sums = [chip_sum("grads_chip_sum_" + n, b, t) for n, b, t in zip(BIG_NAMES, blocks, theirs)]
    others = chip_scatter("grads_to_chips", sums)
    halves = [shard_sum("grads_sum_" + n, s, o) for n, s, o in zip(BIG_NAMES, sums, others)]
    g_shards = {n: h.reshape(BIG[n]) for n, h in zip(BIG_NAMES, sibling_pair("grads_halves", halves))}

    parts = allgather8("gather_partials", PARTIALS.pack(gsmall), False)
    tot = sum_devices("partials_sum", parts)
    g = {n: PARTIALS.get(tot, n) for n in ('ln0_g', 'ln0_b', 'conv_b', 'dt_bias', 'a_log', 'd_skip', 'ssd_norm_g',
                                           'gm_norm_g', 'gm_norm_b', 'w_spatial', 'b_spatial', 'b_gate', 'ln1_g',
                                           'ln1_b', 'ln2_g', 'ln2_b')}
    loss = PARTIALS.get(tot, 'loss', ())
    dmod_c = PARTIALS.get(tot, 'dmod_c')
    g['b_ada'] = PARTIALS.get(tot, 'dmod_x') + dmod_c
    g['conv_w'] = lax.dynamic_slice_in_dim(PARTIALS.get(tot, 'conv_w', (5, 1536)), shard * 384, 384, axis=1)
    o, size = PARTIALS.off['dmod_x']
    dmod_rows = parts.reshape(8, -1)[:, o:o + size]
    dm = jnp.concatenate([dmod_rows, dmod_c[None], jnp.zeros((7, 6 * D), F32)], axis=0)
    dm = lax.dynamic_slice_in_dim(dm, shard * 1536, 1536, axis=1)
    g['w_ada'] = matmul("ada_bwd_w", a16, dm, 'tn', 1024, 512, 16)
    dm_c = jnp.concatenate([dm[8:9], jnp.zeros((15, 1536), F32)], axis=0)
    dc = matmul("ada_bwd_c", dm_c, w_ada[0], 'nt', 16, 1024, 512)
    dc = allgather8("gather_dcctx", dc, False)[:, 0, :]
    dc = ((dc[0] + dc[2]) + dc[4]) + dc[6]
    sg = jax.nn.sigmoid(c_ctx)
    g['c_ctx'] = dc * (sg * (1.0 + c_ctx * (1.0 - sg)))
    for n in BIG_NAMES:
        g[n] = g_shards[n]

    delta, new_m, new_v = {}, {}, {}
    for n in BIG_NAMES + ('w_ada',):
        w2 = wts[n][0]
        T = 352 if n == 'w_ff2' else 256
        d_, m_, v_ = adamw("adamw_" + n, w2, g[n], ms[n][0], vs[n][0], T)
        delta[n], new_m[n], new_v[n] = d_, m_, v_
    lay = Flat([(n, wts[n].size) for n in SMALL_NAMES])
    d_, m_, v_ = adamw("adamw_small", lay.pack(wts), lay.pack(g), lay.pack(ms), lay.pack(vs), lay.rows)
    for n in SMALL_NAMES:
        delta[n], new_m[n], new_v[n] = (lay.get(b, n) for b in (d_, m_, v_))

    shp = lambda d: [d[n].reshape(wts[n].shape) for n in WEIGHTS]
    return (loss, grad_x[None], *shp(g), *shp(delta), *shp(new_m), *shp(new_v))
```

```python
import functools

import jax
import jax.numpy as jnp
from jax import lax
from jax.experimental import pallas as pl
from jax.experimental.pallas import tpu as pltpu

F32 = jnp.float32
BF16 = jnp.bfloat16
MESH = pl.DeviceIdType.MESH

VMEM_LIMIT_V7X = 56 * 1024 * 1024

D = 1024
LC = 256
Q = 128
NH = 16
D_FF = 2816
LN_EPS = 1e-5
ALPHA = 2.0 ** 0.25

PW = 7168
C_GATE, C_UV, C_Z, C_XBC, C_DT = 0, 2048, 4096, 5120, 6656
D_PROJ = 6688

ADAM_LR, ADAM_B1, ADAM_B2, ADAM_EPS, ADAM_WD, ADAM_STEP = 0.001, 0.9, 0.999, 1e-08, 0.01, 10


def _cp(*sem):
    return pltpu.CompilerParams(dimension_semantics=sem, vmem_limit_bytes=VMEM_LIMIT_V7X)


def _dot(a, b, ca, cb):
    return lax.dot_general(a.astype(BF16), b.astype(BF16), (((ca,), (cb,)), ((), ())),
                           preferred_element_type=F32)


@jax.custom_vjp
def mm(a, b):
    return _dot(a, b, 1, 0)


mm.defvjp(lambda a, b: (_dot(a, b, 1, 0), (a, b)),
          lambda r, g: (_dot(g, r[1], 1, 1), _dot(r[0], g, 0, 0)))


@jax.custom_vjp
def mm_nt(a, b):
    return _dot(a, b, 1, 1)


mm_nt.defvjp(lambda a, b: (_dot(a, b, 1, 1), (a, b)),
             lambda r, g: (_dot(g, r[1], 1, 0), _dot(g, r[0], 0, 0)))


@jax.custom_vjp
def mm_tn(a, b):
    return _dot(a, b, 0, 0)


mm_tn.defvjp(lambda a, b: (_dot(a, b, 0, 0), (a, b)),
             lambda r, g: (_dot(r[1], g, 1, 1), _dot(r[0], g, 1, 0)))


def _dot32(a, b):
    return lax.dot_general(a, b, (((1,), (0,)), ((), ())), precision=lax.Precision.HIGHEST,
                           preferred_element_type=F32)


def _cumsum_fn(rev):
    def tri(transpose):
        r = lax.broadcasted_iota(jnp.int32, (Q, Q), 0)
        c = lax.broadcasted_iota(jnp.int32, (Q, Q), 1)
        keep = (r >= c) if (rev == transpose) else (r <= c)
        return jnp.where(keep, 1.0, 0.0).astype(F32)

    @jax.custom_vjp
    def cums(a):
        return _dot32(tri(False), a)

    cums.defvjp(lambda a: (_dot32(tri(False), a), None), lambda _, g: (_dot32(tri(True), g),))
    return cums


def _cols(v, k):
    w = v.shape[1] // k
    return tuple(v[:, w * i:w * (i + 1)] for i in range(k))


def _splitter(k):
    @jax.custom_vjp
    def split(v):
        return _cols(v, k)

    @jax.custom_vjp
    def concat(ps):
        return jnp.concatenate(ps, axis=1)

    split.defvjp(lambda v: (_cols(v, k), None), lambda _, g: (jnp.concatenate(g, axis=1),))
    concat.defvjp(lambda ps: (jnp.concatenate(ps, axis=1), None), lambda _, g: (_cols(g, k),))
    return split, concat


split2, _ = _splitter(2)
split4, _ = _splitter(4)
split8, concat8 = _splitter(8)


def _ln(x, g, b):
    mu = jnp.mean(x, axis=-1, keepdims=True)
    xc = x - mu
    var = jnp.mean(xc * xc, axis=-1, keepdims=True)
    return xc * lax.rsqrt(var + LN_EPS) * g + b


def _silu(x):
    return x * jax.nn.sigmoid(x)


def _gelu(x):
    return 0.5 * x * (1.0 + jnp.tanh(0.7978845608028654 * (x + 0.044715 * (x * x * x))))


def _xspec(T, w, col, roff):
    return pl.BlockSpec((T, w), lambda i, col=col, roff=roff: (jnp.maximum(i + roff, 0), col))


def _pspec(p, sel):
    if sel is None:
        return pl.BlockSpec(p.shape, lambda i, n=p.ndim: (0,) * n)
    return pl.BlockSpec((1,) + p.shape[1:], lambda i, n=p.ndim: (sel(i),) + (0,) * (n - 1))


def _out_plumbing(outs, T, args, in_specs):
    shapes, specs, aliases = [], [], {}
    for k, o in enumerate(outs):
        if o[0] == 'new':
            _, rows, w, roff = o[:4]
            shapes.append(jax.ShapeDtypeStruct((rows, w), o[4] if len(o) > 4 else F32))
            specs.append(_xspec(T, w, 0, roff))
        elif o[0] == 'acc':
            shapes.append(jax.ShapeDtypeStruct(o[1], F32))
            specs.append(pl.BlockSpec(o[1], lambda i, n=len(o[1]): (0,) * n))
        elif o[0] == 'part':
            _, rows, wtot, w, col, roff, dtype = o
            shapes.append(jax.ShapeDtypeStruct((rows, wtot), dtype))
            specs.append(_xspec(T, w, col, roff))
        else:
            _, arr, w, col, roff = o
            aliases[len(args)] = k
            args.append(arr)
            in_specs.append(pl.BlockSpec(memory_space=pl.ANY))
            shapes.append(jax.ShapeDtypeStruct(arr.shape, arr.dtype))
            specs.append(_xspec(T, w, col, roff))
    return shapes, specs, aliases


def stage_fwd(name, f, T, n, xs, ps, outs):
    nx, npar = len(xs), len(ps)
    args = [x[0] for x in xs] + [p[0] for p in ps]
    in_specs = [_xspec(T, w, col, roff) for (_, w, col, roff) in xs] + [_pspec(p, sel) for (p, sel) in ps]
    n_in = len(args)
    shapes, specs, aliases = _out_plumbing(outs, T, args, in_specs)
    n_all_in = len(args)

    def body(*refs):
        i = pl.program_id(0)
        xv = [r[...] for r in refs[:nx]]
        pv = [r[...] if ps[k][1] is None else r[0] for k, r in enumerate(refs[nx:n_in])]
        res = f(*xv, *pv)
        for k, o_ref in enumerate(refs[n_all_in:]):
            if outs[k][0] == 'acc':
                @pl.when(i == 0)
                def _(o_ref=o_ref, v=res[k]):
                    o_ref[...] = v

                @pl.when(i > 0)
                def _(o_ref=o_ref, v=res[k]):
                    o_ref[...] += v
            else:
                o_ref[...] = res[k].astype(o_ref.dtype)

    return pl.pallas_call(body, name=name, grid=(n,), in_specs=in_specs, out_specs=specs, out_shape=shapes,
                          input_output_aliases=aliases, compiler_params=_cp("arbitrary"))(*args)


def stage_bwd(name, f, T, n, xs, ps, cts, dxs, dps, primal=()):
    nx, npar = len(xs), len(ps)
    args = [x[0] for x in xs] + [p[0] for p in ps]
    in_specs = [_xspec(T, w, col, roff) for (_, w, col, roff) in xs] + [_pspec(p, sel) for (p, sel) in ps]
    ct_arrs = [c for c in cts if isinstance(c, tuple)]
    for (a, w, col, roff) in ct_arrs:
        args.append(a)
        in_specs.append(_xspec(T, w, col, roff))
    n_in = len(args)
    outs, out_of = [], []
    for k, o in enumerate(dxs):
        if o is not None:
            outs.append(o)
            out_of.append(('x', k))
    for k, want in enumerate(dps):
        if want:
            p, sel = ps[k]
            outs.append(('acc', p.shape))
            out_of.append(('p', k))
    for k, shape in primal:
        outs.append(('acc', shape))
        out_of.append(('r', k))
    shapes, specs, aliases = _out_plumbing(outs, T, args, in_specs)
    for j, (kind, k) in enumerate(out_of):
        if kind == 'p' and ps[k][1] is not None:
            p, sel = ps[k]
            specs[j] = pl.BlockSpec((1,) + p.shape[1:], lambda i, n=p.ndim, sel=sel: (sel(i),) + (0,) * (n - 1))
    n_all_in = len(args)

    def body(*refs):
        i = pl.program_id(0)
        xv = [r[...] for r in refs[:nx]]
        pv = [r[...] if ps[k][1] is None else r[0] for k, r in enumerate(refs[nx:nx + npar])]
        res, vjp_fn = jax.vjp(f, *xv, *pv)
        ctv, q = [], nx + npar
        for k, c in enumerate(cts):
            if c is None:
                ctv.append(jnp.zeros_like(res[k]))
            elif isinstance(c, tuple):
                v = refs[q][...]
                if c[3] < 0:
                    v = v * (i + c[3] >= 0).astype(F32)
                ctv.append(v)
                q += 1
            else:
                ctv.append(jnp.full_like(res[k], c))
        grads = vjp_fn(tuple(ctv))
        for j, o_ref in enumerate(refs[n_all_in:]):
            kind, k = out_of[j]
            if kind == 'x':
                o_ref[...] = grads[k].astype(o_ref.dtype)
            else:
                g = res[k] if kind == 'r' else grads[nx + k]
                sel = None if kind == 'r' else ps[k][1]
                if sel is None:
                    first = i == 0
                    tgt = o_ref
                else:
                    first = jnp.logical_or(i == 0, sel(i) != sel(jnp.maximum(i - 1, 0)))
                    tgt = o_ref.at[0]

                @pl.when(first)
                def _(tgt=tgt, g=g):
                    tgt[...] = g

                @pl.when(jnp.logical_not(first))
                def _(tgt=tgt, g=g):
                    tgt[...] += g

    return pl.pallas_call(body, name=name, grid=(n,), in_specs=in_specs, out_specs=specs, out_shape=shapes,
                          input_output_aliases=aliases, compiler_params=_cp("arbitrary"))(*args)


_CONTRACT = {'nn': (1, 0), 'nt': (1, 1), 'tn': (0, 0)}


def matmul(name, a, b, mode, tm, tn, tk, out_dtype=F32, add=None):
    if mode == 'nn':
        (M, K), (_, N) = a.shape, b.shape
    elif mode == 'nt':
        (M, K), (N, _) = a.shape, b.shape
    else:
        (K, M), (_, N) = a.shape, b.shape
    assert M % tm == 0 and N % tn == 0 and K % tk == 0, (name, M, N, K, tm, tn, tk)
    a_spec = (pl.BlockSpec((tk, tm), lambda j, i, k: (k, i)) if mode == 'tn'
              else pl.BlockSpec((tm, tk), lambda j, i, k: (i, k)))
    b_spec = (pl.BlockSpec((tn, tk), lambda j, i, k: (j, k)) if mode == 'nt'
              else pl.BlockSpec((tk, tn), lambda j, i, k: (k, j)))
    o_spec = pl.BlockSpec((tm, tn), lambda j, i, k: (i, j))
    return matmul_call(name, (N // tn, M // tm, K // tk), a, a_spec, b, b_spec, (M, N), o_spec, (tm, tn), mode,
                       out_dtype, add)


def matmul_call(name, grid, a, a_spec, b, b_spec, out_shape, o_spec, tile, mode, out_dtype=F32, add=None):
    tm, tn = tile
    nk = grid[2]
    ca, cb = _CONTRACT[mode]
    args, in_specs = [a, b], [a_spec, b_spec]
    if add is not None:
        args.append(add)
        in_specs.append(o_spec)

    def body(*refs):
        a_ref, b_ref = refs[0], refs[1]
        o_ref, acc = refs[-2], refs[-1]
        k = pl.program_id(2)
        p = _dot(a_ref[...], b_ref[...], ca, cb)

        @pl.when(k == 0)
        def _():
            acc[...] = p + refs[2][...] if add is not None else p

        @pl.when(k > 0)
        def _():
            acc[...] += p

        @pl.when(k == nk - 1)
        def _():
            o_ref[...] = acc[...].astype(out_dtype)

    return pl.pallas_call(body, name=name, grid=grid, in_specs=in_specs, out_specs=o_spec,
                          out_shape=jax.ShapeDtypeStruct(out_shape, out_dtype),
                          scratch_shapes=[pltpu.VMEM((tm, tn), F32)],
                          compiler_params=_cp("arbitrary", "arbitrary", "arbitrary"))(*args)


NS, WS = 4, 704


def _resident(name, M, tm, rows, weight, out_shape, out_block, out_map, step, add=None):
    args = [rows[0], weight] + ([] if add is None else [add])
    in_specs = [pl.BlockSpec(rows[1], rows[2]), pl.BlockSpec(weight.shape, lambda i, n=weight.ndim: (0,) * n)]
    if add is not None:
        in_specs.append(pl.BlockSpec(out_block, out_map))
    return pl.pallas_call(step, name=name, grid=(M // tm,), in_specs=in_specs, out_specs=pl.BlockSpec(out_block, out_map),
                          out_shape=jax.ShapeDtypeStruct(out_shape, F32), compiler_params=_cp("arbitrary"))(*args)


def ff_in_fwd(name, h, w3, tm):
    M = h.shape[0]

    def step(h_ref, w_ref, o_ref):
        for s in range(NS):
            o_ref[s] = _dot(h_ref[...], w_ref[s], 1, 0)

    return _resident(name, M, tm, (h, (tm, D), lambda i: (i, 0)), w3, (NS, M, WS), (NS, tm, WS), lambda i: (0, i, 0), step)


def ff_in_bwd_x(name, da3, w3, tm, add=None):
    M = da3.shape[1]

    def step(*refs):
        d_ref, w_ref, o_ref = refs[0], refs[1], refs[-1]
        acc = _dot(d_ref[0], w_ref[0], 1, 1)
        for s in range(1, NS):
            acc = acc + _dot(d_ref[s], w_ref[s], 1, 1)
        o_ref[...] = acc if add is None else acc + refs[2][...]

    return _resident(name, M, tm, (da3, (NS, tm, WS), lambda i: (0, i, 0)), w3, (M, D), (tm, D), lambda i: (i, 0), step, add)


def ff_in_bwd_w(name, h, da3, tk):
    M = h.shape[0]
    return matmul_call(name, (NS, 1, M // tk), h, pl.BlockSpec((tk, D), lambda j, i, k: (k, 0)),
                       da3, pl.BlockSpec((None, tk, WS), lambda j, i, k: (j, k, 0)),
                       (NS, D, WS), pl.BlockSpec((None, D, WS), lambda j, i, k: (j, 0, 0)), (D, WS), 'tn', BF16)


def ff_out_fwd(name, act3, w2, tm):
    M = act3.shape[1]

    def step(a_ref, w_ref, o_ref):
        acc = _dot(a_ref[0], w_ref[0:WS, :], 1, 0)
        for s in range(1, NS):
            acc = acc + _dot(a_ref[s], w_ref[s * WS:(s + 1) * WS, :], 1, 0)
        o_ref[...] = acc

    return _resident(name, M, tm, (act3, (NS, tm, WS), lambda i: (0, i, 0)), w2, (M, D), (tm, D), lambda i: (i, 0), step)


def ff_out_bwd_x(name, dff, w2, tm):
    M = dff.shape[0]

    def step(d_ref, w_ref, o_ref):
        for s in range(NS):
            o_ref[s] = _dot(d_ref[...], w_ref[s * WS:(s + 1) * WS, :], 1, 1)

    return _resident(name, M, tm, (dff, (tm, D), lambda i: (i, 0)), w2, (NS, M, WS), (NS, tm, WS), lambda i: (0, i, 0), step)


def ff_out_bwd_w(name, act3, dff, tk):
    M = dff.shape[0]
    return matmul_call(name, (1, NS, M // tk), act3, pl.BlockSpec((None, tk, WS), lambda j, i, k: (i, k, 0)),
                       dff, pl.BlockSpec((tk, D), lambda j, i, k: (k, 0)),
                       (NS * WS, D), pl.BlockSpec((WS, D), lambda j, i, k: (i, 0)), (WS, D), 'tn', BF16)


def _shift_rows(x, d):
    n = x.shape[0]
    if d == 0:
        return x
    y = pltpu.roll(x, (-d) % n, 0)
    t = lax.broadcasted_iota(jnp.int32, x.shape, 0)
    ok = (t + d >= 0) & (t + d < n)
    return jnp.where(ok, y, 0.0)


def _conv_pre(x, w_ref, b_ref):
    acc = jnp.broadcast_to(b_ref[...], x.shape)
    for k in range(5):
        acc = acc + _shift_rows(x, k - 2) * w_ref[k:k + 1, :]
    return acc


def conv_fwd(name, proj, conv_w, conv_b, R):
    segs = ((0, LC), (LC, R))

    def body(x_ref, w_ref, b_ref, o_ref):
        for (s, e) in segs:
            pre = _conv_pre(x_ref[s:e, :], w_ref, b_ref)
            o_ref[s:e, :] = _silu(pre)

    return pl.pallas_call(
        body, name=name, grid=(12,),
        in_specs=[pl.BlockSpec((R, 128), lambda j: (0, C_XBC // 128 + j)),
                  pl.BlockSpec((8, 128), lambda j: (0, j)), pl.BlockSpec((1, 128), lambda j: (0, j))],
        out_specs=pl.BlockSpec((R, 128), lambda j: (0, j)),
        out_shape=jax.ShapeDtypeStruct((R, 1536), F32), compiler_params=_cp("arbitrary"))(proj, conv_w, conv_b)


def conv_bwd(name, proj, conv_w, conv_b, d_f, d_b, d_skip, dproj, R):
    segs = ((0, LC), (LC, R))

    def body(x_ref, w_ref, b_ref, df_ref, db_ref, ds_ref, _, dx_ref, dw_ref, dbias_ref):
        j = pl.program_id(0)
        has_skip = (j < 8).astype(F32)
        dw = [jnp.zeros((1, 128), F32) for _ in range(5)]
        dbias = jnp.zeros((1, 128), F32)
        for (s, e) in segs:
            x = x_ref[s:e, :]
            pre = _conv_pre(x, w_ref, b_ref)
            sig = jax.nn.sigmoid(pre)
            dy = df_ref[s:e, :] + db_ref[s:e, :]
            if s == LC:
                dy = dy + ds_ref[...] * has_skip
            dpre = dy * (sig * (1.0 + pre * (1.0 - sig)))
            dx = jnp.zeros_like(x)
            for k in range(5):
                dx = dx + _shift_rows(dpre, 2 - k) * w_ref[k:k + 1, :]
                dw[k] = dw[k] + jnp.sum(dpre * _shift_rows(x, k - 2), axis=0, keepdims=True)
            dbias = dbias + jnp.sum(dpre, axis=0, keepdims=True)
            dx_ref[s:e, :] = dx.astype(dx_ref.dtype)
        dw_ref[...] = jnp.zeros_like(dw_ref)
        for k in range(5):
            dw_ref[k:k + 1, :] = dw[k]
        dbias_ref[...] = dbias

    return pl.pallas_call(
        body, name=name, grid=(12,),
        in_specs=[pl.BlockSpec((R, 128), lambda j: (0, C_XBC // 128 + j)),
                  pl.BlockSpec((8, 128), lambda j: (0, j)), pl.BlockSpec((1, 128), lambda j: (0, j)),
                  pl.BlockSpec((R, 128), lambda j: (0, j)), pl.BlockSpec((R, 128), lambda j: (0, j)),
                  pl.BlockSpec((R - LC, 128), lambda j: (0, jnp.minimum(j, 7))),
                  pl.BlockSpec(memory_space=pl.ANY)],
        out_specs=[pl.BlockSpec((R, 128), lambda j: (0, C_XBC // 128 + j)),
                   pl.BlockSpec((8, 128), lambda j: (0, j)), pl.BlockSpec((1, 128), lambda j: (0, j))],
        out_shape=[jax.ShapeDtypeStruct(dproj.shape, dproj.dtype), jax.ShapeDtypeStruct((8, 1536), F32),
                   jax.ShapeDtypeStruct((1, 1536), F32)],
        input_output_aliases={6: 0}, compiler_params=_cp("arbitrary"))(proj, conv_w, conv_b, d_f, d_b, d_skip, dproj)


def _ssd_chunk(rev, dirn):
    cums = _cumsum_fn(rev)

    def f(xs, Bs, Cs, dt, alog, Hs):
        lane = lax.broadcasted_iota(jnp.int32, (1, 128), 1)
        sub = lax.broadcasted_iota(jnp.int32, (Q, 1), 0)
        r = lax.broadcasted_iota(jnp.int32, (Q, Q), 0)
        c = lax.broadcasted_iota(jnp.int32, (Q, Q), 1)
        mask = (r <= c) if rev else (r >= c)
        left = lane < 64
        a = dt * (-jnp.exp(alog))
        s = cums(a)
        sT, dtT = s.T, dt.T
        last_row = (sub == (0 if rev else Q - 1)).astype(F32)
        s_last = jnp.sum(s * last_row, axis=0, keepdims=True)
        G = [mm_nt(Cs[g], Bs[g]) for g in range(2)]
        M, es, wc, ed = [], [], [], []
        for h in range(NH):
            l = 16 * dirn + h
            oh_l = (lane == l).astype(F32)
            oh_s = (sub == l).astype(F32)
            s_col = jnp.sum(s * oh_l, axis=1, keepdims=True)
            dt_col = jnp.sum(dt * oh_l, axis=1, keepdims=True)
            s_row = jnp.sum(sT * oh_s, axis=0, keepdims=True)
            dt_row = jnp.sum(dtT * oh_s, axis=0, keepdims=True)
            sl = jnp.sum(s_last * oh_l, axis=1, keepdims=True)
            seg = jnp.where(mask, s_col - s_row, 0.0)
            lm = jnp.where(mask, jnp.exp(seg), 0.0)
            M.append(G[h // 8] * lm * dt_row)
            es.append(jnp.exp(s_col))
            wc.append(jnp.exp(sl - s_col) * dt_col)
            ed.append(jnp.exp(sl))
        Ys, Hn = [], []
        for j in range(8):
            g = j // 4
            xa = jnp.where(left, xs[j], 0.0)
            xb = jnp.where(left, 0.0, xs[j])
            yd = mm(M[2 * j], xa) + mm(M[2 * j + 1], xb)
            yo = mm(Cs[g], Hs[j]) * jnp.where(left, es[2 * j], es[2 * j + 1])
            Ys.append(yd + yo)
            st = mm_tn(Bs[g], xs[j] * jnp.where(left, wc[2 * j], wc[2 * j + 1]))
            Hn.append(Hs[j] * jnp.where(left, ed[2 * j], ed[2 * j + 1]) + st)
        return Ys, Hn

    return f


def _chunk_of(t, n, rev):
    if not rev:
        return t
    return jnp.where(t < 2, 1 - t, n + 1 - t)


def ssd_fwd(name, xbc, dt, alog, n, rev, dirn):
    chunk = _ssd_chunk(rev, dirn)

    def body(x_ref, b_ref, c_ref, dt_ref, al_ref, y_ref, hs_ref, h_scr):
        @pl.when(pl.program_id(0) == 0)
        def _():
            h_scr[...] = jnp.zeros_like(h_scr)

        xs = [x_ref[:, 128 * j:128 * (j + 1)] for j in range(8)]
        Bs = [b_ref[:, 128 * g:128 * (g + 1)] for g in range(2)]
        Cs = [c_ref[:, 128 * g:128 * (g + 1)] for g in range(2)]
        Hs = [h_scr[:, 128 * j:128 * (j + 1)] for j in range(8)]
        hs_ref[0] = h_scr[...]
        Ys, Hn = chunk(xs, Bs, Cs, dt_ref[...], al_ref[...], Hs)
        for j in range(8):
            y_ref[:, 128 * j:128 * (j + 1)] = Ys[j]
            h_scr[:, 128 * j:128 * (j + 1)] = Hn[j]

    cm = lambda t: _chunk_of(t, n, rev)
    return pl.pallas_call(
        body, name=name, grid=(n,),
        in_specs=[pl.BlockSpec((Q, 1024), lambda t: (cm(t), 0)), pl.BlockSpec((Q, 256), lambda t: (cm(t), 4)),
                  pl.BlockSpec((Q, 256), lambda t: (cm(t), 5)), pl.BlockSpec((Q, 128), lambda t: (cm(t), 0)),
                  pl.BlockSpec((1, 128), lambda t: (0, 0))],
        out_specs=[pl.BlockSpec((Q, 1024), lambda t: (cm(t), 0)), pl.BlockSpec((1, Q, 1024), lambda t: (cm(t), 0, 0))],
        out_shape=[jax.ShapeDtypeStruct((n * Q, 1024), F32), jax.ShapeDtypeStruct((n, Q, 1024), F32)],
        scratch_shapes=[pltpu.VMEM((Q, 1024), F32)], compiler_params=_cp("arbitrary"))(xbc, xbc, xbc, dt, alog)


def ssd_bwd(name, xbc, dt, alog, hs, dy, n, rev, dirn):
    chunk = _ssd_chunk(rev, dirn)

    def body(x_ref, b_ref, c_ref, dt_ref, al_ref, hs_ref, dy_ref, dx_ref, ddt_ref, dal_ref, dh_scr):
        tt = pl.program_id(0)
        ch = _chunk_of(n - 1 - tt, n, rev)

        @pl.when(tt == 0)
        def _():
            dh_scr[...] = jnp.zeros_like(dh_scr)

        xs = [x_ref[:, 128 * j:128 * (j + 1)] for j in range(8)]
        Bs = [b_ref[:, 128 * g:128 * (g + 1)] for g in range(2)]
        Cs = [c_ref[:, 128 * g:128 * (g + 1)] for g in range(2)]
        Hs = [hs_ref[0, :, 128 * j:128 * (j + 1)] for j in range(8)]
        live = (ch >= 2).astype(F32)
        dYs = [dy_ref[:, 128 * j:128 * (j + 1)] * live for j in range(8)]
        dHn = [dh_scr[:, 128 * j:128 * (j + 1)] for j in range(8)]
        _, vjp_fn = jax.vjp(chunk, xs, Bs, Cs, dt_ref[...], al_ref[...], Hs)
        dxs, dBs, dCs, ddt, dal, dHs = vjp_fn((dYs, dHn))
        for j in range(8):
            dx_ref[:, 128 * j:128 * (j + 1)] = dxs[j]
            dh_scr[:, 128 * j:128 * (j + 1)] = dHs[j]
        for g in range(2):
            dx_ref[:, 1024 + 128 * g:1024 + 128 * (g + 1)] = dBs[g]
            dx_ref[:, 1280 + 128 * g:1280 + 128 * (g + 1)] = dCs[g]
        ddt_ref[...] = ddt

        @pl.when(tt == 0)
        def _():
            dal_ref[...] = dal

        @pl.when(tt > 0)
        def _():
            dal_ref[...] += dal

    cm = lambda t: _chunk_of(n - 1 - t, n, rev)
    return pl.pallas_call(
        body, name=name, grid=(n,),
        in_specs=[pl.BlockSpec((Q, 1024), lambda t: (cm(t), 0)), pl.BlockSpec((Q, 256), lambda t: (cm(t), 4)),
                  pl.BlockSpec((Q, 256), lambda t: (cm(t), 5)), pl.BlockSpec((Q, 128), lambda t: (cm(t), 0)),
                  pl.BlockSpec((1, 128), lambda t: (0, 0)), pl.BlockSpec((1, Q, 1024), lambda t: (cm(t), 0, 0)),
                  pl.BlockSpec((Q, 1024), lambda t: (jnp.maximum(cm(t) - 2, 0), 0))],
        out_specs=[pl.BlockSpec((Q, 1536), lambda t: (cm(t), 0)), pl.BlockSpec((Q, 128), lambda t: (cm(t), 0)),
                   pl.BlockSpec((1, 128), lambda t: (0, 0))],
        out_shape=[jax.ShapeDtypeStruct((n * Q, 1536), F32), jax.ShapeDtypeStruct((n * Q, 128), F32),
                   jax.ShapeDtypeStruct((1, 128), F32)],
        scratch_shapes=[pltpu.VMEM((Q, 1024), F32)], compiler_params=_cp("arbitrary"))(xbc, xbc, xbc, dt, alog, hs, dy)


def f_norm0(x, g0, b0, sc, sh):
    x0 = _ln(x, g0, b0)
    return x0, x0 * (1.0 + sc) + sh


def f_dt(raw, bias):
    z = split4(raw)[0] + bias
    dt = jnp.maximum(z, 0.0) + jnp.log1p(jnp.exp(-jnp.abs(z)))
    return dt, dt


def f_gated_norm(yf, yb, xs, z, dcol, g):
    h = (yf + yb + xs * dcol) * _silu(z)
    return (h * lax.rsqrt(jnp.mean(h * h, axis=-1, keepdims=True) + LN_EPS) * g,)


def f_gmlp(uv, gmg, gmb, *wb):
    ws, bs = wb[:8], wb[8:]
    u, v = split2(uv)
    vn = split8(_ln(_gelu(v), gmg, gmb))
    mixed = concat8(tuple(mm(ws[g], vn[g]) + bs[g] for g in range(8)))
    return (_gelu(u) * mixed,)


def f_merge(ps, pg, gates, bg):
    gs, gg = split2(jax.nn.sigmoid(gates + bg))
    return (gs * ps + gg * pg,)


def f_res1(x0, out, g1, lg, lb, sc, sh):
    x1 = _ln(ALPHA * x0 + g1 * out, lg, lb)
    return x1, x1 * (1.0 + sc) + sh


def f_swiglu(a1, a3):
    return (_silu(a1) * a3,)


def f_res2_loss(x1, ff, tgt, g2, lg, lb):
    x2 = _ln(ALPHA * x1 + g2 * ff, lg, lb)
    e = x2 - tgt
    return (0.5 * jnp.sum(jnp.mean(e * e, axis=-1, keepdims=True), axis=0, keepdims=True),)


def _row_tile(M):
    return 544 if M % 544 == 0 else (512 if M % 512 == 0 else M)


def core(x_all, tgt, mod_x, mod_c, W, S):
    R = x_all.shape[0]
    L = R - LC
    n = R // Q
    T = 256
    nt, ntl = R // T, L // T
    tmR, tmL = _row_tile(R), _row_tile(L)
    tkR = 256 if R % 512 else 512
    tkL = 512 if L % 512 == 0 else 256
    row = lambda v: v.reshape(1, -1)
    mx = [row(mod_x[k]) for k in range(6)]
    mc = [row(mod_c[k]) for k in range(6)]
    sel = lambda i: jnp.minimum(i, 1)
    sc1 = jnp.stack([mc[1], mx[1]])
    sh1 = jnp.stack([mc[0], mx[0]])
    ln0 = [(row(S['ln0_g']), None), (row(S['ln0_b']), None), (sc1, sel), (sh1, sel)]

    x0, xm = stage_fwd("norm0_fwd", f_norm0, T, nt, [(x_all, D, 0, 0)], ln0, [('new', R, D, 0), ('new', R, D, 0, BF16)])
    proj = matmul("proj_fwd", xm, W['w_in'], 'nn', tmR, PW // 2, 1024)
    conv_w8 = jnp.pad(S['conv_w'], ((0, 3), (0, 0)))
    conv_b = row(S['conv_b'])
    xbc = conv_fwd("conv_fwd", proj, conv_w8, conv_b, R)
    dt_bias = jnp.pad(S['dt_bias'].reshape(1, 32), ((0, 0), (0, 96)))
    alog = jnp.pad(S['a_log'].reshape(1, 32), ((0, 0), (0, 96)))
    x_dt = [(proj, 512, C_DT // 512, 0)]
    dt_f, dt_b = stage_fwd("dt_fwd", f_dt, T, nt, x_dt, [(dt_bias, None)], [('new', R, 128, 0), ('new', R, 128, 0)])
    y_f, hs_f = ssd_fwd("ssd_fwd_f", xbc, dt_f, alog, n, False, 0)
    y_b, hs_b = ssd_fwd("ssd_fwd_b", xbc, dt_b, alog, n, True, 1)
    dcol = jnp.repeat(S['d_skip'][0] + S['d_skip'][1], 64).reshape(1, D)
    x_gn = [(y_f, D, 0, 1), (y_b, D, 0, 1), (xbc, D, 0, 1), (proj, D, C_Z // D, 1)]
    p_gn = [(dcol, None), (row(S['ssd_norm_g']), None)]
    (yn,) = stage_fwd("gnorm_fwd", f_gated_norm, T, ntl, x_gn, p_gn, [('new', L, D, 0, BF16)])
    x_gm = [(proj, 2 * D, C_UV // (2 * D), LC // Q)]
    p_gm = ([(row(S['gm_norm_g']), None), (row(S['gm_norm_b']), None)]
            + [(S['w_spatial'][g], None) for g in range(8)] + [(S['b_spatial'][g].reshape(Q, 1), None) for g in range(8)])
    (y_gm,) = stage_fwd("gmlp_fwd", f_gmlp, Q, L // Q, x_gm, p_gm, [('new', L, D, 0, BF16)])
    p_ssd = matmul("pssd_fwd", yn, W['w_ssd_proj'], 'nn', tmL, 1024, 1024)
    p_g = matmul("pgm_fwd", y_gm, W['w_gm_proj'], 'nn', tmL, 1024, 1024)
    x_mg = [(p_ssd, D, 0, 0), (p_g, D, 0, 0), (proj, 2 * D, C_GATE // (2 * D), 1)]
    p_mg = [(row(S['b_gate']), None)]
    (merged,) = stage_fwd("merge_fwd", f_merge, T, ntl, x_mg, p_mg, [('new', L, D, 0, BF16)])
    out = matmul("out_fwd", merged, W['w_out'], 'nn', tmL, 1024, 1024)
    x_r1 = [(x0, D, 0, 1), (out, D, 0, 0)]
    p_r1 = [(mx[2], None), (row(S['ln1_g']), None), (row(S['ln1_b']), None), (mx[4], None), (mx[3], None)]
    x1, hm = stage_fwd("res1_fwd", f_res1, T, ntl, x_r1, p_r1, [('new', L, D, 0), ('new', L, D, 0, BF16)])
    a1 = ff_in_fwd("ff1_fwd", hm, W['w_ff1'], tmL).reshape(NS * L, WS)
    a3 = ff_in_fwd("ff3_fwd", hm, W['w_ff3'], tmL).reshape(NS * L, WS)
    x_sw = [(a1, WS, 0, 0), (a3, WS, 0, 0)]
    (act,) = stage_fwd("swiglu_fwd", f_swiglu, T, NS * L // T, x_sw, [], [('new', NS * L, WS, 0, BF16)])
    act = act.reshape(NS, L, WS)
    ff = ff_out_fwd("ff2_fwd", act, W['w_ff2'], tmL)
    x_r2 = [(x1, D, 0, 0), (ff, D, 0, 0), (tgt, D, 0, 0)]
    p_r2 = [(mx[5], None), (row(S['ln2_g']), None), (row(S['ln2_b']), None)]

    dx1_a, dff, dg2, dl2g, dl2b, loss = stage_bwd(
        "res2_bwd", f_res2_loss, T, ntl, x_r2, p_r2, [1.0],
        [('new', L, D, 0), ('new', L, D, 0, BF16), None], [True, True, True], primal=[(0, (1, 1))])
    dact = ff_out_bwd_x("ff2_bwd_x", dff, W['w_ff2'], tmL).reshape(NS * L, WS)
    gw_ff2 = ff_out_bwd_w("ff2_bwd_w", act, dff, tkL)
    da1, da3 = stage_bwd("swiglu_bwd", f_swiglu, T, NS * L // T, x_sw, [], [(dact, WS, 0, 0)],
                         [('new', NS * L, WS, 0, BF16), ('new', NS * L, WS, 0, BF16)], [])
    da1, da3 = da1.reshape(NS, L, WS), da3.reshape(NS, L, WS)
    dhm = ff_in_bwd_x("ff1_bwd_x", da1, W['w_ff1'], tmL)
    dhm = ff_in_bwd_x("ff3_bwd_x", da3, W['w_ff3'], tmL, add=dhm)
    gw_ff1 = ff_in_bwd_w("ff1_bwd_w", hm, da1, tkL)
    gw_ff3 = ff_in_bwd_w("ff3_bwd_w", hm, da3, tkL)
    dx0_a, dout, dg1, dl1g, dl1b, dsc2, dsh2 = stage_bwd(
        "res1_bwd", f_res1, T, ntl, x_r1, p_r1, [(dx1_a, D, 0, 0), (dhm, D, 0, 0)],
        [('new', L, D, 0), ('new', L, D, 0, BF16)], [True] * 5)
    dmerged = matmul("out_bwd_x", dout, W['w_out'], 'nt', tmL, 1024, 1024)
    gw_out = matmul("out_bwd_w", merged, dout, 'tn', 1024, 1024, tkL, BF16)
    lt, lq = -(LC // T), -(LC // Q)
    x_mg_b = [(p_ssd, D, 0, lt), (p_g, D, 0, lt), (proj, 2 * D, C_GATE // (2 * D), 0)]
    dp_ssd, dp_g, dproj, dbg = stage_bwd(
        "merge_bwd", f_merge, T, nt, x_mg_b, p_mg, [(dmerged, D, 0, lt)],
        [('new', L, D, lt, BF16), ('new', L, D, lt, BF16), ('part', R, PW, 2 * D, C_GATE // (2 * D), 0, BF16)], [True])
    dyn = matmul("pssd_bwd_x", dp_ssd, W['w_ssd_proj'], 'nt', tmL, 1024, 1024)
    gw_ssd = matmul("pssd_bwd_w", yn, dp_ssd, 'tn', 1024, 1024, tkL, BF16)
    dy_gm = matmul("pgm_bwd_x", dp_g, W['w_gm_proj'], 'nt', tmL, 1024, 1024)
    gw_gm = matmul("pgm_bwd_w", y_gm, dp_g, 'tn', 1024, 1024, tkL, BF16)
    r_gm = stage_bwd("gmlp_bwd", f_gmlp, Q, n, [(proj, 2 * D, C_UV // (2 * D), 0)], p_gm, [(dy_gm, D, 0, lq)],
                     [('alias', dproj, 2 * D, C_UV // (2 * D), 0)], [True] * 18)
    dproj, dgmg, dgmb, dws, dbs = r_gm[0], r_gm[1], r_gm[2], r_gm[3:11], r_gm[11:19]
    x_gn_b = [(y_f, D, 0, 0), (y_b, D, 0, 0), (xbc, D, 0, 0), (proj, D, C_Z // D, 0)]
    dy, dskipx, dproj, ddcol, dng = stage_bwd(
        "gnorm_bwd", f_gated_norm, T, nt, x_gn_b, p_gn, [(dyn, D, 0, lt)],
        [('new', L, D, lt), None, ('new', L, D, lt), ('alias', dproj, D, C_Z // D, 0)], [True, True])
    dxbc_f, ddt_f, dal_f = ssd_bwd("ssd_bwd_f", xbc, dt_f, alog, hs_f, dy, n, False, 0)
    dxbc_b, ddt_b, dal_b = ssd_bwd("ssd_bwd_b", xbc, dt_b, alog, hs_b, dy, n, True, 1)
    dproj, ddtb = stage_bwd("dt_bwd", f_dt, T, nt, x_dt, [(dt_bias, None)],
                            [(ddt_f, 128, 0, 0), (ddt_b, 128, 0, 0)],
                            [('alias', dproj, 512, C_DT // 512, 0)], [True])
    dproj, dcw8, dcb = conv_bwd("conv_bwd", proj, conv_w8, conv_b, dxbc_f, dxbc_b, dskipx, dproj, R)
    dxm = matmul("proj_bwd_x", dproj, W['w_in'], 'nt', R // 4 if R % 32 == 0 else R, 1024, 1024)
    gw_in = matmul("proj_bwd_w", xm, dproj, 'tn', 1024, PW // 4, tkR, BF16)
    grad_x, dl0g, dl0b, dsc1, dsh1 = stage_bwd(
        "norm0_bwd", f_norm0, T, nt, [(x_all, D, 0, 0)], ln0, [(dx0_a, D, 0, -1), (dxm, D, 0, 0)],
        [('new', L, D, -1)], [True] * 4)

    zero = jnp.zeros((D,), F32)
    flat = lambda v: v.reshape(-1)
    small = {
        'loss': flat(loss), 'ln0_g': flat(dl0g), 'ln0_b': flat(dl0b),
        'dmod_x': jnp.concatenate([flat(dsh1[1]), flat(dsc1[1]), flat(dg1), flat(dsh2), flat(dsc2), flat(dg2)]),
        'dmod_c': jnp.concatenate([flat(dsh1[0]), flat(dsc1[0]), zero, zero, zero, zero]),
        'conv_w': flat(dcw8[:5]), 'conv_b': flat(dcb), 'dt_bias': flat(ddtb[:, :32]),
        'a_log': flat((dal_f + dal_b)[:, :32]),
        'd_skip': flat(jnp.tile(ddcol.reshape(1, NH, 64).sum(-1), (2, 1))),
        'ssd_norm_g': flat(dng), 'gm_norm_g': flat(dgmg), 'gm_norm_b': flat(dgmb),
        'w_spatial': flat(jnp.stack(dws)), 'b_spatial': flat(jnp.stack(dbs)), 'b_gate': flat(dbg),
        'ln1_g': flat(dl1g), 'ln1_b': flat(dl1b), 'ln2_g': flat(dl2g), 'ln2_b': flat(dl2b),
    }
    big = {'w_in': gw_in, 'w_ssd_proj': gw_ssd, 'w_gm_proj': gw_gm, 'w_out': gw_out,
           'w_ff1': gw_ff1, 'w_ff3': gw_ff3, 'w_ff2': gw_ff2}
    return grad_x, big, small


def _place():
    return lax.axis_index("x"), lax.axis_index("y"), lax.axis_index("c")


def allgather8(name, blk, hbm):
    space = pl.ANY if hbm else pltpu.VMEM

    def body(x_ref, out_ref, send_sems, recv_sems, local_sem):
        x, y, c = _place()
        me, sibling = (x, y, c), (x, y, 1 - c)
        chips = [(1 - x, y), (x, 1 - y), (1 - x, 1 - y)]

        def slot(px, py, pc):
            return out_ref.at[4 * px + 2 * py + pc]

        def copy(k, block, to, src=None):
            return pltpu.make_async_remote_copy(
                src_ref=slot(*block) if src is None else src, dst_ref=slot(*block),
                send_sem=send_sems.at[k], recv_sem=recv_sems.at[k], device_id=to, device_id_type=MESH)

        mine = pltpu.make_async_copy(x_ref, slot(*me), local_sem)
        mine.start()
        first = [copy(0, me, sibling, src=x_ref)]
        first += [copy(1 + j, me, (*chip, c), src=x_ref) for j, chip in enumerate(chips)]
        for cp in first:
            cp.start()
        passed = [copy(4 + j, (*chip, c), sibling) for j, chip in enumerate(chips)]
        for j, chip in enumerate(chips):
            copy(1 + j, (*chip, c), me).wait_recv()
            passed[j].start()
        copy(0, sibling, me).wait_recv()
        for j, chip in enumerate(chips):
            copy(4 + j, (*chip, 1 - c), me).wait_recv()
        for cp in first + passed:
            cp.wait_send()
        mine.wait()

    return pl.pallas_call(
        body, name=name, out_shape=jax.ShapeDtypeStruct((8,) + blk.shape, blk.dtype),
        in_specs=[pl.BlockSpec(memory_space=space)], out_specs=pl.BlockSpec(memory_space=space),
        scratch_shapes=[pltpu.SemaphoreType.DMA((7,)), pltpu.SemaphoreType.DMA((7,)), pltpu.SemaphoreType.DMA],
        compiler_params=pltpu.CompilerParams(vmem_limit_bytes=VMEM_LIMIT_V7X))(blk)


def gather_weights(name, blks):
    n = len(blks)

    def body(*refs):
        ins, outs = refs[:n], refs[n:2 * n]
        send_sems, recv_sems, local_sems = refs[2 * n:]
        x, y, c = _place()
        me, sibling = (x, y, c), (x, y, 1 - c)
        chips = [(1 - x, y), (x, 1 - y), (1 - x, 1 - y)]

        def copy(a, k, block, to, own=False):
            dst = outs[a].at[4 * block[0] + 2 * block[1] + block[2]]
            return pltpu.make_async_remote_copy(
                src_ref=ins[a] if own else dst, dst_ref=dst, send_sem=send_sems.at[7 * a + k],
                recv_sem=recv_sems.at[7 * a + k], device_id=to, device_id_type=MESH)

        mine = [pltpu.make_async_copy(ins[a], outs[a].at[4 * x + 2 * y + c], local_sems.at[a]) for a in range(n)]
        for cp in mine:
            cp.start()
        first = []
        for j, chip in enumerate(chips):
            first += [copy(a, 1 + j, me, (*chip, c), own=True) for a in range(n)]
        first += [copy(a, 0, me, sibling, own=True) for a in range(n)]
        for cp in first:
            cp.start()
        passed = []
        for a in range(n):
            for j, chip in enumerate(chips):
                copy(a, 1 + j, (*chip, c), me).wait_recv()
                passed.append(copy(a, 4 + j, (*chip, c), sibling))
                passed[-1].start()
        for a in range(n):
            copy(a, 0, sibling, me).wait_recv()
            for j, chip in enumerate(chips):
                copy(a, 4 + j, (*chip, 1 - c), me).wait_recv()
        for cp in first + passed:
            cp.wait_send()
        for cp in mine:
            cp.wait()

    any_spec = pl.BlockSpec(memory_space=pl.ANY)
    return pl.pallas_call(
        body, name=name, out_shape=[jax.ShapeDtypeStruct((8,) + b.shape, b.dtype) for b in blks],
        in_specs=[any_spec] * n, out_specs=[any_spec] * n,
        scratch_shapes=[pltpu.SemaphoreType.DMA((7 * n,)), pltpu.SemaphoreType.DMA((7 * n,)),
                        pltpu.SemaphoreType.DMA((n,))])(*blks)


def _exchange(name, srcs, out_shapes, ncp, plan):
    n = len(srcs)

    def body(*refs):
        ins, outs = refs[:n], refs[n:n + len(out_shapes)]
        send_sems, recv_sems = refs[n + len(out_shapes):]
        cps = [pltpu.make_async_remote_copy(src_ref=s, dst_ref=d, send_sem=send_sems.at[k], recv_sem=recv_sems.at[k],
                                            device_id=to, device_id_type=MESH)
               for k, (s, d, to) in enumerate(plan(_place(), ins, outs))]
        for cp in cps:
            cp.start()
        for cp in cps:
            cp.wait_recv()
        for cp in cps:
            cp.wait_send()

    any_spec = pl.BlockSpec(memory_space=pl.ANY)
    return pl.pallas_call(
        body, name=name, out_shape=out_shapes, in_specs=[any_spec] * n, out_specs=[any_spec] * len(out_shapes),
        scratch_shapes=[pltpu.SemaphoreType.DMA((ncp,)), pltpu.SemaphoreType.DMA((ncp,))])(*srcs)


def sibling_swap(name, gs):
    def plan(place, ins, outs):
        x, y, c = place
        return [(g.at[s, 1 - c], o.at[s], (x, y, 1 - c)) for g, o in zip(ins, outs) for s in range(4)]

    return _exchange(name, gs, [jax.ShapeDtypeStruct((4,) + g.shape[2:], g.dtype) for g in gs], 4 * len(gs), plan)


def chip_scatter(name, ss):
    def plan(place, ins, outs):
        x, y, c = place
        chips = [(1 - x, y), (x, 1 - y), (1 - x, 1 - y)]
        return [(s.at[2 * px + py], o.at[k], (px, py, c)) for k, (px, py) in enumerate(chips) for s, o in zip(ins, outs)]

    return _exchange(name, ss, [jax.ShapeDtypeStruct((3,) + s.shape[1:], s.dtype) for s in ss], 3 * len(ss), plan)


def sibling_pair(name, hs):
    n = len(hs)

    def body(*refs):
        ins, outs = refs[:n], refs[n:2 * n]
        send_sems, recv_sems = refs[2 * n:]
        x, y, c = _place()
        cps = [pltpu.make_async_remote_copy(src_ref=outs[a].at[c], dst_ref=outs[a].at[c], send_sem=send_sems.at[a],
                                            recv_sem=recv_sems.at[a], device_id=(x, y, 1 - c), device_id_type=MESH)
               for a in range(n)]
        for cp in cps:
            cp.start()
        for a in range(n):
            pltpu.make_async_remote_copy(src_ref=outs[a].at[1 - c], dst_ref=outs[a].at[1 - c], send_sem=send_sems.at[a],
                                         recv_sem=recv_sems.at[a], device_id=(x, y, 1 - c),
                                         device_id_type=MESH).wait_recv()
        for cp in cps:
            cp.wait_send()

    any_spec = pl.BlockSpec(memory_space=pl.ANY)
    return pl.pallas_call(
        body, name=name, out_shape=[jax.ShapeDtypeStruct(h.shape, h.dtype) for h in hs],
        in_specs=[any_spec] * n, out_specs=[any_spec] * n, input_output_aliases={a: a for a in range(n)},
        scratch_shapes=[pltpu.SemaphoreType.DMA((n,)), pltpu.SemaphoreType.DMA((n,))])(*hs)


def chip_sum(name, g, theirs):
    r, w = g.shape[2:]

    def body(_, a_ref, b_ref, o_ref):
        o_ref[...] = (a_ref[...].astype(F32) + b_ref[...].astype(F32)).astype(BF16)

    grid_spec = pltpu.PrefetchScalarGridSpec(
        num_scalar_prefetch=1, grid=(4,),
        in_specs=[pl.BlockSpec((None, None, r, w), lambda s, at: (s, at[0], 0, 0)),
                  pl.BlockSpec((None, r, w), lambda s, at: (s, 0, 0))],
        out_specs=pl.BlockSpec((None, r, w), lambda s, at: (s, 0, 0)))
    at = jnp.stack([lax.axis_index("c")]).astype(jnp.int32)
    return pl.pallas_call(body, name=name, grid_spec=grid_spec, out_shape=jax.ShapeDtypeStruct((4, r, w), BF16),
                          compiler_params=_cp("arbitrary"))(at, g, theirs)


def shard_sum(name, s4, others):
    r, w = s4.shape[1:]

    def body(_, a_ref, b0, b1, b2, o_ref):
        o_ref[...] = ((a_ref[...].astype(F32) + b0[...].astype(F32)) + b1[...].astype(F32)) + b2[...].astype(F32)

    oth = [pl.BlockSpec((None, r, w), lambda i, at, k=k: (k, 0, 0)) for k in range(3)]
    grid_spec = pltpu.PrefetchScalarGridSpec(
        num_scalar_prefetch=1, grid=(1,),
        in_specs=[pl.BlockSpec((None, r, w), lambda i, at: (at[0], 0, 0))] + oth,
        out_specs=pl.BlockSpec((None, r, w), lambda i, at: (at[1], 0, 0)))
    at = jnp.stack([2 * lax.axis_index("x") + lax.axis_index("y"), lax.axis_index("c")]).astype(jnp.int32)
    return pl.pallas_call(body, name=name, grid_spec=grid_spec, out_shape=jax.ShapeDtypeStruct((2, r, w), F32),
                          compiler_params=_cp("arbitrary"))(at, s4, others, others, others)


W_IN_RUNS = ((0, 2, 1296, 376), (376, 3, 0, 1672), (2048, 1, 920, 752), (2800, 2, 0, 1296), (4096, 0, 0, 1024),
             (5120, 0, 1024, 648), (5768, 1, 0, 920))


def w_in_to_padded(name, g4):
    T = 128

    def body(g_ref, o_ref):
        o_ref[:, D_PROJ:PW] = jnp.zeros((T, PW - D_PROJ), o_ref.dtype)
        for (a, s, j0, w) in W_IN_RUNS:
            o_ref[:, a:a + w] = g_ref[s, :, j0:j0 + w]

    return pl.pallas_call(body, name=name, grid=(D // T,), in_specs=[pl.BlockSpec((4, T, 1672), lambda i: (0, i, 0))],
                          out_specs=pl.BlockSpec((T, PW), lambda i: (i, 0)),
                          out_shape=jax.ShapeDtypeStruct((D, PW), g4.dtype), compiler_params=_cp("arbitrary"))(g4)


def w_in_from_padded(name, gp):
    T = 128

    def body(g_ref, o_ref):
        for (a, s, j0, w) in W_IN_RUNS:
            o_ref[s, :, j0:j0 + w] = g_ref[:, a:a + w]

    return pl.pallas_call(body, name=name, grid=(D // T,), in_specs=[pl.BlockSpec((T, PW), lambda i: (i, 0))],
                          out_specs=pl.BlockSpec((4, T, 1672), lambda i: (0, i, 0)),
                          out_shape=jax.ShapeDtypeStruct((4, D, 1672), gp.dtype), compiler_params=_cp("arbitrary"))(gp)


def sum_devices(name, g):
    def body(g_ref, o_ref):
        acc = g_ref[0]
        for k in range(1, 8):
            acc = acc + g_ref[k]
        o_ref[...] = acc

    return pl.pallas_call(body, name=name, out_shape=jax.ShapeDtypeStruct(g.shape[1:], F32),
                          compiler_params=pltpu.CompilerParams(vmem_limit_bytes=VMEM_LIMIT_V7X))(g)


def adamw(name, w, g, m, v, T):
    r, wd = w.shape
    c1 = 1.0 - ADAM_B1 ** ADAM_STEP
    c2 = 1.0 - ADAM_B2 ** ADAM_STEP

    def body(w_ref, g_ref, m_ref, v_ref, d_ref, mo_ref, vo_ref):
        gv = g_ref[...]
        mn = ADAM_B1 * m_ref[...] + (1.0 - ADAM_B1) * gv
        vn = ADAM_B2 * v_ref[...] + (1.0 - ADAM_B2) * (gv * gv)
        d_ref[...] = -ADAM_LR * ((mn / c1) / (jnp.sqrt(vn / c2) + ADAM_EPS) + ADAM_WD * w_ref[...])
        mo_ref[...] = mn
        vo_ref[...] = vn

    spec = pl.BlockSpec((T, wd), lambda i: (i, 0))
    return pl.pallas_call(body, name=name, grid=(r // T,), in_specs=[spec] * 4, out_specs=[spec] * 3,
                          out_shape=[jax.ShapeDtypeStruct((r, wd), F32)] * 3, compiler_params=_cp("arbitrary"))(w, g, m, v)


BIG = {'w_in': (1024, 1672), 'w_ssd_proj': (256, 1024), 'w_gm_proj': (256, 1024), 'w_out': (256, 1024),
       'w_ff1': (1024, 704), 'w_ff3': (1024, 704), 'w_ff2': (704, 1024)}


class Flat:
    def __init__(self, segs):
        self.off, o = {}, 0
        for name, size in segs:
            self.off[name] = (o, size)
            o += -(-size // 128) * 128
        self.rows = -(-o // 1024) * 8

    def pack(self, vals):
        parts = []
        for name, (o, size) in self.off.items():
            v = vals[name].reshape(-1).astype(F32)
            parts.append(jnp.pad(v, (0, -(-size // 128) * 128 - size)))
        buf = jnp.concatenate(parts)
        return jnp.pad(buf, (0, self.rows * 128 - buf.shape[0])).reshape(self.rows, 128)

    def get(self, buf, name, shape=None):
        o, size = self.off[name]
        v = buf.reshape(-1)[o:o + size]
        return v if shape is None else v.reshape(shape)


PARTIALS = Flat([('loss', 1), ('ln0_g', D), ('ln0_b', D), ('dmod_x', 6 * D), ('dmod_c', 6 * D), ('conv_w', 5 * 1536),
                 ('conv_b', 1536), ('dt_bias', 32), ('a_log', 32), ('d_skip', 32), ('ssd_norm_g', D),
                 ('gm_norm_g', D), ('gm_norm_b', D), ('w_spatial', 8 * Q * Q), ('b_spatial', 8 * Q), ('b_gate', 2 * D),
                 ('ln1_g', D), ('ln1_b', D), ('ln2_g', D), ('ln2_b', D)])

WEIGHTS = ('c_ctx', 'ln0_g', 'ln0_b', 'w_ada', 'b_ada', 'w_in', 'conv_w', 'conv_b', 'dt_bias', 'a_log', 'd_skip',
           'ssd_norm_g', 'gm_norm_g', 'gm_norm_b', 'w_spatial', 'b_spatial', 'b_gate', 'w_ssd_proj', 'w_gm_proj',
           'w_out', 'ln1_g', 'ln1_b', 'w_ff1', 'w_ff3', 'w_ff2', 'ln2_g', 'ln2_b')
BIG_NAMES = tuple(BIG)
SMALL_NAMES = tuple(n for n in WEIGHTS if n not in BIG_NAMES and n != 'w_ada')


def kernel(x, c, ctx, c_ctx, ln0_g, ln0_b, w_ada, b_ada, w_in, conv_w, conv_b, dt_bias, a_log, d_skip, ssd_norm_g, gm_norm_g, gm_norm_b, w_spatial, b_spatial, b_gate, w_ssd_proj, w_gm_proj, w_out, ln1_g, ln1_b, w_ff1, w_ff3, w_ff2, ln2_g, ln2_b, loss_target, m_c_ctx, m_ln0_g, m_ln0_b, m_w_ada, m_b_ada, m_w_in, m_conv_w, m_conv_b, m_dt_bias, m_a_log, m_d_skip, m_ssd_norm_g, m_gm_norm_g, m_gm_norm_b, m_w_spatial, m_b_spatial, m_b_gate, m_w_ssd_proj, m_w_gm_proj, m_w_out, m_ln1_g, m_ln1_b, m_w_ff1, m_w_ff3, m_w_ff2, m_ln2_g, m_ln2_b, v_c_ctx, v_ln0_g, v_ln0_b, v_w_ada, v_b_ada, v_w_in, v_conv_w, v_conv_b, v_dt_bias, v_a_log, v_d_skip, v_ssd_norm_g, v_gm_norm_g, v_gm_norm_b, v_w_spatial, v_b_spatial, v_b_gate, v_w_ssd_proj, v_w_gm_proj, v_w_out, v_ln1_g, v_ln1_b, v_w_ff1, v_w_ff3, v_w_ff2, v_ln2_g, v_ln2_b):
    wts = dict(c_ctx=c_ctx, ln0_g=ln0_g, ln0_b=ln0_b, w_ada=w_ada, b_ada=b_ada, w_in=w_in, conv_w=conv_w, conv_b=conv_b,
               dt_bias=dt_bias, a_log=a_log, d_skip=d_skip, ssd_norm_g=ssd_norm_g, gm_norm_g=gm_norm_g,
               gm_norm_b=gm_norm_b, w_spatial=w_spatial, b_spatial=b_spatial, b_gate=b_gate, w_ssd_proj=w_ssd_proj,
               w_gm_proj=w_gm_proj, w_out=w_out, ln1_g=ln1_g, ln1_b=ln1_b, w_ff1=w_ff1, w_ff3=w_ff3, w_ff2=w_ff2,
               ln2_g=ln2_g, ln2_b=ln2_b)
    ms = dict(zip(WEIGHTS, (m_c_ctx, m_ln0_g, m_ln0_b, m_w_ada, m_b_ada, m_w_in, m_conv_w, m_conv_b, m_dt_bias, m_a_log,
                            m_d_skip, m_ssd_norm_g, m_gm_norm_g, m_gm_norm_b, m_w_spatial, m_b_spatial, m_b_gate,
                            m_w_ssd_proj, m_w_gm_proj, m_w_out, m_ln1_g, m_ln1_b, m_w_ff1, m_w_ff3, m_w_ff2, m_ln2_g,
                            m_ln2_b)))
    vs = dict(zip(WEIGHTS, (v_c_ctx, v_ln0_g, v_ln0_b, v_w_ada, v_b_ada, v_w_in, v_conv_w, v_conv_b, v_dt_bias, v_a_log,
                            v_d_skip, v_ssd_norm_g, v_gm_norm_g, v_gm_norm_b, v_w_spatial, v_b_spatial, v_b_gate,
                            v_w_ssd_proj, v_w_gm_proj, v_w_out, v_ln1_g, v_ln1_b, v_w_ff1, v_w_ff3, v_w_ff2, v_ln2_g,
                            v_ln2_b)))
    px, py, pc = _place()
    shard = 2 * px + py
    dev = 2 * shard + pc
    take = lambda a, i, axis=0: lax.dynamic_index_in_dim(a, i, axis, keepdims=False)

    pre = jnp.concatenate([c, jnp.pad(conv_w[0], ((0, 0), (0, D - 384))), jnp.zeros((2, D), F32)], axis=0)
    pre = allgather8("gather_cond", pre, False)
    conv_w_full = pre[0::2, 1:6, :384].transpose(1, 0, 2).reshape(5, 1536)
    a16 = jnp.concatenate([_silu(pre[:, 0, :]), _silu(c_ctx)[None], jnp.zeros((7, D), F32)], axis=0)
    mod = matmul("ada_fwd", a16, w_ada[0], 'nn', 16, 512, 1024)
    mod = mod + lax.dynamic_slice_in_dim(b_ada[0], shard * 1536, 1536)[None]
    mod = allgather8("gather_mod", mod, False)
    mod = jnp.concatenate([mod[0], mod[2], mod[4], mod[6]], axis=1)
    mod_x = take(mod, dev).reshape(6, D)
    mod_c = mod[8].reshape(6, D)

    halves = [take(wts[n][0].reshape(2, BIG[n][0] // 2, BIG[n][1]), pc).astype(BF16) for n in BIG_NAMES]
    W = {}
    for n, blocks in zip(BIG_NAMES, gather_weights("gather_weights", halves)):
        r, w = BIG[n]
        W[n] = blocks.reshape(4, r, w) if w != D else blocks.reshape(4 * r, w)
    W['w_in'] = w_in_to_padded("w_in_layout", W['w_in'])

    S = dict(ln0_g=ln0_g, ln0_b=ln0_b, conv_w=conv_w_full, conv_b=conv_b[0], dt_bias=dt_bias[0], a_log=a_log[0],
             d_skip=d_skip[0], ssd_norm_g=ssd_norm_g[0], gm_norm_g=gm_norm_g[0], gm_norm_b=gm_norm_b[0],
             w_spatial=w_spatial[0], b_spatial=b_spatial[0], b_gate=b_gate[0], ln1_g=ln1_g[0], ln1_b=ln1_b[0],
             ln2_g=ln2_g[0], ln2_b=ln2_b[0])
    x_all = jnp.concatenate([ctx[0], x[0]], axis=0)
    grad_x, gbig, gsmall = core(x_all, loss_target[0], mod_x, mod_c, W, S)

    gbig['w_in'] = w_in_from_padded("w_in_grad_layout", gbig['w_in'])
    blocks = [gbig[n].reshape(4, 2, BIG[n][0] // 2, BIG[n][1]) for n in BIG_NAMES]
    theirs = sibling_swap("grads_to_sibling", blocks)
    sums = [chip_sum("grads_chip_sum_" + n, b, t) for n, b, t in zip(BIG_NAMES, blocks, theirs)]
    others = chip_scatter("grads_to_chips", sums)
    halves = [shard_sum("grads_sum_" + n, s, o) for n, s, o in zip(BIG_NAMES, sums, others)]
    g_shards = {n: h.reshape(BIG[n]) for n, h in zip(BIG_NAMES, sibling_pair("grads_halves", halves))}

    parts = allgather8("gather_partials", PARTIALS.pack(gsmall), False)
    tot = sum_devices("partials_sum", parts)
    g = {n: PARTIALS.get(tot, n) for n in ('ln0_g', 'ln0_b', 'conv_b', 'dt_bias', 'a_log', 'd_skip', 'ssd_norm_g',
                                           'gm_norm_g', 'gm_norm_b', 'w_spatial', 'b_spatial', 'b_gate', 'ln1_g',
                                           'ln1_b', 'ln2_g', 'ln2_b')}
    loss = PARTIALS.get(tot, 'loss', ())
    dmod_c = PARTIALS.get(tot, 'dmod_c')
    g['b_ada'] = PARTIALS.get(tot, 'dmod_x') + dmod_c
    g['conv_w'] = lax.dynamic_slice_in_dim(PARTIALS.get(tot, 'conv_w', (5, 1536)), shard * 384, 384, axis=1)
    o, size = PARTIALS.off['dmod_x']
    dmod_rows = parts.reshape(8, -1)[:, o:o + size]
    dm = jnp.concatenate([dmod_rows, dmod_c[None], jnp.zeros((7, 6 * D), F32)], axis=0)
    dm = lax.dynamic_slice_in_dim(dm, shard * 1536, 1536, axis=1)
    g['w_ada'] = matmul("ada_bwd_w", a16, dm, 'tn', 1024, 512, 16)
    dm_c = jnp.concatenate([dm[8:9], jnp.zeros((15, 1536), F32)], axis=0)
    dc = matmul("ada_bwd_c", dm_c, w_ada[0], 'nt', 16, 1024, 512)
    dc = allgather8("gather_dcctx", dc, False)[:, 0, :]
    dc = ((dc[0] + dc[2]) + dc[4]) + dc[6]
    sg = jax.nn.sigmoid(c_ctx)
    g['c_ctx'] = dc * (sg * (1.0 + c_ctx * (1.0 - sg)))
    for n in BIG_NAMES:
        g[n] = g_shards[n]

    delta, new_m, new_v = {}, {}, {}
    for n in BIG_NAMES + ('w_ada',):
        w2 = wts[n][0]
        T = 352 if n == 'w_ff2' else 256
        d_, m_, v_ = adamw("adamw_" + n, w2, g[n], ms[n][0], vs[n][0], T)
        delta[n], new_m[n], new_v[n] = d_, m_, v_
    lay = Flat([(n, wts[n].size) for n in SMALL_NAMES])
    d_, m_, v_ = adamw("adamw_small", lay.pack(wts), lay.pack(g), lay.pack(ms), lay.pack(vs), lay.rows)
    for n in SMALL_NAMES:
        delta[n], new_m[n], new_v[n] = (lay.get(b, n) for b in (d_, m_, v_))

    shp = lambda d: [d[n].reshape(wts[n].shape) for n in WEIGHTS]
    return (loss, grad_x[None], *shp(g), *shp(delta), *shp(new_m), *shp(new_v))
```

```python
import functools

import jax
import jax.numpy as jnp
from jax import lax
from jax.experimental import pallas as pl
from jax.experimental.pallas import tpu as pltpu

F32 = jnp.float32
BF16 = jnp.bfloat16
MESH = pl.DeviceIdType.MESH

VMEM_LIMIT_V7X = 56 * 1024 * 1024

D = 1024
LC = 256
Q = 128
NH = 16
D_FF = 2816
LN_EPS = 1e-5
ALPHA = 2.0 ** 0.25

PW = 7168
C_GATE, C_UV, C_Z, C_XBC, C_DT = 0, 2048, 4096, 5120, 6656
D_PROJ = 6688

ADAM_LR, ADAM_B1, ADAM_B2, ADAM_EPS, ADAM_WD, ADAM_STEP = 0.001, 0.9, 0.999, 1e-08, 0.01, 10


def _cp(*sem):
    return pltpu.CompilerParams(dimension_semantics=sem, vmem_limit_bytes=VMEM_LIMIT_V7X)


def _dot(a, b, ca, cb):
    return lax.dot_general(a.astype(BF16), b.astype(BF16), (((ca,), (cb,)), ((), ())),
                           preferred_element_type=F32)


@jax.custom_vjp
def mm(a, b):
    return _dot(a, b, 1, 0)


mm.defvjp(lambda a, b: (_dot(a, b, 1, 0), (a, b)),
          lambda r, g: (_dot(g, r[1], 1, 1), _dot(r[0], g, 0, 0)))


@jax.custom_vjp
def mm_nt(a, b):
    return _dot(a, b, 1, 1)


mm_nt.defvjp(lambda a, b: (_dot(a, b, 1, 1), (a, b)),
             lambda r, g: (_dot(g, r[1], 1, 0), _dot(g, r[0], 0, 0)))


@jax.custom_vjp
def mm_tn(a, b):
    return _dot(a, b, 0, 0)


mm_tn.defvjp(lambda a, b: (_dot(a, b, 0, 0), (a, b)),
             lambda r, g: (_dot(r[1], g, 1, 1), _dot(r[0], g, 1, 0)))


def _dot32(a, b):
    return lax.dot_general(a, b, (((1,), (0,)), ((), ())), precision=lax.Precision.HIGHEST,
                           preferred_element_type=F32)


def _cumsum_fn(rev):
    def tri(transpose):
        r = lax.broadcasted_iota(jnp.int32, (Q, Q), 0)
        c = lax.broadcasted_iota(jnp.int32, (Q, Q), 1)
        keep = (r >= c) if (rev == transpose) else (r <= c)
        return jnp.where(keep, 1.0, 0.0).astype(F32)

    @jax.custom_vjp
    def cums(a):
        return _dot32(tri(False), a)

    cums.defvjp(lambda a: (_dot32(tri(False), a), None), lambda _, g: (_dot32(tri(True), g),))
    return cums


def _cols(v, k):
    w = v.shape[1] // k
    return tuple(v[:, w * i:w * (i + 1)] for i in range(k))


def _splitter(k):
    @jax.custom_vjp
    def split(v):
        return _cols(v, k)

    @jax.custom_vjp
    def concat(ps):
        return jnp.concatenate(ps, axis=1)

    split.defvjp(lambda v: (_cols(v, k), None), lambda _, g: (jnp.concatenate(g, axis=1),))
    concat.defvjp(lambda ps: (jnp.concatenate(ps, axis=1), None), lambda _, g: (_cols(g, k),))
    return split, concat


split2, _ = _splitter(2)
split4, _ = _splitter(4)
split8, concat8 = _splitter(8)


def _ln(x, g, b):
    mu = jnp.mean(x, axis=-1, keepdims=True)
    xc = x - mu
    var = jnp.mean(xc * xc, axis=-1, keepdims=True)
    return xc * lax.rsqrt(var + LN_EPS) * g + b


def _silu(x):
    return x * jax.nn.sigmoid(x)


def _gelu(x):
    return 0.5 * x * (1.0 + jnp.tanh(0.7978845608028654 * (x + 0.044715 * (x * x * x))))


def _xspec(T, w, col, roff):
    return pl.BlockSpec((T, w), lambda i, col=col, roff=roff: (jnp.maximum(i + roff, 0), col))


def _pspec(p, sel):
    if sel is None:
        return pl.BlockSpec(p.shape, lambda i, n=p.ndim: (0,) * n)
    return pl.BlockSpec((1,) + p.shape[1:], lambda i, n=p.ndim: (sel(i),) + (0,) * (n - 1))


def _out_plumbing(outs, T, args, in_specs):
    shapes, specs, aliases = [], [], {}
    for k, o in enumerate(outs):
        if o[0] == 'new':
            _, rows, w, roff = o[:4]
            shapes.append(jax.ShapeDtypeStruct((rows, w), o[4] if len(o) > 4 else F32))
            specs.append(_xspec(T, w, 0, roff))
        elif o[0] == 'acc':
            shapes.append(jax.ShapeDtypeStruct(o[1], F32))
            specs.append(pl.BlockSpec(o[1], lambda i, n=len(o[1]): (0,) * n))
        elif o[0] == 'part':
            _, rows, wtot, w, col, roff, dtype = o
            shapes.append(jax.ShapeDtypeStruct((rows, wtot), dtype))
            specs.append(_xspec(T, w, col, roff))
        else:
            _, arr, w, col, roff = o
            aliases[len(args)] = k
            args.append(arr)
            in_specs.append(pl.BlockSpec(memory_space=pl.ANY))
            shapes.append(jax.ShapeDtypeStruct(arr.shape, arr.dtype))
            specs.append(_xspec(T, w, col, roff))
    return shapes, specs, aliases


def stage_fwd(name, f, T, n, xs, ps, outs):
    nx, npar = len(xs), len(ps)
    args = [x[0] for x in xs] + [p[0] for p in ps]
    in_specs = [_xspec(T, w, col, roff) for (_, w, col, roff) in xs] + [_pspec(p, sel) for (p, sel) in ps]
    n_in = len(args)
    shapes, specs, aliases = _out_plumbing(outs, T, args, in_specs)
    n_all_in = len(args)

    def body(*refs):
        i = pl.program_id(0)
        xv = [r[...] for r in refs[:nx]]
        pv = [r[...] if ps[k][1] is None else r[0] for k, r in enumerate(refs[nx:n_in])]
        res = f(*xv, *pv)
        for k, o_ref in enumerate(refs[n_all_in:]):
            if outs[k][0] == 'acc':
                @pl.when(i == 0)
                def _(o_ref=o_ref, v=res[k]):
                    o_ref[...] = v

                @pl.when(i > 0)
                def _(o_ref=o_ref, v=res[k]):
                    o_ref[...] += v
            else:
                o_ref[...] = res[k].astype(o_ref.dtype)

    return pl.pallas_call(body, name=name, grid=(n,), in_specs=in_specs, out_specs=specs, out_shape=shapes,
                          input_output_aliases=aliases, compiler_params=_cp("arbitrary"))(*args)


def stage_bwd(name, f, T, n, xs, ps, cts, dxs, dps, primal=()):
    nx, npar = len(xs), len(ps)
    args = [x[0] for x in xs] + [p[0] for p in ps]
    in_specs = [_xspec(T, w, col, roff) for (_, w, col, roff) in xs] + [_pspec(p, sel) for (p, sel) in ps]
    ct_arrs = [c for c in cts if isinstance(c, tuple)]
    for (a, w, col, roff) in ct_arrs:
        args.append(a)
        in_specs.append(_xspec(T, w, col, roff))
    n_in = len(args)
    outs, out_of = [], []
    for k, o in enumerate(dxs):
        if o is not None:
            outs.append(o)
            out_of.append(('x', k))
    for k, want in enumerate(dps):
        if want:
            p, sel = ps[k]
            outs.append(('acc', p.shape))
            out_of.append(('p', k))
    for k, shape in primal:
        outs.append(('acc', shape))
        out_of.append(('r', k))
    shapes, specs, aliases = _out_plumbing(outs, T, args, in_specs)
    for j, (kind, k) in enumerate(out_of):
        if kind == 'p' and ps[k][1] is not None:
            p, sel = ps[k]
            specs[j] = pl.BlockSpec((1,) + p.shape[1:], lambda i, n=p.ndim, sel=sel: (sel(i),) + (0,) * (n - 1))
    n_all_in = len(args)

    def body(*refs):
        i = pl.program_id(0)
        xv = [r[...] for r in refs[:nx]]
        pv = [r[...] if ps[k][1] is None else r[0] for k, r in enumerate(refs[nx:nx + npar])]
        res, vjp_fn = jax.vjp(f, *xv, *pv)
        ctv, q = [], nx + npar
        for k, c in enumerate(cts):
            if c is None:
                ctv.append(jnp.zeros_like(res[k]))
            elif isinstance(c, tuple):
                v = refs[q][...]
                if c[3] < 0:
                    v = v * (i + c[3] >= 0).astype(F32)
                ctv.append(v)
                q += 1
            else:
                ctv.append(jnp.full_like(res[k], c))
        grads = vjp_fn(tuple(ctv))
        for j, o_ref in enumerate(refs[n_all_in:]):
            kind, k = out_of[j]
            if kind == 'x':
                o_ref[...] = grads[k].astype(o_ref.dtype)
            else:
                g = res[k] if kind == 'r' else grads[nx + k]
                sel = None if kind == 'r' else ps[k][1]
                if sel is None:
                    first = i == 0
                    tgt = o_ref
                else:
                    first = jnp.logical_or(i == 0, sel(i) != sel(jnp.maximum(i - 1, 0)))
                    tgt = o_ref.at[0]

                @pl.when(first)
                def _(tgt=tgt, g=g):
                    tgt[...] = g

                @pl.when(jnp.logical_not(first))
                def _(tgt=tgt, g=g):
                    tgt[...] += g

    return pl.pallas_call(body, name=name, grid=(n,), in_specs=in_specs, out_specs=specs, out_shape=shapes,
                          input_output_aliases=aliases, compiler_params=_cp("arbitrary"))(*args)


_CONTRACT = {'nn': (1, 0), 'nt': (1, 1), 'tn': (0, 0)}


def matmul(name, a, b, mode, tm, tn, tk, out_dtype=F32, add=None):
    if mode == 'nn':
        (M, K), (_, N) = a.shape, b.shape
    elif mode == 'nt':
        (M, K), (N, _) = a.shape, b.shape
    else:
        (K, M), (_, N) = a.shape, b.shape
    assert M % tm == 0 and N % tn == 0 and K % tk == 0, (name, M, N, K, tm, tn, tk)
    a_spec = (pl.BlockSpec((tk, tm), lambda j, i, k: (k, i)) if mode == 'tn'
              else pl.BlockSpec((tm, tk), lambda j, i, k: (i, k)))
    b_spec = (pl.BlockSpec((tn, tk), lambda j, i, k: (j, k)) if mode == 'nt'
              else pl.BlockSpec((tk, tn), lambda j, i, k: (k, j)))
    o_spec = pl.BlockSpec((tm, tn), lambda j, i, k: (i, j))
    return matmul_call(name, (N // tn, M // tm, K // tk), a, a_spec, b, b_spec, (M, N), o_spec, (tm, tn), mode,
                       out_dtype, add)


def matmul_call(name, grid, a, a_spec, b, b_spec, out_shape, o_spec, tile, mode, out_dtype=F32, add=None):
    tm, tn = tile
    nk = grid[2]
    ca, cb = _CONTRACT[mode]
    args, in_specs = [a, b], [a_spec, b_spec]
    if add is not None:
        args.append(add)
        in_specs.append(o_spec)

    def body(*refs):
        a_ref, b_ref = refs[0], refs[1]
        o_ref, acc = refs[-2], refs[-1]
        k = pl.program_id(2)
        if nk == 1:
            p = _dot(a_ref[...], b_ref[...], ca, cb)
            o_ref[...] = (p + refs[2][...] if add is not None else p).astype(out_dtype)
            return

        @pl.when(k == 0)
        def _():
            acc[...] = refs[2][...] if add is not None else jnp.zeros_like(acc)

        acc[...] += _dot(a_ref[...], b_ref[...], ca, cb)

        @pl.when(k == nk - 1)
        def _():
            o_ref[...] = acc[...].astype(out_dtype)

    return pl.pallas_call(body, name=name, grid=grid, in_specs=in_specs, out_specs=o_spec,
                          out_shape=jax.ShapeDtypeStruct(out_shape, out_dtype),
                          scratch_shapes=[pltpu.VMEM((tm, tn) if nk > 1 else (8, 128), F32)],
                          compiler_params=_cp("arbitrary", "arbitrary", "arbitrary"))(*args)


NS, WS = 4, 704


def _resident(name, M, tm, rows, weight, out_shape, out_block, out_map, step, add=None):
    args = [rows[0], weight] + ([] if add is None else [add])
    in_specs = [pl.BlockSpec(rows[1], rows[2]), pl.BlockSpec(weight.shape, lambda i, n=weight.ndim: (0,) * n)]
    if add is not None:
        in_specs.append(pl.BlockSpec(out_block, out_map))
    return pl.pallas_call(step, name=name, grid=(M // tm,), in_specs=in_specs, out_specs=pl.BlockSpec(out_block, out_map),
                          out_shape=jax.ShapeDtypeStruct(out_shape, F32), compiler_params=_cp("arbitrary"))(*args)


def ffn_in_fwd(name, h, w1, w3, tm):
    M = h.shape[0]

    def step(h_ref, w1_ref, w3_ref, a1_ref, a3_ref, act_ref):
        for s in range(NS):
            a1 = _dot(h_ref[...], w1_ref[s], 1, 0)
            a3 = _dot(h_ref[...], w3_ref[s], 1, 0)
            a1_ref[s] = a1
            a3_ref[s] = a3
            act_ref[s] = (_silu(a1) * a3).astype(act_ref.dtype)

    wspec = pl.BlockSpec((NS, D, WS), lambda i: (0, 0, 0))
    ospec = pl.BlockSpec((NS, tm, WS), lambda i: (0, i, 0))
    return pl.pallas_call(
        step, name=name, grid=(M // tm,), in_specs=[pl.BlockSpec((tm, D), lambda i: (i, 0)), wspec, wspec],
        out_specs=[ospec, ospec, ospec],
        out_shape=[jax.ShapeDtypeStruct((NS, M, WS), F32), jax.ShapeDtypeStruct((NS, M, WS), F32),
                   jax.ShapeDtypeStruct((NS, M, WS), BF16)], compiler_params=_cp("arbitrary"))(h, w1, w3)


def ffn_out_bwd_x(name, dff, w2, a1, a3, tm):
    M = dff.shape[0]

    def step(d_ref, w_ref, a1_ref, a3_ref, da1_ref, da3_ref):
        for s in range(NS):
            dact = _dot(d_ref[...], w_ref[s * WS:(s + 1) * WS, :], 1, 1)
            a1 = a1_ref[s]
            sig = jax.nn.sigmoid(a1)
            da3_ref[s] = (dact * (a1 * sig)).astype(da3_ref.dtype)
            da1_ref[s] = (dact * a3_ref[s] * (sig * (1.0 + a1 * (1.0 - sig)))).astype(da1_ref.dtype)

    aspec = pl.BlockSpec((NS, tm, WS), lambda i: (0, i, 0))
    return pl.pallas_call(
        step, name=name, grid=(M // tm,),
        in_specs=[pl.BlockSpec((tm, D), lambda i: (i, 0)), pl.BlockSpec(w2.shape, lambda i: (0, 0)), aspec, aspec],
        out_specs=[aspec, aspec],
        out_shape=[jax.ShapeDtypeStruct((NS, M, WS), BF16)] * 2, compiler_params=_cp("arbitrary"))(dff, w2, a1, a3)


def ff_in_bwd_x(name, da3, w3, tm, add=None):
    M = da3.shape[1]

    def step(*refs):
        d_ref, w_ref, o_ref = refs[0], refs[1], refs[-1]
        acc = _dot(d_ref[0], w_ref[0], 1, 1)
        for s in range(1, NS):
            acc = acc + _dot(d_ref[s], w_ref[s], 1, 1)
        o_ref[...] = acc if add is None else acc + refs[2][...]

    return _resident(name, M, tm, (da3, (NS, tm, WS), lambda i: (0, i, 0)), w3, (M, D), (tm, D), lambda i: (i, 0), step, add)


def ff_in_bwd_w(name, h, da3, tk):
    M = h.shape[0]
    return matmul_call(name, (NS, 1, M // tk), h, pl.BlockSpec((tk, D), lambda j, i, k: (k, 0)),
                       da3, pl.BlockSpec((None, tk, WS), lambda j, i, k: (j, k, 0)),
                       (NS, D, WS), pl.BlockSpec((None, D, WS), lambda j, i, k: (j, 0, 0)), (D, WS), 'tn', BF16)


def ff_out_fwd(name, act3, w2, tm):
    M = act3.shape[1]

    def step(a_ref, w_ref, o_ref):
        acc = _dot(a_ref[0], w_ref[0:WS, :], 1, 0)
        for s in range(1, NS):
            acc = acc + _dot(a_ref[s], w_ref[s * WS:(s + 1) * WS, :], 1, 0)
        o_ref[...] = acc

    return _resident(name, M, tm, (act3, (NS, tm, WS), lambda i: (0, i, 0)), w2, (M, D), (tm, D), lambda i: (i, 0), step)


def ff_out_bwd_w(name, act3, dff, tk):
    M = dff.shape[0]
    return matmul_call(name, (1, NS, M // tk), act3, pl.BlockSpec((None, tk, WS), lambda j, i, k: (i, k, 0)),
                       dff, pl.BlockSpec((tk, D), lambda j, i, k: (k, 0)),
                       (NS * WS, D), pl.BlockSpec((WS, D), lambda j, i, k: (i, 0)), (WS, D), 'tn', BF16)


def _shift_rows(x, d):
    n = x.shape[0]
    if d == 0:
        return x
    y = pltpu.roll(x, (-d) % n, 0)
    t = lax.broadcasted_iota(jnp.int32, x.shape, 0)
    ok = (t + d >= 0) & (t + d < n)
    return jnp.where(ok, y, 0.0)


def _conv_pre(x, w_ref, b_ref):
    acc = jnp.broadcast_to(b_ref[...], x.shape)
    for k in range(5):
        acc = acc + _shift_rows(x, k - 2) * w_ref[k:k + 1, :]
    return acc


def conv_fwd(name, proj, conv_w, conv_b, R):
    segs = ((0, LC), (LC, R))

    def body(x_ref, w_ref, b_ref, o_ref):
        for (s, e) in segs:
            pre = _conv_pre(x_ref[s:e, :], w_ref, b_ref)
            o_ref[s:e, :] = _silu(pre)

    return pl.pallas_call(
        body, name=name, grid=(12,),
        in_specs=[pl.BlockSpec((R, 128), lambda j: (0, C_XBC // 128 + j)),
                  pl.BlockSpec((8, 128), lambda j: (0, j)), pl.BlockSpec((1, 128), lambda j: (0, j))],
        out_specs=pl.BlockSpec((R, 128), lambda j: (0, j)),
        out_shape=jax.ShapeDtypeStruct((R, 1536), F32), compiler_params=_cp("arbitrary"))(proj, conv_w, conv_b)


def conv_bwd(name, proj, conv_w, conv_b, d_f, d_b, d_skip, dproj, R):
    segs = ((0, LC), (LC, R))

    def body(x_ref, w_ref, b_ref, df_ref, db_ref, ds_ref, _, dx_ref, dw_ref, dbias_ref):
        j = pl.program_id(0)
        has_skip = (j < 8).astype(F32)
        dw = [jnp.zeros((1, 128), F32) for _ in range(5)]
        dbias = jnp.zeros((1, 128), F32)
        for (s, e) in segs:
            x = x_ref[s:e, :]
            pre = _conv_pre(x, w_ref, b_ref)
            sig = jax.nn.sigmoid(pre)
            dy = df_ref[s:e, :] + db_ref[s:e, :]
            if s == LC:
                dy = dy + ds_ref[...] * has_skip
            dpre = dy * (sig * (1.0 + pre * (1.0 - sig)))
            dx = jnp.zeros_like(x)
            for k in range(5):
                dx = dx + _shift_rows(dpre, 2 - k) * w_ref[k:k + 1, :]
                dw[k] = dw[k] + jnp.sum(dpre * _shift_rows(x, k - 2), axis=0, keepdims=True)
            dbias = dbias + jnp.sum(dpre, axis=0, keepdims=True)
            dx_ref[s:e, :] = dx.astype(dx_ref.dtype)
        dw_ref[...] = jnp.zeros_like(dw_ref)
        for k in range(5):
            dw_ref[k:k + 1, :] = dw[k]
        dbias_ref[...] = dbias

    return pl.pallas_call(
        body, name=name, grid=(12,),
        in_specs=[pl.BlockSpec((R, 128), lambda j: (0, C_XBC // 128 + j)),
                  pl.BlockSpec((8, 128), lambda j: (0, j)), pl.BlockSpec((1, 128), lambda j: (0, j)),
                  pl.BlockSpec((R, 128), lambda j: (0, j)), pl.BlockSpec((R, 128), lambda j: (0, j)),
                  pl.BlockSpec((R - LC, 128), lambda j: (0, jnp.minimum(j, 7))),
                  pl.BlockSpec(memory_space=pl.ANY)],
        out_specs=[pl.BlockSpec((R, 128), lambda j: (0, C_XBC // 128 + j)),
                   pl.BlockSpec((8, 128), lambda j: (0, j)), pl.BlockSpec((1, 128), lambda j: (0, j))],
        out_shape=[jax.ShapeDtypeStruct(dproj.shape, dproj.dtype), jax.ShapeDtypeStruct((8, 1536), F32),
                   jax.ShapeDtypeStruct((1, 1536), F32)],
        input_output_aliases={6: 0}, compiler_params=_cp("arbitrary"))(proj, conv_w, conv_b, d_f, d_b, d_skip, dproj)


def _ssd_chunk(rev, dirn):
    cums = _cumsum_fn(rev)

    def f(xs, Bs, Cs, dt, alog, Hs):
        lane = lax.broadcasted_iota(jnp.int32, (1, 128), 1)
        sub = lax.broadcasted_iota(jnp.int32, (Q, 1), 0)
        r = lax.broadcasted_iota(jnp.int32, (Q, Q), 0)
        c = lax.broadcasted_iota(jnp.int32, (Q, Q), 1)
        mask = (r <= c) if rev else (r >= c)
        left = lane < 64
        a = dt * (-jnp.exp(alog))
        s = cums(a)
        sT, dtT = s.T, dt.T
        last_row = (sub == (0 if rev else Q - 1)).astype(F32)
        s_last = jnp.sum(s * last_row, axis=0, keepdims=True)
        G = [mm_nt(Cs[g], Bs[g]) for g in range(2)]
        M, es, wc, ed = [], [], [], []
        for h in range(NH):
            l = 16 * dirn + h
            oh_l = (lane == l).astype(F32)
            oh_s = (sub == l).astype(F32)
            s_col = jnp.sum(s * oh_l, axis=1, keepdims=True)
            dt_col = jnp.sum(dt * oh_l, axis=1, keepdims=True)
            s_row = jnp.sum(sT * oh_s, axis=0, keepdims=True)
            dt_row = jnp.sum(dtT * oh_s, axis=0, keepdims=True)
            sl = jnp.sum(s_last * oh_l, axis=1, keepdims=True)
            seg = jnp.where(mask, s_col - s_row, 0.0)
            lm = jnp.where(mask, jnp.exp(seg), 0.0)
            M.append(G[h // 8] * lm * dt_row)
            es.append(jnp.exp(s_col))
            wc.append(jnp.exp(sl - s_col) * dt_col)
            ed.append(jnp.exp(sl))
        Ys, Hn = [], []
        for j in range(8):
            g = j // 4
            xa = jnp.where(left, xs[j], 0.0)
            xb = jnp.where(left, 0.0, xs[j])
            yd = mm(M[2 * j], xa) + mm(M[2 * j + 1], xb)
            yo = mm(Cs[g], Hs[j]) * jnp.where(left, es[2 * j], es[2 * j + 1])
            Ys.append(yd + yo)
            st = mm_tn(Bs[g], xs[j] * jnp.where(left, wc[2 * j], wc[2 * j + 1]))
            Hn.append(Hs[j] * jnp.where(left, ed[2 * j], ed[2 * j + 1]) + st)
        return Ys, Hn

    return f


def _chunk_of(t, n, rev):
    if not rev:
        return t
    return jnp.where(t < 2, 1 - t, n + 1 - t)


def ssd_fwd(name, xbc, dt, alog, n, rev, dirn):
    chunk = _ssd_chunk(rev, dirn)

    def body(x_ref, b_ref, c_ref, dt_ref, al_ref, y_ref, hs_ref, h_scr):
        @pl.when(pl.program_id(0) == 0)
        def _():
            h_scr[...] = jnp.zeros_like(h_scr)

        xs = [x_ref[:, 128 * j:128 * (j + 1)] for j in range(8)]
        Bs = [b_ref[:, 128 * g:128 * (g + 1)] for g in range(2)]
        Cs = [c_ref[:, 128 * g:128 * (g + 1)] for g in range(2)]
        Hs = [h_scr[:, 128 * j:128 * (j + 1)] for j in range(8)]
        hs_ref[0] = h_scr[...]
        Ys, Hn = chunk(xs, Bs, Cs, dt_ref[...], al_ref[...], Hs)
        for j in range(8):
            y_ref[:, 128 * j:128 * (j + 1)] = Ys[j]
            h_scr[:, 128 * j:128 * (j + 1)] = Hn[j]

    cm = lambda t: _chunk_of(t, n, rev)
    return pl.pallas_call(
        body, name=name, grid=(n,),
        in_specs=[pl.BlockSpec((Q, 1024), lambda t: (cm(t), 0)), pl.BlockSpec((Q, 256), lambda t: (cm(t), 4)),
                  pl.BlockSpec((Q, 256), lambda t: (cm(t), 5)), pl.BlockSpec((Q, 128), lambda t: (cm(t), 0)),
                  pl.BlockSpec((1, 128), lambda t: (0, 0))],
        out_specs=[pl.BlockSpec((Q, 1024), lambda t: (cm(t), 0)), pl.BlockSpec((1, Q, 1024), lambda t: (cm(t), 0, 0))],
        out_shape=[jax.ShapeDtypeStruct((n * Q, 1024), F32), jax.ShapeDtypeStruct((n, Q, 1024), F32)],
        scratch_shapes=[pltpu.VMEM((Q, 1024), F32)], compiler_params=_cp("arbitrary"))(xbc, xbc, xbc, dt, alog)


def ssd_bwd(name, xbc, dt, alog, hs, dy, n, rev, dirn):
    chunk = _ssd_chunk(rev, dirn)

    def body(x_ref, b_ref, c_ref, dt_ref, al_ref, hs_ref, dy_ref, dx_ref, ddt_ref, dal_ref, dh_scr):
        tt = pl.program_id(0)
        ch = _chunk_of(n - 1 - tt, n, rev)

        @pl.when(tt == 0)
        def _():
            dh_scr[...] = jnp.zeros_like(dh_scr)

        xs = [x_ref[:, 128 * j:128 * (j + 1)] for j in range(8)]
        Bs = [b_ref[:, 128 * g:128 * (g + 1)] for g in range(2)]
        Cs = [c_ref[:, 128 * g:128 * (g + 1)] for g in range(2)]
        Hs = [hs_ref[0, :, 128 * j:128 * (j + 1)] for j in range(8)]
        live = (ch >= 2).astype(F32)
        dYs = [dy_ref[:, 128 * j:128 * (j + 1)] * live for j in range(8)]
        dHn = [dh_scr[:, 128 * j:128 * (j + 1)] for j in range(8)]
        _, vjp_fn = jax.vjp(chunk, xs, Bs, Cs, dt_ref[...], al_ref[...], Hs)
        dxs, dBs, dCs, ddt, dal, dHs = vjp_fn((dYs, dHn))
        for j in range(8):
            dx_ref[:, 128 * j:128 * (j + 1)] = dxs[j]
            dh_scr[:, 128 * j:128 * (j + 1)] = dHs[j]
        for g in range(2):
            dx_ref[:, 1024 + 128 * g:1024 + 128 * (g + 1)] = dBs[g]
            dx_ref[:, 1280 + 128 * g:1280 + 128 * (g + 1)] = dCs[g]
        ddt_ref[...] = ddt

        @pl.when(tt == 0)
        def _():
            dal_ref[...] = dal

        @pl.when(tt > 0)
        def _():
            dal_ref[...] += dal

    cm = lambda t: _chunk_of(n - 1 - t, n, rev)
    return pl.pallas_call(
        body, name=name, grid=(n,),
        in_specs=[pl.BlockSpec((Q, 1024), lambda t: (cm(t), 0)), pl.BlockSpec((Q, 256), lambda t: (cm(t), 4)),
                  pl.BlockSpec((Q, 256), lambda t: (cm(t), 5)), pl.BlockSpec((Q, 128), lambda t: (cm(t), 0)),
                  pl.BlockSpec((1, 128), lambda t: (0, 0)), pl.BlockSpec((1, Q, 1024), lambda t: (cm(t), 0, 0)),
                  pl.BlockSpec((Q, 1024), lambda t: (jnp.maximum(cm(t) - 2, 0), 0))],
        out_specs=[pl.BlockSpec((Q, 1536), lambda t: (cm(t), 0)), pl.BlockSpec((Q, 128), lambda t: (cm(t), 0)),
                   pl.BlockSpec((1, 128), lambda t: (0, 0))],
        out_shape=[jax.ShapeDtypeStruct((n * Q, 1536), F32), jax.ShapeDtypeStruct((n * Q, 128), F32),
                   jax.ShapeDtypeStruct((1, 128), F32)],
        scratch_shapes=[pltpu.VMEM((Q, 1024), F32)], compiler_params=_cp("arbitrary"))(xbc, xbc, xbc, dt, alog, hs, dy)


def f_norm0(x, g0, b0, sc, sh):
    x0 = _ln(x, g0, b0)
    return x0, x0 * (1.0 + sc) + sh


def f_dt(raw, bias):
    z = split4(raw)[0] + bias
    dt = jnp.maximum(z, 0.0) + jnp.log1p(jnp.exp(-jnp.abs(z)))
    return dt, dt


def f_gated_norm(yf, yb, xs, z, dcol, g):
    h = (yf + yb + xs * dcol) * _silu(z)
    return (h * lax.rsqrt(jnp.mean(h * h, axis=-1, keepdims=True) + LN_EPS) * g,)


def f_gmlp(uv, gmg, gmb, *wb):
    ws, bs = wb[:8], wb[8:]
    u, v = split2(uv)
    vn = split8(_ln(_gelu(v), gmg, gmb))
    mixed = concat8(tuple(mm(ws[g], vn[g]) + bs[g] for g in range(8)))
    return (_gelu(u) * mixed,)


def f_merge(ps, pg, gates, bg):
    gs, gg = split2(jax.nn.sigmoid(gates + bg))
    return (gs * ps + gg * pg,)


def f_res1(x0, out, g1, lg, lb, sc, sh):
    x1 = _ln(ALPHA * x0 + g1 * out, lg, lb)
    return x1, x1 * (1.0 + sc) + sh


def f_res2_loss(x1, ff, tgt, g2, lg, lb):
    x2 = _ln(ALPHA * x1 + g2 * ff, lg, lb)
    e = x2 - tgt
    return (0.5 * jnp.sum(jnp.mean(e * e, axis=-1, keepdims=True), axis=0, keepdims=True),)


def _row_tile(M):
    return 544 if M % 544 == 0 else (512 if M % 512 == 0 else M)


def core(x_all, tgt, mod_x, mod_c, W, S):
    R = x_all.shape[0]
    L = R - LC
    n = R // Q
    T = 256
    nt, ntl = R // T, L // T
    tmR, tmL = _row_tile(R), _row_tile(L)
    tkR = 256 if R % 512 else 512
    tkL = 512 if L % 512 == 0 else 256
    row = lambda v: v.reshape(1, -1)
    mx = [row(mod_x[k]) for k in range(6)]
    mc = [row(mod_c[k]) for k in range(6)]
    sel = lambda i: jnp.minimum(i, 1)
    sc1 = jnp.stack([mc[1], mx[1]])
    sh1 = jnp.stack([mc[0], mx[0]])
    ln0 = [(row(S['ln0_g']), None), (row(S['ln0_b']), None), (sc1, sel), (sh1, sel)]

    x0, xm = stage_fwd("norm0_fwd", f_norm0, T, nt, [(x_all, D, 0, 0)], ln0, [('new', R, D, 0), ('new', R, D, 0, BF16)])
    proj = matmul("proj_fwd", xm, W['w_in'], 'nn', tmR, PW // 2, 1024)
    conv_w8 = jnp.pad(S['conv_w'], ((0, 3), (0, 0)))
    conv_b = row(S['conv_b'])
    xbc = conv_fwd("conv_fwd", proj, conv_w8, conv_b, R)
    dt_bias = jnp.pad(S['dt_bias'].reshape(1, 32), ((0, 0), (0, 96)))
    alog = jnp.pad(S['a_log'].reshape(1, 32), ((0, 0), (0, 96)))
    x_dt = [(proj, 512, C_DT // 512, 0)]
    dt_f, dt_b = stage_fwd("dt_fwd", f_dt, T, nt, x_dt, [(dt_bias, None)], [('new', R, 128, 0), ('new', R, 128, 0)])
    y_f, hs_f = ssd_fwd("ssd_fwd_f", xbc, dt_f, alog, n, False, 0)
    y_b, hs_b = ssd_fwd("ssd_fwd_b", xbc, dt_b, alog, n, True, 1)
    dcol = jnp.repeat(S['d_skip'][0] + S['d_skip'][1], 64).reshape(1, D)
    x_gn = [(y_f, D, 0, 1), (y_b, D, 0, 1), (xbc, D, 0, 1), (proj, D, C_Z // D, 1)]
    p_gn = [(dcol, None), (row(S['ssd_norm_g']), None)]
    (yn,) = stage_fwd("gnorm_fwd", f_gated_norm, T, ntl, x_gn, p_gn, [('new', L, D, 0, BF16)])
    x_gm = [(proj, 2 * D, C_UV // (2 * D), LC // Q)]
    p_gm = ([(row(S['gm_norm_g']), None), (row(S['gm_norm_b']), None)]
            + [(S['w_spatial'][g], None) for g in range(8)] + [(S['b_spatial'][g].reshape(Q, 1), None) for g in range(8)])
    (y_gm,) = stage_fwd("gmlp_fwd", f_gmlp, Q, L // Q, x_gm, p_gm, [('new', L, D, 0, BF16)])
    p_ssd = matmul("pssd_fwd", yn, W['w_ssd_proj'], 'nn', tmL, 1024, 1024)
    p_g = matmul("pgm_fwd", y_gm, W['w_gm_proj'], 'nn', tmL, 1024, 1024)
    x_mg = [(p_ssd, D, 0, 0), (p_g, D, 0, 0), (proj, 2 * D, C_GATE // (2 * D), 1)]
    p_mg = [(row(S['b_gate']), None)]
    (merged,) = stage_fwd("merge_fwd", f_merge, T, ntl, x_mg, p_mg, [('new', L, D, 0, BF16)])
    out = matmul("out_fwd", merged, W['w_out'], 'nn', tmL, 1024, 1024)
    x_r1 = [(x0, D, 0, 1), (out, D, 0, 0)]
    p_r1 = [(mx[2], None), (row(S['ln1_g']), None), (row(S['ln1_b']), None), (mx[4], None), (mx[3], None)]
    x1, hm = stage_fwd("res1_fwd", f_res1, T, ntl, x_r1, p_r1, [('new', L, D, 0), ('new', L, D, 0, BF16)])
    a1, a3, act = ffn_in_fwd("ffn_in_fwd", hm, W['w_ff1'], W['w_ff3'], T)
    ff = ff_out_fwd("ff2_fwd", act, W['w_ff2'], tmL)
    x_r2 = [(x1, D, 0, 0), (ff, D, 0, 0), (tgt, D, 0, 0)]
    p_r2 = [(mx[5], None), (row(S['ln2_g']), None), (row(S['ln2_b']), None)]

    dx1_a, dff, dg2, dl2g, dl2b, loss = stage_bwd(
        "res2_bwd", f_res2_loss, T, ntl, x_r2, p_r2, [1.0],
        [('new', L, D, 0), ('new', L, D, 0, BF16), None], [True, True, True], primal=[(0, (1, 1))])
    da1, da3 = ffn_out_bwd_x("ffn_out_bwd_x", dff, W['w_ff2'], a1, a3, T)
    gw_ff2 = ff_out_bwd_w("ff2_bwd_w", act, dff, tkL)
    dhm = ff_in_bwd_x("ff1_bwd_x", da1, W['w_ff1'], tmL)
    dhm = ff_in_bwd_x("ff3_bwd_x", da3, W['w_ff3'], tmL, add=dhm)
    gw_ff1 = ff_in_bwd_w("ff1_bwd_w", hm, da1, tkL)
    gw_ff3 = ff_in_bwd_w("ff3_bwd_w", hm, da3, tkL)
    dx0_a, dout, dg1, dl1g, dl1b, dsc2, dsh2 = stage_bwd(
        "res1_bwd", f_res1, T, ntl, x_r1, p_r1, [(dx1_a, D, 0, 0), (dhm, D, 0, 0)],
        [('new', L, D, 0), ('new', L, D, 0, BF16)], [True] * 5)
    dmerged = matmul("out_bwd_x", dout, W['w_out'], 'nt', tmL, 1024, 1024)
    gw_out = matmul("out_bwd_w", merged, dout, 'tn', 1024, 1024, tkL, BF16)
    lt, lq = -(LC // T), -(LC // Q)
    x_mg_b = [(p_ssd, D, 0, lt), (p_g, D, 0, lt), (proj, 2 * D, C_GATE // (2 * D), 0)]
    dp_ssd, dp_g, dproj, dbg = stage_bwd(
        "merge_bwd", f_merge, T, nt, x_mg_b, p_mg, [(dmerged, D, 0, lt)],
        [('new', L, D, lt, BF16), ('new', L, D, lt, BF16), ('part', R, PW, 2 * D, C_GATE // (2 * D), 0, BF16)], [True])
    dyn = matmul("pssd_bwd_x", dp_ssd, W['w_ssd_proj'], 'nt', tmL, 1024, 1024)
    gw_ssd = matmul("pssd_bwd_w", yn, dp_ssd, 'tn', 1024, 1024, tkL, BF16)
    dy_gm = matmul("pgm_bwd_x", dp_g, W['w_gm_proj'], 'nt', tmL, 1024, 1024)
    gw_gm = matmul("pgm_bwd_w", y_gm, dp_g, 'tn', 1024, 1024, tkL, BF16)
    r_gm = stage_bwd("gmlp_bwd", f_gmlp, Q, n, [(proj, 2 * D, C_UV // (2 * D), 0)], p_gm, [(dy_gm, D, 0, lq)],
                     [('alias', dproj, 2 * D, C_UV // (2 * D), 0)], [True] * 18)
    dproj, dgmg, dgmb, dws, dbs = r_gm[0], r_gm[1], r_gm[2], r_gm[3:11], r_gm[11:19]
    x_gn_b = [(y_f, D, 0, 0), (y_b, D, 0, 0), (xbc, D, 0, 0), (proj, D, C_Z // D, 0)]
    dy, dskipx, dproj, ddcol, dng = stage_bwd(
        "gnorm_bwd", f_gated_norm, T, nt, x_gn_b, p_gn, [(dyn, D, 0, lt)],
        [('new', L, D, lt), None, ('new', L, D, lt), ('alias', dproj, D, C_Z // D, 0)], [True, True])
    dxbc_f, ddt_f, dal_f = ssd_bwd("ssd_bwd_f", xbc, dt_f, alog, hs_f, dy, n, False, 0)
    dxbc_b, ddt_b, dal_b = ssd_bwd("ssd_bwd_b", xbc, dt_b, alog, hs_b, dy, n, True, 1)
    dproj, ddtb = stage_bwd("dt_bwd", f_dt, T, nt, x_dt, [(dt_bias, None)],
                            [(ddt_f, 128, 0, 0), (ddt_b, 128, 0, 0)],
                            [('alias', dproj, 512, C_DT // 512, 0)], [True])
    dproj, dcw8, dcb = conv_bwd("conv_bwd", proj, conv_w8, conv_b, dxbc_f, dxbc_b, dskipx, dproj, R)
    dxm = matmul("proj_bwd_x", dproj, W['w_in'], 'nt', R // 4 if R % 32 == 0 else R, 1024, 1024)
    gw_in = matmul("proj_bwd_w", xm, dproj, 'tn', 1024, PW // 4, tkR, BF16)
    grad_x, dl0g, dl0b, dsc1, dsh1 = stage_bwd(
        "norm0_bwd", f_norm0, T, nt, [(x_all, D, 0, 0)], ln0, [(dx0_a, D, 0, -1), (dxm, D, 0, 0)],
        [('new', L, D, -1)], [True] * 4)

    zero = jnp.zeros((D,), F32)
    flat = lambda v: v.reshape(-1)
    small = {
        'loss': flat(loss), 'ln0_g': flat(dl0g), 'ln0_b': flat(dl0b),
        'dmod_x': jnp.concatenate([flat(dsh1[1]), flat(dsc1[1]), flat(dg1), flat(dsh2), flat(dsc2), flat(dg2)]),
        'dmod_c': jnp.concatenate([flat(dsh1[0]), flat(dsc1[0]), zero, zero, zero, zero]),
        'conv_w': flat(dcw8[:5]), 'conv_b': flat(dcb), 'dt_bias': flat(ddtb[:, :32]),
        'a_log': flat((dal_f + dal_b)[:, :32]),
        'd_skip': flat(jnp.tile(ddcol.reshape(1, NH, 64).sum(-1), (2, 1))),
        'ssd_norm_g': flat(dng), 'gm_norm_g': flat(dgmg), 'gm_norm_b': flat(dgmb),
        'w_spatial': flat(jnp.stack(dws)), 'b_spatial': flat(jnp.stack(dbs)), 'b_gate': flat(dbg),
        'ln1_g': flat(dl1g), 'ln1_b': flat(dl1b), 'ln2_g': flat(dl2g), 'ln2_b': flat(dl2b),
    }
    big = {'w_in': gw_in, 'w_ssd_proj': gw_ssd, 'w_gm_proj': gw_gm, 'w_out': gw_out,
           'w_ff1': gw_ff1, 'w_ff3': gw_ff3, 'w_ff2': gw_ff2}
    return grad_x, big, small


def _place():
    return lax.axis_index("x"), lax.axis_index("y"), lax.axis_index("c")


def allgather8(name, blk, hbm):
    space = pl.ANY if hbm else pltpu.VMEM

    def body(x_ref, out_ref, send_sems, recv_sems, local_sem):
        x, y, c = _place()
        me, sibling = (x, y, c), (x, y, 1 - c)
        chips = [(1 - x, y), (x, 1 - y), (1 - x, 1 - y)]

        def slot(px, py, pc):
            return out_ref.at[4 * px + 2 * py + pc]

        def copy(k, block, to, src=None):
            return pltpu.make_async_remote_copy(
                src_ref=slot(*block) if src is None else src, dst_ref=slot(*block),
                send_sem=send_sems.at[k], recv_sem=recv_sems.at[k], device_id=to, device_id_type=MESH)

        mine = pltpu.make_async_copy(x_ref, slot(*me), local_sem)
        mine.start()
        first = [copy(0, me, sibling, src=x_ref)]
        first += [copy(1 + j, me, (*chip, c), src=x_ref) for j, chip in enumerate(chips)]
        for cp in first:
            cp.start()
        passed = [copy(4 + j, (*chip, c), sibling) for j, chip in enumerate(chips)]
        for j, chip in enumerate(chips):
            copy(1 + j, (*chip, c), me).wait_recv()
            passed[j].start()
        copy(0, sibling, me).wait_recv()
        for j, chip in enumerate(chips):
            copy(4 + j, (*chip, 1 - c), me).wait_recv()
        for cp in first + passed:
            cp.wait_send()
        mine.wait()

    return pl.pallas_call(
        body, name=name, out_shape=jax.ShapeDtypeStruct((8,) + blk.shape, blk.dtype),
        in_specs=[pl.BlockSpec(memory_space=space)], out_specs=pl.BlockSpec(memory_space=space),
        scratch_shapes=[pltpu.SemaphoreType.DMA((7,)), pltpu.SemaphoreType.DMA((7,)), pltpu.SemaphoreType.DMA],
        compiler_params=pltpu.CompilerParams(vmem_limit_bytes=VMEM_LIMIT_V7X))(blk)


def gather_weights(name, blks):
    n = len(blks)

    def body(*refs):
        ins, outs = refs[:n], refs[n:2 * n]
        send_sems, recv_sems, local_sems = refs[2 * n:]
        x, y, c = _place()
        me, sibling = (x, y, c), (x, y, 1 - c)
        chips = [(1 - x, y), (x, 1 - y), (1 - x, 1 - y)]

        def copy(a, k, block, to, own=False):
            dst = outs[a].at[4 * block[0] + 2 * block[1] + block[2]]
            return pltpu.make_async_remote_copy(
                src_ref=ins[a] if own else dst, dst_ref=dst, send_sem=send_sems.at[7 * a + k],
                recv_sem=recv_sems.at[7 * a + k], device_id=to, device_id_type=MESH)

        mine = [pltpu.make_async_copy(ins[a], outs[a].at[4 * x + 2 * y + c], local_sems.at[a]) for a in range(n)]
        for cp in mine:
            cp.start()
        first = []
        for j, chip in enumerate(chips):
            first += [copy(a, 1 + j, me, (*chip, c), own=True) for a in range(n)]
        first += [copy(a, 0, me, sibling, own=True) for a in range(n)]
        for cp in first:
            cp.start()
        passed = []
        for a in range(n):
            for j, chip in enumerate(chips):
                copy(a, 1 + j, (*chip, c), me).wait_recv()
                passed.append(copy(a, 4 + j, (*chip, c), sibling))
                passed[-1].start()
        for a in range(n):
            copy(a, 0, sibling, me).wait_recv()
            for j, chip in enumerate(chips):
                copy(a, 4 + j, (*chip, 1 - c), me).wait_recv()
        for cp in first + passed:
            cp.wait_send()
        for cp in mine:
            cp.wait()

    any_spec = pl.BlockSpec(memory_space=pl.ANY)
    return pl.pallas_call(
        body, name=name, out_shape=[jax.ShapeDtypeStruct((8,) + b.shape, b.dtype) for b in blks],
        in_specs=[any_spec] * n, out_specs=[any_spec] * n,
        scratch_shapes=[pltpu.SemaphoreType.DMA((7 * n,)), pltpu.SemaphoreType.DMA((7 * n,)),
                        pltpu.SemaphoreType.DMA((n,))])(*blks)


def _exchange(name, srcs, out_shapes, ncp, plan):
    n = len(srcs)

    def body(*refs):
        ins, outs = refs[:n], refs[n:n + len(out_shapes)]
        send_sems, recv_sems = refs[n + len(out_shapes):]
        cps = [pltpu.make_async_remote_copy(src_ref=s, dst_ref=d, send_sem=send_sems.at[k], recv_sem=recv_sems.at[k],
                                            device_id=to, device_id_type=MESH)
               for k, (s, d, to) in enumerate(plan(_place(), ins, outs))]
        for cp in cps:
            cp.start()
        for cp in cps:
            cp.wait_recv()
        for cp in cps:
            cp.wait_send()

    any_spec = pl.BlockSpec(memory_space=pl.ANY)
    return pl.pallas_call(
        body, name=name, out_shape=out_shapes, in_specs=[any_spec] * n, out_specs=[any_spec] * len(out_shapes),
        scratch_shapes=[pltpu.SemaphoreType.DMA((ncp,)), pltpu.SemaphoreType.DMA((ncp,))])(*srcs)


def sibling_swap(name, gs):
    def plan(place, ins, outs):
        x, y, c = place
        return [(g.at[s, 1 - c], o.at[s], (x, y, 1 - c)) for g, o in zip(ins, outs) for s in range(4)]

    return _exchange(name, gs, [jax.ShapeDtypeStruct((4,) + g.shape[2:], g.dtype) for g in gs], 4 * len(gs), plan)


def chip_scatter(name, ss):
    def plan(place, ins, outs):
        x, y, c = place
        chips = [(1 - x, y), (x, 1 - y), (1 - x, 1 - y)]
        return [(s.at[2 * px + py], o.at[k], (px, py, c)) for k, (px, py) in enumerate(chips) for s, o in zip(ins, outs)]

    return _exchange(name, ss, [jax.ShapeDtypeStruct((3,) + s.shape[1:], s.dtype) for s in ss], 3 * len(ss), plan)


def sibling_pair(name, hs):
    n = len(hs)

    def body(*refs):
        ins, outs = refs[:n], refs[n:2 * n]
        send_sems, recv_sems = refs[2 * n:]
        x, y, c = _place()
        cps = [pltpu.make_async_remote_copy(src_ref=outs[a].at[c], dst_ref=outs[a].at[c], send_sem=send_sems.at[a],
                                            recv_sem=recv_sems.at[a], device_id=(x, y, 1 - c), device_id_type=MESH)
               for a in range(n)]
        for cp in cps:
            cp.start()
        for a in range(n):
            pltpu.make_async_remote_copy(src_ref=outs[a].at[1 - c], dst_ref=outs[a].at[1 - c], send_sem=send_sems.at[a],
                                         recv_sem=recv_sems.at[a], device_id=(x, y, 1 - c),
                                         device_id_type=MESH).wait_recv()
        for cp in cps:
            cp.wait_send()

    any_spec = pl.BlockSpec(memory_space=pl.ANY)
    return pl.pallas_call(
        body, name=name, out_shape=[jax.ShapeDtypeStruct(h.shape, h.dtype) for h in hs],
        in_specs=[any_spec] * n, out_specs=[any_spec] * n, input_output_aliases={a: a for a in range(n)},
        scratch_shapes=[pltpu.SemaphoreType.DMA((n,)), pltpu.SemaphoreType.DMA((n,))])(*hs)


def chip_sum(name, g, theirs):
    r, w = g.shape[2:]

    def body(_, a_ref, b_ref, o_ref):
        o_ref[...] = (a_ref[...].astype(F32) + b_ref[...].astype(F32)).astype(BF16)

    grid_spec = pltpu.PrefetchScalarGridSpec(
        num_scalar_prefetch=1, grid=(4,),
        in_specs=[pl.BlockSpec((None, None, r, w), lambda s, at: (s, at[0], 0, 0)),
                  pl.BlockSpec((None, r, w), lambda s, at: (s, 0, 0))],
        out_specs=pl.BlockSpec((None, r, w), lambda s, at: (s, 0, 0)))
    at = jnp.stack([lax.axis_index("c")]).astype(jnp.int32)
    return pl.pallas_call(body, name=name, grid_spec=grid_spec, out_shape=jax.ShapeDtypeStruct((4, r, w), BF16),
                          compiler_params=_cp("arbitrary"))(at, g, theirs)


def shard_sum(name, s4, others):
    r, w = s4.shape[1:]

    def body(_, a_ref, b0, b1, b2, o_ref):
        o_ref[...] = ((a_ref[...].astype(F32) + b0[...].astype(F32)) + b1[...].astype(F32)) + b2[...].astype(F32)

    oth = [pl.BlockSpec((None, r, w), lambda i, at, k=k: (k, 0, 0)) for k in range(3)]
    grid_spec = pltpu.PrefetchScalarGridSpec(
        num_scalar_prefetch=1, grid=(1,),
        in_specs=[pl.BlockSpec((None, r, w), lambda i, at: (at[0], 0, 0))] + oth,
        out_specs=pl.BlockSpec((None, r, w), lambda i, at: (at[1], 0, 0)))
    at = jnp.stack([2 * lax.axis_index("x") + lax.axis_index("y"), lax.axis_index("c")]).astype(jnp.int32)
    return pl.pallas_call(body, name=name, grid_spec=grid_spec, out_shape=jax.ShapeDtypeStruct((2, r, w), F32),
                          compiler_params=_cp("arbitrary"))(at, s4, others, others, others)


W_IN_RUNS = ((0, 2, 1296, 376), (376, 3, 0, 1672), (2048, 1, 920, 752), (2800, 2, 0, 1296), (4096, 0, 0, 1024),
             (5120, 0, 1024, 648), (5768, 1, 0, 920))


def w_in_to_padded(name, g4):
    T = 128

    def body(g_ref, o_ref):
        o_ref[:, D_PROJ:PW] = jnp.zeros((T, PW - D_PROJ), o_ref.dtype)
        for (a, s, j0, w) in W_IN_RUNS:
            o_ref[:, a:a + w] = g_ref[s, :, j0:j0 + w]

    return pl.pallas_call(body, name=name, grid=(D // T,), in_specs=[pl.BlockSpec((4, T, 1672), lambda i: (0, i, 0))],
                          out_specs=pl.BlockSpec((T, PW), lambda i: (i, 0)),
                          out_shape=jax.ShapeDtypeStruct((D, PW), g4.dtype), compiler_params=_cp("arbitrary"))(g4)


def w_in_from_padded(name, gp):
    T = 128

    def body(g_ref, o_ref):
        for (a, s, j0, w) in W_IN_RUNS:
            o_ref[s, :, j0:j0 + w] = g_ref[:, a:a + w]

    return pl.pallas_call(body, name=name, grid=(D // T,), in_specs=[pl.BlockSpec((T, PW), lambda i: (i, 0))],
                          out_specs=pl.BlockSpec((4, T, 1672), lambda i: (0, i, 0)),
                          out_shape=jax.ShapeDtypeStruct((4, D, 1672), gp.dtype), compiler_params=_cp("arbitrary"))(gp)


def sum_devices(name, g):
    def body(g_ref, o_ref):
        acc = g_ref[0]
        for k in range(1, 8):
            acc = acc + g_ref[k]
        o_ref[...] = acc

    return pl.pallas_call(body, name=name, out_shape=jax.ShapeDtypeStruct(g.shape[1:], F32),
                          compiler_params=pltpu.CompilerParams(vmem_limit_bytes=VMEM_LIMIT_V7X))(g)


def adamw(name, w, g, m, v, T):
    r, wd = w.shape
    c1 = 1.0 - ADAM_B1 ** ADAM_STEP
    c2 = 1.0 - ADAM_B2 ** ADAM_STEP

    def body(w_ref, g_ref, m_ref, v_ref, d_ref, mo_ref, vo_ref):
        gv = g_ref[...]
        mn = ADAM_B1 * m_ref[...] + (1.0 - ADAM_B1) * gv
        vn = ADAM_B2 * v_ref[...] + (1.0 - ADAM_B2) * (gv * gv)
        d_ref[...] = -ADAM_LR * ((mn / c1) / (jnp.sqrt(vn / c2) + ADAM_EPS) + ADAM_WD * w_ref[...])
        mo_ref[...] = mn
        vo_ref[...] = vn

    spec = pl.BlockSpec((T, wd), lambda i: (i, 0))
    return pl.pallas_call(body, name=name, grid=(r // T,), in_specs=[spec] * 4, out_specs=[spec] * 3,
                          out_shape=[jax.ShapeDtypeStruct((r, wd), F32)] * 3, compiler_params=_cp("arbitrary"))(w, g, m, v)


BIG = {'w_in': (1024, 1672), 'w_ssd_proj': (256, 1024), 'w_gm_proj': (256, 1024), 'w_out': (256, 1024),
       'w_ff1': (1024, 704), 'w_ff3': (1024, 704), 'w_ff2': (704, 1024)}


class Flat:
    def __init__(self, segs):
        self.off, o = {}, 0
        for name, size in segs:
            self.off[name] = (o, size)
            o += -(-size // 128) * 128
        self.rows = -(-o // 1024) * 8

    def pack(self, vals):
        parts = []
        for name, (o, size) in self.off.items():
            v = vals[name].reshape(-1).astype(F32)
            parts.append(jnp.pad(v, (0, -(-size // 128) * 128 - size)))
        buf = jnp.concatenate(parts)
        return jnp.pad(buf, (0, self.rows * 128 - buf.shape[0])).reshape(self.rows, 128)

    def get(self, buf, name, shape=None):
        o, size = self.off[name]
        v = buf[o // 128:(o + size + 127) // 128].reshape(-1)[:size]
        return v if shape is None else v.reshape(shape)


PARTIALS = Flat([('loss', 1), ('ln0_g', D), ('ln0_b', D), ('dmod_x', 6 * D), ('dmod_c', 6 * D), ('conv_w', 5 * 1536),
                 ('conv_b', 1536), ('dt_bias', 32), ('a_log', 32), ('d_skip', 32), ('ssd_norm_g', D),
                 ('gm_norm_g', D), ('gm_norm_b', D), ('w_spatial', 8 * Q * Q), ('b_spatial', 8 * Q), ('b_gate', 2 * D),
                 ('ln1_g', D), ('ln1_b', D), ('ln2_g', D), ('ln2_b', D)])

WEIGHTS = ('c_ctx', 'ln0_g', 'ln0_b', 'w_ada', 'b_ada', 'w_in', 'conv_w', 'conv_b', 'dt_bias', 'a_log', 'd_skip',
           'ssd_norm_g', 'gm_norm_g', 'gm_norm_b', 'w_spatial', 'b_spatial', 'b_gate', 'w_ssd_proj', 'w_gm_proj',
           'w_out', 'ln1_g', 'ln1_b', 'w_ff1', 'w_ff3', 'w_ff2', 'ln2_g', 'ln2_b')
BIG_NAMES = tuple(BIG)
SMALL_NAMES = tuple(n for n in WEIGHTS if n not in BIG_NAMES and n != 'w_ada')


def kernel(x, c, ctx, c_ctx, ln0_g, ln0_b, w_ada, b_ada, w_in, conv_w, conv_b, dt_bias, a_log, d_skip, ssd_norm_g, gm_norm_g, gm_norm_b, w_spatial, b_spatial, b_gate, w_ssd_proj, w_gm_proj, w_out, ln1_g, ln1_b, w_ff1, w_ff3, w_ff2, ln2_g, ln2_b, loss_target, m_c_ctx, m_ln0_g, m_ln0_b, m_w_ada, m_b_ada, m_w_in, m_conv_w, m_conv_b, m_dt_bias, m_a_log, m_d_skip, m_ssd_norm_g, m_gm_norm_g, m_gm_norm_b, m_w_spatial, m_b_spatial, m_b_gate, m_w_ssd_proj, m_w_gm_proj, m_w_out, m_ln1_g, m_ln1_b, m_w_ff1, m_w_ff3, m_w_ff2, m_ln2_g, m_ln2_b, v_c_ctx, v_ln0_g, v_ln0_b, v_w_ada, v_b_ada, v_w_in, v_conv_w, v_conv_b, v_dt_bias, v_a_log, v_d_skip, v_ssd_norm_g, v_gm_norm_g, v_gm_norm_b, v_w_spatial, v_b_spatial, v_b_gate, v_w_ssd_proj, v_w_gm_proj, v_w_out, v_ln1_g, v_ln1_b, v_w_ff1, v_w_ff3, v_w_ff2, v_ln2_g, v_ln2_b):
    wts = dict(c_ctx=c_ctx, ln0_g=ln0_g, ln0_b=ln0_b, w_ada=w_ada, b_ada=b_ada, w_in=w_in, conv_w=conv_w, conv_b=conv_b,
               dt_bias=dt_bias, a_log=a_log, d_skip=d_skip, ssd_norm_g=ssd_norm_g, gm_norm_g=gm_norm_g,
               gm_norm_b=gm_norm_b, w_spatial=w_spatial, b_spatial=b_spatial, b_gate=b_gate, w_ssd_proj=w_ssd_proj,
               w_gm_proj=w_gm_proj, w_out=w_out, ln1_g=ln1_g, ln1_b=ln1_b, w_ff1=w_ff1, w_ff3=w_ff3, w_ff2=w_ff2,
               ln2_g=ln2_g, ln2_b=ln2_b)
    ms = dict(zip(WEIGHTS, (m_c_ctx, m_ln0_g, m_ln0_b, m_w_ada, m_b_ada, m_w_in, m_conv_w, m_conv_b, m_dt_bias, m_a_log,
                            m_d_skip, m_ssd_norm_g, m_gm_norm_g, m_gm_norm_b, m_w_spatial, m_b_spatial, m_b_gate,
                            m_w_ssd_proj, m_w_gm_proj, m_w_out, m_ln1_g, m_ln1_b, m_w_ff1, m_w_ff3, m_w_ff2, m_ln2_g,
                            m_ln2_b)))
    vs = dict(zip(WEIGHTS, (v_c_ctx, v_ln0_g, v_ln0_b, v_w_ada, v_b_ada, v_w_in, v_conv_w, v_conv_b, v_dt_bias, v_a_log,
                            v_d_skip, v_ssd_norm_g, v_gm_norm_g, v_gm_norm_b, v_w_spatial, v_b_spatial, v_b_gate,
                            v_w_ssd_proj, v_w_gm_proj, v_w_out, v_ln1_g, v_ln1_b, v_w_ff1, v_w_ff3, v_w_ff2, v_ln2_g,
                            v_ln2_b)))
    px, py, pc = _place()
    shard = 2 * px + py
    dev = 2 * shard + pc
    take = lambda a, i, axis=0: lax.dynamic_index_in_dim(a, i, axis, keepdims=False)

    pre = jnp.concatenate([c, jnp.pad(conv_w[0], ((0, 0), (0, D - 384))), jnp.zeros((2, D), F32)], axis=0)
    pre = allgather8("gather_cond", pre, False)
    conv_w_full = pre[0::2, 1:6, :384].transpose(1, 0, 2).reshape(5, 1536)
    a16 = jnp.concatenate([_silu(pre[:, 0, :]), _silu(c_ctx)[None], jnp.zeros((7, D), F32)], axis=0)
    mod = matmul("ada_fwd", a16, w_ada[0], 'nn', 16, 512, 1024)
    mod = mod + lax.dynamic_slice_in_dim(b_ada[0], shard * 1536, 1536)[None]
    mod = allgather8("gather_mod", mod, False)
    mod = jnp.concatenate([mod[0], mod[2], mod[4], mod[6]], axis=1)
    mod_x = take(mod, dev).reshape(6, D)
    mod_c = mod[8].reshape(6, D)

    halves = [take(wts[n][0].reshape(2, BIG[n][0] // 2, BIG[n][1]), pc).astype(BF16) for n in BIG_NAMES]
    W = {}
    for n, blocks in zip(BIG_NAMES, gather_weights("gather_weights", halves)):
        r, w = BIG[n]
        W[n] = blocks.reshape(4, r, w) if w != D else blocks.reshape(4 * r, w)
    W['w_in'] = w_in_to_padded("w_in_layout", W['w_in'])

    S = dict(ln0_g=ln0_g, ln0_b=ln0_b, conv_w=conv_w_full, conv_b=conv_b[0], dt_bias=dt_bias[0], a_log=a_log[0],
             d_skip=d_skip[0], ssd_norm_g=ssd_norm_g[0], gm_norm_g=gm_norm_g[0], gm_norm_b=gm_norm_b[0],
             w_spatial=w_spatial[0], b_spatial=b_spatial[0], b_gate=b_gate[0], ln1_g=ln1_g[0], ln1_b=ln1_b[0],
             ln2_g=ln2_g[0], ln2_b=ln2_b[0])
    x_all = jnp.concatenate([ctx[0], x[0]], axis=0)
    grad_x, gbig, gsmall = core(x_all, loss_target[0], mod_x, mod_c, W, S)

    gbig['w_in'] = w_in_from_padded("w_in_grad_layout", gbig['w_in'])
    blocks = [gbig[n].reshape(4, 2, BIG[n][0] // 2, BIG[n][1]) for n in BIG_NAMES]
    theirs = sibling_swap("grads_to_sibling", blocks)
    sums = [chip_sum("grads_chip_sum_" + n, b, t) for n, b, t in zip(BIG_NAMES, blocks, theirs)]
    others = chip_scatter("grads_to_chips", sums)
    halves = [shard_sum("grads_sum_" + n, s, o) for n, s, o in zip(BIG_NAMES, sums, others)]
    g_shards = {n: h.reshape(BIG[n]) for n, h in zip(BIG_NAMES, sibling_pair("grads_halves", halves))}

    parts = allgather8("gather_partials", PARTIALS.pack(gsmall), False)
    tot = sum_devices("partials_sum", parts)
    g = {n: PARTIALS.get(tot, n) for n in ('ln0_g', 'ln0_b', 'conv_b', 'dt_bias', 'a_log', 'd_skip', 'ssd_norm_g',
                                           'gm_norm_g', 'gm_norm_b', 'w_spatial', 'b_spatial', 'b_gate', 'ln1_g',
                                           'ln1_b', 'ln2_g', 'ln2_b')}
    loss = PARTIALS.get(tot, 'loss', ())
    dmod_c = PARTIALS.get(tot, 'dmod_c')
    g['b_ada'] = PARTIALS.get(tot, 'dmod_x') + dmod_c
    g['conv_w'] = lax.dynamic_slice_in_dim(PARTIALS.get(tot, 'conv_w', (5, 1536)), shard * 384, 384, axis=1)
    o, size = PARTIALS.off['dmod_x']
    dmod_rows = parts[:, o // 128:(o + size) // 128].reshape(8, size)
    dm = jnp.concatenate([dmod_rows, dmod_c[None], jnp.zeros((7, 6 * D), F32)], axis=0)
    dm = lax.dynamic_slice_in_dim(dm, shard * 1536, 1536, axis=1)
    g['w_ada'] = matmul("ada_bwd_w", a16, dm, 'tn', 1024, 512, 16)
    dm_c = jnp.concatenate([dm[8:9], jnp.zeros((15, 1536), F32)], axis=0)
    dc = matmul("ada_bwd_c", dm_c, w_ada[0], 'nt', 16, 1024, 512)
    dc = allgather8("gather_dcctx", dc, False)[:, 0, :]
    dc = ((dc[0] + dc[2]) + dc[4]) + dc[6]
    sg = jax.nn.sigmoid(c_ctx)
    g['c_ctx'] = dc * (sg * (1.0 + c_ctx * (1.0 - sg)))
    for n in BIG_NAMES:
        g[n] = g_shards[n]

    delta, new_m, new_v = {}, {}, {}
    for n in BIG_NAMES + ('w_ada',):
        w2 = wts[n][0]
        T = 352 if n == 'w_ff2' else 256
        d_, m_, v_ = adamw("adamw_" + n, w2, g[n], ms[n][0], vs[n][0], T)
        delta[n], new_m[n], new_v[n] = d_, m_, v_
    lay = Flat([(n, wts[n].size) for n in SMALL_NAMES])
    d_, m_, v_ = adamw("adamw_small", lay.pack(wts), lay.pack(g), lay.pack(ms), lay.pack(vs), lay.rows)
    for n in SMALL_NAMES:
        delta[n], new_m[n], new_v[n] = (lay.get(b, n) for b in (d_, m_, v_))

    shp = lambda d: [d[n].reshape(wts[n].shape) for n in WEIGHTS]
    return (loss, grad_x[None], *shp(g), *shp(delta), *shp(new_m), *shp(new_v))
```

```python
import functools

import jax
import jax.numpy as jnp
from jax import lax
from jax.experimental import pallas as pl
from jax.experimental.pallas import tpu as pltpu

F32 = jnp.float32
BF16 = jnp.bfloat16
MESH = pl.DeviceIdType.MESH

VMEM_LIMIT_V7X = 56 * 1024 * 1024

D = 1024
LC = 256
Q = 128
NH = 16
D_FF = 2816
LN_EPS = 1e-5
ALPHA = 2.0 ** 0.25

PW = 7168
C_GATE, C_UV, C_Z, C_XBC, C_DT = 0, 2048, 4096, 5120, 6656
D_PROJ = 6688

ADAM_LR, ADAM_B1, ADAM_B2, ADAM_EPS, ADAM_WD, ADAM_STEP = 0.001, 0.9, 0.999, 1e-08, 0.01, 10


def _cp(*sem):
    return pltpu.CompilerParams(dimension_semantics=sem, vmem_limit_bytes=VMEM_LIMIT_V7X)


def _dot(a, b, ca, cb):
    return lax.dot_general(a.astype(BF16), b.astype(BF16), (((ca,), (cb,)), ((), ())),
                           preferred_element_type=F32)


@jax.custom_vjp
def mm(a, b):
    return _dot(a, b, 1, 0)


mm.defvjp(lambda a, b: (_dot(a, b, 1, 0), (a, b)),
          lambda r, g: (_dot(g, r[1], 1, 1), _dot(r[0], g, 0, 0)))


@jax.custom_vjp
def mm_nt(a, b):
    return _dot(a, b, 1, 1)


mm_nt.defvjp(lambda a, b: (_dot(a, b, 1, 1), (a, b)),
             lambda r, g: (_dot(g, r[1], 1, 0), _dot(g, r[0], 0, 0)))


@jax.custom_vjp
def mm_tn(a, b):
    return _dot(a, b, 0, 0)


mm_tn.defvjp(lambda a, b: (_dot(a, b, 0, 0), (a, b)),
             lambda r, g: (_dot(r[1], g, 1, 1), _dot(r[0], g, 1, 0)))


def _dot32(a, b):
    return lax.dot_general(a, b, (((1,), (0,)), ((), ())), precision=lax.Precision.HIGHEST,
                           preferred_element_type=F32)


def _cumsum_fn(rev):
    def tri(transpose):
        r = lax.broadcasted_iota(jnp.int32, (Q, Q), 0)
        c = lax.broadcasted_iota(jnp.int32, (Q, Q), 1)
        keep = (r >= c) if (rev == transpose) else (r <= c)
        return jnp.where(keep, 1.0, 0.0).astype(F32)

    @jax.custom_vjp
    def cums(a):
        return _dot32(tri(False), a)

    cums.defvjp(lambda a: (_dot32(tri(False), a), None), lambda _, g: (_dot32(tri(True), g),))
    return cums


def _cols(v, k):
    w = v.shape[1] // k
    return tuple(v[:, w * i:w * (i + 1)] for i in range(k))


def _splitter(k):
    @jax.custom_vjp
    def split(v):
        return _cols(v, k)

    @jax.custom_vjp
    def concat(ps):
        return jnp.concatenate(ps, axis=1)

    split.defvjp(lambda v: (_cols(v, k), None), lambda _, g: (jnp.concatenate(g, axis=1),))
    concat.defvjp(lambda ps: (jnp.concatenate(ps, axis=1), None), lambda _, g: (_cols(g, k),))
    return split, concat


split2, _ = _splitter(2)
split4, _ = _splitter(4)
split8, concat8 = _splitter(8)


def _ln(x, g, b):
    mu = jnp.mean(x, axis=-1, keepdims=True)
    xc = x - mu
    var = jnp.mean(xc * xc, axis=-1, keepdims=True)
    return xc * lax.rsqrt(var + LN_EPS) * g + b


def _silu(x):
    return x * jax.nn.sigmoid(x)


def _gelu(x):
    return 0.5 * x * (1.0 + jnp.tanh(0.7978845608028654 * (x + 0.044715 * (x * x * x))))


def _xspec(T, w, col, roff):
    return pl.BlockSpec((T, w), lambda i, col=col, roff=roff: (jnp.maximum(i + roff, 0), col))


def _pspec(p, sel):
    if sel is None:
        return pl.BlockSpec(p.shape, lambda i, n=p.ndim: (0,) * n)
    return pl.BlockSpec((1,) + p.shape[1:], lambda i, n=p.ndim: (sel(i),) + (0,) * (n - 1))


def _out_plumbing(outs, T, args, in_specs):
    shapes, specs, aliases = [], [], {}
    for k, o in enumerate(outs):
        if o[0] == 'new':
            _, rows, w, roff = o[:4]
            shapes.append(jax.ShapeDtypeStruct((rows, w), o[4] if len(o) > 4 else F32))
            specs.append(_xspec(T, w, 0, roff))
        elif o[0] == 'acc':
            shapes.append(jax.ShapeDtypeStruct(o[1], F32))
            specs.append(pl.BlockSpec(o[1], lambda i, n=len(o[1]): (0,) * n))
        elif o[0] == 'part':
            _, rows, wtot, w, col, roff, dtype = o
            shapes.append(jax.ShapeDtypeStruct((rows, wtot), dtype))
            specs.append(_xspec(T, w, col, roff))
        else:
            _, arr, w, col, roff = o
            aliases[len(args)] = k
            args.append(arr)
            in_specs.append(pl.BlockSpec(memory_space=pl.ANY))
            shapes.append(jax.ShapeDtypeStruct(arr.shape, arr.dtype))
            specs.append(_xspec(T, w, col, roff))
    return shapes, specs, aliases


def stage_fwd(name, f, T, n, xs, ps, outs):
    nx, npar = len(xs), len(ps)
    args = [x[0] for x in xs] + [p[0] for p in ps]
    in_specs = [_xspec(T, w, col, roff) for (_, w, col, roff) in xs] + [_pspec(p, sel) for (p, sel) in ps]
    n_in = len(args)
    shapes, specs, aliases = _out_plumbing(outs, T, args, in_specs)
    n_all_in = len(args)

    def body(*refs):
        i = pl.program_id(0)
        xv = [r[...] for r in refs[:nx]]
        pv = [r[...] if ps[k][1] is None else r[0] for k, r in enumerate(refs[nx:n_in])]
        res = f(*xv, *pv)
        for k, o_ref in enumerate(refs[n_all_in:]):
            if outs[k][0] == 'acc':
                @pl.when(i == 0)
                def _(o_ref=o_ref, v=res[k]):
                    o_ref[...] = v

                @pl.when(i > 0)
                def _(o_ref=o_ref, v=res[k]):
                    o_ref[...] += v
            else:
                o_ref[...] = res[k].astype(o_ref.dtype)

    return pl.pallas_call(body, name=name, grid=(n,), in_specs=in_specs, out_specs=specs, out_shape=shapes,
                          input_output_aliases=aliases, compiler_params=_cp("arbitrary"))(*args)


def stage_bwd(name, f, T, n, xs, ps, cts, dxs, dps, primal=()):
    nx, npar = len(xs), len(ps)
    args = [x[0] for x in xs] + [p[0] for p in ps]
    in_specs = [_xspec(T, w, col, roff) for (_, w, col, roff) in xs] + [_pspec(p, sel) for (p, sel) in ps]
    ct_arrs = [c for c in cts if isinstance(c, tuple)]
    for (a, w, col, roff) in ct_arrs:
        args.append(a)
        in_specs.append(_xspec(T, w, col, roff))
    n_in = len(args)
    outs, out_of = [], []
    for k, o in enumerate(dxs):
        if o is not None:
            outs.append(o)
            out_of.append(('x', k))
    for k, want in enumerate(dps):
        if want:
            p, sel = ps[k]
            outs.append(('acc', p.shape))
            out_of.append(('p', k))
    for k, shape in primal:
        outs.append(('acc', shape))
        out_of.append(('r', k))
    shapes, specs, aliases = _out_plumbing(outs, T, args, in_specs)
    for j, (kind, k) in enumerate(out_of):
        if kind == 'p' and ps[k][1] is not None:
            p, sel = ps[k]
            specs[j] = pl.BlockSpec((1,) + p.shape[1:], lambda i, n=p.ndim, sel=sel: (sel(i),) + (0,) * (n - 1))
    n_all_in = len(args)

    def body(*refs):
        i = pl.program_id(0)
        xv = [r[...] for r in refs[:nx]]
        pv = [r[...] if ps[k][1] is None else r[0] for k, r in enumerate(refs[nx:nx + npar])]
        res, vjp_fn = jax.vjp(f, *xv, *pv)
        ctv, q = [], nx + npar
        for k, c in enumerate(cts):
            if c is None:
                ctv.append(jnp.zeros_like(res[k]))
            elif isinstance(c, tuple):
                v = refs[q][...]
                if c[3] < 0:
                    v = v * (i + c[3] >= 0).astype(F32)
                ctv.append(v)
                q += 1
            else:
                ctv.append(jnp.full_like(res[k], c))
        grads = vjp_fn(tuple(ctv))
        for j, o_ref in enumerate(refs[n_all_in:]):
            kind, k = out_of[j]
            if kind == 'x':
                o_ref[...] = grads[k].astype(o_ref.dtype)
            else:
                g = res[k] if kind == 'r' else grads[nx + k]
                sel = None if kind == 'r' else ps[k][1]
                if sel is None:
                    first = i == 0
                    tgt = o_ref
                else:
                    first = jnp.logical_or(i == 0, sel(i) != sel(jnp.maximum(i - 1, 0)))
                    tgt = o_ref.at[0]

                @pl.when(first)
                def _(tgt=tgt, g=g):
                    tgt[...] = g

                @pl.when(jnp.logical_not(first))
                def _(tgt=tgt, g=g):
                    tgt[...] += g

    return pl.pallas_call(body, name=name, grid=(n,), in_specs=in_specs, out_specs=specs, out_shape=shapes,
                          input_output_aliases=aliases, compiler_params=_cp("arbitrary"))(*args)


_CONTRACT = {'nn': (1, 0), 'nt': (1, 1), 'tn': (0, 0)}


def matmul(name, a, b, mode, tm, tn, tk, out_dtype=F32, add=None, after=None):
    if mode == 'nn':
        (M, K), (_, N) = a.shape, b.shape
    elif mode == 'nt':
        (M, K), (N, _) = a.shape, b.shape
    else:
        (K, M), (_, N) = a.shape, b.shape
    assert M % tm == 0 and N % tn == 0 and K % tk == 0, (name, M, N, K, tm, tn, tk)
    a_spec = (pl.BlockSpec((tk, tm), lambda j, i, k: (k, i)) if mode == 'tn'
              else pl.BlockSpec((tm, tk), lambda j, i, k: (i, k)))
    b_spec = (pl.BlockSpec((tn, tk), lambda j, i, k: (j, k)) if mode == 'nt'
              else pl.BlockSpec((tk, tn), lambda j, i, k: (k, j)))
    o_spec = pl.BlockSpec((tm, tn), lambda j, i, k: (i, j))
    return matmul_call(name, (N // tn, M // tm, K // tk), a, a_spec, b, b_spec, (M, N), o_spec, (tm, tn), mode,
                       out_dtype, add, after)


def matmul_call(name, grid, a, a_spec, b, b_spec, out_shape, o_spec, tile, mode, out_dtype=F32, add=None, after=None):
    tm, tn = tile
    nk = grid[2]
    ca, cb = _CONTRACT[mode]
    args, in_specs = [a, b], [a_spec, b_spec]
    if add is not None:
        args.append(add)
        in_specs.append(o_spec)
    if after is not None:
        args.append(after)
        in_specs.append(pl.BlockSpec(memory_space=pl.ANY))

    def body(*refs):
        a_ref, b_ref = refs[0], refs[1]
        o_ref, acc = refs[-2], refs[-1]
        k = pl.program_id(2)
        if nk == 1:
            p = _dot(a_ref[...], b_ref[...], ca, cb)
            o_ref[...] = (p + refs[2][...] if add is not None else p).astype(out_dtype)
            return

        @pl.when(k == 0)
        def _():
            acc[...] = refs[2][...] if add is not None else jnp.zeros_like(acc)

        acc[...] += _dot(a_ref[...], b_ref[...], ca, cb)

        @pl.when(k == nk - 1)
        def _():
            o_ref[...] = acc[...].astype(out_dtype)

    return pl.pallas_call(body, name=name, grid=grid, in_specs=in_specs, out_specs=o_spec,
                          out_shape=jax.ShapeDtypeStruct(out_shape, out_dtype),
                          scratch_shapes=[pltpu.VMEM((tm, tn) if nk > 1 else (8, 128), F32)],
                          compiler_params=_cp("arbitrary", "arbitrary", "arbitrary"))(*args)


NS, WS = 4, 704


def _resident(name, M, tm, rows, weight, out_shape, out_block, out_map, step, add=None):
    args = [rows[0], weight] + ([] if add is None else [add])
    in_specs = [pl.BlockSpec(rows[1], rows[2]), pl.BlockSpec(weight.shape, lambda i, n=weight.ndim: (0,) * n)]
    if add is not None:
        in_specs.append(pl.BlockSpec(out_block, out_map))
    return pl.pallas_call(step, name=name, grid=(M // tm,), in_specs=in_specs, out_specs=pl.BlockSpec(out_block, out_map),
                          out_shape=jax.ShapeDtypeStruct(out_shape, F32), compiler_params=_cp("arbitrary"))(*args)


def ffn_in_fwd(name, h, w1, w3, tm):
    M = h.shape[0]

    def step(h_ref, w1_ref, w3_ref, a1_ref, a3_ref, act_ref):
        for s in range(NS):
            a1 = _dot(h_ref[...], w1_ref[s], 1, 0)
            a3 = _dot(h_ref[...], w3_ref[s], 1, 0)
            a1_ref[s] = a1
            a3_ref[s] = a3
            act_ref[s] = (_silu(a1) * a3).astype(act_ref.dtype)

    wspec = pl.BlockSpec((NS, D, WS), lambda i: (0, 0, 0))
    ospec = pl.BlockSpec((NS, tm, WS), lambda i: (0, i, 0))
    return pl.pallas_call(
        step, name=name, grid=(M // tm,), in_specs=[pl.BlockSpec((tm, D), lambda i: (i, 0)), wspec, wspec],
        out_specs=[ospec, ospec, ospec],
        out_shape=[jax.ShapeDtypeStruct((NS, M, WS), F32), jax.ShapeDtypeStruct((NS, M, WS), F32),
                   jax.ShapeDtypeStruct((NS, M, WS), BF16)], compiler_params=_cp("arbitrary"))(h, w1, w3)


def ffn_out_bwd_x(name, dff, w2, a1, a3, tm):
    M = dff.shape[0]

    def step(d_ref, w_ref, a1_ref, a3_ref, da1_ref, da3_ref):
        for s in range(NS):
            dact = _dot(d_ref[...], w_ref[s * WS:(s + 1) * WS, :], 1, 1)
            a1 = a1_ref[s]
            sig = jax.nn.sigmoid(a1)
            da3_ref[s] = (dact * (a1 * sig)).astype(da3_ref.dtype)
            da1_ref[s] = (dact * a3_ref[s] * (sig * (1.0 + a1 * (1.0 - sig)))).astype(da1_ref.dtype)

    aspec = pl.BlockSpec((NS, tm, WS), lambda i: (0, i, 0))
    return pl.pallas_call(
        step, name=name, grid=(M // tm,),
        in_specs=[pl.BlockSpec((tm, D), lambda i: (i, 0)), pl.BlockSpec(w2.shape, lambda i: (0, 0)), aspec, aspec],
        out_specs=[aspec, aspec],
        out_shape=[jax.ShapeDtypeStruct((NS, M, WS), BF16)] * 2, compiler_params=_cp("arbitrary"))(dff, w2, a1, a3)


def ff_in_bwd_x(name, da3, w3, tm, add=None):
    M = da3.shape[1]

    def step(*refs):
        d_ref, w_ref, o_ref = refs[0], refs[1], refs[-1]
        acc = _dot(d_ref[0], w_ref[0], 1, 1)
        for s in range(1, NS):
            acc = acc + _dot(d_ref[s], w_ref[s], 1, 1)
        o_ref[...] = acc if add is None else acc + refs[2][...]

    return _resident(name, M, tm, (da3, (NS, tm, WS), lambda i: (0, i, 0)), w3, (M, D), (tm, D), lambda i: (i, 0), step, add)


def ff_in_bwd_w(name, h, da3, tk):
    M = h.shape[0]
    return matmul_call(name, (NS, 1, M // tk), h, pl.BlockSpec((tk, D), lambda j, i, k: (k, 0)),
                       da3, pl.BlockSpec((None, tk, WS), lambda j, i, k: (j, k, 0)),
                       (NS, D, WS), pl.BlockSpec((None, D, WS), lambda j, i, k: (j, 0, 0)), (D, WS), 'tn', BF16)


def ff_out_fwd(name, act3, w2, tm):
    M = act3.shape[1]

    def step(a_ref, w_ref, o_ref):
        acc = _dot(a_ref[0], w_ref[0:WS, :], 1, 0)
        for s in range(1, NS):
            acc = acc + _dot(a_ref[s], w_ref[s * WS:(s + 1) * WS, :], 1, 0)
        o_ref[...] = acc

    return _resident(name, M, tm, (act3, (NS, tm, WS), lambda i: (0, i, 0)), w2, (M, D), (tm, D), lambda i: (i, 0), step)


def ff_out_bwd_w(name, act3, dff, tk):
    M = dff.shape[0]
    return matmul_call(name, (1, NS, M // tk), act3, pl.BlockSpec((None, tk, WS), lambda j, i, k: (i, k, 0)),
                       dff, pl.BlockSpec((tk, D), lambda j, i, k: (k, 0)),
                       (NS * WS, D), pl.BlockSpec((WS, D), lambda j, i, k: (i, 0)), (WS, D), 'tn', BF16)


def _shift_rows(x, d):
    n = x.shape[0]
    if d == 0:
        return x
    y = pltpu.roll(x, (-d) % n, 0)
    t = lax.broadcasted_iota(jnp.int32, x.shape, 0)
    ok = (t + d >= 0) & (t + d < n)
    return jnp.where(ok, y, 0.0)


def _conv_pre(x, w_ref, b_ref):
    acc = jnp.broadcast_to(b_ref[...], x.shape)
    for k in range(5):
        acc = acc + _shift_rows(x, k - 2) * w_ref[k:k + 1, :]
    return acc


def conv_fwd(name, proj, conv_w, conv_b, R):
    segs = ((0, LC), (LC, R))

    def body(x_ref, w_ref, b_ref, o_ref):
        for (s, e) in segs:
            pre = _conv_pre(x_ref[s:e, :], w_ref, b_ref)
            o_ref[s:e, :] = _silu(pre)

    return pl.pallas_call(
        body, name=name, grid=(12,),
        in_specs=[pl.BlockSpec((R, 128), lambda j: (0, C_XBC // 128 + j)),
                  pl.BlockSpec((8, 128), lambda j: (0, j)), pl.BlockSpec((1, 128), lambda j: (0, j))],
        out_specs=pl.BlockSpec((R, 128), lambda j: (0, j)),
        out_shape=jax.ShapeDtypeStruct((R, 1536), F32), compiler_params=_cp("arbitrary"))(proj, conv_w, conv_b)


def conv_bwd(name, proj, conv_w, conv_b, d_f, d_b, d_skip, dproj, R):
    segs = ((0, LC), (LC, R))

    def body(x_ref, w_ref, b_ref, df_ref, db_ref, ds_ref, _, dx_ref, dw_ref, dbias_ref):
        j = pl.program_id(0)
        has_skip = (j < 8).astype(F32)
        dw = [jnp.zeros((1, 128), F32) for _ in range(5)]
        dbias = jnp.zeros((1, 128), F32)
        for (s, e) in segs:
            x = x_ref[s:e, :]
            pre = _conv_pre(x, w_ref, b_ref)
            sig = jax.nn.sigmoid(pre)
            dy = df_ref[s:e, :] + db_ref[s:e, :]
            if s == LC:
                dy = dy + ds_ref[...] * has_skip
            dpre = dy * (sig * (1.0 + pre * (1.0 - sig)))
            dx = jnp.zeros_like(x)
            for k in range(5):
                dx = dx + _shift_rows(dpre, 2 - k) * w_ref[k:k + 1, :]
                dw[k] = dw[k] + jnp.sum(dpre * _shift_rows(x, k - 2), axis=0, keepdims=True)
            dbias = dbias + jnp.sum(dpre, axis=0, keepdims=True)
            dx_ref[s:e, :] = dx.astype(dx_ref.dtype)
        dw_ref[...] = jnp.zeros_like(dw_ref)
        for k in range(5):
            dw_ref[k:k + 1, :] = dw[k]
        dbias_ref[...] = dbias

    return pl.pallas_call(
        body, name=name, grid=(12,),
        in_specs=[pl.BlockSpec((R, 128), lambda j: (0, C_XBC // 128 + j)),
                  pl.BlockSpec((8, 128), lambda j: (0, j)), pl.BlockSpec((1, 128), lambda j: (0, j)),
                  pl.BlockSpec((R, 128), lambda j: (0, j)), pl.BlockSpec((R, 128), lambda j: (0, j)),
                  pl.BlockSpec((R - LC, 128), lambda j: (0, jnp.minimum(j, 7))),
                  pl.BlockSpec(memory_space=pl.ANY)],
        out_specs=[pl.BlockSpec((R, 128), lambda j: (0, C_XBC // 128 + j)),
                   pl.BlockSpec((8, 128), lambda j: (0, j)), pl.BlockSpec((1, 128), lambda j: (0, j))],
        out_shape=[jax.ShapeDtypeStruct(dproj.shape, dproj.dtype), jax.ShapeDtypeStruct((8, 1536), F32),
                   jax.ShapeDtypeStruct((1, 1536), F32)],
        input_output_aliases={6: 0}, compiler_params=_cp("arbitrary"))(proj, conv_w, conv_b, d_f, d_b, d_skip, dproj)


def _ssd_chunk(rev, dirn):
    cums = _cumsum_fn(rev)

    def f(xs, Bs, Cs, dt, alog, Hs):
        lane = lax.broadcasted_iota(jnp.int32, (1, 128), 1)
        sub = lax.broadcasted_iota(jnp.int32, (Q, 1), 0)
        r = lax.broadcasted_iota(jnp.int32, (Q, Q), 0)
        c = lax.broadcasted_iota(jnp.int32, (Q, Q), 1)
        mask = (r <= c) if rev else (r >= c)
        left = lane < 64
        a = dt * (-jnp.exp(alog))
        s = cums(a)
        sT, dtT = s.T, dt.T
        last_row = (sub == (0 if rev else Q - 1)).astype(F32)
        s_last = jnp.sum(s * last_row, axis=0, keepdims=True)
        G = [mm_nt(Cs[g], Bs[g]) for g in range(2)]
        M, es, wc, ed = [], [], [], []
        for h in range(NH):
            l = 16 * dirn + h
            oh_l = (lane == l).astype(F32)
            oh_s = (sub == l).astype(F32)
            s_col = jnp.sum(s * oh_l, axis=1, keepdims=True)
            dt_col = jnp.sum(dt * oh_l, axis=1, keepdims=True)
            s_row = jnp.sum(sT * oh_s, axis=0, keepdims=True)
            dt_row = jnp.sum(dtT * oh_s, axis=0, keepdims=True)
            sl = jnp.sum(s_last * oh_l, axis=1, keepdims=True)
            seg = jnp.where(mask, s_col - s_row, 0.0)
            lm = jnp.where(mask, jnp.exp(seg), 0.0)
            M.append(G[h // 8] * lm * dt_row)
            es.append(jnp.exp(s_col))
            wc.append(jnp.exp(sl - s_col) * dt_col)
            ed.append(jnp.exp(sl))
        Ys, Hn = [], []
        for j in range(8):
            g = j // 4
            xa = jnp.where(left, xs[j], 0.0)
            xb = jnp.where(left, 0.0, xs[j])
            yd = mm(M[2 * j], xa) + mm(M[2 * j + 1], xb)
            yo = mm(Cs[g], Hs[j]) * jnp.where(left, es[2 * j], es[2 * j + 1])
            Ys.append(yd + yo)
            st = mm_tn(Bs[g], xs[j] * jnp.where(left, wc[2 * j], wc[2 * j + 1]))
            Hn.append(Hs[j] * jnp.where(left, ed[2 * j], ed[2 * j + 1]) + st)
        return Ys, Hn

    return f


def _chunk_of(t, n, rev):
    if not rev:
        return t
    return jnp.where(t < 2, 1 - t, n + 1 - t)


def ssd_fwd(name, xbc, dt, alog, n, rev, dirn):
    chunk = _ssd_chunk(rev, dirn)

    def body(x_ref, b_ref, c_ref, dt_ref, al_ref, y_ref, hs_ref, h_scr):
        @pl.when(pl.program_id(0) == 0)
        def _():
            h_scr[...] = jnp.zeros_like(h_scr)

        xs = [x_ref[:, 128 * j:128 * (j + 1)] for j in range(8)]
        Bs = [b_ref[:, 128 * g:128 * (g + 1)] for g in range(2)]
        Cs = [c_ref[:, 128 * g:128 * (g + 1)] for g in range(2)]
        Hs = [h_scr[:, 128 * j:128 * (j + 1)] for j in range(8)]
        hs_ref[0] = h_scr[...]
        Ys, Hn = chunk(xs, Bs, Cs, dt_ref[...], al_ref[...], Hs)
        for j in range(8):
            y_ref[:, 128 * j:128 * (j + 1)] = Ys[j]
            h_scr[:, 128 * j:128 * (j + 1)] = Hn[j]

    cm = lambda t: _chunk_of(t, n, rev)
    return pl.pallas_call(
        body, name=name, grid=(n,),
        in_specs=[pl.BlockSpec((Q, 1024), lambda t: (cm(t), 0)), pl.BlockSpec((Q, 256), lambda t: (cm(t), 4)),
                  pl.BlockSpec((Q, 256), lambda t: (cm(t), 5)), pl.BlockSpec((Q, 128), lambda t: (cm(t), 0)),
                  pl.BlockSpec((1, 128), lambda t: (0, 0))],
        out_specs=[pl.BlockSpec((Q, 1024), lambda t: (cm(t), 0)), pl.BlockSpec((1, Q, 1024), lambda t: (cm(t), 0, 0))],
        out_shape=[jax.ShapeDtypeStruct((n * Q, 1024), F32), jax.ShapeDtypeStruct((n, Q, 1024), F32)],
        scratch_shapes=[pltpu.VMEM((Q, 1024), F32)], compiler_params=_cp("arbitrary"))(xbc, xbc, xbc, dt, alog)


def ssd_bwd(name, xbc, dt, alog, hs, dy, n, rev, dirn):
    chunk = _ssd_chunk(rev, dirn)

    def body(x_ref, b_ref, c_ref, dt_ref, al_ref, hs_ref, dy_ref, dx_ref, ddt_ref, dal_ref, dh_scr):
        tt = pl.program_id(0)
        ch = _chunk_of(n - 1 - tt, n, rev)

        @pl.when(tt == 0)
        def _():
            dh_scr[...] = jnp.zeros_like(dh_scr)

        xs = [x_ref[:, 128 * j:128 * (j + 1)] for j in range(8)]
        Bs = [b_ref[:, 128 * g:128 * (g + 1)] for g in range(2)]
        Cs = [c_ref[:, 128 * g:128 * (g + 1)] for g in range(2)]
        Hs = [hs_ref[0, :, 128 * j:128 * (j + 1)] for j in range(8)]
        live = (ch >= 2).astype(F32)
        dYs = [dy_ref[:, 128 * j:128 * (j + 1)] * live for j in range(8)]
        dHn = [dh_scr[:, 128 * j:128 * (j + 1)] for j in range(8)]
        _, vjp_fn = jax.vjp(chunk, xs, Bs, Cs, dt_ref[...], al_ref[...], Hs)
        dxs, dBs, dCs, ddt, dal, dHs = vjp_fn((dYs, dHn))
        for j in range(8):
            dx_ref[:, 128 * j:128 * (j + 1)] = dxs[j]
            dh_scr[:, 128 * j:128 * (j + 1)] = dHs[j]
        for g in range(2):
            dx_ref[:, 1024 + 128 * g:1024 + 128 * (g + 1)] = dBs[g]
            dx_ref[:, 1280 + 128 * g:1280 + 128 * (g + 1)] = dCs[g]
        ddt_ref[...] = ddt

        @pl.when(tt == 0)
        def _():
            dal_ref[...] = dal

        @pl.when(tt > 0)
        def _():
            dal_ref[...] += dal

    cm = lambda t: _chunk_of(n - 1 - t, n, rev)
    return pl.pallas_call(
        body, name=name, grid=(n,),
        in_specs=[pl.BlockSpec((Q, 1024), lambda t: (cm(t), 0)), pl.BlockSpec((Q, 256), lambda t: (cm(t), 4)),
                  pl.BlockSpec((Q, 256), lambda t: (cm(t), 5)), pl.BlockSpec((Q, 128), lambda t: (cm(t), 0)),
                  pl.BlockSpec((1, 128), lambda t: (0, 0)), pl.BlockSpec((1, Q, 1024), lambda t: (cm(t), 0, 0)),
                  pl.BlockSpec((Q, 1024), lambda t: (jnp.maximum(cm(t) - 2, 0), 0))],
        out_specs=[pl.BlockSpec((Q, 1536), lambda t: (cm(t), 0)), pl.BlockSpec((Q, 128), lambda t: (cm(t), 0)),
                   pl.BlockSpec((1, 128), lambda t: (0, 0))],
        out_shape=[jax.ShapeDtypeStruct((n * Q, 1536), F32), jax.ShapeDtypeStruct((n * Q, 128), F32),
                   jax.ShapeDtypeStruct((1, 128), F32)],
        scratch_shapes=[pltpu.VMEM((Q, 1024), F32)], compiler_params=_cp("arbitrary"))(xbc, xbc, xbc, dt, alog, hs, dy)


def f_norm0(x, g0, b0, sc, sh):
    x0 = _ln(x, g0, b0)
    return x0, x0 * (1.0 + sc) + sh


def f_dt(raw, bias):
    z = split4(raw)[0] + bias
    dt = jnp.maximum(z, 0.0) + jnp.log1p(jnp.exp(-jnp.abs(z)))
    return dt, dt


def f_gated_norm(yf, yb, xs, z, dcol, g):
    h = (yf + yb + xs * dcol) * _silu(z)
    return (h * lax.rsqrt(jnp.mean(h * h, axis=-1, keepdims=True) + LN_EPS) * g,)


def f_gmlp(uv, gmg, gmb, *wb):
    ws, bs = wb[:8], wb[8:]
    u, v = split2(uv)
    vn = split8(_ln(_gelu(v), gmg, gmb))
    mixed = concat8(tuple(mm(ws[g], vn[g]) + bs[g] for g in range(8)))
    return (_gelu(u) * mixed,)


def f_merge(ps, pg, gates, bg):
    gs, gg = split2(jax.nn.sigmoid(gates + bg))
    return (gs * ps + gg * pg,)


def f_res1(x0, out, g1, lg, lb, sc, sh):
    x1 = _ln(ALPHA * x0 + g1 * out, lg, lb)
    return x1, x1 * (1.0 + sc) + sh


def f_res2_loss(x1, ff, tgt, g2, lg, lb):
    x2 = _ln(ALPHA * x1 + g2 * ff, lg, lb)
    e = x2 - tgt
    return (0.5 * jnp.sum(jnp.mean(e * e, axis=-1, keepdims=True), axis=0, keepdims=True),)


def _row_tile(M):
    return 544 if M % 544 == 0 else (512 if M % 512 == 0 else M)


def _tie(v, token):
    return v if token is None else v + token[0:1, 0:1]


def core(x_all, tgt, mod_x, mod_c, X, S):
    R = x_all.shape[0]
    L = R - LC
    n = R // Q
    T = 256
    nt, ntl = R // T, L // T
    tmR, tmL = _row_tile(R), _row_tile(L)
    tkR = 256 if R % 512 else 512
    tkL = 512 if L % 512 == 0 else 256
    row = lambda v: v.reshape(1, -1)
    mx = [row(mod_x[k]) for k in range(6)]
    mc = [row(mod_c[k]) for k in range(6)]
    sel = lambda i: jnp.minimum(i, 1)
    sc1 = jnp.stack([mc[1], mx[1]])
    sh1 = jnp.stack([mc[0], mx[0]])
    ln0 = [(row(S['ln0_g']), None), (row(S['ln0_b']), None), (sc1, sel), (sh1, sel)]

    x0, xm = stage_fwd("norm0_fwd", f_norm0, T, nt, [(x_all, D, 0, 0)], ln0, [('new', R, D, 0), ('new', R, D, 0, BF16)])
    w_in, tok = X.w_in()
    proj = matmul("proj_fwd", xm, w_in, 'nn', tmR, PW // 2, 1024, after=tok)
    conv_w8 = jnp.pad(S['conv_w'], ((0, 3), (0, 0)))
    conv_b = row(S['conv_b'])
    xbc = conv_fwd("conv_fwd", proj, conv_w8, conv_b, R)
    dt_bias = jnp.pad(S['dt_bias'].reshape(1, 32), ((0, 0), (0, 96)))
    alog = jnp.pad(S['a_log'].reshape(1, 32), ((0, 0), (0, 96)))
    x_dt = [(proj, 512, C_DT // 512, 0)]
    dt_f, dt_b = stage_fwd("dt_fwd", f_dt, T, nt, x_dt, [(dt_bias, None)], [('new', R, 128, 0), ('new', R, 128, 0)])
    y_f, hs_f = ssd_fwd("ssd_fwd_f", xbc, dt_f, alog, n, False, 0)
    y_b, hs_b = ssd_fwd("ssd_fwd_b", xbc, dt_b, alog, n, True, 1)
    W = X.rest(hs_b)
    dcol = jnp.repeat(S['d_skip'][0] + S['d_skip'][1], 64).reshape(1, D)
    x_gn = [(y_f, D, 0, 1), (y_b, D, 0, 1), (xbc, D, 0, 1), (proj, D, C_Z // D, 1)]
    p_gn = [(dcol, None), (row(S['ssd_norm_g']), None)]
    (yn,) = stage_fwd("gnorm_fwd", f_gated_norm, T, ntl, x_gn, p_gn, [('new', L, D, 0, BF16)])
    x_gm = [(proj, 2 * D, C_UV // (2 * D), LC // Q)]
    p_gm = ([(row(S['gm_norm_g']), None), (row(S['gm_norm_b']), None)]
            + [(S['w_spatial'][g], None) for g in range(8)] + [(S['b_spatial'][g].reshape(Q, 1), None) for g in range(8)])
    (y_gm,) = stage_fwd("gmlp_fwd", f_gmlp, Q, L // Q, x_gm, p_gm, [('new', L, D, 0, BF16)])
    p_ssd = matmul("pssd_fwd", yn, W['w_ssd_proj'], 'nn', tmL, 1024, 1024)
    p_g = matmul("pgm_fwd", y_gm, W['w_gm_proj'], 'nn', tmL, 1024, 1024)
    x_mg = [(p_ssd, D, 0, 0), (p_g, D, 0, 0), (proj, 2 * D, C_GATE // (2 * D), 1)]
    p_mg = [(row(S['b_gate']), None)]
    (merged,) = stage_fwd("merge_fwd", f_merge, T, ntl, x_mg, p_mg, [('new', L, D, 0, BF16)])
    out = matmul("out_fwd", merged, W['w_out'], 'nn', tmL, 1024, 1024)
    x_r1 = [(x0, D, 0, 1), (out, D, 0, 0)]
    p_r1 = [(mx[2], None), (row(S['ln1_g']), None), (row(S['ln1_b']), None), (mx[4], None), (mx[3], None)]
    x1, hm = stage_fwd("res1_fwd", f_res1, T, ntl, x_r1, p_r1, [('new', L, D, 0), ('new', L, D, 0, BF16)])
    a1, a3, act = ffn_in_fwd("ffn_in_fwd", hm, W['w_ff1'], W['w_ff3'], T)
    ff = ff_out_fwd("ff2_fwd", act, W['w_ff2'], tmL)
    x_r2 = [(x1, D, 0, 0), (ff, D, 0, 0), (tgt, D, 0, 0)]
    p_r2 = [(mx[5], None), (row(S['ln2_g']), None), (row(S['ln2_b']), None)]

    dx1_a, dff, dg2, dl2g, dl2b, loss = stage_bwd(
        "res2_bwd", f_res2_loss, T, ntl, x_r2, p_r2, [1.0],
        [('new', L, D, 0), ('new', L, D, 0, BF16), None], [True, True, True], primal=[(0, (1, 1))])
    da1, da3 = ffn_out_bwd_x("ffn_out_bwd_x", dff, W['w_ff2'], a1, a3, T)
    gw_ff2 = ff_out_bwd_w("ff2_bwd_w", act, dff, tkL)
    dhm = ff_in_bwd_x("ff1_bwd_x", da1, W['w_ff1'], tmL)
    dhm = ff_in_bwd_x("ff3_bwd_x", da3, W['w_ff3'], tmL, add=dhm)
    gw_ff1 = ff_in_bwd_w("ff1_bwd_w", hm, da1, tkL)
    gw_ff3 = ff_in_bwd_w("ff3_bwd_w", hm, da3, tkL)
    tok = X.grads('ffn', {'w_ff2': gw_ff2, 'w_ff1': gw_ff1, 'w_ff3': gw_ff3})
    p_r1 = [(_tie(p_r1[0][0], tok), None)] + p_r1[1:]
    dx0_a, dout, dg1, dl1g, dl1b, dsc2, dsh2 = stage_bwd(
        "res1_bwd", f_res1, T, ntl, x_r1, p_r1, [(dx1_a, D, 0, 0), (dhm, D, 0, 0)],
        [('new', L, D, 0), ('new', L, D, 0, BF16)], [True] * 5)
    dmerged = matmul("out_bwd_x", dout, W['w_out'], 'nt', tmL, 1024, 1024)
    gw_out = matmul("out_bwd_w", merged, dout, 'tn', 1024, 1024, tkL, BF16)
    lt, lq = -(LC // T), -(LC // Q)
    x_mg_b = [(p_ssd, D, 0, lt), (p_g, D, 0, lt), (proj, 2 * D, C_GATE // (2 * D), 0)]
    dp_ssd, dp_g, dproj, dbg = stage_bwd(
        "merge_bwd", f_merge, T, nt, x_mg_b, p_mg, [(dmerged, D, 0, lt)],
        [('new', L, D, lt, BF16), ('new', L, D, lt, BF16), ('part', R, PW, 2 * D, C_GATE // (2 * D), 0, BF16)], [True])
    dyn = matmul("pssd_bwd_x", dp_ssd, W['w_ssd_proj'], 'nt', tmL, 1024, 1024)
    gw_ssd = matmul("pssd_bwd_w", yn, dp_ssd, 'tn', 1024, 1024, tkL, BF16)
    dy_gm = matmul("pgm_bwd_x", dp_g, W['w_gm_proj'], 'nt', tmL, 1024, 1024)
    gw_gm = matmul("pgm_bwd_w", y_gm, dp_g, 'tn', 1024, 1024, tkL, BF16)
    tok = X.grads('proj', {'w_out': gw_out, 'w_ssd_proj': gw_ssd, 'w_gm_proj': gw_gm})
    p_gm = [(_tie(p_gm[0][0], tok), None)] + p_gm[1:]
    r_gm = stage_bwd("gmlp_bwd", f_gmlp, Q, n, [(proj, 2 * D, C_UV // (2 * D), 0)], p_gm, [(dy_gm, D, 0, lq)],
                     [('alias', dproj, 2 * D, C_UV // (2 * D), 0)], [True] * 18)
    dproj, dgmg, dgmb, dws, dbs = r_gm[0], r_gm[1], r_gm[2], r_gm[3:11], r_gm[11:19]
    x_gn_b = [(y_f, D, 0, 0), (y_b, D, 0, 0), (xbc, D, 0, 0), (proj, D, C_Z // D, 0)]
    dy, dskipx, dproj, ddcol, dng = stage_bwd(
        "gnorm_bwd", f_gated_norm, T, nt, x_gn_b, p_gn, [(dyn, D, 0, lt)],
        [('new', L, D, lt), None, ('new', L, D, lt), ('alias', dproj, D, C_Z // D, 0)], [True, True])
    dxbc_f, ddt_f, dal_f = ssd_bwd("ssd_bwd_f", xbc, dt_f, alog, hs_f, dy, n, False, 0)
    dxbc_b, ddt_b, dal_b = ssd_bwd("ssd_bwd_b", xbc, dt_b, alog, hs_b, dy, n, True, 1)
    dproj, ddtb = stage_bwd("dt_bwd", f_dt, T, nt, x_dt, [(dt_bias, None)],
                            [(ddt_f, 128, 0, 0), (ddt_b, 128, 0, 0)],
                            [('alias', dproj, 512, C_DT // 512, 0)], [True])
    dproj, dcw8, dcb = conv_bwd("conv_bwd", proj, conv_w8, conv_b, dxbc_f, dxbc_b, dskipx, dproj, R)
    gw_in = matmul("proj_bwd_w", xm, dproj, 'tn', 1024, PW // 4, tkR, BF16)
    tok = X.grads('in', {'w_in': gw_in})
    dxm = matmul("proj_bwd_x", dproj, w_in, 'nt', R // 4 if R % 32 == 0 else R, 1024, 1024, after=tok)
    grad_x, dl0g, dl0b, dsc1, dsh1 = stage_bwd(
        "norm0_bwd", f_norm0, T, nt, [(x_all, D, 0, 0)], ln0, [(dx0_a, D, 0, -1), (dxm, D, 0, 0)],
        [('new', L, D, -1)], [True] * 4)

    zero = jnp.zeros((D,), F32)
    flat = lambda v: v.reshape(-1)
    small = {
        'loss': flat(loss), 'ln0_g': flat(dl0g), 'ln0_b': flat(dl0b),
        'dmod_x': jnp.concatenate([flat(dsh1[1]), flat(dsc1[1]), flat(dg1), flat(dsh2), flat(dsc2), flat(dg2)]),
        'dmod_c': jnp.concatenate([flat(dsh1[0]), flat(dsc1[0]), zero, zero, zero, zero]),
        'conv_w': flat(dcw8[:5]), 'conv_b': flat(dcb), 'dt_bias': flat(ddtb[:, :32]),
        'a_log': flat((dal_f + dal_b)[:, :32]),
        'd_skip': flat(jnp.tile(ddcol.reshape(1, NH, 64).sum(-1), (2, 1))),
        'ssd_norm_g': flat(dng), 'gm_norm_g': flat(dgmg), 'gm_norm_b': flat(dgmb),
        'w_spatial': flat(jnp.stack(dws)), 'b_spatial': flat(jnp.stack(dbs)), 'b_gate': flat(dbg),
        'ln1_g': flat(dl1g), 'ln1_b': flat(dl1b), 'ln2_g': flat(dl2g), 'ln2_b': flat(dl2b),
    }
    return grad_x, small


def _place():
    return lax.axis_index("x"), lax.axis_index("y"), lax.axis_index("c")


def allgather8(name, blk, hbm):
    space = pl.ANY if hbm else pltpu.VMEM

    def body(x_ref, out_ref, send_sems, recv_sems, local_sem):
        x, y, c = _place()
        me, sibling = (x, y, c), (x, y, 1 - c)
        chips = [(1 - x, y), (x, 1 - y), (1 - x, 1 - y)]

        def slot(px, py, pc):
            return out_ref.at[4 * px + 2 * py + pc]

        def copy(k, block, to, src=None):
            return pltpu.make_async_remote_copy(
                src_ref=slot(*block) if src is None else src, dst_ref=slot(*block),
                send_sem=send_sems.at[k], recv_sem=recv_sems.at[k], device_id=to, device_id_type=MESH)

        mine = pltpu.make_async_copy(x_ref, slot(*me), local_sem)
        mine.start()
        first = [copy(0, me, sibling, src=x_ref)]
        first += [copy(1 + j, me, (*chip, c), src=x_ref) for j, chip in enumerate(chips)]
        for cp in first:
            cp.start()
        passed = [copy(4 + j, (*chip, c), sibling) for j, chip in enumerate(chips)]
        for j, chip in enumerate(chips):
            copy(1 + j, (*chip, c), me).wait_recv()
            passed[j].start()
        copy(0, sibling, me).wait_recv()
        for j, chip in enumerate(chips):
            copy(4 + j, (*chip, 1 - c), me).wait_recv()
        for cp in first + passed:
            cp.wait_send()
        mine.wait()

    return pl.pallas_call(
        body, name=name, out_shape=jax.ShapeDtypeStruct((8,) + blk.shape, blk.dtype),
        in_specs=[pl.BlockSpec(memory_space=space)], out_specs=pl.BlockSpec(memory_space=space),
        scratch_shapes=[pltpu.SemaphoreType.DMA((7,)), pltpu.SemaphoreType.DMA((7,)), pltpu.SemaphoreType.DMA],
        compiler_params=pltpu.CompilerParams(vmem_limit_bytes=VMEM_LIMIT_V7X))(blk)


def gather_weights(name, blks):
    n = len(blks)

    def body(*refs):
        ins, outs = refs[:n], refs[n:2 * n]
        send_sems, recv_sems, local_sems = refs[2 * n:]
        x, y, c = _place()
        me, sibling = (x, y, c), (x, y, 1 - c)
        chips = [(1 - x, y), (x, 1 - y), (1 - x, 1 - y)]

        def copy(a, k, block, to, own=False):
            dst = outs[a].at[4 * block[0] + 2 * block[1] + block[2]]
            return pltpu.make_async_remote_copy(
                src_ref=ins[a] if own else dst, dst_ref=dst, send_sem=send_sems.at[7 * a + k],
                recv_sem=recv_sems.at[7 * a + k], device_id=to, device_id_type=MESH)

        mine = [pltpu.make_async_copy(ins[a], outs[a].at[4 * x + 2 * y + c], local_sems.at[a]) for a in range(n)]
        for cp in mine:
            cp.start()
        first = []
        for j, chip in enumerate(chips):
            first += [copy(a, 1 + j, me, (*chip, c), own=True) for a in range(n)]
        first += [copy(a, 0, me, sibling, own=True) for a in range(n)]
        for cp in first:
            cp.start()
        passed = []
        for a in range(n):
            for j, chip in enumerate(chips):
                copy(a, 1 + j, (*chip, c), me).wait_recv()
                passed.append(copy(a, 4 + j, (*chip, c), sibling))
                passed[-1].start()
        for a in range(n):
            copy(a, 0, sibling, me).wait_recv()
            for j, chip in enumerate(chips):
                copy(a, 4 + j, (*chip, 1 - c), me).wait_recv()
        for cp in first + passed:
            cp.wait_send()
        for cp in mine:
            cp.wait()

    any_spec = pl.BlockSpec(memory_space=pl.ANY)
    return pl.pallas_call(
        body, name=name, out_shape=[jax.ShapeDtypeStruct((8,) + b.shape, b.dtype) for b in blks],
        in_specs=[any_spec] * n, out_specs=[any_spec] * n,
        scratch_shapes=[pltpu.SemaphoreType.DMA((7 * n,)), pltpu.SemaphoreType.DMA((7 * n,)),
                        pltpu.SemaphoreType.DMA((n,))])(*blks)


_HBM = pl.BlockSpec(memory_space=pltpu.HBM)
_SEM = pl.BlockSpec(memory_space=pltpu.SEMAPHORE)
_DATAFLOW = pltpu.SideEffectType.DATAFLOW_SIDE_EFFECTING


def _peers(place):
    x, y, c = place
    return [((1 - x) if k & 4 else x, (1 - y) if k & 2 else y, (1 - c) if k & 1 else c) for k in range(1, 8)]


def _slot(p):
    return 4 * p[0] + 2 * p[1] + p[2]


def plan_gather(place, srcs, lands):
    remote = [(s, l.at[_slot(place)], to) for s, l in zip(srcs, lands) for to in _peers(place)]
    return remote, [(s, l.at[_slot(place)]) for s, l in zip(srcs, lands)]


def plan_to_owner(place, srcs, lands):
    remote = [(s.at[2 * to[0] + to[1], to[2]], l.at[_slot(place)], to) for s, l in zip(srcs, lands) for to in _peers(place)]
    x, y, c = place
    return remote, [(s.at[2 * x + y, c], l.at[_slot(place)]) for s, l in zip(srcs, lands)]


def _hbm(a):
    return pltpu.with_memory_space_constraint(a, pltpu.HBM)


def split_start(name, srcs, land_shapes, plan, after):
    n = len(srcs)

    def body(*refs):
        src, land = refs[:n], refs[n:2 * n]
        send_sems, recv_sems = refs[2 * n + 1], refs[2 * n + 2]
        token, local_sems = refs[-2], refs[-1]
        remote, local = plan(_place(), src, land)
        for k, (s, d, to) in enumerate(remote):
            pltpu.make_async_remote_copy(src_ref=s, dst_ref=d, send_sem=send_sems.at[k], recv_sem=recv_sems.at[k],
                                         device_id=to, device_id_type=MESH).start()
        cps = [pltpu.make_async_copy(s, d, local_sems.at[a]) for a, (s, d) in enumerate(local)]
        for cp in cps:
            cp.start()
        for cp in cps:
            cp.wait()
        token[...] = jnp.zeros_like(token)

    lands = [lax.empty(s.shape, s.dtype) for s in land_shapes]
    thru = [pltpu.HBM(a.shape, a.dtype) for a in list(srcs) + lands]
    res = pl.pallas_call(
        body, name=name,
        out_shape=(pltpu.SemaphoreType.DMA((7 * n,)), pltpu.SemaphoreType.DMA((7 * n,)), *thru,
                   jax.ShapeDtypeStruct((8, 128), F32)),
        in_specs=[_HBM] * (2 * n) + [pl.BlockSpec(memory_space=pl.ANY)],
        out_specs=(_SEM, _SEM, *([_HBM] * (2 * n)), pl.BlockSpec(memory_space=pltpu.VMEM)),
        input_output_aliases={i: 2 + i for i in range(2 * n)},
        scratch_shapes=[pltpu.SemaphoreType.DMA((n,))],
        compiler_params=pltpu.CompilerParams(has_side_effects=_DATAFLOW),
    )(*[_hbm(a) for a in srcs], *[_hbm(a) for a in lands], after)
    return (n, plan, res[0], res[1], res[2:2 + 2 * n]), res[-1]


def split_wait(name, state, after):
    n, plan, send_sems, recv_sems, thru = state

    def body(*refs):
        src, land = refs[:n], refs[n:2 * n]
        send_ref, recv_ref = refs[2 * n], refs[2 * n + 1]
        remote, _ = plan(_place(), src, land)
        for k, (s, d, to) in enumerate(remote):
            cp = pltpu.make_async_remote_copy(src_ref=s, dst_ref=d, send_sem=send_ref.at[k], recv_sem=recv_ref.at[k],
                                              device_id=to, device_id_type=MESH)
            cp.wait_send()
            cp.wait_recv()

    res = pl.pallas_call(
        body, name=name, out_shape=tuple(pltpu.HBM(a.shape, a.dtype) for a in thru),
        in_specs=[_HBM] * (2 * n) + [_SEM, _SEM, pl.BlockSpec(memory_space=pl.ANY)], out_specs=tuple([_HBM] * (2 * n)),
        input_output_aliases={i: i for i in range(2 * n)},
        compiler_params=pltpu.CompilerParams(has_side_effects=_DATAFLOW),
    )(*thru, send_sems, recv_sems, after)
    return res[n:]


def sibling_pair(name, hs):
    n = len(hs)

    def body(*refs):
        ins, outs = refs[:n], refs[n:2 * n]
        send_sems, recv_sems = refs[2 * n:]
        x, y, c = _place()
        cps = [pltpu.make_async_remote_copy(src_ref=outs[a].at[c], dst_ref=outs[a].at[c], send_sem=send_sems.at[a],
                                            recv_sem=recv_sems.at[a], device_id=(x, y, 1 - c), device_id_type=MESH)
               for a in range(n)]
        for cp in cps:
            cp.start()
        for a in range(n):
            pltpu.make_async_remote_copy(src_ref=outs[a].at[1 - c], dst_ref=outs[a].at[1 - c], send_sem=send_sems.at[a],
                                         recv_sem=recv_sems.at[a], device_id=(x, y, 1 - c),
                                         device_id_type=MESH).wait_recv()
        for cp in cps:
            cp.wait_send()

    any_spec = pl.BlockSpec(memory_space=pl.ANY)
    return pl.pallas_call(
        body, name=name, out_shape=[jax.ShapeDtypeStruct(h.shape, h.dtype) for h in hs],
        in_specs=[any_spec] * n, out_specs=[any_spec] * n, input_output_aliases={a: a for a in range(n)},
        scratch_shapes=[pltpu.SemaphoreType.DMA((n,)), pltpu.SemaphoreType.DMA((n,))])(*hs)


def owner_sum(name, land):
    _, r, w = land.shape
    T = r // 2

    def body(_, l_ref, o_ref):
        acc = l_ref[0].astype(F32)
        for j in range(1, 8):
            acc = acc + l_ref[j].astype(F32)
        o_ref[...] = acc

    grid_spec = pltpu.PrefetchScalarGridSpec(
        num_scalar_prefetch=1, grid=(2,),
        in_specs=[pl.BlockSpec((8, T, w), lambda i, at: (0, i, 0))],
        out_specs=pl.BlockSpec((None, T, w), lambda i, at: (at[0], i, 0)))
    at = jnp.stack([lax.axis_index("c")]).astype(jnp.int32)
    return pl.pallas_call(body, name=name, grid_spec=grid_spec, out_shape=jax.ShapeDtypeStruct((2, r, w), F32),
                          compiler_params=_cp("arbitrary"))(at, land)


W_IN_RUNS = ((0, 2, 1296, 376), (376, 3, 0, 1672), (2048, 1, 920, 752), (2800, 2, 0, 1296), (4096, 0, 0, 1024),
             (5120, 0, 1024, 648), (5768, 1, 0, 920))


def w_in_to_padded(name, g4):
    T = 128

    def body(g_ref, o_ref):
        o_ref[:, D_PROJ:PW] = jnp.zeros((T, PW - D_PROJ), o_ref.dtype)
        for (a, s, j0, w) in W_IN_RUNS:
            o_ref[:, a:a + w] = g_ref[s, :, j0:j0 + w]

    return pl.pallas_call(body, name=name, grid=(D // T,), in_specs=[pl.BlockSpec((4, T, 1672), lambda i: (0, i, 0))],
                          out_specs=pl.BlockSpec((T, PW), lambda i: (i, 0)),
                          out_shape=jax.ShapeDtypeStruct((D, PW), g4.dtype), compiler_params=_cp("arbitrary"))(g4)


def w_in_from_padded(name, gp):
    T = 128

    def body(g_ref, o_ref):
        for (a, s, j0, w) in W_IN_RUNS:
            o_ref[s, :, j0:j0 + w] = g_ref[:, a:a + w]

    return pl.pallas_call(body, name=name, grid=(D // T,), in_specs=[pl.BlockSpec((T, PW), lambda i: (i, 0))],
                          out_specs=pl.BlockSpec((4, T, 1672), lambda i: (0, i, 0)),
                          out_shape=jax.ShapeDtypeStruct((4, D, 1672), gp.dtype), compiler_params=_cp("arbitrary"))(gp)


def sum_devices(name, g):
    def body(g_ref, o_ref):
        acc = g_ref[0]
        for k in range(1, 8):
            acc = acc + g_ref[k]
        o_ref[...] = acc

    return pl.pallas_call(body, name=name, out_shape=jax.ShapeDtypeStruct(g.shape[1:], F32),
                          compiler_params=pltpu.CompilerParams(vmem_limit_bytes=VMEM_LIMIT_V7X))(g)


def adamw(name, w, g, m, v, T):
    r, wd = w.shape
    c1 = 1.0 - ADAM_B1 ** ADAM_STEP
    c2 = 1.0 - ADAM_B2 ** ADAM_STEP

    def body(w_ref, g_ref, m_ref, v_ref, d_ref, mo_ref, vo_ref):
        gv = g_ref[...]
        mn = ADAM_B1 * m_ref[...] + (1.0 - ADAM_B1) * gv
        vn = ADAM_B2 * v_ref[...] + (1.0 - ADAM_B2) * (gv * gv)
        d_ref[...] = -ADAM_LR * ((mn / c1) / (jnp.sqrt(vn / c2) + ADAM_EPS) + ADAM_WD * w_ref[...])
        mo_ref[...] = mn
        vo_ref[...] = vn

    spec = pl.BlockSpec((T, wd), lambda i: (i, 0))
    return pl.pallas_call(body, name=name, grid=(r // T,), in_specs=[spec] * 4, out_specs=[spec] * 3,
                          out_shape=[jax.ShapeDtypeStruct((r, wd), F32)] * 3, compiler_params=_cp("arbitrary"))(w, g, m, v)


BIG = {'w_in': (1024, 1672), 'w_ssd_proj': (256, 1024), 'w_gm_proj': (256, 1024), 'w_out': (256, 1024),
       'w_ff1': (1024, 704), 'w_ff3': (1024, 704), 'w_ff2': (704, 1024)}


class Flat:
    def __init__(self, segs):
        self.off, o = {}, 0
        for name, size in segs:
            self.off[name] = (o, size)
            o += -(-size // 128) * 128
        self.rows = -(-o // 1024) * 8

    def pack(self, vals):
        parts = []
        for name, (o, size) in self.off.items():
            v = vals[name].reshape(-1).astype(F32)
            parts.append(jnp.pad(v, (0, -(-size // 128) * 128 - size)))
        buf = jnp.concatenate(parts)
        return jnp.pad(buf, (0, self.rows * 128 - buf.shape[0])).reshape(self.rows, 128)

    def get(self, buf, name, shape=None):
        o, size = self.off[name]
        v = buf[o // 128:(o + size + 127) // 128].reshape(-1)[:size]
        return v if shape is None else v.reshape(shape)


PARTIALS = Flat([('loss', 1), ('ln0_g', D), ('ln0_b', D), ('dmod_x', 6 * D), ('dmod_c', 6 * D), ('conv_w', 5 * 1536),
                 ('conv_b', 1536), ('dt_bias', 32), ('a_log', 32), ('d_skip', 32), ('ssd_norm_g', D),
                 ('gm_norm_g', D), ('gm_norm_b', D), ('w_spatial', 8 * Q * Q), ('b_spatial', 8 * Q), ('b_gate', 2 * D),
                 ('ln1_g', D), ('ln1_b', D), ('ln2_g', D), ('ln2_b', D)])

WEIGHTS = ('c_ctx', 'ln0_g', 'ln0_b', 'w_ada', 'b_ada', 'w_in', 'conv_w', 'conv_b', 'dt_bias', 'a_log', 'd_skip',
           'ssd_norm_g', 'gm_norm_g', 'gm_norm_b', 'w_spatial', 'b_spatial', 'b_gate', 'w_ssd_proj', 'w_gm_proj',
           'w_out', 'ln1_g', 'ln1_b', 'w_ff1', 'w_ff3', 'w_ff2', 'ln2_g', 'ln2_b')
BIG_NAMES = tuple(BIG)
SMALL_NAMES = tuple(n for n in WEIGHTS if n not in BIG_NAMES and n != 'w_ada')


def kernel(x, c, ctx, c_ctx, ln0_g, ln0_b, w_ada, b_ada, w_in, conv_w, conv_b, dt_bias, a_log, d_skip, ssd_norm_g, gm_norm_g, gm_norm_b, w_spatial, b_spatial, b_gate, w_ssd_proj, w_gm_proj, w_out, ln1_g, ln1_b, w_ff1, w_ff3, w_ff2, ln2_g, ln2_b, loss_target, m_c_ctx, m_ln0_g, m_ln0_b, m_w_ada, m_b_ada, m_w_in, m_conv_w, m_conv_b, m_dt_bias, m_a_log, m_d_skip, m_ssd_norm_g, m_gm_norm_g, m_gm_norm_b, m_w_spatial, m_b_spatial, m_b_gate, m_w_ssd_proj, m_w_gm_proj, m_w_out, m_ln1_g, m_ln1_b, m_w_ff1, m_w_ff3, m_w_ff2, m_ln2_g, m_ln2_b, v_c_ctx, v_ln0_g, v_ln0_b, v_w_ada, v_b_ada, v_w_in, v_conv_w, v_conv_b, v_dt_bias, v_a_log, v_d_skip, v_ssd_norm_g, v_gm_norm_g, v_gm_norm_b, v_w_spatial, v_b_spatial, v_b_gate, v_w_ssd_proj, v_w_gm_proj, v_w_out, v_ln1_g, v_ln1_b, v_w_ff1, v_w_ff3, v_w_ff2, v_ln2_g, v_ln2_b):
    wts = dict(c_ctx=c_ctx, ln0_g=ln0_g, ln0_b=ln0_b, w_ada=w_ada, b_ada=b_ada, w_in=w_in, conv_w=conv_w, conv_b=conv_b,
               dt_bias=dt_bias, a_log=a_log, d_skip=d_skip, ssd_norm_g=ssd_norm_g, gm_norm_g=gm_norm_g,
               gm_norm_b=gm_norm_b, w_spatial=w_spatial, b_spatial=b_spatial, b_gate=b_gate, w_ssd_proj=w_ssd_proj,
               w_gm_proj=w_gm_proj, w_out=w_out, ln1_g=ln1_g, ln1_b=ln1_b, w_ff1=w_ff1, w_ff3=w_ff3, w_ff2=w_ff2,
               ln2_g=ln2_g, ln2_b=ln2_b)
    ms = dict(zip(WEIGHTS, (m_c_ctx, m_ln0_g, m_ln0_b, m_w_ada, m_b_ada, m_w_in, m_conv_w, m_conv_b, m_dt_bias, m_a_log,
                            m_d_skip, m_ssd_norm_g, m_gm_norm_g, m_gm_norm_b, m_w_spatial, m_b_spatial, m_b_gate,
                            m_w_ssd_proj, m_w_gm_proj, m_w_out, m_ln1_g, m_ln1_b, m_w_ff1, m_w_ff3, m_w_ff2, m_ln2_g,
                            m_ln2_b)))
    vs = dict(zip(WEIGHTS, (v_c_ctx, v_ln0_g, v_ln0_b, v_w_ada, v_b_ada, v_w_in, v_conv_w, v_conv_b, v_dt_bias, v_a_log,
                            v_d_skip, v_ssd_norm_g, v_gm_norm_g, v_gm_norm_b, v_w_spatial, v_b_spatial, v_b_gate,
                            v_w_ssd_proj, v_w_gm_proj, v_w_out, v_ln1_g, v_ln1_b, v_w_ff1, v_w_ff3, v_w_ff2, v_ln2_g,
                            v_ln2_b)))
    px, py, pc = _place()
    shard = 2 * px + py
    dev = 2 * shard + pc
    take = lambda a, i, axis=0: lax.dynamic_index_in_dim(a, i, axis, keepdims=False)

    pre = jnp.concatenate([c, jnp.pad(conv_w[0], ((0, 0), (0, D - 384))), jnp.zeros((2, D), F32)], axis=0)
    pre = allgather8("gather_cond", pre, False)
    conv_w_full = pre[0::2, 1:6, :384].transpose(1, 0, 2).reshape(5, 1536)
    a16 = jnp.concatenate([_silu(pre[:, 0, :]), _silu(c_ctx)[None], jnp.zeros((7, D), F32)], axis=0)
    mod = matmul("ada_fwd", a16, w_ada[0], 'nn', 16, 512, 1024)
    mod = mod + lax.dynamic_slice_in_dim(b_ada[0], shard * 1536, 1536)[None]
    mod = allgather8("gather_mod", mod, False)
    mod = jnp.concatenate([mod[0], mod[2], mod[4], mod[6]], axis=1)
    mod_x = take(mod, dev).reshape(6, D)
    mod_c = mod[8].reshape(6, D)

    half = lambda n: take(wts[n][0].reshape(2, BIG[n][0] // 2, BIG[n][1]), pc).astype(BF16)

    def full(n, blocks):
        r, w = BIG[n]
        return blocks.reshape(4, r, w) if w != D else blocks.reshape(4 * r, w)

    class Exchanges:
        rest_names = BIG_NAMES[1:]

        def __init__(self):
            self.pending = []

        def w_in(self):
            (blocks,) = gather_weights("gather_w_in", [half('w_in')])
            w = w_in_to_padded("w_in_layout", full('w_in', blocks))
            halves = [half(n) for n in self.rest_names]
            lands = [jax.ShapeDtypeStruct((8,) + h.shape, BF16) for h in halves]
            self.rest_state, token = split_start("gather_rest_start", halves, lands, plan_gather, w)
            return w, token

        def rest(self, after):
            lands = split_wait("gather_rest_wait", self.rest_state, after)
            return {n: full(n, b) for n, b in zip(self.rest_names, lands)}

        def grads(self, group, gs):
            if group == 'in':
                gs = {'w_in': w_in_from_padded("w_in_grad_layout", gs['w_in'])}
            blocks = [g.reshape(4, 2, BIG[n][0] // 2, BIG[n][1]) for n, g in gs.items()]
            lands = [jax.ShapeDtypeStruct((8,) + b.shape[2:], BF16) for b in blocks]
            state, token = split_start("grads_%s_start" % group, blocks, lands, plan_to_owner, blocks[0])
            self.pending.append((group, tuple(gs), state))
            return token

        def finish(self, after):
            names, halves = [], []
            for group, ns, state in self.pending:
                lands = split_wait("grads_%s_wait" % group, state, after)
                names += ns
                halves += [owner_sum("grads_sum_" + n, l) for n, l in zip(ns, lands)]
            return {n: h.reshape(BIG[n]) for n, h in zip(names, sibling_pair("grads_halves", halves))}

    S = dict(ln0_g=ln0_g, ln0_b=ln0_b, conv_w=conv_w_full, conv_b=conv_b[0], dt_bias=dt_bias[0], a_log=a_log[0],
             d_skip=d_skip[0], ssd_norm_g=ssd_norm_g[0], gm_norm_g=gm_norm_g[0], gm_norm_b=gm_norm_b[0],
             w_spatial=w_spatial[0], b_spatial=b_spatial[0], b_gate=b_gate[0], ln1_g=ln1_g[0], ln1_b=ln1_b[0],
             ln2_g=ln2_g[0], ln2_b=ln2_b[0])
    x_all = jnp.concatenate([ctx[0], x[0]], axis=0)
    exchanges = Exchanges()
    grad_x, gsmall = core(x_all, loss_target[0], mod_x, mod_c, exchanges, S)

    parts = allgather8("gather_partials", PARTIALS.pack(gsmall), False)
    tot = sum_devices("partials_sum", parts)
    g_shards = exchanges.finish(tot)
    g = {n: PARTIALS.get(tot, n) for n in ('ln0_g', 'ln0_b', 'conv_b', 'dt_bias', 'a_log', 'd_skip', 'ssd_norm_g',
                                           'gm_norm_g', 'gm_norm_b', 'w_spatial', 'b_spatial', 'b_gate', 'ln1_g',
                                           'ln1_b', 'ln2_g', 'ln2_b')}
    loss = PARTIALS.get(tot, 'loss', ())
    dmod_c = PARTIALS.get(tot, 'dmod_c')
    g['b_ada'] = PARTIALS.get(tot, 'dmod_x') + dmod_c
    g['conv_w'] = lax.dynamic_slice_in_dim(PARTIALS.get(tot, 'conv_w', (5, 1536)), shard * 384, 384, axis=1)
    o, size = PARTIALS.off['dmod_x']
    dmod_rows = parts[:, o // 128:(o + size) // 128].reshape(8, size)
    dm = jnp.concatenate([dmod_rows, dmod_c[None], jnp.zeros((7, 6 * D), F32)], axis=0)
    dm = lax.dynamic_slice_in_dim(dm, shard * 1536, 1536, axis=1)
    g['w_ada'] = matmul("ada_bwd_w", a16, dm, 'tn', 1024, 512, 16)
    dm_c = jnp.concatenate([dm[8:9], jnp.zeros((15, 1536), F32)], axis=0)
    dc = matmul("ada_bwd_c", dm_c, w_ada[0], 'nt', 16, 1024, 512)
    dc = allgather8("gather_dcctx", dc, False)[:, 0, :]
    dc = ((dc[0] + dc[2]) + dc[4]) + dc[6]
    sg = jax.nn.sigmoid(c_ctx)
    g['c_ctx'] = dc * (sg * (1.0 + c_ctx * (1.0 - sg)))
    for n in BIG_NAMES:
        g[n] = g_shards[n]

    delta, new_m, new_v = {}, {}, {}
    for n in BIG_NAMES + ('w_ada',):
        w2 = wts[n][0]
        T = 352 if n == 'w_ff2' else 256
        d_, m_, v_ = adamw("adamw_" + n, w2, g[n], ms[n][0], vs[n][0], T)
        delta[n], new_m[n], new_v[n] = d_, m_, v_
    lay = Flat([(n, wts[n].size) for n in SMALL_NAMES])
    d_, m_, v_ = adamw("adamw_small", lay.pack(wts), lay.pack(g), lay.pack(ms), lay.pack(vs), lay.rows)
    for n in SMALL_NAMES:
        delta[n], new_m[n], new_v[n] = (lay.get(b, n) for b in (d_, m_, v_))

    shp = lambda d: [d[n].reshape(wts[n].shape) for n in WEIGHTS]
    return (loss, grad_x[None], *shp(g), *shp(delta), *shp(new_m), *shp(new_v))
```

```python
import functools

import jax
import jax.numpy as jnp
from jax import lax
from jax.experimental import pallas as pl
from jax.experimental.pallas import tpu as pltpu

F32 = jnp.float32
BF16 = jnp.bfloat16
MESH = pl.DeviceIdType.MESH

VMEM_LIMIT_V7X = 56 * 1024 * 1024

D = 1024
LC = 256
Q = 128
NH = 16
D_FF = 2816
LN_EPS = 1e-5
ALPHA = 2.0 ** 0.25

PW = 7168
C_GATE, C_UV, C_Z, C_XBC, C_DT = 0, 2048, 4096, 5120, 6656
D_PROJ = 6688

ADAM_LR, ADAM_B1, ADAM_B2, ADAM_EPS, ADAM_WD, ADAM_STEP = 0.001, 0.9, 0.999, 1e-08, 0.01, 10


def _cp(*sem):
    return pltpu.CompilerParams(dimension_semantics=sem, vmem_limit_bytes=VMEM_LIMIT_V7X)


def _dot(a, b, ca, cb):
    return lax.dot_general(a.astype(BF16), b.astype(BF16), (((ca,), (cb,)), ((), ())),
                           preferred_element_type=F32)


@jax.custom_vjp
def mm(a, b):
    return _dot(a, b, 1, 0)


mm.defvjp(lambda a, b: (_dot(a, b, 1, 0), (a, b)),
          lambda r, g: (_dot(g, r[1], 1, 1), _dot(r[0], g, 0, 0)))


@jax.custom_vjp
def mm_nt(a, b):
    return _dot(a, b, 1, 1)


mm_nt.defvjp(lambda a, b: (_dot(a, b, 1, 1), (a, b)),
             lambda r, g: (_dot(g, r[1], 1, 0), _dot(g, r[0], 0, 0)))


@jax.custom_vjp
def mm_tn(a, b):
    return _dot(a, b, 0, 0)


mm_tn.defvjp(lambda a, b: (_dot(a, b, 0, 0), (a, b)),
             lambda r, g: (_dot(r[1], g, 1, 1), _dot(r[0], g, 1, 0)))


def _dot32(a, b):
    return lax.dot_general(a, b, (((1,), (0,)), ((), ())), precision=lax.Precision.HIGHEST,
                           preferred_element_type=F32)


def _cumsum_fn(rev):
    def tri(transpose):
        r = lax.broadcasted_iota(jnp.int32, (Q, Q), 0)
        c = lax.broadcasted_iota(jnp.int32, (Q, Q), 1)
        keep = (r >= c) if (rev == transpose) else (r <= c)
        return jnp.where(keep, 1.0, 0.0).astype(F32)

    @jax.custom_vjp
    def cums(a):
        return _dot32(tri(False), a)

    cums.defvjp(lambda a: (_dot32(tri(False), a), None), lambda _, g: (_dot32(tri(True), g),))
    return cums


def _cols(v, k):
    w = v.shape[1] // k
    return tuple(v[:, w * i:w * (i + 1)] for i in range(k))


def _splitter(k):
    @jax.custom_vjp
    def split(v):
        return _cols(v, k)

    @jax.custom_vjp
    def concat(ps):
        return jnp.concatenate(ps, axis=1)

    split.defvjp(lambda v: (_cols(v, k), None), lambda _, g: (jnp.concatenate(g, axis=1),))
    concat.defvjp(lambda ps: (jnp.concatenate(ps, axis=1), None), lambda _, g: (_cols(g, k),))
    return split, concat


split2, _ = _splitter(2)
split4, _ = _splitter(4)
split8, concat8 = _splitter(8)


def _ln(x, g, b):
    mu = jnp.mean(x, axis=-1, keepdims=True)
    xc = x - mu
    var = jnp.mean(xc * xc, axis=-1, keepdims=True)
    return xc * lax.rsqrt(var + LN_EPS) * g + b


def _silu(x):
    return x * jax.nn.sigmoid(x)


def _gelu(x):
    return 0.5 * x * (1.0 + jnp.tanh(0.7978845608028654 * (x + 0.044715 * (x * x * x))))


def _xspec(T, w, col, roff):
    return pl.BlockSpec((T, w), lambda i, col=col, roff=roff: (jnp.maximum(i + roff, 0), col))


def _pspec(p, sel):
    if sel is None:
        return pl.BlockSpec(p.shape, lambda i, n=p.ndim: (0,) * n)
    return pl.BlockSpec((1,) + p.shape[1:], lambda i, n=p.ndim: (sel(i),) + (0,) * (n - 1))


def _out_plumbing(outs, T, args, in_specs):
    shapes, specs, aliases = [], [], {}
    for k, o in enumerate(outs):
        if o[0] == 'new':
            _, rows, w, roff = o[:4]
            shapes.append(jax.ShapeDtypeStruct((rows, w), o[4] if len(o) > 4 else F32))
            specs.append(_xspec(T, w, 0, roff))
        elif o[0] == 'acc':
            shapes.append(jax.ShapeDtypeStruct(o[1], F32))
            specs.append(pl.BlockSpec(o[1], lambda i, n=len(o[1]): (0,) * n))
        elif o[0] == 'part':
            _, rows, wtot, w, col, roff, dtype = o
            shapes.append(jax.ShapeDtypeStruct((rows, wtot), dtype))
            specs.append(_xspec(T, w, col, roff))
        else:
            _, arr, w, col, roff = o
            aliases[len(args)] = k
            args.append(arr)
            in_specs.append(pl.BlockSpec(memory_space=pl.ANY))
            shapes.append(jax.ShapeDtypeStruct(arr.shape, arr.dtype))
            specs.append(_xspec(T, w, col, roff))
    return shapes, specs, aliases


def stage_fwd(name, f, T, n, xs, ps, outs):
    nx, npar = len(xs), len(ps)
    args = [x[0] for x in xs] + [p[0] for p in ps]
    in_specs = [_xspec(T, w, col, roff) for (_, w, col, roff) in xs] + [_pspec(p, sel) for (p, sel) in ps]
    n_in = len(args)
    shapes, specs, aliases = _out_plumbing(outs, T, args, in_specs)
    n_all_in = len(args)

    def body(*refs):
        i = pl.program_id(0)
        xv = [r[...] for r in refs[:nx]]
        pv = [r[...] if ps[k][1] is None else r[0] for k, r in enumerate(refs[nx:n_in])]
        res = f(*xv, *pv)
        for k, o_ref in enumerate(refs[n_all_in:]):
            if outs[k][0] == 'acc':
                @pl.when(i == 0)
                def _(o_ref=o_ref, v=res[k]):
                    o_ref[...] = v

                @pl.when(i > 0)
                def _(o_ref=o_ref, v=res[k]):
                    o_ref[...] += v
            else:
                o_ref[...] = res[k].astype(o_ref.dtype)

    return pl.pallas_call(body, name=name, grid=(n,), in_specs=in_specs, out_specs=specs, out_shape=shapes,
                          input_output_aliases=aliases, compiler_params=_cp("arbitrary"))(*args)


def stage_bwd(name, f, T, n, xs, ps, cts, dxs, dps, primal=()):
    nx, npar = len(xs), len(ps)
    args = [x[0] for x in xs] + [p[0] for p in ps]
    in_specs = [_xspec(T, w, col, roff) for (_, w, col, roff) in xs] + [_pspec(p, sel) for (p, sel) in ps]
    ct_arrs = [c for c in cts if isinstance(c, tuple)]
    for (a, w, col, roff) in ct_arrs:
        args.append(a)
        in_specs.append(_xspec(T, w, col, roff))
    n_in = len(args)
    outs, out_of = [], []
    for k, o in enumerate(dxs):
        if o is not None:
            outs.append(o)
            out_of.append(('x', k))
    for k, want in enumerate(dps):
        if want:
            p, sel = ps[k]
            outs.append(('acc', p.shape))
            out_of.append(('p', k))
    for k, shape in primal:
        outs.append(('acc', shape))
        out_of.append(('r', k))
    shapes, specs, aliases = _out_plumbing(outs, T, args, in_specs)
    for j, (kind, k) in enumerate(out_of):
        if kind == 'p' and ps[k][1] is not None:
            p, sel = ps[k]
            specs[j] = pl.BlockSpec((1,) + p.shape[1:], lambda i, n=p.ndim, sel=sel: (sel(i),) + (0,) * (n - 1))
    n_all_in = len(args)

    def body(*refs):
        i = pl.program_id(0)
        xv = [r[...] for r in refs[:nx]]
        pv = [r[...] if ps[k][1] is None else r[0] for k, r in enumerate(refs[nx:nx + npar])]
        res, vjp_fn = jax.vjp(f, *xv, *pv)
        ctv, q = [], nx + npar
        for k, c in enumerate(cts):
            if c is None:
                ctv.append(jnp.zeros_like(res[k]))
            elif isinstance(c, tuple):
                v = refs[q][...]
                if c[3] < 0:
                    v = v * (i + c[3] >= 0).astype(F32)
                ctv.append(v)
                q += 1
            else:
                ctv.append(jnp.full_like(res[k], c))
        grads = vjp_fn(tuple(ctv))
        for j, o_ref in enumerate(refs[n_all_in:]):
            kind, k = out_of[j]
            if kind == 'x':
                o_ref[...] = grads[k].astype(o_ref.dtype)
            else:
                g = res[k] if kind == 'r' else grads[nx + k]
                sel = None if kind == 'r' else ps[k][1]
                if sel is None:
                    first = i == 0
                    tgt = o_ref
                else:
                    first = jnp.logical_or(i == 0, sel(i) != sel(jnp.maximum(i - 1, 0)))
                    tgt = o_ref.at[0]

                @pl.when(first)
                def _(tgt=tgt, g=g):
                    tgt[...] = g

                @pl.when(jnp.logical_not(first))
                def _(tgt=tgt, g=g):
                    tgt[...] += g

    return pl.pallas_call(body, name=name, grid=(n,), in_specs=in_specs, out_specs=specs, out_shape=shapes,
                          input_output_aliases=aliases, compiler_params=_cp("arbitrary"))(*args)


_CONTRACT = {'nn': (1, 0), 'nt': (1, 1), 'tn': (0, 0)}


def matmul(name, a, b, mode, tm, tn, tk, out_dtype=F32, add=None, after=None):
    if mode == 'nn':
        (M, K), (_, N) = a.shape, b.shape
    elif mode == 'nt':
        (M, K), (N, _) = a.shape, b.shape
    else:
        (K, M), (_, N) = a.shape, b.shape
    assert M % tm == 0 and N % tn == 0 and K % tk == 0, (name, M, N, K, tm, tn, tk)
    a_spec = (pl.BlockSpec((tk, tm), lambda j, i, k: (k, i)) if mode == 'tn'
              else pl.BlockSpec((tm, tk), lambda j, i, k: (i, k)))
    b_spec = (pl.BlockSpec((tn, tk), lambda j, i, k: (j, k)) if mode == 'nt'
              else pl.BlockSpec((tk, tn), lambda j, i, k: (k, j)))
    o_spec = pl.BlockSpec((tm, tn), lambda j, i, k: (i, j))
    return matmul_call(name, (N // tn, M // tm, K // tk), a, a_spec, b, b_spec, (M, N), o_spec, (tm, tn), mode,
                       out_dtype, add, after)


def matmul_call(name, grid, a, a_spec, b, b_spec, out_shape, o_spec, tile, mode, out_dtype=F32, add=None, after=None):
    tm, tn = tile
    nk = grid[2]
    ca, cb = _CONTRACT[mode]
    args, in_specs = [a, b], [a_spec, b_spec]
    if add is not None:
        args.append(add)
        in_specs.append(o_spec)
    if after is not None:
        args.append(after)
        in_specs.append(pl.BlockSpec(memory_space=pl.ANY))

    def body(*refs):
        a_ref, b_ref = refs[0], refs[1]
        o_ref, acc = refs[-2], refs[-1]
        k = pl.program_id(2)
        if nk == 1:
            p = _dot(a_ref[...], b_ref[...], ca, cb)
            o_ref[...] = (p + refs[2][...] if add is not None else p).astype(out_dtype)
            return

        @pl.when(k == 0)
        def _():
            acc[...] = refs[2][...] if add is not None else jnp.zeros_like(acc)

        acc[...] += _dot(a_ref[...], b_ref[...], ca, cb)

        @pl.when(k == nk - 1)
        def _():
            o_ref[...] = acc[...].astype(out_dtype)

    return pl.pallas_call(body, name=name, grid=grid, in_specs=in_specs, out_specs=o_spec,
                          out_shape=jax.ShapeDtypeStruct(out_shape, out_dtype),
                          scratch_shapes=[pltpu.VMEM((tm, tn) if nk > 1 else (8, 128), F32)],
                          compiler_params=_cp("arbitrary", "arbitrary", "arbitrary"))(*args)


NS, WS = 4, 704


def _resident(name, M, tm, rows, weight, out_shape, out_block, out_map, step, add=None):
    args = [rows[0], weight] + ([] if add is None else [add])
    in_specs = [pl.BlockSpec(rows[1], rows[2]), pl.BlockSpec(weight.shape, lambda i, n=weight.ndim: (0,) * n)]
    if add is not None:
        in_specs.append(pl.BlockSpec(out_block, out_map))
    return pl.pallas_call(step, name=name, grid=(M // tm,), in_specs=in_specs, out_specs=pl.BlockSpec(out_block, out_map),
                          out_shape=jax.ShapeDtypeStruct(out_shape, F32), compiler_params=_cp("arbitrary"))(*args)


def ffn_in_fwd(name, h, w1, w3, tm):
    M = h.shape[0]

    def step(h_ref, w1_ref, w3_ref, a1_ref, a3_ref, act_ref):
        for s in range(NS):
            a1 = _dot(h_ref[...], w1_ref[s], 1, 0)
            a3 = _dot(h_ref[...], w3_ref[s], 1, 0)
            a1_ref[s] = a1
            a3_ref[s] = a3
            act_ref[s] = (_silu(a1) * a3).astype(act_ref.dtype)

    wspec = pl.BlockSpec((NS, D, WS), lambda i: (0, 0, 0))
    ospec = pl.BlockSpec((NS, tm, WS), lambda i: (0, i, 0))
    return pl.pallas_call(
        step, name=name, grid=(M // tm,), in_specs=[pl.BlockSpec((tm, D), lambda i: (i, 0)), wspec, wspec],
        out_specs=[ospec, ospec, ospec],
        out_shape=[jax.ShapeDtypeStruct((NS, M, WS), F32), jax.ShapeDtypeStruct((NS, M, WS), F32),
                   jax.ShapeDtypeStruct((NS, M, WS), BF16)], compiler_params=_cp("arbitrary"))(h, w1, w3)


def ffn_out_bwd_x(name, dff, w2, a1, a3, tm):
    M = dff.shape[0]

    def step(d_ref, w_ref, a1_ref, a3_ref, da1_ref, da3_ref):
        for s in range(NS):
            dact = _dot(d_ref[...], w_ref[s * WS:(s + 1) * WS, :], 1, 1)
            a1 = a1_ref[s]
            sig = jax.nn.sigmoid(a1)
            da3_ref[s] = (dact * (a1 * sig)).astype(da3_ref.dtype)
            da1_ref[s] = (dact * a3_ref[s] * (sig * (1.0 + a1 * (1.0 - sig)))).astype(da1_ref.dtype)

    aspec = pl.BlockSpec((NS, tm, WS), lambda i: (0, i, 0))
    return pl.pallas_call(
        step, name=name, grid=(M // tm,),
        in_specs=[pl.BlockSpec((tm, D), lambda i: (i, 0)), pl.BlockSpec(w2.shape, lambda i: (0, 0)), aspec, aspec],
        out_specs=[aspec, aspec],
        out_shape=[jax.ShapeDtypeStruct((NS, M, WS), BF16)] * 2, compiler_params=_cp("arbitrary"))(dff, w2, a1, a3)


def ff_in_bwd_x(name, da3, w3, tm, add=None):
    M = da3.shape[1]

    def step(*refs):
        d_ref, w_ref, o_ref = refs[0], refs[1], refs[-1]
        acc = _dot(d_ref[0], w_ref[0], 1, 1)
        for s in range(1, NS):
            acc = acc + _dot(d_ref[s], w_ref[s], 1, 1)
        o_ref[...] = acc if add is None else acc + refs[2][...]

    return _resident(name, M, tm, (da3, (NS, tm, WS), lambda i: (0, i, 0)), w3, (M, D), (tm, D), lambda i: (i, 0), step, add)


def ff_in_bwd_w(name, h, da3, tk):
    M = h.shape[0]
    return matmul_call(name, (NS, 1, M // tk), h, pl.BlockSpec((tk, D), lambda j, i, k: (k, 0)),
                       da3, pl.BlockSpec((None, tk, WS), lambda j, i, k: (j, k, 0)),
                       (NS, D, WS), pl.BlockSpec((None, D, WS), lambda j, i, k: (j, 0, 0)), (D, WS), 'tn', BF16)


def ff_out_fwd(name, act3, w2, tm):
    M = act3.shape[1]

    def step(a_ref, w_ref, o_ref):
        acc = _dot(a_ref[0], w_ref[0:WS, :], 1, 0)
        for s in range(1, NS):
            acc = acc + _dot(a_ref[s], w_ref[s * WS:(s + 1) * WS, :], 1, 0)
        o_ref[...] = acc

    return _resident(name, M, tm, (act3, (NS, tm, WS), lambda i: (0, i, 0)), w2, (M, D), (tm, D), lambda i: (i, 0), step)


def ff_out_bwd_w(name, act3, dff, tk):
    M = dff.shape[0]
    return matmul_call(name, (1, NS, M // tk), act3, pl.BlockSpec((None, tk, WS), lambda j, i, k: (i, k, 0)),
                       dff, pl.BlockSpec((tk, D), lambda j, i, k: (k, 0)),
                       (NS * WS, D), pl.BlockSpec((WS, D), lambda j, i, k: (i, 0)), (WS, D), 'tn', BF16)


def _shift_rows(x, d):
    n = x.shape[0]
    if d == 0:
        return x
    y = pltpu.roll(x, (-d) % n, 0)
    t = lax.broadcasted_iota(jnp.int32, x.shape, 0)
    ok = (t + d >= 0) & (t + d < n)
    return jnp.where(ok, y, 0.0)


def _conv_pre(x, w_ref, b_ref):
    acc = jnp.broadcast_to(b_ref[...], x.shape)
    for k in range(5):
        acc = acc + _shift_rows(x, k - 2) * w_ref[k:k + 1, :]
    return acc


def conv_fwd(name, proj, conv_w, conv_b, R):
    segs = ((0, LC), (LC, R))

    def body(x_ref, w_ref, b_ref, o_ref):
        for (s, e) in segs:
            pre = _conv_pre(x_ref[s:e, :], w_ref, b_ref)
            o_ref[s:e, :] = _silu(pre)

    return pl.pallas_call(
        body, name=name, grid=(12,),
        in_specs=[pl.BlockSpec((R, 128), lambda j: (0, C_XBC // 128 + j)),
                  pl.BlockSpec((8, 128), lambda j: (0, j)), pl.BlockSpec((1, 128), lambda j: (0, j))],
        out_specs=pl.BlockSpec((R, 128), lambda j: (0, j)),
        out_shape=jax.ShapeDtypeStruct((R, 1536), F32), compiler_params=_cp("arbitrary"))(proj, conv_w, conv_b)


def conv_bwd(name, proj, conv_w, conv_b, d_f, d_b, d_skip, dproj, R):
    segs = ((0, LC), (LC, R))

    def body(x_ref, w_ref, b_ref, df_ref, db_ref, ds_ref, _, dx_ref, dw_ref, dbias_ref):
        j = pl.program_id(0)
        has_skip = (j < 8).astype(F32)
        dw = [jnp.zeros((1, 128), F32) for _ in range(5)]
        dbias = jnp.zeros((1, 128), F32)
        for (s, e) in segs:
            x = x_ref[s:e, :]
            pre = _conv_pre(x, w_ref, b_ref)
            sig = jax.nn.sigmoid(pre)
            dy = df_ref[s:e, :] + db_ref[s:e, :]
            if s == LC:
                dy = dy + ds_ref[...] * has_skip
            dpre = dy * (sig * (1.0 + pre * (1.0 - sig)))
            dx = jnp.zeros_like(x)
            for k in range(5):
                dx = dx + _shift_rows(dpre, 2 - k) * w_ref[k:k + 1, :]
                dw[k] = dw[k] + jnp.sum(dpre * _shift_rows(x, k - 2), axis=0, keepdims=True)
            dbias = dbias + jnp.sum(dpre, axis=0, keepdims=True)
            dx_ref[s:e, :] = dx.astype(dx_ref.dtype)
        dw_ref[...] = jnp.zeros_like(dw_ref)
        for k in range(5):
            dw_ref[k:k + 1, :] = dw[k]
        dbias_ref[...] = dbias

    return pl.pallas_call(
        body, name=name, grid=(12,),
        in_specs=[pl.BlockSpec((R, 128), lambda j: (0, C_XBC // 128 + j)),
                  pl.BlockSpec((8, 128), lambda j: (0, j)), pl.BlockSpec((1, 128), lambda j: (0, j)),
                  pl.BlockSpec((R, 128), lambda j: (0, j)), pl.BlockSpec((R, 128), lambda j: (0, j)),
                  pl.BlockSpec((R - LC, 128), lambda j: (0, jnp.minimum(j, 7))),
                  pl.BlockSpec(memory_space=pl.ANY)],
        out_specs=[pl.BlockSpec((R, 128), lambda j: (0, C_XBC // 128 + j)),
                   pl.BlockSpec((8, 128), lambda j: (0, j)), pl.BlockSpec((1, 128), lambda j: (0, j))],
        out_shape=[jax.ShapeDtypeStruct(dproj.shape, dproj.dtype), jax.ShapeDtypeStruct((8, 1536), F32),
                   jax.ShapeDtypeStruct((1, 1536), F32)],
        input_output_aliases={6: 0}, compiler_params=_cp("arbitrary"))(proj, conv_w, conv_b, d_f, d_b, d_skip, dproj)


def _ssd_chunk(rev, dirn):
    cums = _cumsum_fn(rev)

    def f(xs, Bs, Cs, dt, alog, Hs):
        lane = lax.broadcasted_iota(jnp.int32, (1, 128), 1)
        sub = lax.broadcasted_iota(jnp.int32, (Q, 1), 0)
        r = lax.broadcasted_iota(jnp.int32, (Q, Q), 0)
        c = lax.broadcasted_iota(jnp.int32, (Q, Q), 1)
        mask = (r <= c) if rev else (r >= c)
        left = lane < 64
        a = dt * (-jnp.exp(alog))
        s = cums(a)
        sT, dtT = s.T, dt.T
        last_row = (sub == (0 if rev else Q - 1)).astype(F32)
        s_last = jnp.sum(s * last_row, axis=0, keepdims=True)
        G = [mm_nt(Cs[g], Bs[g]) for g in range(2)]
        M, es, wc, ed = [], [], [], []
        for h in range(NH):
            l = 16 * dirn + h
            oh_l = (lane == l).astype(F32)
            oh_s = (sub == l).astype(F32)
            s_col = jnp.sum(s * oh_l, axis=1, keepdims=True)
            dt_col = jnp.sum(dt * oh_l, axis=1, keepdims=True)
            s_row = jnp.sum(sT * oh_s, axis=0, keepdims=True)
            dt_row = jnp.sum(dtT * oh_s, axis=0, keepdims=True)
            sl = jnp.sum(s_last * oh_l, axis=1, keepdims=True)
            seg = jnp.where(mask, s_col - s_row, 0.0)
            lm = jnp.where(mask, jnp.exp(seg), 0.0)
            M.append(G[h // 8] * lm * dt_row)
            es.append(jnp.exp(s_col))
            wc.append(jnp.exp(sl - s_col) * dt_col)
            ed.append(jnp.exp(sl))
        Ys, Hn = [], []
        for j in range(8):
            g = j // 4
            xa = jnp.where(left, xs[j], 0.0)
            xb = jnp.where(left, 0.0, xs[j])
            yd = mm(M[2 * j], xa) + mm(M[2 * j + 1], xb)
            yo = mm(Cs[g], Hs[j]) * jnp.where(left, es[2 * j], es[2 * j + 1])
            Ys.append(yd + yo)
            st = mm_tn(Bs[g], xs[j] * jnp.where(left, wc[2 * j], wc[2 * j + 1]))
            Hn.append(Hs[j] * jnp.where(left, ed[2 * j], ed[2 * j + 1]) + st)
        return Ys, Hn

    return f


def _chunk_of(t, n, rev):
    if not rev:
        return t
    return jnp.where(t < 2, 1 - t, n + 1 - t)


def ssd_fwd(name, xbc, dt, alog, n, rev, dirn):
    chunk = _ssd_chunk(rev, dirn)

    def body(x_ref, b_ref, c_ref, dt_ref, al_ref, y_ref, hs_ref, h_scr):
        @pl.when(pl.program_id(0) == 0)
        def _():
            h_scr[...] = jnp.zeros_like(h_scr)

        xs = [x_ref[:, 128 * j:128 * (j + 1)] for j in range(8)]
        Bs = [b_ref[:, 128 * g:128 * (g + 1)] for g in range(2)]
        Cs = [c_ref[:, 128 * g:128 * (g + 1)] for g in range(2)]
        Hs = [h_scr[:, 128 * j:128 * (j + 1)] for j in range(8)]
        hs_ref[0] = h_scr[...]
        Ys, Hn = chunk(xs, Bs, Cs, dt_ref[...], al_ref[...], Hs)
        for j in range(8):
            y_ref[:, 128 * j:128 * (j + 1)] = Ys[j]
            h_scr[:, 128 * j:128 * (j + 1)] = Hn[j]

    cm = lambda t: _chunk_of(t, n, rev)
    return pl.pallas_call(
        body, name=name, grid=(n,),
        in_specs=[pl.BlockSpec((Q, 1024), lambda t: (cm(t), 0)), pl.BlockSpec((Q, 256), lambda t: (cm(t), 4)),
                  pl.BlockSpec((Q, 256), lambda t: (cm(t), 5)), pl.BlockSpec((Q, 128), lambda t: (cm(t), 0)),
                  pl.BlockSpec((1, 128), lambda t: (0, 0))],
        out_specs=[pl.BlockSpec((Q, 1024), lambda t: (cm(t), 0)), pl.BlockSpec((1, Q, 1024), lambda t: (cm(t), 0, 0))],
        out_shape=[jax.ShapeDtypeStruct((n * Q, 1024), F32), jax.ShapeDtypeStruct((n, Q, 1024), F32)],
        scratch_shapes=[pltpu.VMEM((Q, 1024), F32)], compiler_params=_cp("arbitrary"))(xbc, xbc, xbc, dt, alog)


def ssd_bwd(name, xbc, dt, alog, hs, dy, n, rev, dirn):
    chunk = _ssd_chunk(rev, dirn)

    def body(x_ref, b_ref, c_ref, dt_ref, al_ref, hs_ref, dy_ref, dx_ref, ddt_ref, dal_ref, dh_scr):
        tt = pl.program_id(0)
        ch = _chunk_of(n - 1 - tt, n, rev)

        @pl.when(tt == 0)
        def _():
            dh_scr[...] = jnp.zeros_like(dh_scr)

        xs = [x_ref[:, 128 * j:128 * (j + 1)] for j in range(8)]
        Bs = [b_ref[:, 128 * g:128 * (g + 1)] for g in range(2)]
        Cs = [c_ref[:, 128 * g:128 * (g + 1)] for g in range(2)]
        Hs = [hs_ref[0, :, 128 * j:128 * (j + 1)] for j in range(8)]
        live = (ch >= 2).astype(F32)
        dYs = [dy_ref[:, 128 * j:128 * (j + 1)] * live for j in range(8)]
        dHn = [dh_scr[:, 128 * j:128 * (j + 1)] for j in range(8)]
        _, vjp_fn = jax.vjp(chunk, xs, Bs, Cs, dt_ref[...], al_ref[...], Hs)
        dxs, dBs, dCs, ddt, dal, dHs = vjp_fn((dYs, dHn))
        for j in range(8):
            dx_ref[:, 128 * j:128 * (j + 1)] = dxs[j]
            dh_scr[:, 128 * j:128 * (j + 1)] = dHs[j]
        for g in range(2):
            dx_ref[:, 1024 + 128 * g:1024 + 128 * (g + 1)] = dBs[g]
            dx_ref[:, 1280 + 128 * g:1280 + 128 * (g + 1)] = dCs[g]
        ddt_ref[...] = ddt

        @pl.when(tt == 0)
        def _():
            dal_ref[...] = dal

        @pl.when(tt > 0)
        def _():
            dal_ref[...] += dal

    cm = lambda t: _chunk_of(n - 1 - t, n, rev)
    return pl.pallas_call(
        body, name=name, grid=(n,),
        in_specs=[pl.BlockSpec((Q, 1024), lambda t: (cm(t), 0)), pl.BlockSpec((Q, 256), lambda t: (cm(t), 4)),
                  pl.BlockSpec((Q, 256), lambda t: (cm(t), 5)), pl.BlockSpec((Q, 128), lambda t: (cm(t), 0)),
                  pl.BlockSpec((1, 128), lambda t: (0, 0)), pl.BlockSpec((1, Q, 1024), lambda t: (cm(t), 0, 0)),
                  pl.BlockSpec((Q, 1024), lambda t: (jnp.maximum(cm(t) - 2, 0), 0))],
        out_specs=[pl.BlockSpec((Q, 1536), lambda t: (cm(t), 0)), pl.BlockSpec((Q, 128), lambda t: (cm(t), 0)),
                   pl.BlockSpec((1, 128), lambda t: (0, 0))],
        out_shape=[jax.ShapeDtypeStruct((n * Q, 1536), F32), jax.ShapeDtypeStruct((n * Q, 128), F32),
                   jax.ShapeDtypeStruct((1, 128), F32)],
        scratch_shapes=[pltpu.VMEM((Q, 1024), F32)], compiler_params=_cp("arbitrary"))(xbc, xbc, xbc, dt, alog, hs, dy)


def f_norm0(x, g0, b0, sc, sh):
    x0 = _ln(x, g0, b0)
    return x0, x0 * (1.0 + sc) + sh


def f_dt(raw, bias):
    z = split4(raw)[0] + bias
    dt = jnp.maximum(z, 0.0) + jnp.log1p(jnp.exp(-jnp.abs(z)))
    return dt, dt


def f_gated_norm(yf, yb, xs, z, dcol, g):
    h = (yf + yb + xs * dcol) * _silu(z)
    return (h * lax.rsqrt(jnp.mean(h * h, axis=-1, keepdims=True) + LN_EPS) * g,)


def f_gmlp(uv, gmg, gmb, *wb):
    ws, bs = wb[:8], wb[8:]
    u, v = split2(uv)
    vn = split8(_ln(_gelu(v), gmg, gmb))
    mixed = concat8(tuple(mm(ws[g], vn[g]) + bs[g] for g in range(8)))
    return (_gelu(u) * mixed,)


def f_merge(ps, pg, gates, bg):
    gs, gg = split2(jax.nn.sigmoid(gates + bg))
    return (gs * ps + gg * pg,)


def f_res1(x0, out, g1, lg, lb, sc, sh):
    x1 = _ln(ALPHA * x0 + g1 * out, lg, lb)
    return x1, x1 * (1.0 + sc) + sh


def f_res2_loss(x1, ff, tgt, g2, lg, lb):
    x2 = _ln(ALPHA * x1 + g2 * ff, lg, lb)
    e = x2 - tgt
    return (0.5 * jnp.sum(jnp.mean(e * e, axis=-1, keepdims=True), axis=0, keepdims=True),)


def _row_tile(M):
    return 544 if M % 544 == 0 else (512 if M % 512 == 0 else M)


def _tie(v, token):
    return v if token is None else v + token[0:1, 0:1]


def core(x_all, tgt, mod_x, mod_c, X, S):
    R = x_all.shape[0]
    L = R - LC
    n = R // Q
    T = 256
    nt, ntl = R // T, L // T
    tmR, tmL = _row_tile(R), _row_tile(L)
    tkR = 256 if R % 512 else 512
    tkL = 512 if L % 512 == 0 else 256
    row = lambda v: v.reshape(1, -1)
    mx = [row(mod_x[k]) for k in range(6)]
    mc = [row(mod_c[k]) for k in range(6)]
    sel = lambda i: jnp.minimum(i, 1)
    sc1 = jnp.stack([mc[1], mx[1]])
    sh1 = jnp.stack([mc[0], mx[0]])
    ln0 = [(row(S['ln0_g']), None), (row(S['ln0_b']), None), (sc1, sel), (sh1, sel)]

    x0, xm = stage_fwd("norm0_fwd", f_norm0, T, nt, [(x_all, D, 0, 0)], ln0, [('new', R, D, 0), ('new', R, D, 0, BF16)])
    w_in, tok = X.w_in()
    proj = matmul("proj_fwd", xm, w_in, 'nn', tmR, PW // 2, 1024, after=tok)
    conv_w8 = jnp.pad(S['conv_w'], ((0, 3), (0, 0)))
    conv_b = row(S['conv_b'])
    xbc = conv_fwd("conv_fwd", proj, conv_w8, conv_b, R)
    dt_bias = jnp.pad(S['dt_bias'].reshape(1, 32), ((0, 0), (0, 96)))
    alog = jnp.pad(S['a_log'].reshape(1, 32), ((0, 0), (0, 96)))
    x_dt = [(proj, 512, C_DT // 512, 0)]
    dt_f, dt_b = stage_fwd("dt_fwd", f_dt, T, nt, x_dt, [(dt_bias, None)], [('new', R, 128, 0), ('new', R, 128, 0)])
    y_f, hs_f = ssd_fwd("ssd_fwd_f", xbc, dt_f, alog, n, False, 0)
    y_b, hs_b = ssd_fwd("ssd_fwd_b", xbc, dt_b, alog, n, True, 1)
    W = X.rest(hs_b)
    dcol = jnp.repeat(S['d_skip'][0] + S['d_skip'][1], 64).reshape(1, D)
    x_gn = [(y_f, D, 0, 1), (y_b, D, 0, 1), (xbc, D, 0, 1), (proj, D, C_Z // D, 1)]
    p_gn = [(dcol, None), (row(S['ssd_norm_g']), None)]
    (yn,) = stage_fwd("gnorm_fwd", f_gated_norm, T, ntl, x_gn, p_gn, [('new', L, D, 0, BF16)])
    x_gm = [(proj, 2 * D, C_UV // (2 * D), LC // Q)]
    p_gm = ([(row(S['gm_norm_g']), None), (row(S['gm_norm_b']), None)]
            + [(S['w_spatial'][g], None) for g in range(8)] + [(S['b_spatial'][g].reshape(Q, 1), None) for g in range(8)])
    (y_gm,) = stage_fwd("gmlp_fwd", f_gmlp, Q, L // Q, x_gm, p_gm, [('new', L, D, 0, BF16)])
    p_ssd = matmul("pssd_fwd", yn, W['w_ssd_proj'], 'nn', tmL, 1024, 1024)
    p_g = matmul("pgm_fwd", y_gm, W['w_gm_proj'], 'nn', tmL, 1024, 1024)
    x_mg = [(p_ssd, D, 0, 0), (p_g, D, 0, 0), (proj, 2 * D, C_GATE // (2 * D), 1)]
    p_mg = [(row(S['b_gate']), None)]
    (merged,) = stage_fwd("merge_fwd", f_merge, T, ntl, x_mg, p_mg, [('new', L, D, 0, BF16)])
    out = matmul("out_fwd", merged, W['w_out'], 'nn', tmL, 1024, 1024)
    x_r1 = [(x0, D, 0, 1), (out, D, 0, 0)]
    p_r1 = [(mx[2], None), (row(S['ln1_g']), None), (row(S['ln1_b']), None), (mx[4], None), (mx[3], None)]
    x1, hm = stage_fwd("res1_fwd", f_res1, T, ntl, x_r1, p_r1, [('new', L, D, 0), ('new', L, D, 0, BF16)])
    a1, a3, act = ffn_in_fwd("ffn_in_fwd", hm, W['w_ff1'], W['w_ff3'], T)
    ff = ff_out_fwd("ff2_fwd", act, W['w_ff2'], tmL)
    x_r2 = [(x1, D, 0, 0), (ff, D, 0, 0), (tgt, D, 0, 0)]
    p_r2 = [(mx[5], None), (row(S['ln2_g']), None), (row(S['ln2_b']), None)]

    dx1_a, dff, dg2, dl2g, dl2b, loss = stage_bwd(
        "res2_bwd", f_res2_loss, T, ntl, x_r2, p_r2, [1.0],
        [('new', L, D, 0), ('new', L, D, 0, BF16), None], [True, True, True], primal=[(0, (1, 1))])
    da1, da3 = ffn_out_bwd_x("ffn_out_bwd_x", dff, W['w_ff2'], a1, a3, T)
    gw_ff2 = ff_out_bwd_w("ff2_bwd_w", act, dff, tkL)
    dhm = ff_in_bwd_x("ff1_bwd_x", da1, W['w_ff1'], tmL)
    dhm = ff_in_bwd_x("ff3_bwd_x", da3, W['w_ff3'], tmL, add=dhm)
    gw_ff1 = ff_in_bwd_w("ff1_bwd_w", hm, da1, tkL)
    gw_ff3 = ff_in_bwd_w("ff3_bwd_w", hm, da3, tkL)
    tok = X.grads('ffn', {'w_ff2': gw_ff2, 'w_ff1': gw_ff1, 'w_ff3': gw_ff3})
    p_r1 = [(_tie(p_r1[0][0], tok), None)] + p_r1[1:]
    dx0_a, dout, dg1, dl1g, dl1b, dsc2, dsh2 = stage_bwd(
        "res1_bwd", f_res1, T, ntl, x_r1, p_r1, [(dx1_a, D, 0, 0), (dhm, D, 0, 0)],
        [('new', L, D, 0), ('new', L, D, 0, BF16)], [True] * 5)
    dmerged = matmul("out_bwd_x", dout, W['w_out'], 'nt', tmL, 1024, 1024)
    gw_out = matmul("out_bwd_w", merged, dout, 'tn', 1024, 1024, tkL, BF16)
    lt, lq = -(LC // T), -(LC // Q)
    x_mg_b = [(p_ssd, D, 0, lt), (p_g, D, 0, lt), (proj, 2 * D, C_GATE // (2 * D), 0)]
    dp_ssd, dp_g, dproj, dbg = stage_bwd(
        "merge_bwd", f_merge, T, nt, x_mg_b, p_mg, [(dmerged, D, 0, lt)],
        [('new', L, D, lt, BF16), ('new', L, D, lt, BF16), ('part', R, PW, 2 * D, C_GATE // (2 * D), 0, BF16)], [True])
    dyn = matmul("pssd_bwd_x", dp_ssd, W['w_ssd_proj'], 'nt', tmL, 1024, 1024)
    gw_ssd = matmul("pssd_bwd_w", yn, dp_ssd, 'tn', 1024, 1024, tkL, BF16)
    dy_gm = matmul("pgm_bwd_x", dp_g, W['w_gm_proj'], 'nt', tmL, 1024, 1024)
    gw_gm = matmul("pgm_bwd_w", y_gm, dp_g, 'tn', 1024, 1024, tkL, BF16)
    tok = X.grads('proj', {'w_out': gw_out, 'w_ssd_proj': gw_ssd, 'w_gm_proj': gw_gm})
    p_gm = [(_tie(p_gm[0][0], tok), None)] + p_gm[1:]
    r_gm = stage_bwd("gmlp_bwd", f_gmlp, Q, n, [(proj, 2 * D, C_UV // (2 * D), 0)], p_gm, [(dy_gm, D, 0, lq)],
                     [('alias', dproj, 2 * D, C_UV // (2 * D), 0)], [True] * 18)
    dproj, dgmg, dgmb, dws, dbs = r_gm[0], r_gm[1], r_gm[2], r_gm[3:11], r_gm[11:19]
    x_gn_b = [(y_f, D, 0, 0), (y_b, D, 0, 0), (xbc, D, 0, 0), (proj, D, C_Z // D, 0)]
    dy, dskipx, dproj, ddcol, dng = stage_bwd(
        "gnorm_bwd", f_gated_norm, T, nt, x_gn_b, p_gn, [(dyn, D, 0, lt)],
        [('new', L, D, lt), None, ('new', L, D, lt), ('alias', dproj, D, C_Z // D, 0)], [True, True])
    dxbc_f, ddt_f, dal_f = ssd_bwd("ssd_bwd_f", xbc, dt_f, alog, hs_f, dy, n, False, 0)
    dxbc_b, ddt_b, dal_b = ssd_bwd("ssd_bwd_b", xbc, dt_b, alog, hs_b, dy, n, True, 1)
    dproj, ddtb = stage_bwd("dt_bwd", f_dt, T, nt, x_dt, [(dt_bias, None)],
                            [(ddt_f, 128, 0, 0), (ddt_b, 128, 0, 0)],
                            [('alias', dproj, 512, C_DT // 512, 0)], [True])
    dproj, dcw8, dcb = conv_bwd("conv_bwd", proj, conv_w8, conv_b, dxbc_f, dxbc_b, dskipx, dproj, R)
    gw_in = matmul("proj_bwd_w", xm, dproj, 'tn', 1024, PW // 4, tkR, BF16)
    tok = X.grads('in', {'w_in': gw_in})
    dxm = matmul("proj_bwd_x", dproj, w_in, 'nt', R // 4 if R % 32 == 0 else R, 1024, 1024, after=tok)
    grad_x, dl0g, dl0b, dsc1, dsh1 = stage_bwd(
        "norm0_bwd", f_norm0, T, nt, [(x_all, D, 0, 0)], ln0, [(dx0_a, D, 0, -1), (dxm, D, 0, 0)],
        [('new', L, D, -1)], [True] * 4)

    zero = jnp.zeros((D,), F32)
    flat = lambda v: v.reshape(-1)
    small = {
        'loss': flat(loss), 'ln0_g': flat(dl0g), 'ln0_b': flat(dl0b),
        'dmod_x': jnp.concatenate([flat(dsh1[1]), flat(dsc1[1]), flat(dg1), flat(dsh2), flat(dsc2), flat(dg2)]),
        'dmod_c': jnp.concatenate([flat(dsh1[0]), flat(dsc1[0]), zero, zero, zero, zero]),
        'conv_w': flat(dcw8[:5]), 'conv_b': flat(dcb), 'dt_bias': flat(ddtb[:, :32]),
        'a_log': flat((dal_f + dal_b)[:, :32]),
        'd_skip': flat(jnp.tile(ddcol.reshape(1, NH, 64).sum(-1), (2, 1))),
        'ssd_norm_g': flat(dng), 'gm_norm_g': flat(dgmg), 'gm_norm_b': flat(dgmb),
        'w_spatial': flat(jnp.stack(dws)), 'b_spatial': flat(jnp.stack(dbs)), 'b_gate': flat(dbg),
        'ln1_g': flat(dl1g), 'ln1_b': flat(dl1b), 'ln2_g': flat(dl2g), 'ln2_b': flat(dl2b),
    }
    return grad_x, small


def _place():
    return lax.axis_index("x"), lax.axis_index("y"), lax.axis_index("c")


def allgather8(name, blk, hbm):
    space = pl.ANY if hbm else pltpu.VMEM

    def body(x_ref, out_ref, send_sems, recv_sems, local_sem):
        x, y, c = _place()
        me, sibling = (x, y, c), (x, y, 1 - c)
        chips = [(1 - x, y), (x, 1 - y), (1 - x, 1 - y)]

        def slot(px, py, pc):
            return out_ref.at[4 * px + 2 * py + pc]

        def copy(k, block, to, src=None):
            return pltpu.make_async_remote_copy(
                src_ref=slot(*block) if src is None else src, dst_ref=slot(*block),
                send_sem=send_sems.at[k], recv_sem=recv_sems.at[k], device_id=to, device_id_type=MESH)

        mine = pltpu.make_async_copy(x_ref, slot(*me), local_sem)
        mine.start()
        first = [copy(0, me, sibling, src=x_ref)]
        first += [copy(1 + j, me, (*chip, c), src=x_ref) for j, chip in enumerate(chips)]
        for cp in first:
            cp.start()
        passed = [copy(4 + j, (*chip, c), sibling) for j, chip in enumerate(chips)]
        for j, chip in enumerate(chips):
            copy(1 + j, (*chip, c), me).wait_recv()
            passed[j].start()
        copy(0, sibling, me).wait_recv()
        for j, chip in enumerate(chips):
            copy(4 + j, (*chip, 1 - c), me).wait_recv()
        for cp in first + passed:
            cp.wait_send()
        mine.wait()

    return pl.pallas_call(
        body, name=name, out_shape=jax.ShapeDtypeStruct((8,) + blk.shape, blk.dtype),
        in_specs=[pl.BlockSpec(memory_space=space)], out_specs=pl.BlockSpec(memory_space=space),
        scratch_shapes=[pltpu.SemaphoreType.DMA((7,)), pltpu.SemaphoreType.DMA((7,)), pltpu.SemaphoreType.DMA],
        compiler_params=pltpu.CompilerParams(vmem_limit_bytes=VMEM_LIMIT_V7X))(blk)


def gather_weights(name, blks):
    n = len(blks)

    def body(*refs):
        ins, outs = refs[:n], refs[n:2 * n]
        send_sems, recv_sems, local_sems = refs[2 * n:]
        x, y, c = _place()
        me, sibling = (x, y, c), (x, y, 1 - c)
        chips = [(1 - x, y), (x, 1 - y), (1 - x, 1 - y)]

        def copy(a, k, block, to, own=False):
            dst = outs[a].at[4 * block[0] + 2 * block[1] + block[2]]
            return pltpu.make_async_remote_copy(
                src_ref=ins[a] if own else dst, dst_ref=dst, send_sem=send_sems.at[7 * a + k],
                recv_sem=recv_sems.at[7 * a + k], device_id=to, device_id_type=MESH)

        mine = [pltpu.make_async_copy(ins[a], outs[a].at[4 * x + 2 * y + c], local_sems.at[a]) for a in range(n)]
        for cp in mine:
            cp.start()
        first = []
        for j, chip in enumerate(chips):
            first += [copy(a, 1 + j, me, (*chip, c), own=True) for a in range(n)]
        first += [copy(a, 0, me, sibling, own=True) for a in range(n)]
        for cp in first:
            cp.start()
        passed = []
        for a in range(n):
            for j, chip in enumerate(chips):
                copy(a, 1 + j, (*chip, c), me).wait_recv()
                passed.append(copy(a, 4 + j, (*chip, c), sibling))
                passed[-1].start()
        for a in range(n):
            copy(a, 0, sibling, me).wait_recv()
            for j, chip in enumerate(chips):
                copy(a, 4 + j, (*chip, 1 - c), me).wait_recv()
        for cp in first + passed:
            cp.wait_send()
        for cp in mine:
            cp.wait()

    any_spec = pl.BlockSpec(memory_space=pl.ANY)
    return pl.pallas_call(
        body, name=name, out_shape=[jax.ShapeDtypeStruct((8,) + b.shape, b.dtype) for b in blks],
        in_specs=[any_spec] * n, out_specs=[any_spec] * n,
        scratch_shapes=[pltpu.SemaphoreType.DMA((7 * n,)), pltpu.SemaphoreType.DMA((7 * n,)),
                        pltpu.SemaphoreType.DMA((n,))])(*blks)


_HBM = pl.BlockSpec(memory_space=pltpu.HBM)
_SEM = pl.BlockSpec(memory_space=pltpu.SEMAPHORE)
_DATAFLOW = pltpu.SideEffectType.DATAFLOW_SIDE_EFFECTING


def _peers(place):
    x, y, c = place
    return [((1 - x) if k & 4 else x, (1 - y) if k & 2 else y, (1 - c) if k & 1 else c) for k in range(1, 8)]


def _slot(p):
    return 4 * p[0] + 2 * p[1] + p[2]


def plan_gather(place, srcs, lands):
    remote = [(s, l.at[_slot(place)], to) for s, l in zip(srcs, lands) for to in _peers(place)]
    return remote, [(s, l.at[_slot(place)]) for s, l in zip(srcs, lands)]


def plan_to_owner(place, srcs, lands):
    remote = [(s.at[2 * to[0] + to[1], to[2]], l.at[_slot(place)], to) for s, l in zip(srcs, lands) for to in _peers(place)]
    x, y, c = place
    return remote, [(s.at[2 * x + y, c], l.at[_slot(place)]) for s, l in zip(srcs, lands)]


def _hbm(a):
    return pltpu.with_memory_space_constraint(a, pltpu.HBM)


def split_start(name, srcs, land_shapes, plan, after):
    n = len(srcs)

    def body(*refs):
        src, land = refs[:n], refs[n:2 * n]
        send_sems, recv_sems = refs[2 * n + 1], refs[2 * n + 2]
        token, local_sems = refs[-2], refs[-1]
        remote, local = plan(_place(), src, land)
        cps = [pltpu.make_async_copy(s, d, local_sems.at[a]) for a, (s, d) in enumerate(local)]
        for cp in cps:
            cp.start()
        for cp in cps:
            cp.wait()
        for k, (s, d, to) in enumerate(remote):
            pltpu.make_async_remote_copy(src_ref=s, dst_ref=d, send_sem=send_sems.at[k], recv_sem=recv_sems.at[k],
                                         device_id=to, device_id_type=MESH).start()
        token[...] = jnp.zeros_like(token)

    lands = [lax.empty(s.shape, s.dtype) for s in land_shapes]
    thru = [pltpu.HBM(a.shape, a.dtype) for a in list(srcs) + lands]
    res = pl.pallas_call(
        body, name=name,
        out_shape=(pltpu.SemaphoreType.DMA((7 * n,)), pltpu.SemaphoreType.DMA((7 * n,)), *thru,
                   jax.ShapeDtypeStruct((8, 128), F32)),
        in_specs=[_HBM] * (2 * n) + [pl.BlockSpec(memory_space=pl.ANY)],
        out_specs=(_SEM, _SEM, *([_HBM] * (2 * n)), pl.BlockSpec(memory_space=pltpu.VMEM)),
        input_output_aliases={i: 2 + i for i in range(2 * n)},
        scratch_shapes=[pltpu.SemaphoreType.DMA((n,))],
        compiler_params=pltpu.CompilerParams(has_side_effects=_DATAFLOW),
    )(*[_hbm(a) for a in srcs], *[_hbm(a) for a in lands], after)
    return (n, plan, res[0], res[1], res[2:2 + 2 * n]), res[-1]


def split_wait(name, state, after):
    n, plan, send_sems, recv_sems, thru = state

    def body(*refs):
        src, land = refs[:n], refs[n:2 * n]
        send_ref, recv_ref = refs[2 * n], refs[2 * n + 1]
        remote, _ = plan(_place(), src, land)
        for k, (s, d, to) in enumerate(remote):
            cp = pltpu.make_async_remote_copy(src_ref=s, dst_ref=d, send_sem=send_ref.at[k], recv_sem=recv_ref.at[k],
                                              device_id=to, device_id_type=MESH)
            cp.wait_send()
            cp.wait_recv()

    res = pl.pallas_call(
        body, name=name, out_shape=tuple(pltpu.HBM(a.shape, a.dtype) for a in thru),
        in_specs=[_HBM] * (2 * n) + [_SEM, _SEM, pl.BlockSpec(memory_space=pl.ANY)], out_specs=tuple([_HBM] * (2 * n)),
        input_output_aliases={i: i for i in range(2 * n)},
        compiler_params=pltpu.CompilerParams(has_side_effects=_DATAFLOW),
    )(*thru, send_sems, recv_sems, after)
    return res[n:]


def sibling_pair(name, hs):
    n = len(hs)

    def body(*refs):
        ins, outs = refs[:n], refs[n:2 * n]
        send_sems, recv_sems = refs[2 * n:]
        x, y, c = _place()
        cps = [pltpu.make_async_remote_copy(src_ref=outs[a].at[c], dst_ref=outs[a].at[c], send_sem=send_sems.at[a],
                                            recv_sem=recv_sems.at[a], device_id=(x, y, 1 - c), device_id_type=MESH)
               for a in range(n)]
        for cp in cps:
            cp.start()
        for a in range(n):
            pltpu.make_async_remote_copy(src_ref=outs[a].at[1 - c], dst_ref=outs[a].at[1 - c], send_sem=send_sems.at[a],
                                         recv_sem=recv_sems.at[a], device_id=(x, y, 1 - c),
                                         device_id_type=MESH).wait_recv()
        for cp in cps:
            cp.wait_send()

    any_spec = pl.BlockSpec(memory_space=pl.ANY)
    return pl.pallas_call(
        body, name=name, out_shape=[jax.ShapeDtypeStruct(h.shape, h.dtype) for h in hs],
        in_specs=[any_spec] * n, out_specs=[any_spec] * n, input_output_aliases={a: a for a in range(n)},
        scratch_shapes=[pltpu.SemaphoreType.DMA((n,)), pltpu.SemaphoreType.DMA((n,))])(*hs)


def owner_sum(name, land):
    _, r, w = land.shape
    T = r // 2

    def body(_, l_ref, o_ref):
        acc = l_ref[0].astype(F32)
        for j in range(1, 8):
            acc = acc + l_ref[j].astype(F32)
        o_ref[...] = acc

    grid_spec = pltpu.PrefetchScalarGridSpec(
        num_scalar_prefetch=1, grid=(2,),
        in_specs=[pl.BlockSpec((8, T, w), lambda i, at: (0, i, 0))],
        out_specs=pl.BlockSpec((None, T, w), lambda i, at: (at[0], i, 0)))
    at = jnp.stack([lax.axis_index("c")]).astype(jnp.int32)
    return pl.pallas_call(body, name=name, grid_spec=grid_spec, out_shape=jax.ShapeDtypeStruct((2, r, w), F32),
                          compiler_params=_cp("arbitrary"))(at, land)


W_IN_RUNS = ((0, 2, 1296, 376), (376, 3, 0, 1672), (2048, 1, 920, 752), (2800, 2, 0, 1296), (4096, 0, 0, 1024),
             (5120, 0, 1024, 648), (5768, 1, 0, 920))


def w_in_to_padded(name, g4):
    T = 128

    def body(g_ref, o_ref):
        o_ref[:, D_PROJ:PW] = jnp.zeros((T, PW - D_PROJ), o_ref.dtype)
        for (a, s, j0, w) in W_IN_RUNS:
            o_ref[:, a:a + w] = g_ref[s, :, j0:j0 + w]

    return pl.pallas_call(body, name=name, grid=(D // T,), in_specs=[pl.BlockSpec((4, T, 1672), lambda i: (0, i, 0))],
                          out_specs=pl.BlockSpec((T, PW), lambda i: (i, 0)),
                          out_shape=jax.ShapeDtypeStruct((D, PW), g4.dtype), compiler_params=_cp("arbitrary"))(g4)


def w_in_from_padded(name, gp):
    T = 128

    def body(g_ref, o_ref):
        for (a, s, j0, w) in W_IN_RUNS:
            o_ref[s, :, j0:j0 + w] = g_ref[:, a:a + w]

    return pl.pallas_call(body, name=name, grid=(D // T,), in_specs=[pl.BlockSpec((T, PW), lambda i: (i, 0))],
                          out_specs=pl.BlockSpec((4, T, 1672), lambda i: (0, i, 0)),
                          out_shape=jax.ShapeDtypeStruct((4, D, 1672), gp.dtype), compiler_params=_cp("arbitrary"))(gp)


def sum_devices(name, g):
    def body(g_ref, o_ref):
        acc = g_ref[0]
        for k in range(1, 8):
            acc = acc + g_ref[k]
        o_ref[...] = acc

    return pl.pallas_call(body, name=name, out_shape=jax.ShapeDtypeStruct(g.shape[1:], F32),
                          compiler_params=pltpu.CompilerParams(vmem_limit_bytes=VMEM_LIMIT_V7X))(g)


def adamw(name, w, g, m, v, T):
    r, wd = w.shape
    c1 = 1.0 - ADAM_B1 ** ADAM_STEP
    c2 = 1.0 - ADAM_B2 ** ADAM_STEP

    def body(w_ref, g_ref, m_ref, v_ref, d_ref, mo_ref, vo_ref):
        gv = g_ref[...]
        mn = ADAM_B1 * m_ref[...] + (1.0 - ADAM_B1) * gv
        vn = ADAM_B2 * v_ref[...] + (1.0 - ADAM_B2) * (gv * gv)
        d_ref[...] = -ADAM_LR * ((mn / c1) / (jnp.sqrt(vn / c2) + ADAM_EPS) + ADAM_WD * w_ref[...])
        mo_ref[...] = mn
        vo_ref[...] = vn

    spec = pl.BlockSpec((T, wd), lambda i: (i, 0))
    return pl.pallas_call(body, name=name, grid=(r // T,), in_specs=[spec] * 4, out_specs=[spec] * 3,
                          out_shape=[jax.ShapeDtypeStruct((r, wd), F32)] * 3, compiler_params=_cp("arbitrary"))(w, g, m, v)


BIG = {'w_in': (1024, 1672), 'w_ssd_proj': (256, 1024), 'w_gm_proj': (256, 1024), 'w_out': (256, 1024),
       'w_ff1': (1024, 704), 'w_ff3': (1024, 704), 'w_ff2': (704, 1024)}


class Flat:
    def __init__(self, segs):
        self.off, o = {}, 0
        for name, size in segs:
            self.off[name] = (o, size)
            o += -(-size // 128) * 128
        self.rows = -(-o // 1024) * 8

    def pack(self, vals):
        parts = []
        for name, (o, size) in self.off.items():
            v = vals[name].reshape(-1).astype(F32)
            parts.append(jnp.pad(v, (0, -(-size // 128) * 128 - size)))
        buf = jnp.concatenate(parts)
        return jnp.pad(buf, (0, self.rows * 128 - buf.shape[0])).reshape(self.rows, 128)

    def get(self, buf, name, shape=None):
        o, size = self.off[name]
        v = buf[o // 128:(o + size + 127) // 128].reshape(-1)[:size]
        return v if shape is None else v.reshape(shape)


PARTIALS = Flat([('loss', 1), ('ln0_g', D), ('ln0_b', D), ('dmod_x', 6 * D), ('dmod_c', 6 * D), ('conv_w', 5 * 1536),
                 ('conv_b', 1536), ('dt_bias', 32), ('a_log', 32), ('d_skip', 32), ('ssd_norm_g', D),
                 ('gm_norm_g', D), ('gm_norm_b', D), ('w_spatial', 8 * Q * Q), ('b_spatial', 8 * Q), ('b_gate', 2 * D),
                 ('ln1_g', D), ('ln1_b', D), ('ln2_g', D), ('ln2_b', D)])

WEIGHTS = ('c_ctx', 'ln0_g', 'ln0_b', 'w_ada', 'b_ada', 'w_in', 'conv_w', 'conv_b', 'dt_bias', 'a_log', 'd_skip',
           'ssd_norm_g', 'gm_norm_g', 'gm_norm_b', 'w_spatial', 'b_spatial', 'b_gate', 'w_ssd_proj', 'w_gm_proj',
           'w_out', 'ln1_g', 'ln1_b', 'w_ff1', 'w_ff3', 'w_ff2', 'ln2_g', 'ln2_b')
BIG_NAMES = tuple(BIG)
SMALL_NAMES = tuple(n for n in WEIGHTS if n not in BIG_NAMES and n != 'w_ada')


def kernel(x, c, ctx, c_ctx, ln0_g, ln0_b, w_ada, b_ada, w_in, conv_w, conv_b, dt_bias, a_log, d_skip, ssd_norm_g, gm_norm_g, gm_norm_b, w_spatial, b_spatial, b_gate, w_ssd_proj, w_gm_proj, w_out, ln1_g, ln1_b, w_ff1, w_ff3, w_ff2, ln2_g, ln2_b, loss_target, m_c_ctx, m_ln0_g, m_ln0_b, m_w_ada, m_b_ada, m_w_in, m_conv_w, m_conv_b, m_dt_bias, m_a_log, m_d_skip, m_ssd_norm_g, m_gm_norm_g, m_gm_norm_b, m_w_spatial, m_b_spatial, m_b_gate, m_w_ssd_proj, m_w_gm_proj, m_w_out, m_ln1_g, m_ln1_b, m_w_ff1, m_w_ff3, m_w_ff2, m_ln2_g, m_ln2_b, v_c_ctx, v_ln0_g, v_ln0_b, v_w_ada, v_b_ada, v_w_in, v_conv_w, v_conv_b, v_dt_bias, v_a_log, v_d_skip, v_ssd_norm_g, v_gm_norm_g, v_gm_norm_b, v_w_spatial, v_b_spatial, v_b_gate, v_w_ssd_proj, v_w_gm_proj, v_w_out, v_ln1_g, v_ln1_b, v_w_ff1, v_w_ff3, v_w_ff2, v_ln2_g, v_ln2_b):
    wts = dict(c_ctx=c_ctx, ln0_g=ln0_g, ln0_b=ln0_b, w_ada=w_ada, b_ada=b_ada, w_in=w_in, conv_w=conv_w, conv_b=conv_b,
               dt_bias=dt_bias, a_log=a_log, d_skip=d_skip, ssd_norm_g=ssd_norm_g, gm_norm_g=gm_norm_g,
               gm_norm_b=gm_norm_b, w_spatial=w_spatial, b_spatial=b_spatial, b_gate=b_gate, w_ssd_proj=w_ssd_proj,
               w_gm_proj=w_gm_proj, w_out=w_out, ln1_g=ln1_g, ln1_b=ln1_b, w_ff1=w_ff1, w_ff3=w_ff3, w_ff2=w_ff2,
               ln2_g=ln2_g, ln2_b=ln2_b)
    ms = dict(zip(WEIGHTS, (m_c_ctx, m_ln0_g, m_ln0_b, m_w_ada, m_b_ada, m_w_in, m_conv_w, m_conv_b, m_dt_bias, m_a_log,
                            m_d_skip, m_ssd_norm_g, m_gm_norm_g, m_gm_norm_b, m_w_spatial, m_b_spatial, m_b_gate,
                            m_w_ssd_proj, m_w_gm_proj, m_w_out, m_ln1_g, m_ln1_b, m_w_ff1, m_w_ff3, m_w_ff2, m_ln2_g,
                            m_ln2_b)))
    vs = dict(zip(WEIGHTS, (v_c_ctx, v_ln0_g, v_ln0_b, v_w_ada, v_b_ada, v_w_in, v_conv_w, v_conv_b, v_dt_bias, v_a_log,
                            v_d_skip, v_ssd_norm_g, v_gm_norm_g, v_gm_norm_b, v_w_spatial, v_b_spatial, v_b_gate,
                            v_w_ssd_proj, v_w_gm_proj, v_w_out, v_ln1_g, v_ln1_b, v_w_ff1, v_w_ff3, v_w_ff2, v_ln2_g,
                            v_ln2_b)))
    px, py, pc = _place()
    shard = 2 * px + py
    dev = 2 * shard + pc
    take = lambda a, i, axis=0: lax.dynamic_index_in_dim(a, i, axis, keepdims=False)

    pre = jnp.concatenate([c, jnp.pad(conv_w[0], ((0, 0), (0, D - 384))), jnp.zeros((2, D), F32)], axis=0)
    pre = allgather8("gather_cond", pre, False)
    conv_w_full = pre[0::2, 1:6, :384].transpose(1, 0, 2).reshape(5, 1536)
    a16 = jnp.concatenate([_silu(pre[:, 0, :]), _silu(c_ctx)[None], jnp.zeros((7, D), F32)], axis=0)
    mod = matmul("ada_fwd", a16, w_ada[0], 'nn', 16, 512, 1024)
    mod = mod + lax.dynamic_slice_in_dim(b_ada[0], shard * 1536, 1536)[None]
    mod = allgather8("gather_mod", mod, False)
    mod = jnp.concatenate([mod[0], mod[2], mod[4], mod[6]], axis=1)
    mod_x = take(mod, dev).reshape(6, D)
    mod_c = mod[8].reshape(6, D)

    half = lambda n: take(wts[n][0].reshape(2, BIG[n][0] // 2, BIG[n][1]), pc).astype(BF16)

    def full(n, blocks):
        r, w = BIG[n]
        return blocks.reshape(4, r, w) if w != D else blocks.reshape(4 * r, w)

    class Exchanges:
        rest_names = BIG_NAMES[1:]

        def __init__(self):
            self.pending = []

        def w_in(self):
            (blocks,) = gather_weights("gather_w_in", [half('w_in')])
            w = w_in_to_padded("w_in_layout", full('w_in', blocks))
            halves = [half(n) for n in self.rest_names]
            lands = [jax.ShapeDtypeStruct((8,) + h.shape, BF16) for h in halves]
            self.rest_state, token = split_start("gather_rest_start", halves, lands, plan_gather, w)
            return w, token

        def rest(self, after):
            lands = split_wait("gather_rest_wait", self.rest_state, after)
            return {n: full(n, b) for n, b in zip(self.rest_names, lands)}

        def grads(self, group, gs):
            if group == 'in':
                gs = {'w_in': w_in_from_padded("w_in_grad_layout", gs['w_in'])}
            blocks = [g.reshape(4, 2, BIG[n][0] // 2, BIG[n][1]) for n, g in gs.items()]
            lands = [jax.ShapeDtypeStruct((8,) + b.shape[2:], BF16) for b in blocks]
            state, token = split_start("grads_%s_start" % group, blocks, lands, plan_to_owner, blocks[0])
            self.pending.append((group, tuple(gs), state))
            return token

        def finish(self, after):
            names, halves = [], []
            for group, ns, state in self.pending:
                lands = split_wait("grads_%s_wait" % group, state, after)
                names += ns
                halves += [owner_sum("grads_sum_" + n, l) for n, l in zip(ns, lands)]
            return {n: h.reshape(BIG[n]) for n, h in zip(names, sibling_pair("grads_halves", halves))}

    S = dict(ln0_g=ln0_g, ln0_b=ln0_b, conv_w=conv_w_full, conv_b=conv_b[0], dt_bias=dt_bias[0], a_log=a_log[0],
             d_skip=d_skip[0], ssd_norm_g=ssd_norm_g[0], gm_norm_g=gm_norm_g[0], gm_norm_b=gm_norm_b[0],
             w_spatial=w_spatial[0], b_spatial=b_spatial[0], b_gate=b_gate[0], ln1_g=ln1_g[0], ln1_b=ln1_b[0],
             ln2_g=ln2_g[0], ln2_b=ln2_b[0])
    x_all = jnp.concatenate([ctx[0], x[0]], axis=0)
    exchanges = Exchanges()
    grad_x, gsmall = core(x_all, loss_target[0], mod_x, mod_c, exchanges, S)

    parts = allgather8("gather_partials", PARTIALS.pack(gsmall), False)
    tot = sum_devices("partials_sum", parts)
    g_shards = exchanges.finish(tot)
    g = {n: PARTIALS.get(tot, n) for n in ('ln0_g', 'ln0_b', 'conv_b', 'dt_bias', 'a_log', 'd_skip', 'ssd_norm_g',
                                           'gm_norm_g', 'gm_norm_b', 'w_spatial', 'b_spatial', 'b_gate', 'ln1_g',
                                           'ln1_b', 'ln2_g', 'ln2_b')}
    loss = PARTIALS.get(tot, 'loss', ())
    dmod_c = PARTIALS.get(tot, 'dmod_c')
    g['b_ada'] = PARTIALS.get(tot, 'dmod_x') + dmod_c
    g['conv_w'] = lax.dynamic_slice_in_dim(PARTIALS.get(tot, 'conv_w', (5, 1536)), shard * 384, 384, axis=1)
    o, size = PARTIALS.off['dmod_x']
    dmod_rows = parts[:, o // 128:(o + size) // 128].reshape(8, size)
    dm = jnp.concatenate([dmod_rows, dmod_c[None], jnp.zeros((7, 6 * D), F32)], axis=0)
    dm = lax.dynamic_slice_in_dim(dm, shard * 1536, 1536, axis=1)
    g['w_ada'] = matmul("ada_bwd_w", a16, dm, 'tn', 1024, 512, 16)
    dm_c = jnp.concatenate([dm[8:9], jnp.zeros((15, 1536), F32)], axis=0)
    dc = matmul("ada_bwd_c", dm_c, w_ada[0], 'nt', 16, 1024, 512)
    dc = allgather8("gather_dcctx", dc, False)[:, 0, :]
    dc = ((dc[0] + dc[2]) + dc[4]) + dc[6]
    sg = jax.nn.sigmoid(c_ctx)
    g['c_ctx'] = dc * (sg * (1.0 + c_ctx * (1.0 - sg)))
    for n in BIG_NAMES:
        g[n] = g_shards[n]

    delta, new_m, new_v = {}, {}, {}
    for n in BIG_NAMES + ('w_ada',):
        w2 = wts[n][0]
        T = 352 if n == 'w_ff2' else 256
        d_, m_, v_ = adamw("adamw_" + n, w2, g[n], ms[n][0], vs[n][0], T)
        delta[n], new_m[n], new_v[n] = d_, m_, v_
    lay = Flat([(n, wts[n].size) for n in SMALL_NAMES])
    d_, m_, v_ = adamw("adamw_small", lay.pack(wts), lay.pack(g), lay.pack(ms), lay.pack(vs), lay.rows)
    for n in SMALL_NAMES:
        delta[n], new_m[n], new_v[n] = (lay.get(b, n) for b in (d_, m_, v_))

    shp = lambda d: [d[n].reshape(wts[n].shape) for n in WEIGHTS]
    return (loss, grad_x[None], *shp(g), *shp(delta), *shp(new_m), *shp(new_v))
```

```python
import functools

import jax
import jax.numpy as jnp
from jax import lax
from jax.experimental import pallas as pl
from jax.experimental.pallas import tpu as pltpu
from jax.experimental.pallas import tpu_sc as plsc

F32 = jnp.float32
BF16 = jnp.bfloat16
MESH = pl.DeviceIdType.MESH

VMEM_LIMIT_V7X = 56 * 1024 * 1024

D = 1024
LC = 256
Q = 128
NH = 16
D_FF = 2816
LN_EPS = 1e-5
ALPHA = 2.0 ** 0.25

PW = 7168
C_GATE, C_UV, C_Z, C_XBC, C_DT = 0, 2048, 4096, 5120, 6656
D_PROJ = 6688

ADAM_LR, ADAM_B1, ADAM_B2, ADAM_EPS, ADAM_WD, ADAM_STEP = 0.001, 0.9, 0.999, 1e-08, 0.01, 10


def _cp(*sem):
    return pltpu.CompilerParams(dimension_semantics=sem, vmem_limit_bytes=VMEM_LIMIT_V7X)


def _dot(a, b, ca, cb):
    return lax.dot_general(a.astype(BF16), b.astype(BF16), (((ca,), (cb,)), ((), ())),
                           preferred_element_type=F32)


@jax.custom_vjp
def mm(a, b):
    return _dot(a, b, 1, 0)


mm.defvjp(lambda a, b: (_dot(a, b, 1, 0), (a, b)),
          lambda r, g: (_dot(g, r[1], 1, 1), _dot(r[0], g, 0, 0)))


@jax.custom_vjp
def mm_nt(a, b):
    return _dot(a, b, 1, 1)


mm_nt.defvjp(lambda a, b: (_dot(a, b, 1, 1), (a, b)),
             lambda r, g: (_dot(g, r[1], 1, 0), _dot(g, r[0], 0, 0)))


@jax.custom_vjp
def mm_tn(a, b):
    return _dot(a, b, 0, 0)


mm_tn.defvjp(lambda a, b: (_dot(a, b, 0, 0), (a, b)),
             lambda r, g: (_dot(r[1], g, 1, 1), _dot(r[0], g, 1, 0)))


def _dot32(a, b):
    return lax.dot_general(a, b, (((1,), (0,)), ((), ())), precision=lax.Precision.HIGHEST,
                           preferred_element_type=F32)


def _cumsum_fn(rev):
    def tri(transpose):
        r = lax.broadcasted_iota(jnp.int32, (Q, Q), 0)
        c = lax.broadcasted_iota(jnp.int32, (Q, Q), 1)
        keep = (r >= c) if (rev == transpose) else (r <= c)
        return jnp.where(keep, 1.0, 0.0).astype(F32)

    @jax.custom_vjp
    def cums(a):
        return _dot32(tri(False), a)

    cums.defvjp(lambda a: (_dot32(tri(False), a), None), lambda _, g: (_dot32(tri(True), g),))
    return cums


def _cols(v, k):
    w = v.shape[1] // k
    return tuple(v[:, w * i:w * (i + 1)] for i in range(k))


def _splitter(k):
    @jax.custom_vjp
    def split(v):
        return _cols(v, k)

    @jax.custom_vjp
    def concat(ps):
        return jnp.concatenate(ps, axis=1)

    split.defvjp(lambda v: (_cols(v, k), None), lambda _, g: (jnp.concatenate(g, axis=1),))
    concat.defvjp(lambda ps: (jnp.concatenate(ps, axis=1), None), lambda _, g: (_cols(g, k),))
    return split, concat


split2, _ = _splitter(2)
split4, _ = _splitter(4)
split8, concat8 = _splitter(8)


def _ln(x, g, b):
    mu = jnp.mean(x, axis=-1, keepdims=True)
    xc = x - mu
    var = jnp.mean(xc * xc, axis=-1, keepdims=True)
    return xc * lax.rsqrt(var + LN_EPS) * g + b


def _silu(x):
    return x * jax.nn.sigmoid(x)


def _gelu(x):
    return 0.5 * x * (1.0 + jnp.tanh(0.7978845608028654 * (x + 0.044715 * (x * x * x))))


def _xspec(T, w, col, roff):
    return pl.BlockSpec((T, w), lambda i, col=col, roff=roff: (jnp.maximum(i + roff, 0), col))


def _pspec(p, sel):
    if sel is None:
        return pl.BlockSpec(p.shape, lambda i, n=p.ndim: (0,) * n)
    return pl.BlockSpec((1,) + p.shape[1:], lambda i, n=p.ndim: (sel(i),) + (0,) * (n - 1))


def _out_plumbing(outs, T, args, in_specs):
    shapes, specs, aliases = [], [], {}
    for k, o in enumerate(outs):
        if o[0] == 'new':
            _, rows, w, roff = o[:4]
            shapes.append(jax.ShapeDtypeStruct((rows, w), o[4] if len(o) > 4 else F32))
            specs.append(_xspec(T, w, 0, roff))
        elif o[0] == 'acc':
            shapes.append(jax.ShapeDtypeStruct(o[1], F32))
            specs.append(pl.BlockSpec(o[1], lambda i, n=len(o[1]): (0,) * n))
        elif o[0] == 'part':
            _, rows, wtot, w, col, roff, dtype = o
            shapes.append(jax.ShapeDtypeStruct((rows, wtot), dtype))
            specs.append(_xspec(T, w, col, roff))
        else:
            _, arr, w, col, roff = o
            aliases[len(args)] = k
            args.append(arr)
            in_specs.append(pl.BlockSpec(memory_space=pl.ANY))
            shapes.append(jax.ShapeDtypeStruct(arr.shape, arr.dtype))
            specs.append(_xspec(T, w, col, roff))
    return shapes, specs, aliases


def stage_fwd(name, f, T, n, xs, ps, outs):
    nx, npar = len(xs), len(ps)
    args = [x[0] for x in xs] + [p[0] for p in ps]
    in_specs = [_xspec(T, w, col, roff) for (_, w, col, roff) in xs] + [_pspec(p, sel) for (p, sel) in ps]
    n_in = len(args)
    shapes, specs, aliases = _out_plumbing(outs, T, args, in_specs)
    n_all_in = len(args)

    def body(*refs):
        i = pl.program_id(0)
        xv = [r[...] for r in refs[:nx]]
        pv = [r[...] if ps[k][1] is None else r[0] for k, r in enumerate(refs[nx:n_in])]
        res = f(*xv, *pv)
        for k, o_ref in enumerate(refs[n_all_in:]):
            if outs[k][0] == 'acc':
                @pl.when(i == 0)
                def _(o_ref=o_ref, v=res[k]):
                    o_ref[...] = v

                @pl.when(i > 0)
                def _(o_ref=o_ref, v=res[k]):
                    o_ref[...] += v
            else:
                o_ref[...] = res[k].astype(o_ref.dtype)

    return pl.pallas_call(body, name=name, grid=(n,), in_specs=in_specs, out_specs=specs, out_shape=shapes,
                          input_output_aliases=aliases, compiler_params=_cp("arbitrary"))(*args)


def stage_bwd(name, f, T, n, xs, ps, cts, dxs, dps, primal=()):
    nx, npar = len(xs), len(ps)
    args = [x[0] for x in xs] + [p[0] for p in ps]
    in_specs = [_xspec(T, w, col, roff) for (_, w, col, roff) in xs] + [_pspec(p, sel) for (p, sel) in ps]
    ct_arrs = [c for c in cts if isinstance(c, tuple)]
    for (a, w, col, roff) in ct_arrs:
        args.append(a)
        in_specs.append(_xspec(T, w, col, roff))
    n_in = len(args)
    outs, out_of = [], []
    for k, o in enumerate(dxs):
        if o is not None:
            outs.append(o)
            out_of.append(('x', k))
    for k, want in enumerate(dps):
        if want:
            p, sel = ps[k]
            outs.append(('acc', p.shape))
            out_of.append(('p', k))
    for k, shape in primal:
        outs.append(('acc', shape))
        out_of.append(('r', k))
    shapes, specs, aliases = _out_plumbing(outs, T, args, in_specs)
    for j, (kind, k) in enumerate(out_of):
        if kind == 'p' and ps[k][1] is not None:
            p, sel = ps[k]
            specs[j] = pl.BlockSpec((1,) + p.shape[1:], lambda i, n=p.ndim, sel=sel: (sel(i),) + (0,) * (n - 1))
    n_all_in = len(args)

    def body(*refs):
        i = pl.program_id(0)
        xv = [r[...] for r in refs[:nx]]
        pv = [r[...] if ps[k][1] is None else r[0] for k, r in enumerate(refs[nx:nx + npar])]
        res, vjp_fn = jax.vjp(f, *xv, *pv)
        ctv, q = [], nx + npar
        for k, c in enumerate(cts):
            if c is None:
                ctv.append(jnp.zeros_like(res[k]))
            elif isinstance(c, tuple):
                v = refs[q][...]
                if c[3] < 0:
                    v = v * (i + c[3] >= 0).astype(F32)
                ctv.append(v)
                q += 1
            else:
                ctv.append(jnp.full_like(res[k], c))
        grads = vjp_fn(tuple(ctv))
        for j, o_ref in enumerate(refs[n_all_in:]):
            kind, k = out_of[j]
            if kind == 'x':
                o_ref[...] = grads[k].astype(o_ref.dtype)
            else:
                g = res[k] if kind == 'r' else grads[nx + k]
                sel = None if kind == 'r' else ps[k][1]
                if sel is None:
                    first = i == 0
                    tgt = o_ref
                else:
                    first = jnp.logical_or(i == 0, sel(i) != sel(jnp.maximum(i - 1, 0)))
                    tgt = o_ref.at[0]

                @pl.when(first)
                def _(tgt=tgt, g=g):
                    tgt[...] = g

                @pl.when(jnp.logical_not(first))
                def _(tgt=tgt, g=g):
                    tgt[...] += g

    return pl.pallas_call(body, name=name, grid=(n,), in_specs=in_specs, out_specs=specs, out_shape=shapes,
                          input_output_aliases=aliases, compiler_params=_cp("arbitrary"))(*args)


_CONTRACT = {'nn': (1, 0), 'nt': (1, 1), 'tn': (0, 0)}


def matmul(name, a, b, mode, tm, tn, tk, out_dtype=F32, add=None, after=None):
    if mode == 'nn':
        (M, K), (_, N) = a.shape, b.shape
    elif mode == 'nt':
        (M, K), (N, _) = a.shape, b.shape
    else:
        (K, M), (_, N) = a.shape, b.shape
    assert M % tm == 0 and N % tn == 0 and K % tk == 0, (name, M, N, K, tm, tn, tk)
    a_spec = (pl.BlockSpec((tk, tm), lambda j, i, k: (k, i)) if mode == 'tn'
              else pl.BlockSpec((tm, tk), lambda j, i, k: (i, k)))
    b_spec = (pl.BlockSpec((tn, tk), lambda j, i, k: (j, k)) if mode == 'nt'
              else pl.BlockSpec((tk, tn), lambda j, i, k: (k, j)))
    o_spec = pl.BlockSpec((tm, tn), lambda j, i, k: (i, j))
    return matmul_call(name, (N // tn, M // tm, K // tk), a, a_spec, b, b_spec, (M, N), o_spec, (tm, tn), mode,
                       out_dtype, add, after)


def matmul_call(name, grid, a, a_spec, b, b_spec, out_shape, o_spec, tile, mode, out_dtype=F32, add=None, after=None):
    tm, tn = tile
    nk = grid[2]
    ca, cb = _CONTRACT[mode]
    args, in_specs = [a, b], [a_spec, b_spec]
    if add is not None:
        args.append(add)
        in_specs.append(o_spec)
    if after is not None:
        args.append(after)
        in_specs.append(pl.BlockSpec(memory_space=pl.ANY))

    def body(*refs):
        a_ref, b_ref = refs[0], refs[1]
        o_ref, acc = refs[-2], refs[-1]
        k = pl.program_id(2)
        if nk == 1:
            p = _dot(a_ref[...], b_ref[...], ca, cb)
            o_ref[...] = (p + refs[2][...] if add is not None else p).astype(out_dtype)
            return

        @pl.when(k == 0)
        def _():
            acc[...] = refs[2][...] if add is not None else jnp.zeros_like(acc)

        acc[...] += _dot(a_ref[...], b_ref[...], ca, cb)

        @pl.when(k == nk - 1)
        def _():
            o_ref[...] = acc[...].astype(out_dtype)

    return pl.pallas_call(body, name=name, grid=grid, in_specs=in_specs, out_specs=o_spec,
                          out_shape=jax.ShapeDtypeStruct(out_shape, out_dtype),
                          scratch_shapes=[pltpu.VMEM((tm, tn) if nk > 1 else (8, 128), F32)],
                          compiler_params=_cp("arbitrary", "arbitrary", "arbitrary"))(*args)


NS, WS = 4, 704


def _resident(name, M, tm, rows, weight, out_shape, out_block, out_map, step, add=None):
    args = [rows[0], weight] + ([] if add is None else [add])
    in_specs = [pl.BlockSpec(rows[1], rows[2]), pl.BlockSpec(weight.shape, lambda i, n=weight.ndim: (0,) * n)]
    if add is not None:
        in_specs.append(pl.BlockSpec(out_block, out_map))
    return pl.pallas_call(step, name=name, grid=(M // tm,), in_specs=in_specs, out_specs=pl.BlockSpec(out_block, out_map),
                          out_shape=jax.ShapeDtypeStruct(out_shape, F32), compiler_params=_cp("arbitrary"))(*args)


def ffn_in_fwd(name, h, w1, w3, tm):
    M = h.shape[0]

    def step(h_ref, w1_ref, w3_ref, a1_ref, a3_ref, act_ref):
        for s in range(NS):
            a1 = _dot(h_ref[...], w1_ref[s], 1, 0)
            a3 = _dot(h_ref[...], w3_ref[s], 1, 0)
            a1_ref[s] = a1
            a3_ref[s] = a3
            act_ref[s] = (_silu(a1) * a3).astype(act_ref.dtype)

    wspec = pl.BlockSpec((NS, D, WS), lambda i: (0, 0, 0))
    ospec = pl.BlockSpec((NS, tm, WS), lambda i: (0, i, 0))
    return pl.pallas_call(
        step, name=name, grid=(M // tm,), in_specs=[pl.BlockSpec((tm, D), lambda i: (i, 0)), wspec, wspec],
        out_specs=[ospec, ospec, ospec],
        out_shape=[jax.ShapeDtypeStruct((NS, M, WS), F32), jax.ShapeDtypeStruct((NS, M, WS), F32),
                   jax.ShapeDtypeStruct((NS, M, WS), BF16)], compiler_params=_cp("arbitrary"))(h, w1, w3)


def ffn_out_bwd_x(name, dff, w2, a1, a3, tm):
    M = dff.shape[0]

    def step(d_ref, w_ref, a1_ref, a3_ref, da1_ref, da3_ref):
        for s in range(NS):
            dact = _dot(d_ref[...], w_ref[s * WS:(s + 1) * WS, :], 1, 1)
            a1 = a1_ref[s]
            sig = jax.nn.sigmoid(a1)
            da3_ref[s] = (dact * (a1 * sig)).astype(da3_ref.dtype)
            da1_ref[s] = (dact * a3_ref[s] * (sig * (1.0 + a1 * (1.0 - sig)))).astype(da1_ref.dtype)

    aspec = pl.BlockSpec((NS, tm, WS), lambda i: (0, i, 0))
    return pl.pallas_call(
        step, name=name, grid=(M // tm,),
        in_specs=[pl.BlockSpec((tm, D), lambda i: (i, 0)), pl.BlockSpec(w2.shape, lambda i: (0, 0)), aspec, aspec],
        out_specs=[aspec, aspec],
        out_shape=[jax.ShapeDtypeStruct((NS, M, WS), BF16)] * 2, compiler_params=_cp("arbitrary"))(dff, w2, a1, a3)


def ff_in_bwd_x(name, da3, w3, tm, add=None):
    M = da3.shape[1]

    def step(*refs):
        d_ref, w_ref, o_ref = refs[0], refs[1], refs[-1]
        acc = _dot(d_ref[0], w_ref[0], 1, 1)
        for s in range(1, NS):
            acc = acc + _dot(d_ref[s], w_ref[s], 1, 1)
        o_ref[...] = acc if add is None else acc + refs[2][...]

    return _resident(name, M, tm, (da3, (NS, tm, WS), lambda i: (0, i, 0)), w3, (M, D), (tm, D), lambda i: (i, 0), step, add)


def ff_in_bwd_w(name, h, da3, tk):
    M = h.shape[0]
    return matmul_call(name, (NS, 1, M // tk), h, pl.BlockSpec((tk, D), lambda j, i, k: (k, 0)),
                       da3, pl.BlockSpec((None, tk, WS), lambda j, i, k: (j, k, 0)),
                       (NS, D, WS), pl.BlockSpec((None, D, WS), lambda j, i, k: (j, 0, 0)), (D, WS), 'tn', BF16)


def ff_out_fwd(name, act3, w2, tm):
    M = act3.shape[1]

    def step(a_ref, w_ref, o_ref):
        acc = _dot(a_ref[0], w_ref[0:WS, :], 1, 0)
        for s in range(1, NS):
            acc = acc + _dot(a_ref[s], w_ref[s * WS:(s + 1) * WS, :], 1, 0)
        o_ref[...] = acc

    return _resident(name, M, tm, (act3, (NS, tm, WS), lambda i: (0, i, 0)), w2, (M, D), (tm, D), lambda i: (i, 0), step)


def ff_out_bwd_w(name, act3, dff, tk):
    M = dff.shape[0]
    return matmul_call(name, (1, NS, M // tk), act3, pl.BlockSpec((None, tk, WS), lambda j, i, k: (i, k, 0)),
                       dff, pl.BlockSpec((tk, D), lambda j, i, k: (k, 0)),
                       (NS * WS, D), pl.BlockSpec((WS, D), lambda j, i, k: (i, 0)), (WS, D), 'tn', BF16)


def _shift_rows(x, d):
    n = x.shape[0]
    if d == 0:
        return x
    y = pltpu.roll(x, (-d) % n, 0)
    t = lax.broadcasted_iota(jnp.int32, x.shape, 0)
    ok = (t + d >= 0) & (t + d < n)
    return jnp.where(ok, y, 0.0)


def _conv_pre(x, w_ref, b_ref):
    acc = jnp.broadcast_to(b_ref[...], x.shape)
    for k in range(5):
        acc = acc + _shift_rows(x, k - 2) * w_ref[k:k + 1, :]
    return acc


def conv_fwd(name, proj, conv_w, conv_b, R):
    segs = ((0, LC), (LC, R))

    def body(x_ref, w_ref, b_ref, o_ref):
        for (s, e) in segs:
            pre = _conv_pre(x_ref[s:e, :], w_ref, b_ref)
            o_ref[s:e, :] = _silu(pre)

    return pl.pallas_call(
        body, name=name, grid=(12,),
        in_specs=[pl.BlockSpec((R, 128), lambda j: (0, C_XBC // 128 + j)),
                  pl.BlockSpec((8, 128), lambda j: (0, j)), pl.BlockSpec((1, 128), lambda j: (0, j))],
        out_specs=pl.BlockSpec((R, 128), lambda j: (0, j)),
        out_shape=jax.ShapeDtypeStruct((R, 1536), F32), compiler_params=_cp("arbitrary"))(proj, conv_w, conv_b)


def conv_bwd(name, proj, conv_w, conv_b, d_f, d_b, d_skip, dproj, R):
    segs = ((0, LC), (LC, R))

    def body(x_ref, w_ref, b_ref, df_ref, db_ref, ds_ref, _, dx_ref, dw_ref, dbias_ref):
        j = pl.program_id(0)
        has_skip = (j < 8).astype(F32)
        dw = [jnp.zeros((1, 128), F32) for _ in range(5)]
        dbias = jnp.zeros((1, 128), F32)
        for (s, e) in segs:
            x = x_ref[s:e, :]
            pre = _conv_pre(x, w_ref, b_ref)
            sig = jax.nn.sigmoid(pre)
            dy = df_ref[s:e, :] + db_ref[s:e, :]
            if s == LC:
                dy = dy + ds_ref[...] * has_skip
            dpre = dy * (sig * (1.0 + pre * (1.0 - sig)))
            dx = jnp.zeros_like(x)
            for k in range(5):
                dx = dx + _shift_rows(dpre, 2 - k) * w_ref[k:k + 1, :]
                dw[k] = dw[k] + jnp.sum(dpre * _shift_rows(x, k - 2), axis=0, keepdims=True)
            dbias = dbias + jnp.sum(dpre, axis=0, keepdims=True)
            dx_ref[s:e, :] = dx.astype(dx_ref.dtype)
        dw_ref[...] = jnp.zeros_like(dw_ref)
        for k in range(5):
            dw_ref[k:k + 1, :] = dw[k]
        dbias_ref[...] = dbias

    return pl.pallas_call(
        body, name=name, grid=(12,),
        in_specs=[pl.BlockSpec((R, 128), lambda j: (0, C_XBC // 128 + j)),
                  pl.BlockSpec((8, 128), lambda j: (0, j)), pl.BlockSpec((1, 128), lambda j: (0, j)),
                  pl.BlockSpec((R, 128), lambda j: (0, j)), pl.BlockSpec((R, 128), lambda j: (0, j)),
                  pl.BlockSpec((R - LC, 128), lambda j: (0, jnp.minimum(j, 7))),
                  pl.BlockSpec(memory_space=pl.ANY)],
        out_specs=[pl.BlockSpec((R, 128), lambda j: (0, C_XBC // 128 + j)),
                   pl.BlockSpec((8, 128), lambda j: (0, j)), pl.BlockSpec((1, 128), lambda j: (0, j))],
        out_shape=[jax.ShapeDtypeStruct(dproj.shape, dproj.dtype), jax.ShapeDtypeStruct((8, 1536), F32),
                   jax.ShapeDtypeStruct((1, 1536), F32)],
        input_output_aliases={6: 0}, compiler_params=_cp("arbitrary"))(proj, conv_w, conv_b, d_f, d_b, d_skip, dproj)


def _ssd_chunk(rev, dirn):
    cums = _cumsum_fn(rev)

    def f(xs, Bs, Cs, dt, alog, Hs):
        lane = lax.broadcasted_iota(jnp.int32, (1, 128), 1)
        sub = lax.broadcasted_iota(jnp.int32, (Q, 1), 0)
        r = lax.broadcasted_iota(jnp.int32, (Q, Q), 0)
        c = lax.broadcasted_iota(jnp.int32, (Q, Q), 1)
        mask = (r <= c) if rev else (r >= c)
        left = lane < 64
        a = dt * (-jnp.exp(alog))
        s = cums(a)
        sT, dtT = s.T, dt.T
        last_row = (sub == (0 if rev else Q - 1)).astype(F32)
        s_last = jnp.sum(s * last_row, axis=0, keepdims=True)
        G = [mm_nt(Cs[g], Bs[g]) for g in range(2)]
        M, es, wc, ed = [], [], [], []
        for h in range(NH):
            l = 16 * dirn + h
            oh_l = (lane == l).astype(F32)
            oh_s = (sub == l).astype(F32)
            s_col = jnp.sum(s * oh_l, axis=1, keepdims=True)
            dt_col = jnp.sum(dt * oh_l, axis=1, keepdims=True)
            s_row = jnp.sum(sT * oh_s, axis=0, keepdims=True)
            dt_row = jnp.sum(dtT * oh_s, axis=0, keepdims=True)
            sl = jnp.sum(s_last * oh_l, axis=1, keepdims=True)
            seg = jnp.where(mask, s_col - s_row, 0.0)
            lm = jnp.where(mask, jnp.exp(seg), 0.0)
            M.append(G[h // 8] * lm * dt_row)
            es.append(jnp.exp(s_col))
            wc.append(jnp.exp(sl - s_col) * dt_col)
            ed.append(jnp.exp(sl))
        Ys, Hn = [], []
        for j in range(8):
            g = j // 4
            xa = jnp.where(left, xs[j], 0.0)
            xb = jnp.where(left, 0.0, xs[j])
            yd = mm(M[2 * j], xa) + mm(M[2 * j + 1], xb)
            yo = mm(Cs[g], Hs[j]) * jnp.where(left, es[2 * j], es[2 * j + 1])
            Ys.append(yd + yo)
            st = mm_tn(Bs[g], xs[j] * jnp.where(left, wc[2 * j], wc[2 * j + 1]))
            Hn.append(Hs[j] * jnp.where(left, ed[2 * j], ed[2 * j + 1]) + st)
        return Ys, Hn

    return f


def _chunk_of(t, n, rev):
    if not rev:
        return t
    return jnp.where(t < 2, 1 - t, n + 1 - t)


def ssd_fwd(name, xbc, dt, alog, n, rev, dirn):
    chunk = _ssd_chunk(rev, dirn)

    def body(x_ref, b_ref, c_ref, dt_ref, al_ref, y_ref, hs_ref, h_scr):
        @pl.when(pl.program_id(0) == 0)
        def _():
            h_scr[...] = jnp.zeros_like(h_scr)

        xs = [x_ref[:, 128 * j:128 * (j + 1)] for j in range(8)]
        Bs = [b_ref[:, 128 * g:128 * (g + 1)] for g in range(2)]
        Cs = [c_ref[:, 128 * g:128 * (g + 1)] for g in range(2)]
        Hs = [h_scr[:, 128 * j:128 * (j + 1)] for j in range(8)]
        hs_ref[0] = h_scr[...]
        Ys, Hn = chunk(xs, Bs, Cs, dt_ref[...], al_ref[...], Hs)
        for j in range(8):
            y_ref[:, 128 * j:128 * (j + 1)] = Ys[j]
            h_scr[:, 128 * j:128 * (j + 1)] = Hn[j]

    cm = lambda t: _chunk_of(t, n, rev)
    return pl.pallas_call(
        body, name=name, grid=(n,),
        in_specs=[pl.BlockSpec((Q, 1024), lambda t: (cm(t), 0)), pl.BlockSpec((Q, 256), lambda t: (cm(t), 4)),
                  pl.BlockSpec((Q, 256), lambda t: (cm(t), 5)), pl.BlockSpec((Q, 128), lambda t: (cm(t), 0)),
                  pl.BlockSpec((1, 128), lambda t: (0, 0))],
        out_specs=[pl.BlockSpec((Q, 1024), lambda t: (cm(t), 0)), pl.BlockSpec((1, Q, 1024), lambda t: (cm(t), 0, 0))],
        out_shape=[jax.ShapeDtypeStruct((n * Q, 1024), F32), jax.ShapeDtypeStruct((n, Q, 1024), F32)],
        scratch_shapes=[pltpu.VMEM((Q, 1024), F32)], compiler_params=_cp("arbitrary"))(xbc, xbc, xbc, dt, alog)


def ssd_bwd(name, xbc, dt, alog, hs, dy, n, rev, dirn):
    chunk = _ssd_chunk(rev, dirn)

    def body(x_ref, b_ref, c_ref, dt_ref, al_ref, hs_ref, dy_ref, dx_ref, ddt_ref, dal_ref, dh_scr):
        tt = pl.program_id(0)
        ch = _chunk_of(n - 1 - tt, n, rev)

        @pl.when(tt == 0)
        def _():
            dh_scr[...] = jnp.zeros_like(dh_scr)

        xs = [x_ref[:, 128 * j:128 * (j + 1)] for j in range(8)]
        Bs = [b_ref[:, 128 * g:128 * (g + 1)] for g in range(2)]
        Cs = [c_ref[:, 128 * g:128 * (g + 1)] for g in range(2)]
        Hs = [hs_ref[0, :, 128 * j:128 * (j + 1)] for j in range(8)]
        live = (ch >= 2).astype(F32)
        dYs = [dy_ref[:, 128 * j:128 * (j + 1)] * live for j in range(8)]
        dHn = [dh_scr[:, 128 * j:128 * (j + 1)] for j in range(8)]
        _, vjp_fn = jax.vjp(chunk, xs, Bs, Cs, dt_ref[...], al_ref[...], Hs)
        dxs, dBs, dCs, ddt, dal, dHs = vjp_fn((dYs, dHn))
        for j in range(8):
            dx_ref[:, 128 * j:128 * (j + 1)] = dxs[j]
            dh_scr[:, 128 * j:128 * (j + 1)] = dHs[j]
        for g in range(2):
            dx_ref[:, 1024 + 128 * g:1024 + 128 * (g + 1)] = dBs[g]
            dx_ref[:, 1280 + 128 * g:1280 + 128 * (g + 1)] = dCs[g]
        ddt_ref[...] = ddt

        @pl.when(tt == 0)
        def _():
            dal_ref[...] = dal

        @pl.when(tt > 0)
        def _():
            dal_ref[...] += dal

    cm = lambda t: _chunk_of(n - 1 - t, n, rev)
    return pl.pallas_call(
        body, name=name, grid=(n,),
        in_specs=[pl.BlockSpec((Q, 1024), lambda t: (cm(t), 0)), pl.BlockSpec((Q, 256), lambda t: (cm(t), 4)),
                  pl.BlockSpec((Q, 256), lambda t: (cm(t), 5)), pl.BlockSpec((Q, 128), lambda t: (cm(t), 0)),
                  pl.BlockSpec((1, 128), lambda t: (0, 0)), pl.BlockSpec((1, Q, 1024), lambda t: (cm(t), 0, 0)),
                  pl.BlockSpec((Q, 1024), lambda t: (jnp.maximum(cm(t) - 2, 0), 0))],
        out_specs=[pl.BlockSpec((Q, 1536), lambda t: (cm(t), 0)), pl.BlockSpec((Q, 128), lambda t: (cm(t), 0)),
                   pl.BlockSpec((1, 128), lambda t: (0, 0))],
        out_shape=[jax.ShapeDtypeStruct((n * Q, 1536), F32), jax.ShapeDtypeStruct((n * Q, 128), F32),
                   jax.ShapeDtypeStruct((1, 128), F32)],
        scratch_shapes=[pltpu.VMEM((Q, 1024), F32)], compiler_params=_cp("arbitrary"))(xbc, xbc, xbc, dt, alog, hs, dy)


def f_norm0(x, g0, b0, sc, sh):
    x0 = _ln(x, g0, b0)
    return x0, x0 * (1.0 + sc) + sh


def f_dt(raw, bias):
    z = split4(raw)[0] + bias
    dt = jnp.maximum(z, 0.0) + jnp.log1p(jnp.exp(-jnp.abs(z)))
    return dt, dt


def f_gated_norm(yf, yb, xs, z, dcol, g):
    h = (yf + yb + xs * dcol) * _silu(z)
    return (h * lax.rsqrt(jnp.mean(h * h, axis=-1, keepdims=True) + LN_EPS) * g,)


def f_gmlp(uv, gmg, gmb, *wb):
    ws, bs = wb[:8], wb[8:]
    u, v = split2(uv)
    vn = split8(_ln(_gelu(v), gmg, gmb))
    mixed = concat8(tuple(mm(ws[g], vn[g]) + bs[g] for g in range(8)))
    return (_gelu(u) * mixed,)


def f_merge(ps, pg, gates, bg):
    gs, gg = split2(jax.nn.sigmoid(gates + bg))
    return (gs * ps + gg * pg,)


def f_res1(x0, out, g1, lg, lb, sc, sh):
    x1 = _ln(ALPHA * x0 + g1 * out, lg, lb)
    return x1, x1 * (1.0 + sc) + sh


def f_res2_loss(x1, ff, tgt, g2, lg, lb):
    x2 = _ln(ALPHA * x1 + g2 * ff, lg, lb)
    e = x2 - tgt
    return (0.5 * jnp.sum(jnp.mean(e * e, axis=-1, keepdims=True), axis=0, keepdims=True),)


def _row_tile(M):
    return 544 if M % 544 == 0 else (512 if M % 512 == 0 else M)


def _tie(v, token):
    return v if token is None else v + token[0:1, 0:1]


def core(x_all, tgt, mod_x, mod_c, X, S):
    R = x_all.shape[0]
    L = R - LC
    n = R // Q
    T = 256
    nt, ntl = R // T, L // T
    tmR, tmL = _row_tile(R), _row_tile(L)
    tkR = 256 if R % 512 else 512
    tkL = 512 if L % 512 == 0 else 256
    row = lambda v: v.reshape(1, -1)
    mx = [row(mod_x[k]) for k in range(6)]
    mc = [row(mod_c[k]) for k in range(6)]
    sel = lambda i: jnp.minimum(i, 1)
    sc1 = jnp.stack([mc[1], mx[1]])
    sh1 = jnp.stack([mc[0], mx[0]])
    ln0 = [(row(S['ln0_g']), None), (row(S['ln0_b']), None), (sc1, sel), (sh1, sel)]

    x0, xm = stage_fwd("norm0_fwd", f_norm0, T, nt, [(x_all, D, 0, 0)], ln0, [('new', R, D, 0), ('new', R, D, 0, BF16)])
    w_in, tok = X.w_in()
    proj = matmul("proj_fwd", xm, w_in, 'nn', tmR, PW // 2, 1024, after=tok)
    conv_w8 = jnp.pad(S['conv_w'], ((0, 3), (0, 0)))
    conv_b = row(S['conv_b'])
    xbc = conv_fwd("conv_fwd", proj, conv_w8, conv_b, R)
    dt_bias = jnp.pad(S['dt_bias'].reshape(1, 32), ((0, 0), (0, 96)))
    alog = jnp.pad(S['a_log'].reshape(1, 32), ((0, 0), (0, 96)))
    x_dt = [(proj, 512, C_DT // 512, 0)]
    dt_f, dt_b = stage_fwd("dt_fwd", f_dt, T, nt, x_dt, [(dt_bias, None)], [('new', R, 128, 0), ('new', R, 128, 0)])
    y_f, hs_f = ssd_fwd("ssd_fwd_f", xbc, dt_f, alog, n, False, 0)
    y_b, hs_b = ssd_fwd("ssd_fwd_b", xbc, dt_b, alog, n, True, 1)
    W = X.rest(hs_b)
    dcol = jnp.repeat(S['d_skip'][0] + S['d_skip'][1], 64).reshape(1, D)
    x_gn = [(y_f, D, 0, 1), (y_b, D, 0, 1), (xbc, D, 0, 1), (proj, D, C_Z // D, 1)]
    p_gn = [(dcol, None), (row(S['ssd_norm_g']), None)]
    (yn,) = stage_fwd("gnorm_fwd", f_gated_norm, T, ntl, x_gn, p_gn, [('new', L, D, 0, BF16)])
    x_gm = [(proj, 2 * D, C_UV // (2 * D), LC // Q)]
    p_gm = ([(row(S['gm_norm_g']), None), (row(S['gm_norm_b']), None)]
            + [(S['w_spatial'][g], None) for g in range(8)] + [(S['b_spatial'][g].reshape(Q, 1), None) for g in range(8)])
    (y_gm,) = stage_fwd("gmlp_fwd", f_gmlp, Q, L // Q, x_gm, p_gm, [('new', L, D, 0, BF16)])
    p_ssd = matmul("pssd_fwd", yn, W['w_ssd_proj'], 'nn', tmL, 1024, 1024)
    p_g = matmul("pgm_fwd", y_gm, W['w_gm_proj'], 'nn', tmL, 1024, 1024)
    x_mg = [(p_ssd, D, 0, 0), (p_g, D, 0, 0), (proj, 2 * D, C_GATE // (2 * D), 1)]
    p_mg = [(row(S['b_gate']), None)]
    (merged,) = stage_fwd("merge_fwd", f_merge, T, ntl, x_mg, p_mg, [('new', L, D, 0, BF16)])
    out = matmul("out_fwd", merged, W['w_out'], 'nn', tmL, 1024, 1024)
    x_r1 = [(x0, D, 0, 1), (out, D, 0, 0)]
    p_r1 = [(mx[2], None), (row(S['ln1_g']), None), (row(S['ln1_b']), None), (mx[4], None), (mx[3], None)]
    x1, hm = stage_fwd("res1_fwd", f_res1, T, ntl, x_r1, p_r1, [('new', L, D, 0), ('new', L, D, 0, BF16)])
    a1, a3, act = ffn_in_fwd("ffn_in_fwd", hm, W['w_ff1'], W['w_ff3'], T)
    ff = ff_out_fwd("ff2_fwd", act, W['w_ff2'], tmL)
    x_r2 = [(x1, D, 0, 0), (ff, D, 0, 0), (tgt, D, 0, 0)]
    p_r2 = [(mx[5], None), (row(S['ln2_g']), None), (row(S['ln2_b']), None)]

    dx1_a, dff, dg2, dl2g, dl2b, loss = stage_bwd(
        "res2_bwd", f_res2_loss, T, ntl, x_r2, p_r2, [1.0],
        [('new', L, D, 0), ('new', L, D, 0, BF16), None], [True, True, True], primal=[(0, (1, 1))])
    da1, da3 = ffn_out_bwd_x("ffn_out_bwd_x", dff, W['w_ff2'], a1, a3, T)
    gw_ff2 = ff_out_bwd_w("ff2_bwd_w", act, dff, tkL)
    dhm = ff_in_bwd_x("ff1_bwd_x", da1, W['w_ff1'], tmL)
    dhm = ff_in_bwd_x("ff3_bwd_x", da3, W['w_ff3'], tmL, add=dhm)
    gw_ff1 = ff_in_bwd_w("ff1_bwd_w", hm, da1, tkL)
    gw_ff3 = ff_in_bwd_w("ff3_bwd_w", hm, da3, tkL)
    tok = X.grads('ffn', {'w_ff2': gw_ff2, 'w_ff1': gw_ff1, 'w_ff3': gw_ff3})
    p_r1 = [(_tie(p_r1[0][0], tok), None)] + p_r1[1:]
    dx0_a, dout, dg1, dl1g, dl1b, dsc2, dsh2 = stage_bwd(
        "res1_bwd", f_res1, T, ntl, x_r1, p_r1, [(dx1_a, D, 0, 0), (dhm, D, 0, 0)],
        [('new', L, D, 0), ('new', L, D, 0, BF16)], [True] * 5)
    dmerged = matmul("out_bwd_x", dout, W['w_out'], 'nt', tmL, 1024, 1024)
    gw_out = matmul("out_bwd_w", merged, dout, 'tn', 1024, 1024, tkL, BF16)
    lt, lq = -(LC // T), -(LC // Q)
    x_mg_b = [(p_ssd, D, 0, lt), (p_g, D, 0, lt), (proj, 2 * D, C_GATE // (2 * D), 0)]
    dp_ssd, dp_g, dproj, dbg = stage_bwd(
        "merge_bwd", f_merge, T, nt, x_mg_b, p_mg, [(dmerged, D, 0, lt)],
        [('new', L, D, lt, BF16), ('new', L, D, lt, BF16), ('part', R, PW, 2 * D, C_GATE // (2 * D), 0, BF16)], [True])
    dyn = matmul("pssd_bwd_x", dp_ssd, W['w_ssd_proj'], 'nt', tmL, 1024, 1024)
    gw_ssd = matmul("pssd_bwd_w", yn, dp_ssd, 'tn', 1024, 1024, tkL, BF16)
    dy_gm = matmul("pgm_bwd_x", dp_g, W['w_gm_proj'], 'nt', tmL, 1024, 1024)
    gw_gm = matmul("pgm_bwd_w", y_gm, dp_g, 'tn', 1024, 1024, tkL, BF16)
    tok = X.grads('proj', {'w_out': gw_out, 'w_ssd_proj': gw_ssd, 'w_gm_proj': gw_gm})
    p_gm = [(_tie(p_gm[0][0], tok), None)] + p_gm[1:]
    r_gm = stage_bwd("gmlp_bwd", f_gmlp, Q, n, [(proj, 2 * D, C_UV // (2 * D), 0)], p_gm, [(dy_gm, D, 0, lq)],
                     [('alias', dproj, 2 * D, C_UV // (2 * D), 0)], [True] * 18)
    dproj, dgmg, dgmb, dws, dbs = r_gm[0], r_gm[1], r_gm[2], r_gm[3:11], r_gm[11:19]
    x_gn_b = [(y_f, D, 0, 0), (y_b, D, 0, 0), (xbc, D, 0, 0), (proj, D, C_Z // D, 0)]
    dy, dskipx, dproj, ddcol, dng = stage_bwd(
        "gnorm_bwd", f_gated_norm, T, nt, x_gn_b, p_gn, [(dyn, D, 0, lt)],
        [('new', L, D, lt), None, ('new', L, D, lt), ('alias', dproj, D, C_Z // D, 0)], [True, True])
    dxbc_f, ddt_f, dal_f = ssd_bwd("ssd_bwd_f", xbc, dt_f, alog, hs_f, dy, n, False, 0)
    dxbc_b, ddt_b, dal_b = ssd_bwd("ssd_bwd_b", xbc, dt_b, alog, hs_b, dy, n, True, 1)
    dproj, ddtb = stage_bwd("dt_bwd", f_dt, T, nt, x_dt, [(dt_bias, None)],
                            [(ddt_f, 128, 0, 0), (ddt_b, 128, 0, 0)],
                            [('alias', dproj, 512, C_DT // 512, 0)], [True])
    dproj, dcw8, dcb = conv_bwd("conv_bwd", proj, conv_w8, conv_b, dxbc_f, dxbc_b, dskipx, dproj, R)
    gw_in = matmul("proj_bwd_w", xm, dproj, 'tn', 1024, PW // 4, tkR, BF16)
    tok = X.grads('in', {'w_in': gw_in})
    dxm = matmul("proj_bwd_x", dproj, w_in, 'nt', R // 4 if R % 32 == 0 else R, 1024, 1024, after=tok)
    grad_x, dl0g, dl0b, dsc1, dsh1 = stage_bwd(
        "norm0_bwd", f_norm0, T, nt, [(x_all, D, 0, 0)], ln0, [(dx0_a, D, 0, -1), (dxm, D, 0, 0)],
        [('new', L, D, -1)], [True] * 4)

    zero = jnp.zeros((D,), F32)
    flat = lambda v: v.reshape(-1)
    small = {
        'loss': flat(loss), 'ln0_g': flat(dl0g), 'ln0_b': flat(dl0b),
        'dmod_x': jnp.concatenate([flat(dsh1[1]), flat(dsc1[1]), flat(dg1), flat(dsh2), flat(dsc2), flat(dg2)]),
        'dmod_c': jnp.concatenate([flat(dsh1[0]), flat(dsc1[0]), zero, zero, zero, zero]),
        'conv_w': flat(dcw8[:5]), 'conv_b': flat(dcb), 'dt_bias': flat(ddtb[:, :32]),
        'a_log': flat((dal_f + dal_b)[:, :32]),
        'd_skip': flat(jnp.tile(ddcol.reshape(1, NH, 64).sum(-1), (2, 1))),
        'ssd_norm_g': flat(dng), 'gm_norm_g': flat(dgmg), 'gm_norm_b': flat(dgmb),
        'w_spatial': flat(jnp.stack(dws)), 'b_spatial': flat(jnp.stack(dbs)), 'b_gate': flat(dbg),
        'ln1_g': flat(dl1g), 'ln1_b': flat(dl1b), 'ln2_g': flat(dl2g), 'ln2_b': flat(dl2b),
    }
    return grad_x, small


def _place():
    return lax.axis_index("x"), lax.axis_index("y"), lax.axis_index("c")


def allgather8(name, blk, hbm):
    space = pl.ANY if hbm else pltpu.VMEM

    def body(x_ref, out_ref, send_sems, recv_sems, local_sem):
        x, y, c = _place()
        me, sibling = (x, y, c), (x, y, 1 - c)
        chips = [(1 - x, y), (x, 1 - y), (1 - x, 1 - y)]

        def slot(px, py, pc):
            return out_ref.at[4 * px + 2 * py + pc]

        def copy(k, block, to, src=None):
            return pltpu.make_async_remote_copy(
                src_ref=slot(*block) if src is None else src, dst_ref=slot(*block),
                send_sem=send_sems.at[k], recv_sem=recv_sems.at[k], device_id=to, device_id_type=MESH)

        mine = pltpu.make_async_copy(x_ref, slot(*me), local_sem)
        mine.start()
        first = [copy(0, me, sibling, src=x_ref)]
        first += [copy(1 + j, me, (*chip, c), src=x_ref) for j, chip in enumerate(chips)]
        for cp in first:
            cp.start()
        passed = [copy(4 + j, (*chip, c), sibling) for j, chip in enumerate(chips)]
        for j, chip in enumerate(chips):
            copy(1 + j, (*chip, c), me).wait_recv()
            passed[j].start()
        copy(0, sibling, me).wait_recv()
        for j, chip in enumerate(chips):
            copy(4 + j, (*chip, 1 - c), me).wait_recv()
        for cp in first + passed:
            cp.wait_send()
        mine.wait()

    return pl.pallas_call(
        body, name=name, out_shape=jax.ShapeDtypeStruct((8,) + blk.shape, blk.dtype),
        in_specs=[pl.BlockSpec(memory_space=space)], out_specs=pl.BlockSpec(memory_space=space),
        scratch_shapes=[pltpu.SemaphoreType.DMA((7,)), pltpu.SemaphoreType.DMA((7,)), pltpu.SemaphoreType.DMA],
        compiler_params=pltpu.CompilerParams(vmem_limit_bytes=VMEM_LIMIT_V7X))(blk)


def gather_weights(name, blks):
    n = len(blks)

    def body(*refs):
        ins, outs = refs[:n], refs[n:2 * n]
        send_sems, recv_sems, local_sems = refs[2 * n:]
        x, y, c = _place()
        me, sibling = (x, y, c), (x, y, 1 - c)
        chips = [(1 - x, y), (x, 1 - y), (1 - x, 1 - y)]

        def copy(a, k, block, to, own=False):
            dst = outs[a].at[4 * block[0] + 2 * block[1] + block[2]]
            return pltpu.make_async_remote_copy(
                src_ref=ins[a] if own else dst, dst_ref=dst, send_sem=send_sems.at[7 * a + k],
                recv_sem=recv_sems.at[7 * a + k], device_id=to, device_id_type=MESH)

        mine = [pltpu.make_async_copy(ins[a], outs[a].at[4 * x + 2 * y + c], local_sems.at[a]) for a in range(n)]
        for cp in mine:
            cp.start()
        first = []
        for j, chip in enumerate(chips):
            first += [copy(a, 1 + j, me, (*chip, c), own=True) for a in range(n)]
        first += [copy(a, 0, me, sibling, own=True) for a in range(n)]
        for cp in first:
            cp.start()
        passed = []
        for a in range(n):
            for j, chip in enumerate(chips):
                copy(a, 1 + j, (*chip, c), me).wait_recv()
                passed.append(copy(a, 4 + j, (*chip, c), sibling))
                passed[-1].start()
        for a in range(n):
            copy(a, 0, sibling, me).wait_recv()
            for j, chip in enumerate(chips):
                copy(a, 4 + j, (*chip, 1 - c), me).wait_recv()
        for cp in first + passed:
            cp.wait_send()
        for cp in mine:
            cp.wait()

    any_spec = pl.BlockSpec(memory_space=pl.ANY)
    return pl.pallas_call(
        body, name=name, out_shape=[jax.ShapeDtypeStruct((8,) + b.shape, b.dtype) for b in blks],
        in_specs=[any_spec] * n, out_specs=[any_spec] * n,
        scratch_shapes=[pltpu.SemaphoreType.DMA((7 * n,)), pltpu.SemaphoreType.DMA((7 * n,)),
                        pltpu.SemaphoreType.DMA((n,))])(*blks)


_HBM = pl.BlockSpec(memory_space=pltpu.HBM)
_SEM = pl.BlockSpec(memory_space=pltpu.SEMAPHORE)
_DATAFLOW = pltpu.SideEffectType.DATAFLOW_SIDE_EFFECTING


def _peers(place):
    x, y, c = place
    return [((1 - x) if k & 4 else x, (1 - y) if k & 2 else y, (1 - c) if k & 1 else c) for k in range(1, 8)]


def _slot(p):
    return 4 * p[0] + 2 * p[1] + p[2]


def plan_gather(place, srcs, lands):
    remote = [(s, l.at[_slot(place)], to) for s, l in zip(srcs, lands) for to in _peers(place)]
    return remote, [(s, l.at[_slot(place)]) for s, l in zip(srcs, lands)]


def plan_to_owner(place, srcs, lands):
    remote = [(s.at[2 * to[0] + to[1], to[2]], l.at[_slot(place)], to) for s, l in zip(srcs, lands) for to in _peers(place)]
    x, y, c = place
    return remote, [(s.at[2 * x + y, c], l.at[_slot(place)]) for s, l in zip(srcs, lands)]


def sequencer_exchange(name, collective_id, srcs, land_shapes, plan):
    n = len(srcs)
    src_refs = [jax.new_ref(a, memory_space=pltpu.MemorySpace.HBM) for a in srcs]
    land_refs = [jax.empty_ref(s, memory_space=pltpu.MemorySpace.HBM) for s in land_shapes]

    @pl.kernel(mesh=plsc.ScalarSubcoreMesh(axis_name="sequencer", num_cores=1), name=name,
               scratch_types=(pltpu.SemaphoreType.DMA((7 * n,)), pltpu.SemaphoreType.DMA((7 * n,)),
                              pltpu.SemaphoreType.DMA((n,))),
               compiler_params=pltpu.CompilerParams(collective_id=collective_id))
    def launch(send_sems, recv_sems, local_sems):
        place = _place()
        barrier = pltpu.get_barrier_semaphore()
        for to in _peers(place):
            pl.semaphore_signal(barrier, inc=1, device_id=to, device_id_type=MESH)
        pl.semaphore_wait(barrier, 7)
        remote, local = plan(place, src_refs, land_refs)
        mine = [pltpu.make_async_copy(s, d, local_sems.at[a]) for a, (s, d) in enumerate(local)]
        for cp in mine:
            cp.start()
        cps = [pltpu.make_async_remote_copy(src_ref=s, dst_ref=d, send_sem=send_sems.at[k], recv_sem=recv_sems.at[k],
                                            device_id=to, device_id_type=MESH) for k, (s, d, to) in enumerate(remote)]
        for cp in cps:
            cp.start()
        for cp in mine:
            cp.wait()
        for cp in cps:
            cp.wait()

    launch()
    return land_refs


def _hbm(a):
    return pltpu.with_memory_space_constraint(a, pltpu.HBM)


def split_start(name, srcs, land_shapes, plan, after):
    n = len(srcs)

    def body(*refs):
        src, land = refs[:n], refs[n:2 * n]
        send_sems, recv_sems = refs[2 * n + 2], refs[2 * n + 3]
        local_sems = refs[-1]
        remote, local = plan(_place(), src, land)
        cps = [pltpu.make_async_copy(s, d, local_sems.at[a]) for a, (s, d) in enumerate(local)]
        for cp in cps:
            cp.start()
        for cp in cps:
            cp.wait()
        for k, (s, d, to) in enumerate(remote):
            pltpu.make_async_remote_copy(src_ref=s, dst_ref=d, send_sem=send_sems.at[k], recv_sem=recv_sems.at[k],
                                         device_id=to, device_id_type=MESH).start()

    lands = [lax.empty(s.shape, s.dtype) for s in land_shapes]
    token = jnp.zeros((8, 128), F32)
    thru = [pltpu.HBM(a.shape, a.dtype) for a in list(srcs) + lands + [token]]
    res = pl.pallas_call(
        body, name=name,
        out_shape=(pltpu.SemaphoreType.DMA((7 * n,)), pltpu.SemaphoreType.DMA((7 * n,)), *thru),
        in_specs=[_HBM] * (2 * n + 1) + [pl.BlockSpec(memory_space=pl.ANY)],
        out_specs=(_SEM, _SEM, *([_HBM] * (2 * n + 1))),
        input_output_aliases={i: 2 + i for i in range(2 * n + 1)},
        scratch_shapes=[pltpu.SemaphoreType.DMA((n,))],
        compiler_params=pltpu.CompilerParams(has_side_effects=_DATAFLOW),
    )(*[_hbm(a) for a in srcs], *[_hbm(a) for a in lands], _hbm(token), after)
    return (n, plan, res[0], res[1], res[2:2 + 2 * n]), res[-1]


def split_wait(name, state, after):
    n, plan, send_sems, recv_sems, thru = state

    def body(*refs):
        src, land = refs[:n], refs[n:2 * n]
        send_ref, recv_ref = refs[2 * n], refs[2 * n + 1]
        remote, _ = plan(_place(), src, land)
        for k, (s, d, to) in enumerate(remote):
            cp = pltpu.make_async_remote_copy(src_ref=s, dst_ref=d, send_sem=send_ref.at[k], recv_sem=recv_ref.at[k],
                                              device_id=to, device_id_type=MESH)
            cp.wait_send()
            cp.wait_recv()

    res = pl.pallas_call(
        body, name=name, out_shape=tuple(pltpu.HBM(a.shape, a.dtype) for a in thru),
        in_specs=[_HBM] * (2 * n) + [_SEM, _SEM, pl.BlockSpec(memory_space=pl.ANY)], out_specs=tuple([_HBM] * (2 * n)),
        input_output_aliases={i: i for i in range(2 * n)},
        compiler_params=pltpu.CompilerParams(has_side_effects=_DATAFLOW),
    )(*thru, send_sems, recv_sems, after)
    return res[n:]


def sibling_pair(name, hs):
    n = len(hs)

    def body(*refs):
        ins, outs = refs[:n], refs[n:2 * n]
        send_sems, recv_sems = refs[2 * n:]
        x, y, c = _place()
        cps = [pltpu.make_async_remote_copy(src_ref=outs[a].at[c], dst_ref=outs[a].at[c], send_sem=send_sems.at[a],
                                            recv_sem=recv_sems.at[a], device_id=(x, y, 1 - c), device_id_type=MESH)
               for a in range(n)]
        for cp in cps:
            cp.start()
        for a in range(n):
            pltpu.make_async_remote_copy(src_ref=outs[a].at[1 - c], dst_ref=outs[a].at[1 - c], send_sem=send_sems.at[a],
                                         recv_sem=recv_sems.at[a], device_id=(x, y, 1 - c),
                                         device_id_type=MESH).wait_recv()
        for cp in cps:
            cp.wait_send()

    any_spec = pl.BlockSpec(memory_space=pl.ANY)
    return pl.pallas_call(
        body, name=name, out_shape=[jax.ShapeDtypeStruct(h.shape, h.dtype) for h in hs],
        in_specs=[any_spec] * n, out_specs=[any_spec] * n, input_output_aliases={a: a for a in range(n)},
        scratch_shapes=[pltpu.SemaphoreType.DMA((n,)), pltpu.SemaphoreType.DMA((n,))])(*hs)


def owner_sum(name, land):
    _, r, w = land.shape
    T = r // 2

    def body(_, l_ref, o_ref):
        acc = l_ref[0].astype(F32)
        for j in range(1, 8):
            acc = acc + l_ref[j].astype(F32)
        o_ref[...] = acc

    grid_spec = pltpu.PrefetchScalarGridSpec(
        num_scalar_prefetch=1, grid=(2,),
        in_specs=[pl.BlockSpec((8, T, w), lambda i, at: (0, i, 0))],
        out_specs=pl.BlockSpec((None, T, w), lambda i, at: (at[0], i, 0)))
    at = jnp.stack([lax.axis_index("c")]).astype(jnp.int32)
    return pl.pallas_call(body, name=name, grid_spec=grid_spec, out_shape=jax.ShapeDtypeStruct((2, r, w), F32),
                          compiler_params=_cp("arbitrary"))(at, land)


W_IN_RUNS = ((0, 2, 1296, 376), (376, 3, 0, 1672), (2048, 1, 920, 752), (2800, 2, 0, 1296), (4096, 0, 0, 1024),
             (5120, 0, 1024, 648), (5768, 1, 0, 920))


def w_in_to_padded(name, g4):
    T = 128

    def body(g_ref, o_ref):
        o_ref[:, D_PROJ:PW] = jnp.zeros((T, PW - D_PROJ), o_ref.dtype)
        for (a, s, j0, w) in W_IN_RUNS:
            o_ref[:, a:a + w] = g_ref[s, :, j0:j0 + w]

    return pl.pallas_call(body, name=name, grid=(D // T,), in_specs=[pl.BlockSpec((4, T, 1672), lambda i: (0, i, 0))],
                          out_specs=pl.BlockSpec((T, PW), lambda i: (i, 0)),
                          out_shape=jax.ShapeDtypeStruct((D, PW), g4.dtype), compiler_params=_cp("arbitrary"))(g4)


def w_in_from_padded(name, gp):
    T = 128

    def body(g_ref, o_ref):
        for (a, s, j0, w) in W_IN_RUNS:
            o_ref[s, :, j0:j0 + w] = g_ref[:, a:a + w]

    return pl.pallas_call(body, name=name, grid=(D // T,), in_specs=[pl.BlockSpec((T, PW), lambda i: (i, 0))],
                          out_specs=pl.BlockSpec((4, T, 1672), lambda i: (0, i, 0)),
                          out_shape=jax.ShapeDtypeStruct((4, D, 1672), gp.dtype), compiler_params=_cp("arbitrary"))(gp)


def sum_devices(name, g):
    def body(g_ref, o_ref):
        acc = g_ref[0]
        for k in range(1, 8):
            acc = acc + g_ref[k]
        o_ref[...] = acc

    return pl.pallas_call(body, name=name, out_shape=jax.ShapeDtypeStruct(g.shape[1:], F32),
                          compiler_params=pltpu.CompilerParams(vmem_limit_bytes=VMEM_LIMIT_V7X))(g)


def adamw(name, w, g, m, v, T):
    r, wd = w.shape
    c1 = 1.0 - ADAM_B1 ** ADAM_STEP
    c2 = 1.0 - ADAM_B2 ** ADAM_STEP

    def body(w_ref, g_ref, m_ref, v_ref, d_ref, mo_ref, vo_ref):
        gv = g_ref[...]
        mn = ADAM_B1 * m_ref[...] + (1.0 - ADAM_B1) * gv
        vn = ADAM_B2 * v_ref[...] + (1.0 - ADAM_B2) * (gv * gv)
        d_ref[...] = -ADAM_LR * ((mn / c1) / (jnp.sqrt(vn / c2) + ADAM_EPS) + ADAM_WD * w_ref[...])
        mo_ref[...] = mn
        vo_ref[...] = vn

    spec = pl.BlockSpec((T, wd), lambda i: (i, 0))
    return pl.pallas_call(body, name=name, grid=(r // T,), in_specs=[spec] * 4, out_specs=[spec] * 3,
                          out_shape=[jax.ShapeDtypeStruct((r, wd), F32)] * 3, compiler_params=_cp("arbitrary"))(w, g, m, v)


BIG = {'w_in': (1024, 1672), 'w_ssd_proj': (256, 1024), 'w_gm_proj': (256, 1024), 'w_out': (256, 1024),
       'w_ff1': (1024, 704), 'w_ff3': (1024, 704), 'w_ff2': (704, 1024)}


class Flat:
    def __init__(self, segs):
        self.off, o = {}, 0
        for name, size in segs:
            self.off[name] = (o, size)
            o += -(-size // 128) * 128
        self.rows = -(-o // 1024) * 8

    def pack(self, vals):
        parts = []
        for name, (o, size) in self.off.items():
            v = vals[name].reshape(-1).astype(F32)
            parts.append(jnp.pad(v, (0, -(-size // 128) * 128 - size)))
        buf = jnp.concatenate(parts)
        return jnp.pad(buf, (0, self.rows * 128 - buf.shape[0])).reshape(self.rows, 128)

    def get(self, buf, name, shape=None):
        o, size = self.off[name]
        v = buf[o // 128:(o + size + 127) // 128].reshape(-1)[:size]
        return v if shape is None else v.reshape(shape)


PARTIALS = Flat([('loss', 1), ('ln0_g', D), ('ln0_b', D), ('dmod_x', 6 * D), ('dmod_c', 6 * D), ('conv_w', 5 * 1536),
                 ('conv_b', 1536), ('dt_bias', 32), ('a_log', 32), ('d_skip', 32), ('ssd_norm_g', D),
                 ('gm_norm_g', D), ('gm_norm_b', D), ('w_spatial', 8 * Q * Q), ('b_spatial', 8 * Q), ('b_gate', 2 * D),
                 ('ln1_g', D), ('ln1_b', D), ('ln2_g', D), ('ln2_b', D)])

WEIGHTS = ('c_ctx', 'ln0_g', 'ln0_b', 'w_ada', 'b_ada', 'w_in', 'conv_w', 'conv_b', 'dt_bias', 'a_log', 'd_skip',
           'ssd_norm_g', 'gm_norm_g', 'gm_norm_b', 'w_spatial', 'b_spatial', 'b_gate', 'w_ssd_proj', 'w_gm_proj',
           'w_out', 'ln1_g', 'ln1_b', 'w_ff1', 'w_ff3', 'w_ff2', 'ln2_g', 'ln2_b')
BIG_NAMES = tuple(BIG)
SMALL_NAMES = tuple(n for n in WEIGHTS if n not in BIG_NAMES and n != 'w_ada')


def kernel(x, c, ctx, c_ctx, ln0_g, ln0_b, w_ada, b_ada, w_in, conv_w, conv_b, dt_bias, a_log, d_skip, ssd_norm_g, gm_norm_g, gm_norm_b, w_spatial, b_spatial, b_gate, w_ssd_proj, w_gm_proj, w_out, ln1_g, ln1_b, w_ff1, w_ff3, w_ff2, ln2_g, ln2_b, loss_target, m_c_ctx, m_ln0_g, m_ln0_b, m_w_ada, m_b_ada, m_w_in, m_conv_w, m_conv_b, m_dt_bias, m_a_log, m_d_skip, m_ssd_norm_g, m_gm_norm_g, m_gm_norm_b, m_w_spatial, m_b_spatial, m_b_gate, m_w_ssd_proj, m_w_gm_proj, m_w_out, m_ln1_g, m_ln1_b, m_w_ff1, m_w_ff3, m_w_ff2, m_ln2_g, m_ln2_b, v_c_ctx, v_ln0_g, v_ln0_b, v_w_ada, v_b_ada, v_w_in, v_conv_w, v_conv_b, v_dt_bias, v_a_log, v_d_skip, v_ssd_norm_g, v_gm_norm_g, v_gm_norm_b, v_w_spatial, v_b_spatial, v_b_gate, v_w_ssd_proj, v_w_gm_proj, v_w_out, v_ln1_g, v_ln1_b, v_w_ff1, v_w_ff3, v_w_ff2, v_ln2_g, v_ln2_b):
    wts = dict(c_ctx=c_ctx, ln0_g=ln0_g, ln0_b=ln0_b, w_ada=w_ada, b_ada=b_ada, w_in=w_in, conv_w=conv_w, conv_b=conv_b,
               dt_bias=dt_bias, a_log=a_log, d_skip=d_skip, ssd_norm_g=ssd_norm_g, gm_norm_g=gm_norm_g,
               gm_norm_b=gm_norm_b, w_spatial=w_spatial, b_spatial=b_spatial, b_gate=b_gate, w_ssd_proj=w_ssd_proj,
               w_gm_proj=w_gm_proj, w_out=w_out, ln1_g=ln1_g, ln1_b=ln1_b, w_ff1=w_ff1, w_ff3=w_ff3, w_ff2=w_ff2,
               ln2_g=ln2_g, ln2_b=ln2_b)
    ms = dict(zip(WEIGHTS, (m_c_ctx, m_ln0_g, m_ln0_b, m_w_ada, m_b_ada, m_w_in, m_conv_w, m_conv_b, m_dt_bias, m_a_log,
                            m_d_skip, m_ssd_norm_g, m_gm_norm_g, m_gm_norm_b, m_w_spatial, m_b_spatial, m_b_gate,
                            m_w_ssd_proj, m_w_gm_proj, m_w_out, m_ln1_g, m_ln1_b, m_w_ff1, m_w_ff3, m_w_ff2, m_ln2_g,
                            m_ln2_b)))
    vs = dict(zip(WEIGHTS, (v_c_ctx, v_ln0_g, v_ln0_b, v_w_ada, v_b_ada, v_w_in, v_conv_w, v_conv_b, v_dt_bias, v_a_log,
                            v_d_skip, v_ssd_norm_g, v_gm_norm_g, v_gm_norm_b, v_w_spatial, v_b_spatial, v_b_gate,
                            v_w_ssd_proj, v_w_gm_proj, v_w_out, v_ln1_g, v_ln1_b, v_w_ff1, v_w_ff3, v_w_ff2, v_ln2_g,
                            v_ln2_b)))
    px, py, pc = _place()
    shard = 2 * px + py
    dev = 2 * shard + pc
    take = lambda a, i, axis=0: lax.dynamic_index_in_dim(a, i, axis, keepdims=False)

    pre = jnp.concatenate([c, jnp.pad(conv_w[0], ((0, 0), (0, D - 384))), jnp.zeros((2, D), F32)], axis=0)
    pre = allgather8("gather_cond", pre, False)
    conv_w_full = pre[0::2, 1:6, :384].transpose(1, 0, 2).reshape(5, 1536)
    a16 = jnp.concatenate([_silu(pre[:, 0, :]), _silu(c_ctx)[None], jnp.zeros((7, D), F32)], axis=0)
    mod = matmul("ada_fwd", a16, w_ada[0], 'nn', 16, 512, 1024)
    mod = mod + lax.dynamic_slice_in_dim(b_ada[0], shard * 1536, 1536)[None]
    mod = allgather8("gather_mod", mod, False)
    mod = jnp.concatenate([mod[0], mod[2], mod[4], mod[6]], axis=1)
    mod_x = take(mod, dev).reshape(6, D)
    mod_c = mod[8].reshape(6, D)

    half = lambda n: take(wts[n][0].reshape(2, BIG[n][0] // 2, BIG[n][1]), pc).astype(BF16)

    def full(n, blocks):
        r, w = BIG[n]
        return blocks.reshape(4, r, w) if w != D else blocks.reshape(4 * r, w)

    class Exchanges:
        rest_names = BIG_NAMES[1:]

        def __init__(self):
            self.pending = []

        def w_in(self):
            (blocks,) = gather_weights("gather_w_in", [half('w_in')])
            w = w_in_to_padded("w_in_layout", full('w_in', blocks))
            halves = [half(n) for n in self.rest_names]
            lands = [jax.ShapeDtypeStruct((8,) + h.shape, BF16) for h in halves]
            self.rest_refs = sequencer_exchange("gather_rest", 1, halves, lands, plan_gather)
            return w, None

        def rest(self, after):
            return {n: full(n, r[...]) for n, r in zip(self.rest_names, self.rest_refs)}

        def grads(self, group, gs):
            if group == 'in':
                gs = {'w_in': w_in_from_padded("w_in_grad_layout", gs['w_in'])}
            blocks = [g.reshape(4, 2, BIG[n][0] // 2, BIG[n][1]) for n, g in gs.items()]
            lands = [jax.ShapeDtypeStruct((8,) + b.shape[2:], BF16) for b in blocks]
            refs = sequencer_exchange("grads_" + group, 2 + len(self.pending), blocks, lands, plan_to_owner)
            self.pending.append((tuple(gs), refs))
            return None

        def finish(self):
            names, halves = [], []
            for ns, refs in self.pending:
                names += ns
                halves += [owner_sum("grads_sum_" + n, r[...]) for n, r in zip(ns, refs)]
            return {n: h.reshape(BIG[n]) for n, h in zip(names, sibling_pair("grads_halves", halves))}

    S = dict(ln0_g=ln0_g, ln0_b=ln0_b, conv_w=conv_w_full, conv_b=conv_b[0], dt_bias=dt_bias[0], a_log=a_log[0],
             d_skip=d_skip[0], ssd_norm_g=ssd_norm_g[0], gm_norm_g=gm_norm_g[0], gm_norm_b=gm_norm_b[0],
             w_spatial=w_spatial[0], b_spatial=b_spatial[0], b_gate=b_gate[0], ln1_g=ln1_g[0], ln1_b=ln1_b[0],
             ln2_g=ln2_g[0], ln2_b=ln2_b[0])
    x_all = jnp.concatenate([ctx[0], x[0]], axis=0)
    exchanges = Exchanges()
    grad_x, gsmall = core(x_all, loss_target[0], mod_x, mod_c, exchanges, S)

    parts = allgather8("gather_partials", PARTIALS.pack(gsmall), False)
    tot = sum_devices("partials_sum", parts)
    g_shards = exchanges.finish()
    g = {n: PARTIALS.get(tot, n) for n in ('ln0_g', 'ln0_b', 'conv_b', 'dt_bias', 'a_log', 'd_skip', 'ssd_norm_g',
                                           'gm_norm_g', 'gm_norm_b', 'w_spatial', 'b_spatial', 'b_gate', 'ln1_g',
                                           'ln1_b', 'ln2_g', 'ln2_b')}
    loss = PARTIALS.get(tot, 'loss', ())
    dmod_c = PARTIALS.get(tot, 'dmod_c')
    g['b_ada'] = PARTIALS.get(tot, 'dmod_x') + dmod_c
    g['conv_w'] = lax.dynamic_slice_in_dim(PARTIALS.get(tot, 'conv_w', (5, 1536)), shard * 384, 384, axis=1)
    o, size = PARTIALS.off['dmod_x']
    dmod_rows = parts[:, o // 128:(o + size) // 128].reshape(8, size)
    dm = jnp.concatenate([dmod_rows, dmod_c[None], jnp.zeros((7, 6 * D), F32)], axis=0)
    dm = lax.dynamic_slice_in_dim(dm, shard * 1536, 1536, axis=1)
    g['w_ada'] = matmul("ada_bwd_w", a16, dm, 'tn', 1024, 512, 16)
    dm_c = jnp.concatenate([dm[8:9], jnp.zeros((15, 1536), F32)], axis=0)
    dc = matmul("ada_bwd_c", dm_c, w_ada[0], 'nt', 16, 1024, 512)
    dc = allgather8("gather_dcctx", dc, False)[:, 0, :]
    dc = ((dc[0] + dc[2]) + dc[4]) + dc[6]
    sg = jax.nn.sigmoid(c_ctx)
    g['c_ctx'] = dc * (sg * (1.0 + c_ctx * (1.0 - sg)))
    for n in BIG_NAMES:
        g[n] = g_shards[n]

    delta, new_m, new_v = {}, {}, {}
    for n in BIG_NAMES + ('w_ada',):
        w2 = wts[n][0]
        T = 352 if n == 'w_ff2' else 256
        d_, m_, v_ = adamw("adamw_" + n, w2, g[n], ms[n][0], vs[n][0], T)
        delta[n], new_m[n], new_v[n] = d_, m_, v_
    lay = Flat([(n, wts[n].size) for n in SMALL_NAMES])
    d_, m_, v_ = adamw("adamw_small", lay.pack(wts), lay.pack(g), lay.pack(ms), lay.pack(vs), lay.rows)
    for n in SMALL_NAMES:
        delta[n], new_m[n], new_v[n] = (lay.get(b, n) for b in (d_, m_, v_))

    shp = lambda d: [d[n].reshape(wts[n].shape) for n in WEIGHTS]
    return (loss, grad_x[None], *shp(g), *shp(delta), *shp(new_m), *shp(new_v))
```

```python
import functools

import jax
import jax.numpy as jnp
from jax import lax
from jax.experimental import pallas as pl
from jax.experimental.pallas import tpu as pltpu
from jax.experimental.pallas import tpu_sc as plsc

F32 = jnp.float32
BF16 = jnp.bfloat16
MESH = pl.DeviceIdType.MESH

VMEM_LIMIT_V7X = 56 * 1024 * 1024

D = 1024
LC = 256
Q = 128
NH = 16
D_FF = 2816
LN_EPS = 1e-5
ALPHA = 2.0 ** 0.25

PW = 7168
C_GATE, C_UV, C_Z, C_XBC, C_DT = 0, 2048, 4096, 5120, 6656
D_PROJ = 6688

ADAM_LR, ADAM_B1, ADAM_B2, ADAM_EPS, ADAM_WD, ADAM_STEP = 0.001, 0.9, 0.999, 1e-08, 0.01, 10


def _cp(*sem):
    return pltpu.CompilerParams(dimension_semantics=sem, vmem_limit_bytes=VMEM_LIMIT_V7X)


def _dot(a, b, ca, cb):
    return lax.dot_general(a.astype(BF16), b.astype(BF16), (((ca,), (cb,)), ((), ())),
                           preferred_element_type=F32)


@jax.custom_vjp
def mm(a, b):
    return _dot(a, b, 1, 0)


mm.defvjp(lambda a, b: (_dot(a, b, 1, 0), (a, b)),
          lambda r, g: (_dot(g, r[1], 1, 1), _dot(r[0], g, 0, 0)))


@jax.custom_vjp
def mm_nt(a, b):
    return _dot(a, b, 1, 1)


mm_nt.defvjp(lambda a, b: (_dot(a, b, 1, 1), (a, b)),
             lambda r, g: (_dot(g, r[1], 1, 0), _dot(g, r[0], 0, 0)))


@jax.custom_vjp
def mm_tn(a, b):
    return _dot(a, b, 0, 0)


mm_tn.defvjp(lambda a, b: (_dot(a, b, 0, 0), (a, b)),
             lambda r, g: (_dot(r[1], g, 1, 1), _dot(r[0], g, 1, 0)))


def _dot32(a, b):
    return lax.dot_general(a, b, (((1,), (0,)), ((), ())), precision=lax.Precision.HIGHEST,
                           preferred_element_type=F32)


def _cumsum_fn(rev):
    def tri(transpose):
        r = lax.broadcasted_iota(jnp.int32, (Q, Q), 0)
        c = lax.broadcasted_iota(jnp.int32, (Q, Q), 1)
        keep = (r >= c) if (rev == transpose) else (r <= c)
        return jnp.where(keep, 1.0, 0.0).astype(F32)

    @jax.custom_vjp
    def cums(a):
        return _dot32(tri(False), a)

    cums.defvjp(lambda a: (_dot32(tri(False), a), None), lambda _, g: (_dot32(tri(True), g),))
    return cums


def _cols(v, k):
    w = v.shape[1] // k
    return tuple(v[:, w * i:w * (i + 1)] for i in range(k))


def _splitter(k):
    @jax.custom_vjp
    def split(v):
        return _cols(v, k)

    @jax.custom_vjp
    def concat(ps):
        return jnp.concatenate(ps, axis=1)

    split.defvjp(lambda v: (_cols(v, k), None), lambda _, g: (jnp.concatenate(g, axis=1),))
    concat.defvjp(lambda ps: (jnp.concatenate(ps, axis=1), None), lambda _, g: (_cols(g, k),))
    return split, concat


split2, _ = _splitter(2)
split4, _ = _splitter(4)
split8, concat8 = _splitter(8)


def _ln(x, g, b):
    mu = jnp.mean(x, axis=-1, keepdims=True)
    xc = x - mu
    var = jnp.mean(xc * xc, axis=-1, keepdims=True)
    return xc * lax.rsqrt(var + LN_EPS) * g + b


def _silu(x):
    return x * jax.nn.sigmoid(x)


def _gelu(x):
    return 0.5 * x * (1.0 + jnp.tanh(0.7978845608028654 * (x + 0.044715 * (x * x * x))))


def _xspec(T, w, col, roff):
    return pl.BlockSpec((T, w), lambda i, col=col, roff=roff: (jnp.maximum(i + roff, 0), col))


def _pspec(p, sel):
    if sel is None:
        return pl.BlockSpec(p.shape, lambda i, n=p.ndim: (0,) * n)
    return pl.BlockSpec((1,) + p.shape[1:], lambda i, n=p.ndim: (sel(i),) + (0,) * (n - 1))


def _out_plumbing(outs, T, args, in_specs):
    shapes, specs, aliases = [], [], {}
    for k, o in enumerate(outs):
        if o[0] == 'new':
            _, rows, w, roff = o[:4]
            shapes.append(jax.ShapeDtypeStruct((rows, w), o[4] if len(o) > 4 else F32))
            specs.append(_xspec(T, w, 0, roff))
        elif o[0] == 'acc':
            shapes.append(jax.ShapeDtypeStruct(o[1], F32))
            specs.append(pl.BlockSpec(o[1], lambda i, n=len(o[1]): (0,) * n))
        elif o[0] == 'part':
            _, rows, wtot, w, col, roff, dtype = o
            shapes.append(jax.ShapeDtypeStruct((rows, wtot), dtype))
            specs.append(_xspec(T, w, col, roff))
        else:
            _, arr, w, col, roff = o
            aliases[len(args)] = k
            args.append(arr)
            in_specs.append(pl.BlockSpec(memory_space=pl.ANY))
            shapes.append(jax.ShapeDtypeStruct(arr.shape, arr.dtype))
            specs.append(_xspec(T, w, col, roff))
    return shapes, specs, aliases


def stage_fwd(name, f, T, n, xs, ps, outs):
    nx, npar = len(xs), len(ps)
    args = [x[0] for x in xs] + [p[0] for p in ps]
    in_specs = [_xspec(T, w, col, roff) for (_, w, col, roff) in xs] + [_pspec(p, sel) for (p, sel) in ps]
    n_in = len(args)
    shapes, specs, aliases = _out_plumbing(outs, T, args, in_specs)
    n_all_in = len(args)

    def body(*refs):
        i = pl.program_id(0)
        xv = [r[...] for r in refs[:nx]]
        pv = [r[...] if ps[k][1] is None else r[0] for k, r in enumerate(refs[nx:n_in])]
        res = f(*xv, *pv)
        for k, o_ref in enumerate(refs[n_all_in:]):
            if outs[k][0] == 'acc':
                @pl.when(i == 0)
                def _(o_ref=o_ref, v=res[k]):
                    o_ref[...] = v

                @pl.when(i > 0)
                def _(o_ref=o_ref, v=res[k]):
                    o_ref[...] += v
            else:
                o_ref[...] = res[k].astype(o_ref.dtype)

    return pl.pallas_call(body, name=name, grid=(n,), in_specs=in_specs, out_specs=specs, out_shape=shapes,
                          input_output_aliases=aliases, compiler_params=_cp("arbitrary"))(*args)


def stage_bwd(name, f, T, n, xs, ps, cts, dxs, dps, primal=()):
    nx, npar = len(xs), len(ps)
    args = [x[0] for x in xs] + [p[0] for p in ps]
    in_specs = [_xspec(T, w, col, roff) for (_, w, col, roff) in xs] + [_pspec(p, sel) for (p, sel) in ps]
    ct_arrs = [c for c in cts if isinstance(c, tuple)]
    for (a, w, col, roff) in ct_arrs:
        args.append(a)
        in_specs.append(_xspec(T, w, col, roff))
    n_in = len(args)
    outs, out_of = [], []
    for k, o in enumerate(dxs):
        if o is not None:
            outs.append(o)
            out_of.append(('x', k))
    for k, want in enumerate(dps):
        if want:
            p, sel = ps[k]
            outs.append(('acc', p.shape))
            out_of.append(('p', k))
    for k, shape in primal:
        outs.append(('acc', shape))
        out_of.append(('r', k))
    shapes, specs, aliases = _out_plumbing(outs, T, args, in_specs)
    for j, (kind, k) in enumerate(out_of):
        if kind == 'p' and ps[k][1] is not None:
            p, sel = ps[k]
            specs[j] = pl.BlockSpec((1,) + p.shape[1:], lambda i, n=p.ndim, sel=sel: (sel(i),) + (0,) * (n - 1))
    n_all_in = len(args)

    def body(*refs):
        i = pl.program_id(0)
        xv = [r[...] for r in refs[:nx]]
        pv = [r[...] if ps[k][1] is None else r[0] for k, r in enumerate(refs[nx:nx + npar])]
        res, vjp_fn = jax.vjp(f, *xv, *pv)
        ctv, q = [], nx + npar
        for k, c in enumerate(cts):
            if c is None:
                ctv.append(jnp.zeros_like(res[k]))
            elif isinstance(c, tuple):
                v = refs[q][...]
                if c[3] < 0:
                    v = v * (i + c[3] >= 0).astype(F32)
                ctv.append(v)
                q += 1
            else:
                ctv.append(jnp.full_like(res[k], c))
        grads = vjp_fn(tuple(ctv))
        for j, o_ref in enumerate(refs[n_all_in:]):
            kind, k = out_of[j]
            if kind == 'x':
                o_ref[...] = grads[k].astype(o_ref.dtype)
            else:
                g = res[k] if kind == 'r' else grads[nx + k]
                sel = None if kind == 'r' else ps[k][1]
                if sel is None:
                    first = i == 0
                    tgt = o_ref
                else:
                    first = jnp.logical_or(i == 0, sel(i) != sel(jnp.maximum(i - 1, 0)))
                    tgt = o_ref.at[0]

                @pl.when(first)
                def _(tgt=tgt, g=g):
                    tgt[...] = g

                @pl.when(jnp.logical_not(first))
                def _(tgt=tgt, g=g):
                    tgt[...] += g

    return pl.pallas_call(body, name=name, grid=(n,), in_specs=in_specs, out_specs=specs, out_shape=shapes,
                          input_output_aliases=aliases, compiler_params=_cp("arbitrary"))(*args)


_CONTRACT = {'nn': (1, 0), 'nt': (1, 1), 'tn': (0, 0)}


def matmul(name, a, b, mode, tm, tn, tk, out_dtype=F32, add=None):
    if mode == 'nn':
        (M, K), (_, N) = a.shape, b.shape
    elif mode == 'nt':
        (M, K), (N, _) = a.shape, b.shape
    else:
        (K, M), (_, N) = a.shape, b.shape
    assert M % tm == 0 and N % tn == 0 and K % tk == 0, (name, M, N, K, tm, tn, tk)
    a_spec = (pl.BlockSpec((tk, tm), lambda j, i, k: (k, i)) if mode == 'tn'
              else pl.BlockSpec((tm, tk), lambda j, i, k: (i, k)))
    b_spec = (pl.BlockSpec((tn, tk), lambda j, i, k: (j, k)) if mode == 'nt'
              else pl.BlockSpec((tk, tn), lambda j, i, k: (k, j)))
    o_spec = pl.BlockSpec((tm, tn), lambda j, i, k: (i, j))
    return matmul_call(name, (N // tn, M // tm, K // tk), a, a_spec, b, b_spec, (M, N), o_spec, (tm, tn), mode,
                       out_dtype, add)


def matmul_call(name, grid, a, a_spec, b, b_spec, out_shape, o_spec, tile, mode, out_dtype=F32, add=None):
    tm, tn = tile
    nk = grid[2]
    ca, cb = _CONTRACT[mode]
    args, in_specs = [a, b], [a_spec, b_spec]
    if add is not None:
        args.append(add)
        in_specs.append(o_spec)

    def body(*refs):
        a_ref, b_ref = refs[0], refs[1]
        o_ref, acc = refs[-2], refs[-1]
        k = pl.program_id(2)
        if nk == 1:
            p = _dot(a_ref[...], b_ref[...], ca, cb)
            o_ref[...] = (p + refs[2][...] if add is not None else p).astype(out_dtype)
            return

        @pl.when(k == 0)
        def _():
            acc[...] = refs[2][...] if add is not None else jnp.zeros_like(acc)

        acc[...] += _dot(a_ref[...], b_ref[...], ca, cb)

        @pl.when(k == nk - 1)
        def _():
            o_ref[...] = acc[...].astype(out_dtype)

    return pl.pallas_call(body, name=name, grid=grid, in_specs=in_specs, out_specs=o_spec,
                          out_shape=jax.ShapeDtypeStruct(out_shape, out_dtype),
                          scratch_shapes=[pltpu.VMEM((tm, tn) if nk > 1 else (8, 128), F32)],
                          compiler_params=_cp("arbitrary", "arbitrary", "arbitrary"))(*args)


NS, WS = 4, 704


def _resident(name, M, tm, rows, weight, out_shape, out_block, out_map, step, add=None):
    args = [rows[0], weight] + ([] if add is None else [add])
    in_specs = [pl.BlockSpec(rows[1], rows[2]), pl.BlockSpec(weight.shape, lambda i, n=weight.ndim: (0,) * n)]
    if add is not None:
        in_specs.append(pl.BlockSpec(out_block, out_map))
    return pl.pallas_call(step, name=name, grid=(M // tm,), in_specs=in_specs, out_specs=pl.BlockSpec(out_block, out_map),
                          out_shape=jax.ShapeDtypeStruct(out_shape, F32), compiler_params=_cp("arbitrary"))(*args)


def ffn_in_fwd(name, h, w1, w3, tm):
    M = h.shape[0]

    def step(h_ref, w1_ref, w3_ref, a1_ref, a3_ref, act_ref):
        for s in range(NS):
            a1 = _dot(h_ref[...], w1_ref[s], 1, 0)
            a3 = _dot(h_ref[...], w3_ref[s], 1, 0)
            a1_ref[s] = a1
            a3_ref[s] = a3
            act_ref[s] = (_silu(a1) * a3).astype(act_ref.dtype)

    wspec = pl.BlockSpec((NS, D, WS), lambda i: (0, 0, 0))
    ospec = pl.BlockSpec((NS, tm, WS), lambda i: (0, i, 0))
    return pl.pallas_call(
        step, name=name, grid=(M // tm,), in_specs=[pl.BlockSpec((tm, D), lambda i: (i, 0)), wspec, wspec],
        out_specs=[ospec, ospec, ospec],
        out_shape=[jax.ShapeDtypeStruct((NS, M, WS), F32), jax.ShapeDtypeStruct((NS, M, WS), F32),
                   jax.ShapeDtypeStruct((NS, M, WS), BF16)], compiler_params=_cp("arbitrary"))(h, w1, w3)


def ffn_out_bwd_x(name, dff, w2, a1, a3, tm):
    M = dff.shape[0]

    def step(d_ref, w_ref, a1_ref, a3_ref, da1_ref, da3_ref):
        for s in range(NS):
            dact = _dot(d_ref[...], w_ref[s * WS:(s + 1) * WS, :], 1, 1)
            a1 = a1_ref[s]
            sig = jax.nn.sigmoid(a1)
            da3_ref[s] = (dact * (a1 * sig)).astype(da3_ref.dtype)
            da1_ref[s] = (dact * a3_ref[s] * (sig * (1.0 + a1 * (1.0 - sig)))).astype(da1_ref.dtype)

    aspec = pl.BlockSpec((NS, tm, WS), lambda i: (0, i, 0))
    return pl.pallas_call(
        step, name=name, grid=(M // tm,),
        in_specs=[pl.BlockSpec((tm, D), lambda i: (i, 0)), pl.BlockSpec(w2.shape, lambda i: (0, 0)), aspec, aspec],
        out_specs=[aspec, aspec],
        out_shape=[jax.ShapeDtypeStruct((NS, M, WS), BF16)] * 2, compiler_params=_cp("arbitrary"))(dff, w2, a1, a3)


def ff_in_bwd_x(name, da3, w3, tm, add=None):
    M = da3.shape[1]

    def step(*refs):
        d_ref, w_ref, o_ref = refs[0], refs[1], refs[-1]
        acc = _dot(d_ref[0], w_ref[0], 1, 1)
        for s in range(1, NS):
            acc = acc + _dot(d_ref[s], w_ref[s], 1, 1)
        o_ref[...] = acc if add is None else acc + refs[2][...]

    return _resident(name, M, tm, (da3, (NS, tm, WS), lambda i: (0, i, 0)), w3, (M, D), (tm, D), lambda i: (i, 0), step, add)


def ff_in_bwd_w(name, h, da3, tk):
    M = h.shape[0]
    return matmul_call(name, (NS, 1, M // tk), h, pl.BlockSpec((tk, D), lambda j, i, k: (k, 0)),
                       da3, pl.BlockSpec((None, tk, WS), lambda j, i, k: (j, k, 0)),
                       (NS, D, WS), pl.BlockSpec((None, D, WS), lambda j, i, k: (j, 0, 0)), (D, WS), 'tn', BF16)


def ff_out_fwd(name, act3, w2, tm):
    M = act3.shape[1]

    def step(a_ref, w_ref, o_ref):
        acc = _dot(a_ref[0], w_ref[0:WS, :], 1, 0)
        for s in range(1, NS):
            acc = acc + _dot(a_ref[s], w_ref[s * WS:(s + 1) * WS, :], 1, 0)
        o_ref[...] = acc

    return _resident(name, M, tm, (act3, (NS, tm, WS), lambda i: (0, i, 0)), w2, (M, D), (tm, D), lambda i: (i, 0), step)


def ff_out_bwd_w(name, act3, dff, tk):
    M = dff.shape[0]
    return matmul_call(name, (1, NS, M // tk), act3, pl.BlockSpec((None, tk, WS), lambda j, i, k: (i, k, 0)),
                       dff, pl.BlockSpec((tk, D), lambda j, i, k: (k, 0)),
                       (NS * WS, D), pl.BlockSpec((WS, D), lambda j, i, k: (i, 0)), (WS, D), 'tn', BF16)


def _shift_rows(x, d):
    n = x.shape[0]
    if d == 0:
        return x
    y = pltpu.roll(x, (-d) % n, 0)
    t = lax.broadcasted_iota(jnp.int32, x.shape, 0)
    ok = (t + d >= 0) & (t + d < n)
    return jnp.where(ok, y, 0.0)


def _conv_pre(x, w_ref, b_ref):
    acc = jnp.broadcast_to(b_ref[...], x.shape)
    for k in range(5):
        acc = acc + _shift_rows(x, k - 2) * w_ref[k:k + 1, :]
    return acc


def conv_fwd(name, proj, conv_w, conv_b, R):
    segs = ((0, LC), (LC, R))

    def body(x_ref, w_ref, b_ref, o_ref):
        for (s, e) in segs:
            pre = _conv_pre(x_ref[s:e, :], w_ref, b_ref)
            o_ref[s:e, :] = _silu(pre)

    return pl.pallas_call(
        body, name=name, grid=(12,),
        in_specs=[pl.BlockSpec((R, 128), lambda j: (0, C_XBC // 128 + j)),
                  pl.BlockSpec((8, 128), lambda j: (0, j)), pl.BlockSpec((1, 128), lambda j: (0, j))],
        out_specs=pl.BlockSpec((R, 128), lambda j: (0, j)),
        out_shape=jax.ShapeDtypeStruct((R, 1536), F32), compiler_params=_cp("arbitrary"))(proj, conv_w, conv_b)


def conv_bwd(name, proj, conv_w, conv_b, d_f, d_b, d_skip, dproj, R):
    segs = ((0, LC), (LC, R))

    def body(x_ref, w_ref, b_ref, df_ref, db_ref, ds_ref, _, dx_ref, dw_ref, dbias_ref):
        j = pl.program_id(0)
        has_skip = (j < 8).astype(F32)
        dw = [jnp.zeros((1, 128), F32) for _ in range(5)]
        dbias = jnp.zeros((1, 128), F32)
        for (s, e) in segs:
            x = x_ref[s:e, :]
            pre = _conv_pre(x, w_ref, b_ref)
            sig = jax.nn.sigmoid(pre)
            dy = df_ref[s:e, :] + db_ref[s:e, :]
            if s == LC:
                dy = dy + ds_ref[...] * has_skip
            dpre = dy * (sig * (1.0 + pre * (1.0 - sig)))
            dx = jnp.zeros_like(x)
            for k in range(5):
                dx = dx + _shift_rows(dpre, 2 - k) * w_ref[k:k + 1, :]
                dw[k] = dw[k] + jnp.sum(dpre * _shift_rows(x, k - 2), axis=0, keepdims=True)
            dbias = dbias + jnp.sum(dpre, axis=0, keepdims=True)
            dx_ref[s:e, :] = dx.astype(dx_ref.dtype)
        dw_ref[...] = jnp.zeros_like(dw_ref)
        for k in range(5):
            dw_ref[k:k + 1, :] = dw[k]
        dbias_ref[...] = dbias

    return pl.pallas_call(
        body, name=name, grid=(12,),
        in_specs=[pl.BlockSpec((R, 128), lambda j: (0, C_XBC // 128 + j)),
                  pl.BlockSpec((8, 128), lambda j: (0, j)), pl.BlockSpec((1, 128), lambda j: (0, j)),
                  pl.BlockSpec((R, 128), lambda j: (0, j)), pl.BlockSpec((R, 128), lambda j: (0, j)),
                  pl.BlockSpec((R - LC, 128), lambda j: (0, jnp.minimum(j, 7))),
                  pl.BlockSpec(memory_space=pl.ANY)],
        out_specs=[pl.BlockSpec((R, 128), lambda j: (0, C_XBC // 128 + j)),
                   pl.BlockSpec((8, 128), lambda j: (0, j)), pl.BlockSpec((1, 128), lambda j: (0, j))],
        out_shape=[jax.ShapeDtypeStruct(dproj.shape, dproj.dtype), jax.ShapeDtypeStruct((8, 1536), F32),
                   jax.ShapeDtypeStruct((1, 1536), F32)],
        input_output_aliases={6: 0}, compiler_params=_cp("arbitrary"))(proj, conv_w, conv_b, d_f, d_b, d_skip, dproj)


def _ssd_chunk(rev, dirn):
    cums = _cumsum_fn(rev)

    def f(xs, Bs, Cs, dt, alog, Hs):
        lane = lax.broadcasted_iota(jnp.int32, (1, 128), 1)
        sub = lax.broadcasted_iota(jnp.int32, (Q, 1), 0)
        r = lax.broadcasted_iota(jnp.int32, (Q, Q), 0)
        c = lax.broadcasted_iota(jnp.int32, (Q, Q), 1)
        mask = (r <= c) if rev else (r >= c)
        left = lane < 64
        a = dt * (-jnp.exp(alog))
        s = cums(a)
        sT, dtT = s.T, dt.T
        last_row = (sub == (0 if rev else Q - 1)).astype(F32)
        s_last = jnp.sum(s * last_row, axis=0, keepdims=True)
        G = [mm_nt(Cs[g], Bs[g]) for g in range(2)]
        M, es, wc, ed = [], [], [], []
        for h in range(NH):
            l = 16 * dirn + h
            oh_l = (lane == l).astype(F32)
            oh_s = (sub == l).astype(F32)
            s_col = jnp.sum(s * oh_l, axis=1, keepdims=True)
            dt_col = jnp.sum(dt * oh_l, axis=1, keepdims=True)
            s_row = jnp.sum(sT * oh_s, axis=0, keepdims=True)
            dt_row = jnp.sum(dtT * oh_s, axis=0, keepdims=True)
            sl = jnp.sum(s_last * oh_l, axis=1, keepdims=True)
            seg = jnp.where(mask, s_col - s_row, 0.0)
            lm = jnp.where(mask, jnp.exp(seg), 0.0)
            M.append(G[h // 8] * lm * dt_row)
            es.append(jnp.exp(s_col))
            wc.append(jnp.exp(sl - s_col) * dt_col)
            ed.append(jnp.exp(sl))
        Ys, Hn = [], []
        for j in range(8):
            g = j // 4
            xa = jnp.where(left, xs[j], 0.0)
            xb = jnp.where(left, 0.0, xs[j])
            yd = mm(M[2 * j], xa) + mm(M[2 * j + 1], xb)
            yo = mm(Cs[g], Hs[j]) * jnp.where(left, es[2 * j], es[2 * j + 1])
            Ys.append(yd + yo)
            st = mm_tn(Bs[g], xs[j] * jnp.where(left, wc[2 * j], wc[2 * j + 1]))
            Hn.append(Hs[j] * jnp.where(left, ed[2 * j], ed[2 * j + 1]) + st)
        return Ys, Hn

    return f


def _chunk_of(t, n, rev):
    if not rev:
        return t
    return jnp.where(t < 2, 1 - t, n + 1 - t)


def ssd_fwd(name, xbc, dt, alog, n, rev, dirn):
    chunk = _ssd_chunk(rev, dirn)

    def body(x_ref, b_ref, c_ref, dt_ref, al_ref, y_ref, hs_ref, h_scr):
        @pl.when(pl.program_id(0) == 0)
        def _():
            h_scr[...] = jnp.zeros_like(h_scr)

        xs = [x_ref[:, 128 * j:128 * (j + 1)] for j in range(8)]
        Bs = [b_ref[:, 128 * g:128 * (g + 1)] for g in range(2)]
        Cs = [c_ref[:, 128 * g:128 * (g + 1)] for g in range(2)]
        Hs = [h_scr[:, 128 * j:128 * (j + 1)] for j in range(8)]
        hs_ref[0] = h_scr[...]
        Ys, Hn = chunk(xs, Bs, Cs, dt_ref[...], al_ref[...], Hs)
        for j in range(8):
            y_ref[:, 128 * j:128 * (j + 1)] = Ys[j]
            h_scr[:, 128 * j:128 * (j + 1)] = Hn[j]

    cm = lambda t: _chunk_of(t, n, rev)
    return pl.pallas_call(
        body, name=name, grid=(n,),
        in_specs=[pl.BlockSpec((Q, 1024), lambda t: (cm(t), 0)), pl.BlockSpec((Q, 256), lambda t: (cm(t), 4)),
                  pl.BlockSpec((Q, 256), lambda t: (cm(t), 5)), pl.BlockSpec((Q, 128), lambda t: (cm(t), 0)),
                  pl.BlockSpec((1, 128), lambda t: (0, 0))],
        out_specs=[pl.BlockSpec((Q, 1024), lambda t: (cm(t), 0)), pl.BlockSpec((1, Q, 1024), lambda t: (cm(t), 0, 0))],
        out_shape=[jax.ShapeDtypeStruct((n * Q, 1024), F32), jax.ShapeDtypeStruct((n, Q, 1024), F32)],
        scratch_shapes=[pltpu.VMEM((Q, 1024), F32)], compiler_params=_cp("arbitrary"))(xbc, xbc, xbc, dt, alog)


def ssd_bwd(name, xbc, dt, alog, hs, dy, n, rev, dirn):
    chunk = _ssd_chunk(rev, dirn)

    def body(x_ref, b_ref, c_ref, dt_ref, al_ref, hs_ref, dy_ref, dx_ref, ddt_ref, dal_ref, dh_scr):
        tt = pl.program_id(0)
        ch = _chunk_of(n - 1 - tt, n, rev)

        @pl.when(tt == 0)
        def _():
            dh_scr[...] = jnp.zeros_like(dh_scr)

        xs = [x_ref[:, 128 * j:128 * (j + 1)] for j in range(8)]
        Bs = [b_ref[:, 128 * g:128 * (g + 1)] for g in range(2)]
        Cs = [c_ref[:, 128 * g:128 * (g + 1)] for g in range(2)]
        Hs = [hs_ref[0, :, 128 * j:128 * (j + 1)] for j in range(8)]
        live = (ch >= 2).astype(F32)
        dYs = [dy_ref[:, 128 * j:128 * (j + 1)] * live for j in range(8)]
        dHn = [dh_scr[:, 128 * j:128 * (j + 1)] for j in range(8)]
        _, vjp_fn = jax.vjp(chunk, xs, Bs, Cs, dt_ref[...], al_ref[...], Hs)
        dxs, dBs, dCs, ddt, dal, dHs = vjp_fn((dYs, dHn))
        for j in range(8):
            dx_ref[:, 128 * j:128 * (j + 1)] = dxs[j]
            dh_scr[:, 128 * j:128 * (j + 1)] = dHs[j]
        for g in range(2):
            dx_ref[:, 1024 + 128 * g:1024 + 128 * (g + 1)] = dBs[g]
            dx_ref[:, 1280 + 128 * g:1280 + 128 * (g + 1)] = dCs[g]
        ddt_ref[...] = ddt

        @pl.when(tt == 0)
        def _():
            dal_ref[...] = dal

        @pl.when(tt > 0)
        def _():
            dal_ref[...] += dal

    cm = lambda t: _chunk_of(n - 1 - t, n, rev)
    return pl.pallas_call(
        body, name=name, grid=(n,),
        in_specs=[pl.BlockSpec((Q, 1024), lambda t: (cm(t), 0)), pl.BlockSpec((Q, 256), lambda t: (cm(t), 4)),
                  pl.BlockSpec((Q, 256), lambda t: (cm(t), 5)), pl.BlockSpec((Q, 128), lambda t: (cm(t), 0)),
                  pl.BlockSpec((1, 128), lambda t: (0, 0)), pl.BlockSpec((1, Q, 1024), lambda t: (cm(t), 0, 0)),
                  pl.BlockSpec((Q, 1024), lambda t: (jnp.maximum(cm(t) - 2, 0), 0))],
        out_specs=[pl.BlockSpec((Q, 1536), lambda t: (cm(t), 0)), pl.BlockSpec((Q, 128), lambda t: (cm(t), 0)),
                   pl.BlockSpec((1, 128), lambda t: (0, 0))],
        out_shape=[jax.ShapeDtypeStruct((n * Q, 1536), F32), jax.ShapeDtypeStruct((n * Q, 128), F32),
                   jax.ShapeDtypeStruct((1, 128), F32)],
        scratch_shapes=[pltpu.VMEM((Q, 1024), F32)], compiler_params=_cp("arbitrary"))(xbc, xbc, xbc, dt, alog, hs, dy)


def f_norm0(x, g0, b0, sc, sh):
    x0 = _ln(x, g0, b0)
    return x0, x0 * (1.0 + sc) + sh


def f_dt(raw, bias):
    z = split4(raw)[0] + bias
    dt = jnp.maximum(z, 0.0) + jnp.log1p(jnp.exp(-jnp.abs(z)))
    return dt, dt


def f_gated_norm(yf, yb, xs, z, dcol, g):
    h = (yf + yb + xs * dcol) * _silu(z)
    return (h * lax.rsqrt(jnp.mean(h * h, axis=-1, keepdims=True) + LN_EPS) * g,)


def f_gmlp(uv, gmg, gmb, *wb):
    ws, bs = wb[:8], wb[8:]
    u, v = split2(uv)
    vn = split8(_ln(_gelu(v), gmg, gmb))
    mixed = concat8(tuple(mm(ws[g], vn[g]) + bs[g] for g in range(8)))
    return (_gelu(u) * mixed,)


def f_merge(ps, pg, gates, bg):
    gs, gg = split2(jax.nn.sigmoid(gates + bg))
    return (gs * ps + gg * pg,)


def f_res1(x0, out, g1, lg, lb, sc, sh):
    x1 = _ln(ALPHA * x0 + g1 * out, lg, lb)
    return x1, x1 * (1.0 + sc) + sh


def f_res2_loss(x1, ff, tgt, g2, lg, lb):
    x2 = _ln(ALPHA * x1 + g2 * ff, lg, lb)
    e = x2 - tgt
    return (0.5 * jnp.sum(jnp.mean(e * e, axis=-1, keepdims=True), axis=0, keepdims=True),)


def _row_tile(M):
    return 544 if M % 544 == 0 else (512 if M % 512 == 0 else M)


def core(x_all, tgt, mod_x, mod_c, X, S):
    R = x_all.shape[0]
    L = R - LC
    n = R // Q
    T = 256
    nt, ntl = R // T, L // T
    tmR, tmL = _row_tile(R), _row_tile(L)
    tkR = 256 if R % 512 else 512
    tkL = 512 if L % 512 == 0 else 256
    row = lambda v: v.reshape(1, -1)
    mx = [row(mod_x[k]) for k in range(6)]
    mc = [row(mod_c[k]) for k in range(6)]
    sel = lambda i: jnp.minimum(i, 1)
    sc1 = jnp.stack([mc[1], mx[1]])
    sh1 = jnp.stack([mc[0], mx[0]])
    ln0 = [(row(S['ln0_g']), None), (row(S['ln0_b']), None), (sc1, sel), (sh1, sel)]

    x0, xm = stage_fwd("norm0_fwd", f_norm0, T, nt, [(x_all, D, 0, 0)], ln0, [('new', R, D, 0), ('new', R, D, 0, BF16)])
    w_in = X.w_in()
    proj = matmul("proj_fwd", xm, w_in, 'nn', tmR, PW // 2, 1024)
    conv_w8 = jnp.pad(S['conv_w'], ((0, 3), (0, 0)))
    conv_b = row(S['conv_b'])
    xbc = conv_fwd("conv_fwd", proj, conv_w8, conv_b, R)
    dt_bias = jnp.pad(S['dt_bias'].reshape(1, 32), ((0, 0), (0, 96)))
    alog = jnp.pad(S['a_log'].reshape(1, 32), ((0, 0), (0, 96)))
    x_dt = [(proj, 512, C_DT // 512, 0)]
    dt_f, dt_b = stage_fwd("dt_fwd", f_dt, T, nt, x_dt, [(dt_bias, None)], [('new', R, 128, 0), ('new', R, 128, 0)])
    y_f, hs_f = ssd_fwd("ssd_fwd_f", xbc, dt_f, alog, n, False, 0)
    y_b, hs_b = ssd_fwd("ssd_fwd_b", xbc, dt_b, alog, n, True, 1)
    W = X.rest()
    dcol = jnp.repeat(S['d_skip'][0] + S['d_skip'][1], 64).reshape(1, D)
    x_gn = [(y_f, D, 0, 1), (y_b, D, 0, 1), (xbc, D, 0, 1), (proj, D, C_Z // D, 1)]
    p_gn = [(dcol, None), (row(S['ssd_norm_g']), None)]
    (yn,) = stage_fwd("gnorm_fwd", f_gated_norm, T, ntl, x_gn, p_gn, [('new', L, D, 0, BF16)])
    x_gm = [(proj, 2 * D, C_UV // (2 * D), LC // Q)]
    p_gm = ([(row(S['gm_norm_g']), None), (row(S['gm_norm_b']), None)]
            + [(S['w_spatial'][g], None) for g in range(8)] + [(S['b_spatial'][g].reshape(Q, 1), None) for g in range(8)])
    (y_gm,) = stage_fwd("gmlp_fwd", f_gmlp, Q, L // Q, x_gm, p_gm, [('new', L, D, 0, BF16)])
    p_ssd = matmul("pssd_fwd", yn, W['w_ssd_proj'], 'nn', tmL, 1024, 1024)
    p_g = matmul("pgm_fwd", y_gm, W['w_gm_proj'], 'nn', tmL, 1024, 1024)
    x_mg = [(p_ssd, D, 0, 0), (p_g, D, 0, 0), (proj, 2 * D, C_GATE // (2 * D), 1)]
    p_mg = [(row(S['b_gate']), None)]
    (merged,) = stage_fwd("merge_fwd", f_merge, T, ntl, x_mg, p_mg, [('new', L, D, 0, BF16)])
    out = matmul("out_fwd", merged, W['w_out'], 'nn', tmL, 1024, 1024)
    x_r1 = [(x0, D, 0, 1), (out, D, 0, 0)]
    p_r1 = [(mx[2], None), (row(S['ln1_g']), None), (row(S['ln1_b']), None), (mx[4], None), (mx[3], None)]
    x1, hm = stage_fwd("res1_fwd", f_res1, T, ntl, x_r1, p_r1, [('new', L, D, 0), ('new', L, D, 0, BF16)])
    a1, a3, act = ffn_in_fwd("ffn_in_fwd", hm, W['w_ff1'], W['w_ff3'], T)
    ff = ff_out_fwd("ff2_fwd", act, W['w_ff2'], tmL)
    x_r2 = [(x1, D, 0, 0), (ff, D, 0, 0), (tgt, D, 0, 0)]
    p_r2 = [(mx[5], None), (row(S['ln2_g']), None), (row(S['ln2_b']), None)]

    dx1_a, dff, dg2, dl2g, dl2b, loss = stage_bwd(
        "res2_bwd", f_res2_loss, T, ntl, x_r2, p_r2, [1.0],
        [('new', L, D, 0), ('new', L, D, 0, BF16), None], [True, True, True], primal=[(0, (1, 1))])
    da1, da3 = ffn_out_bwd_x("ffn_out_bwd_x", dff, W['w_ff2'], a1, a3, T)
    gw_ff2 = ff_out_bwd_w("ff2_bwd_w", act, dff, tkL)
    dhm = ff_in_bwd_x("ff1_bwd_x", da1, W['w_ff1'], tmL)
    dhm = ff_in_bwd_x("ff3_bwd_x", da3, W['w_ff3'], tmL, add=dhm)
    gw_ff1 = ff_in_bwd_w("ff1_bwd_w", hm, da1, tkL)
    gw_ff3 = ff_in_bwd_w("ff3_bwd_w", hm, da3, tkL)
    X.grads('ffn', {'w_ff2': gw_ff2, 'w_ff1': gw_ff1, 'w_ff3': gw_ff3})
    dx0_a, dout, dg1, dl1g, dl1b, dsc2, dsh2 = stage_bwd(
        "res1_bwd", f_res1, T, ntl, x_r1, p_r1, [(dx1_a, D, 0, 0), (dhm, D, 0, 0)],
        [('new', L, D, 0), ('new', L, D, 0, BF16)], [True] * 5)
    dmerged = matmul("out_bwd_x", dout, W['w_out'], 'nt', tmL, 1024, 1024)
    gw_out = matmul("out_bwd_w", merged, dout, 'tn', 1024, 1024, tkL, BF16)
    lt, lq = -(LC // T), -(LC // Q)
    x_mg_b = [(p_ssd, D, 0, lt), (p_g, D, 0, lt), (proj, 2 * D, C_GATE // (2 * D), 0)]
    dp_ssd, dp_g, dproj, dbg = stage_bwd(
        "merge_bwd", f_merge, T, nt, x_mg_b, p_mg, [(dmerged, D, 0, lt)],
        [('new', L, D, lt, BF16), ('new', L, D, lt, BF16), ('part', R, PW, 2 * D, C_GATE // (2 * D), 0, BF16)], [True])
    dyn = matmul("pssd_bwd_x", dp_ssd, W['w_ssd_proj'], 'nt', tmL, 1024, 1024)
    gw_ssd = matmul("pssd_bwd_w", yn, dp_ssd, 'tn', 1024, 1024, tkL, BF16)
    dy_gm = matmul("pgm_bwd_x", dp_g, W['w_gm_proj'], 'nt', tmL, 1024, 1024)
    gw_gm = matmul("pgm_bwd_w", y_gm, dp_g, 'tn', 1024, 1024, tkL, BF16)
    X.grads('proj', {'w_out': gw_out, 'w_ssd_proj': gw_ssd, 'w_gm_proj': gw_gm})
    r_gm = stage_bwd("gmlp_bwd", f_gmlp, Q, n, [(proj, 2 * D, C_UV // (2 * D), 0)], p_gm, [(dy_gm, D, 0, lq)],
                     [('alias', dproj, 2 * D, C_UV // (2 * D), 0)], [True] * 18)
    dproj, dgmg, dgmb, dws, dbs = r_gm[0], r_gm[1], r_gm[2], r_gm[3:11], r_gm[11:19]
    x_gn_b = [(y_f, D, 0, 0), (y_b, D, 0, 0), (xbc, D, 0, 0), (proj, D, C_Z // D, 0)]
    dy, dskipx, dproj, ddcol, dng = stage_bwd(
        "gnorm_bwd", f_gated_norm, T, nt, x_gn_b, p_gn, [(dyn, D, 0, lt)],
        [('new', L, D, lt), None, ('new', L, D, lt), ('alias', dproj, D, C_Z // D, 0)], [True, True])
    dxbc_f, ddt_f, dal_f = ssd_bwd("ssd_bwd_f", xbc, dt_f, alog, hs_f, dy, n, False, 0)
    dxbc_b, ddt_b, dal_b = ssd_bwd("ssd_bwd_b", xbc, dt_b, alog, hs_b, dy, n, True, 1)
    dproj, ddtb = stage_bwd("dt_bwd", f_dt, T, nt, x_dt, [(dt_bias, None)],
                            [(ddt_f, 128, 0, 0), (ddt_b, 128, 0, 0)],
                            [('alias', dproj, 512, C_DT // 512, 0)], [True])
    dproj, dcw8, dcb = conv_bwd("conv_bwd", proj, conv_w8, conv_b, dxbc_f, dxbc_b, dskipx, dproj, R)
    gw_in = matmul("proj_bwd_w", xm, dproj, 'tn', 1024, PW // 4, tkR, BF16)
    X.grads('in', {'w_in': gw_in})
    dxm = matmul("proj_bwd_x", dproj, w_in, 'nt', R // 4 if R % 32 == 0 else R, 1024, 1024)
    grad_x, dl0g, dl0b, dsc1, dsh1 = stage_bwd(
        "norm0_bwd", f_norm0, T, nt, [(x_all, D, 0, 0)], ln0, [(dx0_a, D, 0, -1), (dxm, D, 0, 0)],
        [('new', L, D, -1)], [True] * 4)

    zero = jnp.zeros((D,), F32)
    flat = lambda v: v.reshape(-1)
    small = {
        'loss': flat(loss), 'ln0_g': flat(dl0g), 'ln0_b': flat(dl0b),
        'dmod_x': jnp.concatenate([flat(dsh1[1]), flat(dsc1[1]), flat(dg1), flat(dsh2), flat(dsc2), flat(dg2)]),
        'dmod_c': jnp.concatenate([flat(dsh1[0]), flat(dsc1[0]), zero, zero, zero, zero]),
        'conv_w': flat(dcw8[:5]), 'conv_b': flat(dcb), 'dt_bias': flat(ddtb[:, :32]),
        'a_log': flat((dal_f + dal_b)[:, :32]),
        'd_skip': flat(jnp.tile(ddcol.reshape(1, NH, 64).sum(-1), (2, 1))),
        'ssd_norm_g': flat(dng), 'gm_norm_g': flat(dgmg), 'gm_norm_b': flat(dgmb),
        'w_spatial': flat(jnp.stack(dws)), 'b_spatial': flat(jnp.stack(dbs)), 'b_gate': flat(dbg),
        'ln1_g': flat(dl1g), 'ln1_b': flat(dl1b), 'ln2_g': flat(dl2g), 'ln2_b': flat(dl2b),
    }
    return grad_x, small


def _place():
    return lax.axis_index("x"), lax.axis_index("y"), lax.axis_index("c")


def allgather8(name, blk, hbm):
    space = pl.ANY if hbm else pltpu.VMEM

    def body(x_ref, out_ref, send_sems, recv_sems, local_sem):
        x, y, c = _place()
        me, sibling = (x, y, c), (x, y, 1 - c)
        chips = [(1 - x, y), (x, 1 - y), (1 - x, 1 - y)]

        def slot(px, py, pc):
            return out_ref.at[4 * px + 2 * py + pc]

        def copy(k, block, to, src=None):
            return pltpu.make_async_remote_copy(
                src_ref=slot(*block) if src is None else src, dst_ref=slot(*block),
                send_sem=send_sems.at[k], recv_sem=recv_sems.at[k], device_id=to, device_id_type=MESH)

        mine = pltpu.make_async_copy(x_ref, slot(*me), local_sem)
        mine.start()
        first = [copy(0, me, sibling, src=x_ref)]
        first += [copy(1 + j, me, (*chip, c), src=x_ref) for j, chip in enumerate(chips)]
        for cp in first:
            cp.start()
        passed = [copy(4 + j, (*chip, c), sibling) for j, chip in enumerate(chips)]
        for j, chip in enumerate(chips):
            copy(1 + j, (*chip, c), me).wait_recv()
            passed[j].start()
        copy(0, sibling, me).wait_recv()
        for j, chip in enumerate(chips):
            copy(4 + j, (*chip, 1 - c), me).wait_recv()
        for cp in first + passed:
            cp.wait_send()
        mine.wait()

    return pl.pallas_call(
        body, name=name, out_shape=jax.ShapeDtypeStruct((8,) + blk.shape, blk.dtype),
        in_specs=[pl.BlockSpec(memory_space=space)], out_specs=pl.BlockSpec(memory_space=space),
        scratch_shapes=[pltpu.SemaphoreType.DMA((7,)), pltpu.SemaphoreType.DMA((7,)), pltpu.SemaphoreType.DMA],
        compiler_params=pltpu.CompilerParams(vmem_limit_bytes=VMEM_LIMIT_V7X))(blk)


def _peers(place):
    x, y, c = place
    return [((1 - x) if k & 4 else x, (1 - y) if k & 2 else y, (1 - c) if k & 1 else c) for k in range(1, 8)]


def _slot(p):
    return 4 * p[0] + 2 * p[1] + p[2]


def plan_gather(place, srcs, lands):
    remote = [(s, l.at[_slot(place)], to) for s, l in zip(srcs, lands) for to in _peers(place)]
    return remote, [(s, l.at[_slot(place)]) for s, l in zip(srcs, lands)]


def plan_to_owner(place, srcs, lands):
    remote = [(s.at[2 * to[0] + to[1], to[2]], l.at[_slot(place)], to) for s, l in zip(srcs, lands) for to in _peers(place)]
    x, y, c = place
    return remote, [(s.at[2 * x + y, c], l.at[_slot(place)]) for s, l in zip(srcs, lands)]


def sequencer_exchange(name, collective_id, srcs, land_shapes, plan):
    n = len(srcs)
    src_refs = [jax.new_ref(a, memory_space=pltpu.MemorySpace.HBM) for a in srcs]
    land_refs = [jax.empty_ref(s, memory_space=pltpu.MemorySpace.HBM) for s in land_shapes]

    @pl.kernel(mesh=plsc.ScalarSubcoreMesh(axis_name="sequencer", num_cores=1), name=name,
               scratch_types=(pltpu.SemaphoreType.DMA((7 * n,)), pltpu.SemaphoreType.DMA((7 * n,)),
                              pltpu.SemaphoreType.DMA((n,))),
               compiler_params=pltpu.CompilerParams(collective_id=collective_id))
    def launch(send_sems, recv_sems, local_sems):
        place = _place()
        barrier = pltpu.get_barrier_semaphore()
        for to in _peers(place):
            pl.semaphore_signal(barrier, inc=1, device_id=to, device_id_type=MESH)
        pl.semaphore_wait(barrier, 7)
        remote, local = plan(place, src_refs, land_refs)
        mine = [pltpu.make_async_copy(s, d, local_sems.at[a]) for a, (s, d) in enumerate(local)]
        for cp in mine:
            cp.start()
        cps = [pltpu.make_async_remote_copy(src_ref=s, dst_ref=d, send_sem=send_sems.at[k], recv_sem=recv_sems.at[k],
                                            device_id=to, device_id_type=MESH) for k, (s, d, to) in enumerate(remote)]
        for cp in cps:
            cp.start()
        for cp in mine:
            cp.wait()
        for cp in cps:
            cp.wait()

    launch()
    return land_refs


def sibling_pair(name, hs):
    n = len(hs)

    def body(*refs):
        ins, outs = refs[:n], refs[n:2 * n]
        send_sems, recv_sems = refs[2 * n:]
        x, y, c = _place()
        cps = [pltpu.make_async_remote_copy(src_ref=outs[a].at[c], dst_ref=outs[a].at[c], send_sem=send_sems.at[a],
                                            recv_sem=recv_sems.at[a], device_id=(x, y, 1 - c), device_id_type=MESH)
               for a in range(n)]
        for cp in cps:
            cp.start()
        for a in range(n):
            pltpu.make_async_remote_copy(src_ref=outs[a].at[1 - c], dst_ref=outs[a].at[1 - c], send_sem=send_sems.at[a],
                                         recv_sem=recv_sems.at[a], device_id=(x, y, 1 - c),
                                         device_id_type=MESH).wait_recv()
        for cp in cps:
            cp.wait_send()

    any_spec = pl.BlockSpec(memory_space=pl.ANY)
    return pl.pallas_call(
        body, name=name, out_shape=[jax.ShapeDtypeStruct(h.shape, h.dtype) for h in hs],
        in_specs=[any_spec] * n, out_specs=[any_spec] * n, input_output_aliases={a: a for a in range(n)},
        scratch_shapes=[pltpu.SemaphoreType.DMA((n,)), pltpu.SemaphoreType.DMA((n,))])(*hs)


def owner_sum(name, land):
    _, r, w = land.shape
    T = r // 2

    def body(_, l_ref, o_ref):
        acc = l_ref[0].astype(F32)
        for j in range(1, 8):
            acc = acc + l_ref[j].astype(F32)
        o_ref[...] = acc

    grid_spec = pltpu.PrefetchScalarGridSpec(
        num_scalar_prefetch=1, grid=(2,),
        in_specs=[pl.BlockSpec((8, T, w), lambda i, at: (0, i, 0))],
        out_specs=pl.BlockSpec((None, T, w), lambda i, at: (at[0], i, 0)))
    at = jnp.stack([lax.axis_index("c")]).astype(jnp.int32)
    return pl.pallas_call(body, name=name, grid_spec=grid_spec, out_shape=jax.ShapeDtypeStruct((2, r, w), F32),
                          compiler_params=_cp("arbitrary"))(at, land)


W_IN_RUNS = ((0, 2, 1296, 376), (376, 3, 0, 1672), (2048, 1, 920, 752), (2800, 2, 0, 1296), (4096, 0, 0, 1024),
             (5120, 0, 1024, 648), (5768, 1, 0, 920))


def w_in_to_padded(name, g4):
    T = 128

    def body(g_ref, o_ref):
        o_ref[:, D_PROJ:PW] = jnp.zeros((T, PW - D_PROJ), o_ref.dtype)
        for (a, s, j0, w) in W_IN_RUNS:
            o_ref[:, a:a + w] = g_ref[s, :, j0:j0 + w]

    return pl.pallas_call(body, name=name, grid=(D // T,), in_specs=[pl.BlockSpec((4, T, 1672), lambda i: (0, i, 0))],
                          out_specs=pl.BlockSpec((T, PW), lambda i: (i, 0)),
                          out_shape=jax.ShapeDtypeStruct((D, PW), g4.dtype), compiler_params=_cp("arbitrary"))(g4)


def w_in_from_padded(name, gp):
    T = 128

    def body(g_ref, o_ref):
        for (a, s, j0, w) in W_IN_RUNS:
            o_ref[s, :, j0:j0 + w] = g_ref[:, a:a + w]

    return pl.pallas_call(body, name=name, grid=(D // T,), in_specs=[pl.BlockSpec((T, PW), lambda i: (i, 0))],
                          out_specs=pl.BlockSpec((4, T, 1672), lambda i: (0, i, 0)),
                          out_shape=jax.ShapeDtypeStruct((4, D, 1672), gp.dtype), compiler_params=_cp("arbitrary"))(gp)


def sum_devices(name, g):
    def body(g_ref, o_ref):
        acc = g_ref[0]
        for k in range(1, 8):
            acc = acc + g_ref[k]
        o_ref[...] = acc

    return pl.pallas_call(body, name=name, out_shape=jax.ShapeDtypeStruct(g.shape[1:], F32),
                          compiler_params=pltpu.CompilerParams(vmem_limit_bytes=VMEM_LIMIT_V7X))(g)


def adamw(name, w, g, m, v, T):
    r, wd = w.shape
    c1 = 1.0 - ADAM_B1 ** ADAM_STEP
    c2 = 1.0 - ADAM_B2 ** ADAM_STEP

    def body(w_ref, g_ref, m_ref, v_ref, d_ref, mo_ref, vo_ref):
        gv = g_ref[...]
        mn = ADAM_B1 * m_ref[...] + (1.0 - ADAM_B1) * gv
        vn = ADAM_B2 * v_ref[...] + (1.0 - ADAM_B2) * (gv * gv)
        d_ref[...] = -ADAM_LR * ((mn / c1) / (jnp.sqrt(vn / c2) + ADAM_EPS) + ADAM_WD * w_ref[...])
        mo_ref[...] = mn
        vo_ref[...] = vn

    spec = pl.BlockSpec((T, wd), lambda i: (i, 0))
    return pl.pallas_call(body, name=name, grid=(r // T,), in_specs=[spec] * 4, out_specs=[spec] * 3,
                          out_shape=[jax.ShapeDtypeStruct((r, wd), F32)] * 3, compiler_params=_cp("arbitrary"))(w, g, m, v)


BIG = {'w_in': (1024, 1672), 'w_ssd_proj': (256, 1024), 'w_gm_proj': (256, 1024), 'w_out': (256, 1024),
       'w_ff1': (1024, 704), 'w_ff3': (1024, 704), 'w_ff2': (704, 1024)}


class Flat:
    def __init__(self, segs):
        self.off, o = {}, 0
        for name, size in segs:
            self.off[name] = (o, size)
            o += -(-size // 128) * 128
        self.rows = -(-o // 1024) * 8

    def pack(self, vals):
        parts = []
        for name, (o, size) in self.off.items():
            v = vals[name].reshape(-1).astype(F32)
            parts.append(jnp.pad(v, (0, -(-size // 128) * 128 - size)))
        buf = jnp.concatenate(parts)
        return jnp.pad(buf, (0, self.rows * 128 - buf.shape[0])).reshape(self.rows, 128)

    def get(self, buf, name, shape=None):
        o, size = self.off[name]
        v = buf[o // 128:(o + size + 127) // 128].reshape(-1)[:size]
        return v if shape is None else v.reshape(shape)


PARTIALS = Flat([('loss', 1), ('ln0_g', D), ('ln0_b', D), ('dmod_x', 6 * D), ('dmod_c', 6 * D), ('conv_w', 5 * 1536),
                 ('conv_b', 1536), ('dt_bias', 32), ('a_log', 32), ('d_skip', 32), ('ssd_norm_g', D),
                 ('gm_norm_g', D), ('gm_norm_b', D), ('w_spatial', 8 * Q * Q), ('b_spatial', 8 * Q), ('b_gate', 2 * D),
                 ('ln1_g', D), ('ln1_b', D), ('ln2_g', D), ('ln2_b', D)])

WEIGHTS = ('c_ctx', 'ln0_g', 'ln0_b', 'w_ada', 'b_ada', 'w_in', 'conv_w', 'conv_b', 'dt_bias', 'a_log', 'd_skip',
           'ssd_norm_g', 'gm_norm_g', 'gm_norm_b', 'w_spatial', 'b_spatial', 'b_gate', 'w_ssd_proj', 'w_gm_proj',
           'w_out', 'ln1_g', 'ln1_b', 'w_ff1', 'w_ff3', 'w_ff2', 'ln2_g', 'ln2_b')
BIG_NAMES = tuple(BIG)
SMALL_NAMES = tuple(n for n in WEIGHTS if n not in BIG_NAMES and n != 'w_ada')


def kernel(x, c, ctx, c_ctx, ln0_g, ln0_b, w_ada, b_ada, w_in, conv_w, conv_b, dt_bias, a_log, d_skip, ssd_norm_g, gm_norm_g, gm_norm_b, w_spatial, b_spatial, b_gate, w_ssd_proj, w_gm_proj, w_out, ln1_g, ln1_b, w_ff1, w_ff3, w_ff2, ln2_g, ln2_b, loss_target, m_c_ctx, m_ln0_g, m_ln0_b, m_w_ada, m_b_ada, m_w_in, m_conv_w, m_conv_b, m_dt_bias, m_a_log, m_d_skip, m_ssd_norm_g, m_gm_norm_g, m_gm_norm_b, m_w_spatial, m_b_spatial, m_b_gate, m_w_ssd_proj, m_w_gm_proj, m_w_out, m_ln1_g, m_ln1_b, m_w_ff1, m_w_ff3, m_w_ff2, m_ln2_g, m_ln2_b, v_c_ctx, v_ln0_g, v_ln0_b, v_w_ada, v_b_ada, v_w_in, v_conv_w, v_conv_b, v_dt_bias, v_a_log, v_d_skip, v_ssd_norm_g, v_gm_norm_g, v_gm_norm_b, v_w_spatial, v_b_spatial, v_b_gate, v_w_ssd_proj, v_w_gm_proj, v_w_out, v_ln1_g, v_ln1_b, v_w_ff1, v_w_ff3, v_w_ff2, v_ln2_g, v_ln2_b):
    wts = dict(c_ctx=c_ctx, ln0_g=ln0_g, ln0_b=ln0_b, w_ada=w_ada, b_ada=b_ada, w_in=w_in, conv_w=conv_w, conv_b=conv_b,
               dt_bias=dt_bias, a_log=a_log, d_skip=d_skip, ssd_norm_g=ssd_norm_g, gm_norm_g=gm_norm_g,
               gm_norm_b=gm_norm_b, w_spatial=w_spatial, b_spatial=b_spatial, b_gate=b_gate, w_ssd_proj=w_ssd_proj,
               w_gm_proj=w_gm_proj, w_out=w_out, ln1_g=ln1_g, ln1_b=ln1_b, w_ff1=w_ff1, w_ff3=w_ff3, w_ff2=w_ff2,
               ln2_g=ln2_g, ln2_b=ln2_b)
    ms = dict(zip(WEIGHTS, (m_c_ctx, m_ln0_g, m_ln0_b, m_w_ada, m_b_ada, m_w_in, m_conv_w, m_conv_b, m_dt_bias, m_a_log,
                            m_d_skip, m_ssd_norm_g, m_gm_norm_g, m_gm_norm_b, m_w_spatial, m_b_spatial, m_b_gate,
                            m_w_ssd_proj, m_w_gm_proj, m_w_out, m_ln1_g, m_ln1_b, m_w_ff1, m_w_ff3, m_w_ff2, m_ln2_g,
                            m_ln2_b)))
    vs = dict(zip(WEIGHTS, (v_c_ctx, v_ln0_g, v_ln0_b, v_w_ada, v_b_ada, v_w_in, v_conv_w, v_conv_b, v_dt_bias, v_a_log,
                            v_d_skip, v_ssd_norm_g, v_gm_norm_g, v_gm_norm_b, v_w_spatial, v_b_spatial, v_b_gate,
                            v_w_ssd_proj, v_w_gm_proj, v_w_out, v_ln1_g, v_ln1_b, v_w_ff1, v_w_ff3, v_w_ff2, v_ln2_g,
                            v_ln2_b)))
    px, py, pc = _place()
    shard = 2 * px + py
    dev = 2 * shard + pc
    take = lambda a, i, axis=0: lax.dynamic_index_in_dim(a, i, axis, keepdims=False)

    half = lambda n: take(wts[n][0].reshape(2, BIG[n][0] // 2, BIG[n][1]), pc).astype(BF16)
    w_in_ref, = sequencer_exchange("gather_w_in", 5, [half('w_in')],
                                   [jax.ShapeDtypeStruct((8, BIG['w_in'][0] // 2, BIG['w_in'][1]), BF16)], plan_gather)

    pre = jnp.concatenate([c, jnp.pad(conv_w[0], ((0, 0), (0, D - 384))), jnp.zeros((2, D), F32)], axis=0)
    pre = allgather8("gather_cond", pre, False)
    conv_w_full = pre[0::2, 1:6, :384].transpose(1, 0, 2).reshape(5, 1536)
    a16 = jnp.concatenate([_silu(pre[:, 0, :]), _silu(c_ctx)[None], jnp.zeros((7, D), F32)], axis=0)
    mod = matmul("ada_fwd", a16, w_ada[0], 'nn', 16, 512, 1024)
    mod = mod + lax.dynamic_slice_in_dim(b_ada[0], shard * 1536, 1536)[None]
    mod = allgather8("gather_mod", mod, False)
    mod = jnp.concatenate([mod[0], mod[2], mod[4], mod[6]], axis=1)
    mod_x = take(mod, dev).reshape(6, D)
    mod_c = mod[8].reshape(6, D)

    def full(n, blocks):
        r, w = BIG[n]
        return blocks.reshape(4, r, w) if w != D else blocks.reshape(4 * r, w)

    class Exchanges:
        rest_names = BIG_NAMES[1:]

        def __init__(self):
            self.pending = []

        def w_in(self):
            blocks = w_in_ref[...]
            w = w_in_to_padded("w_in_layout", full('w_in', blocks))
            halves = [half(n) for n in self.rest_names]
            halves[0], _ = lax.optimization_barrier((halves[0], blocks))
            lands = [jax.ShapeDtypeStruct((8,) + h.shape, BF16) for h in halves]
            self.rest_refs = sequencer_exchange("gather_rest", 1, halves, lands, plan_gather)
            return w

        def rest(self):
            return {n: full(n, r[...]) for n, r in zip(self.rest_names, self.rest_refs)}

        def grads(self, group, gs):
            if group == 'in':
                gs = {'w_in': w_in_from_padded("w_in_grad_layout", gs['w_in'])}
            blocks = [g.reshape(4, 2, BIG[n][0] // 2, BIG[n][1]) for n, g in gs.items()]
            lands = [jax.ShapeDtypeStruct((8,) + b.shape[2:], BF16) for b in blocks]
            refs = sequencer_exchange("grads_" + group, 2 + len(self.pending), blocks, lands, plan_to_owner)
            self.pending.append((tuple(gs), refs))

        def finish(self):
            names, halves = [], []
            for ns, refs in self.pending:
                names += ns
                halves += [owner_sum("grads_sum_" + n, r[...]) for n, r in zip(ns, refs)]
            return {n: h.reshape(BIG[n]) for n, h in zip(names, sibling_pair("grads_halves", halves))}

    S = dict(ln0_g=ln0_g, ln0_b=ln0_b, conv_w=conv_w_full, conv_b=conv_b[0], dt_bias=dt_bias[0], a_log=a_log[0],
             d_skip=d_skip[0], ssd_norm_g=ssd_norm_g[0], gm_norm_g=gm_norm_g[0], gm_norm_b=gm_norm_b[0],
             w_spatial=w_spatial[0], b_spatial=b_spatial[0], b_gate=b_gate[0], ln1_g=ln1_g[0], ln1_b=ln1_b[0],
             ln2_g=ln2_g[0], ln2_b=ln2_b[0])
    x_all = jnp.concatenate([ctx[0], x[0]], axis=0)
    exchanges = Exchanges()
    grad_x, gsmall = core(x_all, loss_target[0], mod_x, mod_c, exchanges, S)

    parts = allgather8("gather_partials", PARTIALS.pack(gsmall), False)
    tot = sum_devices("partials_sum", parts)
    g_shards = exchanges.finish()
    g = {n: PARTIALS.get(tot, n) for n in ('ln0_g', 'ln0_b', 'conv_b', 'dt_bias', 'a_log', 'd_skip', 'ssd_norm_g',
                                           'gm_norm_g', 'gm_norm_b', 'w_spatial', 'b_spatial', 'b_gate', 'ln1_g',
                                           'ln1_b', 'ln2_g', 'ln2_b')}
    loss = PARTIALS.get(tot, 'loss', ())
    dmod_c = PARTIALS.get(tot, 'dmod_c')
    g['b_ada'] = PARTIALS.get(tot, 'dmod_x') + dmod_c
    g['conv_w'] = lax.dynamic_slice_in_dim(PARTIALS.get(tot, 'conv_w', (5, 1536)), shard * 384, 384, axis=1)
    o, size = PARTIALS.off['dmod_x']
    dmod_rows = parts[:, o // 128:(o + size) // 128].reshape(8, size)
    dm = jnp.concatenate([dmod_rows, dmod_c[None], jnp.zeros((7, 6 * D), F32)], axis=0)
    dm = lax.dynamic_slice_in_dim(dm, shard * 1536, 1536, axis=1)
    g['w_ada'] = matmul("ada_bwd_w", a16, dm, 'tn', 1024, 512, 16)
    dm_c = jnp.concatenate([dm[8:9], jnp.zeros((15, 1536), F32)], axis=0)
    dc = matmul("ada_bwd_c", dm_c, w_ada[0], 'nt', 16, 1024, 512)
    dc = allgather8("gather_dcctx", dc, False)[:, 0, :]
    dc = ((dc[0] + dc[2]) + dc[4]) + dc[6]
    sg = jax.nn.sigmoid(c_ctx)
    g['c_ctx'] = dc * (sg * (1.0 + c_ctx * (1.0 - sg)))
    for n in BIG_NAMES:
        g[n] = g_shards[n]

    delta, new_m, new_v = {}, {}, {}
    for n in BIG_NAMES + ('w_ada',):
        w2 = wts[n][0]
        T = 352 if n == 'w_ff2' else 256
        d_, m_, v_ = adamw("adamw_" + n, w2, g[n], ms[n][0], vs[n][0], T)
        delta[n], new_m[n], new_v[n] = d_, m_, v_
    lay = Flat([(n, wts[n].size) for n in SMALL_NAMES])
    d_, m_, v_ = adamw("adamw_small", lay.pack(wts), lay.pack(g), lay.pack(ms), lay.pack(vs), lay.rows)
    for n in SMALL_NAMES:
        delta[n], new_m[n], new_v[n] = (lay.get(b, n) for b in (d_, m_, v_))

    shp = lambda d: [d[n].reshape(wts[n].shape) for n in WEIGHTS]
    return (loss, grad_x[None], *shp(g), *shp(delta), *shp(new_m), *shp(new_v))
```

```python
import functools

import jax
import jax.numpy as jnp
from jax import lax
from jax.experimental import pallas as pl
from jax.experimental.pallas import tpu as pltpu
from jax.experimental.pallas import tpu_sc as plsc

F32 = jnp.float32
BF16 = jnp.bfloat16
MESH = pl.DeviceIdType.MESH

VMEM_LIMIT_V7X = 56 * 1024 * 1024

D = 1024
LC = 256
Q = 128
NH = 16
D_FF = 2816
LN_EPS = 1e-5
ALPHA = 2.0 ** 0.25

PW = 7168
C_GATE, C_UV, C_Z, C_XBC, C_DT = 0, 2048, 4096, 5120, 6656
D_PROJ = 6688

ADAM_LR, ADAM_B1, ADAM_B2, ADAM_EPS, ADAM_WD, ADAM_STEP = 0.001, 0.9, 0.999, 1e-08, 0.01, 10


def _cp(*sem):
    return pltpu.CompilerParams(dimension_semantics=sem, vmem_limit_bytes=VMEM_LIMIT_V7X)


def _dot(a, b, ca, cb):
    return lax.dot_general(a.astype(BF16), b.astype(BF16), (((ca,), (cb,)), ((), ())),
                           preferred_element_type=F32)


@jax.custom_vjp
def mm(a, b):
    return _dot(a, b, 1, 0)


mm.defvjp(lambda a, b: (_dot(a, b, 1, 0), (a, b)),
          lambda r, g: (_dot(g, r[1], 1, 1), _dot(r[0], g, 0, 0)))


@jax.custom_vjp
def mm_nt(a, b):
    return _dot(a, b, 1, 1)


mm_nt.defvjp(lambda a, b: (_dot(a, b, 1, 1), (a, b)),
             lambda r, g: (_dot(g, r[1], 1, 0), _dot(g, r[0], 0, 0)))


@jax.custom_vjp
def mm_tn(a, b):
    return _dot(a, b, 0, 0)


mm_tn.defvjp(lambda a, b: (_dot(a, b, 0, 0), (a, b)),
             lambda r, g: (_dot(r[1], g, 1, 1), _dot(r[0], g, 1, 0)))


def _dot32(a, b):
    return lax.dot_general(a, b, (((1,), (0,)), ((), ())), precision=lax.Precision.HIGHEST,
                           preferred_element_type=F32)


def _cumsum_fn(rev):
    def tri(transpose):
        r = lax.broadcasted_iota(jnp.int32, (Q, Q), 0)
        c = lax.broadcasted_iota(jnp.int32, (Q, Q), 1)
        keep = (r >= c) if (rev == transpose) else (r <= c)
        return jnp.where(keep, 1.0, 0.0).astype(F32)

    @jax.custom_vjp
    def cums(a):
        return _dot32(tri(False), a)

    cums.defvjp(lambda a: (_dot32(tri(False), a), None), lambda _, g: (_dot32(tri(True), g),))
    return cums


def _cols(v, k):
    w = v.shape[1] // k
    return tuple(v[:, w * i:w * (i + 1)] for i in range(k))


def _splitter(k):
    @jax.custom_vjp
    def split(v):
        return _cols(v, k)

    @jax.custom_vjp
    def concat(ps):
        return jnp.concatenate(ps, axis=1)

    split.defvjp(lambda v: (_cols(v, k), None), lambda _, g: (jnp.concatenate(g, axis=1),))
    concat.defvjp(lambda ps: (jnp.concatenate(ps, axis=1), None), lambda _, g: (_cols(g, k),))
    return split, concat


split2, _ = _splitter(2)
split4, _ = _splitter(4)
split8, concat8 = _splitter(8)


def _ln(x, g, b):
    mu = jnp.mean(x, axis=-1, keepdims=True)
    xc = x - mu
    var = jnp.mean(xc * xc, axis=-1, keepdims=True)
    return xc * lax.rsqrt(var + LN_EPS) * g + b


def _silu(x):
    return x * jax.nn.sigmoid(x)


def _gelu(x):
    return 0.5 * x * (1.0 + jnp.tanh(0.7978845608028654 * (x + 0.044715 * (x * x * x))))


def _xspec(T, w, col, roff):
    return pl.BlockSpec((T, w), lambda i, col=col, roff=roff: (jnp.maximum(i + roff, 0), col))


def _pspec(p, sel):
    if sel is None:
        return pl.BlockSpec(p.shape, lambda i, n=p.ndim: (0,) * n)
    return pl.BlockSpec((1,) + p.shape[1:], lambda i, n=p.ndim: (sel(i),) + (0,) * (n - 1))


def _out_plumbing(outs, T, args, in_specs):
    shapes, specs, aliases = [], [], {}
    for k, o in enumerate(outs):
        if o[0] == 'new':
            _, rows, w, roff = o[:4]
            shapes.append(jax.ShapeDtypeStruct((rows, w), o[4] if len(o) > 4 else F32))
            specs.append(_xspec(T, w, 0, roff))
        elif o[0] == 'acc':
            shapes.append(jax.ShapeDtypeStruct(o[1], F32))
            specs.append(pl.BlockSpec(o[1], lambda i, n=len(o[1]): (0,) * n))
        elif o[0] == 'part':
            _, rows, wtot, w, col, roff, dtype = o
            shapes.append(jax.ShapeDtypeStruct((rows, wtot), dtype))
            specs.append(_xspec(T, w, col, roff))
        else:
            _, arr, w, col, roff = o
            aliases[len(args)] = k
            args.append(arr)
            in_specs.append(pl.BlockSpec(memory_space=pl.ANY))
            shapes.append(jax.ShapeDtypeStruct(arr.shape, arr.dtype))
            specs.append(_xspec(T, w, col, roff))
    return shapes, specs, aliases


def stage_fwd(name, f, T, n, xs, ps, outs):
    nx, npar = len(xs), len(ps)
    args = [x[0] for x in xs] + [p[0] for p in ps]
    in_specs = [_xspec(T, w, col, roff) for (_, w, col, roff) in xs] + [_pspec(p, sel) for (p, sel) in ps]
    n_in = len(args)
    shapes, specs, aliases = _out_plumbing(outs, T, args, in_specs)
    n_all_in = len(args)

    def body(*refs):
        i = pl.program_id(0)
        xv = [r[...] for r in refs[:nx]]
        pv = [r[...] if ps[k][1] is None else r[0] for k, r in enumerate(refs[nx:n_in])]
        res = f(*xv, *pv)
        for k, o_ref in enumerate(refs[n_all_in:]):
            if outs[k][0] == 'acc':
                @pl.when(i == 0)
                def _(o_ref=o_ref, v=res[k]):
                    o_ref[...] = v

                @pl.when(i > 0)
                def _(o_ref=o_ref, v=res[k]):
                    o_ref[...] += v
            else:
                o_ref[...] = res[k].astype(o_ref.dtype)

    return pl.pallas_call(body, name=name, grid=(n,), in_specs=in_specs, out_specs=specs, out_shape=shapes,
                          input_output_aliases=aliases, compiler_params=_cp("arbitrary"))(*args)


def stage_bwd(name, f, T, n, xs, ps, cts, dxs, dps, primal=()):
    nx, npar = len(xs), len(ps)
    args = [x[0] for x in xs] + [p[0] for p in ps]
    in_specs = [_xspec(T, w, col, roff) for (_, w, col, roff) in xs] + [_pspec(p, sel) for (p, sel) in ps]
    ct_arrs = [c for c in cts if isinstance(c, tuple)]
    for (a, w, col, roff) in ct_arrs:
        args.append(a)
        in_specs.append(_xspec(T, w, col, roff))
    n_in = len(args)
    outs, out_of = [], []
    for k, o in enumerate(dxs):
        if o is not None:
            outs.append(o)
            out_of.append(('x', k))
    for k, want in enumerate(dps):
        if want:
            p, sel = ps[k]
            outs.append(('acc', p.shape))
            out_of.append(('p', k))
    for k, shape in primal:
        outs.append(('acc', shape))
        out_of.append(('r', k))
    shapes, specs, aliases = _out_plumbing(outs, T, args, in_specs)
    for j, (kind, k) in enumerate(out_of):
        if kind == 'p' and ps[k][1] is not None:
            p, sel = ps[k]
            specs[j] = pl.BlockSpec((1,) + p.shape[1:], lambda i, n=p.ndim, sel=sel: (sel(i),) + (0,) * (n - 1))
    n_all_in = len(args)

    def body(*refs):
        i = pl.program_id(0)
        xv = [r[...] for r in refs[:nx]]
        pv = [r[...] if ps[k][1] is None else r[0] for k, r in enumerate(refs[nx:nx + npar])]
        res, vjp_fn = jax.vjp(f, *xv, *pv)
        ctv, q = [], nx + npar
        for k, c in enumerate(cts):
            if c is None:
                ctv.append(jnp.zeros_like(res[k]))
            elif isinstance(c, tuple):
                v = refs[q][...]
                if c[3] < 0:
                    v = v * (i + c[3] >= 0).astype(F32)
                ctv.append(v)
                q += 1
            else:
                ctv.append(jnp.full_like(res[k], c))
        grads = vjp_fn(tuple(ctv))
        for j, o_ref in enumerate(refs[n_all_in:]):
            kind, k = out_of[j]
            if kind == 'x':
                o_ref[...] = grads[k].astype(o_ref.dtype)
            else:
                g = res[k] if kind == 'r' else grads[nx + k]
                sel = None if kind == 'r' else ps[k][1]
                if sel is None:
                    first = i == 0
                    tgt = o_ref
                else:
                    first = jnp.logical_or(i == 0, sel(i) != sel(jnp.maximum(i - 1, 0)))
                    tgt = o_ref.at[0]

                @pl.when(first)
                def _(tgt=tgt, g=g):
                    tgt[...] = g

                @pl.when(jnp.logical_not(first))
                def _(tgt=tgt, g=g):
                    tgt[...] += g

    return pl.pallas_call(body, name=name, grid=(n,), in_specs=in_specs, out_specs=specs, out_shape=shapes,
                          input_output_aliases=aliases, compiler_params=_cp("arbitrary"))(*args)


_CONTRACT = {'nn': (1, 0), 'nt': (1, 1), 'tn': (0, 0)}


def matmul(name, a, b, mode, tm, tn, tk, out_dtype=F32, add=None):
    if mode == 'nn':
        (M, K), (_, N) = a.shape, b.shape
    elif mode == 'nt':
        (M, K), (N, _) = a.shape, b.shape
    else:
        (K, M), (_, N) = a.shape, b.shape
    assert M % tm == 0 and N % tn == 0 and K % tk == 0, (name, M, N, K, tm, tn, tk)
    a_spec = (pl.BlockSpec((tk, tm), lambda j, i, k: (k, i)) if mode == 'tn'
              else pl.BlockSpec((tm, tk), lambda j, i, k: (i, k)))
    b_spec = (pl.BlockSpec((tn, tk), lambda j, i, k: (j, k)) if mode == 'nt'
              else pl.BlockSpec((tk, tn), lambda j, i, k: (k, j)))
    o_spec = pl.BlockSpec((tm, tn), lambda j, i, k: (i, j))
    return matmul_call(name, (N // tn, M // tm, K // tk), a, a_spec, b, b_spec, (M, N), o_spec, (tm, tn), mode,
                       out_dtype, add)


def matmul_call(name, grid, a, a_spec, b, b_spec, out_shape, o_spec, tile, mode, out_dtype=F32, add=None):
    tm, tn = tile
    nk = grid[2]
    ca, cb = _CONTRACT[mode]
    args, in_specs = [a, b], [a_spec, b_spec]
    if add is not None:
        args.append(add)
        in_specs.append(o_spec)

    def body(*refs):
        a_ref, b_ref = refs[0], refs[1]
        o_ref, acc = refs[-2], refs[-1]
        k = pl.program_id(2)
        if nk == 1:
            p = _dot(a_ref[...], b_ref[...], ca, cb)
            o_ref[...] = (p + refs[2][...] if add is not None else p).astype(out_dtype)
            return

        @pl.when(k == 0)
        def _():
            acc[...] = refs[2][...] if add is not None else jnp.zeros_like(acc)

        acc[...] += _dot(a_ref[...], b_ref[...], ca, cb)

        @pl.when(k == nk - 1)
        def _():
            o_ref[...] = acc[...].astype(out_dtype)

    return pl.pallas_call(body, name=name, grid=grid, in_specs=in_specs, out_specs=o_spec,
                          out_shape=jax.ShapeDtypeStruct(out_shape, out_dtype),
                          scratch_shapes=[pltpu.VMEM((tm, tn) if nk > 1 else (8, 128), F32)],
                          compiler_params=_cp("arbitrary", "arbitrary", "arbitrary"))(*args)


NS, WS = 4, 704


def _resident(name, M, tm, rows, weight, out_shape, out_block, out_map, step, add=None):
    args = [rows[0], weight] + ([] if add is None else [add])
    in_specs = [pl.BlockSpec(rows[1], rows[2]), pl.BlockSpec(weight.shape, lambda i, n=weight.ndim: (0,) * n)]
    if add is not None:
        in_specs.append(pl.BlockSpec(out_block, out_map))
    return pl.pallas_call(step, name=name, grid=(M // tm,), in_specs=in_specs, out_specs=pl.BlockSpec(out_block, out_map),
                          out_shape=jax.ShapeDtypeStruct(out_shape, F32), compiler_params=_cp("arbitrary"))(*args)


def ffn_in_fwd(name, h, w1, w3, tm):
    M = h.shape[0]

    def step(h_ref, w1_ref, w3_ref, a1_ref, a3_ref, act_ref):
        for s in range(NS):
            a1 = _dot(h_ref[...], w1_ref[s], 1, 0)
            a3 = _dot(h_ref[...], w3_ref[s], 1, 0)
            a1_ref[s] = a1.astype(a1_ref.dtype)
            a3_ref[s] = a3.astype(a3_ref.dtype)
            act_ref[s] = (_silu(a1) * a3).astype(act_ref.dtype)

    wspec = pl.BlockSpec((NS, D, WS), lambda i: (0, 0, 0))
    ospec = pl.BlockSpec((NS, tm, WS), lambda i: (0, i, 0))
    return pl.pallas_call(
        step, name=name, grid=(M // tm,), in_specs=[pl.BlockSpec((tm, D), lambda i: (i, 0)), wspec, wspec],
        out_specs=[ospec, ospec, ospec],
        out_shape=[jax.ShapeDtypeStruct((NS, M, WS), BF16)] * 3, compiler_params=_cp("arbitrary"))(h, w1, w3)


def ffn_out_bwd_x(name, dff, w2, a1, a3, tm):
    M = dff.shape[0]

    def step(d_ref, w_ref, a1_ref, a3_ref, da1_ref, da3_ref):
        for s in range(NS):
            dact = _dot(d_ref[...], w_ref[s * WS:(s + 1) * WS, :], 1, 1)
            a1 = a1_ref[s].astype(F32)
            sig = jax.nn.sigmoid(a1)
            da3_ref[s] = (dact * (a1 * sig)).astype(da3_ref.dtype)
            da1_ref[s] = (dact * a3_ref[s].astype(F32) * (sig * (1.0 + a1 * (1.0 - sig)))).astype(da1_ref.dtype)

    aspec = pl.BlockSpec((NS, tm, WS), lambda i: (0, i, 0))
    return pl.pallas_call(
        step, name=name, grid=(M // tm,),
        in_specs=[pl.BlockSpec((tm, D), lambda i: (i, 0)), pl.BlockSpec(w2.shape, lambda i: (0, 0)), aspec, aspec],
        out_specs=[aspec, aspec],
        out_shape=[jax.ShapeDtypeStruct((NS, M, WS), BF16)] * 2, compiler_params=_cp("arbitrary"))(dff, w2, a1, a3)


def ff_in_bwd_x(name, da3, w3, tm, add=None):
    M = da3.shape[1]

    def step(*refs):
        d_ref, w_ref, o_ref = refs[0], refs[1], refs[-1]
        acc = _dot(d_ref[0], w_ref[0], 1, 1)
        for s in range(1, NS):
            acc = acc + _dot(d_ref[s], w_ref[s], 1, 1)
        o_ref[...] = acc if add is None else acc + refs[2][...]

    return _resident(name, M, tm, (da3, (NS, tm, WS), lambda i: (0, i, 0)), w3, (M, D), (tm, D), lambda i: (i, 0), step, add)


def ff_in_bwd_w(name, h, da3, tk):
    M = h.shape[0]
    return matmul_call(name, (NS, 1, M // tk), h, pl.BlockSpec((tk, D), lambda j, i, k: (k, 0)),
                       da3, pl.BlockSpec((None, tk, WS), lambda j, i, k: (j, k, 0)),
                       (NS, D, WS), pl.BlockSpec((None, D, WS), lambda j, i, k: (j, 0, 0)), (D, WS), 'tn', BF16)


def ff_out_fwd(name, act3, w2, tm):
    M = act3.shape[1]

    def step(a_ref, w_ref, o_ref):
        acc = _dot(a_ref[0], w_ref[0:WS, :], 1, 0)
        for s in range(1, NS):
            acc = acc + _dot(a_ref[s], w_ref[s * WS:(s + 1) * WS, :], 1, 0)
        o_ref[...] = acc

    return _resident(name, M, tm, (act3, (NS, tm, WS), lambda i: (0, i, 0)), w2, (M, D), (tm, D), lambda i: (i, 0), step)


def ff_out_bwd_w(name, act3, dff, tk):
    M = dff.shape[0]
    return matmul_call(name, (1, NS, M // tk), act3, pl.BlockSpec((None, tk, WS), lambda j, i, k: (i, k, 0)),
                       dff, pl.BlockSpec((tk, D), lambda j, i, k: (k, 0)),
                       (NS * WS, D), pl.BlockSpec((WS, D), lambda j, i, k: (i, 0)), (WS, D), 'tn', BF16)


def _shift_rows(x, d):
    n = x.shape[0]
    if d == 0:
        return x
    y = pltpu.roll(x, (-d) % n, 0)
    t = lax.broadcasted_iota(jnp.int32, x.shape, 0)
    ok = (t + d >= 0) & (t + d < n)
    return jnp.where(ok, y, 0.0)


def _conv_pre(x, w_ref, b_ref):
    acc = jnp.broadcast_to(b_ref[...], x.shape)
    for k in range(5):
        acc = acc + _shift_rows(x, k - 2) * w_ref[k:k + 1, :]
    return acc


def conv_fwd(name, proj, conv_w, conv_b, R):
    segs = ((0, LC), (LC, R))

    def body(x_ref, w_ref, b_ref, o_ref):
        for (s, e) in segs:
            pre = _conv_pre(x_ref[s:e, :], w_ref, b_ref)
            o_ref[s:e, :] = _silu(pre)

    return pl.pallas_call(
        body, name=name, grid=(12,),
        in_specs=[pl.BlockSpec((R, 128), lambda j: (0, C_XBC // 128 + j)),
                  pl.BlockSpec((8, 128), lambda j: (0, j)), pl.BlockSpec((1, 128), lambda j: (0, j))],
        out_specs=pl.BlockSpec((R, 128), lambda j: (0, j)),
        out_shape=jax.ShapeDtypeStruct((R, 1536), F32), compiler_params=_cp("arbitrary"))(proj, conv_w, conv_b)


def conv_bwd(name, proj, conv_w, conv_b, d_f, d_b, d_skip, dproj, R):
    segs = ((0, LC), (LC, R))

    def body(x_ref, w_ref, b_ref, df_ref, db_ref, ds_ref, _, dx_ref, dw_ref, dbias_ref):
        j = pl.program_id(0)
        has_skip = (j < 8).astype(F32)
        dw = [jnp.zeros((1, 128), F32) for _ in range(5)]
        dbias = jnp.zeros((1, 128), F32)
        for (s, e) in segs:
            x = x_ref[s:e, :]
            pre = _conv_pre(x, w_ref, b_ref)
            sig = jax.nn.sigmoid(pre)
            dy = df_ref[s:e, :] + db_ref[s:e, :]
            if s == LC:
                dy = dy + ds_ref[...] * has_skip
            dpre = dy * (sig * (1.0 + pre * (1.0 - sig)))
            dx = jnp.zeros_like(x)
            for k in range(5):
                dx = dx + _shift_rows(dpre, 2 - k) * w_ref[k:k + 1, :]
                dw[k] = dw[k] + jnp.sum(dpre * _shift_rows(x, k - 2), axis=0, keepdims=True)
            dbias = dbias + jnp.sum(dpre, axis=0, keepdims=True)
            dx_ref[s:e, :] = dx.astype(dx_ref.dtype)
        dw_ref[...] = jnp.zeros_like(dw_ref)
        for k in range(5):
            dw_ref[k:k + 1, :] = dw[k]
        dbias_ref[...] = dbias

    return pl.pallas_call(
        body, name=name, grid=(12,),
        in_specs=[pl.BlockSpec((R, 128), lambda j: (0, C_XBC // 128 + j)),
                  pl.BlockSpec((8, 128), lambda j: (0, j)), pl.BlockSpec((1, 128), lambda j: (0, j)),
                  pl.BlockSpec((R, 128), lambda j: (0, j)), pl.BlockSpec((R, 128), lambda j: (0, j)),
                  pl.BlockSpec((R - LC, 128), lambda j: (0, jnp.minimum(j, 7))),
                  pl.BlockSpec(memory_space=pl.ANY)],
        out_specs=[pl.BlockSpec((R, 128), lambda j: (0, C_XBC // 128 + j)),
                   pl.BlockSpec((8, 128), lambda j: (0, j)), pl.BlockSpec((1, 128), lambda j: (0, j))],
        out_shape=[jax.ShapeDtypeStruct(dproj.shape, dproj.dtype), jax.ShapeDtypeStruct((8, 1536), F32),
                   jax.ShapeDtypeStruct((1, 1536), F32)],
        input_output_aliases={6: 0}, compiler_params=_cp("arbitrary"))(proj, conv_w, conv_b, d_f, d_b, d_skip, dproj)


def _rows16_fn(base):
    @jax.custom_vjp
    def rows16(a):
        return tuple(a[base + h:base + h + 1, :] for h in range(NH))

    def bwd(_, g):
        pick = lax.broadcasted_iota(jnp.int32, (NH, 1), 0)
        blk = sum(jnp.where(pick == h, g[h], 0.0) for h in range(NH))
        parts = [jnp.zeros((base, 128), F32)] * (base > 0) + [blk, jnp.zeros((Q - base - NH, 128), F32)]
        return (jnp.concatenate(parts, axis=0),)

    rows16.defvjp(lambda a: (rows16(a), None), bwd)
    return rows16


def _ssd_chunk(rev, dirn):
    cums = _cumsum_fn(rev)
    rows16 = _rows16_fn(16 * dirn)

    def f(xs, Bs, Cs, dt, alog, Hs):
        lane = lax.broadcasted_iota(jnp.int32, (1, 128), 1)
        sub = lax.broadcasted_iota(jnp.int32, (Q, 1), 0)
        r = lax.broadcasted_iota(jnp.int32, (Q, Q), 0)
        c = lax.broadcasted_iota(jnp.int32, (Q, Q), 1)
        mask = (r <= c) if rev else (r >= c)
        left = lane < 64
        a = dt * (-jnp.exp(alog))
        s = cums(a)
        s_rows = rows16(s.T)
        last_row = (sub == (0 if rev else Q - 1)).astype(F32)
        s_last = jnp.sum(s * last_row, axis=0, keepdims=True)
        G = [mm_nt(Cs[g], Bs[g]) for g in range(2)]
        M, es, wc, ed, dtc = [], [], [], [], []
        for h in range(NH):
            oh_l = (lane == 16 * dirn + h).astype(F32)
            s_col = jnp.sum(s * oh_l, axis=1, keepdims=True)
            dt_col = jnp.sum(dt * oh_l, axis=1, keepdims=True)
            sl = jnp.sum(s_last * oh_l, axis=1, keepdims=True)
            M.append(G[h // 8] * jnp.exp(jnp.where(mask, s_col - s_rows[h], -1e30)))
            es.append(jnp.exp(s_col))
            wc.append(jnp.exp(sl - s_col) * dt_col)
            ed.append(jnp.exp(sl))
            dtc.append(dt_col)
        Ys, Hn = [], []
        for j in range(8):
            g = j // 4
            xd = xs[j] * jnp.where(left, dtc[2 * j], dtc[2 * j + 1])
            xa = jnp.where(left, xd, 0.0)
            xb = jnp.where(left, 0.0, xd)
            yd = mm(M[2 * j], xa) + mm(M[2 * j + 1], xb)
            yo = mm(Cs[g], Hs[j]) * jnp.where(left, es[2 * j], es[2 * j + 1])
            Ys.append(yd + yo)
            st = mm_tn(Bs[g], xs[j] * jnp.where(left, wc[2 * j], wc[2 * j + 1]))
            Hn.append(Hs[j] * jnp.where(left, ed[2 * j], ed[2 * j + 1]) + st)
        return Ys, Hn

    return f


def _chunk_of(t, n, rev):
    if not rev:
        return t
    return jnp.where(t < 2, 1 - t, n + 1 - t)


def ssd_fwd(name, xbc, dt, alog, n, rev, dirn):
    chunk = _ssd_chunk(rev, dirn)

    def body(x_ref, b_ref, c_ref, dt_ref, al_ref, y_ref, hs_ref, h_scr):
        @pl.when(pl.program_id(0) == 0)
        def _():
            h_scr[...] = jnp.zeros_like(h_scr)

        xs = [x_ref[:, 128 * j:128 * (j + 1)] for j in range(8)]
        Bs = [b_ref[:, 128 * g:128 * (g + 1)] for g in range(2)]
        Cs = [c_ref[:, 128 * g:128 * (g + 1)] for g in range(2)]
        Hs = [h_scr[:, 128 * j:128 * (j + 1)] for j in range(8)]
        hs_ref[0] = h_scr[...]
        Ys, Hn = chunk(xs, Bs, Cs, dt_ref[...], al_ref[...], Hs)
        for j in range(8):
            y_ref[:, 128 * j:128 * (j + 1)] = Ys[j]
            h_scr[:, 128 * j:128 * (j + 1)] = Hn[j]

    cm = lambda t: _chunk_of(t, n, rev)
    return pl.pallas_call(
        body, name=name, grid=(n,),
        in_specs=[pl.BlockSpec((Q, 1024), lambda t: (cm(t), 0)), pl.BlockSpec((Q, 256), lambda t: (cm(t), 4)),
                  pl.BlockSpec((Q, 256), lambda t: (cm(t), 5)), pl.BlockSpec((Q, 128), lambda t: (cm(t), 0)),
                  pl.BlockSpec((1, 128), lambda t: (0, 0))],
        out_specs=[pl.BlockSpec((Q, 1024), lambda t: (cm(t), 0)), pl.BlockSpec((1, Q, 1024), lambda t: (cm(t), 0, 0))],
        out_shape=[jax.ShapeDtypeStruct((n * Q, 1024), F32), jax.ShapeDtypeStruct((n, Q, 1024), F32)],
        scratch_shapes=[pltpu.VMEM((Q, 1024), F32)], compiler_params=_cp("arbitrary"))(xbc, xbc, xbc, dt, alog)


def ssd_bwd(name, xbc, dt, alog, hs, dy, n, rev, dirn):
    chunk = _ssd_chunk(rev, dirn)

    def body(x_ref, b_ref, c_ref, dt_ref, al_ref, hs_ref, dy_ref, dx_ref, ddt_ref, dal_ref, dh_scr):
        tt = pl.program_id(0)
        ch = _chunk_of(n - 1 - tt, n, rev)

        @pl.when(tt == 0)
        def _():
            dh_scr[...] = jnp.zeros_like(dh_scr)

        xs = [x_ref[:, 128 * j:128 * (j + 1)] for j in range(8)]
        Bs = [b_ref[:, 128 * g:128 * (g + 1)] for g in range(2)]
        Cs = [c_ref[:, 128 * g:128 * (g + 1)] for g in range(2)]
        Hs = [hs_ref[0, :, 128 * j:128 * (j + 1)] for j in range(8)]
        live = (ch >= 2).astype(F32)
        dYs = [dy_ref[:, 128 * j:128 * (j + 1)] * live for j in range(8)]
        dHn = [dh_scr[:, 128 * j:128 * (j + 1)] for j in range(8)]
        _, vjp_fn = jax.vjp(chunk, xs, Bs, Cs, dt_ref[...], al_ref[...], Hs)
        dxs, dBs, dCs, ddt, dal, dHs = vjp_fn((dYs, dHn))
        for j in range(8):
            dx_ref[:, 128 * j:128 * (j + 1)] = dxs[j]
            dh_scr[:, 128 * j:128 * (j + 1)] = dHs[j]
        for g in range(2):
            dx_ref[:, 1024 + 128 * g:1024 + 128 * (g + 1)] = dBs[g]
            dx_ref[:, 1280 + 128 * g:1280 + 128 * (g + 1)] = dCs[g]
        ddt_ref[...] = ddt

        @pl.when(tt == 0)
        def _():
            dal_ref[...] = dal

        @pl.when(tt > 0)
        def _():
            dal_ref[...] += dal

    cm = lambda t: _chunk_of(n - 1 - t, n, rev)
    return pl.pallas_call(
        body, name=name, grid=(n,),
        in_specs=[pl.BlockSpec((Q, 1024), lambda t: (cm(t), 0)), pl.BlockSpec((Q, 256), lambda t: (cm(t), 4)),
                  pl.BlockSpec((Q, 256), lambda t: (cm(t), 5)), pl.BlockSpec((Q, 128), lambda t: (cm(t), 0)),
                  pl.BlockSpec((1, 128), lambda t: (0, 0)), pl.BlockSpec((1, Q, 1024), lambda t: (cm(t), 0, 0)),
                  pl.BlockSpec((Q, 1024), lambda t: (jnp.maximum(cm(t) - 2, 0), 0))],
        out_specs=[pl.BlockSpec((Q, 1536), lambda t: (cm(t), 0)), pl.BlockSpec((Q, 128), lambda t: (cm(t), 0)),
                   pl.BlockSpec((1, 128), lambda t: (0, 0))],
        out_shape=[jax.ShapeDtypeStruct((n * Q, 1536), F32), jax.ShapeDtypeStruct((n * Q, 128), F32),
                   jax.ShapeDtypeStruct((1, 128), F32)],
        scratch_shapes=[pltpu.VMEM((Q, 1024), F32)], compiler_params=_cp("arbitrary"))(xbc, xbc, xbc, dt, alog, hs, dy)


def f_norm0(x, g0, b0, sc, sh):
    x0 = _ln(x, g0, b0)
    return x0, x0 * (1.0 + sc) + sh


def f_dt(raw, bias):
    z = split4(raw)[0] + bias
    dt = jnp.maximum(z, 0.0) + jnp.log1p(jnp.exp(-jnp.abs(z)))
    return dt, dt


def f_gated_norm(yf, yb, xs, z, dcol, g):
    h = (yf + yb + xs * dcol) * _silu(z)
    return (h * lax.rsqrt(jnp.mean(h * h, axis=-1, keepdims=True) + LN_EPS) * g,)


def f_gmlp(uv, gmg, gmb, *wb):
    ws, bs = wb[:8], wb[8:]
    u, v = split2(uv)
    vn = split8(_ln(_gelu(v), gmg, gmb))
    mixed = concat8(tuple(mm(ws[g], vn[g]) + bs[g] for g in range(8)))
    return (_gelu(u) * mixed,)


def f_merge(ps, pg, gates, bg):
    gs, gg = split2(jax.nn.sigmoid(gates + bg))
    return (gs * ps + gg * pg,)


def f_res1(x0, out, g1, lg, lb, sc, sh):
    x1 = _ln(ALPHA * x0 + g1 * out, lg, lb)
    return x1, x1 * (1.0 + sc) + sh


def f_res2_loss(x1, ff, tgt, g2, lg, lb):
    x2 = _ln(ALPHA * x1 + g2 * ff, lg, lb)
    e = x2 - tgt
    return (0.5 * jnp.sum(jnp.mean(e * e, axis=-1, keepdims=True), axis=0, keepdims=True),)


def _row_tile(M):
    return 544 if M % 544 == 0 else (512 if M % 512 == 0 else M)


def core(x_all, tgt, mod_x, mod_c, X, S):
    R = x_all.shape[0]
    L = R - LC
    n = R // Q
    T = 256
    nt, ntl = R // T, L // T
    tmR, tmL = _row_tile(R), _row_tile(L)
    tkR = 256 if R % 512 else 512
    tkL = 512 if L % 512 == 0 else 256
    row = lambda v: v.reshape(1, -1)
    mx = [row(mod_x[k]) for k in range(6)]
    mc = [row(mod_c[k]) for k in range(6)]
    sel = lambda i: jnp.minimum(i, 1)
    sc1 = jnp.stack([mc[1], mx[1]])
    sh1 = jnp.stack([mc[0], mx[0]])
    ln0 = [(row(S['ln0_g']), None), (row(S['ln0_b']), None), (sc1, sel), (sh1, sel)]

    x0, xm = stage_fwd("norm0_fwd", f_norm0, T, nt, [(x_all, D, 0, 0)], ln0, [('new', R, D, 0), ('new', R, D, 0, BF16)])
    w_in = X.w_in()
    proj = matmul("proj_fwd", xm, w_in, 'nn', tmR, PW // 2, 1024)
    conv_w8 = jnp.pad(S['conv_w'], ((0, 3), (0, 0)))
    conv_b = row(S['conv_b'])
    xbc = conv_fwd("conv_fwd", proj, conv_w8, conv_b, R)
    dt_bias = jnp.pad(S['dt_bias'].reshape(1, 32), ((0, 0), (0, 96)))
    alog = jnp.pad(S['a_log'].reshape(1, 32), ((0, 0), (0, 96)))
    x_dt = [(proj, 512, C_DT // 512, 0)]
    dt_f, dt_b = stage_fwd("dt_fwd", f_dt, T, nt, x_dt, [(dt_bias, None)], [('new', R, 128, 0), ('new', R, 128, 0)])
    y_f, hs_f = ssd_fwd("ssd_fwd_f", xbc, dt_f, alog, n, False, 0)
    y_b, hs_b = ssd_fwd("ssd_fwd_b", xbc, dt_b, alog, n, True, 1)
    W = X.rest()
    dcol = jnp.repeat(S['d_skip'][0] + S['d_skip'][1], 64).reshape(1, D)
    x_gn = [(y_f, D, 0, 1), (y_b, D, 0, 1), (xbc, D, 0, 1), (proj, D, C_Z // D, 1)]
    p_gn = [(dcol, None), (row(S['ssd_norm_g']), None)]
    (yn,) = stage_fwd("gnorm_fwd", f_gated_norm, T, ntl, x_gn, p_gn, [('new', L, D, 0, BF16)])
    x_gm = [(proj, 2 * D, C_UV // (2 * D), LC // Q)]
    p_gm = ([(row(S['gm_norm_g']), None), (row(S['gm_norm_b']), None)]
            + [(S['w_spatial'][g], None) for g in range(8)] + [(S['b_spatial'][g].reshape(Q, 1), None) for g in range(8)])
    (y_gm,) = stage_fwd("gmlp_fwd", f_gmlp, Q, L // Q, x_gm, p_gm, [('new', L, D, 0, BF16)])
    p_ssd = matmul("pssd_fwd", yn, W['w_ssd_proj'], 'nn', tmL, 1024, 1024)
    p_g = matmul("pgm_fwd", y_gm, W['w_gm_proj'], 'nn', tmL, 1024, 1024)
    x_mg = [(p_ssd, D, 0, 0), (p_g, D, 0, 0), (proj, 2 * D, C_GATE // (2 * D), 1)]
    p_mg = [(row(S['b_gate']), None)]
    (merged,) = stage_fwd("merge_fwd", f_merge, T, ntl, x_mg, p_mg, [('new', L, D, 0, BF16)])
    out = matmul("out_fwd", merged, W['w_out'], 'nn', tmL, 1024, 1024)
    x_r1 = [(x0, D, 0, 1), (out, D, 0, 0)]
    p_r1 = [(mx[2], None), (row(S['ln1_g']), None), (row(S['ln1_b']), None), (mx[4], None), (mx[3], None)]
    x1, hm = stage_fwd("res1_fwd", f_res1, T, ntl, x_r1, p_r1, [('new', L, D, 0), ('new', L, D, 0, BF16)])
    a1, a3, act = ffn_in_fwd("ffn_in_fwd", hm, W['w_ff1'], W['w_ff3'], T)
    ff = ff_out_fwd("ff2_fwd", act, W['w_ff2'], tmL)
    x_r2 = [(x1, D, 0, 0), (ff, D, 0, 0), (tgt, D, 0, 0)]
    p_r2 = [(mx[5], None), (row(S['ln2_g']), None), (row(S['ln2_b']), None)]

    dx1_a, dff, dg2, dl2g, dl2b, loss = stage_bwd(
        "res2_bwd", f_res2_loss, T, ntl, x_r2, p_r2, [1.0],
        [('new', L, D, 0), ('new', L, D, 0, BF16), None], [True, True, True], primal=[(0, (1, 1))])
    da1, da3 = ffn_out_bwd_x("ffn_out_bwd_x", dff, W['w_ff2'], a1, a3, T)
    gw_ff2 = ff_out_bwd_w("ff2_bwd_w", act, dff, tkL)
    dhm = ff_in_bwd_x("ff1_bwd_x", da1, W['w_ff1'], tmL)
    dhm = ff_in_bwd_x("ff3_bwd_x", da3, W['w_ff3'], tmL, add=dhm)
    gw_ff1 = ff_in_bwd_w("ff1_bwd_w", hm, da1, tkL)
    gw_ff3 = ff_in_bwd_w("ff3_bwd_w", hm, da3, tkL)
    X.grads('ffn', {'w_ff2': gw_ff2, 'w_ff1': gw_ff1, 'w_ff3': gw_ff3})
    dx0_a, dout, dg1, dl1g, dl1b, dsc2, dsh2 = stage_bwd(
        "res1_bwd", f_res1, T, ntl, x_r1, p_r1, [(dx1_a, D, 0, 0), (dhm, D, 0, 0)],
        [('new', L, D, 0), ('new', L, D, 0, BF16)], [True] * 5)
    dmerged = matmul("out_bwd_x", dout, W['w_out'], 'nt', tmL, 1024, 1024)
    gw_out = matmul("out_bwd_w", merged, dout, 'tn', 1024, 1024, tkL, BF16)
    lt, lq = -(LC // T), -(LC // Q)
    x_mg_b = [(p_ssd, D, 0, lt), (p_g, D, 0, lt), (proj, 2 * D, C_GATE // (2 * D), 0)]
    dp_ssd, dp_g, dproj, dbg = stage_bwd(
        "merge_bwd", f_merge, T, nt, x_mg_b, p_mg, [(dmerged, D, 0, lt)],
        [('new', L, D, lt, BF16), ('new', L, D, lt, BF16), ('part', R, PW, 2 * D, C_GATE // (2 * D), 0, BF16)], [True])
    dyn = matmul("pssd_bwd_x", dp_ssd, W['w_ssd_proj'], 'nt', tmL, 1024, 1024)
    gw_ssd = matmul("pssd_bwd_w", yn, dp_ssd, 'tn', 1024, 1024, tkL, BF16)
    dy_gm = matmul("pgm_bwd_x", dp_g, W['w_gm_proj'], 'nt', tmL, 1024, 1024)
    gw_gm = matmul("pgm_bwd_w", y_gm, dp_g, 'tn', 1024, 1024, tkL, BF16)
    X.grads('proj', {'w_out': gw_out, 'w_ssd_proj': gw_ssd, 'w_gm_proj': gw_gm})
    r_gm = stage_bwd("gmlp_bwd", f_gmlp, Q, n, [(proj, 2 * D, C_UV // (2 * D), 0)], p_gm, [(dy_gm, D, 0, lq)],
                     [('alias', dproj, 2 * D, C_UV // (2 * D), 0)], [True] * 18)
    dproj, dgmg, dgmb, dws, dbs = r_gm[0], r_gm[1], r_gm[2], r_gm[3:11], r_gm[11:19]
    x_gn_b = [(y_f, D, 0, 0), (y_b, D, 0, 0), (xbc, D, 0, 0), (proj, D, C_Z // D, 0)]
    dy, dskipx, dproj, ddcol, dng = stage_bwd(
        "gnorm_bwd", f_gated_norm, T, nt, x_gn_b, p_gn, [(dyn, D, 0, lt)],
        [('new', L, D, lt), None, ('new', L, D, lt), ('alias', dproj, D, C_Z // D, 0)], [True, True])
    dxbc_f, ddt_f, dal_f = ssd_bwd("ssd_bwd_f", xbc, dt_f, alog, hs_f, dy, n, False, 0)
    dxbc_b, ddt_b, dal_b = ssd_bwd("ssd_bwd_b", xbc, dt_b, alog, hs_b, dy, n, True, 1)
    dproj, ddtb = stage_bwd("dt_bwd", f_dt, T, nt, x_dt, [(dt_bias, None)],
                            [(ddt_f, 128, 0, 0), (ddt_b, 128, 0, 0)],
                            [('alias', dproj, 512, C_DT // 512, 0)], [True])
    dproj, dcw8, dcb = conv_bwd("conv_bwd", proj, conv_w8, conv_b, dxbc_f, dxbc_b, dskipx, dproj, R)
    gw_in = matmul("proj_bwd_w", xm, dproj, 'tn', 1024, PW // 4, tkR, BF16)
    X.grads('in', {'w_in': gw_in})
    dxm = matmul("proj_bwd_x", dproj, w_in, 'nt', R // 4 if R % 32 == 0 else R, 1024, 1024)
    grad_x, dl0g, dl0b, dsc1, dsh1 = stage_bwd(
        "norm0_bwd", f_norm0, T, nt, [(x_all, D, 0, 0)], ln0, [(dx0_a, D, 0, -1), (dxm, D, 0, 0)],
        [('new', L, D, -1)], [True] * 4)

    zero = jnp.zeros((D,), F32)
    flat = lambda v: v.reshape(-1)
    small = {
        'loss': flat(loss), 'ln0_g': flat(dl0g), 'ln0_b': flat(dl0b),
        'dmod_x': jnp.concatenate([flat(dsh1[1]), flat(dsc1[1]), flat(dg1), flat(dsh2), flat(dsc2), flat(dg2)]),
        'dmod_c': jnp.concatenate([flat(dsh1[0]), flat(dsc1[0]), zero, zero, zero, zero]),
        'conv_w': flat(dcw8[:5]), 'conv_b': flat(dcb), 'dt_bias': flat(ddtb[:, :32]),
        'a_log': flat((dal_f + dal_b)[:, :32]),
        'd_skip': flat(jnp.tile(ddcol.reshape(1, NH, 64).sum(-1), (2, 1))),
        'ssd_norm_g': flat(dng), 'gm_norm_g': flat(dgmg), 'gm_norm_b': flat(dgmb),
        'w_spatial': flat(jnp.stack(dws)), 'b_spatial': flat(jnp.stack(dbs)), 'b_gate': flat(dbg),
        'ln1_g': flat(dl1g), 'ln1_b': flat(dl1b), 'ln2_g': flat(dl2g), 'ln2_b': flat(dl2b),
    }
    return grad_x, small


def _place():
    return lax.axis_index("x"), lax.axis_index("y"), lax.axis_index("c")


def allgather8(name, blk, hbm):
    space = pl.ANY if hbm else pltpu.VMEM

    def body(x_ref, out_ref, send_sems, recv_sems, local_sem):
        x, y, c = _place()
        me, sibling = (x, y, c), (x, y, 1 - c)
        chips = [(1 - x, y), (x, 1 - y), (1 - x, 1 - y)]

        def slot(px, py, pc):
            return out_ref.at[4 * px + 2 * py + pc]

        def copy(k, block, to, src=None):
            return pltpu.make_async_remote_copy(
                src_ref=slot(*block) if src is None else src, dst_ref=slot(*block),
                send_sem=send_sems.at[k], recv_sem=recv_sems.at[k], device_id=to, device_id_type=MESH)

        mine = pltpu.make_async_copy(x_ref, slot(*me), local_sem)
        mine.start()
        first = [copy(0, me, sibling, src=x_ref)]
        first += [copy(1 + j, me, (*chip, c), src=x_ref) for j, chip in enumerate(chips)]
        for cp in first:
            cp.start()
        passed = [copy(4 + j, (*chip, c), sibling) for j, chip in enumerate(chips)]
        for j, chip in enumerate(chips):
            copy(1 + j, (*chip, c), me).wait_recv()
            passed[j].start()
        copy(0, sibling, me).wait_recv()
        for j, chip in enumerate(chips):
            copy(4 + j, (*chip, 1 - c), me).wait_recv()
        for cp in first + passed:
            cp.wait_send()
        mine.wait()

    return pl.pallas_call(
        body, name=name, out_shape=jax.ShapeDtypeStruct((8,) + blk.shape, blk.dtype),
        in_specs=[pl.BlockSpec(memory_space=space)], out_specs=pl.BlockSpec(memory_space=space),
        scratch_shapes=[pltpu.SemaphoreType.DMA((7,)), pltpu.SemaphoreType.DMA((7,)), pltpu.SemaphoreType.DMA],
        compiler_params=pltpu.CompilerParams(vmem_limit_bytes=VMEM_LIMIT_V7X))(blk)


def _peers(place):
    x, y, c = place
    return [((1 - x) if k & 4 else x, (1 - y) if k & 2 else y, (1 - c) if k & 1 else c) for k in range(1, 8)]


def _slot(p):
    return 4 * p[0] + 2 * p[1] + p[2]


def plan_gather(place, srcs, lands):
    remote = [(s, l.at[_slot(place)], to) for s, l in zip(srcs, lands) for to in _peers(place)]
    return remote, [(s, l.at[_slot(place)]) for s, l in zip(srcs, lands)]


def plan_to_owner(place, srcs, lands):
    remote = [(s.at[2 * to[0] + to[1], to[2]], l.at[_slot(place)], to) for s, l in zip(srcs, lands) for to in _peers(place)]
    x, y, c = place
    return remote, [(s.at[2 * x + y, c], l.at[_slot(place)]) for s, l in zip(srcs, lands)]


def sequencer_exchange(name, collective_id, srcs, land_shapes, plan):
    n = len(srcs)
    src_refs = [jax.new_ref(a, memory_space=pltpu.MemorySpace.HBM) for a in srcs]
    land_refs = [jax.empty_ref(s, memory_space=pltpu.MemorySpace.HBM) for s in land_shapes]

    @pl.kernel(mesh=plsc.ScalarSubcoreMesh(axis_name="sequencer", num_cores=1), name=name,
               scratch_types=(pltpu.SemaphoreType.DMA((7 * n,)), pltpu.SemaphoreType.DMA((7 * n,)),
                              pltpu.SemaphoreType.DMA((n,))),
               compiler_params=pltpu.CompilerParams(collective_id=collective_id))
    def launch(send_sems, recv_sems, local_sems):
        place = _place()
        barrier = pltpu.get_barrier_semaphore()
        for to in _peers(place):
            pl.semaphore_signal(barrier, inc=1, device_id=to, device_id_type=MESH)
        pl.semaphore_wait(barrier, 7)
        remote, local = plan(place, src_refs, land_refs)
        mine = [pltpu.make_async_copy(s, d, local_sems.at[a]) for a, (s, d) in enumerate(local)]
        for cp in mine:
            cp.start()
        cps = [pltpu.make_async_remote_copy(src_ref=s, dst_ref=d, send_sem=send_sems.at[k], recv_sem=recv_sems.at[k],
                                            device_id=to, device_id_type=MESH) for k, (s, d, to) in enumerate(remote)]
        for cp in cps:
            cp.start()
        for cp in mine:
            cp.wait()
        for cp in cps:
            cp.wait()

    launch()
    return land_refs


def sibling_pair(name, hs):
    n = len(hs)

    def body(*refs):
        ins, outs = refs[:n], refs[n:2 * n]
        send_sems, recv_sems = refs[2 * n:]
        x, y, c = _place()
        cps = [pltpu.make_async_remote_copy(src_ref=outs[a].at[c], dst_ref=outs[a].at[c], send_sem=send_sems.at[a],
                                            recv_sem=recv_sems.at[a], device_id=(x, y, 1 - c), device_id_type=MESH)
               for a in range(n)]
        for cp in cps:
            cp.start()
        for a in range(n):
            pltpu.make_async_remote_copy(src_ref=outs[a].at[1 - c], dst_ref=outs[a].at[1 - c], send_sem=send_sems.at[a],
                                         recv_sem=recv_sems.at[a], device_id=(x, y, 1 - c),
                                         device_id_type=MESH).wait_recv()
        for cp in cps:
            cp.wait_send()

    any_spec = pl.BlockSpec(memory_space=pl.ANY)
    return pl.pallas_call(
        body, name=name, out_shape=[jax.ShapeDtypeStruct(h.shape, h.dtype) for h in hs],
        in_specs=[any_spec] * n, out_specs=[any_spec] * n, input_output_aliases={a: a for a in range(n)},
        scratch_shapes=[pltpu.SemaphoreType.DMA((n,)), pltpu.SemaphoreType.DMA((n,))])(*hs)


def owner_sum(name, land):
    _, r, w = land.shape
    T = r // 2

    def body(_, l_ref, o_ref):
        acc = l_ref[0].astype(F32)
        for j in range(1, 8):
            acc = acc + l_ref[j].astype(F32)
        o_ref[...] = acc

    grid_spec = pltpu.PrefetchScalarGridSpec(
        num_scalar_prefetch=1, grid=(2,),
        in_specs=[pl.BlockSpec((8, T, w), lambda i, at: (0, i, 0))],
        out_specs=pl.BlockSpec((None, T, w), lambda i, at: (at[0], i, 0)))
    at = jnp.stack([lax.axis_index("c")]).astype(jnp.int32)
    return pl.pallas_call(body, name=name, grid_spec=grid_spec, out_shape=jax.ShapeDtypeStruct((2, r, w), F32),
                          compiler_params=_cp("arbitrary"))(at, land)


W_IN_RUNS = ((0, 2, 1296, 376), (376, 3, 0, 1672), (2048, 1, 920, 752), (2800, 2, 0, 1296), (4096, 0, 0, 1024),
             (5120, 0, 1024, 648), (5768, 1, 0, 920))


def w_in_to_padded(name, g4):
    T = 128

    def body(g_ref, o_ref):
        o_ref[:, D_PROJ:PW] = jnp.zeros((T, PW - D_PROJ), o_ref.dtype)
        for (a, s, j0, w) in W_IN_RUNS:
            o_ref[:, a:a + w] = g_ref[s, :, j0:j0 + w]

    return pl.pallas_call(body, name=name, grid=(D // T,), in_specs=[pl.BlockSpec((4, T, 1672), lambda i: (0, i, 0))],
                          out_specs=pl.BlockSpec((T, PW), lambda i: (i, 0)),
                          out_shape=jax.ShapeDtypeStruct((D, PW), g4.dtype), compiler_params=_cp("arbitrary"))(g4)


def w_in_from_padded(name, gp):
    T = 128

    def body(g_ref, o_ref):
        for (a, s, j0, w) in W_IN_RUNS:
            o_ref[s, :, j0:j0 + w] = g_ref[:, a:a + w]

    return pl.pallas_call(body, name=name, grid=(D // T,), in_specs=[pl.BlockSpec((T, PW), lambda i: (i, 0))],
                          out_specs=pl.BlockSpec((4, T, 1672), lambda i: (0, i, 0)),
                          out_shape=jax.ShapeDtypeStruct((4, D, 1672), gp.dtype), compiler_params=_cp("arbitrary"))(gp)


def sum_devices(name, g):
    def body(g_ref, o_ref):
        acc = g_ref[0]
        for k in range(1, 8):
            acc = acc + g_ref[k]
        o_ref[...] = acc

    return pl.pallas_call(body, name=name, out_shape=jax.ShapeDtypeStruct(g.shape[1:], F32),
                          compiler_params=pltpu.CompilerParams(vmem_limit_bytes=VMEM_LIMIT_V7X))(g)


def adamw(name, w, g, m, v, T):
    r, wd = w.shape
    c1 = 1.0 - ADAM_B1 ** ADAM_STEP
    c2 = 1.0 - ADAM_B2 ** ADAM_STEP

    def body(w_ref, g_ref, m_ref, v_ref, d_ref, mo_ref, vo_ref):
        gv = g_ref[...]
        mn = ADAM_B1 * m_ref[...] + (1.0 - ADAM_B1) * gv
        vn = ADAM_B2 * v_ref[...] + (1.0 - ADAM_B2) * (gv * gv)
        d_ref[...] = -ADAM_LR * ((mn / c1) / (jnp.sqrt(vn / c2) + ADAM_EPS) + ADAM_WD * w_ref[...])
        mo_ref[...] = mn
        vo_ref[...] = vn

    spec = pl.BlockSpec((T, wd), lambda i: (i, 0))
    return pl.pallas_call(body, name=name, grid=(r // T,), in_specs=[spec] * 4, out_specs=[spec] * 3,
                          out_shape=[jax.ShapeDtypeStruct((r, wd), F32)] * 3, compiler_params=_cp("arbitrary"))(w, g, m, v)


BIG = {'w_in': (1024, 1672), 'w_ssd_proj': (256, 1024), 'w_gm_proj': (256, 1024), 'w_out': (256, 1024),
       'w_ff1': (1024, 704), 'w_ff3': (1024, 704), 'w_ff2': (704, 1024)}


class Flat:
    def __init__(self, segs):
        self.off, o = {}, 0
        for name, size in segs:
            self.off[name] = (o, size)
            o += -(-size // 128) * 128
        self.rows = -(-o // 1024) * 8

    def pack(self, vals):
        parts = []
        for name, (o, size) in self.off.items():
            v = vals[name].reshape(-1).astype(F32)
            parts.append(jnp.pad(v, (0, -(-size // 128) * 128 - size)))
        buf = jnp.concatenate(parts)
        return jnp.pad(buf, (0, self.rows * 128 - buf.shape[0])).reshape(self.rows, 128)

    def get(self, buf, name, shape=None):
        o, size = self.off[name]
        v = buf[o // 128:(o + size + 127) // 128].reshape(-1)[:size]
        return v if shape is None else v.reshape(shape)


PARTIALS = Flat([('loss', 1), ('ln0_g', D), ('ln0_b', D), ('dmod_x', 6 * D), ('dmod_c', 6 * D), ('conv_w', 5 * 1536),
                 ('conv_b', 1536), ('dt_bias', 32), ('a_log', 32), ('d_skip', 32), ('ssd_norm_g', D),
                 ('gm_norm_g', D), ('gm_norm_b', D), ('w_spatial', 8 * Q * Q), ('b_spatial', 8 * Q), ('b_gate', 2 * D),
                 ('ln1_g', D), ('ln1_b', D), ('ln2_g', D), ('ln2_b', D)])

WEIGHTS = ('c_ctx', 'ln0_g', 'ln0_b', 'w_ada', 'b_ada', 'w_in', 'conv_w', 'conv_b', 'dt_bias', 'a_log', 'd_skip',
           'ssd_norm_g', 'gm_norm_g', 'gm_norm_b', 'w_spatial', 'b_spatial', 'b_gate', 'w_ssd_proj', 'w_gm_proj',
           'w_out', 'ln1_g', 'ln1_b', 'w_ff1', 'w_ff3', 'w_ff2', 'ln2_g', 'ln2_b')
BIG_NAMES = tuple(BIG)
SMALL_NAMES = tuple(n for n in WEIGHTS if n not in BIG_NAMES and n != 'w_ada')


def kernel(x, c, ctx, c_ctx, ln0_g, ln0_b, w_ada, b_ada, w_in, conv_w, conv_b, dt_bias, a_log, d_skip, ssd_norm_g, gm_norm_g, gm_norm_b, w_spatial, b_spatial, b_gate, w_ssd_proj, w_gm_proj, w_out, ln1_g, ln1_b, w_ff1, w_ff3, w_ff2, ln2_g, ln2_b, loss_target, m_c_ctx, m_ln0_g, m_ln0_b, m_w_ada, m_b_ada, m_w_in, m_conv_w, m_conv_b, m_dt_bias, m_a_log, m_d_skip, m_ssd_norm_g, m_gm_norm_g, m_gm_norm_b, m_w_spatial, m_b_spatial, m_b_gate, m_w_ssd_proj, m_w_gm_proj, m_w_out, m_ln1_g, m_ln1_b, m_w_ff1, m_w_ff3, m_w_ff2, m_ln2_g, m_ln2_b, v_c_ctx, v_ln0_g, v_ln0_b, v_w_ada, v_b_ada, v_w_in, v_conv_w, v_conv_b, v_dt_bias, v_a_log, v_d_skip, v_ssd_norm_g, v_gm_norm_g, v_gm_norm_b, v_w_spatial, v_b_spatial, v_b_gate, v_w_ssd_proj, v_w_gm_proj, v_w_out, v_ln1_g, v_ln1_b, v_w_ff1, v_w_ff3, v_w_ff2, v_ln2_g, v_ln2_b):
    wts = dict(c_ctx=c_ctx, ln0_g=ln0_g, ln0_b=ln0_b, w_ada=w_ada, b_ada=b_ada, w_in=w_in, conv_w=conv_w, conv_b=conv_b,
               dt_bias=dt_bias, a_log=a_log, d_skip=d_skip, ssd_norm_g=ssd_norm_g, gm_norm_g=gm_norm_g,
               gm_norm_b=gm_norm_b, w_spatial=w_spatial, b_spatial=b_spatial, b_gate=b_gate, w_ssd_proj=w_ssd_proj,
               w_gm_proj=w_gm_proj, w_out=w_out, ln1_g=ln1_g, ln1_b=ln1_b, w_ff1=w_ff1, w_ff3=w_ff3, w_ff2=w_ff2,
               ln2_g=ln2_g, ln2_b=ln2_b)
    ms = dict(zip(WEIGHTS, (m_c_ctx, m_ln0_g, m_ln0_b, m_w_ada, m_b_ada, m_w_in, m_conv_w, m_conv_b, m_dt_bias, m_a_log,
                            m_d_skip, m_ssd_norm_g, m_gm_norm_g, m_gm_norm_b, m_w_spatial, m_b_spatial, m_b_gate,
                            m_w_ssd_proj, m_w_gm_proj, m_w_out, m_ln1_g, m_ln1_b, m_w_ff1, m_w_ff3, m_w_ff2, m_ln2_g,
                            m_ln2_b)))
    vs = dict(zip(WEIGHTS, (v_c_ctx, v_ln0_g, v_ln0_b, v_w_ada, v_b_ada, v_w_in, v_conv_w, v_conv_b, v_dt_bias, v_a_log,
                            v_d_skip, v_ssd_norm_g, v_gm_norm_g, v_gm_norm_b, v_w_spatial, v_b_spatial, v_b_gate,
                            v_w_ssd_proj, v_w_gm_proj, v_w_out, v_ln1_g, v_ln1_b, v_w_ff1, v_w_ff3, v_w_ff2, v_ln2_g,
                            v_ln2_b)))
    px, py, pc = _place()
    shard = 2 * px + py
    dev = 2 * shard + pc
    take = lambda a, i, axis=0: lax.dynamic_index_in_dim(a, i, axis, keepdims=False)

    half = lambda n: take(wts[n][0].reshape(2, BIG[n][0] // 2, BIG[n][1]), pc).astype(BF16)

    pre = jnp.concatenate([c, jnp.pad(conv_w[0], ((0, 0), (0, D - 384))), jnp.zeros((2, D), F32)], axis=0)
    pre = allgather8("gather_cond", pre, False)
    conv_w_full = pre[0::2, 1:6, :384].transpose(1, 0, 2).reshape(5, 1536)
    a16 = jnp.concatenate([_silu(pre[:, 0, :]), _silu(c_ctx)[None], jnp.zeros((7, D), F32)], axis=0)
    mod = matmul("ada_fwd", a16, w_ada[0], 'nn', 16, 512, 1024)
    mod = mod + lax.dynamic_slice_in_dim(b_ada[0], shard * 1536, 1536)[None]
    mod = allgather8("gather_mod", mod, False)
    mod = jnp.concatenate([mod[0], mod[2], mod[4], mod[6]], axis=1)
    mod_x = take(mod, dev).reshape(6, D)
    mod_c = mod[8].reshape(6, D)

    def full(n, blocks):
        r, w = BIG[n]
        return blocks.reshape(4, r, w) if w != D else blocks.reshape(4 * r, w)

    class Exchanges:
        rest_names = BIG_NAMES[1:]

        def __init__(self):
            self.pending = []

        def w_in(self):
            blocks = allgather8("gather_w_in", half('w_in'), True)
            w = w_in_to_padded("w_in_layout", full('w_in', blocks))
            halves = [half(n) for n in self.rest_names]
            halves[0], _ = lax.optimization_barrier((halves[0], blocks))
            lands = [jax.ShapeDtypeStruct((8,) + h.shape, BF16) for h in halves]
            self.rest_refs = sequencer_exchange("gather_rest", 1, halves, lands, plan_gather)
            return w

        def rest(self):
            return {n: full(n, r[...]) for n, r in zip(self.rest_names, self.rest_refs)}

        def grads(self, group, gs):
            if group == 'in':
                gs = {'w_in': w_in_from_padded("w_in_grad_layout", gs['w_in'])}
            blocks = [g.reshape(4, 2, BIG[n][0] // 2, BIG[n][1]) for n, g in gs.items()]
            lands = [jax.ShapeDtypeStruct((8,) + b.shape[2:], BF16) for b in blocks]
            refs = sequencer_exchange("grads_" + group, 2 + len(self.pending), blocks, lands, plan_to_owner)
            self.pending.append((tuple(gs), refs))

        def finish(self):
            names, halves = [], []
            for ns, refs in self.pending:
                names += ns
                halves += [owner_sum("grads_sum_" + n, r[...]) for n, r in zip(ns, refs)]
            return {n: h.reshape(BIG[n]) for n, h in zip(names, sibling_pair("grads_halves", halves))}

    S = dict(ln0_g=ln0_g, ln0_b=ln0_b, conv_w=conv_w_full, conv_b=conv_b[0], dt_bias=dt_bias[0], a_log=a_log[0],
             d_skip=d_skip[0], ssd_norm_g=ssd_norm_g[0], gm_norm_g=gm_norm_g[0], gm_norm_b=gm_norm_b[0],
             w_spatial=w_spatial[0], b_spatial=b_spatial[0], b_gate=b_gate[0], ln1_g=ln1_g[0], ln1_b=ln1_b[0],
             ln2_g=ln2_g[0], ln2_b=ln2_b[0])
    x_all = jnp.concatenate([ctx[0], x[0]], axis=0)
    exchanges = Exchanges()
    grad_x, gsmall = core(x_all, loss_target[0], mod_x, mod_c, exchanges, S)

    parts = allgather8("gather_partials", PARTIALS.pack(gsmall), False)
    tot = sum_devices("partials_sum", parts)
    g_shards = exchanges.finish()
    g = {n: PARTIALS.get(tot, n) for n in ('ln0_g', 'ln0_b', 'conv_b', 'dt_bias', 'a_log', 'd_skip', 'ssd_norm_g',
                                           'gm_norm_g', 'gm_norm_b', 'w_spatial', 'b_spatial', 'b_gate', 'ln1_g',
                                           'ln1_b', 'ln2_g', 'ln2_b')}
    loss = PARTIALS.get(tot, 'loss', ())
    dmod_c = PARTIALS.get(tot, 'dmod_c')
    g['b_ada'] = PARTIALS.get(tot, 'dmod_x') + dmod_c
    g['conv_w'] = lax.dynamic_slice_in_dim(PARTIALS.get(tot, 'conv_w', (5, 1536)), shard * 384, 384, axis=1)
    o, size = PARTIALS.off['dmod_x']
    dmod_rows = parts[:, o // 128:(o + size) // 128].reshape(8, size)
    dm = jnp.concatenate([dmod_rows, dmod_c[None], jnp.zeros((7, 6 * D), F32)], axis=0)
    dm = lax.dynamic_slice_in_dim(dm, shard * 1536, 1536, axis=1)
    g['w_ada'] = matmul("ada_bwd_w", a16, dm, 'tn', 1024, 512, 16)
    dm_c = jnp.concatenate([dm[8:9], jnp.zeros((15, 1536), F32)], axis=0)
    dc = matmul("ada_bwd_c", dm_c, w_ada[0], 'nt', 16, 1024, 512)
    dc = allgather8("gather_dcctx", dc, False)[:, 0, :]
    dc = ((dc[0] + dc[2]) + dc[4]) + dc[6]
    sg = jax.nn.sigmoid(c_ctx)
    g['c_ctx'] = dc * (sg * (1.0 + c_ctx * (1.0 - sg)))
    for n in BIG_NAMES:
        g[n] = g_shards[n]

    delta, new_m, new_v = {}, {}, {}
    for n in BIG_NAMES + ('w_ada',):
        w2 = wts[n][0]
        T = 352 if n == 'w_ff2' else 256
        d_, m_, v_ = adamw("adamw_" + n, w2, g[n], ms[n][0], vs[n][0], T)
        delta[n], new_m[n], new_v[n] = d_, m_, v_
    lay = Flat([(n, wts[n].size) for n in SMALL_NAMES])
    d_, m_, v_ = adamw("adamw_small", lay.pack(wts), lay.pack(g), lay.pack(ms), lay.pack(vs), lay.rows)
    for n in SMALL_NAMES:
        delta[n], new_m[n], new_v[n] = (lay.get(b, n) for b in (d_, m_, v_))

    shp = lambda d: [d[n].reshape(wts[n].shape) for n in WEIGHTS]
    return (loss, grad_x[None], *shp(g), *shp(delta), *shp(new_m), *shp(new_v))
```

```python
import functools

import jax
import jax.numpy as jnp
from jax import lax
from jax.experimental import pallas as pl
from jax.experimental.pallas import tpu as pltpu
from jax.experimental.pallas import tpu_sc as plsc

F32 = jnp.float32
BF16 = jnp.bfloat16
MESH = pl.DeviceIdType.MESH

VMEM_LIMIT_V7X = 56 * 1024 * 1024

D = 1024
LC = 256
Q = 128
NH = 16
D_FF = 2816
LN_EPS = 1e-5
ALPHA = 2.0 ** 0.25

PW = 7168
C_GATE, C_UV, C_Z, C_XBC, C_DT = 0, 2048, 4096, 5120, 6656
D_PROJ = 6688

ADAM_LR, ADAM_B1, ADAM_B2, ADAM_EPS, ADAM_WD, ADAM_STEP = 0.001, 0.9, 0.999, 1e-08, 0.01, 10


def _cp(*sem):
    return pltpu.CompilerParams(dimension_semantics=sem, vmem_limit_bytes=VMEM_LIMIT_V7X)


def _dot(a, b, ca, cb):
    return lax.dot_general(a.astype(BF16), b.astype(BF16), (((ca,), (cb,)), ((), ())),
                           preferred_element_type=F32)


@jax.custom_vjp
def mm(a, b):
    return _dot(a, b, 1, 0)


mm.defvjp(lambda a, b: (_dot(a, b, 1, 0), (a, b)),
          lambda r, g: (_dot(g, r[1], 1, 1), _dot(r[0], g, 0, 0)))


@jax.custom_vjp
def mm_nt(a, b):
    return _dot(a, b, 1, 1)


mm_nt.defvjp(lambda a, b: (_dot(a, b, 1, 1), (a, b)),
             lambda r, g: (_dot(g, r[1], 1, 0), _dot(g, r[0], 0, 0)))


@jax.custom_vjp
def mm_tn(a, b):
    return _dot(a, b, 0, 0)


mm_tn.defvjp(lambda a, b: (_dot(a, b, 0, 0), (a, b)),
             lambda r, g: (_dot(r[1], g, 1, 1), _dot(r[0], g, 1, 0)))


def _dot32(a, b):
    return lax.dot_general(a, b, (((1,), (0,)), ((), ())), precision=lax.Precision.HIGHEST,
                           preferred_element_type=F32)


def _cumsum_fn(rev):
    def tri(transpose):
        r = lax.broadcasted_iota(jnp.int32, (Q, Q), 0)
        c = lax.broadcasted_iota(jnp.int32, (Q, Q), 1)
        keep = (r >= c) if (rev == transpose) else (r <= c)
        return jnp.where(keep, 1.0, 0.0).astype(F32)

    @jax.custom_vjp
    def cums(a):
        return _dot32(tri(False), a)

    cums.defvjp(lambda a: (_dot32(tri(False), a), None), lambda _, g: (_dot32(tri(True), g),))
    return cums


def _cols(v, k):
    w = v.shape[1] // k
    return tuple(v[:, w * i:w * (i + 1)] for i in range(k))


def _splitter(k):
    @jax.custom_vjp
    def split(v):
        return _cols(v, k)

    @jax.custom_vjp
    def concat(ps):
        return jnp.concatenate(ps, axis=1)

    split.defvjp(lambda v: (_cols(v, k), None), lambda _, g: (jnp.concatenate(g, axis=1),))
    concat.defvjp(lambda ps: (jnp.concatenate(ps, axis=1), None), lambda _, g: (_cols(g, k),))
    return split, concat


split2, _ = _splitter(2)
split4, _ = _splitter(4)
split8, concat8 = _splitter(8)


def _ln(x, g, b):
    mu = jnp.mean(x, axis=-1, keepdims=True)
    xc = x - mu
    var = jnp.mean(xc * xc, axis=-1, keepdims=True)
    return xc * lax.rsqrt(var + LN_EPS) * g + b


def _silu(x):
    return x * jax.nn.sigmoid(x)


def _gelu(x):
    return 0.5 * x * (1.0 + jnp.tanh(0.7978845608028654 * (x + 0.044715 * (x * x * x))))


def _xspec(T, w, col, roff):
    return pl.BlockSpec((T, w), lambda i, col=col, roff=roff: (jnp.maximum(i + roff, 0), col))


def _pspec(p, sel):
    if sel is None:
        return pl.BlockSpec(p.shape, lambda i, n=p.ndim: (0,) * n)
    return pl.BlockSpec((1,) + p.shape[1:], lambda i, n=p.ndim: (sel(i),) + (0,) * (n - 1))


def _out_plumbing(outs, T, args, in_specs):
    shapes, specs, aliases = [], [], {}
    for k, o in enumerate(outs):
        if o[0] == 'new':
            _, rows, w, roff = o[:4]
            shapes.append(jax.ShapeDtypeStruct((rows, w), o[4] if len(o) > 4 else F32))
            specs.append(_xspec(T, w, 0, roff))
        elif o[0] == 'acc':
            shapes.append(jax.ShapeDtypeStruct(o[1], F32))
            specs.append(pl.BlockSpec(o[1], lambda i, n=len(o[1]): (0,) * n))
        elif o[0] == 'transposed':
            _, rows, w, dtype, _ = o
            shapes.append(jax.ShapeDtypeStruct((w, rows), dtype))
            specs.append(pl.BlockSpec((w, T), lambda i: (0, i)))
        elif o[0] == 'part':
            _, rows, wtot, w, col, roff, dtype = o
            shapes.append(jax.ShapeDtypeStruct((rows, wtot), dtype))
            specs.append(_xspec(T, w, col, roff))
        else:
            _, arr, w, col, roff = o
            aliases[len(args)] = k
            args.append(arr)
            in_specs.append(pl.BlockSpec(memory_space=pl.ANY))
            shapes.append(jax.ShapeDtypeStruct(arr.shape, arr.dtype))
            specs.append(_xspec(T, w, col, roff))
    return shapes, specs, aliases


def stage_fwd(name, f, T, n, xs, ps, outs):
    nx, npar = len(xs), len(ps)
    args = [x[0] for x in xs] + [p[0] for p in ps]
    in_specs = [_xspec(T, w, col, roff) for (_, w, col, roff) in xs] + [_pspec(p, sel) for (p, sel) in ps]
    n_in = len(args)
    shapes, specs, aliases = _out_plumbing(outs, T, args, in_specs)
    n_all_in = len(args)

    def body(*refs):
        i = pl.program_id(0)
        xv = [r[...] for r in refs[:nx]]
        pv = [r[...] if ps[k][1] is None else r[0] for k, r in enumerate(refs[nx:n_in])]
        res = f(*xv, *pv)
        for k, o_ref in enumerate(refs[n_all_in:]):
            if outs[k][0] == 'acc':
                @pl.when(i == 0)
                def _(o_ref=o_ref, v=res[k]):
                    o_ref[...] = v

                @pl.when(i > 0)
                def _(o_ref=o_ref, v=res[k]):
                    o_ref[...] += v
            elif outs[k][0] == 'transposed':
                o_ref[...] = res[outs[k][4]].T.astype(o_ref.dtype)
            else:
                o_ref[...] = res[k].astype(o_ref.dtype)

    return pl.pallas_call(body, name=name, grid=(n,), in_specs=in_specs, out_specs=specs, out_shape=shapes,
                          input_output_aliases=aliases, compiler_params=_cp("arbitrary"))(*args)


def stage_bwd(name, f, T, n, xs, ps, cts, dxs, dps, primal=()):
    nx, npar = len(xs), len(ps)
    args = [x[0] for x in xs] + [p[0] for p in ps]
    in_specs = [_xspec(T, w, col, roff) for (_, w, col, roff) in xs] + [_pspec(p, sel) for (p, sel) in ps]
    ct_arrs = [c for c in cts if isinstance(c, tuple)]
    for (a, w, col, roff) in ct_arrs:
        args.append(a)
        in_specs.append(_xspec(T, w, col, roff))
    n_in = len(args)
    outs, out_of = [], []
    for k, o in enumerate(dxs):
        if o is not None:
            outs.append(o)
            out_of.append(('x', k))
    for k, want in enumerate(dps):
        if want:
            p, sel = ps[k]
            outs.append(('acc', p.shape))
            out_of.append(('p', k))
    for k, shape in primal:
        outs.append(('acc', shape))
        out_of.append(('r', k))
    shapes, specs, aliases = _out_plumbing(outs, T, args, in_specs)
    for j, (kind, k) in enumerate(out_of):
        if kind == 'p' and ps[k][1] is not None:
            p, sel = ps[k]
            specs[j] = pl.BlockSpec((1,) + p.shape[1:], lambda i, n=p.ndim, sel=sel: (sel(i),) + (0,) * (n - 1))
    n_all_in = len(args)

    def body(*refs):
        i = pl.program_id(0)
        xv = [r[...] for r in refs[:nx]]
        pv = [r[...] if ps[k][1] is None else r[0] for k, r in enumerate(refs[nx:nx + npar])]
        res, vjp_fn = jax.vjp(f, *xv, *pv)
        ctv, q = [], nx + npar
        for k, c in enumerate(cts):
            if c is None:
                ctv.append(jnp.zeros_like(res[k]))
            elif isinstance(c, tuple):
                v = refs[q][...]
                if c[3] < 0:
                    v = v * (i + c[3] >= 0).astype(F32)
                ctv.append(v)
                q += 1
            else:
                ctv.append(jnp.full_like(res[k], c))
        grads = vjp_fn(tuple(ctv))
        for j, o_ref in enumerate(refs[n_all_in:]):
            kind, k = out_of[j]
            if kind == 'x':
                o_ref[...] = grads[k].astype(o_ref.dtype)
            else:
                g = res[k] if kind == 'r' else grads[nx + k]
                sel = None if kind == 'r' else ps[k][1]
                if sel is None:
                    first = i == 0
                    tgt = o_ref
                else:
                    first = jnp.logical_or(i == 0, sel(i) != sel(jnp.maximum(i - 1, 0)))
                    tgt = o_ref.at[0]

                @pl.when(first)
                def _(tgt=tgt, g=g):
                    tgt[...] = g

                @pl.when(jnp.logical_not(first))
                def _(tgt=tgt, g=g):
                    tgt[...] += g

    return pl.pallas_call(body, name=name, grid=(n,), in_specs=in_specs, out_specs=specs, out_shape=shapes,
                          input_output_aliases=aliases, compiler_params=_cp("arbitrary"))(*args)


_CONTRACT = {'nn': (1, 0), 'nt': (1, 1), 'tn': (0, 0)}


def matmul(name, a, b, mode, tm, tn, tk, out_dtype=F32, add=None):
    if mode == 'nn':
        (M, K), (_, N) = a.shape, b.shape
    elif mode == 'nt':
        (M, K), (N, _) = a.shape, b.shape
    else:
        (K, M), (_, N) = a.shape, b.shape
    assert M % tm == 0 and N % tn == 0 and K % tk == 0, (name, M, N, K, tm, tn, tk)
    a_spec = (pl.BlockSpec((tk, tm), lambda j, i, k: (k, i)) if mode == 'tn'
              else pl.BlockSpec((tm, tk), lambda j, i, k: (i, k)))
    b_spec = (pl.BlockSpec((tn, tk), lambda j, i, k: (j, k)) if mode == 'nt'
              else pl.BlockSpec((tk, tn), lambda j, i, k: (k, j)))
    o_spec = pl.BlockSpec((tm, tn), lambda j, i, k: (i, j))
    return matmul_call(name, (N // tn, M // tm, K // tk), a, a_spec, b, b_spec, (M, N), o_spec, (tm, tn), mode,
                       out_dtype, add)


def matmul_call(name, grid, a, a_spec, b, b_spec, out_shape, o_spec, tile, mode, out_dtype=F32, add=None):
    tm, tn = tile
    nk = grid[2]
    ca, cb = _CONTRACT[mode]
    args, in_specs = [a, b], [a_spec, b_spec]
    if add is not None:
        args.append(add)
        in_specs.append(o_spec)

    def body(*refs):
        a_ref, b_ref = refs[0], refs[1]
        o_ref, acc = refs[-2], refs[-1]
        k = pl.program_id(2)
        if nk == 1:
            p = _dot(a_ref[...], b_ref[...], ca, cb)
            o_ref[...] = (p + refs[2][...] if add is not None else p).astype(out_dtype)
            return

        @pl.when(k == 0)
        def _():
            acc[...] = refs[2][...] if add is not None else jnp.zeros_like(acc)

        acc[...] += _dot(a_ref[...], b_ref[...], ca, cb)

        @pl.when(k == nk - 1)
        def _():
            o_ref[...] = acc[...].astype(out_dtype)

    return pl.pallas_call(body, name=name, grid=grid, in_specs=in_specs, out_specs=o_spec,
                          out_shape=jax.ShapeDtypeStruct(out_shape, out_dtype),
                          scratch_shapes=[pltpu.VMEM((tm, tn) if nk > 1 else (8, 128), F32)],
                          compiler_params=_cp("arbitrary", "arbitrary", "arbitrary"))(*args)


NS, WS = 4, 704


def _resident(name, M, tm, rows, weight, out_shape, out_block, out_map, step, add=None):
    args = [rows[0], weight] + ([] if add is None else [add])
    in_specs = [pl.BlockSpec(rows[1], rows[2]), pl.BlockSpec(weight.shape, lambda i, n=weight.ndim: (0,) * n)]
    if add is not None:
        in_specs.append(pl.BlockSpec(out_block, out_map))
    return pl.pallas_call(step, name=name, grid=(M // tm,), in_specs=in_specs, out_specs=pl.BlockSpec(out_block, out_map),
                          out_shape=jax.ShapeDtypeStruct(out_shape, F32), compiler_params=_cp("arbitrary"))(*args)


def ffn_in_fwd(name, h, w1, w3, tm):
    M = h.shape[0]

    def step(h_ref, w1_ref, w3_ref, a1_ref, a3_ref, act_ref):
        for s in range(NS):
            a1 = _dot(h_ref[...], w1_ref[s], 1, 0)
            a3 = _dot(h_ref[...], w3_ref[s], 1, 0)
            a1_ref[s] = a1.astype(a1_ref.dtype)
            a3_ref[s] = a3.astype(a3_ref.dtype)
            act_ref[s] = (_silu(a1) * a3).astype(act_ref.dtype)

    wspec = pl.BlockSpec((NS, D, WS), lambda i: (0, 0, 0))
    ospec = pl.BlockSpec((NS, tm, WS), lambda i: (0, i, 0))
    return pl.pallas_call(
        step, name=name, grid=(M // tm,), in_specs=[pl.BlockSpec((tm, D), lambda i: (i, 0)), wspec, wspec],
        out_specs=[ospec, ospec, ospec],
        out_shape=[jax.ShapeDtypeStruct((NS, M, WS), BF16)] * 3, compiler_params=_cp("arbitrary"))(h, w1, w3)


def ffn_out_bwd_x(name, dff, w2, a1, a3, tm):
    M = dff.shape[0]

    def step(d_ref, w_ref, a1_ref, a3_ref, da1_ref, da3_ref):
        for s in range(NS):
            dact = _dot(d_ref[...], w_ref[s * WS:(s + 1) * WS, :], 1, 1)
            a1 = a1_ref[s].astype(F32)
            sig = jax.nn.sigmoid(a1)
            da3_ref[s] = (dact * (a1 * sig)).astype(da3_ref.dtype)
            da1_ref[s] = (dact * a3_ref[s].astype(F32) * (sig * (1.0 + a1 * (1.0 - sig)))).astype(da1_ref.dtype)

    aspec = pl.BlockSpec((NS, tm, WS), lambda i: (0, i, 0))
    return pl.pallas_call(
        step, name=name, grid=(M // tm,),
        in_specs=[pl.BlockSpec((tm, D), lambda i: (i, 0)), pl.BlockSpec(w2.shape, lambda i: (0, 0)), aspec, aspec],
        out_specs=[aspec, aspec],
        out_shape=[jax.ShapeDtypeStruct((NS, M, WS), BF16)] * 2, compiler_params=_cp("arbitrary"))(dff, w2, a1, a3)


def ff_in_bwd_x(name, da3, w3, tm, add=None):
    M = da3.shape[1]

    def step(*refs):
        d_ref, w_ref, o_ref = refs[0], refs[1], refs[-1]
        acc = _dot(d_ref[0], w_ref[0], 1, 1)
        for s in range(1, NS):
            acc = acc + _dot(d_ref[s], w_ref[s], 1, 1)
        o_ref[...] = acc if add is None else acc + refs[2][...]

    return _resident(name, M, tm, (da3, (NS, tm, WS), lambda i: (0, i, 0)), w3, (M, D), (tm, D), lambda i: (i, 0), step, add)


def ff_in_bwd_w(name, hT, da3, tk):
    M = hT.shape[1]
    return matmul_call(name, (NS, 1, M // tk), hT, pl.BlockSpec((D, tk), lambda j, i, k: (0, k)),
                       da3, pl.BlockSpec((None, tk, WS), lambda j, i, k: (j, k, 0)),
                       (NS, D, WS), pl.BlockSpec((None, D, WS), lambda j, i, k: (j, 0, 0)), (D, WS), 'nn', BF16)


def ff_out_fwd(name, act3, w2, tm):
    M = act3.shape[1]

    def step(a_ref, w_ref, o_ref):
        acc = _dot(a_ref[0], w_ref[0:WS, :], 1, 0)
        for s in range(1, NS):
            acc = acc + _dot(a_ref[s], w_ref[s * WS:(s + 1) * WS, :], 1, 0)
        o_ref[...] = acc

    return _resident(name, M, tm, (act3, (NS, tm, WS), lambda i: (0, i, 0)), w2, (M, D), (tm, D), lambda i: (i, 0), step)


def ff_out_bwd_w(name, act3, dff, tk):
    M = dff.shape[0]
    return matmul_call(name, (1, NS, M // tk), act3, pl.BlockSpec((None, tk, WS), lambda j, i, k: (i, k, 0)),
                       dff, pl.BlockSpec((tk, D), lambda j, i, k: (k, 0)),
                       (NS * WS, D), pl.BlockSpec((WS, D), lambda j, i, k: (i, 0)), (WS, D), 'tn', BF16)


def _shift_rows(x, d):
    n = x.shape[0]
    if d == 0:
        return x
    y = pltpu.roll(x, (-d) % n, 0)
    t = lax.broadcasted_iota(jnp.int32, x.shape, 0)
    ok = (t + d >= 0) & (t + d < n)
    return jnp.where(ok, y, 0.0)


def _conv_pre(x, w_ref, b_ref):
    acc = jnp.broadcast_to(b_ref[...], x.shape)
    for k in range(5):
        acc = acc + _shift_rows(x, k - 2) * w_ref[k:k + 1, :]
    return acc


def conv_fwd(name, proj, conv_w, conv_b, R):
    segs = ((0, LC), (LC, R))

    def body(x_ref, w_ref, b_ref, o_ref):
        for (s, e) in segs:
            pre = _conv_pre(x_ref[s:e, :], w_ref, b_ref)
            o_ref[s:e, :] = _silu(pre)

    return pl.pallas_call(
        body, name=name, grid=(12,),
        in_specs=[pl.BlockSpec((R, 128), lambda j: (0, C_XBC // 128 + j)),
                  pl.BlockSpec((8, 128), lambda j: (0, j)), pl.BlockSpec((1, 128), lambda j: (0, j))],
        out_specs=pl.BlockSpec((R, 128), lambda j: (0, j)),
        out_shape=jax.ShapeDtypeStruct((R, 1536), F32), compiler_params=_cp("arbitrary"))(proj, conv_w, conv_b)


def conv_bwd(name, proj, conv_w, conv_b, d_f, d_b, d_skip, dproj, R):
    segs = ((0, LC), (LC, R))

    def body(x_ref, w_ref, b_ref, df_ref, db_ref, ds_ref, _, dx_ref, dw_ref, dbias_ref):
        j = pl.program_id(0)
        has_skip = (j < 8).astype(F32)
        dw = [jnp.zeros((1, 128), F32) for _ in range(5)]
        dbias = jnp.zeros((1, 128), F32)
        for (s, e) in segs:
            x = x_ref[s:e, :]
            pre = _conv_pre(x, w_ref, b_ref)
            sig = jax.nn.sigmoid(pre)
            dy = df_ref[s:e, :] + db_ref[s:e, :]
            if s == LC:
                dy = dy + ds_ref[...] * has_skip
            dpre = dy * (sig * (1.0 + pre * (1.0 - sig)))
            dx = jnp.zeros_like(x)
            for k in range(5):
                dx = dx + _shift_rows(dpre, 2 - k) * w_ref[k:k + 1, :]
                dw[k] = dw[k] + jnp.sum(dpre * _shift_rows(x, k - 2), axis=0, keepdims=True)
            dbias = dbias + jnp.sum(dpre, axis=0, keepdims=True)
            dx_ref[s:e, :] = dx.astype(dx_ref.dtype)
        dw_ref[...] = jnp.zeros_like(dw_ref)
        for k in range(5):
            dw_ref[k:k + 1, :] = dw[k]
        dbias_ref[...] = dbias

    return pl.pallas_call(
        body, name=name, grid=(12,),
        in_specs=[pl.BlockSpec((R, 128), lambda j: (0, C_XBC // 128 + j)),
                  pl.BlockSpec((8, 128), lambda j: (0, j)), pl.BlockSpec((1, 128), lambda j: (0, j)),
                  pl.BlockSpec((R, 128), lambda j: (0, j)), pl.BlockSpec((R, 128), lambda j: (0, j)),
                  pl.BlockSpec((R - LC, 128), lambda j: (0, jnp.minimum(j, 7))),
                  pl.BlockSpec(memory_space=pl.ANY)],
        out_specs=[pl.BlockSpec((R, 128), lambda j: (0, C_XBC // 128 + j)),
                   pl.BlockSpec((8, 128), lambda j: (0, j)), pl.BlockSpec((1, 128), lambda j: (0, j))],
        out_shape=[jax.ShapeDtypeStruct(dproj.shape, dproj.dtype), jax.ShapeDtypeStruct((8, 1536), F32),
                   jax.ShapeDtypeStruct((1, 1536), F32)],
        input_output_aliases={6: 0}, compiler_params=_cp("arbitrary"))(proj, conv_w, conv_b, d_f, d_b, d_skip, dproj)


def _ssd_chunk(rev, dirn):
    cums = _cumsum_fn(rev)

    def f(xs, Bs, Cs, dt, alog, Hs):
        lane = lax.broadcasted_iota(jnp.int32, (1, 128), 1)
        sub = lax.broadcasted_iota(jnp.int32, (Q, 1), 0)
        r = lax.broadcasted_iota(jnp.int32, (Q, Q), 0)
        c = lax.broadcasted_iota(jnp.int32, (Q, Q), 1)
        mask = (r <= c) if rev else (r >= c)
        left = lane < 64
        a = dt * (-jnp.exp(alog))
        s = cums(a)
        sT, dtT = s.T, dt.T
        last_row = (sub == (0 if rev else Q - 1)).astype(F32)
        s_last = jnp.sum(s * last_row, axis=0, keepdims=True)
        G = [mm_nt(Cs[g], Bs[g]) for g in range(2)]
        M, es, wc, ed = [], [], [], []
        for h in range(NH):
            l = 16 * dirn + h
            oh_l = (lane == l).astype(F32)
            oh_s = (sub == l).astype(F32)
            s_col = jnp.sum(s * oh_l, axis=1, keepdims=True)
            dt_col = jnp.sum(dt * oh_l, axis=1, keepdims=True)
            s_row = jnp.sum(sT * oh_s, axis=0, keepdims=True)
            dt_row = jnp.sum(dtT * oh_s, axis=0, keepdims=True)
            sl = jnp.sum(s_last * oh_l, axis=1, keepdims=True)
            seg = jnp.where(mask, s_col - s_row, 0.0)
            lm = jnp.where(mask, jnp.exp(seg), 0.0)
            M.append(G[h // 8] * lm * dt_row)
            es.append(jnp.exp(s_col))
            wc.append(jnp.exp(sl - s_col) * dt_col)
            ed.append(jnp.exp(sl))
        Ys, Hn = [], []
        for j in range(8):
            g = j // 4
            xa = jnp.where(left, xs[j], 0.0)
            xb = jnp.where(left, 0.0, xs[j])
            yd = mm(M[2 * j], xa) + mm(M[2 * j + 1], xb)
            yo = mm(Cs[g], Hs[j]) * jnp.where(left, es[2 * j], es[2 * j + 1])
            Ys.append(yd + yo)
            st = mm_tn(Bs[g], xs[j] * jnp.where(left, wc[2 * j], wc[2 * j + 1]))
            Hn.append(Hs[j] * jnp.where(left, ed[2 * j], ed[2 * j + 1]) + st)
        return Ys, Hn

    return f


def _chunk_of(t, n, rev):
    if not rev:
        return t
    return jnp.where(t < 2, 1 - t, n + 1 - t)


def ssd_fwd(name, xbc, dt, alog, n, rev, dirn):
    chunk = _ssd_chunk(rev, dirn)

    def body(x_ref, b_ref, c_ref, dt_ref, al_ref, y_ref, hs_ref, h_scr):
        @pl.when(pl.program_id(0) == 0)
        def _():
            h_scr[...] = jnp.zeros_like(h_scr)

        xs = [x_ref[:, 128 * j:128 * (j + 1)] for j in range(8)]
        Bs = [b_ref[:, 128 * g:128 * (g + 1)] for g in range(2)]
        Cs = [c_ref[:, 128 * g:128 * (g + 1)] for g in range(2)]
        Hs = [h_scr[:, 128 * j:128 * (j + 1)] for j in range(8)]
        hs_ref[0] = h_scr[...]
        Ys, Hn = chunk(xs, Bs, Cs, dt_ref[...], al_ref[...], Hs)
        for j in range(8):
            y_ref[:, 128 * j:128 * (j + 1)] = Ys[j]
            h_scr[:, 128 * j:128 * (j + 1)] = Hn[j]

    cm = lambda t: _chunk_of(t, n, rev)
    return pl.pallas_call(
        body, name=name, grid=(n,),
        in_specs=[pl.BlockSpec((Q, 1024), lambda t: (cm(t), 0)), pl.BlockSpec((Q, 256), lambda t: (cm(t), 4)),
                  pl.BlockSpec((Q, 256), lambda t: (cm(t), 5)), pl.BlockSpec((Q, 128), lambda t: (cm(t), 0)),
                  pl.BlockSpec((1, 128), lambda t: (0, 0))],
        out_specs=[pl.BlockSpec((Q, 1024), lambda t: (cm(t), 0)), pl.BlockSpec((1, Q, 1024), lambda t: (cm(t), 0, 0))],
        out_shape=[jax.ShapeDtypeStruct((n * Q, 1024), F32), jax.ShapeDtypeStruct((n, Q, 1024), F32)],
        scratch_shapes=[pltpu.VMEM((Q, 1024), F32)], compiler_params=_cp("arbitrary"))(xbc, xbc, xbc, dt, alog)


def ssd_bwd(name, xbc, dt, alog, hs, dy, n, rev, dirn):
    chunk = _ssd_chunk(rev, dirn)

    def body(x_ref, b_ref, c_ref, dt_ref, al_ref, hs_ref, dy_ref, dx_ref, ddt_ref, dal_ref, dh_scr):
        tt = pl.program_id(0)
        ch = _chunk_of(n - 1 - tt, n, rev)

        @pl.when(tt == 0)
        def _():
            dh_scr[...] = jnp.zeros_like(dh_scr)

        xs = [x_ref[:, 128 * j:128 * (j + 1)] for j in range(8)]
        Bs = [b_ref[:, 128 * g:128 * (g + 1)] for g in range(2)]
        Cs = [c_ref[:, 128 * g:128 * (g + 1)] for g in range(2)]
        Hs = [hs_ref[0, :, 128 * j:128 * (j + 1)] for j in range(8)]
        live = (ch >= 2).astype(F32)
        dYs = [dy_ref[:, 128 * j:128 * (j + 1)] * live for j in range(8)]
        dHn = [dh_scr[:, 128 * j:128 * (j + 1)] for j in range(8)]
        _, vjp_fn = jax.vjp(chunk, xs, Bs, Cs, dt_ref[...], al_ref[...], Hs)
        dxs, dBs, dCs, ddt, dal, dHs = vjp_fn((dYs, dHn))
        for j in range(8):
            dx_ref[:, 128 * j:128 * (j + 1)] = dxs[j]
            dh_scr[:, 128 * j:128 * (j + 1)] = dHs[j]
        for g in range(2):
            dx_ref[:, 1024 + 128 * g:1024 + 128 * (g + 1)] = dBs[g]
            dx_ref[:, 1280 + 128 * g:1280 + 128 * (g + 1)] = dCs[g]
        ddt_ref[...] = ddt

        @pl.when(tt == 0)
        def _():
            dal_ref[...] = dal

        @pl.when(tt > 0)
        def _():
            dal_ref[...] += dal

    cm = lambda t: _chunk_of(n - 1 - t, n, rev)
    return pl.pallas_call(
        body, name=name, grid=(n,),
        in_specs=[pl.BlockSpec((Q, 1024), lambda t: (cm(t), 0)), pl.BlockSpec((Q, 256), lambda t: (cm(t), 4)),
                  pl.BlockSpec((Q, 256), lambda t: (cm(t), 5)), pl.BlockSpec((Q, 128), lambda t: (cm(t), 0)),
                  pl.BlockSpec((1, 128), lambda t: (0, 0)), pl.BlockSpec((1, Q, 1024), lambda t: (cm(t), 0, 0)),
                  pl.BlockSpec((Q, 1024), lambda t: (jnp.maximum(cm(t) - 2, 0), 0))],
        out_specs=[pl.BlockSpec((Q, 1536), lambda t: (cm(t), 0)), pl.BlockSpec((Q, 128), lambda t: (cm(t), 0)),
                   pl.BlockSpec((1, 128), lambda t: (0, 0))],
        out_shape=[jax.ShapeDtypeStruct((n * Q, 1536), F32), jax.ShapeDtypeStruct((n * Q, 128), F32),
                   jax.ShapeDtypeStruct((1, 128), F32)],
        scratch_shapes=[pltpu.VMEM((Q, 1024), F32)], compiler_params=_cp("arbitrary"))(xbc, xbc, xbc, dt, alog, hs, dy)


def f_norm0(x, g0, b0, sc, sh):
    x0 = _ln(x, g0, b0)
    return x0, x0 * (1.0 + sc) + sh


def f_dt(raw, bias):
    z = split4(raw)[0] + bias
    dt = jnp.maximum(z, 0.0) + jnp.log1p(jnp.exp(-jnp.abs(z)))
    return dt, dt


def f_gated_norm(yf, yb, xs, z, dcol, g):
    h = (yf + yb + xs * dcol) * _silu(z)
    return (h * lax.rsqrt(jnp.mean(h * h, axis=-1, keepdims=True) + LN_EPS) * g,)


def f_gmlp(uv, gmg, gmb, *wb):
    ws, bs = wb[:8], wb[8:]
    u, v = split2(uv)
    vn = split8(_ln(_gelu(v), gmg, gmb))
    mixed = concat8(tuple(mm(ws[g], vn[g]) + bs[g] for g in range(8)))
    return (_gelu(u) * mixed,)


def f_merge(ps, pg, gates, bg):
    gs, gg = split2(jax.nn.sigmoid(gates + bg))
    return (gs * ps + gg * pg,)


def f_res1(x0, out, g1, lg, lb, sc, sh):
    x1 = _ln(ALPHA * x0 + g1 * out, lg, lb)
    return x1, x1 * (1.0 + sc) + sh


def f_res2_loss(x1, ff, tgt, g2, lg, lb):
    x2 = _ln(ALPHA * x1 + g2 * ff, lg, lb)
    e = x2 - tgt
    return (0.5 * jnp.sum(jnp.mean(e * e, axis=-1, keepdims=True), axis=0, keepdims=True),)


def _row_tile(M):
    return 544 if M % 544 == 0 else (512 if M % 512 == 0 else M)


def core(x_all, tgt, mod_x, mod_c, X, S):
    R = x_all.shape[0]
    L = R - LC
    n = R // Q
    T = 256
    nt, ntl = R // T, L // T
    tmR, tmL = _row_tile(R), _row_tile(L)
    tkR = 256 if R % 512 else 512
    tkL = 512 if L % 512 == 0 else 256
    row = lambda v: v.reshape(1, -1)
    mx = [row(mod_x[k]) for k in range(6)]
    mc = [row(mod_c[k]) for k in range(6)]
    sel = lambda i: jnp.minimum(i, 1)
    sc1 = jnp.stack([mc[1], mx[1]])
    sh1 = jnp.stack([mc[0], mx[0]])
    ln0 = [(row(S['ln0_g']), None), (row(S['ln0_b']), None), (sc1, sel), (sh1, sel)]

    x0, xm, xmT = stage_fwd("norm0_fwd", f_norm0, T, nt, [(x_all, D, 0, 0)], ln0,
                            [('new', R, D, 0), ('new', R, D, 0, BF16), ('transposed', R, D, BF16, 1)])
    w_in = X.w_in()
    proj = matmul("proj_fwd", xm, w_in, 'nn', tmR, PW // 2, 1024)
    conv_w8 = jnp.pad(S['conv_w'], ((0, 3), (0, 0)))
    conv_b = row(S['conv_b'])
    xbc = conv_fwd("conv_fwd", proj, conv_w8, conv_b, R)
    dt_bias = jnp.pad(S['dt_bias'].reshape(1, 32), ((0, 0), (0, 96)))
    alog = jnp.pad(S['a_log'].reshape(1, 32), ((0, 0), (0, 96)))
    x_dt = [(proj, 512, C_DT // 512, 0)]
    dt_f, dt_b = stage_fwd("dt_fwd", f_dt, T, nt, x_dt, [(dt_bias, None)], [('new', R, 128, 0), ('new', R, 128, 0)])
    y_f, hs_f = ssd_fwd("ssd_fwd_f", xbc, dt_f, alog, n, False, 0)
    y_b, hs_b = ssd_fwd("ssd_fwd_b", xbc, dt_b, alog, n, True, 1)
    W = X.rest()
    dcol = jnp.repeat(S['d_skip'][0] + S['d_skip'][1], 64).reshape(1, D)
    x_gn = [(y_f, D, 0, 1), (y_b, D, 0, 1), (xbc, D, 0, 1), (proj, D, C_Z // D, 1)]
    p_gn = [(dcol, None), (row(S['ssd_norm_g']), None)]
    (yn,) = stage_fwd("gnorm_fwd", f_gated_norm, T, ntl, x_gn, p_gn, [('new', L, D, 0, BF16)])
    x_gm = [(proj, 2 * D, C_UV // (2 * D), LC // Q)]
    p_gm = ([(row(S['gm_norm_g']), None), (row(S['gm_norm_b']), None)]
            + [(S['w_spatial'][g], None) for g in range(8)] + [(S['b_spatial'][g].reshape(Q, 1), None) for g in range(8)])
    (y_gm,) = stage_fwd("gmlp_fwd", f_gmlp, Q, L // Q, x_gm, p_gm, [('new', L, D, 0, BF16)])
    p_ssd = matmul("pssd_fwd", yn, W['w_ssd_proj'], 'nn', tmL, 1024, 1024)
    p_g = matmul("pgm_fwd", y_gm, W['w_gm_proj'], 'nn', tmL, 1024, 1024)
    x_mg = [(p_ssd, D, 0, 0), (p_g, D, 0, 0), (proj, 2 * D, C_GATE // (2 * D), 1)]
    p_mg = [(row(S['b_gate']), None)]
    (merged,) = stage_fwd("merge_fwd", f_merge, T, ntl, x_mg, p_mg, [('new', L, D, 0, BF16)])
    out = matmul("out_fwd", merged, W['w_out'], 'nn', tmL, 1024, 1024)
    x_r1 = [(x0, D, 0, 1), (out, D, 0, 0)]
    p_r1 = [(mx[2], None), (row(S['ln1_g']), None), (row(S['ln1_b']), None), (mx[4], None), (mx[3], None)]
    x1, hm, hmT = stage_fwd("res1_fwd", f_res1, T, ntl, x_r1, p_r1,
                            [('new', L, D, 0), ('new', L, D, 0, BF16), ('transposed', L, D, BF16, 1)])
    a1, a3, act = ffn_in_fwd("ffn_in_fwd", hm, W['w_ff1'], W['w_ff3'], T)
    ff = ff_out_fwd("ff2_fwd", act, W['w_ff2'], tmL)
    x_r2 = [(x1, D, 0, 0), (ff, D, 0, 0), (tgt, D, 0, 0)]
    p_r2 = [(mx[5], None), (row(S['ln2_g']), None), (row(S['ln2_b']), None)]

    dx1_a, dff, dg2, dl2g, dl2b, loss = stage_bwd(
        "res2_bwd", f_res2_loss, T, ntl, x_r2, p_r2, [1.0],
        [('new', L, D, 0), ('new', L, D, 0, BF16), None], [True, True, True], primal=[(0, (1, 1))])
    da1, da3 = ffn_out_bwd_x("ffn_out_bwd_x", dff, W['w_ff2'], a1, a3, T)
    gw_ff2 = ff_out_bwd_w("ff2_bwd_w", act, dff, tkL)
    dhm = ff_in_bwd_x("ff1_bwd_x", da1, W['w_ff1'], tmL)
    dhm = ff_in_bwd_x("ff3_bwd_x", da3, W['w_ff3'], tmL, add=dhm)
    gw_ff1 = ff_in_bwd_w("ff1_bwd_w", hmT, da1, tkL)
    gw_ff3 = ff_in_bwd_w("ff3_bwd_w", hmT, da3, tkL)
    X.grads('ffn', {'w_ff2': gw_ff2, 'w_ff1': gw_ff1, 'w_ff3': gw_ff3})
    dx0_a, dout, dg1, dl1g, dl1b, dsc2, dsh2 = stage_bwd(
        "res1_bwd", f_res1, T, ntl, x_r1, p_r1, [(dx1_a, D, 0, 0), (dhm, D, 0, 0)],
        [('new', L, D, 0), ('new', L, D, 0, BF16)], [True] * 5)
    dmerged = matmul("out_bwd_x", dout, W['w_out'], 'nt', tmL, 1024, 1024)
    gw_out = matmul("out_bwd_w", merged, dout, 'tn', 1024, 1024, tkL, BF16)
    lt, lq = -(LC // T), -(LC // Q)
    x_mg_b = [(p_ssd, D, 0, lt), (p_g, D, 0, lt), (proj, 2 * D, C_GATE // (2 * D), 0)]
    dp_ssd, dp_g, dproj, dbg = stage_bwd(
        "merge_bwd", f_merge, T, nt, x_mg_b, p_mg, [(dmerged, D, 0, lt)],
        [('new', L, D, lt, BF16), ('new', L, D, lt, BF16), ('part', R, PW, 2 * D, C_GATE // (2 * D), 0, BF16)], [True])
    dyn = matmul("pssd_bwd_x", dp_ssd, W['w_ssd_proj'], 'nt', tmL, 1024, 1024)
    gw_ssd = matmul("pssd_bwd_w", yn, dp_ssd, 'tn', 1024, 1024, tkL, BF16)
    dy_gm = matmul("pgm_bwd_x", dp_g, W['w_gm_proj'], 'nt', tmL, 1024, 1024)
    gw_gm = matmul("pgm_bwd_w", y_gm, dp_g, 'tn', 1024, 1024, tkL, BF16)
    X.grads('proj', {'w_out': gw_out, 'w_ssd_proj': gw_ssd, 'w_gm_proj': gw_gm})
    r_gm = stage_bwd("gmlp_bwd", f_gmlp, Q, n, [(proj, 2 * D, C_UV // (2 * D), 0)], p_gm, [(dy_gm, D, 0, lq)],
                     [('alias', dproj, 2 * D, C_UV // (2 * D), 0)], [True] * 18)
    dproj, dgmg, dgmb, dws, dbs = r_gm[0], r_gm[1], r_gm[2], r_gm[3:11], r_gm[11:19]
    x_gn_b = [(y_f, D, 0, 0), (y_b, D, 0, 0), (xbc, D, 0, 0), (proj, D, C_Z // D, 0)]
    dy, dskipx, dproj, ddcol, dng = stage_bwd(
        "gnorm_bwd", f_gated_norm, T, nt, x_gn_b, p_gn, [(dyn, D, 0, lt)],
        [('new', L, D, lt), None, ('new', L, D, lt), ('alias', dproj, D, C_Z // D, 0)], [True, True])
    dxbc_f, ddt_f, dal_f = ssd_bwd("ssd_bwd_f", xbc, dt_f, alog, hs_f, dy, n, False, 0)
    dxbc_b, ddt_b, dal_b = ssd_bwd("ssd_bwd_b", xbc, dt_b, alog, hs_b, dy, n, True, 1)
    dproj, ddtb = stage_bwd("dt_bwd", f_dt, T, nt, x_dt, [(dt_bias, None)],
                            [(ddt_f, 128, 0, 0), (ddt_b, 128, 0, 0)],
                            [('alias', dproj, 512, C_DT // 512, 0)], [True])
    dproj, dcw8, dcb = conv_bwd("conv_bwd", proj, conv_w8, conv_b, dxbc_f, dxbc_b, dskipx, dproj, R)
    gw_in = matmul("proj_bwd_w", xmT, dproj, 'nn', 1024, PW // 4, tkR, BF16)
    X.grads('in', {'w_in': gw_in})
    dxm = matmul("proj_bwd_x", dproj, w_in, 'nt', R // 4 if R % 32 == 0 else R, 1024, 1024)
    grad_x, dl0g, dl0b, dsc1, dsh1 = stage_bwd(
        "norm0_bwd", f_norm0, T, nt, [(x_all, D, 0, 0)], ln0, [(dx0_a, D, 0, -1), (dxm, D, 0, 0)],
        [('new', L, D, -1)], [True] * 4)

    zero = jnp.zeros((D,), F32)
    flat = lambda v: v.reshape(-1)
    small = {
        'loss': flat(loss), 'ln0_g': flat(dl0g), 'ln0_b': flat(dl0b),
        'dmod_x': jnp.concatenate([flat(dsh1[1]), flat(dsc1[1]), flat(dg1), flat(dsh2), flat(dsc2), flat(dg2)]),
        'dmod_c': jnp.concatenate([flat(dsh1[0]), flat(dsc1[0]), zero, zero, zero, zero]),
        'conv_w': flat(dcw8[:5]), 'conv_b': flat(dcb), 'dt_bias': flat(ddtb[:, :32]),
        'a_log': flat((dal_f + dal_b)[:, :32]),
        'd_skip': flat(jnp.tile(ddcol.reshape(1, NH, 64).sum(-1), (2, 1))),
        'ssd_norm_g': flat(dng), 'gm_norm_g': flat(dgmg), 'gm_norm_b': flat(dgmb),
        'w_spatial': flat(jnp.stack(dws)), 'b_spatial': flat(jnp.stack(dbs)), 'b_gate': flat(dbg),
        'ln1_g': flat(dl1g), 'ln1_b': flat(dl1b), 'ln2_g': flat(dl2g), 'ln2_b': flat(dl2b),
    }
    return grad_x, small


def _place():
    return lax.axis_index("x"), lax.axis_index("y"), lax.axis_index("c")


def allgather8(name, blk, hbm):
    space = pl.ANY if hbm else pltpu.VMEM

    def body(x_ref, out_ref, send_sems, recv_sems, local_sem):
        x, y, c = _place()
        me, sibling = (x, y, c), (x, y, 1 - c)
        chips = [(1 - x, y), (x, 1 - y), (1 - x, 1 - y)]

        def slot(px, py, pc):
            return out_ref.at[4 * px + 2 * py + pc]

        def copy(k, block, to, src=None):
            return pltpu.make_async_remote_copy(
                src_ref=slot(*block) if src is None else src, dst_ref=slot(*block),
                send_sem=send_sems.at[k], recv_sem=recv_sems.at[k], device_id=to, device_id_type=MESH)

        mine = pltpu.make_async_copy(x_ref, slot(*me), local_sem)
        mine.start()
        first = [copy(0, me, sibling, src=x_ref)]
        first += [copy(1 + j, me, (*chip, c), src=x_ref) for j, chip in enumerate(chips)]
        for cp in first:
            cp.start()
        passed = [copy(4 + j, (*chip, c), sibling) for j, chip in enumerate(chips)]
        for j, chip in enumerate(chips):
            copy(1 + j, (*chip, c), me).wait_recv()
            passed[j].start()
        copy(0, sibling, me).wait_recv()
        for j, chip in enumerate(chips):
            copy(4 + j, (*chip, 1 - c), me).wait_recv()
        for cp in first + passed:
            cp.wait_send()
        mine.wait()

    return pl.pallas_call(
        body, name=name, out_shape=jax.ShapeDtypeStruct((8,) + blk.shape, blk.dtype),
        in_specs=[pl.BlockSpec(memory_space=space)], out_specs=pl.BlockSpec(memory_space=space),
        scratch_shapes=[pltpu.SemaphoreType.DMA((7,)), pltpu.SemaphoreType.DMA((7,)), pltpu.SemaphoreType.DMA],
        compiler_params=pltpu.CompilerParams(vmem_limit_bytes=VMEM_LIMIT_V7X))(blk)


def _peers(place):
    x, y, c = place
    return [((1 - x) if k & 4 else x, (1 - y) if k & 2 else y, (1 - c) if k & 1 else c) for k in range(1, 8)]


def _slot(p):
    return 4 * p[0] + 2 * p[1] + p[2]


def plan_gather(place, srcs, lands):
    remote = [(s, l.at[_slot(place)], to) for s, l in zip(srcs, lands) for to in _peers(place)]
    return remote, [(s, l.at[_slot(place)]) for s, l in zip(srcs, lands)]


def plan_to_owner(place, srcs, lands):
    remote = [(s.at[2 * to[0] + to[1], to[2]], l.at[_slot(place)], to) for s, l in zip(srcs, lands) for to in _peers(place)]
    x, y, c = place
    return remote, [(s.at[2 * x + y, c], l.at[_slot(place)]) for s, l in zip(srcs, lands)]


def sequencer_exchange(name, collective_id, srcs, land_shapes, plan):
    n = len(srcs)
    src_refs = [jax.new_ref(a, memory_space=pltpu.MemorySpace.HBM) for a in srcs]
    land_refs = [jax.empty_ref(s, memory_space=pltpu.MemorySpace.HBM) for s in land_shapes]

    @pl.kernel(mesh=plsc.ScalarSubcoreMesh(axis_name="sequencer", num_cores=1), name=name,
               scratch_types=(pltpu.SemaphoreType.DMA((7 * n,)), pltpu.SemaphoreType.DMA((7 * n,)),
                              pltpu.SemaphoreType.DMA((n,))),
               compiler_params=pltpu.CompilerParams(collective_id=collective_id))
    def launch(send_sems, recv_sems, local_sems):
        place = _place()
        barrier = pltpu.get_barrier_semaphore()
        for to in _peers(place):
            pl.semaphore_signal(barrier, inc=1, device_id=to, device_id_type=MESH)
        pl.semaphore_wait(barrier, 7)
        remote, local = plan(place, src_refs, land_refs)
        mine = [pltpu.make_async_copy(s, d, local_sems.at[a]) for a, (s, d) in enumerate(local)]
        for cp in mine:
            cp.start()
        cps = [pltpu.make_async_remote_copy(src_ref=s, dst_ref=d, send_sem=send_sems.at[k], recv_sem=recv_sems.at[k],
                                            device_id=to, device_id_type=MESH) for k, (s, d, to) in enumerate(remote)]
        for cp in cps:
            cp.start()
        for cp in mine:
            cp.wait()
        for cp in cps:
            cp.wait()

    launch()
    return land_refs


def sibling_pair(name, hs):
    n = len(hs)

    def body(*refs):
        ins, outs = refs[:n], refs[n:2 * n]
        send_sems, recv_sems = refs[2 * n:]
        x, y, c = _place()
        cps = [pltpu.make_async_remote_copy(src_ref=outs[a].at[c], dst_ref=outs[a].at[c], send_sem=send_sems.at[a],
                                            recv_sem=recv_sems.at[a], device_id=(x, y, 1 - c), device_id_type=MESH)
               for a in range(n)]
        for cp in cps:
            cp.start()
        for a in range(n):
            pltpu.make_async_remote_copy(src_ref=outs[a].at[1 - c], dst_ref=outs[a].at[1 - c], send_sem=send_sems.at[a],
                                         recv_sem=recv_sems.at[a], device_id=(x, y, 1 - c),
                                         device_id_type=MESH).wait_recv()
        for cp in cps:
            cp.wait_send()

    any_spec = pl.BlockSpec(memory_space=pl.ANY)
    return pl.pallas_call(
        body, name=name, out_shape=[jax.ShapeDtypeStruct(h.shape, h.dtype) for h in hs],
        in_specs=[any_spec] * n, out_specs=[any_spec] * n, input_output_aliases={a: a for a in range(n)},
        scratch_shapes=[pltpu.SemaphoreType.DMA((n,)), pltpu.SemaphoreType.DMA((n,))])(*hs)


def owner_sum(name, land):
    _, r, w = land.shape
    T = r // 2

    def body(_, l_ref, o_ref):
        acc = l_ref[0].astype(F32)
        for j in range(1, 8):
            acc = acc + l_ref[j].astype(F32)
        o_ref[...] = acc

    grid_spec = pltpu.PrefetchScalarGridSpec(
        num_scalar_prefetch=1, grid=(2,),
        in_specs=[pl.BlockSpec((8, T, w), lambda i, at: (0, i, 0))],
        out_specs=pl.BlockSpec((None, T, w), lambda i, at: (at[0], i, 0)))
    at = jnp.stack([lax.axis_index("c")]).astype(jnp.int32)
    return pl.pallas_call(body, name=name, grid_spec=grid_spec, out_shape=jax.ShapeDtypeStruct((2, r, w), F32),
                          compiler_params=_cp("arbitrary"))(at, land)


W_IN_RUNS = ((0, 2, 1296, 376), (376, 3, 0, 1672), (2048, 1, 920, 752), (2800, 2, 0, 1296), (4096, 0, 0, 1024),
             (5120, 0, 1024, 648), (5768, 1, 0, 920))


def w_in_to_padded(name, g4):
    T = 128

    def body(g_ref, o_ref):
        o_ref[:, D_PROJ:PW] = jnp.zeros((T, PW - D_PROJ), o_ref.dtype)
        for (a, s, j0, w) in W_IN_RUNS:
            o_ref[:, a:a + w] = g_ref[s, :, j0:j0 + w]

    return pl.pallas_call(body, name=name, grid=(D // T,), in_specs=[pl.BlockSpec((4, T, 1672), lambda i: (0, i, 0))],
                          out_specs=pl.BlockSpec((T, PW), lambda i: (i, 0)),
                          out_shape=jax.ShapeDtypeStruct((D, PW), g4.dtype), compiler_params=_cp("arbitrary"))(g4)


def w_in_from_padded(name, gp):
    T = 128

    def body(g_ref, o_ref):
        for (a, s, j0, w) in W_IN_RUNS:
            o_ref[s, :, j0:j0 + w] = g_ref[:, a:a + w]

    return pl.pallas_call(body, name=name, grid=(D // T,), in_specs=[pl.BlockSpec((T, PW), lambda i: (i, 0))],
                          out_specs=pl.BlockSpec((4, T, 1672), lambda i: (0, i, 0)),
                          out_shape=jax.ShapeDtypeStruct((4, D, 1672), gp.dtype), compiler_params=_cp("arbitrary"))(gp)


def sum_devices(name, g):
    def body(g_ref, o_ref):
        acc = g_ref[0]
        for k in range(1, 8):
            acc = acc + g_ref[k]
        o_ref[...] = acc

    return pl.pallas_call(body, name=name, out_shape=jax.ShapeDtypeStruct(g.shape[1:], F32),
                          compiler_params=pltpu.CompilerParams(vmem_limit_bytes=VMEM_LIMIT_V7X))(g)


def adamw(name, w, g, m, v, T):
    r, wd = w.shape
    c1 = 1.0 - ADAM_B1 ** ADAM_STEP
    c2 = 1.0 - ADAM_B2 ** ADAM_STEP

    def body(w_ref, g_ref, m_ref, v_ref, d_ref, mo_ref, vo_ref):
        gv = g_ref[...]
        mn = ADAM_B1 * m_ref[...] + (1.0 - ADAM_B1) * gv
        vn = ADAM_B2 * v_ref[...] + (1.0 - ADAM_B2) * (gv * gv)
        d_ref[...] = -ADAM_LR * ((mn / c1) / (jnp.sqrt(vn / c2) + ADAM_EPS) + ADAM_WD * w_ref[...])
        mo_ref[...] = mn
        vo_ref[...] = vn

    spec = pl.BlockSpec((T, wd), lambda i: (i, 0))
    return pl.pallas_call(body, name=name, grid=(r // T,), in_specs=[spec] * 4, out_specs=[spec] * 3,
                          out_shape=[jax.ShapeDtypeStruct((r, wd), F32)] * 3, compiler_params=_cp("arbitrary"))(w, g, m, v)


BIG = {'w_in': (1024, 1672), 'w_ssd_proj': (256, 1024), 'w_gm_proj': (256, 1024), 'w_out': (256, 1024),
       'w_ff1': (1024, 704), 'w_ff3': (1024, 704), 'w_ff2': (704, 1024)}


class Flat:
    def __init__(self, segs):
        self.off, o = {}, 0
        for name, size in segs:
            self.off[name] = (o, size)
            o += -(-size // 128) * 128
        self.rows = -(-o // 1024) * 8

    def pack(self, vals):
        parts = []
        for name, (o, size) in self.off.items():
            v = vals[name].reshape(-1).astype(F32)
            parts.append(jnp.pad(v, (0, -(-size // 128) * 128 - size)))
        buf = jnp.concatenate(parts)
        return jnp.pad(buf, (0, self.rows * 128 - buf.shape[0])).reshape(self.rows, 128)

    def get(self, buf, name, shape=None):
        o, size = self.off[name]
        v = buf[o // 128:(o + size + 127) // 128].reshape(-1)[:size]
        return v if shape is None else v.reshape(shape)


PARTIALS = Flat([('loss', 1), ('ln0_g', D), ('ln0_b', D), ('dmod_x', 6 * D), ('dmod_c', 6 * D), ('conv_w', 5 * 1536),
                 ('conv_b', 1536), ('dt_bias', 32), ('a_log', 32), ('d_skip', 32), ('ssd_norm_g', D),
                 ('gm_norm_g', D), ('gm_norm_b', D), ('w_spatial', 8 * Q * Q), ('b_spatial', 8 * Q), ('b_gate', 2 * D),
                 ('ln1_g', D), ('ln1_b', D), ('ln2_g', D), ('ln2_b', D)])

WEIGHTS = ('c_ctx', 'ln0_g', 'ln0_b', 'w_ada', 'b_ada', 'w_in', 'conv_w', 'conv_b', 'dt_bias', 'a_log', 'd_skip',
           'ssd_norm_g', 'gm_norm_g', 'gm_norm_b', 'w_spatial', 'b_spatial', 'b_gate', 'w_ssd_proj', 'w_gm_proj',
           'w_out', 'ln1_g', 'ln1_b', 'w_ff1', 'w_ff3', 'w_ff2', 'ln2_g', 'ln2_b')
BIG_NAMES = tuple(BIG)
SMALL_NAMES = tuple(n for n in WEIGHTS if n not in BIG_NAMES and n != 'w_ada')


def kernel(x, c, ctx, c_ctx, ln0_g, ln0_b, w_ada, b_ada, w_in, conv_w, conv_b, dt_bias, a_log, d_skip, ssd_norm_g, gm_norm_g, gm_norm_b, w_spatial, b_spatial, b_gate, w_ssd_proj, w_gm_proj, w_out, ln1_g, ln1_b, w_ff1, w_ff3, w_ff2, ln2_g, ln2_b, loss_target, m_c_ctx, m_ln0_g, m_ln0_b, m_w_ada, m_b_ada, m_w_in, m_conv_w, m_conv_b, m_dt_bias, m_a_log, m_d_skip, m_ssd_norm_g, m_gm_norm_g, m_gm_norm_b, m_w_spatial, m_b_spatial, m_b_gate, m_w_ssd_proj, m_w_gm_proj, m_w_out, m_ln1_g, m_ln1_b, m_w_ff1, m_w_ff3, m_w_ff2, m_ln2_g, m_ln2_b, v_c_ctx, v_ln0_g, v_ln0_b, v_w_ada, v_b_ada, v_w_in, v_conv_w, v_conv_b, v_dt_bias, v_a_log, v_d_skip, v_ssd_norm_g, v_gm_norm_g, v_gm_norm_b, v_w_spatial, v_b_spatial, v_b_gate, v_w_ssd_proj, v_w_gm_proj, v_w_out, v_ln1_g, v_ln1_b, v_w_ff1, v_w_ff3, v_w_ff2, v_ln2_g, v_ln2_b):
    wts = dict(c_ctx=c_ctx, ln0_g=ln0_g, ln0_b=ln0_b, w_ada=w_ada, b_ada=b_ada, w_in=w_in, conv_w=conv_w, conv_b=conv_b,
               dt_bias=dt_bias, a_log=a_log, d_skip=d_skip, ssd_norm_g=ssd_norm_g, gm_norm_g=gm_norm_g,
               gm_norm_b=gm_norm_b, w_spatial=w_spatial, b_spatial=b_spatial, b_gate=b_gate, w_ssd_proj=w_ssd_proj,
               w_gm_proj=w_gm_proj, w_out=w_out, ln1_g=ln1_g, ln1_b=ln1_b, w_ff1=w_ff1, w_ff3=w_ff3, w_ff2=w_ff2,
               ln2_g=ln2_g, ln2_b=ln2_b)
    ms = dict(zip(WEIGHTS, (m_c_ctx, m_ln0_g, m_ln0_b, m_w_ada, m_b_ada, m_w_in, m_conv_w, m_conv_b, m_dt_bias, m_a_log,
                            m_d_skip, m_ssd_norm_g, m_gm_norm_g, m_gm_norm_b, m_w_spatial, m_b_spatial, m_b_gate,
                            m_w_ssd_proj, m_w_gm_proj, m_w_out, m_ln1_g, m_ln1_b, m_w_ff1, m_w_ff3, m_w_ff2, m_ln2_g,
                            m_ln2_b)))
    vs = dict(zip(WEIGHTS, (v_c_ctx, v_ln0_g, v_ln0_b, v_w_ada, v_b_ada, v_w_in, v_conv_w, v_conv_b, v_dt_bias, v_a_log,
                            v_d_skip, v_ssd_norm_g, v_gm_norm_g, v_gm_norm_b, v_w_spatial, v_b_spatial, v_b_gate,
                            v_w_ssd_proj, v_w_gm_proj, v_w_out, v_ln1_g, v_ln1_b, v_w_ff1, v_w_ff3, v_w_ff2, v_ln2_g,
                            v_ln2_b)))
    px, py, pc = _place()
    shard = 2 * px + py
    dev = 2 * shard + pc
    take = lambda a, i, axis=0: lax.dynamic_index_in_dim(a, i, axis, keepdims=False)

    half = lambda n: take(wts[n][0].reshape(2, BIG[n][0] // 2, BIG[n][1]), pc).astype(BF16)

    pre = jnp.concatenate([c, jnp.pad(conv_w[0], ((0, 0), (0, D - 384))), jnp.zeros((2, D), F32)], axis=0)
    pre = allgather8("gather_cond", pre, False)
    conv_w_full = pre[0::2, 1:6, :384].transpose(1, 0, 2).reshape(5, 1536)
    a16 = jnp.concatenate([_silu(pre[:, 0, :]), _silu(c_ctx)[None], jnp.zeros((7, D), F32)], axis=0)
    mod = matmul("ada_fwd", a16, w_ada[0], 'nn', 16, 512, 1024)
    mod = mod + lax.dynamic_slice_in_dim(b_ada[0], shard * 1536, 1536)[None]
    mod = allgather8("gather_mod", mod, False)
    mod = jnp.concatenate([mod[0], mod[2], mod[4], mod[6]], axis=1)
    mod_x = take(mod, dev).reshape(6, D)
    mod_c = mod[8].reshape(6, D)

    def full(n, blocks):
        r, w = BIG[n]
        return blocks.reshape(4, r, w) if w != D else blocks.reshape(4 * r, w)

    class Exchanges:
        rest_names = BIG_NAMES[1:]

        def __init__(self):
            self.pending = []

        def w_in(self):
            blocks = allgather8("gather_w_in", half('w_in'), True)
            w = w_in_to_padded("w_in_layout", full('w_in', blocks))
            halves = [half(n) for n in self.rest_names]
            halves[0], _ = lax.optimization_barrier((halves[0], blocks))
            lands = [jax.ShapeDtypeStruct((8,) + h.shape, BF16) for h in halves]
            self.rest_refs = sequencer_exchange("gather_rest", 1, halves, lands, plan_gather)
            return w

        def rest(self):
            return {n: full(n, r[...]) for n, r in zip(self.rest_names, self.rest_refs)}

        def grads(self, group, gs):
            if group == 'in':
                gs = {'w_in': w_in_from_padded("w_in_grad_layout", gs['w_in'])}
            blocks = [g.reshape(4, 2, BIG[n][0] // 2, BIG[n][1]) for n, g in gs.items()]
            lands = [jax.ShapeDtypeStruct((8,) + b.shape[2:], BF16) for b in blocks]
            refs = sequencer_exchange("grads_" + group, 2 + len(self.pending), blocks, lands, plan_to_owner)
            self.pending.append((tuple(gs), refs))

        def finish(self):
            names, halves = [], []
            for ns, refs in self.pending:
                names += ns
                halves += [owner_sum("grads_sum_" + n, r[...]) for n, r in zip(ns, refs)]
            return {n: h.reshape(BIG[n]) for n, h in zip(names, sibling_pair("grads_halves", halves))}

    S = dict(ln0_g=ln0_g, ln0_b=ln0_b, conv_w=conv_w_full, conv_b=conv_b[0], dt_bias=dt_bias[0], a_log=a_log[0],
             d_skip=d_skip[0], ssd_norm_g=ssd_norm_g[0], gm_norm_g=gm_norm_g[0], gm_norm_b=gm_norm_b[0],
             w_spatial=w_spatial[0], b_spatial=b_spatial[0], b_gate=b_gate[0], ln1_g=ln1_g[0], ln1_b=ln1_b[0],
             ln2_g=ln2_g[0], ln2_b=ln2_b[0])
    x_all = jnp.concatenate([ctx[0], x[0]], axis=0)
    exchanges = Exchanges()
    grad_x, gsmall = core(x_all, loss_target[0], mod_x, mod_c, exchanges, S)

    packed = PARTIALS.pack(gsmall)
    parts_ref, = sequencer_exchange("gather_partials", 5, [packed], [jax.ShapeDtypeStruct((8,) + packed.shape, F32)],
                                    plan_gather)
    delta, new_m, new_v = {}, {}, {}

    def step(n, gn):
        T = 352 if n == 'w_ff2' else 256
        delta[n], new_m[n], new_v[n] = adamw("adamw_" + n, wts[n][0], gn, ms[n][0], vs[n][0], T)

    g = exchanges.finish()
    for n in BIG_NAMES:
        step(n, g[n])

    parts = parts_ref[...]
    tot = sum_devices("partials_sum", parts)
    g.update({n: PARTIALS.get(tot, n) for n in ('ln0_g', 'ln0_b', 'conv_b', 'dt_bias', 'a_log', 'd_skip', 'ssd_norm_g',
                                                'gm_norm_g', 'gm_norm_b', 'w_spatial', 'b_spatial', 'b_gate', 'ln1_g',
                                                'ln1_b', 'ln2_g', 'ln2_b')})
    loss = PARTIALS.get(tot, 'loss', ())
    dmod_c = PARTIALS.get(tot, 'dmod_c')
    g['b_ada'] = PARTIALS.get(tot, 'dmod_x') + dmod_c
    g['conv_w'] = lax.dynamic_slice_in_dim(PARTIALS.get(tot, 'conv_w', (5, 1536)), shard * 384, 384, axis=1)
    o, size = PARTIALS.off['dmod_x']
    dmod_rows = parts[:, o // 128:(o + size) // 128].reshape(8, size)
    dm = jnp.concatenate([dmod_rows, dmod_c[None], jnp.zeros((7, 6 * D), F32)], axis=0)
    dm = lax.dynamic_slice_in_dim(dm, shard * 1536, 1536, axis=1)
    g['w_ada'] = matmul("ada_bwd_w", a16, dm, 'tn', 1024, 512, 16)
    dm_c = jnp.concatenate([dm[8:9], jnp.zeros((15, 1536), F32)], axis=0)
    dc = matmul("ada_bwd_c", dm_c, w_ada[0], 'nt', 16, 1024, 512)
    dc = allgather8("gather_dcctx", dc, False)[:, 0, :]
    dc = ((dc[0] + dc[2]) + dc[4]) + dc[6]
    sg = jax.nn.sigmoid(c_ctx)
    g['c_ctx'] = dc * (sg * (1.0 + c_ctx * (1.0 - sg)))

    step('w_ada', g['w_ada'])
    lay = Flat([(n, wts[n].size) for n in SMALL_NAMES])
    d_, m_, v_ = adamw("adamw_small", lay.pack(wts), lay.pack(g), lay.pack(ms), lay.pack(vs), lay.rows)
    for n in SMALL_NAMES:
        delta[n], new_m[n], new_v[n] = (lay.get(b, n) for b in (d_, m_, v_))

    shp = lambda d: [d[n].reshape(wts[n].shape) for n in WEIGHTS]
    return (loss, grad_x[None], *shp(g), *shp(delta), *shp(new_m), *shp(new_v))
```

```python
import functools

import jax
import jax.numpy as jnp
from jax import lax
from jax.experimental import pallas as pl
from jax.experimental.pallas import tpu as pltpu
from jax.experimental.pallas import tpu_sc as plsc

F32 = jnp.float32
BF16 = jnp.bfloat16
MESH = pl.DeviceIdType.MESH

VMEM_LIMIT_V7X = 56 * 1024 * 1024

D = 1024
LC = 256
Q = 128
NH = 16
D_FF = 2816
LN_EPS = 1e-5
ALPHA = 2.0 ** 0.25

PW = 7168
C_GATE, C_UV, C_Z, C_XBC, C_DT = 0, 2048, 4096, 5120, 6656
D_PROJ = 6688

ADAM_LR, ADAM_B1, ADAM_B2, ADAM_EPS, ADAM_WD, ADAM_STEP = 0.001, 0.9, 0.999, 1e-08, 0.01, 10


def _cp(*sem):
    return pltpu.CompilerParams(dimension_semantics=sem, vmem_limit_bytes=VMEM_LIMIT_V7X)


def _dot(a, b, ca, cb):
    return lax.dot_general(a.astype(BF16), b.astype(BF16), (((ca,), (cb,)), ((), ())),
                           preferred_element_type=F32)


@jax.custom_vjp
def mm(a, b):
    return _dot(a, b, 1, 0)


mm.defvjp(lambda a, b: (_dot(a, b, 1, 0), (a, b)),
          lambda r, g: (_dot(g, r[1], 1, 1), _dot(r[0], g, 0, 0)))


@jax.custom_vjp
def mm_nt(a, b):
    return _dot(a, b, 1, 1)


mm_nt.defvjp(lambda a, b: (_dot(a, b, 1, 1), (a, b)),
             lambda r, g: (_dot(g, r[1], 1, 0), _dot(g, r[0], 0, 0)))


@jax.custom_vjp
def mm_tn(a, b):
    return _dot(a, b, 0, 0)


mm_tn.defvjp(lambda a, b: (_dot(a, b, 0, 0), (a, b)),
             lambda r, g: (_dot(r[1], g, 1, 1), _dot(r[0], g, 1, 0)))


def _dot32(a, b):
    return lax.dot_general(a, b, (((1,), (0,)), ((), ())), precision=lax.Precision.HIGHEST,
                           preferred_element_type=F32)


def _cumsum_fn(rev):
    def tri(transpose):
        r = lax.broadcasted_iota(jnp.int32, (Q, Q), 0)
        c = lax.broadcasted_iota(jnp.int32, (Q, Q), 1)
        keep = (r >= c) if (rev == transpose) else (r <= c)
        return jnp.where(keep, 1.0, 0.0).astype(F32)

    @jax.custom_vjp
    def cums(a):
        return _dot32(tri(False), a)

    cums.defvjp(lambda a: (_dot32(tri(False), a), None), lambda _, g: (_dot32(tri(True), g),))
    return cums


def _cols(v, k):
    w = v.shape[1] // k
    return tuple(v[:, w * i:w * (i + 1)] for i in range(k))


def _splitter(k):
    @jax.custom_vjp
    def split(v):
        return _cols(v, k)

    @jax.custom_vjp
    def concat(ps):
        return jnp.concatenate(ps, axis=1)

    split.defvjp(lambda v: (_cols(v, k), None), lambda _, g: (jnp.concatenate(g, axis=1),))
    concat.defvjp(lambda ps: (jnp.concatenate(ps, axis=1), None), lambda _, g: (_cols(g, k),))
    return split, concat


split2, _ = _splitter(2)
split4, _ = _splitter(4)
split8, concat8 = _splitter(8)


def _ln(x, g, b):
    mu = jnp.mean(x, axis=-1, keepdims=True)
    xc = x - mu
    var = jnp.mean(xc * xc, axis=-1, keepdims=True)
    return xc * lax.rsqrt(var + LN_EPS) * g + b


def _silu(x):
    return x * jax.nn.sigmoid(x)


def _gelu(x):
    return 0.5 * x * (1.0 + jnp.tanh(0.7978845608028654 * (x + 0.044715 * (x * x * x))))


def _xspec(T, w, col, roff):
    return pl.BlockSpec((T, w), lambda i, col=col, roff=roff: (jnp.maximum(i + roff, 0), col))


def _pspec(p, sel):
    if sel is None:
        return pl.BlockSpec(p.shape, lambda i, n=p.ndim: (0,) * n)
    return pl.BlockSpec((1,) + p.shape[1:], lambda i, n=p.ndim: (sel(i),) + (0,) * (n - 1))


def _out_plumbing(outs, T, args, in_specs):
    shapes, specs, aliases = [], [], {}
    for k, o in enumerate(outs):
        if o[0] == 'new':
            _, rows, w, roff = o[:4]
            shapes.append(jax.ShapeDtypeStruct((rows, w), o[4] if len(o) > 4 else F32))
            specs.append(_xspec(T, w, 0, roff))
        elif o[0] == 'acc':
            shapes.append(jax.ShapeDtypeStruct(o[1], F32))
            specs.append(pl.BlockSpec(o[1], lambda i, n=len(o[1]): (0,) * n))
        elif o[0] == 'transposed':
            _, rows, w, dtype, _ = o
            shapes.append(jax.ShapeDtypeStruct((w, rows), dtype))
            specs.append(pl.BlockSpec((w, T), lambda i: (0, i)))
        elif o[0] == 'part':
            _, rows, wtot, w, col, roff, dtype = o
            shapes.append(jax.ShapeDtypeStruct((rows, wtot), dtype))
            specs.append(_xspec(T, w, col, roff))
        else:
            _, arr, w, col, roff = o
            aliases[len(args)] = k
            args.append(arr)
            in_specs.append(pl.BlockSpec(memory_space=pl.ANY))
            shapes.append(jax.ShapeDtypeStruct(arr.shape, arr.dtype))
            specs.append(_xspec(T, w, col, roff))
    return shapes, specs, aliases


def stage_fwd(name, f, T, n, xs, ps, outs):
    nx, npar = len(xs), len(ps)
    args = [x[0] for x in xs] + [p[0] for p in ps]
    in_specs = [_xspec(T, w, col, roff) for (_, w, col, roff) in xs] + [_pspec(p, sel) for (p, sel) in ps]
    n_in = len(args)
    shapes, specs, aliases = _out_plumbing(outs, T, args, in_specs)
    n_all_in = len(args)

    def body(*refs):
        i = pl.program_id(0)
        xv = [r[...] for r in refs[:nx]]
        pv = [r[...] if ps[k][1] is None else r[0] for k, r in enumerate(refs[nx:n_in])]
        res = f(*xv, *pv)
        for k, o_ref in enumerate(refs[n_all_in:]):
            if outs[k][0] == 'acc':
                @pl.when(i == 0)
                def _(o_ref=o_ref, v=res[k]):
                    o_ref[...] = v

                @pl.when(i > 0)
                def _(o_ref=o_ref, v=res[k]):
                    o_ref[...] += v
            elif outs[k][0] == 'transposed':
                o_ref[...] = res[outs[k][4]].T.astype(o_ref.dtype)
            else:
                o_ref[...] = res[k].astype(o_ref.dtype)

    return pl.pallas_call(body, name=name, grid=(n,), in_specs=in_specs, out_specs=specs, out_shape=shapes,
                          input_output_aliases=aliases, compiler_params=_cp("arbitrary"))(*args)


def stage_bwd(name, f, T, n, xs, ps, cts, dxs, dps, primal=()):
    nx, npar = len(xs), len(ps)
    args = [x[0] for x in xs] + [p[0] for p in ps]
    in_specs = [_xspec(T, w, col, roff) for (_, w, col, roff) in xs] + [_pspec(p, sel) for (p, sel) in ps]
    ct_arrs = [c for c in cts if isinstance(c, tuple)]
    for (a, w, col, roff) in ct_arrs:
        args.append(a)
        in_specs.append(_xspec(T, w, col, roff))
    n_in = len(args)
    outs, out_of = [], []
    for k, o in enumerate(dxs):
        if o is not None:
            outs.append(o)
            out_of.append(('x', k))
    for k, want in enumerate(dps):
        if want:
            p, sel = ps[k]
            outs.append(('acc', p.shape))
            out_of.append(('p', k))
    for k, shape in primal:
        outs.append(('acc', shape))
        out_of.append(('r', k))
    shapes, specs, aliases = _out_plumbing(outs, T, args, in_specs)
    for j, (kind, k) in enumerate(out_of):
        if kind == 'p' and ps[k][1] is not None:
            p, sel = ps[k]
            specs[j] = pl.BlockSpec((1,) + p.shape[1:], lambda i, n=p.ndim, sel=sel: (sel(i),) + (0,) * (n - 1))
    n_all_in = len(args)

    def body(*refs):
        i = pl.program_id(0)
        xv = [r[...] for r in refs[:nx]]
        pv = [r[...] if ps[k][1] is None else r[0] for k, r in enumerate(refs[nx:nx + npar])]
        res, vjp_fn = jax.vjp(f, *xv, *pv)
        ctv, q = [], nx + npar
        for k, c in enumerate(cts):
            if c is None:
                ctv.append(jnp.zeros_like(res[k]))
            elif isinstance(c, tuple):
                v = refs[q][...]
                if c[3] < 0:
                    v = v * (i + c[3] >= 0).astype(F32)
                ctv.append(v)
                q += 1
            else:
                ctv.append(jnp.full_like(res[k], c))
        grads = vjp_fn(tuple(ctv))
        for j, o_ref in enumerate(refs[n_all_in:]):
            kind, k = out_of[j]
            if kind == 'x':
                o_ref[...] = grads[k].astype(o_ref.dtype)
            else:
                g = res[k] if kind == 'r' else grads[nx + k]
                sel = None if kind == 'r' else ps[k][1]
                if sel is None:
                    first = i == 0
                    tgt = o_ref
                else:
                    first = jnp.logical_or(i == 0, sel(i) != sel(jnp.maximum(i - 1, 0)))
                    tgt = o_ref.at[0]

                @pl.when(first)
                def _(tgt=tgt, g=g):
                    tgt[...] = g

                @pl.when(jnp.logical_not(first))
                def _(tgt=tgt, g=g):
                    tgt[...] += g

    return pl.pallas_call(body, name=name, grid=(n,), in_specs=in_specs, out_specs=specs, out_shape=shapes,
                          input_output_aliases=aliases, compiler_params=_cp("arbitrary"))(*args)


_CONTRACT = {'nn': (1, 0), 'nt': (1, 1), 'tn': (0, 0)}


def matmul(name, a, b, mode, tm, tn, tk, out_dtype=F32, add=None):
    if mode == 'nn':
        (M, K), (_, N) = a.shape, b.shape
    elif mode == 'nt':
        (M, K), (N, _) = a.shape, b.shape
    else:
        (K, M), (_, N) = a.shape, b.shape
    assert M % tm == 0 and N % tn == 0 and K % tk == 0, (name, M, N, K, tm, tn, tk)
    a_spec = (pl.BlockSpec((tk, tm), lambda j, i, k: (k, i)) if mode == 'tn'
              else pl.BlockSpec((tm, tk), lambda j, i, k: (i, k)))
    b_spec = (pl.BlockSpec((tn, tk), lambda j, i, k: (j, k)) if mode == 'nt'
              else pl.BlockSpec((tk, tn), lambda j, i, k: (k, j)))
    o_spec = pl.BlockSpec((tm, tn), lambda j, i, k: (i, j))
    return matmul_call(name, (N // tn, M // tm, K // tk), a, a_spec, b, b_spec, (M, N), o_spec, (tm, tn), mode,
                       out_dtype, add)


def matmul_call(name, grid, a, a_spec, b, b_spec, out_shape, o_spec, tile, mode, out_dtype=F32, add=None):
    tm, tn = tile
    nk = grid[2]
    ca, cb = _CONTRACT[mode]
    args, in_specs = [a, b], [a_spec, b_spec]
    if add is not None:
        args.append(add)
        in_specs.append(o_spec)

    def body(*refs):
        a_ref, b_ref = refs[0], refs[1]
        o_ref, acc = refs[-2], refs[-1]
        k = pl.program_id(2)
        if nk == 1:
            p = _dot(a_ref[...], b_ref[...], ca, cb)
            o_ref[...] = (p + refs[2][...] if add is not None else p).astype(out_dtype)
            return

        @pl.when(k == 0)
        def _():
            acc[...] = refs[2][...] if add is not None else jnp.zeros_like(acc)

        acc[...] += _dot(a_ref[...], b_ref[...], ca, cb)

        @pl.when(k == nk - 1)
        def _():
            o_ref[...] = acc[...].astype(out_dtype)

    return pl.pallas_call(body, name=name, grid=grid, in_specs=in_specs, out_specs=o_spec,
                          out_shape=jax.ShapeDtypeStruct(out_shape, out_dtype),
                          scratch_shapes=[pltpu.VMEM((tm, tn) if nk > 1 else (8, 128), F32)],
                          compiler_params=_cp("arbitrary", "arbitrary", "arbitrary"))(*args)


NS, WS = 4, 704


def _resident(name, M, tm, rows, weight, out_shape, out_block, out_map, step, add=None):
    args = [rows[0], weight] + ([] if add is None else [add])
    in_specs = [pl.BlockSpec(rows[1], rows[2]), pl.BlockSpec(weight.shape, lambda i, n=weight.ndim: (0,) * n)]
    if add is not None:
        in_specs.append(pl.BlockSpec(out_block, out_map))
    return pl.pallas_call(step, name=name, grid=(M // tm,), in_specs=in_specs, out_specs=pl.BlockSpec(out_block, out_map),
                          out_shape=jax.ShapeDtypeStruct(out_shape, F32), compiler_params=_cp("arbitrary"))(*args)


def ffn_in_fwd(name, h, w1, w3, tm):
    M = h.shape[0]

    def step(h_ref, w1_ref, w3_ref, a1_ref, a3_ref, act_ref):
        for s in range(NS):
            a1 = _dot(h_ref[...], w1_ref[s], 1, 0)
            a3 = _dot(h_ref[...], w3_ref[s], 1, 0)
            a1_ref[s] = a1.astype(a1_ref.dtype)
            a3_ref[s] = a3.astype(a3_ref.dtype)
            act_ref[s] = (_silu(a1) * a3).astype(act_ref.dtype)

    wspec = pl.BlockSpec((NS, D, WS), lambda i: (0, 0, 0))
    ospec = pl.BlockSpec((NS, tm, WS), lambda i: (0, i, 0))
    return pl.pallas_call(
        step, name=name, grid=(M // tm,), in_specs=[pl.BlockSpec((tm, D), lambda i: (i, 0)), wspec, wspec],
        out_specs=[ospec, ospec, ospec],
        out_shape=[jax.ShapeDtypeStruct((NS, M, WS), BF16)] * 3, compiler_params=_cp("arbitrary"))(h, w1, w3)


def ffn_out_bwd_x(name, dff, w2, a1, a3, tm):
    M = dff.shape[0]

    def step(d_ref, w_ref, a1_ref, a3_ref, da1_ref, da3_ref):
        for s in range(NS):
            dact = _dot(d_ref[...], w_ref[s * WS:(s + 1) * WS, :], 1, 1)
            a1 = a1_ref[s].astype(F32)
            sig = jax.nn.sigmoid(a1)
            da3_ref[s] = (dact * (a1 * sig)).astype(da3_ref.dtype)
            da1_ref[s] = (dact * a3_ref[s].astype(F32) * (sig * (1.0 + a1 * (1.0 - sig)))).astype(da1_ref.dtype)

    aspec = pl.BlockSpec((NS, tm, WS), lambda i: (0, i, 0))
    return pl.pallas_call(
        step, name=name, grid=(M // tm,),
        in_specs=[pl.BlockSpec((tm, D), lambda i: (i, 0)), pl.BlockSpec(w2.shape, lambda i: (0, 0)), aspec, aspec],
        out_specs=[aspec, aspec],
        out_shape=[jax.ShapeDtypeStruct((NS, M, WS), BF16)] * 2, compiler_params=_cp("arbitrary"))(dff, w2, a1, a3)


def ff_in_bwd_x(name, da3, w3, tm, add=None):
    M = da3.shape[1]

    def step(*refs):
        d_ref, w_ref, o_ref = refs[0], refs[1], refs[-1]
        acc = _dot(d_ref[0], w_ref[0], 1, 1)
        for s in range(1, NS):
            acc = acc + _dot(d_ref[s], w_ref[s], 1, 1)
        o_ref[...] = acc if add is None else acc + refs[2][...]

    return _resident(name, M, tm, (da3, (NS, tm, WS), lambda i: (0, i, 0)), w3, (M, D), (tm, D), lambda i: (i, 0), step, add)


def ff_in_bwd_w(name, hT, da3, tk):
    M = hT.shape[1]
    return matmul_call(name, (NS, 1, M // tk), hT, pl.BlockSpec((D, tk), lambda j, i, k: (0, k)),
                       da3, pl.BlockSpec((None, tk, WS), lambda j, i, k: (j, k, 0)),
                       (NS, D, WS), pl.BlockSpec((None, D, WS), lambda j, i, k: (j, 0, 0)), (D, WS), 'nn', BF16)


def ff_out_fwd(name, act3, w2, tm):
    M = act3.shape[1]

    def step(a_ref, w_ref, o_ref):
        acc = _dot(a_ref[0], w_ref[0:WS, :], 1, 0)
        for s in range(1, NS):
            acc = acc + _dot(a_ref[s], w_ref[s * WS:(s + 1) * WS, :], 1, 0)
        o_ref[...] = acc

    return _resident(name, M, tm, (act3, (NS, tm, WS), lambda i: (0, i, 0)), w2, (M, D), (tm, D), lambda i: (i, 0), step)


def ff_out_bwd_w(name, act3, dff, tk):
    M = dff.shape[0]
    return matmul_call(name, (1, NS, M // tk), act3, pl.BlockSpec((None, tk, WS), lambda j, i, k: (i, k, 0)),
                       dff, pl.BlockSpec((tk, D), lambda j, i, k: (k, 0)),
                       (NS * WS, D), pl.BlockSpec((WS, D), lambda j, i, k: (i, 0)), (WS, D), 'tn', BF16)


def _shift_rows(x, d):
    n = x.shape[0]
    if d == 0:
        return x
    y = pltpu.roll(x, (-d) % n, 0)
    t = lax.broadcasted_iota(jnp.int32, x.shape, 0)
    ok = (t + d >= 0) & (t + d < n)
    return jnp.where(ok, y, 0.0)


def _conv_pre(x, w_ref, b_ref):
    acc = jnp.broadcast_to(b_ref[...], x.shape)
    for k in range(5):
        acc = acc + _shift_rows(x, k - 2) * w_ref[k:k + 1, :]
    return acc


def conv_fwd(name, proj, conv_w, conv_b, R):
    segs = ((0, LC), (LC, R))

    def body(x_ref, w_ref, b_ref, o_ref):
        for (s, e) in segs:
            pre = _conv_pre(x_ref[s:e, :], w_ref, b_ref)
            o_ref[s:e, :] = _silu(pre)

    return pl.pallas_call(
        body, name=name, grid=(12,),
        in_specs=[pl.BlockSpec((R, 128), lambda j: (0, C_XBC // 128 + j)),
                  pl.BlockSpec((8, 128), lambda j: (0, j)), pl.BlockSpec((1, 128), lambda j: (0, j))],
        out_specs=pl.BlockSpec((R, 128), lambda j: (0, j)),
        out_shape=jax.ShapeDtypeStruct((R, 1536), F32), compiler_params=_cp("arbitrary"))(proj, conv_w, conv_b)


def conv_bwd(name, proj, conv_w, conv_b, d_f, d_b, d_skip, dproj, R):
    segs = ((0, LC), (LC, R))

    def body(x_ref, w_ref, b_ref, df_ref, db_ref, ds_ref, _, dx_ref, dw_ref, dbias_ref):
        j = pl.program_id(0)
        has_skip = (j < 8).astype(F32)
        dw = [jnp.zeros((1, 128), F32) for _ in range(5)]
        dbias = jnp.zeros((1, 128), F32)
        for (s, e) in segs:
            x = x_ref[s:e, :]
            pre = _conv_pre(x, w_ref, b_ref)
            sig = jax.nn.sigmoid(pre)
            dy = df_ref[s:e, :] + db_ref[s:e, :]
            if s == LC:
                dy = dy + ds_ref[...] * has_skip
            dpre = dy * (sig * (1.0 + pre * (1.0 - sig)))
            dx = jnp.zeros_like(x)
            for k in range(5):
                dx = dx + _shift_rows(dpre, 2 - k) * w_ref[k:k + 1, :]
                dw[k] = dw[k] + jnp.sum(dpre * _shift_rows(x, k - 2), axis=0, keepdims=True)
            dbias = dbias + jnp.sum(dpre, axis=0, keepdims=True)
            dx_ref[s:e, :] = dx.astype(dx_ref.dtype)
        dw_ref[...] = jnp.zeros_like(dw_ref)
        for k in range(5):
            dw_ref[k:k + 1, :] = dw[k]
        dbias_ref[...] = dbias

    return pl.pallas_call(
        body, name=name, grid=(12,),
        in_specs=[pl.BlockSpec((R, 128), lambda j: (0, C_XBC // 128 + j)),
                  pl.BlockSpec((8, 128), lambda j: (0, j)), pl.BlockSpec((1, 128), lambda j: (0, j)),
                  pl.BlockSpec((R, 128), lambda j: (0, j)), pl.BlockSpec((R, 128), lambda j: (0, j)),
                  pl.BlockSpec((R - LC, 128), lambda j: (0, jnp.minimum(j, 7))),
                  pl.BlockSpec(memory_space=pl.ANY)],
        out_specs=[pl.BlockSpec((R, 128), lambda j: (0, C_XBC // 128 + j)),
                   pl.BlockSpec((8, 128), lambda j: (0, j)), pl.BlockSpec((1, 128), lambda j: (0, j))],
        out_shape=[jax.ShapeDtypeStruct(dproj.shape, dproj.dtype), jax.ShapeDtypeStruct((8, 1536), F32),
                   jax.ShapeDtypeStruct((1, 1536), F32)],
        input_output_aliases={6: 0}, compiler_params=_cp("arbitrary"))(proj, conv_w, conv_b, d_f, d_b, d_skip, dproj)


def _ssd_chunk(rev, dirn):
    cums = _cumsum_fn(rev)

    def f(xs, Bs, Cs, dt, alog, Hs):
        lane = lax.broadcasted_iota(jnp.int32, (1, 128), 1)
        sub = lax.broadcasted_iota(jnp.int32, (Q, 1), 0)
        r = lax.broadcasted_iota(jnp.int32, (Q, Q), 0)
        c = lax.broadcasted_iota(jnp.int32, (Q, Q), 1)
        mask = (r <= c) if rev else (r >= c)
        left = lane < 64
        a = dt * (-jnp.exp(alog))
        s = cums(a)
        sT, dtT = s.T, dt.T
        last_row = (sub == (0 if rev else Q - 1)).astype(F32)
        s_last = jnp.sum(s * last_row, axis=0, keepdims=True)
        G = [mm_nt(Cs[g], Bs[g]) for g in range(2)]
        M, es, wc, ed = [], [], [], []
        for h in range(NH):
            l = 16 * dirn + h
            oh_l = (lane == l).astype(F32)
            oh_s = (sub == l).astype(F32)
            s_col = jnp.sum(s * oh_l, axis=1, keepdims=True)
            dt_col = jnp.sum(dt * oh_l, axis=1, keepdims=True)
            s_row = jnp.sum(sT * oh_s, axis=0, keepdims=True)
            dt_row = jnp.sum(dtT * oh_s, axis=0, keepdims=True)
            sl = jnp.sum(s_last * oh_l, axis=1, keepdims=True)
            seg = jnp.where(mask, s_col - s_row, 0.0)
            lm = jnp.where(mask, jnp.exp(seg), 0.0)
            M.append(G[h // 8] * lm * dt_row)
            es.append(jnp.exp(s_col))
            wc.append(jnp.exp(sl - s_col) * dt_col)
            ed.append(jnp.exp(sl))
        Ys, Hn = [], []
        for j in range(8):
            g = j // 4
            xa = jnp.where(left, xs[j], 0.0)
            xb = jnp.where(left, 0.0, xs[j])
            yd = mm(M[2 * j], xa) + mm(M[2 * j + 1], xb)
            yo = mm(Cs[g], Hs[j]) * jnp.where(left, es[2 * j], es[2 * j + 1])
            Ys.append(yd + yo)
            st = mm_tn(Bs[g], xs[j] * jnp.where(left, wc[2 * j], wc[2 * j + 1]))
            Hn.append(Hs[j] * jnp.where(left, ed[2 * j], ed[2 * j + 1]) + st)
        return Ys, Hn

    return f


def _chunk_of(t, n, rev):
    if not rev:
        return t
    return jnp.where(t < 2, 1 - t, n + 1 - t)


def ssd_fwd(name, xbc, dt, alog, n, rev, dirn):
    chunk = _ssd_chunk(rev, dirn)

    def body(x_ref, b_ref, c_ref, dt_ref, al_ref, y_ref, hs_ref, h_scr):
        @pl.when(pl.program_id(0) == 0)
        def _():
            h_scr[...] = jnp.zeros_like(h_scr)

        xs = [x_ref[:, 128 * j:128 * (j + 1)] for j in range(8)]
        Bs = [b_ref[:, 128 * g:128 * (g + 1)] for g in range(2)]
        Cs = [c_ref[:, 128 * g:128 * (g + 1)] for g in range(2)]
        Hs = [h_scr[:, 128 * j:128 * (j + 1)] for j in range(8)]
        hs_ref[0] = h_scr[...]
        Ys, Hn = chunk(xs, Bs, Cs, dt_ref[...], al_ref[...], Hs)
        for j in range(8):
            y_ref[:, 128 * j:128 * (j + 1)] = Ys[j]
            h_scr[:, 128 * j:128 * (j + 1)] = Hn[j]

    cm = lambda t: _chunk_of(t, n, rev)
    return pl.pallas_call(
        body, name=name, grid=(n,),
        in_specs=[pl.BlockSpec((Q, 1024), lambda t: (cm(t), 0)), pl.BlockSpec((Q, 256), lambda t: (cm(t), 4)),
                  pl.BlockSpec((Q, 256), lambda t: (cm(t), 5)), pl.BlockSpec((Q, 128), lambda t: (cm(t), 0)),
                  pl.BlockSpec((1, 128), lambda t: (0, 0))],
        out_specs=[pl.BlockSpec((Q, 1024), lambda t: (cm(t), 0)), pl.BlockSpec((1, Q, 1024), lambda t: (cm(t), 0, 0))],
        out_shape=[jax.ShapeDtypeStruct((n * Q, 1024), F32), jax.ShapeDtypeStruct((n, Q, 1024), F32)],
        scratch_shapes=[pltpu.VMEM((Q, 1024), F32)], compiler_params=_cp("arbitrary"))(xbc, xbc, xbc, dt, alog)


def ssd_bwd(name, xbc, dt, alog, hs, dy, n, rev, dirn):
    chunk = _ssd_chunk(rev, dirn)

    def body(x_ref, b_ref, c_ref, dt_ref, al_ref, hs_ref, dy_ref, dx_ref, ddt_ref, dal_ref, dh_scr):
        tt = pl.program_id(0)
        ch = _chunk_of(n - 1 - tt, n, rev)

        @pl.when(tt == 0)
        def _():
            dh_scr[...] = jnp.zeros_like(dh_scr)

        xs = [x_ref[:, 128 * j:128 * (j + 1)] for j in range(8)]
        Bs = [b_ref[:, 128 * g:128 * (g + 1)] for g in range(2)]
        Cs = [c_ref[:, 128 * g:128 * (g + 1)] for g in range(2)]
        Hs = [hs_ref[0, :, 128 * j:128 * (j + 1)] for j in range(8)]
        live = (ch >= 2).astype(F32)
        dYs = [dy_ref[:, 128 * j:128 * (j + 1)] * live for j in range(8)]
        dHn = [dh_scr[:, 128 * j:128 * (j + 1)] for j in range(8)]
        _, vjp_fn = jax.vjp(chunk, xs, Bs, Cs, dt_ref[...], al_ref[...], Hs)
        dxs, dBs, dCs, ddt, dal, dHs = vjp_fn((dYs, dHn))
        for j in range(8):
            dx_ref[:, 128 * j:128 * (j + 1)] = dxs[j]
            dh_scr[:, 128 * j:128 * (j + 1)] = dHs[j]
        for g in range(2):
            dx_ref[:, 1024 + 128 * g:1024 + 128 * (g + 1)] = dBs[g]
            dx_ref[:, 1280 + 128 * g:1280 + 128 * (g + 1)] = dCs[g]
        ddt_ref[...] = ddt

        @pl.when(tt == 0)
        def _():
            dal_ref[...] = dal

        @pl.when(tt > 0)
        def _():
            dal_ref[...] += dal

    cm = lambda t: _chunk_of(n - 1 - t, n, rev)
    return pl.pallas_call(
        body, name=name, grid=(n,),
        in_specs=[pl.BlockSpec((Q, 1024), lambda t: (cm(t), 0)), pl.BlockSpec((Q, 256), lambda t: (cm(t), 4)),
                  pl.BlockSpec((Q, 256), lambda t: (cm(t), 5)), pl.BlockSpec((Q, 128), lambda t: (cm(t), 0)),
                  pl.BlockSpec((1, 128), lambda t: (0, 0)), pl.BlockSpec((1, Q, 1024), lambda t: (cm(t), 0, 0)),
                  pl.BlockSpec((Q, 1024), lambda t: (jnp.maximum(cm(t) - 2, 0), 0))],
        out_specs=[pl.BlockSpec((Q, 1536), lambda t: (cm(t), 0)), pl.BlockSpec((Q, 128), lambda t: (cm(t), 0)),
                   pl.BlockSpec((1, 128), lambda t: (0, 0))],
        out_shape=[jax.ShapeDtypeStruct((n * Q, 1536), F32), jax.ShapeDtypeStruct((n * Q, 128), F32),
                   jax.ShapeDtypeStruct((1, 128), F32)],
        scratch_shapes=[pltpu.VMEM((Q, 1024), F32)], compiler_params=_cp("arbitrary"))(xbc, xbc, xbc, dt, alog, hs, dy)


def f_norm0(x, g0, b0, sc, sh):
    x0 = _ln(x, g0, b0)
    return x0, x0 * (1.0 + sc) + sh


def f_dt(raw, bias):
    z = split4(raw)[0] + bias
    dt = jnp.maximum(z, 0.0) + jnp.log1p(jnp.exp(-jnp.abs(z)))
    return dt, dt


def f_gated_norm(yf, yb, xs, z, dcol, g):
    h = (yf + yb + xs * dcol) * _silu(z)
    return (h * lax.rsqrt(jnp.mean(h * h, axis=-1, keepdims=True) + LN_EPS) * g,)


def f_gmlp(uv, gmg, gmb, *wb):
    ws, bs = wb[:8], wb[8:]
    u, v = split2(uv)
    vn = split8(_ln(_gelu(v), gmg, gmb))
    mixed = concat8(tuple(mm(ws[g], vn[g]) + bs[g] for g in range(8)))
    return (_gelu(u) * mixed,)


def f_merge(ps, pg, gates, bg):
    gs, gg = split2(jax.nn.sigmoid(gates + bg))
    return (gs * ps + gg * pg,)


def f_res1(x0, out, g1, lg, lb, sc, sh):
    x1 = _ln(ALPHA * x0 + g1 * out, lg, lb)
    return x1, x1 * (1.0 + sc) + sh


def f_res2_loss(x1, ff, tgt, g2, lg, lb):
    x2 = _ln(ALPHA * x1 + g2 * ff, lg, lb)
    e = x2 - tgt
    return (0.5 * jnp.sum(jnp.mean(e * e, axis=-1, keepdims=True), axis=0, keepdims=True),)


def _row_tile(M):
    return 544 if M % 544 == 0 else (512 if M % 512 == 0 else M)


def core(x_all, tgt, mod_x, mod_c, X, S):
    R = x_all.shape[0]
    L = R - LC
    n = R // Q
    T = 256
    nt, ntl = R // T, L // T
    tmR, tmL = _row_tile(R), _row_tile(L)
    tkR = 256 if R % 512 else 512
    tkL = 512 if L % 512 == 0 else 256
    row = lambda v: v.reshape(1, -1)
    mx = [row(mod_x[k]) for k in range(6)]
    mc = [row(mod_c[k]) for k in range(6)]
    sel = lambda i: jnp.minimum(i, 1)
    sc1 = jnp.stack([mc[1], mx[1]])
    sh1 = jnp.stack([mc[0], mx[0]])
    ln0 = [(row(S['ln0_g']), None), (row(S['ln0_b']), None), (sc1, sel), (sh1, sel)]

    x0, xm, xmT = stage_fwd("norm0_fwd", f_norm0, T, nt, [(x_all, D, 0, 0)], ln0,
                            [('new', R, D, 0), ('new', R, D, 0, BF16), ('transposed', R, D, BF16, 1)])
    w_in = X.w_in()
    proj = matmul("proj_fwd", xm, w_in, 'nn', tmR, PW // 2, 1024)
    conv_w8 = jnp.pad(S['conv_w'], ((0, 3), (0, 0)))
    conv_b = row(S['conv_b'])
    xbc = conv_fwd("conv_fwd", proj, conv_w8, conv_b, R)
    dt_bias = jnp.pad(S['dt_bias'].reshape(1, 32), ((0, 0), (0, 96)))
    alog = jnp.pad(S['a_log'].reshape(1, 32), ((0, 0), (0, 96)))
    x_dt = [(proj, 512, C_DT // 512, 0)]
    dt_f, dt_b = stage_fwd("dt_fwd", f_dt, T, nt, x_dt, [(dt_bias, None)], [('new', R, 128, 0), ('new', R, 128, 0)])
    y_f, hs_f = ssd_fwd("ssd_fwd_f", xbc, dt_f, alog, n, False, 0)
    y_b, hs_b = ssd_fwd("ssd_fwd_b", xbc, dt_b, alog, n, True, 1)
    W = X.rest()
    dcol = jnp.repeat(S['d_skip'][0] + S['d_skip'][1], 64).reshape(1, D)
    x_gn = [(y_f, D, 0, 1), (y_b, D, 0, 1), (xbc, D, 0, 1), (proj, D, C_Z // D, 1)]
    p_gn = [(dcol, None), (row(S['ssd_norm_g']), None)]
    (yn,) = stage_fwd("gnorm_fwd", f_gated_norm, T, ntl, x_gn, p_gn, [('new', L, D, 0, BF16)])
    x_gm = [(proj, 2 * D, C_UV // (2 * D), LC // Q)]
    p_gm = ([(row(S['gm_norm_g']), None), (row(S['gm_norm_b']), None)]
            + [(S['w_spatial'][g], None) for g in range(8)] + [(S['b_spatial'][g].reshape(Q, 1), None) for g in range(8)])
    (y_gm,) = stage_fwd("gmlp_fwd", f_gmlp, Q, L // Q, x_gm, p_gm, [('new', L, D, 0, BF16)])
    p_ssd = matmul("pssd_fwd", yn, W['w_ssd_proj'], 'nn', tmL, 1024, 1024)
    p_g = matmul("pgm_fwd", y_gm, W['w_gm_proj'], 'nn', tmL, 1024, 1024)
    x_mg = [(p_ssd, D, 0, 0), (p_g, D, 0, 0), (proj, 2 * D, C_GATE // (2 * D), 1)]
    p_mg = [(row(S['b_gate']), None)]
    (merged,) = stage_fwd("merge_fwd", f_merge, T, ntl, x_mg, p_mg, [('new', L, D, 0, BF16)])
    out = matmul("out_fwd", merged, W['w_out'], 'nn', tmL, 1024, 1024)
    x_r1 = [(x0, D, 0, 1), (out, D, 0, 0)]
    p_r1 = [(mx[2], None), (row(S['ln1_g']), None), (row(S['ln1_b']), None), (mx[4], None), (mx[3], None)]
    x1, hm, hmT = stage_fwd("res1_fwd", f_res1, T, ntl, x_r1, p_r1,
                            [('new', L, D, 0), ('new', L, D, 0, BF16), ('transposed', L, D, BF16, 1)])
    a1, a3, act = ffn_in_fwd("ffn_in_fwd", hm, W['w_ff1'], W['w_ff3'], T)
    ff = ff_out_fwd("ff2_fwd", act, W['w_ff2'], tmL)
    x_r2 = [(x1, D, 0, 0), (ff, D, 0, 0), (tgt, D, 0, 0)]
    p_r2 = [(mx[5], None), (row(S['ln2_g']), None), (row(S['ln2_b']), None)]

    dx1_a, dff, dg2, dl2g, dl2b, loss = stage_bwd(
        "res2_bwd", f_res2_loss, T, ntl, x_r2, p_r2, [1.0],
        [('new', L, D, 0), ('new', L, D, 0, BF16), None], [True, True, True], primal=[(0, (1, 1))])
    da1, da3 = ffn_out_bwd_x("ffn_out_bwd_x", dff, W['w_ff2'], a1, a3, T)
    gw_ff2 = ff_out_bwd_w("ff2_bwd_w", act, dff, tkL)
    dhm = ff_in_bwd_x("ff1_bwd_x", da1, W['w_ff1'], tmL)
    dhm = ff_in_bwd_x("ff3_bwd_x", da3, W['w_ff3'], tmL, add=dhm)
    gw_ff1 = ff_in_bwd_w("ff1_bwd_w", hmT, da1, tkL)
    gw_ff3 = ff_in_bwd_w("ff3_bwd_w", hmT, da3, tkL)
    dhm, (gw_ff2, gw_ff1, gw_ff3) = lax.optimization_barrier((dhm, (gw_ff2, gw_ff1, gw_ff3)))
    X.grads('ffn', {'w_ff2': gw_ff2, 'w_ff1': gw_ff1, 'w_ff3': gw_ff3})
    dx0_a, dout, dg1, dl1g, dl1b, dsc2, dsh2 = stage_bwd(
        "res1_bwd", f_res1, T, ntl, x_r1, p_r1, [(dx1_a, D, 0, 0), (dhm, D, 0, 0)],
        [('new', L, D, 0), ('new', L, D, 0, BF16)], [True] * 5)
    dmerged = matmul("out_bwd_x", dout, W['w_out'], 'nt', tmL, 1024, 1024)
    gw_out = matmul("out_bwd_w", merged, dout, 'tn', 1024, 1024, tkL, BF16)
    lt, lq = -(LC // T), -(LC // Q)
    x_mg_b = [(p_ssd, D, 0, lt), (p_g, D, 0, lt), (proj, 2 * D, C_GATE // (2 * D), 0)]
    dp_ssd, dp_g, dproj, dbg = stage_bwd(
        "merge_bwd", f_merge, T, nt, x_mg_b, p_mg, [(dmerged, D, 0, lt)],
        [('new', L, D, lt, BF16), ('new', L, D, lt, BF16), ('part', R, PW, 2 * D, C_GATE // (2 * D), 0, BF16)], [True])
    dyn = matmul("pssd_bwd_x", dp_ssd, W['w_ssd_proj'], 'nt', tmL, 1024, 1024)
    gw_ssd = matmul("pssd_bwd_w", yn, dp_ssd, 'tn', 1024, 1024, tkL, BF16)
    dy_gm = matmul("pgm_bwd_x", dp_g, W['w_gm_proj'], 'nt', tmL, 1024, 1024)
    gw_gm = matmul("pgm_bwd_w", y_gm, dp_g, 'tn', 1024, 1024, tkL, BF16)
    dy_gm, (gw_out, gw_ssd, gw_gm) = lax.optimization_barrier((dy_gm, (gw_out, gw_ssd, gw_gm)))
    X.grads('proj', {'w_out': gw_out, 'w_ssd_proj': gw_ssd, 'w_gm_proj': gw_gm})
    r_gm = stage_bwd("gmlp_bwd", f_gmlp, Q, n, [(proj, 2 * D, C_UV // (2 * D), 0)], p_gm, [(dy_gm, D, 0, lq)],
                     [('alias', dproj, 2 * D, C_UV // (2 * D), 0)], [True] * 18)
    dproj, dgmg, dgmb, dws, dbs = r_gm[0], r_gm[1], r_gm[2], r_gm[3:11], r_gm[11:19]
    x_gn_b = [(y_f, D, 0, 0), (y_b, D, 0, 0), (xbc, D, 0, 0), (proj, D, C_Z // D, 0)]
    dy, dskipx, dproj, ddcol, dng = stage_bwd(
        "gnorm_bwd", f_gated_norm, T, nt, x_gn_b, p_gn, [(dyn, D, 0, lt)],
        [('new', L, D, lt), None, ('new', L, D, lt), ('alias', dproj, D, C_Z // D, 0)], [True, True])
    dxbc_f, ddt_f, dal_f = ssd_bwd("ssd_bwd_f", xbc, dt_f, alog, hs_f, dy, n, False, 0)
    dxbc_b, ddt_b, dal_b = ssd_bwd("ssd_bwd_b", xbc, dt_b, alog, hs_b, dy, n, True, 1)
    dproj, ddtb = stage_bwd("dt_bwd", f_dt, T, nt, x_dt, [(dt_bias, None)],
                            [(ddt_f, 128, 0, 0), (ddt_b, 128, 0, 0)],
                            [('alias', dproj, 512, C_DT // 512, 0)], [True])
    dproj, dcw8, dcb = conv_bwd("conv_bwd", proj, conv_w8, conv_b, dxbc_f, dxbc_b, dskipx, dproj, R)
    gw_in = matmul("proj_bwd_w", xmT, dproj, 'nn', 1024, PW // 4, tkR, BF16)
    dproj, gw_in = lax.optimization_barrier((dproj, gw_in))
    X.grads('in', {'w_in': gw_in})
    dxm = matmul("proj_bwd_x", dproj, w_in, 'nt', R // 4 if R % 32 == 0 else R, 1024, 1024)
    grad_x, dl0g, dl0b, dsc1, dsh1 = stage_bwd(
        "norm0_bwd", f_norm0, T, nt, [(x_all, D, 0, 0)], ln0, [(dx0_a, D, 0, -1), (dxm, D, 0, 0)],
        [('new', L, D, -1)], [True] * 4)

    zero = jnp.zeros((D,), F32)
    flat = lambda v: v.reshape(-1)
    small = {
        'loss': flat(loss), 'ln0_g': flat(dl0g), 'ln0_b': flat(dl0b),
        'dmod_x': jnp.concatenate([flat(dsh1[1]), flat(dsc1[1]), flat(dg1), flat(dsh2), flat(dsc2), flat(dg2)]),
        'dmod_c': jnp.concatenate([flat(dsh1[0]), flat(dsc1[0]), zero, zero, zero, zero]),
        'conv_w': flat(dcw8[:5]), 'conv_b': flat(dcb), 'dt_bias': flat(ddtb[:, :32]),
        'a_log': flat((dal_f + dal_b)[:, :32]),
        'd_skip': flat(jnp.tile(ddcol.reshape(1, NH, 64).sum(-1), (2, 1))),
        'ssd_norm_g': flat(dng), 'gm_norm_g': flat(dgmg), 'gm_norm_b': flat(dgmb),
        'w_spatial': flat(jnp.stack(dws)), 'b_spatial': flat(jnp.stack(dbs)), 'b_gate': flat(dbg),
        'ln1_g': flat(dl1g), 'ln1_b': flat(dl1b), 'ln2_g': flat(dl2g), 'ln2_b': flat(dl2b),
    }
    return grad_x, small


def _place():
    return lax.axis_index("x"), lax.axis_index("y"), lax.axis_index("c")


def allgather8(name, blk, hbm):
    space = pl.ANY if hbm else pltpu.VMEM

    def body(x_ref, out_ref, send_sems, recv_sems, local_sem):
        x, y, c = _place()
        me, sibling = (x, y, c), (x, y, 1 - c)
        chips = [(1 - x, y), (x, 1 - y), (1 - x, 1 - y)]

        def slot(px, py, pc):
            return out_ref.at[4 * px + 2 * py + pc]

        def copy(k, block, to, src=None):
            return pltpu.make_async_remote_copy(
                src_ref=slot(*block) if src is None else src, dst_ref=slot(*block),
                send_sem=send_sems.at[k], recv_sem=recv_sems.at[k], device_id=to, device_id_type=MESH)

        mine = pltpu.make_async_copy(x_ref, slot(*me), local_sem)
        mine.start()
        first = [copy(0, me, sibling, src=x_ref)]
        first += [copy(1 + j, me, (*chip, c), src=x_ref) for j, chip in enumerate(chips)]
        for cp in first:
            cp.start()
        passed = [copy(4 + j, (*chip, c), sibling) for j, chip in enumerate(chips)]
        for j, chip in enumerate(chips):
            copy(1 + j, (*chip, c), me).wait_recv()
            passed[j].start()
        copy(0, sibling, me).wait_recv()
        for j, chip in enumerate(chips):
            copy(4 + j, (*chip, 1 - c), me).wait_recv()
        for cp in first + passed:
            cp.wait_send()
        mine.wait()

    return pl.pallas_call(
        body, name=name, out_shape=jax.ShapeDtypeStruct((8,) + blk.shape, blk.dtype),
        in_specs=[pl.BlockSpec(memory_space=space)], out_specs=pl.BlockSpec(memory_space=space),
        scratch_shapes=[pltpu.SemaphoreType.DMA((7,)), pltpu.SemaphoreType.DMA((7,)), pltpu.SemaphoreType.DMA],
        compiler_params=pltpu.CompilerParams(vmem_limit_bytes=VMEM_LIMIT_V7X))(blk)


def _peers(place):
    x, y, c = place
    return [((1 - x) if k & 4 else x, (1 - y) if k & 2 else y, (1 - c) if k & 1 else c) for k in range(1, 8)]


def _slot(p):
    return 4 * p[0] + 2 * p[1] + p[2]


def plan_gather(place, srcs, lands):
    remote = [(s, l.at[_slot(place)], to) for s, l in zip(srcs, lands) for to in _peers(place)]
    return remote, [(s, l.at[_slot(place)]) for s, l in zip(srcs, lands)]


def plan_to_owner(place, srcs, lands):
    remote = [(s.at[2 * to[0] + to[1], to[2]], l.at[_slot(place)], to) for s, l in zip(srcs, lands) for to in _peers(place)]
    x, y, c = place
    return remote, [(s.at[2 * x + y, c], l.at[_slot(place)]) for s, l in zip(srcs, lands)]


def sequencer_exchange(name, collective_id, srcs, land_shapes, plan):
    n = len(srcs)
    src_refs = [jax.new_ref(a, memory_space=pltpu.MemorySpace.HBM) for a in srcs]
    land_refs = [jax.empty_ref(s, memory_space=pltpu.MemorySpace.HBM) for s in land_shapes]

    @pl.kernel(mesh=plsc.ScalarSubcoreMesh(axis_name="sequencer", num_cores=1), name=name,
               scratch_types=(pltpu.SemaphoreType.DMA((7 * n,)), pltpu.SemaphoreType.DMA((7 * n,)),
                              pltpu.SemaphoreType.DMA((n,))),
               compiler_params=pltpu.CompilerParams(collective_id=collective_id))
    def launch(send_sems, recv_sems, local_sems):
        place = _place()
        barrier = pltpu.get_barrier_semaphore()
        for to in _peers(place):
            pl.semaphore_signal(barrier, inc=1, device_id=to, device_id_type=MESH)
        pl.semaphore_wait(barrier, 7)
        remote, local = plan(place, src_refs, land_refs)
        mine = [pltpu.make_async_copy(s, d, local_sems.at[a]) for a, (s, d) in enumerate(local)]
        for cp in mine:
            cp.start()
        cps = [pltpu.make_async_remote_copy(src_ref=s, dst_ref=d, send_sem=send_sems.at[k], recv_sem=recv_sems.at[k],
                                            device_id=to, device_id_type=MESH) for k, (s, d, to) in enumerate(remote)]
        for cp in cps:
            cp.start()
        for cp in mine:
            cp.wait()
        for cp in cps:
            cp.wait()

    launch()
    return land_refs


def sibling_pair(name, hs):
    n = len(hs)

    def body(*refs):
        ins, outs = refs[:n], refs[n:2 * n]
        send_sems, recv_sems = refs[2 * n:]
        x, y, c = _place()
        cps = [pltpu.make_async_remote_copy(src_ref=outs[a].at[c], dst_ref=outs[a].at[c], send_sem=send_sems.at[a],
                                            recv_sem=recv_sems.at[a], device_id=(x, y, 1 - c), device_id_type=MESH)
               for a in range(n)]
        for cp in cps:
            cp.start()
        for a in range(n):
            pltpu.make_async_remote_copy(src_ref=outs[a].at[1 - c], dst_ref=outs[a].at[1 - c], send_sem=send_sems.at[a],
                                         recv_sem=recv_sems.at[a], device_id=(x, y, 1 - c),
                                         device_id_type=MESH).wait_recv()
        for cp in cps:
            cp.wait_send()

    any_spec = pl.BlockSpec(memory_space=pl.ANY)
    return pl.pallas_call(
        body, name=name, out_shape=[jax.ShapeDtypeStruct(h.shape, h.dtype) for h in hs],
        in_specs=[any_spec] * n, out_specs=[any_spec] * n, input_output_aliases={a: a for a in range(n)},
        scratch_shapes=[pltpu.SemaphoreType.DMA((n,)), pltpu.SemaphoreType.DMA((n,))])(*hs)


def owner_sum(name, land):
    _, r, w = land.shape
    T = r // 2

    def body(_, l_ref, o_ref):
        acc = l_ref[0].astype(F32)
        for j in range(1, 8):
            acc = acc + l_ref[j].astype(F32)
        o_ref[...] = acc

    grid_spec = pltpu.PrefetchScalarGridSpec(
        num_scalar_prefetch=1, grid=(2,),
        in_specs=[pl.BlockSpec((8, T, w), lambda i, at: (0, i, 0))],
        out_specs=pl.BlockSpec((None, T, w), lambda i, at: (at[0], i, 0)))
    at = jnp.stack([lax.axis_index("c")]).astype(jnp.int32)
    return pl.pallas_call(body, name=name, grid_spec=grid_spec, out_shape=jax.ShapeDtypeStruct((2, r, w), F32),
                          compiler_params=_cp("arbitrary"))(at, land)


W_IN_RUNS = ((0, 2, 1296, 376), (376, 3, 0, 1672), (2048, 1, 920, 752), (2800, 2, 0, 1296), (4096, 0, 0, 1024),
             (5120, 0, 1024, 648), (5768, 1, 0, 920))


def w_in_to_padded(name, g4):
    T = 128

    def body(g_ref, o_ref):
        o_ref[:, D_PROJ:PW] = jnp.zeros((T, PW - D_PROJ), o_ref.dtype)
        for (a, s, j0, w) in W_IN_RUNS:
            o_ref[:, a:a + w] = g_ref[s, :, j0:j0 + w]

    return pl.pallas_call(body, name=name, grid=(D // T,), in_specs=[pl.BlockSpec((4, T, 1672), lambda i: (0, i, 0))],
                          out_specs=pl.BlockSpec((T, PW), lambda i: (i, 0)),
                          out_shape=jax.ShapeDtypeStruct((D, PW), g4.dtype), compiler_params=_cp("arbitrary"))(g4)


def w_in_from_padded(name, gp):
    T = 128

    def body(g_ref, o_ref):
        for (a, s, j0, w) in W_IN_RUNS:
            o_ref[s, :, j0:j0 + w] = g_ref[:, a:a + w]

    return pl.pallas_call(body, name=name, grid=(D // T,), in_specs=[pl.BlockSpec((T, PW), lambda i: (i, 0))],
                          out_specs=pl.BlockSpec((4, T, 1672), lambda i: (0, i, 0)),
                          out_shape=jax.ShapeDtypeStruct((4, D, 1672), gp.dtype), compiler_params=_cp("arbitrary"))(gp)


def sum_devices(name, g):
    def body(g_ref, o_ref):
        acc = g_ref[0]
        for k in range(1, 8):
            acc = acc + g_ref[k]
        o_ref[...] = acc

    return pl.pallas_call(body, name=name, out_shape=jax.ShapeDtypeStruct(g.shape[1:], F32),
                          compiler_params=pltpu.CompilerParams(vmem_limit_bytes=VMEM_LIMIT_V7X))(g)


def adamw(name, w, g, m, v, T):
    r, wd = w.shape
    c1 = 1.0 - ADAM_B1 ** ADAM_STEP
    c2 = 1.0 - ADAM_B2 ** ADAM_STEP

    def body(w_ref, g_ref, m_ref, v_ref, d_ref, mo_ref, vo_ref):
        gv = g_ref[...]
        mn = ADAM_B1 * m_ref[...] + (1.0 - ADAM_B1) * gv
        vn = ADAM_B2 * v_ref[...] + (1.0 - ADAM_B2) * (gv * gv)
        d_ref[...] = -ADAM_LR * ((mn / c1) / (jnp.sqrt(vn / c2) + ADAM_EPS) + ADAM_WD * w_ref[...])
        mo_ref[...] = mn
        vo_ref[...] = vn

    spec = pl.BlockSpec((T, wd), lambda i: (i, 0))
    return pl.pallas_call(body, name=name, grid=(r // T,), in_specs=[spec] * 4, out_specs=[spec] * 3,
                          out_shape=[jax.ShapeDtypeStruct((r, wd), F32)] * 3, compiler_params=_cp("arbitrary"))(w, g, m, v)


BIG = {'w_in': (1024, 1672), 'w_ssd_proj': (256, 1024), 'w_gm_proj': (256, 1024), 'w_out': (256, 1024),
       'w_ff1': (1024, 704), 'w_ff3': (1024, 704), 'w_ff2': (704, 1024)}


class Flat:
    def __init__(self, segs):
        self.off, o = {}, 0
        for name, size in segs:
            self.off[name] = (o, size)
            o += -(-size // 128) * 128
        self.rows = -(-o // 1024) * 8

    def pack(self, vals):
        parts = []
        for name, (o, size) in self.off.items():
            v = vals[name].reshape(-1).astype(F32)
            parts.append(jnp.pad(v, (0, -(-size // 128) * 128 - size)))
        buf = jnp.concatenate(parts)
        return jnp.pad(buf, (0, self.rows * 128 - buf.shape[0])).reshape(self.rows, 128)

    def get(self, buf, name, shape=None):
        o, size = self.off[name]
        v = buf[o // 128:(o + size + 127) // 128].reshape(-1)[:size]
        return v if shape is None else v.reshape(shape)


PARTIALS = Flat([('loss', 1), ('ln0_g', D), ('ln0_b', D), ('dmod_x', 6 * D), ('dmod_c', 6 * D), ('conv_w', 5 * 1536),
                 ('conv_b', 1536), ('dt_bias', 32), ('a_log', 32), ('d_skip', 32), ('ssd_norm_g', D),
                 ('gm_norm_g', D), ('gm_norm_b', D), ('w_spatial', 8 * Q * Q), ('b_spatial', 8 * Q), ('b_gate', 2 * D),
                 ('ln1_g', D), ('ln1_b', D), ('ln2_g', D), ('ln2_b', D)])

WEIGHTS = ('c_ctx', 'ln0_g', 'ln0_b', 'w_ada', 'b_ada', 'w_in', 'conv_w', 'conv_b', 'dt_bias', 'a_log', 'd_skip',
           'ssd_norm_g', 'gm_norm_g', 'gm_norm_b', 'w_spatial', 'b_spatial', 'b_gate', 'w_ssd_proj', 'w_gm_proj',
           'w_out', 'ln1_g', 'ln1_b', 'w_ff1', 'w_ff3', 'w_ff2', 'ln2_g', 'ln2_b')
BIG_NAMES = tuple(BIG)
SMALL_NAMES = tuple(n for n in WEIGHTS if n not in BIG_NAMES and n != 'w_ada')


def kernel(x, c, ctx, c_ctx, ln0_g, ln0_b, w_ada, b_ada, w_in, conv_w, conv_b, dt_bias, a_log, d_skip, ssd_norm_g, gm_norm_g, gm_norm_b, w_spatial, b_spatial, b_gate, w_ssd_proj, w_gm_proj, w_out, ln1_g, ln1_b, w_ff1, w_ff3, w_ff2, ln2_g, ln2_b, loss_target, m_c_ctx, m_ln0_g, m_ln0_b, m_w_ada, m_b_ada, m_w_in, m_conv_w, m_conv_b, m_dt_bias, m_a_log, m_d_skip, m_ssd_norm_g, m_gm_norm_g, m_gm_norm_b, m_w_spatial, m_b_spatial, m_b_gate, m_w_ssd_proj, m_w_gm_proj, m_w_out, m_ln1_g, m_ln1_b, m_w_ff1, m_w_ff3, m_w_ff2, m_ln2_g, m_ln2_b, v_c_ctx, v_ln0_g, v_ln0_b, v_w_ada, v_b_ada, v_w_in, v_conv_w, v_conv_b, v_dt_bias, v_a_log, v_d_skip, v_ssd_norm_g, v_gm_norm_g, v_gm_norm_b, v_w_spatial, v_b_spatial, v_b_gate, v_w_ssd_proj, v_w_gm_proj, v_w_out, v_ln1_g, v_ln1_b, v_w_ff1, v_w_ff3, v_w_ff2, v_ln2_g, v_ln2_b):
    wts = dict(c_ctx=c_ctx, ln0_g=ln0_g, ln0_b=ln0_b, w_ada=w_ada, b_ada=b_ada, w_in=w_in, conv_w=conv_w, conv_b=conv_b,
               dt_bias=dt_bias, a_log=a_log, d_skip=d_skip, ssd_norm_g=ssd_norm_g, gm_norm_g=gm_norm_g,
               gm_norm_b=gm_norm_b, w_spatial=w_spatial, b_spatial=b_spatial, b_gate=b_gate, w_ssd_proj=w_ssd_proj,
               w_gm_proj=w_gm_proj, w_out=w_out, ln1_g=ln1_g, ln1_b=ln1_b, w_ff1=w_ff1, w_ff3=w_ff3, w_ff2=w_ff2,
               ln2_g=ln2_g, ln2_b=ln2_b)
    ms = dict(zip(WEIGHTS, (m_c_ctx, m_ln0_g, m_ln0_b, m_w_ada, m_b_ada, m_w_in, m_conv_w, m_conv_b, m_dt_bias, m_a_log,
                            m_d_skip, m_ssd_norm_g, m_gm_norm_g, m_gm_norm_b, m_w_spatial, m_b_spatial, m_b_gate,
                            m_w_ssd_proj, m_w_gm_proj, m_w_out, m_ln1_g, m_ln1_b, m_w_ff1, m_w_ff3, m_w_ff2, m_ln2_g,
                            m_ln2_b)))
    vs = dict(zip(WEIGHTS, (v_c_ctx, v_ln0_g, v_ln0_b, v_w_ada, v_b_ada, v_w_in, v_conv_w, v_conv_b, v_dt_bias, v_a_log,
                            v_d_skip, v_ssd_norm_g, v_gm_norm_g, v_gm_norm_b, v_w_spatial, v_b_spatial, v_b_gate,
                            v_w_ssd_proj, v_w_gm_proj, v_w_out, v_ln1_g, v_ln1_b, v_w_ff1, v_w_ff3, v_w_ff2, v_ln2_g,
                            v_ln2_b)))
    px, py, pc = _place()
    shard = 2 * px + py
    dev = 2 * shard + pc
    take = lambda a, i, axis=0: lax.dynamic_index_in_dim(a, i, axis, keepdims=False)

    half = lambda n: take(wts[n][0].reshape(2, BIG[n][0] // 2, BIG[n][1]), pc).astype(BF16)

    pre = jnp.concatenate([c, jnp.pad(conv_w[0], ((0, 0), (0, D - 384))), jnp.zeros((2, D), F32)], axis=0)
    pre = allgather8("gather_cond", pre, False)
    conv_w_full = pre[0::2, 1:6, :384].transpose(1, 0, 2).reshape(5, 1536)
    a16 = jnp.concatenate([_silu(pre[:, 0, :]), _silu(c_ctx)[None], jnp.zeros((7, D), F32)], axis=0)
    mod = matmul("ada_fwd", a16, w_ada[0], 'nn', 16, 512, 1024)
    mod = mod + lax.dynamic_slice_in_dim(b_ada[0], shard * 1536, 1536)[None]
    mod = allgather8("gather_mod", mod, False)
    mod = jnp.concatenate([mod[0], mod[2], mod[4], mod[6]], axis=1)
    mod_x = take(mod, dev).reshape(6, D)
    mod_c = mod[8].reshape(6, D)

    def full(n, blocks):
        r, w = BIG[n]
        return blocks.reshape(4, r, w) if w != D else blocks.reshape(4 * r, w)

    class Exchanges:
        rest_names = BIG_NAMES[1:]

        def __init__(self):
            self.pending = []

        def w_in(self):
            blocks = allgather8("gather_w_in", half('w_in'), True)
            w = w_in_to_padded("w_in_layout", full('w_in', blocks))
            halves = [half(n) for n in self.rest_names]
            halves[0], _ = lax.optimization_barrier((halves[0], blocks))
            lands = [jax.ShapeDtypeStruct((8,) + h.shape, BF16) for h in halves]
            self.rest_refs = sequencer_exchange("gather_rest", 1, halves, lands, plan_gather)
            return w

        def rest(self):
            return {n: full(n, r[...]) for n, r in zip(self.rest_names, self.rest_refs)}

        def grads(self, group, gs):
            if group == 'in':
                gs = {'w_in': w_in_from_padded("w_in_grad_layout", gs['w_in'])}
            blocks = [g.reshape(4, 2, BIG[n][0] // 2, BIG[n][1]) for n, g in gs.items()]
            lands = [jax.ShapeDtypeStruct((8,) + b.shape[2:], BF16) for b in blocks]
            refs = sequencer_exchange("grads_" + group, 2 + len(self.pending), blocks, lands, plan_to_owner)
            self.pending.append((tuple(gs), refs))

        def finish(self, first, last):
            names, halves = [], []
            for ns, refs in self.pending[first:last]:
                names += ns
                halves += [owner_sum("grads_sum_" + n, r[...]) for n, r in zip(ns, refs)]
            pairs = sibling_pair("grads_halves_%d" % first, halves)
            return {n: h.reshape(BIG[n]) for n, h in zip(names, pairs)}

    S = dict(ln0_g=ln0_g, ln0_b=ln0_b, conv_w=conv_w_full, conv_b=conv_b[0], dt_bias=dt_bias[0], a_log=a_log[0],
             d_skip=d_skip[0], ssd_norm_g=ssd_norm_g[0], gm_norm_g=gm_norm_g[0], gm_norm_b=gm_norm_b[0],
             w_spatial=w_spatial[0], b_spatial=b_spatial[0], b_gate=b_gate[0], ln1_g=ln1_g[0], ln1_b=ln1_b[0],
             ln2_g=ln2_g[0], ln2_b=ln2_b[0])
    x_all = jnp.concatenate([ctx[0], x[0]], axis=0)
    exchanges = Exchanges()
    grad_x, gsmall = core(x_all, loss_target[0], mod_x, mod_c, exchanges, S)

    packed = PARTIALS.pack(gsmall)
    parts_ref, = sequencer_exchange("gather_partials", 5, [packed], [jax.ShapeDtypeStruct((8,) + packed.shape, F32)],
                                    plan_gather)
    delta, new_m, new_v = {}, {}, {}

    def step(n, gn):
        T = 352 if n == 'w_ff2' else 256
        delta[n], new_m[n], new_v[n] = adamw("adamw_" + n, wts[n][0], gn, ms[n][0], vs[n][0], T)

    g = {}
    for first, last in ((0, 2), (2, 3)):
        shards = exchanges.finish(first, last)
        for n, gn in shards.items():
            step(n, gn)
        g.update(shards)

    parts = parts_ref[...]
    tot = sum_devices("partials_sum", parts)
    g.update({n: PARTIALS.get(tot, n) for n in ('ln0_g', 'ln0_b', 'conv_b', 'dt_bias', 'a_log', 'd_skip', 'ssd_norm_g',
                                                'gm_norm_g', 'gm_norm_b', 'w_spatial', 'b_spatial', 'b_gate', 'ln1_g',
                                                'ln1_b', 'ln2_g', 'ln2_b')})
    loss = PARTIALS.get(tot, 'loss', ())
    dmod_c = PARTIALS.get(tot, 'dmod_c')
    g['b_ada'] = PARTIALS.get(tot, 'dmod_x') + dmod_c
    g['conv_w'] = lax.dynamic_slice_in_dim(PARTIALS.get(tot, 'conv_w', (5, 1536)), shard * 384, 384, axis=1)
    o, size = PARTIALS.off['dmod_x']
    dmod_rows = parts[:, o // 128:(o + size) // 128].reshape(8, size)
    dm = jnp.concatenate([dmod_rows, dmod_c[None], jnp.zeros((7, 6 * D), F32)], axis=0)
    dm = lax.dynamic_slice_in_dim(dm, shard * 1536, 1536, axis=1)
    g['w_ada'] = matmul("ada_bwd_w", a16, dm, 'tn', 1024, 512, 16)
    dm_c = jnp.concatenate([dm[8:9], jnp.zeros((15, 1536), F32)], axis=0)
    dc = matmul("ada_bwd_c", dm_c, w_ada[0], 'nt', 16, 1024, 512)
    dc = allgather8("gather_dcctx", dc, False)[:, 0, :]
    dc = ((dc[0] + dc[2]) + dc[4]) + dc[6]
    sg = jax.nn.sigmoid(c_ctx)
    g['c_ctx'] = dc * (sg * (1.0 + c_ctx * (1.0 - sg)))

    step('w_ada', g['w_ada'])
    lay = Flat([(n, wts[n].size) for n in SMALL_NAMES])
    d_, m_, v_ = adamw("adamw_small", lay.pack(wts), lay.pack(g), lay.pack(ms), lay.pack(vs), lay.rows)
    for n in SMALL_NAMES:
        delta[n], new_m[n], new_v[n] = (lay.get(b, n) for b in (d_, m_, v_))

    shp = lambda d: [d[n].reshape(wts[n].shape) for n in WEIGHTS]
    return (loss, grad_x[None], *shp(g), *shp(delta), *shp(new_m), *shp(new_v))
```

```python
import functools

import jax
import jax.numpy as jnp
from jax import lax
from jax.experimental import pallas as pl
from jax.experimental.pallas import tpu as pltpu
from jax.experimental.pallas import tpu_sc as plsc

F32 = jnp.float32
BF16 = jnp.bfloat16
MESH = pl.DeviceIdType.MESH

VMEM_LIMIT_V7X = 56 * 1024 * 1024

D = 1024
LC = 256
Q = 128
NH = 16
D_FF = 2816
LN_EPS = 1e-5
ALPHA = 2.0 ** 0.25

PW = 7168
C_GATE, C_UV, C_Z, C_XBC, C_DT = 0, 2048, 4096, 5120, 6656
D_PROJ = 6688

ADAM_LR, ADAM_B1, ADAM_B2, ADAM_EPS, ADAM_WD, ADAM_STEP = 0.001, 0.9, 0.999, 1e-08, 0.01, 10


def _cp(*sem):
    return pltpu.CompilerParams(dimension_semantics=sem, vmem_limit_bytes=VMEM_LIMIT_V7X)


def _dot(a, b, ca, cb):
    return lax.dot_general(a.astype(BF16), b.astype(BF16), (((ca,), (cb,)), ((), ())),
                           preferred_element_type=F32)


@jax.custom_vjp
def mm(a, b):
    return _dot(a, b, 1, 0)


mm.defvjp(lambda a, b: (_dot(a, b, 1, 0), (a, b)),
          lambda r, g: (_dot(g, r[1], 1, 1), _dot(r[0], g, 0, 0)))


@jax.custom_vjp
def mm_nt(a, b):
    return _dot(a, b, 1, 1)


mm_nt.defvjp(lambda a, b: (_dot(a, b, 1, 1), (a, b)),
             lambda r, g: (_dot(g, r[1], 1, 0), _dot(g, r[0], 0, 0)))


@jax.custom_vjp
def mm_tn(a, b):
    return _dot(a, b, 0, 0)


mm_tn.defvjp(lambda a, b: (_dot(a, b, 0, 0), (a, b)),
             lambda r, g: (_dot(r[1], g, 1, 1), _dot(r[0], g, 1, 0)))


def _dot32(a, b):
    return lax.dot_general(a, b, (((1,), (0,)), ((), ())), precision=lax.Precision.HIGHEST,
                           preferred_element_type=F32)


def _cumsum_fn(rev):
    def tri(transpose):
        r = lax.broadcasted_iota(jnp.int32, (Q, Q), 0)
        c = lax.broadcasted_iota(jnp.int32, (Q, Q), 1)
        keep = (r >= c) if (rev == transpose) else (r <= c)
        return jnp.where(keep, 1.0, 0.0).astype(F32)

    @jax.custom_vjp
    def cums(a):
        return _dot32(tri(False), a)

    cums.defvjp(lambda a: (_dot32(tri(False), a), None), lambda _, g: (_dot32(tri(True), g),))
    return cums


def _cols(v, k):
    w = v.shape[1] // k
    return tuple(v[:, w * i:w * (i + 1)] for i in range(k))


def _splitter(k):
    @jax.custom_vjp
    def split(v):
        return _cols(v, k)

    @jax.custom_vjp
    def concat(ps):
        return jnp.concatenate(ps, axis=1)

    split.defvjp(lambda v: (_cols(v, k), None), lambda _, g: (jnp.concatenate(g, axis=1),))
    concat.defvjp(lambda ps: (jnp.concatenate(ps, axis=1), None), lambda _, g: (_cols(g, k),))
    return split, concat


split2, _ = _splitter(2)
split4, _ = _splitter(4)
split8, concat8 = _splitter(8)


def _ln(x, g, b):
    mu = jnp.mean(x, axis=-1, keepdims=True)
    xc = x - mu
    var = jnp.mean(xc * xc, axis=-1, keepdims=True)
    return xc * lax.rsqrt(var + LN_EPS) * g + b


def _silu(x):
    return x * jax.nn.sigmoid(x)


def _gelu(x):
    return 0.5 * x * (1.0 + jnp.tanh(0.7978845608028654 * (x + 0.044715 * (x * x * x))))


def _xspec(T, w, col, roff):
    return pl.BlockSpec((T, w), lambda i, col=col, roff=roff: (jnp.maximum(i + roff, 0), col))


def _pspec(p, sel):
    if sel is None:
        return pl.BlockSpec(p.shape, lambda i, n=p.ndim: (0,) * n)
    return pl.BlockSpec((1,) + p.shape[1:], lambda i, n=p.ndim: (sel(i),) + (0,) * (n - 1))


def _out_plumbing(outs, T, args, in_specs):
    shapes, specs, aliases = [], [], {}
    for k, o in enumerate(outs):
        if o[0] == 'new':
            _, rows, w, roff = o[:4]
            shapes.append(jax.ShapeDtypeStruct((rows, w), o[4] if len(o) > 4 else F32))
            specs.append(_xspec(T, w, 0, roff))
        elif o[0] == 'acc':
            shapes.append(jax.ShapeDtypeStruct(o[1], F32))
            specs.append(pl.BlockSpec(o[1], lambda i, n=len(o[1]): (0,) * n))
        elif o[0] == 'part':
            _, rows, wtot, w, col, roff, dtype = o
            shapes.append(jax.ShapeDtypeStruct((rows, wtot), dtype))
            specs.append(_xspec(T, w, col, roff))
        else:
            _, arr, w, col, roff = o
            aliases[len(args)] = k
            args.append(arr)
            in_specs.append(pl.BlockSpec(memory_space=pl.ANY))
            shapes.append(jax.ShapeDtypeStruct(arr.shape, arr.dtype))
            specs.append(_xspec(T, w, col, roff))
    return shapes, specs, aliases


def stage_fwd(name, f, T, n, xs, ps, outs):
    nx, npar = len(xs), len(ps)
    args = [x[0] for x in xs] + [p[0] for p in ps]
    in_specs = [_xspec(T, w, col, roff) for (_, w, col, roff) in xs] + [_pspec(p, sel) for (p, sel) in ps]
    n_in = len(args)
    shapes, specs, aliases = _out_plumbing(outs, T, args, in_specs)
    n_all_in = len(args)

    def body(*refs):
        i = pl.program_id(0)
        xv = [r[...] for r in refs[:nx]]
        pv = [r[...] if ps[k][1] is None else r[0] for k, r in enumerate(refs[nx:n_in])]
        res = f(*xv, *pv)
        for k, o_ref in enumerate(refs[n_all_in:]):
            if outs[k][0] == 'acc':
                @pl.when(i == 0)
                def _(o_ref=o_ref, v=res[k]):
                    o_ref[...] = v

                @pl.when(i > 0)
                def _(o_ref=o_ref, v=res[k]):
                    o_ref[...] += v
            else:
                o_ref[...] = res[k].astype(o_ref.dtype)

    return pl.pallas_call(body, name=name, grid=(n,), in_specs=in_specs, out_specs=specs, out_shape=shapes,
                          input_output_aliases=aliases, compiler_params=_cp("arbitrary"))(*args)


def stage_bwd(name, f, T, n, xs, ps, cts, dxs, dps, primal=()):
    nx, npar = len(xs), len(ps)
    args = [x[0] for x in xs] + [p[0] for p in ps]
    in_specs = [_xspec(T, w, col, roff) for (_, w, col, roff) in xs] + [_pspec(p, sel) for (p, sel) in ps]
    ct_arrs = [c for c in cts if isinstance(c, tuple)]
    for (a, w, col, roff) in ct_arrs:
        args.append(a)
        in_specs.append(_xspec(T, w, col, roff))
    n_in = len(args)
    outs, out_of = [], []
    for k, o in enumerate(dxs):
        if o is not None:
            outs.append(o)
            out_of.append(('x', k))
    for k, want in enumerate(dps):
        if want:
            p, sel = ps[k]
            outs.append(('acc', p.shape))
            out_of.append(('p', k))
    for k, shape in primal:
        outs.append(('acc', shape))
        out_of.append(('r', k))
    shapes, specs, aliases = _out_plumbing(outs, T, args, in_specs)
    for j, (kind, k) in enumerate(out_of):
        if kind == 'p' and ps[k][1] is not None:
            p, sel = ps[k]
            specs[j] = pl.BlockSpec((1,) + p.shape[1:], lambda i, n=p.ndim, sel=sel: (sel(i),) + (0,) * (n - 1))
    n_all_in = len(args)

    def body(*refs):
        i = pl.program_id(0)
        xv = [r[...] for r in refs[:nx]]
        pv = [r[...] if ps[k][1] is None else r[0] for k, r in enumerate(refs[nx:nx + npar])]
        res, vjp_fn = jax.vjp(f, *xv, *pv)
        ctv, q = [], nx + npar
        for k, c in enumerate(cts):
            if c is None:
                ctv.append(jnp.zeros_like(res[k]))
            elif isinstance(c, tuple):
                v = refs[q][...]
                if c[3] < 0:
                    v = v * (i + c[3] >= 0).astype(F32)
                ctv.append(v)
                q += 1
            else:
                ctv.append(jnp.full_like(res[k], c))
        grads = vjp_fn(tuple(ctv))
        for j, o_ref in enumerate(refs[n_all_in:]):
            kind, k = out_of[j]
            if kind == 'x':
                o_ref[...] = grads[k].astype(o_ref.dtype)
            else:
                g = res[k] if kind == 'r' else grads[nx + k]
                sel = None if kind == 'r' else ps[k][1]
                if sel is None:
                    first = i == 0
                    tgt = o_ref
                else:
                    first = jnp.logical_or(i == 0, sel(i) != sel(jnp.maximum(i - 1, 0)))
                    tgt = o_ref.at[0]

                @pl.when(first)
                def _(tgt=tgt, g=g):
                    tgt[...] = g

                @pl.when(jnp.logical_not(first))
                def _(tgt=tgt, g=g):
                    tgt[...] += g

    return pl.pallas_call(body, name=name, grid=(n,), in_specs=in_specs, out_specs=specs, out_shape=shapes,
                          input_output_aliases=aliases, compiler_params=_cp("arbitrary"))(*args)


_CONTRACT = {'nn': (1, 0), 'nt': (1, 1), 'tn': (0, 0)}


def matmul(name, a, b, mode, tm, tn, tk, out_dtype=F32, add=None):
    if mode == 'nn':
        (M, K), (_, N) = a.shape, b.shape
    elif mode == 'nt':
        (M, K), (N, _) = a.shape, b.shape
    else:
        (K, M), (_, N) = a.shape, b.shape
    assert M % tm == 0 and N % tn == 0 and K % tk == 0, (name, M, N, K, tm, tn, tk)
    a_spec = (pl.BlockSpec((tk, tm), lambda j, i, k: (k, i)) if mode == 'tn'
              else pl.BlockSpec((tm, tk), lambda j, i, k: (i, k)))
    b_spec = (pl.BlockSpec((tn, tk), lambda j, i, k: (j, k)) if mode == 'nt'
              else pl.BlockSpec((tk, tn), lambda j, i, k: (k, j)))
    o_spec = pl.BlockSpec((tm, tn), lambda j, i, k: (i, j))
    return matmul_call(name, (N // tn, M // tm, K // tk), a, a_spec, b, b_spec, (M, N), o_spec, (tm, tn), mode,
                       out_dtype, add)


def matmul_call(name, grid, a, a_spec, b, b_spec, out_shape, o_spec, tile, mode, out_dtype=F32, add=None):
    tm, tn = tile
    nk = grid[2]
    ca, cb = _CONTRACT[mode]
    args, in_specs = [a, b], [a_spec, b_spec]
    if add is not None:
        args.append(add)
        in_specs.append(o_spec)

    def body(*refs):
        a_ref, b_ref = refs[0], refs[1]
        o_ref, acc = refs[-2], refs[-1]
        k = pl.program_id(2)
        if nk == 1:
            p = _dot(a_ref[...], b_ref[...], ca, cb)
            o_ref[...] = (p + refs[2][...] if add is not None else p).astype(out_dtype)
            return

        @pl.when(k == 0)
        def _():
            acc[...] = refs[2][...] if add is not None else jnp.zeros_like(acc)

        acc[...] += _dot(a_ref[...], b_ref[...], ca, cb)

        @pl.when(k == nk - 1)
        def _():
            o_ref[...] = acc[...].astype(out_dtype)

    return pl.pallas_call(body, name=name, grid=grid, in_specs=in_specs, out_specs=o_spec,
                          out_shape=jax.ShapeDtypeStruct(out_shape, out_dtype),
                          scratch_shapes=[pltpu.VMEM((tm, tn) if nk > 1 else (8, 128), F32)],
                          compiler_params=_cp("arbitrary", "arbitrary", "arbitrary"))(*args)


NS, WS = 4, 704


def _resident(name, M, tm, rows, weight, out_shape, out_block, out_map, step, add=None):
    args = [rows[0], weight] + ([] if add is None else [add])
    in_specs = [pl.BlockSpec(rows[1], rows[2]), pl.BlockSpec(weight.shape, lambda i, n=weight.ndim: (0,) * n)]
    if add is not None:
        in_specs.append(pl.BlockSpec(out_block, out_map))
    return pl.pallas_call(step, name=name, grid=(M // tm,), in_specs=in_specs, out_specs=pl.BlockSpec(out_block, out_map),
                          out_shape=jax.ShapeDtypeStruct(out_shape, F32), compiler_params=_cp("arbitrary"))(*args)


def ffn_in_fwd(name, h, w1, w3, tm):
    M = h.shape[0]

    def step(h_ref, w1_ref, w3_ref, a1_ref, a3_ref, act_ref):
        for s in range(NS):
            a1 = _dot(h_ref[...], w1_ref[s], 1, 0)
            a3 = _dot(h_ref[...], w3_ref[s], 1, 0)
            a1_ref[s] = a1.astype(a1_ref.dtype)
            a3_ref[s] = a3.astype(a3_ref.dtype)
            act_ref[s] = (_silu(a1) * a3).astype(act_ref.dtype)

    wspec = pl.BlockSpec((NS, D, WS), lambda i: (0, 0, 0))
    ospec = pl.BlockSpec((NS, tm, WS), lambda i: (0, i, 0))
    return pl.pallas_call(
        step, name=name, grid=(M // tm,), in_specs=[pl.BlockSpec((tm, D), lambda i: (i, 0)), wspec, wspec],
        out_specs=[ospec, ospec, ospec],
        out_shape=[jax.ShapeDtypeStruct((NS, M, WS), BF16)] * 3, compiler_params=_cp("arbitrary"))(h, w1, w3)


def ffn_out_bwd_x(name, dff, w2, a1, a3, tm):
    M = dff.shape[0]

    def step(d_ref, w_ref, a1_ref, a3_ref, da1_ref, da3_ref):
        for s in range(NS):
            dact = _dot(d_ref[...], w_ref[s * WS:(s + 1) * WS, :], 1, 1)
            a1 = a1_ref[s].astype(F32)
            sig = jax.nn.sigmoid(a1)
            da3_ref[s] = (dact * (a1 * sig)).astype(da3_ref.dtype)
            da1_ref[s] = (dact * a3_ref[s].astype(F32) * (sig * (1.0 + a1 * (1.0 - sig)))).astype(da1_ref.dtype)

    aspec = pl.BlockSpec((NS, tm, WS), lambda i: (0, i, 0))
    return pl.pallas_call(
        step, name=name, grid=(M // tm,),
        in_specs=[pl.BlockSpec((tm, D), lambda i: (i, 0)), pl.BlockSpec(w2.shape, lambda i: (0, 0)), aspec, aspec],
        out_specs=[aspec, aspec],
        out_shape=[jax.ShapeDtypeStruct((NS, M, WS), BF16)] * 2, compiler_params=_cp("arbitrary"))(dff, w2, a1, a3)


def ff_in_bwd_x(name, da3, w3, tm, add=None):
    M = da3.shape[1]

    def step(*refs):
        d_ref, w_ref, o_ref = refs[0], refs[1], refs[-1]
        acc = _dot(d_ref[0], w_ref[0], 1, 1)
        for s in range(1, NS):
            acc = acc + _dot(d_ref[s], w_ref[s], 1, 1)
        o_ref[...] = acc if add is None else acc + refs[2][...]

    return _resident(name, M, tm, (da3, (NS, tm, WS), lambda i: (0, i, 0)), w3, (M, D), (tm, D), lambda i: (i, 0), step, add)


def ff_in_bwd_w(name, h, da3, tk):
    M = h.shape[0]

    def step(h_ref, d_ref, acc):
        for s in range(NS):
            acc[s] += _dot(h_ref[...], d_ref[s], 0, 0)

    return _token_sum(name, M // tk, [pl.BlockSpec((tk, D), lambda k: (k, 0)), pl.BlockSpec((NS, tk, WS), lambda k: (0, k, 0))],
                      (NS, D, WS), (NS, D, WS), step, (h, da3))


def ff_out_fwd(name, act3, w2, tm):
    M = act3.shape[1]

    def step(a_ref, w_ref, o_ref):
        acc = _dot(a_ref[0], w_ref[0:WS, :], 1, 0)
        for s in range(1, NS):
            acc = acc + _dot(a_ref[s], w_ref[s * WS:(s + 1) * WS, :], 1, 0)
        o_ref[...] = acc

    return _resident(name, M, tm, (act3, (NS, tm, WS), lambda i: (0, i, 0)), w2, (M, D), (tm, D), lambda i: (i, 0), step)


def _token_sum(name, nk, in_specs, out_shape, acc_shape, step, args):
    def body(*refs):
        o_ref, acc = refs[-2], refs[-1]
        k = pl.program_id(0)

        @pl.when(k == 0)
        def _():
            acc[...] = jnp.zeros_like(acc)

        step(*refs[:-2], acc)

        @pl.when(k == nk - 1)
        def _():
            o_ref[...] = acc[...].astype(o_ref.dtype)

    return pl.pallas_call(body, name=name, grid=(nk,), in_specs=in_specs,
                          out_specs=pl.BlockSpec(out_shape, lambda k, n=len(out_shape): (0,) * n),
                          out_shape=jax.ShapeDtypeStruct(out_shape, BF16), scratch_shapes=[pltpu.VMEM(acc_shape, F32)],
                          compiler_params=_cp("arbitrary"))(*args)


def ff_out_bwd_w(name, act3, dff, tk):
    M = dff.shape[0]

    def step(a_ref, d_ref, acc):
        for s in range(NS):
            acc[s * WS:(s + 1) * WS, :] += _dot(a_ref[s], d_ref[...], 0, 0)

    return _token_sum(name, M // tk, [pl.BlockSpec((NS, tk, WS), lambda k: (0, k, 0)), pl.BlockSpec((tk, D), lambda k: (k, 0))],
                      (NS * WS, D), (NS * WS, D), step, (act3, dff))


def _shift_rows(x, d):
    n = x.shape[0]
    if d == 0:
        return x
    y = pltpu.roll(x, (-d) % n, 0)
    t = lax.broadcasted_iota(jnp.int32, x.shape, 0)
    ok = (t + d >= 0) & (t + d < n)
    return jnp.where(ok, y, 0.0)


def _conv_pre(x, w_ref, b_ref):
    acc = jnp.broadcast_to(b_ref[...], x.shape)
    for k in range(5):
        acc = acc + _shift_rows(x, k - 2) * w_ref[k:k + 1, :]
    return acc


def conv_fwd(name, proj, conv_w, conv_b, R):
    segs = ((0, LC), (LC, R))

    def body(x_ref, w_ref, b_ref, o_ref):
        for (s, e) in segs:
            pre = _conv_pre(x_ref[s:e, :], w_ref, b_ref)
            o_ref[s:e, :] = _silu(pre)

    return pl.pallas_call(
        body, name=name, grid=(12,),
        in_specs=[pl.BlockSpec((R, 128), lambda j: (0, C_XBC // 128 + j)),
                  pl.BlockSpec((8, 128), lambda j: (0, j)), pl.BlockSpec((1, 128), lambda j: (0, j))],
        out_specs=pl.BlockSpec((R, 128), lambda j: (0, j)),
        out_shape=jax.ShapeDtypeStruct((R, 1536), F32), compiler_params=_cp("arbitrary"))(proj, conv_w, conv_b)


def conv_bwd(name, proj, conv_w, conv_b, d_f, d_b, d_skip, dproj, R):
    segs = ((0, LC), (LC, R))

    def body(x_ref, w_ref, b_ref, df_ref, db_ref, ds_ref, _, dx_ref, dw_ref, dbias_ref):
        j = pl.program_id(0)
        has_skip = (j < 8).astype(F32)
        dw = [jnp.zeros((1, 128), F32) for _ in range(5)]
        dbias = jnp.zeros((1, 128), F32)
        for (s, e) in segs:
            x = x_ref[s:e, :]
            pre = _conv_pre(x, w_ref, b_ref)
            sig = jax.nn.sigmoid(pre)
            dy = df_ref[s:e, :] + db_ref[s:e, :]
            if s == LC:
                dy = dy + ds_ref[...] * has_skip
            dpre = dy * (sig * (1.0 + pre * (1.0 - sig)))
            dx = jnp.zeros_like(x)
            for k in range(5):
                dx = dx + _shift_rows(dpre, 2 - k) * w_ref[k:k + 1, :]
                dw[k] = dw[k] + jnp.sum(dpre * _shift_rows(x, k - 2), axis=0, keepdims=True)
            dbias = dbias + jnp.sum(dpre, axis=0, keepdims=True)
            dx_ref[s:e, :] = dx.astype(dx_ref.dtype)
        dw_ref[...] = jnp.zeros_like(dw_ref)
        for k in range(5):
            dw_ref[k:k + 1, :] = dw[k]
        dbias_ref[...] = dbias

    return pl.pallas_call(
        body, name=name, grid=(12,),
        in_specs=[pl.BlockSpec((R, 128), lambda j: (0, C_XBC // 128 + j)),
                  pl.BlockSpec((8, 128), lambda j: (0, j)), pl.BlockSpec((1, 128), lambda j: (0, j)),
                  pl.BlockSpec((R, 128), lambda j: (0, j)), pl.BlockSpec((R, 128), lambda j: (0, j)),
                  pl.BlockSpec((R - LC, 128), lambda j: (0, jnp.minimum(j, 7))),
                  pl.BlockSpec(memory_space=pl.ANY)],
        out_specs=[pl.BlockSpec((R, 128), lambda j: (0, C_XBC // 128 + j)),
                   pl.BlockSpec((8, 128), lambda j: (0, j)), pl.BlockSpec((1, 128), lambda j: (0, j))],
        out_shape=[jax.ShapeDtypeStruct(dproj.shape, dproj.dtype), jax.ShapeDtypeStruct((8, 1536), F32),
                   jax.ShapeDtypeStruct((1, 1536), F32)],
        input_output_aliases={6: 0}, compiler_params=_cp("arbitrary"))(proj, conv_w, conv_b, d_f, d_b, d_skip, dproj)


def _ssd_chunk(rev, dirn):
    cums = _cumsum_fn(rev)

    def f(xs, Bs, Cs, dt, alog, Hs):
        lane = lax.broadcasted_iota(jnp.int32, (1, 128), 1)
        sub = lax.broadcasted_iota(jnp.int32, (Q, 1), 0)
        r = lax.broadcasted_iota(jnp.int32, (Q, Q), 0)
        c = lax.broadcasted_iota(jnp.int32, (Q, Q), 1)
        mask = (r <= c) if rev else (r >= c)
        left = lane < 64
        a = dt * (-jnp.exp(alog))
        s = cums(a)
        sT, dtT = s.T, dt.T
        last_row = (sub == (0 if rev else Q - 1)).astype(F32)
        s_last = jnp.sum(s * last_row, axis=0, keepdims=True)
        G = [mm_nt(Cs[g], Bs[g]) for g in range(2)]
        M, es, wc, ed = [], [], [], []
        for h in range(NH):
            l = 16 * dirn + h
            oh_l = (lane == l).astype(F32)
            oh_s = (sub == l).astype(F32)
            s_col = jnp.sum(s * oh_l, axis=1, keepdims=True)
            dt_col = jnp.sum(dt * oh_l, axis=1, keepdims=True)
            s_row = jnp.sum(sT * oh_s, axis=0, keepdims=True)
            dt_row = jnp.sum(dtT * oh_s, axis=0, keepdims=True)
            sl = jnp.sum(s_last * oh_l, axis=1, keepdims=True)
            seg = jnp.where(mask, s_col - s_row, 0.0)
            lm = jnp.where(mask, jnp.exp(seg), 0.0)
            M.append(G[h // 8] * lm * dt_row)
            es.append(jnp.exp(s_col))
            wc.append(jnp.exp(sl - s_col) * dt_col)
            ed.append(jnp.exp(sl))
        Ys, Hn = [], []
        for j in range(8):
            g = j // 4
            xa = jnp.where(left, xs[j], 0.0)
            xb = jnp.where(left, 0.0, xs[j])
            yd = mm(M[2 * j], xa) + mm(M[2 * j + 1], xb)
            yo = mm(Cs[g], Hs[j]) * jnp.where(left, es[2 * j], es[2 * j + 1])
            Ys.append(yd + yo)
            st = mm_tn(Bs[g], xs[j] * jnp.where(left, wc[2 * j], wc[2 * j + 1]))
            Hn.append(Hs[j] * jnp.where(left, ed[2 * j], ed[2 * j + 1]) + st)
        return Ys, Hn

    return f


def _chunk_of(t, n, rev):
    if not rev:
        return t
    return jnp.where(t < 2, 1 - t, n + 1 - t)


def ssd_fwd(name, xbc, dt, alog, n, rev, dirn):
    chunk = _ssd_chunk(rev, dirn)

    def body(x_ref, b_ref, c_ref, dt_ref, al_ref, y_ref, hs_ref, h_scr):
        @pl.when(pl.program_id(0) == 0)
        def _():
            h_scr[...] = jnp.zeros_like(h_scr)

        xs = [x_ref[:, 128 * j:128 * (j + 1)] for j in range(8)]
        Bs = [b_ref[:, 128 * g:128 * (g + 1)] for g in range(2)]
        Cs = [c_ref[:, 128 * g:128 * (g + 1)] for g in range(2)]
        Hs = [h_scr[:, 128 * j:128 * (j + 1)] for j in range(8)]
        hs_ref[0] = h_scr[...]
        Ys, Hn = chunk(xs, Bs, Cs, dt_ref[...], al_ref[...], Hs)
        for j in range(8):
            y_ref[:, 128 * j:128 * (j + 1)] = Ys[j]
            h_scr[:, 128 * j:128 * (j + 1)] = Hn[j]

    cm = lambda t: _chunk_of(t, n, rev)
    return pl.pallas_call(
        body, name=name, grid=(n,),
        in_specs=[pl.BlockSpec((Q, 1024), lambda t: (cm(t), 0)), pl.BlockSpec((Q, 256), lambda t: (cm(t), 4)),
                  pl.BlockSpec((Q, 256), lambda t: (cm(t), 5)), pl.BlockSpec((Q, 128), lambda t: (cm(t), 0)),
                  pl.BlockSpec((1, 128), lambda t: (0, 0))],
        out_specs=[pl.BlockSpec((Q, 1024), lambda t: (cm(t), 0)), pl.BlockSpec((1, Q, 1024), lambda t: (cm(t), 0, 0))],
        out_shape=[jax.ShapeDtypeStruct((n * Q, 1024), F32), jax.ShapeDtypeStruct((n, Q, 1024), F32)],
        scratch_shapes=[pltpu.VMEM((Q, 1024), F32)], compiler_params=_cp("arbitrary"))(xbc, xbc, xbc, dt, alog)


def ssd_bwd(name, xbc, dt, alog, hs, dy, n, rev, dirn):
    chunk = _ssd_chunk(rev, dirn)

    def body(x_ref, b_ref, c_ref, dt_ref, al_ref, hs_ref, dy_ref, dx_ref, ddt_ref, dal_ref, dh_scr):
        tt = pl.program_id(0)
        ch = _chunk_of(n - 1 - tt, n, rev)

        @pl.when(tt == 0)
        def _():
            dh_scr[...] = jnp.zeros_like(dh_scr)

        xs = [x_ref[:, 128 * j:128 * (j + 1)] for j in range(8)]
        Bs = [b_ref[:, 128 * g:128 * (g + 1)] for g in range(2)]
        Cs = [c_ref[:, 128 * g:128 * (g + 1)] for g in range(2)]
        Hs = [hs_ref[0, :, 128 * j:128 * (j + 1)] for j in range(8)]
        live = (ch >= 2).astype(F32)
        dYs = [dy_ref[:, 128 * j:128 * (j + 1)] * live for j in range(8)]
        dHn = [dh_scr[:, 128 * j:128 * (j + 1)] for j in range(8)]
        _, vjp_fn = jax.vjp(chunk, xs, Bs, Cs, dt_ref[...], al_ref[...], Hs)
        dxs, dBs, dCs, ddt, dal, dHs = vjp_fn((dYs, dHn))
        for j in range(8):
            dx_ref[:, 128 * j:128 * (j + 1)] = dxs[j]
            dh_scr[:, 128 * j:128 * (j + 1)] = dHs[j]
        for g in range(2):
            dx_ref[:, 1024 + 128 * g:1024 + 128 * (g + 1)] = dBs[g]
            dx_ref[:, 1280 + 128 * g:1280 + 128 * (g + 1)] = dCs[g]
        ddt_ref[...] = ddt

        @pl.when(tt == 0)
        def _():
            dal_ref[...] = dal

        @pl.when(tt > 0)
        def _():
            dal_ref[...] += dal

    cm = lambda t: _chunk_of(n - 1 - t, n, rev)
    return pl.pallas_call(
        body, name=name, grid=(n,),
        in_specs=[pl.BlockSpec((Q, 1024), lambda t: (cm(t), 0)), pl.BlockSpec((Q, 256), lambda t: (cm(t), 4)),
                  pl.BlockSpec((Q, 256), lambda t: (cm(t), 5)), pl.BlockSpec((Q, 128), lambda t: (cm(t), 0)),
                  pl.BlockSpec((1, 128), lambda t: (0, 0)), pl.BlockSpec((1, Q, 1024), lambda t: (cm(t), 0, 0)),
                  pl.BlockSpec((Q, 1024), lambda t: (jnp.maximum(cm(t) - 2, 0), 0))],
        out_specs=[pl.BlockSpec((Q, 1536), lambda t: (cm(t), 0)), pl.BlockSpec((Q, 128), lambda t: (cm(t), 0)),
                   pl.BlockSpec((1, 128), lambda t: (0, 0))],
        out_shape=[jax.ShapeDtypeStruct((n * Q, 1536), F32), jax.ShapeDtypeStruct((n * Q, 128), F32),
                   jax.ShapeDtypeStruct((1, 128), F32)],
        scratch_shapes=[pltpu.VMEM((Q, 1024), F32)], compiler_params=_cp("arbitrary"))(xbc, xbc, xbc, dt, alog, hs, dy)


def f_norm0(x, g0, b0, sc, sh):
    x0 = _ln(x, g0, b0)
    return x0, x0 * (1.0 + sc) + sh


def f_dt(raw, bias):
    z = split4(raw)[0] + bias
    dt = jnp.maximum(z, 0.0) + jnp.log1p(jnp.exp(-jnp.abs(z)))
    return dt, dt


def f_gated_norm(yf, yb, xs, z, dcol, g):
    h = (yf + yb + xs * dcol) * _silu(z)
    return (h * lax.rsqrt(jnp.mean(h * h, axis=-1, keepdims=True) + LN_EPS) * g,)


def f_gmlp(uv, gmg, gmb, *wb):
    ws, bs = wb[:8], wb[8:]
    u, v = split2(uv)
    vn = split8(_ln(_gelu(v), gmg, gmb))
    mixed = concat8(tuple(mm(ws[g], vn[g]) + bs[g] for g in range(8)))
    return (_gelu(u) * mixed,)


def f_merge(ps, pg, gates, bg):
    gs, gg = split2(jax.nn.sigmoid(gates + bg))
    return (gs * ps + gg * pg,)


def f_res1(x0, out, g1, lg, lb, sc, sh):
    x1 = _ln(ALPHA * x0 + g1 * out, lg, lb)
    return x1, x1 * (1.0 + sc) + sh


def f_res2_loss(x1, ff, tgt, g2, lg, lb):
    x2 = _ln(ALPHA * x1 + g2 * ff, lg, lb)
    e = x2 - tgt
    return (0.5 * jnp.sum(jnp.mean(e * e, axis=-1, keepdims=True), axis=0, keepdims=True),)


def _row_tile(M):
    return 544 if M % 544 == 0 else (512 if M % 512 == 0 else M)


def core(x_all, tgt, mod_x, mod_c, X, S):
    R = x_all.shape[0]
    L = R - LC
    n = R // Q
    T = 256
    nt, ntl = R // T, L // T
    tmR, tmL = _row_tile(R), _row_tile(L)
    tkR = 256 if R % 512 else 512
    tkL = 512 if L % 512 == 0 else 256
    row = lambda v: v.reshape(1, -1)
    mx = [row(mod_x[k]) for k in range(6)]
    mc = [row(mod_c[k]) for k in range(6)]
    sel = lambda i: jnp.minimum(i, 1)
    sc1 = jnp.stack([mc[1], mx[1]])
    sh1 = jnp.stack([mc[0], mx[0]])
    ln0 = [(row(S['ln0_g']), None), (row(S['ln0_b']), None), (sc1, sel), (sh1, sel)]

    x0, xm = stage_fwd("norm0_fwd", f_norm0, T, nt, [(x_all, D, 0, 0)], ln0, [('new', R, D, 0), ('new', R, D, 0, BF16)])
    w_in = X.w_in()
    proj = matmul("proj_fwd", xm, w_in, 'nn', tmR, PW // 2, 1024)
    conv_w8 = jnp.pad(S['conv_w'], ((0, 3), (0, 0)))
    conv_b = row(S['conv_b'])
    xbc = conv_fwd("conv_fwd", proj, conv_w8, conv_b, R)
    dt_bias = jnp.pad(S['dt_bias'].reshape(1, 32), ((0, 0), (0, 96)))
    alog = jnp.pad(S['a_log'].reshape(1, 32), ((0, 0), (0, 96)))
    x_dt = [(proj, 512, C_DT // 512, 0)]
    dt_f, dt_b = stage_fwd("dt_fwd", f_dt, T, nt, x_dt, [(dt_bias, None)], [('new', R, 128, 0), ('new', R, 128, 0)])
    y_f, hs_f = ssd_fwd("ssd_fwd_f", xbc, dt_f, alog, n, False, 0)
    y_b, hs_b = ssd_fwd("ssd_fwd_b", xbc, dt_b, alog, n, True, 1)
    W = X.rest()
    dcol = jnp.repeat(S['d_skip'][0] + S['d_skip'][1], 64).reshape(1, D)
    x_gn = [(y_f, D, 0, 1), (y_b, D, 0, 1), (xbc, D, 0, 1), (proj, D, C_Z // D, 1)]
    p_gn = [(dcol, None), (row(S['ssd_norm_g']), None)]
    (yn,) = stage_fwd("gnorm_fwd", f_gated_norm, T, ntl, x_gn, p_gn, [('new', L, D, 0, BF16)])
    x_gm = [(proj, 2 * D, C_UV // (2 * D), LC // Q)]
    p_gm = ([(row(S['gm_norm_g']), None), (row(S['gm_norm_b']), None)]
            + [(S['w_spatial'][g], None) for g in range(8)] + [(S['b_spatial'][g].reshape(Q, 1), None) for g in range(8)])
    (y_gm,) = stage_fwd("gmlp_fwd", f_gmlp, Q, L // Q, x_gm, p_gm, [('new', L, D, 0, BF16)])
    p_ssd = matmul("pssd_fwd", yn, W['w_ssd_proj'], 'nn', tmL, 1024, 1024)
    p_g = matmul("pgm_fwd", y_gm, W['w_gm_proj'], 'nn', tmL, 1024, 1024)
    x_mg = [(p_ssd, D, 0, 0), (p_g, D, 0, 0), (proj, 2 * D, C_GATE // (2 * D), 1)]
    p_mg = [(row(S['b_gate']), None)]
    (merged,) = stage_fwd("merge_fwd", f_merge, T, ntl, x_mg, p_mg, [('new', L, D, 0, BF16)])
    out = matmul("out_fwd", merged, W['w_out'], 'nn', tmL, 1024, 1024)
    x_r1 = [(x0, D, 0, 1), (out, D, 0, 0)]
    p_r1 = [(mx[2], None), (row(S['ln1_g']), None), (row(S['ln1_b']), None), (mx[4], None), (mx[3], None)]
    x1, hm = stage_fwd("res1_fwd", f_res1, T, ntl, x_r1, p_r1, [('new', L, D, 0), ('new', L, D, 0, BF16)])
    a1, a3, act = ffn_in_fwd("ffn_in_fwd", hm, W['w_ff1'], W['w_ff3'], T)
    ff = ff_out_fwd("ff2_fwd", act, W['w_ff2'], tmL)
    x_r2 = [(x1, D, 0, 0), (ff, D, 0, 0), (tgt, D, 0, 0)]
    p_r2 = [(mx[5], None), (row(S['ln2_g']), None), (row(S['ln2_b']), None)]

    dx1_a, dff, dg2, dl2g, dl2b, loss = stage_bwd(
        "res2_bwd", f_res2_loss, T, ntl, x_r2, p_r2, [1.0],
        [('new', L, D, 0), ('new', L, D, 0, BF16), None], [True, True, True], primal=[(0, (1, 1))])
    da1, da3 = ffn_out_bwd_x("ffn_out_bwd_x", dff, W['w_ff2'], a1, a3, T)
    gw_ff2 = ff_out_bwd_w("ff2_bwd_w", act, dff, tkL)
    dhm = ff_in_bwd_x("ff1_bwd_x", da1, W['w_ff1'], tmL)
    dhm = ff_in_bwd_x("ff3_bwd_x", da3, W['w_ff3'], tmL, add=dhm)
    gw_ff1 = ff_in_bwd_w("ff1_bwd_w", hm, da1, tkL)
    gw_ff3 = ff_in_bwd_w("ff3_bwd_w", hm, da3, tkL)
    X.grads('ffn', {'w_ff2': gw_ff2, 'w_ff1': gw_ff1, 'w_ff3': gw_ff3})
    dx0_a, dout, dg1, dl1g, dl1b, dsc2, dsh2 = stage_bwd(
        "res1_bwd", f_res1, T, ntl, x_r1, p_r1, [(dx1_a, D, 0, 0), (dhm, D, 0, 0)],
        [('new', L, D, 0), ('new', L, D, 0, BF16)], [True] * 5)
    dmerged = matmul("out_bwd_x", dout, W['w_out'], 'nt', tmL, 1024, 1024)
    gw_out = matmul("out_bwd_w", merged, dout, 'tn', 1024, 1024, tkL, BF16)
    lt, lq = -(LC // T), -(LC // Q)
    x_mg_b = [(p_ssd, D, 0, lt), (p_g, D, 0, lt), (proj, 2 * D, C_GATE // (2 * D), 0)]
    dp_ssd, dp_g, dproj, dbg = stage_bwd(
        "merge_bwd", f_merge, T, nt, x_mg_b, p_mg, [(dmerged, D, 0, lt)],
        [('new', L, D, lt, BF16), ('new', L, D, lt, BF16), ('part', R, PW, 2 * D, C_GATE // (2 * D), 0, BF16)], [True])
    dyn = matmul("pssd_bwd_x", dp_ssd, W['w_ssd_proj'], 'nt', tmL, 1024, 1024)
    gw_ssd = matmul("pssd_bwd_w", yn, dp_ssd, 'tn', 1024, 1024, tkL, BF16)
    dy_gm = matmul("pgm_bwd_x", dp_g, W['w_gm_proj'], 'nt', tmL, 1024, 1024)
    gw_gm = matmul("pgm_bwd_w", y_gm, dp_g, 'tn', 1024, 1024, tkL, BF16)
    X.grads('proj', {'w_out': gw_out, 'w_ssd_proj': gw_ssd, 'w_gm_proj': gw_gm})
    r_gm = stage_bwd("gmlp_bwd", f_gmlp, Q, n, [(proj, 2 * D, C_UV // (2 * D), 0)], p_gm, [(dy_gm, D, 0, lq)],
                     [('alias', dproj, 2 * D, C_UV // (2 * D), 0)], [True] * 18)
    dproj, dgmg, dgmb, dws, dbs = r_gm[0], r_gm[1], r_gm[2], r_gm[3:11], r_gm[11:19]
    x_gn_b = [(y_f, D, 0, 0), (y_b, D, 0, 0), (xbc, D, 0, 0), (proj, D, C_Z // D, 0)]
    dy, dskipx, dproj, ddcol, dng = stage_bwd(
        "gnorm_bwd", f_gated_norm, T, nt, x_gn_b, p_gn, [(dyn, D, 0, lt)],
        [('new', L, D, lt), None, ('new', L, D, lt), ('alias', dproj, D, C_Z // D, 0)], [True, True])
    dxbc_f, ddt_f, dal_f = ssd_bwd("ssd_bwd_f", xbc, dt_f, alog, hs_f, dy, n, False, 0)
    dxbc_b, ddt_b, dal_b = ssd_bwd("ssd_bwd_b", xbc, dt_b, alog, hs_b, dy, n, True, 1)
    dproj, ddtb = stage_bwd("dt_bwd", f_dt, T, nt, x_dt, [(dt_bias, None)],
                            [(ddt_f, 128, 0, 0), (ddt_b, 128, 0, 0)],
                            [('alias', dproj, 512, C_DT // 512, 0)], [True])
    dproj, dcw8, dcb = conv_bwd("conv_bwd", proj, conv_w8, conv_b, dxbc_f, dxbc_b, dskipx, dproj, R)
    gw_in = matmul("proj_bwd_w", xm, dproj, 'tn', 1024, PW // 2, tkR, BF16)
    X.grads('in', {'w_in': gw_in})
    dxm = matmul("proj_bwd_x", dproj, w_in, 'nt', R // 2 if R % 16 == 0 else R, 1024, 1024)
    grad_x, dl0g, dl0b, dsc1, dsh1 = stage_bwd(
        "norm0_bwd", f_norm0, T, nt, [(x_all, D, 0, 0)], ln0, [(dx0_a, D, 0, -1), (dxm, D, 0, 0)],
        [('new', L, D, -1)], [True] * 4)

    zero = jnp.zeros((D,), F32)
    flat = lambda v: v.reshape(-1)
    small = {
        'loss': flat(loss), 'ln0_g': flat(dl0g), 'ln0_b': flat(dl0b),
        'dmod_x': jnp.concatenate([flat(dsh1[1]), flat(dsc1[1]), flat(dg1), flat(dsh2), flat(dsc2), flat(dg2)]),
        'dmod_c': jnp.concatenate([flat(dsh1[0]), flat(dsc1[0]), zero, zero, zero, zero]),
        'conv_w': flat(dcw8[:5]), 'conv_b': flat(dcb), 'dt_bias': flat(ddtb[:, :32]),
        'a_log': flat((dal_f + dal_b)[:, :32]),
        'd_skip': flat(jnp.tile(ddcol.reshape(1, NH, 64).sum(-1), (2, 1))),
        'ssd_norm_g': flat(dng), 'gm_norm_g': flat(dgmg), 'gm_norm_b': flat(dgmb),
        'w_spatial': flat(jnp.stack(dws)), 'b_spatial': flat(jnp.stack(dbs)), 'b_gate': flat(dbg),
        'ln1_g': flat(dl1g), 'ln1_b': flat(dl1b), 'ln2_g': flat(dl2g), 'ln2_b': flat(dl2b),
    }
    return grad_x, small


def _place():
    return lax.axis_index("x"), lax.axis_index("y"), lax.axis_index("c")


def allgather8(name, blk, hbm):
    space = pl.ANY if hbm else pltpu.VMEM

    def body(x_ref, out_ref, send_sems, recv_sems, local_sem):
        x, y, c = _place()
        me, sibling = (x, y, c), (x, y, 1 - c)
        chips = [(1 - x, y), (x, 1 - y), (1 - x, 1 - y)]

        def slot(px, py, pc):
            return out_ref.at[4 * px + 2 * py + pc]

        def copy(k, block, to, src=None):
            return pltpu.make_async_remote_copy(
                src_ref=slot(*block) if src is None else src, dst_ref=slot(*block),
                send_sem=send_sems.at[k], recv_sem=recv_sems.at[k], device_id=to, device_id_type=MESH)

        mine = pltpu.make_async_copy(x_ref, slot(*me), local_sem)
        mine.start()
        first = [copy(0, me, sibling, src=x_ref)]
        first += [copy(1 + j, me, (*chip, c), src=x_ref) for j, chip in enumerate(chips)]
        for cp in first:
            cp.start()
        passed = [copy(4 + j, (*chip, c), sibling) for j, chip in enumerate(chips)]
        for j, chip in enumerate(chips):
            copy(1 + j, (*chip, c), me).wait_recv()
            passed[j].start()
        copy(0, sibling, me).wait_recv()
        for j, chip in enumerate(chips):
            copy(4 + j, (*chip, 1 - c), me).wait_recv()
        for cp in first + passed:
            cp.wait_send()
        mine.wait()

    return pl.pallas_call(
        body, name=name, out_shape=jax.ShapeDtypeStruct((8,) + blk.shape, blk.dtype),
        in_specs=[pl.BlockSpec(memory_space=space)], out_specs=pl.BlockSpec(memory_space=space),
        scratch_shapes=[pltpu.SemaphoreType.DMA((7,)), pltpu.SemaphoreType.DMA((7,)), pltpu.SemaphoreType.DMA],
        compiler_params=pltpu.CompilerParams(vmem_limit_bytes=VMEM_LIMIT_V7X))(blk)


def _peers(place):
    x, y, c = place
    return [((1 - x) if k & 4 else x, (1 - y) if k & 2 else y, (1 - c) if k & 1 else c) for k in range(1, 8)]


def _slot(p):
    return 4 * p[0] + 2 * p[1] + p[2]


def plan_gather(place, srcs, lands):
    remote = [(s, l.at[_slot(place)], to) for s, l in zip(srcs, lands) for to in _peers(place)]
    return remote, [(s, l.at[_slot(place)]) for s, l in zip(srcs, lands)]


def plan_to_owner(place, srcs, lands):
    remote = [(s.at[2 * to[0] + to[1], to[2]], l.at[_slot(place)], to) for s, l in zip(srcs, lands) for to in _peers(place)]
    x, y, c = place
    return remote, [(s.at[2 * x + y, c], l.at[_slot(place)]) for s, l in zip(srcs, lands)]


def sequencer_exchange(name, collective_id, srcs, land_shapes, plan):
    n = len(srcs)
    src_refs = [jax.new_ref(a, memory_space=pltpu.MemorySpace.HBM) for a in srcs]
    land_refs = [jax.empty_ref(s, memory_space=pltpu.MemorySpace.HBM) for s in land_shapes]

    @pl.kernel(mesh=plsc.ScalarSubcoreMesh(axis_name="sequencer", num_cores=1), name=name,
               scratch_types=(pltpu.SemaphoreType.DMA((7 * n,)), pltpu.SemaphoreType.DMA((7 * n,)),
                              pltpu.SemaphoreType.DMA((n,))),
               compiler_params=pltpu.CompilerParams(collective_id=collective_id))
    def launch(send_sems, recv_sems, local_sems):
        place = _place()
        barrier = pltpu.get_barrier_semaphore()
        for to in _peers(place):
            pl.semaphore_signal(barrier, inc=1, device_id=to, device_id_type=MESH)
        pl.semaphore_wait(barrier, 7)
        remote, local = plan(place, src_refs, land_refs)
        mine = [pltpu.make_async_copy(s, d, local_sems.at[a]) for a, (s, d) in enumerate(local)]
        for cp in mine:
            cp.start()
        cps = [pltpu.make_async_remote_copy(src_ref=s, dst_ref=d, send_sem=send_sems.at[k], recv_sem=recv_sems.at[k],
                                            device_id=to, device_id_type=MESH) for k, (s, d, to) in enumerate(remote)]
        for cp in cps:
            cp.start()
        for cp in mine:
            cp.wait()
        for cp in cps:
            cp.wait()

    launch()
    return land_refs


def sibling_pair(name, hs):
    n = len(hs)

    def body(*refs):
        ins, outs = refs[:n], refs[n:2 * n]
        send_sems, recv_sems = refs[2 * n:]
        x, y, c = _place()
        cps = [pltpu.make_async_remote_copy(src_ref=outs[a].at[c], dst_ref=outs[a].at[c], send_sem=send_sems.at[a],
                                            recv_sem=recv_sems.at[a], device_id=(x, y, 1 - c), device_id_type=MESH)
               for a in range(n)]
        for cp in cps:
            cp.start()
        for a in range(n):
            pltpu.make_async_remote_copy(src_ref=outs[a].at[1 - c], dst_ref=outs[a].at[1 - c], send_sem=send_sems.at[a],
                                         recv_sem=recv_sems.at[a], device_id=(x, y, 1 - c),
                                         device_id_type=MESH).wait_recv()
        for cp in cps:
            cp.wait_send()

    any_spec = pl.BlockSpec(memory_space=pl.ANY)
    return pl.pallas_call(
        body, name=name, out_shape=[jax.ShapeDtypeStruct(h.shape, h.dtype) for h in hs],
        in_specs=[any_spec] * n, out_specs=[any_spec] * n, input_output_aliases={a: a for a in range(n)},
        scratch_shapes=[pltpu.SemaphoreType.DMA((n,)), pltpu.SemaphoreType.DMA((n,))])(*hs)


def owner_sum(name, land):
    _, r, w = land.shape
    T = r // 2

    def body(_, l_ref, o_ref):
        acc = l_ref[0].astype(F32)
        for j in range(1, 8):
            acc = acc + l_ref[j].astype(F32)
        o_ref[...] = acc

    grid_spec = pltpu.PrefetchScalarGridSpec(
        num_scalar_prefetch=1, grid=(2,),
        in_specs=[pl.BlockSpec((8, T, w), lambda i, at: (0, i, 0))],
        out_specs=pl.BlockSpec((None, T, w), lambda i, at: (at[0], i, 0)))
    at = jnp.stack([lax.axis_index("c")]).astype(jnp.int32)
    return pl.pallas_call(body, name=name, grid_spec=grid_spec, out_shape=jax.ShapeDtypeStruct((2, r, w), F32),
                          compiler_params=_cp("arbitrary"))(at, land)


W_IN_RUNS = ((0, 2, 1296, 376), (376, 3, 0, 1672), (2048, 1, 920, 752), (2800, 2, 0, 1296), (4096, 0, 0, 1024),
             (5120, 0, 1024, 648), (5768, 1, 0, 920))


def w_in_to_padded(name, g4):
    T = 128

    def body(g_ref, o_ref):
        o_ref[:, D_PROJ:PW] = jnp.zeros((T, PW - D_PROJ), o_ref.dtype)
        for (a, s, j0, w) in W_IN_RUNS:
            o_ref[:, a:a + w] = g_ref[s, :, j0:j0 + w]

    return pl.pallas_call(body, name=name, grid=(D // T,), in_specs=[pl.BlockSpec((4, T, 1672), lambda i: (0, i, 0))],
                          out_specs=pl.BlockSpec((T, PW), lambda i: (i, 0)),
                          out_shape=jax.ShapeDtypeStruct((D, PW), g4.dtype), compiler_params=_cp("arbitrary"))(g4)


def w_in_from_padded(name, gp):
    T = 128

    def body(g_ref, o_ref):
        for (a, s, j0, w) in W_IN_RUNS:
            o_ref[s, :, j0:j0 + w] = g_ref[:, a:a + w]

    return pl.pallas_call(body, name=name, grid=(D // T,), in_specs=[pl.BlockSpec((T, PW), lambda i: (i, 0))],
                          out_specs=pl.BlockSpec((4, T, 1672), lambda i: (0, i, 0)),
                          out_shape=jax.ShapeDtypeStruct((4, D, 1672), gp.dtype), compiler_params=_cp("arbitrary"))(gp)


def sum_devices(name, g):
    def body(g_ref, o_ref):
        acc = g_ref[0]
        for k in range(1, 8):
            acc = acc + g_ref[k]
        o_ref[...] = acc

    return pl.pallas_call(body, name=name, out_shape=jax.ShapeDtypeStruct(g.shape[1:], F32),
                          compiler_params=pltpu.CompilerParams(vmem_limit_bytes=VMEM_LIMIT_V7X))(g)


def adamw(name, w, g, m, v, T):
    r, wd = w.shape
    c1 = 1.0 - ADAM_B1 ** ADAM_STEP
    c2 = 1.0 - ADAM_B2 ** ADAM_STEP

    def body(w_ref, g_ref, m_ref, v_ref, d_ref, mo_ref, vo_ref):
        gv = g_ref[...]
        mn = ADAM_B1 * m_ref[...] + (1.0 - ADAM_B1) * gv
        vn = ADAM_B2 * v_ref[...] + (1.0 - ADAM_B2) * (gv * gv)
        d_ref[...] = -ADAM_LR * ((mn / c1) / (jnp.sqrt(vn / c2) + ADAM_EPS) + ADAM_WD * w_ref[...])
        mo_ref[...] = mn
        vo_ref[...] = vn

    spec = pl.BlockSpec((T, wd), lambda i: (i, 0))
    return pl.pallas_call(body, name=name, grid=(r // T,), in_specs=[spec] * 4, out_specs=[spec] * 3,
                          out_shape=[jax.ShapeDtypeStruct((r, wd), F32)] * 3, compiler_params=_cp("arbitrary"))(w, g, m, v)


BIG = {'w_in': (1024, 1672), 'w_ssd_proj': (256, 1024), 'w_gm_proj': (256, 1024), 'w_out': (256, 1024),
       'w_ff1': (1024, 704), 'w_ff3': (1024, 704), 'w_ff2': (704, 1024)}


class Flat:
    def __init__(self, segs):
        self.off, o = {}, 0
        for name, size in segs:
            self.off[name] = (o, size)
            o += -(-size // 128) * 128
        self.rows = -(-o // 1024) * 8

    def pack(self, vals):
        parts = []
        for name, (o, size) in self.off.items():
            v = vals[name].reshape(-1).astype(F32)
            parts.append(jnp.pad(v, (0, -(-size // 128) * 128 - size)))
        buf = jnp.concatenate(parts)
        return jnp.pad(buf, (0, self.rows * 128 - buf.shape[0])).reshape(self.rows, 128)

    def get(self, buf, name, shape=None):
        o, size = self.off[name]
        v = buf[o // 128:(o + size + 127) // 128].reshape(-1)[:size]
        return v if shape is None else v.reshape(shape)


PARTIALS = Flat([('loss', 1), ('ln0_g', D), ('ln0_b', D), ('dmod_x', 6 * D), ('dmod_c', 6 * D), ('conv_w', 5 * 1536),
                 ('conv_b', 1536), ('dt_bias', 32), ('a_log', 32), ('d_skip', 32), ('ssd_norm_g', D),
                 ('gm_norm_g', D), ('gm_norm_b', D), ('w_spatial', 8 * Q * Q), ('b_spatial', 8 * Q), ('b_gate', 2 * D),
                 ('ln1_g', D), ('ln1_b', D), ('ln2_g', D), ('ln2_b', D)])

WEIGHTS = ('c_ctx', 'ln0_g', 'ln0_b', 'w_ada', 'b_ada', 'w_in', 'conv_w', 'conv_b', 'dt_bias', 'a_log', 'd_skip',
           'ssd_norm_g', 'gm_norm_g', 'gm_norm_b', 'w_spatial', 'b_spatial', 'b_gate', 'w_ssd_proj', 'w_gm_proj',
           'w_out', 'ln1_g', 'ln1_b', 'w_ff1', 'w_ff3', 'w_ff2', 'ln2_g', 'ln2_b')
BIG_NAMES = tuple(BIG)
SMALL_NAMES = tuple(n for n in WEIGHTS if n not in BIG_NAMES and n != 'w_ada')


def kernel(x, c, ctx, c_ctx, ln0_g, ln0_b, w_ada, b_ada, w_in, conv_w, conv_b, dt_bias, a_log, d_skip, ssd_norm_g, gm_norm_g, gm_norm_b, w_spatial, b_spatial, b_gate, w_ssd_proj, w_gm_proj, w_out, ln1_g, ln1_b, w_ff1, w_ff3, w_ff2, ln2_g, ln2_b, loss_target, m_c_ctx, m_ln0_g, m_ln0_b, m_w_ada, m_b_ada, m_w_in, m_conv_w, m_conv_b, m_dt_bias, m_a_log, m_d_skip, m_ssd_norm_g, m_gm_norm_g, m_gm_norm_b, m_w_spatial, m_b_spatial, m_b_gate, m_w_ssd_proj, m_w_gm_proj, m_w_out, m_ln1_g, m_ln1_b, m_w_ff1, m_w_ff3, m_w_ff2, m_ln2_g, m_ln2_b, v_c_ctx, v_ln0_g, v_ln0_b, v_w_ada, v_b_ada, v_w_in, v_conv_w, v_conv_b, v_dt_bias, v_a_log, v_d_skip, v_ssd_norm_g, v_gm_norm_g, v_gm_norm_b, v_w_spatial, v_b_spatial, v_b_gate, v_w_ssd_proj, v_w_gm_proj, v_w_out, v_ln1_g, v_ln1_b, v_w_ff1, v_w_ff3, v_w_ff2, v_ln2_g, v_ln2_b):
    wts = dict(c_ctx=c_ctx, ln0_g=ln0_g, ln0_b=ln0_b, w_ada=w_ada, b_ada=b_ada, w_in=w_in, conv_w=conv_w, conv_b=conv_b,
               dt_bias=dt_bias, a_log=a_log, d_skip=d_skip, ssd_norm_g=ssd_norm_g, gm_norm_g=gm_norm_g,
               gm_norm_b=gm_norm_b, w_spatial=w_spatial, b_spatial=b_spatial, b_gate=b_gate, w_ssd_proj=w_ssd_proj,
               w_gm_proj=w_gm_proj, w_out=w_out, ln1_g=ln1_g, ln1_b=ln1_b, w_ff1=w_ff1, w_ff3=w_ff3, w_ff2=w_ff2,
               ln2_g=ln2_g, ln2_b=ln2_b)
    ms = dict(zip(WEIGHTS, (m_c_ctx, m_ln0_g, m_ln0_b, m_w_ada, m_b_ada, m_w_in, m_conv_w, m_conv_b, m_dt_bias, m_a_log,
                            m_d_skip, m_ssd_norm_g, m_gm_norm_g, m_gm_norm_b, m_w_spatial, m_b_spatial, m_b_gate,
                            m_w_ssd_proj, m_w_gm_proj, m_w_out, m_ln1_g, m_ln1_b, m_w_ff1, m_w_ff3, m_w_ff2, m_ln2_g,
                            m_ln2_b)))
    vs = dict(zip(WEIGHTS, (v_c_ctx, v_ln0_g, v_ln0_b, v_w_ada, v_b_ada, v_w_in, v_conv_w, v_conv_b, v_dt_bias, v_a_log,
                            v_d_skip, v_ssd_norm_g, v_gm_norm_g, v_gm_norm_b, v_w_spatial, v_b_spatial, v_b_gate,
                            v_w_ssd_proj, v_w_gm_proj, v_w_out, v_ln1_g, v_ln1_b, v_w_ff1, v_w_ff3, v_w_ff2, v_ln2_g,
                            v_ln2_b)))
    px, py, pc = _place()
    shard = 2 * px + py
    dev = 2 * shard + pc
    take = lambda a, i, axis=0: lax.dynamic_index_in_dim(a, i, axis, keepdims=False)

    half = lambda n: take(wts[n][0].reshape(2, BIG[n][0] // 2, BIG[n][1]), pc).astype(BF16)

    pre = jnp.concatenate([c, jnp.pad(conv_w[0], ((0, 0), (0, D - 384))), jnp.zeros((2, D), F32)], axis=0)
    pre = allgather8("gather_cond", pre, False)
    conv_w_full = pre[0::2, 1:6, :384].transpose(1, 0, 2).reshape(5, 1536)
    a16 = jnp.concatenate([_silu(pre[:, 0, :]), _silu(c_ctx)[None], jnp.zeros((7, D), F32)], axis=0)
    mod = matmul("ada_fwd", a16, w_ada[0], 'nn', 16, 512, 1024)
    mod = mod + lax.dynamic_slice_in_dim(b_ada[0], shard * 1536, 1536)[None]
    mod = allgather8("gather_mod", mod, False)
    mod = jnp.concatenate([mod[0], mod[2], mod[4], mod[6]], axis=1)
    mod_x = take(mod, dev).reshape(6, D)
    mod_c = mod[8].reshape(6, D)

    def full(n, blocks):
        r, w = BIG[n]
        return blocks.reshape(4, r, w) if w != D else blocks.reshape(4 * r, w)

    class Exchanges:
        rest_names = BIG_NAMES[1:]

        def __init__(self):
            self.pending = []

        def w_in(self):
            blocks = allgather8("gather_w_in", half('w_in'), True)
            w = w_in_to_padded("w_in_layout", full('w_in', blocks))
            halves = [half(n) for n in self.rest_names]
            halves[0], _ = lax.optimization_barrier((halves[0], blocks))
            lands = [jax.ShapeDtypeStruct((8,) + h.shape, BF16) for h in halves]
            self.rest_refs = sequencer_exchange("gather_rest", 1, halves, lands, plan_gather)
            return w

        def rest(self):
            return {n: full(n, r[...]) for n, r in zip(self.rest_names, self.rest_refs)}

        def grads(self, group, gs):
            if group == 'in':
                gs = {'w_in': w_in_from_padded("w_in_grad_layout", gs['w_in'])}
            blocks = [g.reshape(4, 2, BIG[n][0] // 2, BIG[n][1]) for n, g in gs.items()]
            lands = [jax.ShapeDtypeStruct((8,) + b.shape[2:], BF16) for b in blocks]
            refs = sequencer_exchange("grads_" + group, 2 + len(self.pending), blocks, lands, plan_to_owner)
            self.pending.append((tuple(gs), refs))

        def finish(self):
            names, halves = [], []
            for ns, refs in self.pending:
                names += ns
                halves += [owner_sum("grads_sum_" + n, r[...]) for n, r in zip(ns, refs)]
            return {n: h.reshape(BIG[n]) for n, h in zip(names, sibling_pair("grads_halves", halves))}

    S = dict(ln0_g=ln0_g, ln0_b=ln0_b, conv_w=conv_w_full, conv_b=conv_b[0], dt_bias=dt_bias[0], a_log=a_log[0],
             d_skip=d_skip[0], ssd_norm_g=ssd_norm_g[0], gm_norm_g=gm_norm_g[0], gm_norm_b=gm_norm_b[0],
             w_spatial=w_spatial[0], b_spatial=b_spatial[0], b_gate=b_gate[0], ln1_g=ln1_g[0], ln1_b=ln1_b[0],
             ln2_g=ln2_g[0], ln2_b=ln2_b[0])
    x_all = jnp.concatenate([ctx[0], x[0]], axis=0)
    exchanges = Exchanges()
    grad_x, gsmall = core(x_all, loss_target[0], mod_x, mod_c, exchanges, S)

    parts = allgather8("gather_partials", PARTIALS.pack(gsmall), False)
    tot = sum_devices("partials_sum", parts)
    g_shards = exchanges.finish()
    g = {n: PARTIALS.get(tot, n) for n in ('ln0_g', 'ln0_b', 'conv_b', 'dt_bias', 'a_log', 'd_skip', 'ssd_norm_g',
                                           'gm_norm_g', 'gm_norm_b', 'w_spatial', 'b_spatial', 'b_gate', 'ln1_g',
                                           'ln1_b', 'ln2_g', 'ln2_b')}
    loss = PARTIALS.get(tot, 'loss', ())
    dmod_c = PARTIALS.get(tot, 'dmod_c')
    g['b_ada'] = PARTIALS.get(tot, 'dmod_x') + dmod_c
    g['conv_w'] = lax.dynamic_slice_in_dim(PARTIALS.get(tot, 'conv_w', (5, 1536)), shard * 384, 384, axis=1)
    o, size = PARTIALS.off['dmod_x']
    dmod_rows = parts[:, o // 128:(o + size) // 128].reshape(8, size)
    dm = jnp.concatenate([dmod_rows, dmod_c[None], jnp.zeros((7, 6 * D), F32)], axis=0)
    dm = lax.dynamic_slice_in_dim(dm, shard * 1536, 1536, axis=1)
    g['w_ada'] = matmul("ada_bwd_w", a16, dm, 'tn', 1024, 512, 16)
    dm_c = jnp.concatenate([dm[8:9], jnp.zeros((15, 1536), F32)], axis=0)
    dc = matmul("ada_bwd_c", dm_c, w_ada[0], 'nt', 16, 1024, 512)
    dc = allgather8("gather_dcctx", dc, False)[:, 0, :]
    dc = ((dc[0] + dc[2]) + dc[4]) + dc[6]
    sg = jax.nn.sigmoid(c_ctx)
    g['c_ctx'] = dc * (sg * (1.0 + c_ctx * (1.0 - sg)))
    for n in BIG_NAMES:
        g[n] = g_shards[n]

    delta, new_m, new_v = {}, {}, {}
    for n in BIG_NAMES + ('w_ada',):
        w2 = wts[n][0]
        T = 352 if n == 'w_ff2' else 256
        d_, m_, v_ = adamw("adamw_" + n, w2, g[n], ms[n][0], vs[n][0], T)
        delta[n], new_m[n], new_v[n] = d_, m_, v_
    lay = Flat([(n, wts[n].size) for n in SMALL_NAMES])
    d_, m_, v_ = adamw("adamw_small", lay.pack(wts), lay.pack(g), lay.pack(ms), lay.pack(vs), lay.rows)
    for n in SMALL_NAMES:
        delta[n], new_m[n], new_v[n] = (lay.get(b, n) for b in (d_, m_, v_))

    shp = lambda d: [d[n].reshape(wts[n].shape) for n in WEIGHTS]
    return (loss, grad_x[None], *shp(g), *shp(delta), *shp(new_m), *shp(new_v))
```

```python
import functools

import jax
import jax.numpy as jnp
from jax import lax
from jax.experimental import pallas as pl
from jax.experimental.pallas import tpu as pltpu
from jax.experimental.pallas import tpu_sc as plsc

F32 = jnp.float32
BF16 = jnp.bfloat16
MESH = pl.DeviceIdType.MESH

VMEM_LIMIT_V7X = 56 * 1024 * 1024

D = 1024
LC = 256
Q = 128
NH = 16
D_FF = 2816
LN_EPS = 1e-5
ALPHA = 2.0 ** 0.25

PW = 7168
C_GATE, C_UV, C_Z, C_XBC, C_DT = 0, 2048, 4096, 5120, 6656
D_PROJ = 6688

ADAM_LR, ADAM_B1, ADAM_B2, ADAM_EPS, ADAM_WD, ADAM_STEP = 0.001, 0.9, 0.999, 1e-08, 0.01, 10


def _cp(*sem):
    return pltpu.CompilerParams(dimension_semantics=sem, vmem_limit_bytes=VMEM_LIMIT_V7X)


def _dot(a, b, ca, cb):
    return lax.dot_general(a.astype(BF16), b.astype(BF16), (((ca,), (cb,)), ((), ())),
                           preferred_element_type=F32)


@jax.custom_vjp
def mm(a, b):
    return _dot(a, b, 1, 0)


mm.defvjp(lambda a, b: (_dot(a, b, 1, 0), (a, b)),
          lambda r, g: (_dot(g, r[1], 1, 1), _dot(r[0], g, 0, 0)))


@jax.custom_vjp
def mm_nt(a, b):
    return _dot(a, b, 1, 1)


mm_nt.defvjp(lambda a, b: (_dot(a, b, 1, 1), (a, b)),
             lambda r, g: (_dot(g, r[1], 1, 0), _dot(g, r[0], 0, 0)))


@jax.custom_vjp
def mm_tn(a, b):
    return _dot(a, b, 0, 0)


mm_tn.defvjp(lambda a, b: (_dot(a, b, 0, 0), (a, b)),
             lambda r, g: (_dot(r[1], g, 1, 1), _dot(r[0], g, 1, 0)))


def _dot32(a, b):
    return lax.dot_general(a, b, (((1,), (0,)), ((), ())), precision=lax.Precision.HIGHEST,
                           preferred_element_type=F32)


def _cumsum_fn(rev):
    def tri(transpose):
        r = lax.broadcasted_iota(jnp.int32, (Q, Q), 0)
        c = lax.broadcasted_iota(jnp.int32, (Q, Q), 1)
        keep = (r >= c) if (rev == transpose) else (r <= c)
        return jnp.where(keep, 1.0, 0.0).astype(F32)

    @jax.custom_vjp
    def cums(a):
        return _dot32(tri(False), a)

    cums.defvjp(lambda a: (_dot32(tri(False), a), None), lambda _, g: (_dot32(tri(True), g),))
    return cums


def _cols(v, k):
    w = v.shape[1] // k
    return tuple(v[:, w * i:w * (i + 1)] for i in range(k))


def _splitter(k):
    @jax.custom_vjp
    def split(v):
        return _cols(v, k)

    @jax.custom_vjp
    def concat(ps):
        return jnp.concatenate(ps, axis=1)

    split.defvjp(lambda v: (_cols(v, k), None), lambda _, g: (jnp.concatenate(g, axis=1),))
    concat.defvjp(lambda ps: (jnp.concatenate(ps, axis=1), None), lambda _, g: (_cols(g, k),))
    return split, concat


split2, _ = _splitter(2)
split4, _ = _splitter(4)
split8, concat8 = _splitter(8)


def _ln(x, g, b):
    mu = jnp.mean(x, axis=-1, keepdims=True)
    xc = x - mu
    var = jnp.mean(xc * xc, axis=-1, keepdims=True)
    return xc * lax.rsqrt(var + LN_EPS) * g + b


def _silu(x):
    return x * jax.nn.sigmoid(x)


def _gelu(x):
    return 0.5 * x * (1.0 + jnp.tanh(0.7978845608028654 * (x + 0.044715 * (x * x * x))))


def _xspec(T, w, col, roff):
    return pl.BlockSpec((T, w), lambda i, col=col, roff=roff: (jnp.maximum(i + roff, 0), col))


def _pspec(p, sel):
    if sel is None:
        return pl.BlockSpec(p.shape, lambda i, n=p.ndim: (0,) * n)
    return pl.BlockSpec((1,) + p.shape[1:], lambda i, n=p.ndim: (sel(i),) + (0,) * (n - 1))


def _out_plumbing(outs, T, args, in_specs):
    shapes, specs, aliases = [], [], {}
    for k, o in enumerate(outs):
        if o[0] == 'new':
            _, rows, w, roff = o[:4]
            shapes.append(jax.ShapeDtypeStruct((rows, w), o[4] if len(o) > 4 else F32))
            specs.append(_xspec(T, w, 0, roff))
        elif o[0] == 'acc':
            shapes.append(jax.ShapeDtypeStruct(o[1], F32))
            specs.append(pl.BlockSpec(o[1], lambda i, n=len(o[1]): (0,) * n))
        elif o[0] == 'part':
            _, rows, wtot, w, col, roff, dtype = o
            shapes.append(jax.ShapeDtypeStruct((rows, wtot), dtype))
            specs.append(_xspec(T, w, col, roff))
        else:
            _, arr, w, col, roff = o
            aliases[len(args)] = k
            args.append(arr)
            in_specs.append(pl.BlockSpec(memory_space=pl.ANY))
            shapes.append(jax.ShapeDtypeStruct(arr.shape, arr.dtype))
            specs.append(_xspec(T, w, col, roff))
    return shapes, specs, aliases


def stage_fwd(name, f, T, n, xs, ps, outs):
    nx, npar = len(xs), len(ps)
    args = [x[0] for x in xs] + [p[0] for p in ps]
    in_specs = [_xspec(T, w, col, roff) for (_, w, col, roff) in xs] + [_pspec(p, sel) for (p, sel) in ps]
    n_in = len(args)
    shapes, specs, aliases = _out_plumbing(outs, T, args, in_specs)
    n_all_in = len(args)

    def body(*refs):
        i = pl.program_id(0)
        xv = [r[...] for r in refs[:nx]]
        pv = [r[...] if ps[k][1] is None else r[0] for k, r in enumerate(refs[nx:n_in])]
        res = f(*xv, *pv)
        for k, o_ref in enumerate(refs[n_all_in:]):
            if outs[k][0] == 'acc':
                @pl.when(i == 0)
                def _(o_ref=o_ref, v=res[k]):
                    o_ref[...] = v

                @pl.when(i > 0)
                def _(o_ref=o_ref, v=res[k]):
                    o_ref[...] += v
            else:
                o_ref[...] = res[k].astype(o_ref.dtype)

    return pl.pallas_call(body, name=name, grid=(n,), in_specs=in_specs, out_specs=specs, out_shape=shapes,
                          input_output_aliases=aliases, compiler_params=_cp("arbitrary"))(*args)


def stage_bwd(name, f, T, n, xs, ps, cts, dxs, dps, primal=()):
    nx, npar = len(xs), len(ps)
    args = [x[0] for x in xs] + [p[0] for p in ps]
    in_specs = [_xspec(T, w, col, roff) for (_, w, col, roff) in xs] + [_pspec(p, sel) for (p, sel) in ps]
    ct_arrs = [c for c in cts if isinstance(c, tuple)]
    for (a, w, col, roff) in ct_arrs:
        args.append(a)
        in_specs.append(_xspec(T, w, col, roff))
    n_in = len(args)
    outs, out_of = [], []
    for k, o in enumerate(dxs):
        if o is not None:
            outs.append(o)
            out_of.append(('x', k))
    for k, want in enumerate(dps):
        if want:
            p, sel = ps[k]
            outs.append(('acc', p.shape))
            out_of.append(('p', k))
    for k, shape in primal:
        outs.append(('acc', shape))
        out_of.append(('r', k))
    shapes, specs, aliases = _out_plumbing(outs, T, args, in_specs)
    for j, (kind, k) in enumerate(out_of):
        if kind == 'p' and ps[k][1] is not None:
            p, sel = ps[k]
            specs[j] = pl.BlockSpec((1,) + p.shape[1:], lambda i, n=p.ndim, sel=sel: (sel(i),) + (0,) * (n - 1))
    n_all_in = len(args)

    def body(*refs):
        i = pl.program_id(0)
        xv = [r[...] for r in refs[:nx]]
        pv = [r[...] if ps[k][1] is None else r[0] for k, r in enumerate(refs[nx:nx + npar])]
        res, vjp_fn = jax.vjp(f, *xv, *pv)
        ctv, q = [], nx + npar
        for k, c in enumerate(cts):
            if c is None:
                ctv.append(jnp.zeros_like(res[k]))
            elif isinstance(c, tuple):
                v = refs[q][...]
                if c[3] < 0:
                    v = v * (i + c[3] >= 0).astype(F32)
                ctv.append(v)
                q += 1
            else:
                ctv.append(jnp.full_like(res[k], c))
        grads = vjp_fn(tuple(ctv))
        for j, o_ref in enumerate(refs[n_all_in:]):
            kind, k = out_of[j]
            if kind == 'x':
                o_ref[...] = grads[k].astype(o_ref.dtype)
            else:
                g = res[k] if kind == 'r' else grads[nx + k]
                sel = None if kind == 'r' else ps[k][1]
                if sel is None:
                    first = i == 0
                    tgt = o_ref
                else:
                    first = jnp.logical_or(i == 0, sel(i) != sel(jnp.maximum(i - 1, 0)))
                    tgt = o_ref.at[0]

                @pl.when(first)
                def _(tgt=tgt, g=g):
                    tgt[...] = g

                @pl.when(jnp.logical_not(first))
                def _(tgt=tgt, g=g):
                    tgt[...] += g

    return pl.pallas_call(body, name=name, grid=(n,), in_specs=in_specs, out_specs=specs, out_shape=shapes,
                          input_output_aliases=aliases, compiler_params=_cp("arbitrary"))(*args)


_CONTRACT = {'nn': (1, 0), 'nt': (1, 1), 'tn': (0, 0)}


def matmul(name, a, b, mode, tm, tn, tk, out_dtype=F32, add=None):
    if mode == 'nn':
        (M, K), (_, N) = a.shape, b.shape
    elif mode == 'nt':
        (M, K), (N, _) = a.shape, b.shape
    else:
        (K, M), (_, N) = a.shape, b.shape
    assert M % tm == 0 and N % tn == 0 and K % tk == 0, (name, M, N, K, tm, tn, tk)
    a_spec = (pl.BlockSpec((tk, tm), lambda j, i, k: (k, i)) if mode == 'tn'
              else pl.BlockSpec((tm, tk), lambda j, i, k: (i, k)))
    b_spec = (pl.BlockSpec((tn, tk), lambda j, i, k: (j, k)) if mode == 'nt'
              else pl.BlockSpec((tk, tn), lambda j, i, k: (k, j)))
    o_spec = pl.BlockSpec((tm, tn), lambda j, i, k: (i, j))
    return matmul_call(name, (N // tn, M // tm, K // tk), a, a_spec, b, b_spec, (M, N), o_spec, (tm, tn), mode,
                       out_dtype, add)


def matmul_call(name, grid, a, a_spec, b, b_spec, out_shape, o_spec, tile, mode, out_dtype=F32, add=None):
    tm, tn = tile
    nk = grid[2]
    ca, cb = _CONTRACT[mode]
    args, in_specs = [a, b], [a_spec, b_spec]
    if add is not None:
        args.append(add)
        in_specs.append(o_spec)

    def body(*refs):
        a_ref, b_ref = refs[0], refs[1]
        o_ref, acc = refs[-2], refs[-1]
        k = pl.program_id(2)
        if nk == 1:
            p = _dot(a_ref[...], b_ref[...], ca, cb)
            o_ref[...] = (p + refs[2][...] if add is not None else p).astype(out_dtype)
            return

        @pl.when(k == 0)
        def _():
            acc[...] = refs[2][...] if add is not None else jnp.zeros_like(acc)

        acc[...] += _dot(a_ref[...], b_ref[...], ca, cb)

        @pl.when(k == nk - 1)
        def _():
            o_ref[...] = acc[...].astype(out_dtype)

    return pl.pallas_call(body, name=name, grid=grid, in_specs=in_specs, out_specs=o_spec,
                          out_shape=jax.ShapeDtypeStruct(out_shape, out_dtype),
                          scratch_shapes=[pltpu.VMEM((tm, tn) if nk > 1 else (8, 128), F32)],
                          compiler_params=_cp("arbitrary", "arbitrary", "arbitrary"))(*args)


NS, WS = 4, 704


def _resident(name, M, tm, rows, weight, out_shape, out_block, out_map, step, add=None):
    args = [rows[0], weight] + ([] if add is None else [add])
    in_specs = [pl.BlockSpec(rows[1], rows[2]), pl.BlockSpec(weight.shape, lambda i, n=weight.ndim: (0,) * n)]
    if add is not None:
        in_specs.append(pl.BlockSpec(out_block, out_map))
    return pl.pallas_call(step, name=name, grid=(M // tm,), in_specs=in_specs, out_specs=pl.BlockSpec(out_block, out_map),
                          out_shape=jax.ShapeDtypeStruct(out_shape, F32), compiler_params=_cp("arbitrary"))(*args)


def ffn_in_fwd(name, h, w1, w3, tm):
    M = h.shape[0]

    def step(h_ref, w1_ref, w3_ref, a1_ref, a3_ref, act_ref):
        for s in range(NS):
            a1 = _dot(h_ref[...], w1_ref[s], 1, 0)
            a3 = _dot(h_ref[...], w3_ref[s], 1, 0)
            a1_ref[s] = a1.astype(a1_ref.dtype)
            a3_ref[s] = a3.astype(a3_ref.dtype)
            act_ref[s] = (_silu(a1) * a3).astype(act_ref.dtype)

    wspec = pl.BlockSpec((NS, D, WS), lambda i: (0, 0, 0))
    ospec = pl.BlockSpec((NS, tm, WS), lambda i: (0, i, 0))
    return pl.pallas_call(
        step, name=name, grid=(M // tm,), in_specs=[pl.BlockSpec((tm, D), lambda i: (i, 0)), wspec, wspec],
        out_specs=[ospec, ospec, ospec],
        out_shape=[jax.ShapeDtypeStruct((NS, M, WS), BF16)] * 3, compiler_params=_cp("arbitrary"))(h, w1, w3)


def ffn_out_bwd_x(name, dff, w2, a1, a3, tm):
    M = dff.shape[0]

    def step(d_ref, w_ref, a1_ref, a3_ref, da1_ref, da3_ref):
        for s in range(NS):
            dact = _dot(d_ref[...], w_ref[s * WS:(s + 1) * WS, :], 1, 1)
            a1 = a1_ref[s].astype(F32)
            sig = jax.nn.sigmoid(a1)
            da3_ref[s] = (dact * (a1 * sig)).astype(da3_ref.dtype)
            da1_ref[s] = (dact * a3_ref[s].astype(F32) * (sig * (1.0 + a1 * (1.0 - sig)))).astype(da1_ref.dtype)

    aspec = pl.BlockSpec((NS, tm, WS), lambda i: (0, i, 0))
    return pl.pallas_call(
        step, name=name, grid=(M // tm,),
        in_specs=[pl.BlockSpec((tm, D), lambda i: (i, 0)), pl.BlockSpec(w2.shape, lambda i: (0, 0)), aspec, aspec],
        out_specs=[aspec, aspec],
        out_shape=[jax.ShapeDtypeStruct((NS, M, WS), BF16)] * 2, compiler_params=_cp("arbitrary"))(dff, w2, a1, a3)


def ff_in_bwd_x(name, da3, w3, tm, add=None):
    M = da3.shape[1]

    def step(*refs):
        d_ref, w_ref, o_ref = refs[0], refs[1], refs[-1]
        acc = _dot(d_ref[0], w_ref[0], 1, 1)
        for s in range(1, NS):
            acc = acc + _dot(d_ref[s], w_ref[s], 1, 1)
        o_ref[...] = acc if add is None else acc + refs[2][...]

    return _resident(name, M, tm, (da3, (NS, tm, WS), lambda i: (0, i, 0)), w3, (M, D), (tm, D), lambda i: (i, 0), step, add)


def ff_in_bwd_w(name, h, da3, tk):
    M = h.shape[0]

    def step(h_ref, d_ref, acc):
        for s in range(NS):
            acc[s] += _dot(h_ref[...], d_ref[s], 0, 0)

    return _token_sum(name, M // tk, [pl.BlockSpec((tk, D), lambda k: (k, 0)), pl.BlockSpec((NS, tk, WS), lambda k: (0, k, 0))],
                      (NS, D, WS), (NS, D, WS), step, (h, da3))


def ff_out_fwd(name, act3, w2, tm):
    M = act3.shape[1]

    def step(a_ref, w_ref, o_ref):
        acc = _dot(a_ref[0], w_ref[0:WS, :], 1, 0)
        for s in range(1, NS):
            acc = acc + _dot(a_ref[s], w_ref[s * WS:(s + 1) * WS, :], 1, 0)
        o_ref[...] = acc

    return _resident(name, M, tm, (act3, (NS, tm, WS), lambda i: (0, i, 0)), w2, (M, D), (tm, D), lambda i: (i, 0), step)


def _token_sum(name, nk, in_specs, out_shape, acc_shape, step, args):
    def body(*refs):
        o_ref, acc = refs[-2], refs[-1]
        k = pl.program_id(0)

        @pl.when(k == 0)
        def _():
            acc[...] = jnp.zeros_like(acc)

        step(*refs[:-2], acc)

        @pl.when(k == nk - 1)
        def _():
            o_ref[...] = acc[...].astype(o_ref.dtype)

    return pl.pallas_call(body, name=name, grid=(nk,), in_specs=in_specs,
                          out_specs=pl.BlockSpec(out_shape, lambda k, n=len(out_shape): (0,) * n),
                          out_shape=jax.ShapeDtypeStruct(out_shape, BF16), scratch_shapes=[pltpu.VMEM(acc_shape, F32)],
                          compiler_params=_cp("arbitrary"))(*args)


def ff_out_bwd_w(name, act3, dff, tk):
    M = dff.shape[0]

    def step(a_ref, d_ref, acc):
        for s in range(NS):
            acc[s * WS:(s + 1) * WS, :] += _dot(a_ref[s], d_ref[...], 0, 0)

    return _token_sum(name, M // tk, [pl.BlockSpec((NS, tk, WS), lambda k: (0, k, 0)), pl.BlockSpec((tk, D), lambda k: (k, 0))],
                      (NS * WS, D), (NS * WS, D), step, (act3, dff))


def _shift_rows(x, d):
    n = x.shape[0]
    if d == 0:
        return x
    y = pltpu.roll(x, (-d) % n, 0)
    t = lax.broadcasted_iota(jnp.int32, x.shape, 0)
    ok = (t + d >= 0) & (t + d < n)
    return jnp.where(ok, y, 0.0)


def _conv_pre(x, w_ref, b_ref):
    acc = jnp.broadcast_to(b_ref[...], x.shape)
    for k in range(5):
        acc = acc + _shift_rows(x, k - 2) * w_ref[k:k + 1, :]
    return acc


def conv_fwd(name, proj, conv_w, conv_b, R):
    segs = ((0, LC), (LC, R))

    def body(x_ref, w_ref, b_ref, o_ref):
        for (s, e) in segs:
            pre = _conv_pre(x_ref[s:e, :], w_ref, b_ref)
            o_ref[s:e, :] = _silu(pre)

    return pl.pallas_call(
        body, name=name, grid=(12,),
        in_specs=[pl.BlockSpec((R, 128), lambda j: (0, C_XBC // 128 + j)),
                  pl.BlockSpec((8, 128), lambda j: (0, j)), pl.BlockSpec((1, 128), lambda j: (0, j))],
        out_specs=pl.BlockSpec((R, 128), lambda j: (0, j)),
        out_shape=jax.ShapeDtypeStruct((R, 1536), F32), compiler_params=_cp("arbitrary"))(proj, conv_w, conv_b)


def conv_bwd(name, proj, conv_w, conv_b, d_f, d_b, d_skip, dproj, R):
    segs = ((0, LC), (LC, R))

    def body(x_ref, w_ref, b_ref, df_ref, db_ref, ds_ref, _, dx_ref, dw_ref, dbias_ref):
        j = pl.program_id(0)
        has_skip = (j < 8).astype(F32)
        dw = [jnp.zeros((1, 128), F32) for _ in range(5)]
        dbias = jnp.zeros((1, 128), F32)
        for (s, e) in segs:
            x = x_ref[s:e, :]
            pre = _conv_pre(x, w_ref, b_ref)
            sig = jax.nn.sigmoid(pre)
            dy = df_ref[s:e, :] + db_ref[s:e, :]
            if s == LC:
                dy = dy + ds_ref[...] * has_skip
            dpre = dy * (sig * (1.0 + pre * (1.0 - sig)))
            dx = jnp.zeros_like(x)
            for k in range(5):
                dx = dx + _shift_rows(dpre, 2 - k) * w_ref[k:k + 1, :]
                dw[k] = dw[k] + jnp.sum(dpre * _shift_rows(x, k - 2), axis=0, keepdims=True)
            dbias = dbias + jnp.sum(dpre, axis=0, keepdims=True)
            dx_ref[s:e, :] = dx.astype(dx_ref.dtype)
        dw_ref[...] = jnp.zeros_like(dw_ref)
        for k in range(5):
            dw_ref[k:k + 1, :] = dw[k]
        dbias_ref[...] = dbias

    return pl.pallas_call(
        body, name=name, grid=(12,),
        in_specs=[pl.BlockSpec((R, 128), lambda j: (0, C_XBC // 128 + j)),
                  pl.BlockSpec((8, 128), lambda j: (0, j)), pl.BlockSpec((1, 128), lambda j: (0, j)),
                  pl.BlockSpec((R, 128), lambda j: (0, j)), pl.BlockSpec((R, 128), lambda j: (0, j)),
                  pl.BlockSpec((R - LC, 128), lambda j: (0, jnp.minimum(j, 7))),
                  pl.BlockSpec(memory_space=pl.ANY)],
        out_specs=[pl.BlockSpec((R, 128), lambda j: (0, C_XBC // 128 + j)),
                   pl.BlockSpec((8, 128), lambda j: (0, j)), pl.BlockSpec((1, 128), lambda j: (0, j))],
        out_shape=[jax.ShapeDtypeStruct(dproj.shape, dproj.dtype), jax.ShapeDtypeStruct((8, 1536), F32),
                   jax.ShapeDtypeStruct((1, 1536), F32)],
        input_output_aliases={6: 0}, compiler_params=_cp("arbitrary"))(proj, conv_w, conv_b, d_f, d_b, d_skip, dproj)


def _ssd_chunk(rev, dirn):
    cums = _cumsum_fn(rev)

    def f(xs, Bs, Cs, dt, alog, Hs):
        lane = lax.broadcasted_iota(jnp.int32, (1, 128), 1)
        sub = lax.broadcasted_iota(jnp.int32, (Q, 1), 0)
        r = lax.broadcasted_iota(jnp.int32, (Q, Q), 0)
        c = lax.broadcasted_iota(jnp.int32, (Q, Q), 1)
        mask = (r <= c) if rev else (r >= c)
        left = lane < 64
        a = dt * (-jnp.exp(alog))
        s = cums(a)
        sT, dtT = s.T, dt.T
        last_row = (sub == (0 if rev else Q - 1)).astype(F32)
        s_last = jnp.sum(s * last_row, axis=0, keepdims=True)
        G = [mm_nt(Cs[g], Bs[g]) for g in range(2)]
        M, es, wc, ed = [], [], [], []
        for h in range(NH):
            l = 16 * dirn + h
            oh_l = (lane == l).astype(F32)
            oh_s = (sub == l).astype(F32)
            s_col = jnp.sum(s * oh_l, axis=1, keepdims=True)
            dt_col = jnp.sum(dt * oh_l, axis=1, keepdims=True)
            s_row = jnp.sum(sT * oh_s, axis=0, keepdims=True)
            dt_row = jnp.sum(dtT * oh_s, axis=0, keepdims=True)
            sl = jnp.sum(s_last * oh_l, axis=1, keepdims=True)
            seg = jnp.where(mask, s_col - s_row, 0.0)
            lm = jnp.where(mask, jnp.exp(seg), 0.0)
            M.append(G[h // 8] * lm * dt_row)
            es.append(jnp.exp(s_col))
            wc.append(jnp.exp(sl - s_col) * dt_col)
            ed.append(jnp.exp(sl))
        Ys, Hn = [], []
        for j in range(8):
            g = j // 4
            xa = jnp.where(left, xs[j], 0.0)
            xb = jnp.where(left, 0.0, xs[j])
            yd = mm(M[2 * j], xa) + mm(M[2 * j + 1], xb)
            yo = mm(Cs[g], Hs[j]) * jnp.where(left, es[2 * j], es[2 * j + 1])
            Ys.append(yd + yo)
            st = mm_tn(Bs[g], xs[j] * jnp.where(left, wc[2 * j], wc[2 * j + 1]))
            Hn.append(Hs[j] * jnp.where(left, ed[2 * j], ed[2 * j + 1]) + st)
        return Ys, Hn

    return f


def _chunk_of(t, n, rev):
    if not rev:
        return t
    return jnp.where(t < 2, 1 - t, n + 1 - t)


def ssd_fwd(name, xbc, dt, alog, n, rev, dirn):
    chunk = _ssd_chunk(rev, dirn)

    def body(x_ref, b_ref, c_ref, dt_ref, al_ref, y_ref, hs_ref, h_scr):
        @pl.when(pl.program_id(0) == 0)
        def _():
            h_scr[...] = jnp.zeros_like(h_scr)

        xs = [x_ref[:, 128 * j:128 * (j + 1)] for j in range(8)]
        Bs = [b_ref[:, 128 * g:128 * (g + 1)] for g in range(2)]
        Cs = [c_ref[:, 128 * g:128 * (g + 1)] for g in range(2)]
        Hs = [h_scr[:, 128 * j:128 * (j + 1)] for j in range(8)]
        hs_ref[0] = h_scr[...]
        Ys, Hn = chunk(xs, Bs, Cs, dt_ref[...], al_ref[...], Hs)
        for j in range(8):
            y_ref[:, 128 * j:128 * (j + 1)] = Ys[j]
            h_scr[:, 128 * j:128 * (j + 1)] = Hn[j]

    cm = lambda t: _chunk_of(t, n, rev)
    return pl.pallas_call(
        body, name=name, grid=(n,),
        in_specs=[pl.BlockSpec((Q, 1024), lambda t: (cm(t), 0)), pl.BlockSpec((Q, 256), lambda t: (cm(t), 4)),
                  pl.BlockSpec((Q, 256), lambda t: (cm(t), 5)), pl.BlockSpec((Q, 128), lambda t: (cm(t), 0)),
                  pl.BlockSpec((1, 128), lambda t: (0, 0))],
        out_specs=[pl.BlockSpec((Q, 1024), lambda t: (cm(t), 0)), pl.BlockSpec((1, Q, 1024), lambda t: (cm(t), 0, 0))],
        out_shape=[jax.ShapeDtypeStruct((n * Q, 1024), F32), jax.ShapeDtypeStruct((n, Q, 1024), F32)],
        scratch_shapes=[pltpu.VMEM((Q, 1024), F32)], compiler_params=_cp("arbitrary"))(xbc, xbc, xbc, dt, alog)


def ssd_bwd(name, xbc, dt, alog, hs, dy, n, rev, dirn):
    chunk = _ssd_chunk(rev, dirn)

    def body(x_ref, b_ref, c_ref, dt_ref, al_ref, hs_ref, dy_ref, dx_ref, ddt_ref, dal_ref, dh_scr):
        tt = pl.program_id(0)
        ch = _chunk_of(n - 1 - tt, n, rev)

        @pl.when(tt == 0)
        def _():
            dh_scr[...] = jnp.zeros_like(dh_scr)

        xs = [x_ref[:, 128 * j:128 * (j + 1)] for j in range(8)]
        Bs = [b_ref[:, 128 * g:128 * (g + 1)] for g in range(2)]
        Cs = [c_ref[:, 128 * g:128 * (g + 1)] for g in range(2)]
        Hs = [hs_ref[0, :, 128 * j:128 * (j + 1)] for j in range(8)]
        live = (ch >= 2).astype(F32)
        dYs = [dy_ref[:, 128 * j:128 * (j + 1)] * live for j in range(8)]
        dHn = [dh_scr[:, 128 * j:128 * (j + 1)] for j in range(8)]
        _, vjp_fn = jax.vjp(chunk, xs, Bs, Cs, dt_ref[...], al_ref[...], Hs)
        dxs, dBs, dCs, ddt, dal, dHs = vjp_fn((dYs, dHn))
        for j in range(8):
            dx_ref[:, 128 * j:128 * (j + 1)] = dxs[j]
            dh_scr[:, 128 * j:128 * (j + 1)] = dHs[j]
        for g in range(2):
            dx_ref[:, 1024 + 128 * g:1024 + 128 * (g + 1)] = dBs[g]
            dx_ref[:, 1280 + 128 * g:1280 + 128 * (g + 1)] = dCs[g]
        ddt_ref[...] = ddt

        @pl.when(tt == 0)
        def _():
            dal_ref[...] = dal

        @pl.when(tt > 0)
        def _():
            dal_ref[...] += dal

    cm = lambda t: _chunk_of(n - 1 - t, n, rev)
    return pl.pallas_call(
        body, name=name, grid=(n,),
        in_specs=[pl.BlockSpec((Q, 1024), lambda t: (cm(t), 0)), pl.BlockSpec((Q, 256), lambda t: (cm(t), 4)),
                  pl.BlockSpec((Q, 256), lambda t: (cm(t), 5)), pl.BlockSpec((Q, 128), lambda t: (cm(t), 0)),
                  pl.BlockSpec((1, 128), lambda t: (0, 0)), pl.BlockSpec((1, Q, 1024), lambda t: (cm(t), 0, 0)),
                  pl.BlockSpec((Q, 1024), lambda t: (jnp.maximum(cm(t) - 2, 0), 0))],
        out_specs=[pl.BlockSpec((Q, 1536), lambda t: (cm(t), 0)), pl.BlockSpec((Q, 128), lambda t: (cm(t), 0)),
                   pl.BlockSpec((1, 128), lambda t: (0, 0))],
        out_shape=[jax.ShapeDtypeStruct((n * Q, 1536), F32), jax.ShapeDtypeStruct((n * Q, 128), F32),
                   jax.ShapeDtypeStruct((1, 128), F32)],
        scratch_shapes=[pltpu.VMEM((Q, 1024), F32)], compiler_params=_cp("arbitrary"))(xbc, xbc, xbc, dt, alog, hs, dy)


def f_norm0(x, g0, b0, sc, sh):
    x0 = _ln(x, g0, b0)
    return x0, x0 * (1.0 + sc) + sh


def f_dt(raw, bias):
    z = split4(raw)[0] + bias
    dt = jnp.maximum(z, 0.0) + jnp.log1p(jnp.exp(-jnp.abs(z)))
    return dt, dt


def f_gated_norm(yf, yb, xs, z, dcol, g):
    h = (yf + yb + xs * dcol) * _silu(z)
    return (h * lax.rsqrt(jnp.mean(h * h, axis=-1, keepdims=True) + LN_EPS) * g,)


def f_gmlp(uv, gmg, gmb, *wb):
    ws, bs = wb[:8], wb[8:]
    u, v = split2(uv)
    vn = split8(_ln(_gelu(v), gmg, gmb))
    mixed = concat8(tuple(mm(ws[g], vn[g]) + bs[g] for g in range(8)))
    return (_gelu(u) * mixed,)


def f_merge(ps, pg, gates, bg):
    gs, gg = split2(jax.nn.sigmoid(gates + bg))
    return (gs * ps + gg * pg,)


def f_res1(x0, out, g1, lg, lb, sc, sh):
    x1 = _ln(ALPHA * x0 + g1 * out, lg, lb)
    return x1, x1 * (1.0 + sc) + sh


def f_res2_loss(x1, ff, tgt, g2, lg, lb):
    x2 = _ln(ALPHA * x1 + g2 * ff, lg, lb)
    e = x2 - tgt
    return (0.5 * jnp.sum(jnp.mean(e * e, axis=-1, keepdims=True), axis=0, keepdims=True),)


def _row_tile(M):
    return 544 if M % 544 == 0 else (512 if M % 512 == 0 else M)


def core(x_all, tgt, mod_x, mod_c, X, S):
    R = x_all.shape[0]
    L = R - LC
    n = R // Q
    T = 256
    nt, ntl = R // T, L // T
    tmR, tmL = _row_tile(R), _row_tile(L)
    tkR = 256 if R % 512 else 512
    tkL = 512 if L % 512 == 0 else 256
    row = lambda v: v.reshape(1, -1)
    mx = [row(mod_x[k]) for k in range(6)]
    mc = [row(mod_c[k]) for k in range(6)]
    sel = lambda i: jnp.minimum(i, 1)
    sc1 = jnp.stack([mc[1], mx[1]])
    sh1 = jnp.stack([mc[0], mx[0]])
    ln0 = [(row(S['ln0_g']), None), (row(S['ln0_b']), None), (sc1, sel), (sh1, sel)]

    x0, xm = stage_fwd("norm0_fwd", f_norm0, T, nt, [(x_all, D, 0, 0)], ln0, [('new', R, D, 0), ('new', R, D, 0, BF16)])
    w_in = X.w_in()
    proj = matmul("proj_fwd", xm, w_in, 'nn', tmR, PW // 2, 1024)
    conv_w8 = jnp.pad(S['conv_w'], ((0, 3), (0, 0)))
    conv_b = row(S['conv_b'])
    xbc = conv_fwd("conv_fwd", proj, conv_w8, conv_b, R)
    dt_bias = jnp.pad(S['dt_bias'].reshape(1, 32), ((0, 0), (0, 96)))
    alog = jnp.pad(S['a_log'].reshape(1, 32), ((0, 0), (0, 96)))
    x_dt = [(proj, 512, C_DT // 512, 0)]
    dt_f, dt_b = stage_fwd("dt_fwd", f_dt, T, nt, x_dt, [(dt_bias, None)], [('new', R, 128, 0), ('new', R, 128, 0)])
    y_f, hs_f = ssd_fwd("ssd_fwd_f", xbc, dt_f, alog, n, False, 0)
    y_b, hs_b = ssd_fwd("ssd_fwd_b", xbc, dt_b, alog, n, True, 1)
    W = X.rest()
    dcol = jnp.repeat(S['d_skip'][0] + S['d_skip'][1], 64).reshape(1, D)
    x_gn = [(y_f, D, 0, 1), (y_b, D, 0, 1), (xbc, D, 0, 1), (proj, D, C_Z // D, 1)]
    p_gn = [(dcol, None), (row(S['ssd_norm_g']), None)]
    (yn,) = stage_fwd("gnorm_fwd", f_gated_norm, T, ntl, x_gn, p_gn, [('new', L, D, 0, BF16)])
    x_gm = [(proj, 2 * D, C_UV // (2 * D), LC // Q)]
    p_gm = ([(row(S['gm_norm_g']), None), (row(S['gm_norm_b']), None)]
            + [(S['w_spatial'][g], None) for g in range(8)] + [(S['b_spatial'][g].reshape(Q, 1), None) for g in range(8)])
    (y_gm,) = stage_fwd("gmlp_fwd", f_gmlp, Q, L // Q, x_gm, p_gm, [('new', L, D, 0, BF16)])
    p_ssd = matmul("pssd_fwd", yn, W['w_ssd_proj'], 'nn', tmL, 1024, 1024)
    p_g = matmul("pgm_fwd", y_gm, W['w_gm_proj'], 'nn', tmL, 1024, 1024)
    x_mg = [(p_ssd, D, 0, 0), (p_g, D, 0, 0), (proj, 2 * D, C_GATE // (2 * D), 1)]
    p_mg = [(row(S['b_gate']), None)]
    (merged,) = stage_fwd("merge_fwd", f_merge, T, ntl, x_mg, p_mg, [('new', L, D, 0, BF16)])
    out = matmul("out_fwd", merged, W['w_out'], 'nn', tmL, 1024, 1024)
    x_r1 = [(x0, D, 0, 1), (out, D, 0, 0)]
    p_r1 = [(mx[2], None), (row(S['ln1_g']), None), (row(S['ln1_b']), None), (mx[4], None), (mx[3], None)]
    x1, hm = stage_fwd("res1_fwd", f_res1, T, ntl, x_r1, p_r1, [('new', L, D, 0), ('new', L, D, 0, BF16)])
    a1, a3, act = ffn_in_fwd("ffn_in_fwd", hm, W['w_ff1'], W['w_ff3'], T)
    ff = ff_out_fwd("ff2_fwd", act, W['w_ff2'], tmL)
    x_r2 = [(x1, D, 0, 0), (ff, D, 0, 0), (tgt, D, 0, 0)]
    p_r2 = [(mx[5], None), (row(S['ln2_g']), None), (row(S['ln2_b']), None)]

    dx1_a, dff, dg2, dl2g, dl2b, loss = stage_bwd(
        "res2_bwd", f_res2_loss, T, ntl, x_r2, p_r2, [1.0],
        [('new', L, D, 0), ('new', L, D, 0, BF16), None], [True, True, True], primal=[(0, (1, 1))])
    da1, da3 = ffn_out_bwd_x("ffn_out_bwd_x", dff, W['w_ff2'], a1, a3, T)
    gw_ff2 = ff_out_bwd_w("ff2_bwd_w", act, dff, tkL)
    dhm = ff_in_bwd_x("ff1_bwd_x", da1, W['w_ff1'], tmL)
    dhm = ff_in_bwd_x("ff3_bwd_x", da3, W['w_ff3'], tmL, add=dhm)
    gw_ff1 = ff_in_bwd_w("ff1_bwd_w", hm, da1, tkL)
    gw_ff3 = ff_in_bwd_w("ff3_bwd_w", hm, da3, tkL)
    X.grads('ffn', {'w_ff2': gw_ff2, 'w_ff1': gw_ff1, 'w_ff3': gw_ff3})
    dx0_a, dout, dg1, dl1g, dl1b, dsc2, dsh2 = stage_bwd(
        "res1_bwd", f_res1, T, ntl, x_r1, p_r1, [(dx1_a, D, 0, 0), (dhm, D, 0, 0)],
        [('new', L, D, 0), ('new', L, D, 0, BF16)], [True] * 5)
    dmerged = matmul("out_bwd_x", dout, W['w_out'], 'nt', tmL, 1024, 1024)
    gw_out = matmul("out_bwd_w", merged, dout, 'tn', 1024, 1024, tkL, BF16)
    lt, lq = -(LC // T), -(LC // Q)
    x_mg_b = [(p_ssd, D, 0, lt), (p_g, D, 0, lt), (proj, 2 * D, C_GATE // (2 * D), 0)]
    dp_ssd, dp_g, dproj, dbg = stage_bwd(
        "merge_bwd", f_merge, T, nt, x_mg_b, p_mg, [(dmerged, D, 0, lt)],
        [('new', L, D, lt, BF16), ('new', L, D, lt, BF16), ('part', R, PW, 2 * D, C_GATE // (2 * D), 0, BF16)], [True])
    dyn = matmul("pssd_bwd_x", dp_ssd, W['w_ssd_proj'], 'nt', tmL, 1024, 1024)
    gw_ssd = matmul("pssd_bwd_w", yn, dp_ssd, 'tn', 1024, 1024, tkL, BF16)
    dy_gm = matmul("pgm_bwd_x", dp_g, W['w_gm_proj'], 'nt', tmL, 1024, 1024)
    gw_gm = matmul("pgm_bwd_w", y_gm, dp_g, 'tn', 1024, 1024, tkL, BF16)
    X.grads('proj', {'w_out': gw_out, 'w_ssd_proj': gw_ssd, 'w_gm_proj': gw_gm})
    r_gm = stage_bwd("gmlp_bwd", f_gmlp, Q, n, [(proj, 2 * D, C_UV // (2 * D), 0)], p_gm, [(dy_gm, D, 0, lq)],
                     [('alias', dproj, 2 * D, C_UV // (2 * D), 0)], [True] * 18)
    dproj, dgmg, dgmb, dws, dbs = r_gm[0], r_gm[1], r_gm[2], r_gm[3:11], r_gm[11:19]
    x_gn_b = [(y_f, D, 0, 0), (y_b, D, 0, 0), (xbc, D, 0, 0), (proj, D, C_Z // D, 0)]
    dy, dskipx, dproj, ddcol, dng = stage_bwd(
        "gnorm_bwd", f_gated_norm, T, nt, x_gn_b, p_gn, [(dyn, D, 0, lt)],
        [('new', L, D, lt), None, ('new', L, D, lt), ('alias', dproj, D, C_Z // D, 0)], [True, True])
    dxbc_f, ddt_f, dal_f = ssd_bwd("ssd_bwd_f", xbc, dt_f, alog, hs_f, dy, n, False, 0)
    dxbc_b, ddt_b, dal_b = ssd_bwd("ssd_bwd_b", xbc, dt_b, alog, hs_b, dy, n, True, 1)
    dproj, ddtb = stage_bwd("dt_bwd", f_dt, T, nt, x_dt, [(dt_bias, None)],
                            [(ddt_f, 128, 0, 0), (ddt_b, 128, 0, 0)],
                            [('alias', dproj, 512, C_DT // 512, 0)], [True])
    dproj, dcw8, dcb = conv_bwd("conv_bwd", proj, conv_w8, conv_b, dxbc_f, dxbc_b, dskipx, dproj, R)
    gw_in = matmul("proj_bwd_w", xm, dproj, 'tn', 1024, PW // 2, tkR, BF16)
    X.grads('in', {'w_in': gw_in})
    dxm = matmul("proj_bwd_x", dproj, w_in, 'nt', R // 2 if R % 16 == 0 else R, 1024, 1024)
    grad_x, dl0g, dl0b, dsc1, dsh1 = stage_bwd(
        "norm0_bwd", f_norm0, T, nt, [(x_all, D, 0, 0)], ln0, [(dx0_a, D, 0, -1), (dxm, D, 0, 0)],
        [('new', L, D, -1)], [True] * 4)

    zero = jnp.zeros((D,), F32)
    flat = lambda v: v.reshape(-1)
    small = {
        'loss': flat(loss), 'ln0_g': flat(dl0g), 'ln0_b': flat(dl0b),
        'dmod_x': jnp.concatenate([flat(dsh1[1]), flat(dsc1[1]), flat(dg1), flat(dsh2), flat(dsc2), flat(dg2)]),
        'dmod_c': jnp.concatenate([flat(dsh1[0]), flat(dsc1[0]), zero, zero, zero, zero]),
        'conv_w': flat(dcw8[:5]), 'conv_b': flat(dcb), 'dt_bias': flat(ddtb[:, :32]),
        'a_log': flat((dal_f + dal_b)[:, :32]),
        'd_skip': flat(jnp.tile(ddcol.reshape(1, NH, 64).sum(-1), (2, 1))),
        'ssd_norm_g': flat(dng), 'gm_norm_g': flat(dgmg), 'gm_norm_b': flat(dgmb),
        'w_spatial': flat(jnp.stack(dws)), 'b_spatial': flat(jnp.stack(dbs)), 'b_gate': flat(dbg),
        'ln1_g': flat(dl1g), 'ln1_b': flat(dl1b), 'ln2_g': flat(dl2g), 'ln2_b': flat(dl2b),
    }
    return grad_x, small


def _place():
    return lax.axis_index("x"), lax.axis_index("y"), lax.axis_index("c")


def allgather8(name, blk, hbm):
    space = pl.ANY if hbm else pltpu.VMEM

    def body(x_ref, out_ref, send_sems, recv_sems, local_sem):
        x, y, c = _place()
        me, sibling = (x, y, c), (x, y, 1 - c)
        chips = [(1 - x, y), (x, 1 - y), (1 - x, 1 - y)]

        def slot(px, py, pc):
            return out_ref.at[4 * px + 2 * py + pc]

        def copy(k, block, to, src=None):
            return pltpu.make_async_remote_copy(
                src_ref=slot(*block) if src is None else src, dst_ref=slot(*block),
                send_sem=send_sems.at[k], recv_sem=recv_sems.at[k], device_id=to, device_id_type=MESH)

        mine = pltpu.make_async_copy(x_ref, slot(*me), local_sem)
        mine.start()
        first = [copy(0, me, sibling, src=x_ref)]
        first += [copy(1 + j, me, (*chip, c), src=x_ref) for j, chip in enumerate(chips)]
        for cp in first:
            cp.start()
        passed = [copy(4 + j, (*chip, c), sibling) for j, chip in enumerate(chips)]
        for j, chip in enumerate(chips):
            copy(1 + j, (*chip, c), me).wait_recv()
            passed[j].start()
        copy(0, sibling, me).wait_recv()
        for j, chip in enumerate(chips):
            copy(4 + j, (*chip, 1 - c), me).wait_recv()
        for cp in first + passed:
            cp.wait_send()
        mine.wait()

    return pl.pallas_call(
        body, name=name, out_shape=jax.ShapeDtypeStruct((8,) + blk.shape, blk.dtype),
        in_specs=[pl.BlockSpec(memory_space=space)], out_specs=pl.BlockSpec(memory_space=space),
        scratch_shapes=[pltpu.SemaphoreType.DMA((7,)), pltpu.SemaphoreType.DMA((7,)), pltpu.SemaphoreType.DMA],
        compiler_params=pltpu.CompilerParams(vmem_limit_bytes=VMEM_LIMIT_V7X))(blk)


def _peers(place):
    x, y, c = place
    return [((1 - x) if k & 4 else x, (1 - y) if k & 2 else y, (1 - c) if k & 1 else c) for k in range(1, 8)]


def _slot(p):
    return 4 * p[0] + 2 * p[1] + p[2]


def plan_gather(place, srcs, lands):
    remote = [(s, l.at[_slot(place)], to) for s, l in zip(srcs, lands) for to in _peers(place)]
    return remote, [(s, l.at[_slot(place)]) for s, l in zip(srcs, lands)]


def plan_to_owner(place, srcs, lands):
    remote = [(s.at[2 * to[0] + to[1], to[2]], l.at[_slot(place)], to) for s, l in zip(srcs, lands) for to in _peers(place)]
    x, y, c = place
    return remote, [(s.at[2 * x + y, c], l.at[_slot(place)]) for s, l in zip(srcs, lands)]


def sequencer_exchange(name, collective_id, srcs, land_shapes, plan):
    n = len(srcs)
    src_refs = [jax.new_ref(a, memory_space=pltpu.MemorySpace.HBM) for a in srcs]
    land_refs = [jax.empty_ref(s, memory_space=pltpu.MemorySpace.HBM) for s in land_shapes]

    @pl.kernel(mesh=plsc.ScalarSubcoreMesh(axis_name="sequencer", num_cores=1), name=name,
               scratch_types=(pltpu.SemaphoreType.DMA((7 * n,)), pltpu.SemaphoreType.DMA((7 * n,)),
                              pltpu.SemaphoreType.DMA((n,))),
               compiler_params=pltpu.CompilerParams(collective_id=collective_id))
    def launch(send_sems, recv_sems, local_sems):
        place = _place()
        barrier = pltpu.get_barrier_semaphore()
        for to in _peers(place):
            pl.semaphore_signal(barrier, inc=1, device_id=to, device_id_type=MESH)
        pl.semaphore_wait(barrier, 7)
        remote, local = plan(place, src_refs, land_refs)
        mine = [pltpu.make_async_copy(s, d, local_sems.at[a]) for a, (s, d) in enumerate(local)]
        for cp in mine:
            cp.start()
        cps = [pltpu.make_async_remote_copy(src_ref=s, dst_ref=d, send_sem=send_sems.at[k], recv_sem=recv_sems.at[k],
                                            device_id=to, device_id_type=MESH) for k, (s, d, to) in enumerate(remote)]
        for cp in cps:
            cp.start()
        for cp in mine:
            cp.wait()
        for cp in cps:
            cp.wait()

    launch()
    return land_refs


def sibling_pair(name, hs):
    n = len(hs)

    def body(*refs):
        ins, outs = refs[:n], refs[n:2 * n]
        send_sems, recv_sems = refs[2 * n:]
        x, y, c = _place()
        cps = [pltpu.make_async_remote_copy(src_ref=outs[a].at[c], dst_ref=outs[a].at[c], send_sem=send_sems.at[a],
                                            recv_sem=recv_sems.at[a], device_id=(x, y, 1 - c), device_id_type=MESH)
               for a in range(n)]
        for cp in cps:
            cp.start()
        for a in range(n):
            pltpu.make_async_remote_copy(src_ref=outs[a].at[1 - c], dst_ref=outs[a].at[1 - c], send_sem=send_sems.at[a],
                                         recv_sem=recv_sems.at[a], device_id=(x, y, 1 - c),
                                         device_id_type=MESH).wait_recv()
        for cp in cps:
            cp.wait_send()

    any_spec = pl.BlockSpec(memory_space=pl.ANY)
    return pl.pallas_call(
        body, name=name, out_shape=[jax.ShapeDtypeStruct(h.shape, h.dtype) for h in hs],
        in_specs=[any_spec] * n, out_specs=[any_spec] * n, input_output_aliases={a: a for a in range(n)},
        scratch_shapes=[pltpu.SemaphoreType.DMA((n,)), pltpu.SemaphoreType.DMA((n,))])(*hs)


def owner_sum(name, land):
    _, r, w = land.shape
    T = r // 2

    def body(_, l_ref, o_ref):
        acc = l_ref[0].astype(F32)
        for j in range(1, 8):
            acc = acc + l_ref[j].astype(F32)
        o_ref[...] = acc

    grid_spec = pltpu.PrefetchScalarGridSpec(
        num_scalar_prefetch=1, grid=(2,),
        in_specs=[pl.BlockSpec((8, T, w), lambda i, at: (0, i, 0))],
        out_specs=pl.BlockSpec((None, T, w), lambda i, at: (at[0], i, 0)))
    at = jnp.stack([lax.axis_index("c")]).astype(jnp.int32)
    return pl.pallas_call(body, name=name, grid_spec=grid_spec, out_shape=jax.ShapeDtypeStruct((2, r, w), F32),
                          compiler_params=_cp("arbitrary"))(at, land)


W_IN_RUNS = ((0, 2, 1296, 376), (376, 3, 0, 1672), (2048, 1, 920, 752), (2800, 2, 0, 1296), (4096, 0, 0, 1024),
             (5120, 0, 1024, 648), (5768, 1, 0, 920))


def w_in_to_padded(name, g4):
    T = 128

    def body(g_ref, o_ref):
        o_ref[:, D_PROJ:PW] = jnp.zeros((T, PW - D_PROJ), o_ref.dtype)
        for (a, s, j0, w) in W_IN_RUNS:
            o_ref[:, a:a + w] = g_ref[s, :, j0:j0 + w]

    return pl.pallas_call(body, name=name, grid=(D // T,), in_specs=[pl.BlockSpec((4, T, 1672), lambda i: (0, i, 0))],
                          out_specs=pl.BlockSpec((T, PW), lambda i: (i, 0)),
                          out_shape=jax.ShapeDtypeStruct((D, PW), g4.dtype), compiler_params=_cp("arbitrary"))(g4)


def w_in_from_padded(name, gp):
    T = 128

    def body(g_ref, o_ref):
        for (a, s, j0, w) in W_IN_RUNS:
            o_ref[s, :, j0:j0 + w] = g_ref[:, a:a + w]

    return pl.pallas_call(body, name=name, grid=(D // T,), in_specs=[pl.BlockSpec((T, PW), lambda i: (i, 0))],
                          out_specs=pl.BlockSpec((4, T, 1672), lambda i: (0, i, 0)),
                          out_shape=jax.ShapeDtypeStruct((4, D, 1672), gp.dtype), compiler_params=_cp("arbitrary"))(gp)


def sum_devices(name, g):
    def body(g_ref, o_ref):
        acc = g_ref[0]
        for k in range(1, 8):
            acc = acc + g_ref[k]
        o_ref[...] = acc

    return pl.pallas_call(body, name=name, out_shape=jax.ShapeDtypeStruct(g.shape[1:], F32),
                          compiler_params=pltpu.CompilerParams(vmem_limit_bytes=VMEM_LIMIT_V7X))(g)


def adamw(name, w, g, m, v, T):
    _, r, wd = w.shape
    c1 = 1.0 - ADAM_B1 ** ADAM_STEP
    c2 = 1.0 - ADAM_B2 ** ADAM_STEP

    def body(w_ref, g_ref, m_ref, v_ref, d_ref, mo_ref, vo_ref):
        gv = g_ref[...]
        mn = ADAM_B1 * m_ref[...] + (1.0 - ADAM_B1) * gv
        vn = ADAM_B2 * v_ref[...] + (1.0 - ADAM_B2) * (gv * gv)
        d_ref[...] = -ADAM_LR * ((mn / c1) / (jnp.sqrt(vn / c2) + ADAM_EPS) + ADAM_WD * w_ref[...])
        mo_ref[...] = mn
        vo_ref[...] = vn

    spec = pl.BlockSpec((None, T, wd), lambda i: (0, i, 0))
    return pl.pallas_call(body, name=name, grid=(r // T,), in_specs=[spec] * 4, out_specs=[spec] * 3,
                          out_shape=[jax.ShapeDtypeStruct((1, r, wd), F32)] * 3, compiler_params=_cp("arbitrary"))(w, g, m, v)


BIG = {'w_in': (1024, 1672), 'w_ssd_proj': (256, 1024), 'w_gm_proj': (256, 1024), 'w_out': (256, 1024),
       'w_ff1': (1024, 704), 'w_ff3': (1024, 704), 'w_ff2': (704, 1024)}


class Flat:
    def __init__(self, segs):
        self.off, o = {}, 0
        for name, size in segs:
            self.off[name] = (o, size)
            o += -(-size // 128) * 128
        self.rows = -(-o // 1024) * 8

    def pack(self, vals):
        parts = []
        for name, (o, size) in self.off.items():
            v = vals[name].reshape(-1).astype(F32)
            parts.append(jnp.pad(v, (0, -(-size // 128) * 128 - size)))
        buf = jnp.concatenate(parts)
        return jnp.pad(buf, (0, self.rows * 128 - buf.shape[0])).reshape(self.rows, 128)

    def get(self, buf, name, shape=None):
        o, size = self.off[name]
        v = buf[o // 128:(o + size + 127) // 128].reshape(-1)[:size]
        return v if shape is None else v.reshape(shape)


PARTIALS = Flat([('loss', 1), ('ln0_g', D), ('ln0_b', D), ('dmod_x', 6 * D), ('dmod_c', 6 * D), ('conv_w', 5 * 1536),
                 ('conv_b', 1536), ('dt_bias', 32), ('a_log', 32), ('d_skip', 32), ('ssd_norm_g', D),
                 ('gm_norm_g', D), ('gm_norm_b', D), ('w_spatial', 8 * Q * Q), ('b_spatial', 8 * Q), ('b_gate', 2 * D),
                 ('ln1_g', D), ('ln1_b', D), ('ln2_g', D), ('ln2_b', D)])

WEIGHTS = ('c_ctx', 'ln0_g', 'ln0_b', 'w_ada', 'b_ada', 'w_in', 'conv_w', 'conv_b', 'dt_bias', 'a_log', 'd_skip',
           'ssd_norm_g', 'gm_norm_g', 'gm_norm_b', 'w_spatial', 'b_spatial', 'b_gate', 'w_ssd_proj', 'w_gm_proj',
           'w_out', 'ln1_g', 'ln1_b', 'w_ff1', 'w_ff3', 'w_ff2', 'ln2_g', 'ln2_b')
BIG_NAMES = tuple(BIG)
SMALL_NAMES = tuple(n for n in WEIGHTS if n not in BIG_NAMES and n != 'w_ada')


def kernel(x, c, ctx, c_ctx, ln0_g, ln0_b, w_ada, b_ada, w_in, conv_w, conv_b, dt_bias, a_log, d_skip, ssd_norm_g, gm_norm_g, gm_norm_b, w_spatial, b_spatial, b_gate, w_ssd_proj, w_gm_proj, w_out, ln1_g, ln1_b, w_ff1, w_ff3, w_ff2, ln2_g, ln2_b, loss_target, m_c_ctx, m_ln0_g, m_ln0_b, m_w_ada, m_b_ada, m_w_in, m_conv_w, m_conv_b, m_dt_bias, m_a_log, m_d_skip, m_ssd_norm_g, m_gm_norm_g, m_gm_norm_b, m_w_spatial, m_b_spatial, m_b_gate, m_w_ssd_proj, m_w_gm_proj, m_w_out, m_ln1_g, m_ln1_b, m_w_ff1, m_w_ff3, m_w_ff2, m_ln2_g, m_ln2_b, v_c_ctx, v_ln0_g, v_ln0_b, v_w_ada, v_b_ada, v_w_in, v_conv_w, v_conv_b, v_dt_bias, v_a_log, v_d_skip, v_ssd_norm_g, v_gm_norm_g, v_gm_norm_b, v_w_spatial, v_b_spatial, v_b_gate, v_w_ssd_proj, v_w_gm_proj, v_w_out, v_ln1_g, v_ln1_b, v_w_ff1, v_w_ff3, v_w_ff2, v_ln2_g, v_ln2_b):
    wts = dict(c_ctx=c_ctx, ln0_g=ln0_g, ln0_b=ln0_b, w_ada=w_ada, b_ada=b_ada, w_in=w_in, conv_w=conv_w, conv_b=conv_b,
               dt_bias=dt_bias, a_log=a_log, d_skip=d_skip, ssd_norm_g=ssd_norm_g, gm_norm_g=gm_norm_g,
               gm_norm_b=gm_norm_b, w_spatial=w_spatial, b_spatial=b_spatial, b_gate=b_gate, w_ssd_proj=w_ssd_proj,
               w_gm_proj=w_gm_proj, w_out=w_out, ln1_g=ln1_g, ln1_b=ln1_b, w_ff1=w_ff1, w_ff3=w_ff3, w_ff2=w_ff2,
               ln2_g=ln2_g, ln2_b=ln2_b)
    ms = dict(zip(WEIGHTS, (m_c_ctx, m_ln0_g, m_ln0_b, m_w_ada, m_b_ada, m_w_in, m_conv_w, m_conv_b, m_dt_bias, m_a_log,
                            m_d_skip, m_ssd_norm_g, m_gm_norm_g, m_gm_norm_b, m_w_spatial, m_b_spatial, m_b_gate,
                            m_w_ssd_proj, m_w_gm_proj, m_w_out, m_ln1_g, m_ln1_b, m_w_ff1, m_w_ff3, m_w_ff2, m_ln2_g,
                            m_ln2_b)))
    vs = dict(zip(WEIGHTS, (v_c_ctx, v_ln0_g, v_ln0_b, v_w_ada, v_b_ada, v_w_in, v_conv_w, v_conv_b, v_dt_bias, v_a_log,
                            v_d_skip, v_ssd_norm_g, v_gm_norm_g, v_gm_norm_b, v_w_spatial, v_b_spatial, v_b_gate,
                            v_w_ssd_proj, v_w_gm_proj, v_w_out, v_ln1_g, v_ln1_b, v_w_ff1, v_w_ff3, v_w_ff2, v_ln2_g,
                            v_ln2_b)))
    px, py, pc = _place()
    shard = 2 * px + py
    dev = 2 * shard + pc
    take = lambda a, i, axis=0: lax.dynamic_index_in_dim(a, i, axis, keepdims=False)

    half = lambda n: take(wts[n][0].reshape(2, BIG[n][0] // 2, BIG[n][1]), pc).astype(BF16)

    pre = jnp.concatenate([c, jnp.pad(conv_w[0], ((0, 0), (0, D - 384))), jnp.zeros((2, D), F32)], axis=0)
    pre = allgather8("gather_cond", pre, False)
    conv_w_full = pre[0::2, 1:6, :384].transpose(1, 0, 2).reshape(5, 1536)
    a16 = jnp.concatenate([_silu(pre[:, 0, :]), _silu(c_ctx)[None], jnp.zeros((7, D), F32)], axis=0)
    mod = matmul("ada_fwd", a16, w_ada[0], 'nn', 16, 512, 1024)
    mod = mod + lax.dynamic_slice_in_dim(b_ada[0], shard * 1536, 1536)[None]
    mod = allgather8("gather_mod", mod, False)
    mod = jnp.concatenate([mod[0], mod[2], mod[4], mod[6]], axis=1)
    mod_x = take(mod, dev).reshape(6, D)
    mod_c = mod[8].reshape(6, D)

    def full(n, blocks):
        r, w = BIG[n]
        return blocks.reshape(4, r, w) if w != D else blocks.reshape(4 * r, w)

    class Exchanges:
        rest_names = BIG_NAMES[1:]

        def __init__(self):
            self.pending = []

        def w_in(self):
            blocks = allgather8("gather_w_in", half('w_in'), True)
            w = w_in_to_padded("w_in_layout", full('w_in', blocks))
            halves = [half(n) for n in self.rest_names]
            halves[0], _ = lax.optimization_barrier((halves[0], blocks))
            lands = [jax.ShapeDtypeStruct((8,) + h.shape, BF16) for h in halves]
            self.rest_refs = sequencer_exchange("gather_rest", 1, halves, lands, plan_gather)
            return w

        def rest(self):
            return {n: full(n, r[...]) for n, r in zip(self.rest_names, self.rest_refs)}

        def grads(self, group, gs):
            if group == 'in':
                gs = {'w_in': w_in_from_padded("w_in_grad_layout", gs['w_in'])}
            blocks = [g.reshape(4, 2, BIG[n][0] // 2, BIG[n][1]) for n, g in gs.items()]
            lands = [jax.ShapeDtypeStruct((8,) + b.shape[2:], BF16) for b in blocks]
            refs = sequencer_exchange("grads_" + group, 2 + len(self.pending), blocks, lands, plan_to_owner)
            self.pending.append((tuple(gs), refs))

        def finish(self):
            names, halves = [], []
            for ns, refs in self.pending:
                names += ns
                halves += [owner_sum("grads_sum_" + n, r[...]) for n, r in zip(ns, refs)]
            return {n: h.reshape(BIG[n]) for n, h in zip(names, sibling_pair("grads_halves", halves))}

    S = dict(ln0_g=ln0_g, ln0_b=ln0_b, conv_w=conv_w_full, conv_b=conv_b[0], dt_bias=dt_bias[0], a_log=a_log[0],
             d_skip=d_skip[0], ssd_norm_g=ssd_norm_g[0], gm_norm_g=gm_norm_g[0], gm_norm_b=gm_norm_b[0],
             w_spatial=w_spatial[0], b_spatial=b_spatial[0], b_gate=b_gate[0], ln1_g=ln1_g[0], ln1_b=ln1_b[0],
             ln2_g=ln2_g[0], ln2_b=ln2_b[0])
    x_all = jnp.concatenate([ctx[0], x[0]], axis=0)
    exchanges = Exchanges()
    grad_x, gsmall = core(x_all, loss_target[0], mod_x, mod_c, exchanges, S)

    parts = allgather8("gather_partials", PARTIALS.pack(gsmall), False)
    tot = sum_devices("partials_sum", parts)
    g_shards = exchanges.finish()
    g = {n: PARTIALS.get(tot, n) for n in ('ln0_g', 'ln0_b', 'conv_b', 'dt_bias', 'a_log', 'd_skip', 'ssd_norm_g',
                                           'gm_norm_g', 'gm_norm_b', 'w_spatial', 'b_spatial', 'b_gate', 'ln1_g',
                                           'ln1_b', 'ln2_g', 'ln2_b')}
    loss = PARTIALS.get(tot, 'loss', ())
    dmod_c = PARTIALS.get(tot, 'dmod_c')
    g['b_ada'] = PARTIALS.get(tot, 'dmod_x') + dmod_c
    g['conv_w'] = lax.dynamic_slice_in_dim(PARTIALS.get(tot, 'conv_w', (5, 1536)), shard * 384, 384, axis=1)
    o, size = PARTIALS.off['dmod_x']
    dmod_rows = parts[:, o // 128:(o + size) // 128].reshape(8, size)
    dm = jnp.concatenate([dmod_rows, dmod_c[None], jnp.zeros((7, 6 * D), F32)], axis=0)
    dm = lax.dynamic_slice_in_dim(dm, shard * 1536, 1536, axis=1)
    g['w_ada'] = matmul("ada_bwd_w", a16, dm, 'tn', 1024, 512, 16)
    dm_c = jnp.concatenate([dm[8:9], jnp.zeros((15, 1536), F32)], axis=0)
    dc = matmul("ada_bwd_c", dm_c, w_ada[0], 'nt', 16, 1024, 512)
    dc = allgather8("gather_dcctx", dc, False)[:, 0, :]
    dc = ((dc[0] + dc[2]) + dc[4]) + dc[6]
    sg = jax.nn.sigmoid(c_ctx)
    g['c_ctx'] = dc * (sg * (1.0 + c_ctx * (1.0 - sg)))
    for n in BIG_NAMES:
        g[n] = g_shards[n]

    delta, new_m, new_v = {}, {}, {}
    for n in BIG_NAMES + ('w_ada',):
        T = 352 if n == 'w_ff2' else 256
        d_, m_, v_ = adamw("adamw_" + n, wts[n], g[n][None], ms[n], vs[n], T)
        delta[n], new_m[n], new_v[n] = d_, m_, v_
    lay = Flat([(n, wts[n].size) for n in SMALL_NAMES])
    d_, m_, v_ = adamw("adamw_small", *(lay.pack(t)[None] for t in (wts, g, ms, vs)), lay.rows)
    for n in SMALL_NAMES:
        delta[n], new_m[n], new_v[n] = (lay.get(b[0], n) for b in (d_, m_, v_))

    shp = lambda d: [d[n].reshape(wts[n].shape) for n in WEIGHTS]
    return (loss, grad_x[None], *shp(g), *shp(delta), *shp(new_m), *shp(new_v))
```

```python
import functools

import jax
import jax.numpy as jnp
from jax import lax
from jax.experimental import pallas as pl
from jax.experimental.pallas import tpu as pltpu
from jax.experimental.pallas import tpu_sc as plsc

F32 = jnp.float32
BF16 = jnp.bfloat16
MESH = pl.DeviceIdType.MESH

VMEM_LIMIT_V7X = 56 * 1024 * 1024

D = 1024
LC = 256
Q = 128
NH = 16
D_FF = 2816
LN_EPS = 1e-5
ALPHA = 2.0 ** 0.25

PW = 7168
C_GATE, C_UV, C_Z, C_XBC, C_DT = 0, 2048, 4096, 5120, 6656
D_PROJ = 6688

ADAM_LR, ADAM_B1, ADAM_B2, ADAM_EPS, ADAM_WD, ADAM_STEP = 0.001, 0.9, 0.999, 1e-08, 0.01, 10


def _cp(*sem):
    return pltpu.CompilerParams(dimension_semantics=sem, vmem_limit_bytes=VMEM_LIMIT_V7X)


def _dot(a, b, ca, cb):
    return lax.dot_general(a.astype(BF16), b.astype(BF16), (((ca,), (cb,)), ((), ())),
                           preferred_element_type=F32)


@jax.custom_vjp
def mm(a, b):
    return _dot(a, b, 1, 0)


mm.defvjp(lambda a, b: (_dot(a, b, 1, 0), (a, b)),
          lambda r, g: (_dot(g, r[1], 1, 1), _dot(r[0], g, 0, 0)))


@jax.custom_vjp
def mm_nt(a, b):
    return _dot(a, b, 1, 1)


mm_nt.defvjp(lambda a, b: (_dot(a, b, 1, 1), (a, b)),
             lambda r, g: (_dot(g, r[1], 1, 0), _dot(g, r[0], 0, 0)))


@jax.custom_vjp
def mm_tn(a, b):
    return _dot(a, b, 0, 0)


mm_tn.defvjp(lambda a, b: (_dot(a, b, 0, 0), (a, b)),
             lambda r, g: (_dot(r[1], g, 1, 1), _dot(r[0], g, 1, 0)))


def _dot32(a, b):
    return lax.dot_general(a, b, (((1,), (0,)), ((), ())), precision=lax.Precision.HIGHEST,
                           preferred_element_type=F32)


def _cumsum_fn(rev):
    def tri(transpose):
        r = lax.broadcasted_iota(jnp.int32, (Q, Q), 0)
        c = lax.broadcasted_iota(jnp.int32, (Q, Q), 1)
        keep = (r >= c) if (rev == transpose) else (r <= c)
        return jnp.where(keep, 1.0, 0.0).astype(F32)

    @jax.custom_vjp
    def cums(a):
        return _dot32(tri(False), a)

    cums.defvjp(lambda a: (_dot32(tri(False), a), None), lambda _, g: (_dot32(tri(True), g),))
    return cums


def _cols(v, k):
    w = v.shape[1] // k
    return tuple(v[:, w * i:w * (i + 1)] for i in range(k))


def _splitter(k):
    @jax.custom_vjp
    def split(v):
        return _cols(v, k)

    @jax.custom_vjp
    def concat(ps):
        return jnp.concatenate(ps, axis=1)

    split.defvjp(lambda v: (_cols(v, k), None), lambda _, g: (jnp.concatenate(g, axis=1),))
    concat.defvjp(lambda ps: (jnp.concatenate(ps, axis=1), None), lambda _, g: (_cols(g, k),))
    return split, concat


split2, _ = _splitter(2)
split4, _ = _splitter(4)
split8, concat8 = _splitter(8)


def _ln(x, g, b):
    mu = jnp.mean(x, axis=-1, keepdims=True)
    xc = x - mu
    var = jnp.mean(xc * xc, axis=-1, keepdims=True)
    return xc * lax.rsqrt(var + LN_EPS) * g + b


def _silu(x):
    return x * jax.nn.sigmoid(x)


def _gelu(x):
    return 0.5 * x * (1.0 + jnp.tanh(0.7978845608028654 * (x + 0.044715 * (x * x * x))))


def _xspec(T, w, col, roff):
    return pl.BlockSpec((T, w), lambda i, col=col, roff=roff: (jnp.maximum(i + roff, 0), col))


def _pspec(p, sel):
    if sel is None:
        return pl.BlockSpec(p.shape, lambda i, n=p.ndim: (0,) * n)
    return pl.BlockSpec((1,) + p.shape[1:], lambda i, n=p.ndim: (sel(i),) + (0,) * (n - 1))


def _out_plumbing(outs, T, args, in_specs):
    shapes, specs, aliases = [], [], {}
    for k, o in enumerate(outs):
        if o[0] == 'new':
            _, rows, w, roff = o[:4]
            shapes.append(jax.ShapeDtypeStruct((rows, w), o[4] if len(o) > 4 else F32))
            specs.append(_xspec(T, w, 0, roff))
        elif o[0] == 'acc':
            shapes.append(jax.ShapeDtypeStruct(o[1], F32))
            specs.append(pl.BlockSpec(o[1], lambda i, n=len(o[1]): (0,) * n))
        elif o[0] == 'part':
            _, rows, wtot, w, col, roff, dtype = o
            shapes.append(jax.ShapeDtypeStruct((rows, wtot), dtype))
            specs.append(_xspec(T, w, col, roff))
        else:
            _, arr, w, col, roff = o
            aliases[len(args)] = k
            args.append(arr)
            in_specs.append(pl.BlockSpec(memory_space=pl.ANY))
            shapes.append(jax.ShapeDtypeStruct(arr.shape, arr.dtype))
            specs.append(_xspec(T, w, col, roff))
    return shapes, specs, aliases


def stage_fwd(name, f, T, n, xs, ps, outs):
    nx, npar = len(xs), len(ps)
    args = [x[0] for x in xs] + [p[0] for p in ps]
    in_specs = [_xspec(T, w, col, roff) for (_, w, col, roff) in xs] + [_pspec(p, sel) for (p, sel) in ps]
    n_in = len(args)
    shapes, specs, aliases = _out_plumbing(outs, T, args, in_specs)
    n_all_in = len(args)

    def body(*refs):
        i = pl.program_id(0)
        xv = [r[...] for r in refs[:nx]]
        pv = [r[...] if ps[k][1] is None else r[0] for k, r in enumerate(refs[nx:n_in])]
        res = f(*xv, *pv)
        for k, o_ref in enumerate(refs[n_all_in:]):
            if outs[k][0] == 'acc':
                @pl.when(i == 0)
                def _(o_ref=o_ref, v=res[k]):
                    o_ref[...] = v

                @pl.when(i > 0)
                def _(o_ref=o_ref, v=res[k]):
                    o_ref[...] += v
            else:
                o_ref[...] = res[k].astype(o_ref.dtype)

    return pl.pallas_call(body, name=name, grid=(n,), in_specs=in_specs, out_specs=specs, out_shape=shapes,
                          input_output_aliases=aliases, compiler_params=_cp("arbitrary"))(*args)


def stage_bwd(name, f, T, n, xs, ps, cts, dxs, dps, primal=()):
    nx, npar = len(xs), len(ps)
    args = [x[0] for x in xs] + [p[0] for p in ps]
    in_specs = [_xspec(T, w, col, roff) for (_, w, col, roff) in xs] + [_pspec(p, sel) for (p, sel) in ps]
    ct_arrs = [c for c in cts if isinstance(c, tuple)]
    for (a, w, col, roff) in ct_arrs:
        args.append(a)
        in_specs.append(_xspec(T, w, col, roff))
    n_in = len(args)
    outs, out_of = [], []
    for k, o in enumerate(dxs):
        if o is not None:
            outs.append(o)
            out_of.append(('x', k))
    for k, want in enumerate(dps):
        if want:
            p, sel = ps[k]
            outs.append(('acc', p.shape))
            out_of.append(('p', k))
    for k, shape in primal:
        outs.append(('acc', shape))
        out_of.append(('r', k))
    shapes, specs, aliases = _out_plumbing(outs, T, args, in_specs)
    for j, (kind, k) in enumerate(out_of):
        if kind == 'p' and ps[k][1] is not None:
            p, sel = ps[k]
            specs[j] = pl.BlockSpec((1,) + p.shape[1:], lambda i, n=p.ndim, sel=sel: (sel(i),) + (0,) * (n - 1))
    n_all_in = len(args)

    def body(*refs):
        i = pl.program_id(0)
        xv = [r[...] for r in refs[:nx]]
        pv = [r[...] if ps[k][1] is None else r[0] for k, r in enumerate(refs[nx:nx + npar])]
        res, vjp_fn = jax.vjp(f, *xv, *pv)
        ctv, q = [], nx + npar
        for k, c in enumerate(cts):
            if c is None:
                ctv.append(jnp.zeros_like(res[k]))
            elif isinstance(c, tuple):
                v = refs[q][...]
                if c[3] < 0:
                    v = v * (i + c[3] >= 0).astype(F32)
                ctv.append(v)
                q += 1
            else:
                ctv.append(jnp.full_like(res[k], c))
        grads = vjp_fn(tuple(ctv))
        for j, o_ref in enumerate(refs[n_all_in:]):
            kind, k = out_of[j]
            if kind == 'x':
                o_ref[...] = grads[k].astype(o_ref.dtype)
            else:
                g = res[k] if kind == 'r' else grads[nx + k]
                sel = None if kind == 'r' else ps[k][1]
                if sel is None:
                    first = i == 0
                    tgt = o_ref
                else:
                    first = jnp.logical_or(i == 0, sel(i) != sel(jnp.maximum(i - 1, 0)))
                    tgt = o_ref.at[0]

                @pl.when(first)
                def _(tgt=tgt, g=g):
                    tgt[...] = g

                @pl.when(jnp.logical_not(first))
                def _(tgt=tgt, g=g):
                    tgt[...] += g

    return pl.pallas_call(body, name=name, grid=(n,), in_specs=in_specs, out_specs=specs, out_shape=shapes,
                          input_output_aliases=aliases, compiler_params=_cp("arbitrary"))(*args)


_CONTRACT = {'nn': (1, 0), 'nt': (1, 1), 'tn': (0, 0)}


def matmul(name, a, b, mode, tm, tn, tk, out_dtype=F32, add=None):
    if mode == 'nn':
        (M, K), (_, N) = a.shape, b.shape
    elif mode == 'nt':
        (M, K), (N, _) = a.shape, b.shape
    else:
        (K, M), (_, N) = a.shape, b.shape
    assert M % tm == 0 and N % tn == 0 and K % tk == 0, (name, M, N, K, tm, tn, tk)
    a_spec = (pl.BlockSpec((tk, tm), lambda j, i, k: (k, i)) if mode == 'tn'
              else pl.BlockSpec((tm, tk), lambda j, i, k: (i, k)))
    b_spec = (pl.BlockSpec((tn, tk), lambda j, i, k: (j, k)) if mode == 'nt'
              else pl.BlockSpec((tk, tn), lambda j, i, k: (k, j)))
    o_spec = pl.BlockSpec((tm, tn), lambda j, i, k: (i, j))
    return matmul_call(name, (N // tn, M // tm, K // tk), a, a_spec, b, b_spec, (M, N), o_spec, (tm, tn), mode,
                       out_dtype, add)


def matmul_call(name, grid, a, a_spec, b, b_spec, out_shape, o_spec, tile, mode, out_dtype=F32, add=None):
    tm, tn = tile
    nk = grid[2]
    ca, cb = _CONTRACT[mode]
    args, in_specs = [a, b], [a_spec, b_spec]
    if add is not None:
        args.append(add)
        in_specs.append(o_spec)

    def body(*refs):
        a_ref, b_ref = refs[0], refs[1]
        o_ref, acc = refs[-2], refs[-1]
        k = pl.program_id(2)
        if nk == 1:
            p = _dot(a_ref[...], b_ref[...], ca, cb)
            o_ref[...] = (p + refs[2][...] if add is not None else p).astype(out_dtype)
            return

        @pl.when(k == 0)
        def _():
            acc[...] = refs[2][...] if add is not None else jnp.zeros_like(acc)

        acc[...] += _dot(a_ref[...], b_ref[...], ca, cb)

        @pl.when(k == nk - 1)
        def _():
            o_ref[...] = acc[...].astype(out_dtype)

    return pl.pallas_call(body, name=name, grid=grid, in_specs=in_specs, out_specs=o_spec,
                          out_shape=jax.ShapeDtypeStruct(out_shape, out_dtype),
                          scratch_shapes=[pltpu.VMEM((tm, tn) if nk > 1 else (8, 128), F32)],
                          compiler_params=_cp("arbitrary", "arbitrary", "arbitrary"))(*args)


NS, WS = 4, 704


def _resident(name, M, tm, rows, weight, out_shape, out_block, out_map, step, add=None):
    args = [rows[0], weight] + ([] if add is None else [add])
    in_specs = [pl.BlockSpec(rows[1], rows[2]), pl.BlockSpec(weight.shape, lambda i, n=weight.ndim: (0,) * n)]
    if add is not None:
        in_specs.append(pl.BlockSpec(out_block, out_map))
    return pl.pallas_call(step, name=name, grid=(M // tm,), in_specs=in_specs, out_specs=pl.BlockSpec(out_block, out_map),
                          out_shape=jax.ShapeDtypeStruct(out_shape, F32), compiler_params=_cp("arbitrary"))(*args)


def ffn_in_fwd(name, h, w1, w3, tm):
    M = h.shape[0]

    def step(h_ref, w1_ref, w3_ref, a1_ref, a3_ref, act_ref):
        for s in range(NS):
            a1 = _dot(h_ref[...], w1_ref[s], 1, 0)
            a3 = _dot(h_ref[...], w3_ref[s], 1, 0)
            a1_ref[s] = a1.astype(a1_ref.dtype)
            a3_ref[s] = a3.astype(a3_ref.dtype)
            act_ref[s] = (_silu(a1) * a3).astype(act_ref.dtype)

    wspec = pl.BlockSpec((NS, D, WS), lambda i: (0, 0, 0))
    ospec = pl.BlockSpec((NS, tm, WS), lambda i: (0, i, 0))
    return pl.pallas_call(
        step, name=name, grid=(M // tm,), in_specs=[pl.BlockSpec((tm, D), lambda i: (i, 0)), wspec, wspec],
        out_specs=[ospec, ospec, ospec],
        out_shape=[jax.ShapeDtypeStruct((NS, M, WS), BF16)] * 3, compiler_params=_cp("arbitrary"))(h, w1, w3)


def ffn_out_bwd_x(name, dff, w2, a1, a3, tm):
    M = dff.shape[0]

    def step(d_ref, w_ref, a1_ref, a3_ref, da1_ref, da3_ref):
        for s in range(NS):
            dact = _dot(d_ref[...], w_ref[s * WS:(s + 1) * WS, :], 1, 1)
            a1 = a1_ref[s].astype(F32)
            sig = jax.nn.sigmoid(a1)
            da3_ref[s] = (dact * (a1 * sig)).astype(da3_ref.dtype)
            da1_ref[s] = (dact * a3_ref[s].astype(F32) * (sig * (1.0 + a1 * (1.0 - sig)))).astype(da1_ref.dtype)

    aspec = pl.BlockSpec((NS, tm, WS), lambda i: (0, i, 0))
    return pl.pallas_call(
        step, name=name, grid=(M // tm,),
        in_specs=[pl.BlockSpec((tm, D), lambda i: (i, 0)), pl.BlockSpec(w2.shape, lambda i: (0, 0)), aspec, aspec],
        out_specs=[aspec, aspec],
        out_shape=[jax.ShapeDtypeStruct((NS, M, WS), BF16)] * 2, compiler_params=_cp("arbitrary"))(dff, w2, a1, a3)


def ff_in_bwd_x(name, da3, w3, tm, add=None):
    M = da3.shape[1]

    def step(*refs):
        d_ref, w_ref, o_ref = refs[0], refs[1], refs[-1]
        acc = _dot(d_ref[0], w_ref[0], 1, 1)
        for s in range(1, NS):
            acc = acc + _dot(d_ref[s], w_ref[s], 1, 1)
        o_ref[...] = acc if add is None else acc + refs[2][...]

    return _resident(name, M, tm, (da3, (NS, tm, WS), lambda i: (0, i, 0)), w3, (M, D), (tm, D), lambda i: (i, 0), step, add)


def ff_in_bwd_w(name, h, da3, tk):
    M = h.shape[0]

    def step(h_ref, d_ref, acc):
        for s in range(NS):
            acc[s] += _dot(h_ref[...], d_ref[s], 0, 0)

    return _token_sum(name, M // tk, [pl.BlockSpec((tk, D), lambda k: (k, 0)), pl.BlockSpec((NS, tk, WS), lambda k: (0, k, 0))],
                      (NS, D, WS), (NS, D, WS), step, (h, da3))


def ff_out_fwd(name, act3, w2, tm):
    M = act3.shape[1]

    def step(a_ref, w_ref, o_ref):
        acc = _dot(a_ref[0], w_ref[0:WS, :], 1, 0)
        for s in range(1, NS):
            acc = acc + _dot(a_ref[s], w_ref[s * WS:(s + 1) * WS, :], 1, 0)
        o_ref[...] = acc

    return _resident(name, M, tm, (act3, (NS, tm, WS), lambda i: (0, i, 0)), w2, (M, D), (tm, D), lambda i: (i, 0), step)


def _token_sum(name, nk, in_specs, out_shape, acc_shape, step, args):
    def body(*refs):
        o_ref, acc = refs[-2], refs[-1]
        k = pl.program_id(0)

        @pl.when(k == 0)
        def _():
            acc[...] = jnp.zeros_like(acc)

        step(*refs[:-2], acc)

        @pl.when(k == nk - 1)
        def _():
            o_ref[...] = acc[...].astype(o_ref.dtype)

    return pl.pallas_call(body, name=name, grid=(nk,), in_specs=in_specs,
                          out_specs=pl.BlockSpec(out_shape, lambda k, n=len(out_shape): (0,) * n),
                          out_shape=jax.ShapeDtypeStruct(out_shape, BF16), scratch_shapes=[pltpu.VMEM(acc_shape, F32)],
                          compiler_params=_cp("arbitrary"))(*args)


def ff_out_bwd_w(name, act3, dff, tk):
    M = dff.shape[0]

    def step(a_ref, d_ref, acc):
        for s in range(NS):
            acc[s * WS:(s + 1) * WS, :] += _dot(a_ref[s], d_ref[...], 0, 0)

    return _token_sum(name, M // tk, [pl.BlockSpec((NS, tk, WS), lambda k: (0, k, 0)), pl.BlockSpec((tk, D), lambda k: (k, 0))],
                      (NS * WS, D), (NS * WS, D), step, (act3, dff))


HALO = 8


def _windows(pad_ref, v):
    n = v.shape[0]
    edge = jnp.zeros((HALO, 128), F32)
    pad_ref[0:HALO, :] = edge
    pad_ref[HALO:HALO + n, :] = v
    pad_ref[HALO + n:2 * HALO + n, :] = edge
    return lambda k: pad_ref[HALO + k - 2:HALO + k - 2 + n, :]


def _conv_pre(tap, w_ref, b_ref):
    acc = tap(0) * w_ref[0:1, :] + b_ref[...]
    for k in range(1, 5):
        acc = acc + tap(k) * w_ref[k:k + 1, :]
    return acc


def conv_fwd(name, proj, conv_w, conv_b, R):
    segs = ((0, LC), (LC, R))

    def body(x_ref, w_ref, b_ref, o_ref, xp):
        for (s, e) in segs:
            pre = _conv_pre(_windows(xp, x_ref[s:e, :]), w_ref, b_ref)
            o_ref[s:e, :] = _silu(pre)

    return pl.pallas_call(
        body, name=name, grid=(12,),
        in_specs=[pl.BlockSpec((R, 128), lambda j: (0, C_XBC // 128 + j)),
                  pl.BlockSpec((8, 128), lambda j: (0, j)), pl.BlockSpec((1, 128), lambda j: (0, j))],
        out_specs=pl.BlockSpec((R, 128), lambda j: (0, j)),
        out_shape=jax.ShapeDtypeStruct((R, 1536), F32), scratch_shapes=[pltpu.VMEM((R - LC + 2 * HALO, 128), F32)],
        compiler_params=_cp("arbitrary"))(proj, conv_w, conv_b)


def conv_bwd(name, proj, conv_w, conv_b, d_f, d_b, d_skip, dproj, R):
    segs = ((0, LC), (LC, R))

    def body(x_ref, w_ref, b_ref, df_ref, db_ref, ds_ref, _, dx_ref, dw_ref, dbias_ref, xp, dp):
        j = pl.program_id(0)
        has_skip = (j < 8).astype(F32)
        dw = [jnp.zeros((1, 128), F32) for _ in range(5)]
        dbias = jnp.zeros((1, 128), F32)
        for (s, e) in segs:
            xw = _windows(xp, x_ref[s:e, :])
            pre = _conv_pre(xw, w_ref, b_ref)
            sig = jax.nn.sigmoid(pre)
            dy = df_ref[s:e, :] + db_ref[s:e, :]
            if s == LC:
                dy = dy + ds_ref[...] * has_skip
            dpre = dy * (sig * (1.0 + pre * (1.0 - sig)))
            dtap = _windows(dp, dpre)
            dx = dtap(4) * w_ref[0:1, :]
            for k in range(1, 5):
                dx = dx + dtap(4 - k) * w_ref[k:k + 1, :]
            for k in range(5):
                dw[k] = dw[k] + jnp.sum(dtap(2) * xw(k), axis=0, keepdims=True)
            dbias = dbias + jnp.sum(dpre, axis=0, keepdims=True)
            dx_ref[s:e, :] = dx.astype(dx_ref.dtype)
        dw_ref[...] = jnp.zeros_like(dw_ref)
        for k in range(5):
            dw_ref[k:k + 1, :] = dw[k]
        dbias_ref[...] = dbias

    pad = pltpu.VMEM((R - LC + 2 * HALO, 128), F32)
    return pl.pallas_call(
        body, name=name, grid=(12,),
        in_specs=[pl.BlockSpec((R, 128), lambda j: (0, C_XBC // 128 + j)),
                  pl.BlockSpec((8, 128), lambda j: (0, j)), pl.BlockSpec((1, 128), lambda j: (0, j)),
                  pl.BlockSpec((R, 128), lambda j: (0, j)), pl.BlockSpec((R, 128), lambda j: (0, j)),
                  pl.BlockSpec((R - LC, 128), lambda j: (0, jnp.minimum(j, 7))),
                  pl.BlockSpec(memory_space=pl.ANY)],
        out_specs=[pl.BlockSpec((R, 128), lambda j: (0, C_XBC // 128 + j)),
                   pl.BlockSpec((8, 128), lambda j: (0, j)), pl.BlockSpec((1, 128), lambda j: (0, j))],
        out_shape=[jax.ShapeDtypeStruct(dproj.shape, dproj.dtype), jax.ShapeDtypeStruct((8, 1536), F32),
                   jax.ShapeDtypeStruct((1, 1536), F32)], scratch_shapes=[pad, pad],
        input_output_aliases={6: 0}, compiler_params=_cp("arbitrary"))(proj, conv_w, conv_b, d_f, d_b, d_skip, dproj)


def _ssd_chunk(rev, dirn):
    cums = _cumsum_fn(rev)

    def f(xs, Bs, Cs, dt, alog, Hs):
        lane = lax.broadcasted_iota(jnp.int32, (1, 128), 1)
        sub = lax.broadcasted_iota(jnp.int32, (Q, 1), 0)
        r = lax.broadcasted_iota(jnp.int32, (Q, Q), 0)
        c = lax.broadcasted_iota(jnp.int32, (Q, Q), 1)
        mask = (r <= c) if rev else (r >= c)
        left = lane < 64
        a = dt * (-jnp.exp(alog))
        s = cums(a)
        sT, dtT = s.T, dt.T
        last_row = (sub == (0 if rev else Q - 1)).astype(F32)
        s_last = jnp.sum(s * last_row, axis=0, keepdims=True)
        G = [mm_nt(Cs[g], Bs[g]) for g in range(2)]
        M, es, wc, ed = [], [], [], []
        for h in range(NH):
            l = 16 * dirn + h
            oh_l = (lane == l).astype(F32)
            oh_s = (sub == l).astype(F32)
            s_col = jnp.sum(s * oh_l, axis=1, keepdims=True)
            dt_col = jnp.sum(dt * oh_l, axis=1, keepdims=True)
            s_row = jnp.sum(sT * oh_s, axis=0, keepdims=True)
            dt_row = jnp.sum(dtT * oh_s, axis=0, keepdims=True)
            sl = jnp.sum(s_last * oh_l, axis=1, keepdims=True)
            seg = jnp.where(mask, s_col - s_row, 0.0)
            lm = jnp.where(mask, jnp.exp(seg), 0.0)
            M.append(G[h // 8] * lm * dt_row)
            es.append(jnp.exp(s_col))
            wc.append(jnp.exp(sl - s_col) * dt_col)
            ed.append(jnp.exp(sl))
        Ys, Hn = [], []
        for j in range(8):
            g = j // 4
            xa = jnp.where(left, xs[j], 0.0)
            xb = jnp.where(left, 0.0, xs[j])
            yd = mm(M[2 * j], xa) + mm(M[2 * j + 1], xb)
            yo = mm(Cs[g], Hs[j]) * jnp.where(left, es[2 * j], es[2 * j + 1])
            Ys.append(yd + yo)
            st = mm_tn(Bs[g], xs[j] * jnp.where(left, wc[2 * j], wc[2 * j + 1]))
            Hn.append(Hs[j] * jnp.where(left, ed[2 * j], ed[2 * j + 1]) + st)
        return Ys, Hn

    return f


def _chunk_of(t, n, rev):
    if not rev:
        return t
    return jnp.where(t < 2, 1 - t, n + 1 - t)


def ssd_fwd(name, xbc, dt, alog, n, rev, dirn):
    chunk = _ssd_chunk(rev, dirn)

    def body(x_ref, b_ref, c_ref, dt_ref, al_ref, y_ref, hs_ref, h_scr):
        @pl.when(pl.program_id(0) == 0)
        def _():
            h_scr[...] = jnp.zeros_like(h_scr)

        xs = [x_ref[:, 128 * j:128 * (j + 1)] for j in range(8)]
        Bs = [b_ref[:, 128 * g:128 * (g + 1)] for g in range(2)]
        Cs = [c_ref[:, 128 * g:128 * (g + 1)] for g in range(2)]
        Hs = [h_scr[:, 128 * j:128 * (j + 1)] for j in range(8)]
        hs_ref[0] = h_scr[...]
        Ys, Hn = chunk(xs, Bs, Cs, dt_ref[...], al_ref[...], Hs)
        for j in range(8):
            y_ref[:, 128 * j:128 * (j + 1)] = Ys[j]
            h_scr[:, 128 * j:128 * (j + 1)] = Hn[j]

    cm = lambda t: _chunk_of(t, n, rev)
    return pl.pallas_call(
        body, name=name, grid=(n,),
        in_specs=[pl.BlockSpec((Q, 1024), lambda t: (cm(t), 0)), pl.BlockSpec((Q, 256), lambda t: (cm(t), 4)),
                  pl.BlockSpec((Q, 256), lambda t: (cm(t), 5)), pl.BlockSpec((Q, 128), lambda t: (cm(t), 0)),
                  pl.BlockSpec((1, 128), lambda t: (0, 0))],
        out_specs=[pl.BlockSpec((Q, 1024), lambda t: (cm(t), 0)), pl.BlockSpec((1, Q, 1024), lambda t: (cm(t), 0, 0))],
        out_shape=[jax.ShapeDtypeStruct((n * Q, 1024), F32), jax.ShapeDtypeStruct((n, Q, 1024), F32)],
        scratch_shapes=[pltpu.VMEM((Q, 1024), F32)], compiler_params=_cp("arbitrary"))(xbc, xbc, xbc, dt, alog)


def ssd_bwd(name, xbc, dt, alog, hs, dy, n, rev, dirn):
    chunk = _ssd_chunk(rev, dirn)

    def body(x_ref, b_ref, c_ref, dt_ref, al_ref, hs_ref, dy_ref, dx_ref, ddt_ref, dal_ref, dh_scr):
        tt = pl.program_id(0)
        ch = _chunk_of(n - 1 - tt, n, rev)

        @pl.when(tt == 0)
        def _():
            dh_scr[...] = jnp.zeros_like(dh_scr)

        xs = [x_ref[:, 128 * j:128 * (j + 1)] for j in range(8)]
        Bs = [b_ref[:, 128 * g:128 * (g + 1)] for g in range(2)]
        Cs = [c_ref[:, 128 * g:128 * (g + 1)] for g in range(2)]
        Hs = [hs_ref[0, :, 128 * j:128 * (j + 1)] for j in range(8)]
        live = (ch >= 2).astype(F32)
        dYs = [dy_ref[:, 128 * j:128 * (j + 1)] * live for j in range(8)]
        dHn = [dh_scr[:, 128 * j:128 * (j + 1)] for j in range(8)]
        _, vjp_fn = jax.vjp(chunk, xs, Bs, Cs, dt_ref[...], al_ref[...], Hs)
        dxs, dBs, dCs, ddt, dal, dHs = vjp_fn((dYs, dHn))
        for j in range(8):
            dx_ref[:, 128 * j:128 * (j + 1)] = dxs[j]
            dh_scr[:, 128 * j:128 * (j + 1)] = dHs[j]
        for g in range(2):
            dx_ref[:, 1024 + 128 * g:1024 + 128 * (g + 1)] = dBs[g]
            dx_ref[:, 1280 + 128 * g:1280 + 128 * (g + 1)] = dCs[g]
        ddt_ref[...] = ddt

        @pl.when(tt == 0)
        def _():
            dal_ref[...] = dal

        @pl.when(tt > 0)
        def _():
            dal_ref[...] += dal

    cm = lambda t: _chunk_of(n - 1 - t, n, rev)
    return pl.pallas_call(
        body, name=name, grid=(n,),
        in_specs=[pl.BlockSpec((Q, 1024), lambda t: (cm(t), 0)), pl.BlockSpec((Q, 256), lambda t: (cm(t), 4)),
                  pl.BlockSpec((Q, 256), lambda t: (cm(t), 5)), pl.BlockSpec((Q, 128), lambda t: (cm(t), 0)),
                  pl.BlockSpec((1, 128), lambda t: (0, 0)), pl.BlockSpec((1, Q, 1024), lambda t: (cm(t), 0, 0)),
                  pl.BlockSpec((Q, 1024), lambda t: (jnp.maximum(cm(t) - 2, 0), 0))],
        out_specs=[pl.BlockSpec((Q, 1536), lambda t: (cm(t), 0)), pl.BlockSpec((Q, 128), lambda t: (cm(t), 0)),
                   pl.BlockSpec((1, 128), lambda t: (0, 0))],
        out_shape=[jax.ShapeDtypeStruct((n * Q, 1536), F32), jax.ShapeDtypeStruct((n * Q, 128), F32),
                   jax.ShapeDtypeStruct((1, 128), F32)],
        scratch_shapes=[pltpu.VMEM((Q, 1024), F32)], compiler_params=_cp("arbitrary"))(xbc, xbc, xbc, dt, alog, hs, dy)


def f_norm0(x, g0, b0, sc, sh):
    x0 = _ln(x, g0, b0)
    return x0, x0 * (1.0 + sc) + sh


def f_dt(raw, bias):
    z = split4(raw)[0] + bias
    dt = jnp.maximum(z, 0.0) + jnp.log1p(jnp.exp(-jnp.abs(z)))
    return dt, dt


def f_gated_norm(yf, yb, xs, z, dcol, g):
    h = (yf + yb + xs * dcol) * _silu(z)
    return (h * lax.rsqrt(jnp.mean(h * h, axis=-1, keepdims=True) + LN_EPS) * g,)


def f_gmlp(uv, gmg, gmb, *wb):
    ws, bs = wb[:8], wb[8:]
    u, v = split2(uv)
    vn = split8(_ln(_gelu(v), gmg, gmb))
    mixed = concat8(tuple(mm(ws[g], vn[g]) + bs[g] for g in range(8)))
    return (_gelu(u) * mixed,)


def f_merge(ps, pg, gates, bg):
    gs, gg = split2(jax.nn.sigmoid(gates + bg))
    return (gs * ps + gg * pg,)


def f_res1(x0, out, g1, lg, lb, sc, sh):
    x1 = _ln(ALPHA * x0 + g1 * out, lg, lb)
    return x1, x1 * (1.0 + sc) + sh


def f_res2_loss(x1, ff, tgt, g2, lg, lb):
    x2 = _ln(ALPHA * x1 + g2 * ff, lg, lb)
    e = x2 - tgt
    return (0.5 * jnp.sum(jnp.mean(e * e, axis=-1, keepdims=True), axis=0, keepdims=True),)


def _row_tile(M):
    return 544 if M % 544 == 0 else (512 if M % 512 == 0 else M)


def core(x_all, tgt, mod_x, mod_c, X, S):
    R = x_all.shape[0]
    L = R - LC
    n = R // Q
    T = 256
    nt, ntl = R // T, L // T
    tmR, tmL = _row_tile(R), _row_tile(L)
    tkR = 256 if R % 512 else 512
    tkL = 512 if L % 512 == 0 else 256
    row = lambda v: v.reshape(1, -1)
    mx = [row(mod_x[k]) for k in range(6)]
    mc = [row(mod_c[k]) for k in range(6)]
    sel = lambda i: jnp.minimum(i, 1)
    sc1 = jnp.stack([mc[1], mx[1]])
    sh1 = jnp.stack([mc[0], mx[0]])
    ln0 = [(row(S['ln0_g']), None), (row(S['ln0_b']), None), (sc1, sel), (sh1, sel)]

    x0, xm = stage_fwd("norm0_fwd", f_norm0, T, nt, [(x_all, D, 0, 0)], ln0, [('new', R, D, 0), ('new', R, D, 0, BF16)])
    w_in = X.w_in()
    proj = matmul("proj_fwd", xm, w_in, 'nn', tmR, PW // 2, 1024)
    conv_w8 = jnp.pad(S['conv_w'], ((0, 3), (0, 0)))
    conv_b = row(S['conv_b'])
    xbc = conv_fwd("conv_fwd", proj, conv_w8, conv_b, R)
    dt_bias = jnp.pad(S['dt_bias'].reshape(1, 32), ((0, 0), (0, 96)))
    alog = jnp.pad(S['a_log'].reshape(1, 32), ((0, 0), (0, 96)))
    x_dt = [(proj, 512, C_DT // 512, 0)]
    dt_f, dt_b = stage_fwd("dt_fwd", f_dt, T, nt, x_dt, [(dt_bias, None)], [('new', R, 128, 0), ('new', R, 128, 0)])
    y_f, hs_f = ssd_fwd("ssd_fwd_f", xbc, dt_f, alog, n, False, 0)
    y_b, hs_b = ssd_fwd("ssd_fwd_b", xbc, dt_b, alog, n, True, 1)
    W = X.rest()
    dcol = jnp.repeat(S['d_skip'][0] + S['d_skip'][1], 64).reshape(1, D)
    x_gn = [(y_f, D, 0, 1), (y_b, D, 0, 1), (xbc, D, 0, 1), (proj, D, C_Z // D, 1)]
    p_gn = [(dcol, None), (row(S['ssd_norm_g']), None)]
    (yn,) = stage_fwd("gnorm_fwd", f_gated_norm, T, ntl, x_gn, p_gn, [('new', L, D, 0, BF16)])
    x_gm = [(proj, 2 * D, C_UV // (2 * D), LC // Q)]
    p_gm = ([(row(S['gm_norm_g']), None), (row(S['gm_norm_b']), None)]
            + [(S['w_spatial'][g], None) for g in range(8)] + [(S['b_spatial'][g].reshape(Q, 1), None) for g in range(8)])
    (y_gm,) = stage_fwd("gmlp_fwd", f_gmlp, Q, L // Q, x_gm, p_gm, [('new', L, D, 0, BF16)])
    p_ssd = matmul("pssd_fwd", yn, W['w_ssd_proj'], 'nn', tmL, 1024, 1024)
    p_g = matmul("pgm_fwd", y_gm, W['w_gm_proj'], 'nn', tmL, 1024, 1024)
    x_mg = [(p_ssd, D, 0, 0), (p_g, D, 0, 0), (proj, 2 * D, C_GATE // (2 * D), 1)]
    p_mg = [(row(S['b_gate']), None)]
    (merged,) = stage_fwd("merge_fwd", f_merge, T, ntl, x_mg, p_mg, [('new', L, D, 0, BF16)])
    out = matmul("out_fwd", merged, W['w_out'], 'nn', tmL, 1024, 1024)
    x_r1 = [(x0, D, 0, 1), (out, D, 0, 0)]
    p_r1 = [(mx[2], None), (row(S['ln1_g']), None), (row(S['ln1_b']), None), (mx[4], None), (mx[3], None)]
    x1, hm = stage_fwd("res1_fwd", f_res1, T, ntl, x_r1, p_r1, [('new', L, D, 0), ('new', L, D, 0, BF16)])
    a1, a3, act = ffn_in_fwd("ffn_in_fwd", hm, W['w_ff1'], W['w_ff3'], T)
    ff = ff_out_fwd("ff2_fwd", act, W['w_ff2'], tmL)
    x_r2 = [(x1, D, 0, 0), (ff, D, 0, 0), (tgt, D, 0, 0)]
    p_r2 = [(mx[5], None), (row(S['ln2_g']), None), (row(S['ln2_b']), None)]

    dx1_a, dff, dg2, dl2g, dl2b, loss = stage_bwd(
        "res2_bwd", f_res2_loss, T, ntl, x_r2, p_r2, [1.0],
        [('new', L, D, 0), ('new', L, D, 0, BF16), None], [True, True, True], primal=[(0, (1, 1))])
    da1, da3 = ffn_out_bwd_x("ffn_out_bwd_x", dff, W['w_ff2'], a1, a3, T)
    gw_ff2 = ff_out_bwd_w("ff2_bwd_w", act, dff, tkL)
    dhm = ff_in_bwd_x("ff1_bwd_x", da1, W['w_ff1'], tmL)
    dhm = ff_in_bwd_x("ff3_bwd_x", da3, W['w_ff3'], tmL, add=dhm)
    gw_ff1 = ff_in_bwd_w("ff1_bwd_w", hm, da1, tkL)
    gw_ff3 = ff_in_bwd_w("ff3_bwd_w", hm, da3, tkL)
    X.grads('ffn', {'w_ff2': gw_ff2, 'w_ff1': gw_ff1, 'w_ff3': gw_ff3})
    dx0_a, dout, dg1, dl1g, dl1b, dsc2, dsh2 = stage_bwd(
        "res1_bwd", f_res1, T, ntl, x_r1, p_r1, [(dx1_a, D, 0, 0), (dhm, D, 0, 0)],
        [('new', L, D, 0), ('new', L, D, 0, BF16)], [True] * 5)
    dmerged = matmul("out_bwd_x", dout, W['w_out'], 'nt', tmL, 1024, 1024)
    gw_out = matmul("out_bwd_w", merged, dout, 'tn', 1024, 1024, tkL, BF16)
    lt, lq = -(LC // T), -(LC // Q)
    x_mg_b = [(p_ssd, D, 0, lt), (p_g, D, 0, lt), (proj, 2 * D, C_GATE // (2 * D), 0)]
    dp_ssd, dp_g, dproj, dbg = stage_bwd(
        "merge_bwd", f_merge, T, nt, x_mg_b, p_mg, [(dmerged, D, 0, lt)],
        [('new', L, D, lt, BF16), ('new', L, D, lt, BF16), ('part', R, PW, 2 * D, C_GATE // (2 * D), 0, BF16)], [True])
    dyn = matmul("pssd_bwd_x", dp_ssd, W['w_ssd_proj'], 'nt', tmL, 1024, 1024)
    gw_ssd = matmul("pssd_bwd_w", yn, dp_ssd, 'tn', 1024, 1024, tkL, BF16)
    dy_gm = matmul("pgm_bwd_x", dp_g, W['w_gm_proj'], 'nt', tmL, 1024, 1024)
    gw_gm = matmul("pgm_bwd_w", y_gm, dp_g, 'tn', 1024, 1024, tkL, BF16)
    X.grads('proj', {'w_out': gw_out, 'w_ssd_proj': gw_ssd, 'w_gm_proj': gw_gm})
    r_gm = stage_bwd("gmlp_bwd", f_gmlp, Q, n, [(proj, 2 * D, C_UV // (2 * D), 0)], p_gm, [(dy_gm, D, 0, lq)],
                     [('alias', dproj, 2 * D, C_UV // (2 * D), 0)], [True] * 18)
    dproj, dgmg, dgmb, dws, dbs = r_gm[0], r_gm[1], r_gm[2], r_gm[3:11], r_gm[11:19]
    x_gn_b = [(y_f, D, 0, 0), (y_b, D, 0, 0), (xbc, D, 0, 0), (proj, D, C_Z // D, 0)]
    dy, dskipx, dproj, ddcol, dng = stage_bwd(
        "gnorm_bwd", f_gated_norm, T, nt, x_gn_b, p_gn, [(dyn, D, 0, lt)],
        [('new', L, D, lt), None, ('new', L, D, lt), ('alias', dproj, D, C_Z // D, 0)], [True, True])
    dxbc_f, ddt_f, dal_f = ssd_bwd("ssd_bwd_f", xbc, dt_f, alog, hs_f, dy, n, False, 0)
    dxbc_b, ddt_b, dal_b = ssd_bwd("ssd_bwd_b", xbc, dt_b, alog, hs_b, dy, n, True, 1)
    dproj, ddtb = stage_bwd("dt_bwd", f_dt, T, nt, x_dt, [(dt_bias, None)],
                            [(ddt_f, 128, 0, 0), (ddt_b, 128, 0, 0)],
                            [('alias', dproj, 512, C_DT // 512, 0)], [True])
    dproj, dcw8, dcb = conv_bwd("conv_bwd", proj, conv_w8, conv_b, dxbc_f, dxbc_b, dskipx, dproj, R)
    gw_in = matmul("proj_bwd_w", xm, dproj, 'tn', 1024, PW // 2, tkR, BF16)
    X.grads('in', {'w_in': gw_in})
    dxm = matmul("proj_bwd_x", dproj, w_in, 'nt', R // 2 if R % 16 == 0 else R, 1024, 1024)
    grad_x, dl0g, dl0b, dsc1, dsh1 = stage_bwd(
        "norm0_bwd", f_norm0, T, nt, [(x_all, D, 0, 0)], ln0, [(dx0_a, D, 0, -1), (dxm, D, 0, 0)],
        [('new', L, D, -1)], [True] * 4)

    zero = jnp.zeros((D,), F32)
    flat = lambda v: v.reshape(-1)
    small = {
        'loss': flat(loss), 'ln0_g': flat(dl0g), 'ln0_b': flat(dl0b),
        'dmod_x': jnp.concatenate([flat(dsh1[1]), flat(dsc1[1]), flat(dg1), flat(dsh2), flat(dsc2), flat(dg2)]),
        'dmod_c': jnp.concatenate([flat(dsh1[0]), flat(dsc1[0]), zero, zero, zero, zero]),
        'conv_w': flat(dcw8[:5]), 'conv_b': flat(dcb), 'dt_bias': flat(ddtb[:, :32]),
        'a_log': flat((dal_f + dal_b)[:, :32]),
        'd_skip': flat(jnp.tile(ddcol.reshape(1, NH, 64).sum(-1), (2, 1))),
        'ssd_norm_g': flat(dng), 'gm_norm_g': flat(dgmg), 'gm_norm_b': flat(dgmb),
        'w_spatial': flat(jnp.stack(dws)), 'b_spatial': flat(jnp.stack(dbs)), 'b_gate': flat(dbg),
        'ln1_g': flat(dl1g), 'ln1_b': flat(dl1b), 'ln2_g': flat(dl2g), 'ln2_b': flat(dl2b),
    }
    return grad_x, small


def _place():
    return lax.axis_index("x"), lax.axis_index("y"), lax.axis_index("c")


def allgather8(name, blk, hbm):
    space = pl.ANY if hbm else pltpu.VMEM

    def body(x_ref, out_ref, send_sems, recv_sems, local_sem):
        x, y, c = _place()
        me, sibling = (x, y, c), (x, y, 1 - c)
        chips = [(1 - x, y), (x, 1 - y), (1 - x, 1 - y)]

        def slot(px, py, pc):
            return out_ref.at[4 * px + 2 * py + pc]

        def copy(k, block, to, src=None):
            return pltpu.make_async_remote_copy(
                src_ref=slot(*block) if src is None else src, dst_ref=slot(*block),
                send_sem=send_sems.at[k], recv_sem=recv_sems.at[k], device_id=to, device_id_type=MESH)

        mine = pltpu.make_async_copy(x_ref, slot(*me), local_sem)
        mine.start()
        first = [copy(0, me, sibling, src=x_ref)]
        first += [copy(1 + j, me, (*chip, c), src=x_ref) for j, chip in enumerate(chips)]
        for cp in first:
            cp.start()
        passed = [copy(4 + j, (*chip, c), sibling) for j, chip in enumerate(chips)]
        for j, chip in enumerate(chips):
            copy(1 + j, (*chip, c), me).wait_recv()
            passed[j].start()
        copy(0, sibling, me).wait_recv()
        for j, chip in enumerate(chips):
            copy(4 + j, (*chip, 1 - c), me).wait_recv()
        for cp in first + passed:
            cp.wait_send()
        mine.wait()

    return pl.pallas_call(
        body, name=name, out_shape=jax.ShapeDtypeStruct((8,) + blk.shape, blk.dtype),
        in_specs=[pl.BlockSpec(memory_space=space)], out_specs=pl.BlockSpec(memory_space=space),
        scratch_shapes=[pltpu.SemaphoreType.DMA((7,)), pltpu.SemaphoreType.DMA((7,)), pltpu.SemaphoreType.DMA],
        compiler_params=pltpu.CompilerParams(vmem_limit_bytes=VMEM_LIMIT_V7X))(blk)


def allgather8_small(name, blk):
    def body(x_ref, out_ref, send_sems, recv_sems):
        place = _place()
        out_ref[_slot(place)] = x_ref[...]
        cps = [pltpu.make_async_remote_copy(src_ref=x_ref, dst_ref=out_ref.at[_slot(place)], send_sem=send_sems.at[k],
                                            recv_sem=recv_sems.at[k], device_id=to, device_id_type=MESH)
               for k, to in enumerate(_peers(place))]
        for cp in cps:
            cp.start()
        for cp in cps:
            cp.wait()

    vmem = pl.BlockSpec(memory_space=pltpu.VMEM)
    return pl.pallas_call(
        body, name=name, out_shape=jax.ShapeDtypeStruct((8,) + blk.shape, blk.dtype), in_specs=[vmem], out_specs=vmem,
        scratch_shapes=[pltpu.SemaphoreType.DMA((7,)), pltpu.SemaphoreType.DMA((7,))])(blk)


def _peers(place):
    x, y, c = place
    return [((1 - x) if k & 4 else x, (1 - y) if k & 2 else y, (1 - c) if k & 1 else c) for k in range(1, 8)]


def _slot(p):
    return 4 * p[0] + 2 * p[1] + p[2]


def plan_gather(place, srcs, lands):
    remote = [(s, l.at[_slot(place)], to) for s, l in zip(srcs, lands) for to in _peers(place)]
    return remote, [(s, l.at[_slot(place)]) for s, l in zip(srcs, lands)]


def plan_to_owner(place, srcs, lands):
    remote = [(s.at[2 * to[0] + to[1], to[2]], l.at[_slot(place)], to) for s, l in zip(srcs, lands) for to in _peers(place)]
    x, y, c = place
    return remote, [(s.at[2 * x + y, c], l.at[_slot(place)]) for s, l in zip(srcs, lands)]


def sequencer_exchange(name, collective_id, srcs, land_shapes, plan):
    n = len(srcs)
    src_refs = [jax.new_ref(a, memory_space=pltpu.MemorySpace.HBM) for a in srcs]
    land_refs = [jax.empty_ref(s, memory_space=pltpu.MemorySpace.HBM) for s in land_shapes]

    @pl.kernel(mesh=plsc.ScalarSubcoreMesh(axis_name="sequencer", num_cores=1), name=name,
               scratch_types=(pltpu.SemaphoreType.DMA((7 * n,)), pltpu.SemaphoreType.DMA((7 * n,)),
                              pltpu.SemaphoreType.DMA((n,))),
               compiler_params=pltpu.CompilerParams(collective_id=collective_id))
    def launch(send_sems, recv_sems, local_sems):
        place = _place()
        barrier = pltpu.get_barrier_semaphore()
        for to in _peers(place):
            pl.semaphore_signal(barrier, inc=1, device_id=to, device_id_type=MESH)
        pl.semaphore_wait(barrier, 7)
        remote, local = plan(place, src_refs, land_refs)
        mine = [pltpu.make_async_copy(s, d, local_sems.at[a]) for a, (s, d) in enumerate(local)]
        for cp in mine:
            cp.start()
        cps = [pltpu.make_async_remote_copy(src_ref=s, dst_ref=d, send_sem=send_sems.at[k], recv_sem=recv_sems.at[k],
                                            device_id=to, device_id_type=MESH) for k, (s, d, to) in enumerate(remote)]
        for cp in cps:
            cp.start()
        for cp in mine:
            cp.wait()
        for cp in cps:
            cp.wait()

    launch()
    return land_refs


def sibling_pair(name, hs):
    n = len(hs)

    def body(*refs):
        ins, outs = refs[:n], refs[n:2 * n]
        send_sems, recv_sems = refs[2 * n:]
        x, y, c = _place()
        cps = [pltpu.make_async_remote_copy(src_ref=outs[a].at[c], dst_ref=outs[a].at[c], send_sem=send_sems.at[a],
                                            recv_sem=recv_sems.at[a], device_id=(x, y, 1 - c), device_id_type=MESH)
               for a in range(n)]
        for cp in cps:
            cp.start()
        for a in range(n):
            pltpu.make_async_remote_copy(src_ref=outs[a].at[1 - c], dst_ref=outs[a].at[1 - c], send_sem=send_sems.at[a],
                                         recv_sem=recv_sems.at[a], device_id=(x, y, 1 - c),
                                         device_id_type=MESH).wait_recv()
        for cp in cps:
            cp.wait_send()

    any_spec = pl.BlockSpec(memory_space=pl.ANY)
    return pl.pallas_call(
        body, name=name, out_shape=[jax.ShapeDtypeStruct(h.shape, h.dtype) for h in hs],
        in_specs=[any_spec] * n, out_specs=[any_spec] * n, input_output_aliases={a: a for a in range(n)},
        scratch_shapes=[pltpu.SemaphoreType.DMA((n,)), pltpu.SemaphoreType.DMA((n,))])(*hs)


def owner_sum(name, land):
    _, r, w = land.shape
    T = r // 2

    def body(_, l_ref, o_ref):
        acc = l_ref[0].astype(F32)
        for j in range(1, 8):
            acc = acc + l_ref[j].astype(F32)
        o_ref[...] = acc

    grid_spec = pltpu.PrefetchScalarGridSpec(
        num_scalar_prefetch=1, grid=(2,),
        in_specs=[pl.BlockSpec((8, T, w), lambda i, at: (0, i, 0))],
        out_specs=pl.BlockSpec((None, T, w), lambda i, at: (at[0], i, 0)))
    at = jnp.stack([lax.axis_index("c")]).astype(jnp.int32)
    return pl.pallas_call(body, name=name, grid_spec=grid_spec, out_shape=jax.ShapeDtypeStruct((2, r, w), F32),
                          compiler_params=_cp("arbitrary"))(at, land)


W_IN_RUNS = ((0, 2, 1296, 376), (376, 3, 0, 1672), (2048, 1, 920, 752), (2800, 2, 0, 1296), (4096, 0, 0, 1024),
             (5120, 0, 1024, 648), (5768, 1, 0, 920))


def w_in_to_padded(name, g4):
    T = 128

    def body(g_ref, o_ref):
        o_ref[:, D_PROJ:PW] = jnp.zeros((T, PW - D_PROJ), o_ref.dtype)
        for (a, s, j0, w) in W_IN_RUNS:
            o_ref[:, a:a + w] = g_ref[s, :, j0:j0 + w]

    return pl.pallas_call(body, name=name, grid=(D // T,), in_specs=[pl.BlockSpec((4, T, 1672), lambda i: (0, i, 0))],
                          out_specs=pl.BlockSpec((T, PW), lambda i: (i, 0)),
                          out_shape=jax.ShapeDtypeStruct((D, PW), g4.dtype), compiler_params=_cp("arbitrary"))(g4)


def w_in_from_padded(name, gp):
    T = 128

    def body(g_ref, o_ref):
        for (a, s, j0, w) in W_IN_RUNS:
            o_ref[s, :, j0:j0 + w] = g_ref[:, a:a + w]

    return pl.pallas_call(body, name=name, grid=(D // T,), in_specs=[pl.BlockSpec((T, PW), lambda i: (i, 0))],
                          out_specs=pl.BlockSpec((4, T, 1672), lambda i: (0, i, 0)),
                          out_shape=jax.ShapeDtypeStruct((4, D, 1672), gp.dtype), compiler_params=_cp("arbitrary"))(gp)


def sum_devices(name, g):
    def body(g_ref, o_ref):
        acc = g_ref[0]
        for k in range(1, 8):
            acc = acc + g_ref[k]
        o_ref[...] = acc

    return pl.pallas_call(body, name=name, out_shape=jax.ShapeDtypeStruct(g.shape[1:], F32),
                          compiler_params=pltpu.CompilerParams(vmem_limit_bytes=VMEM_LIMIT_V7X))(g)


def adamw(name, w, g, m, v, T):
    r, wd = w.shape
    c1 = 1.0 - ADAM_B1 ** ADAM_STEP
    c2 = 1.0 - ADAM_B2 ** ADAM_STEP

    def body(w_ref, g_ref, m_ref, v_ref, d_ref, mo_ref, vo_ref):
        gv = g_ref[...]
        mn = ADAM_B1 * m_ref[...] + (1.0 - ADAM_B1) * gv
        vn = ADAM_B2 * v_ref[...] + (1.0 - ADAM_B2) * (gv * gv)
        d_ref[...] = -ADAM_LR * ((mn / c1) / (jnp.sqrt(vn / c2) + ADAM_EPS) + ADAM_WD * w_ref[...])
        mo_ref[...] = mn
        vo_ref[...] = vn

    spec = pl.BlockSpec((T, wd), lambda i: (i, 0))
    return pl.pallas_call(body, name=name, grid=(r // T,), in_specs=[spec] * 4, out_specs=[spec] * 3,
                          out_shape=[jax.ShapeDtypeStruct((r, wd), F32)] * 3, compiler_params=_cp("arbitrary"))(w, g, m, v)


BIG = {'w_in': (1024, 1672), 'w_ssd_proj': (256, 1024), 'w_gm_proj': (256, 1024), 'w_out': (256, 1024),
       'w_ff1': (1024, 704), 'w_ff3': (1024, 704), 'w_ff2': (704, 1024)}


class Flat:
    def __init__(self, segs):
        self.off, o = {}, 0
        for name, size in segs:
            self.off[name] = (o, size)
            o += -(-size // 128) * 128
        self.rows = -(-o // 1024) * 8

    def pack(self, vals):
        parts = []
        for name, (o, size) in self.off.items():
            v = vals[name].reshape(-1).astype(F32)
            parts.append(jnp.pad(v, (0, -(-size // 128) * 128 - size)))
        buf = jnp.concatenate(parts)
        return jnp.pad(buf, (0, self.rows * 128 - buf.shape[0])).reshape(self.rows, 128)

    def get(self, buf, name, shape=None):
        o, size = self.off[name]
        v = buf[o // 128:(o + size + 127) // 128].reshape(-1)[:size]
        return v if shape is None else v.reshape(shape)


PARTIALS = Flat([('loss', 1), ('ln0_g', D), ('ln0_b', D), ('dmod_x', 6 * D), ('dmod_c', 6 * D), ('conv_w', 5 * 1536),
                 ('conv_b', 1536), ('dt_bias', 32), ('a_log', 32), ('d_skip', 32), ('ssd_norm_g', D),
                 ('gm_norm_g', D), ('gm_norm_b', D), ('w_spatial', 8 * Q * Q), ('b_spatial', 8 * Q), ('b_gate', 2 * D),
                 ('ln1_g', D), ('ln1_b', D), ('ln2_g', D), ('ln2_b', D)])

WEIGHTS = ('c_ctx', 'ln0_g', 'ln0_b', 'w_ada', 'b_ada', 'w_in', 'conv_w', 'conv_b', 'dt_bias', 'a_log', 'd_skip',
           'ssd_norm_g', 'gm_norm_g', 'gm_norm_b', 'w_spatial', 'b_spatial', 'b_gate', 'w_ssd_proj', 'w_gm_proj',
           'w_out', 'ln1_g', 'ln1_b', 'w_ff1', 'w_ff3', 'w_ff2', 'ln2_g', 'ln2_b')
BIG_NAMES = tuple(BIG)
SMALL_NAMES = tuple(n for n in WEIGHTS if n not in BIG_NAMES and n != 'w_ada')


def kernel(x, c, ctx, c_ctx, ln0_g, ln0_b, w_ada, b_ada, w_in, conv_w, conv_b, dt_bias, a_log, d_skip, ssd_norm_g, gm_norm_g, gm_norm_b, w_spatial, b_spatial, b_gate, w_ssd_proj, w_gm_proj, w_out, ln1_g, ln1_b, w_ff1, w_ff3, w_ff2, ln2_g, ln2_b, loss_target, m_c_ctx, m_ln0_g, m_ln0_b, m_w_ada, m_b_ada, m_w_in, m_conv_w, m_conv_b, m_dt_bias, m_a_log, m_d_skip, m_ssd_norm_g, m_gm_norm_g, m_gm_norm_b, m_w_spatial, m_b_spatial, m_b_gate, m_w_ssd_proj, m_w_gm_proj, m_w_out, m_ln1_g, m_ln1_b, m_w_ff1, m_w_ff3, m_w_ff2, m_ln2_g, m_ln2_b, v_c_ctx, v_ln0_g, v_ln0_b, v_w_ada, v_b_ada, v_w_in, v_conv_w, v_conv_b, v_dt_bias, v_a_log, v_d_skip, v_ssd_norm_g, v_gm_norm_g, v_gm_norm_b, v_w_spatial, v_b_spatial, v_b_gate, v_w_ssd_proj, v_w_gm_proj, v_w_out, v_ln1_g, v_ln1_b, v_w_ff1, v_w_ff3, v_w_ff2, v_ln2_g, v_ln2_b):
    wts = dict(c_ctx=c_ctx, ln0_g=ln0_g, ln0_b=ln0_b, w_ada=w_ada, b_ada=b_ada, w_in=w_in, conv_w=conv_w, conv_b=conv_b,
               dt_bias=dt_bias, a_log=a_log, d_skip=d_skip, ssd_norm_g=ssd_norm_g, gm_norm_g=gm_norm_g,
               gm_norm_b=gm_norm_b, w_spatial=w_spatial, b_spatial=b_spatial, b_gate=b_gate, w_ssd_proj=w_ssd_proj,
               w_gm_proj=w_gm_proj, w_out=w_out, ln1_g=ln1_g, ln1_b=ln1_b, w_ff1=w_ff1, w_ff3=w_ff3, w_ff2=w_ff2,
               ln2_g=ln2_g, ln2_b=ln2_b)
    ms = dict(zip(WEIGHTS, (m_c_ctx, m_ln0_g, m_ln0_b, m_w_ada, m_b_ada, m_w_in, m_conv_w, m_conv_b, m_dt_bias, m_a_log,
                            m_d_skip, m_ssd_norm_g, m_gm_norm_g, m_gm_norm_b, m_w_spatial, m_b_spatial, m_b_gate,
                            m_w_ssd_proj, m_w_gm_proj, m_w_out, m_ln1_g, m_ln1_b, m_w_ff1, m_w_ff3, m_w_ff2, m_ln2_g,
                            m_ln2_b)))
    vs = dict(zip(WEIGHTS, (v_c_ctx, v_ln0_g, v_ln0_b, v_w_ada, v_b_ada, v_w_in, v_conv_w, v_conv_b, v_dt_bias, v_a_log,
                            v_d_skip, v_ssd_norm_g, v_gm_norm_g, v_gm_norm_b, v_w_spatial, v_b_spatial, v_b_gate,
                            v_w_ssd_proj, v_w_gm_proj, v_w_out, v_ln1_g, v_ln1_b, v_w_ff1, v_w_ff3, v_w_ff2, v_ln2_g,
                            v_ln2_b)))
    px, py, pc = _place()
    shard = 2 * px + py
    dev = 2 * shard + pc
    take = lambda a, i, axis=0: lax.dynamic_index_in_dim(a, i, axis, keepdims=False)

    half = lambda n: take(wts[n][0].reshape(2, BIG[n][0] // 2, BIG[n][1]), pc).astype(BF16)

    pre = jnp.concatenate([c, jnp.pad(conv_w[0], ((0, 0), (0, D - 384))), jnp.zeros((2, D), F32)], axis=0)
    pre = allgather8_small("gather_cond", pre)
    conv_w_full = pre[0::2, 1:6, :384].transpose(1, 0, 2).reshape(5, 1536)
    a16 = jnp.concatenate([_silu(pre[:, 0, :]), _silu(c_ctx)[None], jnp.zeros((7, D), F32)], axis=0)
    mod = matmul("ada_fwd", a16, w_ada[0], 'nn', 16, 512, 1024)
    mod = mod + lax.dynamic_slice_in_dim(b_ada[0], shard * 1536, 1536)[None]
    mod = allgather8_small("gather_mod", mod)
    mod = jnp.concatenate([mod[0], mod[2], mod[4], mod[6]], axis=1)
    mod_x = take(mod, dev).reshape(6, D)
    mod_c = mod[8].reshape(6, D)

    def full(n, blocks):
        r, w = BIG[n]
        return blocks.reshape(4, r, w) if w != D else blocks.reshape(4 * r, w)

    class Exchanges:
        rest_names = BIG_NAMES[1:]

        def __init__(self):
            self.pending = []

        def w_in(self):
            blocks = allgather8("gather_w_in", half('w_in'), True)
            w = w_in_to_padded("w_in_layout", full('w_in', blocks))
            halves = [half(n) for n in self.rest_names]
            halves[0], _ = lax.optimization_barrier((halves[0], blocks))
            lands = [jax.ShapeDtypeStruct((8,) + h.shape, BF16) for h in halves]
            self.rest_refs = sequencer_exchange("gather_rest", 1, halves, lands, plan_gather)
            return w

        def rest(self):
            return {n: full(n, r[...]) for n, r in zip(self.rest_names, self.rest_refs)}

        def grads(self, group, gs):
            if group == 'in':
                gs = {'w_in': w_in_from_padded("w_in_grad_layout", gs['w_in'])}
            blocks = [g.reshape(4, 2, BIG[n][0] // 2, BIG[n][1]) for n, g in gs.items()]
            lands = [jax.ShapeDtypeStruct((8,) + b.shape[2:], BF16) for b in blocks]
            refs = sequencer_exchange("grads_" + group, 2 + len(self.pending), blocks, lands, plan_to_owner)
            self.pending.append((tuple(gs), refs))

        def finish(self):
            names, halves = [], []
            for ns, refs in self.pending:
                names += ns
                halves += [owner_sum("grads_sum_" + n, r[...]) for n, r in zip(ns, refs)]
            return {n: h.reshape(BIG[n]) for n, h in zip(names, sibling_pair("grads_halves", halves))}

    S = dict(ln0_g=ln0_g, ln0_b=ln0_b, conv_w=conv_w_full, conv_b=conv_b[0], dt_bias=dt_bias[0], a_log=a_log[0],
             d_skip=d_skip[0], ssd_norm_g=ssd_norm_g[0], gm_norm_g=gm_norm_g[0], gm_norm_b=gm_norm_b[0],
             w_spatial=w_spatial[0], b_spatial=b_spatial[0], b_gate=b_gate[0], ln1_g=ln1_g[0], ln1_b=ln1_b[0],
             ln2_g=ln2_g[0], ln2_b=ln2_b[0])
    x_all = jnp.concatenate([ctx[0], x[0]], axis=0)
    exchanges = Exchanges()
    grad_x, gsmall = core(x_all, loss_target[0], mod_x, mod_c, exchanges, S)

    parts = allgather8("gather_partials", PARTIALS.pack(gsmall), False)
    tot = sum_devices("partials_sum", parts)
    g_shards = exchanges.finish()
    g = {n: PARTIALS.get(tot, n) for n in ('ln0_g', 'ln0_b', 'conv_b', 'dt_bias', 'a_log', 'd_skip', 'ssd_norm_g',
                                           'gm_norm_g', 'gm_norm_b', 'w_spatial', 'b_spatial', 'b_gate', 'ln1_g',
                                           'ln1_b', 'ln2_g', 'ln2_b')}
    loss = PARTIALS.get(tot, 'loss', ())
    dmod_c = PARTIALS.get(tot, 'dmod_c')
    g['b_ada'] = PARTIALS.get(tot, 'dmod_x') + dmod_c
    g['conv_w'] = lax.dynamic_slice_in_dim(PARTIALS.get(tot, 'conv_w', (5, 1536)), shard * 384, 384, axis=1)
    o, size = PARTIALS.off['dmod_x']
    dmod_rows = parts[:, o // 128:(o + size) // 128].reshape(8, size)
    dm = jnp.concatenate([dmod_rows, dmod_c[None], jnp.zeros((7, 6 * D), F32)], axis=0)
    dm = lax.dynamic_slice_in_dim(dm, shard * 1536, 1536, axis=1)
    g['w_ada'] = matmul("ada_bwd_w", a16, dm, 'tn', 1024, 512, 16)
    dm_c = jnp.concatenate([dm[8:9], jnp.zeros((15, 1536), F32)], axis=0)
    dc = matmul("ada_bwd_c", dm_c, w_ada[0], 'nt', 16, 1024, 512)
    dc = allgather8_small("gather_dcctx", dc)[:, 0, :]
    dc = ((dc[0] + dc[2]) + dc[4]) + dc[6]
    sg = jax.nn.sigmoid(c_ctx)
    g['c_ctx'] = dc * (sg * (1.0 + c_ctx * (1.0 - sg)))
    for n in BIG_NAMES:
        g[n] = g_shards[n]

    delta, new_m, new_v = {}, {}, {}
    for n in BIG_NAMES + ('w_ada',):
        w2 = wts[n][0]
        T = 352 if n == 'w_ff2' else 256
        d_, m_, v_ = adamw("adamw_" + n, w2, g[n], ms[n][0], vs[n][0], T)
        delta[n], new_m[n], new_v[n] = d_, m_, v_
    lay = Flat([(n, wts[n].size) for n in SMALL_NAMES])
    d_, m_, v_ = adamw("adamw_small", lay.pack(wts), lay.pack(g), lay.pack(ms), lay.pack(vs), lay.rows)
    for n in SMALL_NAMES:
        delta[n], new_m[n], new_v[n] = (lay.get(b, n) for b in (d_, m_, v_))

    shp = lambda d: [d[n].reshape(wts[n].shape) for n in WEIGHTS]
    return (loss, grad_x[None], *shp(g), *shp(delta), *shp(new_m), *shp(new_v))
```

```python
import functools

import jax
import jax.numpy as jnp
from jax import lax
from jax.experimental import pallas as pl
from jax.experimental.pallas import tpu as pltpu
from jax.experimental.pallas import tpu_sc as plsc

F32 = jnp.float32
BF16 = jnp.bfloat16
MESH = pl.DeviceIdType.MESH

VMEM_LIMIT_V7X = 56 * 1024 * 1024

D = 1024
LC = 256
Q = 128
NH = 16
D_FF = 2816
LN_EPS = 1e-5
ALPHA = 2.0 ** 0.25

PW = 7168
C_GATE, C_UV, C_Z, C_XBC, C_DT = 0, 2048, 4096, 5120, 6656
D_PROJ = 6688

ADAM_LR, ADAM_B1, ADAM_B2, ADAM_EPS, ADAM_WD, ADAM_STEP = 0.001, 0.9, 0.999, 1e-08, 0.01, 10


def _cp(*sem):
    return pltpu.CompilerParams(dimension_semantics=sem, vmem_limit_bytes=VMEM_LIMIT_V7X)


def _dot(a, b, ca, cb):
    return lax.dot_general(a.astype(BF16), b.astype(BF16), (((ca,), (cb,)), ((), ())),
                           preferred_element_type=F32)


@jax.custom_vjp
def mm(a, b):
    return _dot(a, b, 1, 0)


mm.defvjp(lambda a, b: (_dot(a, b, 1, 0), (a, b)),
          lambda r, g: (_dot(g, r[1], 1, 1), _dot(r[0], g, 0, 0)))


@jax.custom_vjp
def mm_nt(a, b):
    return _dot(a, b, 1, 1)


mm_nt.defvjp(lambda a, b: (_dot(a, b, 1, 1), (a, b)),
             lambda r, g: (_dot(g, r[1], 1, 0), _dot(g, r[0], 0, 0)))


@jax.custom_vjp
def mm_tn(a, b):
    return _dot(a, b, 0, 0)


mm_tn.defvjp(lambda a, b: (_dot(a, b, 0, 0), (a, b)),
             lambda r, g: (_dot(r[1], g, 1, 1), _dot(r[0], g, 1, 0)))


def _dot32(a, b):
    return lax.dot_general(a, b, (((1,), (0,)), ((), ())), precision=lax.Precision.HIGHEST,
                           preferred_element_type=F32)


def _cumsum_fn(rev):
    def tri(transpose):
        r = lax.broadcasted_iota(jnp.int32, (Q, Q), 0)
        c = lax.broadcasted_iota(jnp.int32, (Q, Q), 1)
        keep = (r >= c) if (rev == transpose) else (r <= c)
        return jnp.where(keep, 1.0, 0.0).astype(F32)

    @jax.custom_vjp
    def cums(a):
        return _dot32(tri(False), a)

    cums.defvjp(lambda a: (_dot32(tri(False), a), None), lambda _, g: (_dot32(tri(True), g),))
    return cums


def _cols(v, k):
    w = v.shape[1] // k
    return tuple(v[:, w * i:w * (i + 1)] for i in range(k))


def _splitter(k):
    @jax.custom_vjp
    def split(v):
        return _cols(v, k)

    @jax.custom_vjp
    def concat(ps):
        return jnp.concatenate(ps, axis=1)

    split.defvjp(lambda v: (_cols(v, k), None), lambda _, g: (jnp.concatenate(g, axis=1),))
    concat.defvjp(lambda ps: (jnp.concatenate(ps, axis=1), None), lambda _, g: (_cols(g, k),))
    return split, concat


split2, _ = _splitter(2)
split4, _ = _splitter(4)
split8, concat8 = _splitter(8)


def _ln(x, g, b):
    mu = jnp.mean(x, axis=-1, keepdims=True)
    xc = x - mu
    var = jnp.mean(xc * xc, axis=-1, keepdims=True)
    return xc * lax.rsqrt(var + LN_EPS) * g + b


def _silu(x):
    return x * jax.nn.sigmoid(x)


def _gelu(x):
    return 0.5 * x * (1.0 + jnp.tanh(0.7978845608028654 * (x + 0.044715 * (x * x * x))))


def _xspec(T, w, col, roff):
    return pl.BlockSpec((T, w), lambda i, col=col, roff=roff: (jnp.maximum(i + roff, 0), col))


def _pspec(p, sel):
    if sel is None:
        return pl.BlockSpec(p.shape, lambda i, n=p.ndim: (0,) * n)
    return pl.BlockSpec((1,) + p.shape[1:], lambda i, n=p.ndim: (sel(i),) + (0,) * (n - 1))


def _out_plumbing(outs, T, args, in_specs):
    shapes, specs, aliases = [], [], {}
    for k, o in enumerate(outs):
        if o[0] == 'new':
            _, rows, w, roff = o[:4]
            shapes.append(jax.ShapeDtypeStruct((rows, w), o[4] if len(o) > 4 else F32))
            specs.append(_xspec(T, w, 0, roff))
        elif o[0] == 'acc':
            shapes.append(jax.ShapeDtypeStruct(o[1], F32))
            specs.append(pl.BlockSpec(o[1], lambda i, n=len(o[1]): (0,) * n))
        elif o[0] == 'part':
            _, rows, wtot, w, col, roff, dtype = o
            shapes.append(jax.ShapeDtypeStruct((rows, wtot), dtype))
            specs.append(_xspec(T, w, col, roff))
        else:
            _, arr, w, col, roff = o
            aliases[len(args)] = k
            args.append(arr)
            in_specs.append(pl.BlockSpec(memory_space=pl.ANY))
            shapes.append(jax.ShapeDtypeStruct(arr.shape, arr.dtype))
            specs.append(_xspec(T, w, col, roff))
    return shapes, specs, aliases


def stage_fwd(name, f, T, n, xs, ps, outs):
    nx, npar = len(xs), len(ps)
    args = [x[0] for x in xs] + [p[0] for p in ps]
    in_specs = [_xspec(T, w, col, roff) for (_, w, col, roff) in xs] + [_pspec(p, sel) for (p, sel) in ps]
    n_in = len(args)
    shapes, specs, aliases = _out_plumbing(outs, T, args, in_specs)
    n_all_in = len(args)

    def body(*refs):
        i = pl.program_id(0)
        xv = [r[...] for r in refs[:nx]]
        pv = [r[...] if ps[k][1] is None else r[0] for k, r in enumerate(refs[nx:n_in])]
        res = f(*xv, *pv)
        for k, o_ref in enumerate(refs[n_all_in:]):
            if outs[k][0] == 'acc':
                @pl.when(i == 0)
                def _(o_ref=o_ref, v=res[k]):
                    o_ref[...] = v

                @pl.when(i > 0)
                def _(o_ref=o_ref, v=res[k]):
                    o_ref[...] += v
            else:
                o_ref[...] = res[k].astype(o_ref.dtype)

    return pl.pallas_call(body, name=name, grid=(n,), in_specs=in_specs, out_specs=specs, out_shape=shapes,
                          input_output_aliases=aliases, compiler_params=_cp("arbitrary"))(*args)


def stage_bwd(name, f, T, n, xs, ps, cts, dxs, dps, primal=()):
    nx, npar = len(xs), len(ps)
    args = [x[0] for x in xs] + [p[0] for p in ps]
    in_specs = [_xspec(T, w, col, roff) for (_, w, col, roff) in xs] + [_pspec(p, sel) for (p, sel) in ps]
    ct_arrs = [c for c in cts if isinstance(c, tuple)]
    for (a, w, col, roff) in ct_arrs:
        args.append(a)
        in_specs.append(_xspec(T, w, col, roff))
    n_in = len(args)
    outs, out_of = [], []
    for k, o in enumerate(dxs):
        if o is not None:
            outs.append(o)
            out_of.append(('x', k))
    for k, want in enumerate(dps):
        if want:
            p, sel = ps[k]
            outs.append(('acc', p.shape))
            out_of.append(('p', k))
    for k, shape in primal:
        outs.append(('acc', shape))
        out_of.append(('r', k))
    shapes, specs, aliases = _out_plumbing(outs, T, args, in_specs)
    for j, (kind, k) in enumerate(out_of):
        if kind == 'p' and ps[k][1] is not None:
            p, sel = ps[k]
            specs[j] = pl.BlockSpec((1,) + p.shape[1:], lambda i, n=p.ndim, sel=sel: (sel(i),) + (0,) * (n - 1))
    n_all_in = len(args)

    def body(*refs):
        i = pl.program_id(0)
        xv = [r[...] for r in refs[:nx]]
        pv = [r[...] if ps[k][1] is None else r[0] for k, r in enumerate(refs[nx:nx + npar])]
        res, vjp_fn = jax.vjp(f, *xv, *pv)
        ctv, q = [], nx + npar
        for k, c in enumerate(cts):
            if c is None:
                ctv.append(jnp.zeros_like(res[k]))
            elif isinstance(c, tuple):
                v = refs[q][...]
                if c[3] < 0:
                    v = v * (i + c[3] >= 0).astype(F32)
                ctv.append(v)
                q += 1
            else:
                ctv.append(jnp.full_like(res[k], c))
        grads = vjp_fn(tuple(ctv))
        for j, o_ref in enumerate(refs[n_all_in:]):
            kind, k = out_of[j]
            if kind == 'x':
                o_ref[...] = grads[k].astype(o_ref.dtype)
            else:
                g = res[k] if kind == 'r' else grads[nx + k]
                sel = None if kind == 'r' else ps[k][1]
                if sel is None:
                    first = i == 0
                    tgt = o_ref
                else:
                    first = jnp.logical_or(i == 0, sel(i) != sel(jnp.maximum(i - 1, 0)))
                    tgt = o_ref.at[0]

                @pl.when(first)
                def _(tgt=tgt, g=g):
                    tgt[...] = g

                @pl.when(jnp.logical_not(first))
                def _(tgt=tgt, g=g):
                    tgt[...] += g

    return pl.pallas_call(body, name=name, grid=(n,), in_specs=in_specs, out_specs=specs, out_shape=shapes,
                          input_output_aliases=aliases, compiler_params=_cp("arbitrary"))(*args)


_CONTRACT = {'nn': (1, 0), 'nt': (1, 1), 'tn': (0, 0)}


def matmul(name, a, b, mode, tm, tn, tk, out_dtype=F32, add=None):
    if mode == 'nn':
        (M, K), (_, N) = a.shape, b.shape
    elif mode == 'nt':
        (M, K), (N, _) = a.shape, b.shape
    else:
        (K, M), (_, N) = a.shape, b.shape
    assert M % tm == 0 and N % tn == 0 and K % tk == 0, (name, M, N, K, tm, tn, tk)
    a_spec = (pl.BlockSpec((tk, tm), lambda j, i, k: (k, i)) if mode == 'tn'
              else pl.BlockSpec((tm, tk), lambda j, i, k: (i, k)))
    b_spec = (pl.BlockSpec((tn, tk), lambda j, i, k: (j, k)) if mode == 'nt'
              else pl.BlockSpec((tk, tn), lambda j, i, k: (k, j)))
    o_spec = pl.BlockSpec((tm, tn), lambda j, i, k: (i, j))
    return matmul_call(name, (N // tn, M // tm, K // tk), a, a_spec, b, b_spec, (M, N), o_spec, (tm, tn), mode,
                       out_dtype, add)


def matmul_call(name, grid, a, a_spec, b, b_spec, out_shape, o_spec, tile, mode, out_dtype=F32, add=None):
    tm, tn = tile
    nk = grid[2]
    ca, cb = _CONTRACT[mode]
    args, in_specs = [a, b], [a_spec, b_spec]
    if add is not None:
        args.append(add)
        in_specs.append(o_spec)

    def body(*refs):
        a_ref, b_ref = refs[0], refs[1]
        o_ref, acc = refs[-2], refs[-1]
        k = pl.program_id(2)
        if nk == 1:
            p = _dot(a_ref[...], b_ref[...], ca, cb)
            o_ref[...] = (p + refs[2][...] if add is not None else p).astype(out_dtype)
            return

        @pl.when(k == 0)
        def _():
            acc[...] = refs[2][...] if add is not None else jnp.zeros_like(acc)

        acc[...] += _dot(a_ref[...], b_ref[...], ca, cb)

        @pl.when(k == nk - 1)
        def _():
            o_ref[...] = acc[...].astype(out_dtype)

    return pl.pallas_call(body, name=name, grid=grid, in_specs=in_specs, out_specs=o_spec,
                          out_shape=jax.ShapeDtypeStruct(out_shape, out_dtype),
                          scratch_shapes=[pltpu.VMEM((tm, tn) if nk > 1 else (8, 128), F32)],
                          compiler_params=_cp("arbitrary", "arbitrary", "arbitrary"))(*args)


NS, WS = 4, 704


def _resident(name, M, tm, rows, weight, out_shape, out_block, out_map, step, add=None):
    args = [rows[0], weight] + ([] if add is None else [add])
    in_specs = [pl.BlockSpec(rows[1], rows[2]), pl.BlockSpec(weight.shape, lambda i, n=weight.ndim: (0,) * n)]
    if add is not None:
        in_specs.append(pl.BlockSpec(out_block, out_map))
    return pl.pallas_call(step, name=name, grid=(M // tm,), in_specs=in_specs, out_specs=pl.BlockSpec(out_block, out_map),
                          out_shape=jax.ShapeDtypeStruct(out_shape, F32), compiler_params=_cp("arbitrary"))(*args)


def ffn_in_fwd(name, h, w1, w3, tm):
    M = h.shape[0]

    def step(h_ref, w1_ref, w3_ref, a1_ref, a3_ref, act_ref):
        for s in range(NS):
            a1 = _dot(h_ref[...], w1_ref[s], 1, 0)
            a3 = _dot(h_ref[...], w3_ref[s], 1, 0)
            a1_ref[s] = a1.astype(a1_ref.dtype)
            a3_ref[s] = a3.astype(a3_ref.dtype)
            act_ref[s] = (_silu(a1) * a3).astype(act_ref.dtype)

    wspec = pl.BlockSpec((NS, D, WS), lambda i: (0, 0, 0))
    ospec = pl.BlockSpec((NS, tm, WS), lambda i: (0, i, 0))
    return pl.pallas_call(
        step, name=name, grid=(M // tm,), in_specs=[pl.BlockSpec((tm, D), lambda i: (i, 0)), wspec, wspec],
        out_specs=[ospec, ospec, ospec],
        out_shape=[jax.ShapeDtypeStruct((NS, M, WS), BF16)] * 3, compiler_params=_cp("arbitrary"))(h, w1, w3)


def ffn_out_bwd_x(name, dff, w2, a1, a3, tm):
    M = dff.shape[0]

    def step(d_ref, w_ref, a1_ref, a3_ref, da1_ref, da3_ref):
        for s in range(NS):
            dact = _dot(d_ref[...], w_ref[s * WS:(s + 1) * WS, :], 1, 1)
            a1 = a1_ref[s].astype(F32)
            sig = jax.nn.sigmoid(a1)
            da3_ref[s] = (dact * (a1 * sig)).astype(da3_ref.dtype)
            da1_ref[s] = (dact * a3_ref[s].astype(F32) * (sig * (1.0 + a1 * (1.0 - sig)))).astype(da1_ref.dtype)

    aspec = pl.BlockSpec((NS, tm, WS), lambda i: (0, i, 0))
    return pl.pallas_call(
        step, name=name, grid=(M // tm,),
        in_specs=[pl.BlockSpec((tm, D), lambda i: (i, 0)), pl.BlockSpec(w2.shape, lambda i: (0, 0)), aspec, aspec],
        out_specs=[aspec, aspec],
        out_shape=[jax.ShapeDtypeStruct((NS, M, WS), BF16)] * 2, compiler_params=_cp("arbitrary"))(dff, w2, a1, a3)


def ff_in_bwd_x(name, da3, w3, tm, add=None):
    M = da3.shape[1]

    def step(*refs):
        d_ref, w_ref, o_ref = refs[0], refs[1], refs[-1]
        acc = _dot(d_ref[0], w_ref[0], 1, 1)
        for s in range(1, NS):
            acc = acc + _dot(d_ref[s], w_ref[s], 1, 1)
        o_ref[...] = acc if add is None else acc + refs[2][...]

    return _resident(name, M, tm, (da3, (NS, tm, WS), lambda i: (0, i, 0)), w3, (M, D), (tm, D), lambda i: (i, 0), step, add)


def ff_in_bwd_w(name, h, da3, tk):
    M = h.shape[0]

    def step(h_ref, d_ref, acc):
        for s in range(NS):
            acc[s] += _dot(h_ref[...], d_ref[s], 0, 0)

    return _token_sum(name, M // tk, [pl.BlockSpec((tk, D), lambda k: (k, 0)), pl.BlockSpec((NS, tk, WS), lambda k: (0, k, 0))],
                      (NS, D, WS), (NS, D, WS), step, (h, da3))


def ff_out_fwd(name, act3, w2, tm):
    M = act3.shape[1]

    def step(a_ref, w_ref, o_ref):
        acc = _dot(a_ref[0], w_ref[0:WS, :], 1, 0)
        for s in range(1, NS):
            acc = acc + _dot(a_ref[s], w_ref[s * WS:(s + 1) * WS, :], 1, 0)
        o_ref[...] = acc

    return _resident(name, M, tm, (act3, (NS, tm, WS), lambda i: (0, i, 0)), w2, (M, D), (tm, D), lambda i: (i, 0), step)


def _token_sum(name, nk, in_specs, out_shape, acc_shape, step, args):
    def body(*refs):
        o_ref, acc = refs[-2], refs[-1]
        k = pl.program_id(0)

        @pl.when(k == 0)
        def _():
            acc[...] = jnp.zeros_like(acc)

        step(*refs[:-2], acc)

        @pl.when(k == nk - 1)
        def _():
            o_ref[...] = acc[...].astype(o_ref.dtype)

    return pl.pallas_call(body, name=name, grid=(nk,), in_specs=in_specs,
                          out_specs=pl.BlockSpec(out_shape, lambda k, n=len(out_shape): (0,) * n),
                          out_shape=jax.ShapeDtypeStruct(out_shape, BF16), scratch_shapes=[pltpu.VMEM(acc_shape, F32)],
                          compiler_params=_cp("arbitrary"))(*args)


def ff_out_bwd_w(name, act3, dff, tk):
    M = dff.shape[0]

    def step(a_ref, d_ref, acc):
        for s in range(NS):
            acc[s * WS:(s + 1) * WS, :] += _dot(a_ref[s], d_ref[...], 0, 0)

    return _token_sum(name, M // tk, [pl.BlockSpec((NS, tk, WS), lambda k: (0, k, 0)), pl.BlockSpec((tk, D), lambda k: (k, 0))],
                      (NS * WS, D), (NS * WS, D), step, (act3, dff))


HALO = 8


def _windows(pad_ref, v):
    n = v.shape[0]
    edge = jnp.zeros((HALO, 128), F32)
    pad_ref[0:HALO, :] = edge
    pad_ref[HALO:HALO + n, :] = v
    pad_ref[HALO + n:2 * HALO + n, :] = edge
    return lambda k: pad_ref[HALO + k - 2:HALO + k - 2 + n, :]


def _conv_pre(tap, w_ref, b_ref):
    acc = tap(0) * w_ref[0:1, :] + b_ref[...]
    for k in range(1, 5):
        acc = acc + tap(k) * w_ref[k:k + 1, :]
    return acc


def conv_fwd(name, proj, conv_w, conv_b, R):
    segs = ((0, LC), (LC, R))

    def body(x_ref, w_ref, b_ref, o_ref, xp):
        for (s, e) in segs:
            pre = _conv_pre(_windows(xp, x_ref[s:e, :]), w_ref, b_ref)
            o_ref[s:e, :] = _silu(pre)

    return pl.pallas_call(
        body, name=name, grid=(12,),
        in_specs=[pl.BlockSpec((R, 128), lambda j: (0, C_XBC // 128 + j)),
                  pl.BlockSpec((8, 128), lambda j: (0, j)), pl.BlockSpec((1, 128), lambda j: (0, j))],
        out_specs=pl.BlockSpec((R, 128), lambda j: (0, j)),
        out_shape=jax.ShapeDtypeStruct((R, 1536), F32), scratch_shapes=[pltpu.VMEM((R - LC + 2 * HALO, 128), F32)],
        compiler_params=_cp("arbitrary"))(proj, conv_w, conv_b)


def conv_bwd(name, proj, conv_w, conv_b, d_f, d_b, d_skip, dproj, R):
    segs = ((0, LC), (LC, R))

    def body(x_ref, w_ref, b_ref, df_ref, db_ref, ds_ref, _, dx_ref, dw_ref, dbias_ref, xp, dp):
        j = pl.program_id(0)
        has_skip = (j < 8).astype(F32)
        dw = [jnp.zeros((1, 128), F32) for _ in range(5)]
        dbias = jnp.zeros((1, 128), F32)
        for (s, e) in segs:
            xw = _windows(xp, x_ref[s:e, :])
            pre = _conv_pre(xw, w_ref, b_ref)
            sig = jax.nn.sigmoid(pre)
            dy = df_ref[s:e, :] + db_ref[s:e, :]
            if s == LC:
                dy = dy + ds_ref[...] * has_skip
            dpre = dy * (sig * (1.0 + pre * (1.0 - sig)))
            dtap = _windows(dp, dpre)
            dx = dtap(4) * w_ref[0:1, :]
            for k in range(1, 5):
                dx = dx + dtap(4 - k) * w_ref[k:k + 1, :]
            for k in range(5):
                dw[k] = dw[k] + jnp.sum(dtap(2) * xw(k), axis=0, keepdims=True)
            dbias = dbias + jnp.sum(dpre, axis=0, keepdims=True)
            dx_ref[s:e, :] = dx.astype(dx_ref.dtype)
        dw_ref[...] = jnp.zeros_like(dw_ref)
        for k in range(5):
            dw_ref[k:k + 1, :] = dw[k]
        dbias_ref[...] = dbias

    pad = pltpu.VMEM((R - LC + 2 * HALO, 128), F32)
    return pl.pallas_call(
        body, name=name, grid=(12,),
        in_specs=[pl.BlockSpec((R, 128), lambda j: (0, C_XBC // 128 + j)),
                  pl.BlockSpec((8, 128), lambda j: (0, j)), pl.BlockSpec((1, 128), lambda j: (0, j)),
                  pl.BlockSpec((R, 128), lambda j: (0, j)), pl.BlockSpec((R, 128), lambda j: (0, j)),
                  pl.BlockSpec((R - LC, 128), lambda j: (0, jnp.minimum(j, 7))),
                  pl.BlockSpec(memory_space=pl.ANY)],
        out_specs=[pl.BlockSpec((R, 128), lambda j: (0, C_XBC // 128 + j)),
                   pl.BlockSpec((8, 128), lambda j: (0, j)), pl.BlockSpec((1, 128), lambda j: (0, j))],
        out_shape=[jax.ShapeDtypeStruct(dproj.shape, dproj.dtype), jax.ShapeDtypeStruct((8, 1536), F32),
                   jax.ShapeDtypeStruct((1, 1536), F32)], scratch_shapes=[pad, pad],
        input_output_aliases={6: 0}, compiler_params=_cp("arbitrary"))(proj, conv_w, conv_b, d_f, d_b, d_skip, dproj)


def _ssd_chunk(rev, dirn):
    cums = _cumsum_fn(rev)

    def f(xs, Bs, Cs, dt, alog, Hs):
        lane = lax.broadcasted_iota(jnp.int32, (1, 128), 1)
        sub = lax.broadcasted_iota(jnp.int32, (Q, 1), 0)
        r = lax.broadcasted_iota(jnp.int32, (Q, Q), 0)
        c = lax.broadcasted_iota(jnp.int32, (Q, Q), 1)
        mask = (r <= c) if rev else (r >= c)
        left = lane < 64
        a = dt * (-jnp.exp(alog))
        s = cums(a)
        sT, dtT = s.T, dt.T
        last_row = (sub == (0 if rev else Q - 1)).astype(F32)
        s_last = jnp.sum(s * last_row, axis=0, keepdims=True)
        G = [mm_nt(Cs[g], Bs[g]) for g in range(2)]
        M, es, wc, ed = [], [], [], []
        for h in range(NH):
            l = 16 * dirn + h
            oh_l = (lane == l).astype(F32)
            oh_s = (sub == l).astype(F32)
            s_col = jnp.sum(s * oh_l, axis=1, keepdims=True)
            dt_col = jnp.sum(dt * oh_l, axis=1, keepdims=True)
            s_row = jnp.sum(sT * oh_s, axis=0, keepdims=True)
            dt_row = jnp.sum(dtT * oh_s, axis=0, keepdims=True)
            sl = jnp.sum(s_last * oh_l, axis=1, keepdims=True)
            seg = jnp.where(mask, s_col - s_row, 0.0)
            lm = jnp.where(mask, jnp.exp(seg), 0.0)
            M.append(G[h // 8] * lm * dt_row)
            es.append(jnp.exp(s_col))
            wc.append(jnp.exp(sl - s_col) * dt_col)
            ed.append(jnp.exp(sl))
        Ys, Hn = [], []
        for j in range(8):
            g = j // 4
            xa = jnp.where(left, xs[j], 0.0)
            xb = jnp.where(left, 0.0, xs[j])
            yd = mm(M[2 * j], xa) + mm(M[2 * j + 1], xb)
            yo = mm(Cs[g], Hs[j]) * jnp.where(left, es[2 * j], es[2 * j + 1])
            Ys.append(yd + yo)
            st = mm_tn(Bs[g], xs[j] * jnp.where(left, wc[2 * j], wc[2 * j + 1]))
            Hn.append(Hs[j] * jnp.where(left, ed[2 * j], ed[2 * j + 1]) + st)
        return Ys, Hn

    return f


def _chunk_of(t, n, rev):
    if not rev:
        return t
    return jnp.where(t < 2, 1 - t, n + 1 - t)


def ssd_fwd(name, xbc, dt, alog, n, rev, dirn):
    chunk = _ssd_chunk(rev, dirn)

    def body(x_ref, b_ref, c_ref, dt_ref, al_ref, y_ref, hs_ref, h_scr):
        @pl.when(pl.program_id(0) == 0)
        def _():
            h_scr[...] = jnp.zeros_like(h_scr)

        xs = [x_ref[:, 128 * j:128 * (j + 1)] for j in range(8)]
        Bs = [b_ref[:, 128 * g:128 * (g + 1)] for g in range(2)]
        Cs = [c_ref[:, 128 * g:128 * (g + 1)] for g in range(2)]
        Hs = [h_scr[:, 128 * j:128 * (j + 1)] for j in range(8)]
        hs_ref[0] = h_scr[...]
        Ys, Hn = chunk(xs, Bs, Cs, dt_ref[...], al_ref[...], Hs)
        for j in range(8):
            y_ref[:, 128 * j:128 * (j + 1)] = Ys[j]
            h_scr[:, 128 * j:128 * (j + 1)] = Hn[j]

    cm = lambda t: _chunk_of(t, n, rev)
    return pl.pallas_call(
        body, name=name, grid=(n,),
        in_specs=[pl.BlockSpec((Q, 1024), lambda t: (cm(t), 0)), pl.BlockSpec((Q, 256), lambda t: (cm(t), 4)),
                  pl.BlockSpec((Q, 256), lambda t: (cm(t), 5)), pl.BlockSpec((Q, 128), lambda t: (cm(t), 0)),
                  pl.BlockSpec((1, 128), lambda t: (0, 0))],
        out_specs=[pl.BlockSpec((Q, 1024), lambda t: (cm(t), 0)), pl.BlockSpec((1, Q, 1024), lambda t: (cm(t), 0, 0))],
        out_shape=[jax.ShapeDtypeStruct((n * Q, 1024), F32), jax.ShapeDtypeStruct((n, Q, 1024), F32)],
        scratch_shapes=[pltpu.VMEM((Q, 1024), F32)], compiler_params=_cp("arbitrary"))(xbc, xbc, xbc, dt, alog)


def ssd_bwd(name, xbc, dt, alog, hs, dy, n, rev, dirn):
    chunk = _ssd_chunk(rev, dirn)

    def body(x_ref, b_ref, c_ref, dt_ref, al_ref, hs_ref, dy_ref, dx_ref, ddt_ref, dal_ref, dh_scr):
        tt = pl.program_id(0)
        ch = _chunk_of(n - 1 - tt, n, rev)

        @pl.when(tt == 0)
        def _():
            dh_scr[...] = jnp.zeros_like(dh_scr)

        xs = [x_ref[:, 128 * j:128 * (j + 1)] for j in range(8)]
        Bs = [b_ref[:, 128 * g:128 * (g + 1)] for g in range(2)]
        Cs = [c_ref[:, 128 * g:128 * (g + 1)] for g in range(2)]
        Hs = [hs_ref[0, :, 128 * j:128 * (j + 1)] for j in range(8)]
        live = (ch >= 2).astype(F32)
        dYs = [dy_ref[:, 128 * j:128 * (j + 1)] * live for j in range(8)]
        dHn = [dh_scr[:, 128 * j:128 * (j + 1)] for j in range(8)]
        _, vjp_fn = jax.vjp(chunk, xs, Bs, Cs, dt_ref[...], al_ref[...], Hs)
        dxs, dBs, dCs, ddt, dal, dHs = vjp_fn((dYs, dHn))
        for j in range(8):
            dx_ref[:, 128 * j:128 * (j + 1)] = dxs[j]
            dh_scr[:, 128 * j:128 * (j + 1)] = dHs[j]
        for g in range(2):
            dx_ref[:, 1024 + 128 * g:1024 + 128 * (g + 1)] = dBs[g]
            dx_ref[:, 1280 + 128 * g:1280 + 128 * (g + 1)] = dCs[g]
        ddt_ref[...] = ddt

        @pl.when(tt == 0)
        def _():
            dal_ref[...] = dal

        @pl.when(tt > 0)
        def _():
            dal_ref[...] += dal

    cm = lambda t: _chunk_of(n - 1 - t, n, rev)
    return pl.pallas_call(
        body, name=name, grid=(n,),
        in_specs=[pl.BlockSpec((Q, 1024), lambda t: (cm(t), 0)), pl.BlockSpec((Q, 256), lambda t: (cm(t), 4)),
                  pl.BlockSpec((Q, 256), lambda t: (cm(t), 5)), pl.BlockSpec((Q, 128), lambda t: (cm(t), 0)),
                  pl.BlockSpec((1, 128), lambda t: (0, 0)), pl.BlockSpec((1, Q, 1024), lambda t: (cm(t), 0, 0)),
                  pl.BlockSpec((Q, 1024), lambda t: (jnp.maximum(cm(t) - 2, 0), 0))],
        out_specs=[pl.BlockSpec((Q, 1536), lambda t: (cm(t), 0)), pl.BlockSpec((Q, 128), lambda t: (cm(t), 0)),
                   pl.BlockSpec((1, 128), lambda t: (0, 0))],
        out_shape=[jax.ShapeDtypeStruct((n * Q, 1536), F32), jax.ShapeDtypeStruct((n * Q, 128), F32),
                   jax.ShapeDtypeStruct((1, 128), F32)],
        scratch_shapes=[pltpu.VMEM((Q, 1024), F32)], compiler_params=_cp("arbitrary"))(xbc, xbc, xbc, dt, alog, hs, dy)


def f_norm0(x, g0, b0, sc, sh):
    x0 = _ln(x, g0, b0)
    return x0, x0 * (1.0 + sc) + sh


def f_dt(raw, bias):
    z = split4(raw)[0] + bias
    dt = jnp.maximum(z, 0.0) + jnp.log1p(jnp.exp(-jnp.abs(z)))
    return dt, dt


def f_gated_norm(yf, yb, xs, z, dcol, g):
    h = (yf + yb + xs * dcol) * _silu(z)
    return (h * lax.rsqrt(jnp.mean(h * h, axis=-1, keepdims=True) + LN_EPS) * g,)


def f_gmlp(uv, gmg, gmb, *wb):
    ws, bs = wb[:8], wb[8:]
    u, v = split2(uv)
    vn = split8(_ln(_gelu(v), gmg, gmb))
    mixed = concat8(tuple(mm(ws[g], vn[g]) + bs[g] for g in range(8)))
    return (_gelu(u) * mixed,)


def f_merge(ps, pg, gates, bg):
    gs, gg = split2(jax.nn.sigmoid(gates + bg))
    return (gs * ps + gg * pg,)


def f_res1(x0, out, g1, lg, lb, sc, sh):
    x1 = _ln(ALPHA * x0 + g1 * out, lg, lb)
    return x1, x1 * (1.0 + sc) + sh


def f_res2_loss(x1, ff, tgt, g2, lg, lb):
    x2 = _ln(ALPHA * x1 + g2 * ff, lg, lb)
    e = x2 - tgt
    return (0.5 * jnp.sum(jnp.mean(e * e, axis=-1, keepdims=True), axis=0, keepdims=True),)


def _row_tile(M):
    return 544 if M % 544 == 0 else (512 if M % 512 == 0 else M)


def core(x_all, tgt, mod_x, mod_c, X, S):
    R = x_all.shape[0]
    L = R - LC
    n = R // Q
    T = 256
    nt, ntl = R // T, L // T
    tmR, tmL = _row_tile(R), _row_tile(L)
    tkR = 256 if R % 512 else 512
    tkL = 512 if L % 512 == 0 else 256
    row = lambda v: v.reshape(1, -1)
    mx = [row(mod_x[k]) for k in range(6)]
    mc = [row(mod_c[k]) for k in range(6)]
    sel = lambda i: jnp.minimum(i, 1)
    sc1 = jnp.stack([mc[1], mx[1]])
    sh1 = jnp.stack([mc[0], mx[0]])
    ln0 = [(row(S['ln0_g']), None), (row(S['ln0_b']), None), (sc1, sel), (sh1, sel)]

    x0, xm = stage_fwd("norm0_fwd", f_norm0, T, nt, [(x_all, D, 0, 0)], ln0, [('new', R, D, 0), ('new', R, D, 0, BF16)])
    w_in = X.w_in()
    proj = matmul("proj_fwd", xm, w_in, 'nn', tmR, PW // 2, 1024)
    conv_w8 = jnp.pad(S['conv_w'], ((0, 3), (0, 0)))
    conv_b = row(S['conv_b'])
    xbc = conv_fwd("conv_fwd", proj, conv_w8, conv_b, R)
    dt_bias = jnp.pad(S['dt_bias'].reshape(1, 32), ((0, 0), (0, 96)))
    alog = jnp.pad(S['a_log'].reshape(1, 32), ((0, 0), (0, 96)))
    x_dt = [(proj, 512, C_DT // 512, 0)]
    dt_f, dt_b = stage_fwd("dt_fwd", f_dt, T, nt, x_dt, [(dt_bias, None)], [('new', R, 128, 0), ('new', R, 128, 0)])
    y_f, hs_f = ssd_fwd("ssd_fwd_f", xbc, dt_f, alog, n, False, 0)
    y_b, hs_b = ssd_fwd("ssd_fwd_b", xbc, dt_b, alog, n, True, 1)
    W = X.rest()
    dcol = jnp.repeat(S['d_skip'][0] + S['d_skip'][1], 64).reshape(1, D)
    x_gn = [(y_f, D, 0, 1), (y_b, D, 0, 1), (xbc, D, 0, 1), (proj, D, C_Z // D, 1)]
    p_gn = [(dcol, None), (row(S['ssd_norm_g']), None)]
    (yn,) = stage_fwd("gnorm_fwd", f_gated_norm, T, ntl, x_gn, p_gn, [('new', L, D, 0, BF16)])
    x_gm = [(proj, 2 * D, C_UV // (2 * D), LC // Q)]
    p_gm = ([(row(S['gm_norm_g']), None), (row(S['gm_norm_b']), None)]
            + [(S['w_spatial'][g], None) for g in range(8)] + [(S['b_spatial'][g].reshape(Q, 1), None) for g in range(8)])
    (y_gm,) = stage_fwd("gmlp_fwd", f_gmlp, Q, L // Q, x_gm, p_gm, [('new', L, D, 0, BF16)])
    p_ssd = matmul("pssd_fwd", yn, W['w_ssd_proj'], 'nn', tmL, 1024, 1024)
    p_g = matmul("pgm_fwd", y_gm, W['w_gm_proj'], 'nn', tmL, 1024, 1024)
    x_mg = [(p_ssd, D, 0, 0), (p_g, D, 0, 0), (proj, 2 * D, C_GATE // (2 * D), 1)]
    p_mg = [(row(S['b_gate']), None)]
    (merged,) = stage_fwd("merge_fwd", f_merge, T, ntl, x_mg, p_mg, [('new', L, D, 0, BF16)])
    out = matmul("out_fwd", merged, W['w_out'], 'nn', tmL, 1024, 1024)
    x_r1 = [(x0, D, 0, 1), (out, D, 0, 0)]
    p_r1 = [(mx[2], None), (row(S['ln1_g']), None), (row(S['ln1_b']), None), (mx[4], None), (mx[3], None)]
    x1, hm = stage_fwd("res1_fwd", f_res1, T, ntl, x_r1, p_r1, [('new', L, D, 0), ('new', L, D, 0, BF16)])
    a1, a3, act = ffn_in_fwd("ffn_in_fwd", hm, W['w_ff1'], W['w_ff3'], T)
    ff = ff_out_fwd("ff2_fwd", act, W['w_ff2'], tmL)
    x_r2 = [(x1, D, 0, 0), (ff, D, 0, 0), (tgt, D, 0, 0)]
    p_r2 = [(mx[5], None), (row(S['ln2_g']), None), (row(S['ln2_b']), None)]

    dx1_a, dff, dg2, dl2g, dl2b, loss = stage_bwd(
        "res2_bwd", f_res2_loss, T, ntl, x_r2, p_r2, [1.0],
        [('new', L, D, 0), ('new', L, D, 0, BF16), None], [True, True, True], primal=[(0, (1, 1))])
    da1, da3 = ffn_out_bwd_x("ffn_out_bwd_x", dff, W['w_ff2'], a1, a3, T)
    gw_ff2 = ff_out_bwd_w("ff2_bwd_w", act, dff, tkL)
    dhm = ff_in_bwd_x("ff1_bwd_x", da1, W['w_ff1'], tmL)
    dhm = ff_in_bwd_x("ff3_bwd_x", da3, W['w_ff3'], tmL, add=dhm)
    gw_ff1 = ff_in_bwd_w("ff1_bwd_w", hm, da1, tkL)
    gw_ff3 = ff_in_bwd_w("ff3_bwd_w", hm, da3, tkL)
    X.grads('ffn', {'w_ff2': gw_ff2, 'w_ff1': gw_ff1, 'w_ff3': gw_ff3})
    dx0_a, dout, dg1, dl1g, dl1b, dsc2, dsh2 = stage_bwd(
        "res1_bwd", f_res1, T, ntl, x_r1, p_r1, [(dx1_a, D, 0, 0), (dhm, D, 0, 0)],
        [('new', L, D, 0), ('new', L, D, 0, BF16)], [True] * 5)
    dmerged = matmul("out_bwd_x", dout, W['w_out'], 'nt', tmL, 1024, 1024)
    gw_out = matmul("out_bwd_w", merged, dout, 'tn', 1024, 1024, tkL, BF16)
    lt, lq = -(LC // T), -(LC // Q)
    x_mg_b = [(p_ssd, D, 0, lt), (p_g, D, 0, lt), (proj, 2 * D, C_GATE // (2 * D), 0)]
    dp_ssd, dp_g, dproj, dbg = stage_bwd(
        "merge_bwd", f_merge, T, nt, x_mg_b, p_mg, [(dmerged, D, 0, lt)],
        [('new', L, D, lt, BF16), ('new', L, D, lt, BF16), ('part', R, PW, 2 * D, C_GATE // (2 * D), 0, BF16)], [True])
    dyn = matmul("pssd_bwd_x", dp_ssd, W['w_ssd_proj'], 'nt', tmL, 1024, 1024)
    gw_ssd = matmul("pssd_bwd_w", yn, dp_ssd, 'tn', 1024, 1024, tkL, BF16)
    dy_gm = matmul("pgm_bwd_x", dp_g, W['w_gm_proj'], 'nt', tmL, 1024, 1024)
    gw_gm = matmul("pgm_bwd_w", y_gm, dp_g, 'tn', 1024, 1024, tkL, BF16)
    X.grads('proj', {'w_out': gw_out, 'w_ssd_proj': gw_ssd, 'w_gm_proj': gw_gm})
    r_gm = stage_bwd("gmlp_bwd", f_gmlp, Q, n, [(proj, 2 * D, C_UV // (2 * D), 0)], p_gm, [(dy_gm, D, 0, lq)],
                     [('alias', dproj, 2 * D, C_UV // (2 * D), 0)], [True] * 18)
    dproj, dgmg, dgmb, dws, dbs = r_gm[0], r_gm[1], r_gm[2], r_gm[3:11], r_gm[11:19]
    x_gn_b = [(y_f, D, 0, 0), (y_b, D, 0, 0), (xbc, D, 0, 0), (proj, D, C_Z // D, 0)]
    dy, dskipx, dproj, ddcol, dng = stage_bwd(
        "gnorm_bwd", f_gated_norm, T, nt, x_gn_b, p_gn, [(dyn, D, 0, lt)],
        [('new', L, D, lt), None, ('new', L, D, lt), ('alias', dproj, D, C_Z // D, 0)], [True, True])
    dxbc_f, ddt_f, dal_f = ssd_bwd("ssd_bwd_f", xbc, dt_f, alog, hs_f, dy, n, False, 0)
    dxbc_b, ddt_b, dal_b = ssd_bwd("ssd_bwd_b", xbc, dt_b, alog, hs_b, dy, n, True, 1)
    dproj, ddtb = stage_bwd("dt_bwd", f_dt, T, nt, x_dt, [(dt_bias, None)],
                            [(ddt_f, 128, 0, 0), (ddt_b, 128, 0, 0)],
                            [('alias', dproj, 512, C_DT // 512, 0)], [True])
    dproj, dcw8, dcb = conv_bwd("conv_bwd", proj, conv_w8, conv_b, dxbc_f, dxbc_b, dskipx, dproj, R)
    gw_in = matmul("proj_bwd_w", xm, dproj, 'tn', 1024, PW // 2, tkR, BF16)
    X.grads('in', {'w_in': gw_in})
    dxm = matmul("proj_bwd_x", dproj, w_in, 'nt', R // 2 if R % 16 == 0 else R, 1024, 1024)
    grad_x, dl0g, dl0b, dsc1, dsh1 = stage_bwd(
        "norm0_bwd", f_norm0, T, nt, [(x_all, D, 0, 0)], ln0, [(dx0_a, D, 0, -1), (dxm, D, 0, 0)],
        [('new', L, D, -1)], [True] * 4)

    zero = jnp.zeros((D,), F32)
    flat = lambda v: v.reshape(-1)
    small = {
        'loss': flat(loss), 'ln0_g': flat(dl0g), 'ln0_b': flat(dl0b),
        'dmod_x': jnp.concatenate([flat(dsh1[1]), flat(dsc1[1]), flat(dg1), flat(dsh2), flat(dsc2), flat(dg2)]),
        'dmod_c': jnp.concatenate([flat(dsh1[0]), flat(dsc1[0]), zero, zero, zero, zero]),
        'conv_w': flat(dcw8[:5]), 'conv_b': flat(dcb), 'dt_bias': flat(ddtb[:, :32]),
        'a_log': flat((dal_f + dal_b)[:, :32]),
        'd_skip': flat(jnp.tile(ddcol.reshape(1, NH, 64).sum(-1), (2, 1))),
        'ssd_norm_g': flat(dng), 'gm_norm_g': flat(dgmg), 'gm_norm_b': flat(dgmb),
        'w_spatial': flat(jnp.stack(dws)), 'b_spatial': flat(jnp.stack(dbs)), 'b_gate': flat(dbg),
        'ln1_g': flat(dl1g), 'ln1_b': flat(dl1b), 'ln2_g': flat(dl2g), 'ln2_b': flat(dl2b),
    }
    return grad_x, small


def _place():
    return lax.axis_index("x"), lax.axis_index("y"), lax.axis_index("c")


def allgather8(name, blk, hbm):
    space = pl.ANY if hbm else pltpu.VMEM

    def body(x_ref, out_ref, send_sems, recv_sems, local_sem):
        x, y, c = _place()
        me, sibling = (x, y, c), (x, y, 1 - c)
        chips = [(1 - x, y), (x, 1 - y), (1 - x, 1 - y)]

        def slot(px, py, pc):
            return out_ref.at[4 * px + 2 * py + pc]

        def copy(k, block, to, src=None):
            return pltpu.make_async_remote_copy(
                src_ref=slot(*block) if src is None else src, dst_ref=slot(*block),
                send_sem=send_sems.at[k], recv_sem=recv_sems.at[k], device_id=to, device_id_type=MESH)

        mine = pltpu.make_async_copy(x_ref, slot(*me), local_sem)
        mine.start()
        first = [copy(0, me, sibling, src=x_ref)]
        first += [copy(1 + j, me, (*chip, c), src=x_ref) for j, chip in enumerate(chips)]
        for cp in first:
            cp.start()
        passed = [copy(4 + j, (*chip, c), sibling) for j, chip in enumerate(chips)]
        for j, chip in enumerate(chips):
            copy(1 + j, (*chip, c), me).wait_recv()
            passed[j].start()
        copy(0, sibling, me).wait_recv()
        for j, chip in enumerate(chips):
            copy(4 + j, (*chip, 1 - c), me).wait_recv()
        for cp in first + passed:
            cp.wait_send()
        mine.wait()

    return pl.pallas_call(
        body, name=name, out_shape=jax.ShapeDtypeStruct((8,) + blk.shape, blk.dtype),
        in_specs=[pl.BlockSpec(memory_space=space)], out_specs=pl.BlockSpec(memory_space=space),
        scratch_shapes=[pltpu.SemaphoreType.DMA((7,)), pltpu.SemaphoreType.DMA((7,)), pltpu.SemaphoreType.DMA],
        compiler_params=pltpu.CompilerParams(vmem_limit_bytes=VMEM_LIMIT_V7X))(blk)


def _peers(place):
    x, y, c = place
    return [((1 - x) if k & 4 else x, (1 - y) if k & 2 else y, (1 - c) if k & 1 else c) for k in range(1, 8)]


def _slot(p):
    return 4 * p[0] + 2 * p[1] + p[2]


def plan_gather(place, srcs, lands):
    remote = [(s, l.at[_slot(place)], to) for s, l in zip(srcs, lands) for to in _peers(place)]
    return remote, [(s, l.at[_slot(place)]) for s, l in zip(srcs, lands)]


def plan_to_owner(place, srcs, lands):
    remote = [(s.at[2 * to[0] + to[1], to[2]], l.at[_slot(place)], to) for s, l in zip(srcs, lands) for to in _peers(place)]
    x, y, c = place
    return remote, [(s.at[2 * x + y, c], l.at[_slot(place)]) for s, l in zip(srcs, lands)]


def sequencer_exchange(name, collective_id, srcs, land_shapes, plan):
    n = len(srcs)
    src_refs = [jax.new_ref(a, memory_space=pltpu.MemorySpace.HBM) for a in srcs]
    land_refs = [jax.empty_ref(s, memory_space=pltpu.MemorySpace.HBM) for s in land_shapes]

    @pl.kernel(mesh=plsc.ScalarSubcoreMesh(axis_name="sequencer", num_cores=1), name=name,
               scratch_types=(pltpu.SemaphoreType.DMA((7 * n,)), pltpu.SemaphoreType.DMA((7 * n,)),
                              pltpu.SemaphoreType.DMA((n,))),
               compiler_params=pltpu.CompilerParams(collective_id=collective_id))
    def launch(send_sems, recv_sems, local_sems):
        place = _place()
        barrier = pltpu.get_barrier_semaphore()
        for to in _peers(place):
            pl.semaphore_signal(barrier, inc=1, device_id=to, device_id_type=MESH)
        pl.semaphore_wait(barrier, 7)
        remote, local = plan(place, src_refs, land_refs)
        mine = [pltpu.make_async_copy(s, d, local_sems.at[a]) for a, (s, d) in enumerate(local)]
        for cp in mine:
            cp.start()
        cps = [pltpu.make_async_remote_copy(src_ref=s, dst_ref=d, send_sem=send_sems.at[k], recv_sem=recv_sems.at[k],
                                            device_id=to, device_id_type=MESH) for k, (s, d, to) in enumerate(remote)]
        for cp in cps:
            cp.start()
        for cp in mine:
            cp.wait()
        for cp in cps:
            cp.wait()

    launch()
    return land_refs


def sibling_pair(name, hs):
    n = len(hs)

    def body(*refs):
        ins, outs = refs[:n], refs[n:2 * n]
        send_sems, recv_sems = refs[2 * n:]
        x, y, c = _place()
        cps = [pltpu.make_async_remote_copy(src_ref=outs[a].at[c], dst_ref=outs[a].at[c], send_sem=send_sems.at[a],
                                            recv_sem=recv_sems.at[a], device_id=(x, y, 1 - c), device_id_type=MESH)
               for a in range(n)]
        for cp in cps:
            cp.start()
        for a in range(n):
            pltpu.make_async_remote_copy(src_ref=outs[a].at[1 - c], dst_ref=outs[a].at[1 - c], send_sem=send_sems.at[a],
                                         recv_sem=recv_sems.at[a], device_id=(x, y, 1 - c),
                                         device_id_type=MESH).wait_recv()
        for cp in cps:
            cp.wait_send()

    any_spec = pl.BlockSpec(memory_space=pl.ANY)
    return pl.pallas_call(
        body, name=name, out_shape=[jax.ShapeDtypeStruct(h.shape, h.dtype) for h in hs],
        in_specs=[any_spec] * n, out_specs=[any_spec] * n, input_output_aliases={a: a for a in range(n)},
        scratch_shapes=[pltpu.SemaphoreType.DMA((n,)), pltpu.SemaphoreType.DMA((n,))])(*hs)


def owner_sum(name, land):
    _, r, w = land.shape
    T = r // 2

    def body(_, l_ref, o_ref):
        acc = l_ref[0].astype(F32)
        for j in range(1, 8):
            acc = acc + l_ref[j].astype(F32)
        o_ref[...] = acc

    grid_spec = pltpu.PrefetchScalarGridSpec(
        num_scalar_prefetch=1, grid=(2,),
        in_specs=[pl.BlockSpec((8, T, w), lambda i, at: (0, i, 0))],
        out_specs=pl.BlockSpec((None, T, w), lambda i, at: (at[0], i, 0)))
    at = jnp.stack([lax.axis_index("c")]).astype(jnp.int32)
    return pl.pallas_call(body, name=name, grid_spec=grid_spec, out_shape=jax.ShapeDtypeStruct((2, r, w), F32),
                          compiler_params=_cp("arbitrary"))(at, land)


W_IN_RUNS = ((0, 2, 1296, 376), (376, 3, 0, 1672), (2048, 1, 920, 752), (2800, 2, 0, 1296), (4096, 0, 0, 1024),
             (5120, 0, 1024, 648), (5768, 1, 0, 920))


def w_in_to_padded(name, g4):
    T = 128

    def body(g_ref, o_ref):
        o_ref[:, D_PROJ:PW] = jnp.zeros((T, PW - D_PROJ), o_ref.dtype)
        for (a, s, j0, w) in W_IN_RUNS:
            o_ref[:, a:a + w] = g_ref[s, :, j0:j0 + w]

    return pl.pallas_call(body, name=name, grid=(D // T,), in_specs=[pl.BlockSpec((4, T, 1672), lambda i: (0, i, 0))],
                          out_specs=pl.BlockSpec((T, PW), lambda i: (i, 0)),
                          out_shape=jax.ShapeDtypeStruct((D, PW), g4.dtype), compiler_params=_cp("arbitrary"))(g4)


def w_in_from_padded(name, gp):
    T = 128

    def body(g_ref, o_ref):
        for (a, s, j0, w) in W_IN_RUNS:
            o_ref[s, :, j0:j0 + w] = g_ref[:, a:a + w]

    return pl.pallas_call(body, name=name, grid=(D // T,), in_specs=[pl.BlockSpec((T, PW), lambda i: (i, 0))],
                          out_specs=pl.BlockSpec((4, T, 1672), lambda i: (0, i, 0)),
                          out_shape=jax.ShapeDtypeStruct((4, D, 1672), gp.dtype), compiler_params=_cp("arbitrary"))(gp)


def sum_devices(name, g):
    def body(g_ref, o_ref):
        acc = g_ref[0]
        for k in range(1, 8):
            acc = acc + g_ref[k]
        o_ref[...] = acc

    return pl.pallas_call(body, name=name, out_shape=jax.ShapeDtypeStruct(g.shape[1:], F32),
                          compiler_params=pltpu.CompilerParams(vmem_limit_bytes=VMEM_LIMIT_V7X))(g)


def adamw(name, w, g, m, v, T):
    r, wd = w.shape
    c1 = 1.0 - ADAM_B1 ** ADAM_STEP
    c2 = 1.0 - ADAM_B2 ** ADAM_STEP

    def body(w_ref, g_ref, m_ref, v_ref, d_ref, mo_ref, vo_ref):
        gv = g_ref[...]
        mn = ADAM_B1 * m_ref[...] + (1.0 - ADAM_B1) * gv
        vn = ADAM_B2 * v_ref[...] + (1.0 - ADAM_B2) * (gv * gv)
        d_ref[...] = -ADAM_LR * ((mn / c1) / (jnp.sqrt(vn / c2) + ADAM_EPS) + ADAM_WD * w_ref[...])
        mo_ref[...] = mn
        vo_ref[...] = vn

    spec = pl.BlockSpec((T, wd), lambda i: (i, 0))
    return pl.pallas_call(body, name=name, grid=(r // T,), in_specs=[spec] * 4, out_specs=[spec] * 3,
                          out_shape=[jax.ShapeDtypeStruct((r, wd), F32)] * 3, compiler_params=_cp("arbitrary"))(w, g, m, v)


BIG = {'w_in': (1024, 1672), 'w_ssd_proj': (256, 1024), 'w_gm_proj': (256, 1024), 'w_out': (256, 1024),
       'w_ff1': (1024, 704), 'w_ff3': (1024, 704), 'w_ff2': (704, 1024)}


class Flat:
    def __init__(self, segs):
        self.off, o = {}, 0
        for name, size in segs:
            self.off[name] = (o, size)
            o += -(-size // 128) * 128
        self.rows = -(-o // 1024) * 8

    def pack(self, vals):
        parts = []
        for name, (o, size) in self.off.items():
            v = vals[name].reshape(-1).astype(F32)
            parts.append(jnp.pad(v, (0, -(-size // 128) * 128 - size)))
        buf = jnp.concatenate(parts)
        return jnp.pad(buf, (0, self.rows * 128 - buf.shape[0])).reshape(self.rows, 128)

    def get(self, buf, name, shape=None):
        o, size = self.off[name]
        v = buf[o // 128:(o + size + 127) // 128].reshape(-1)[:size]
        return v if shape is None else v.reshape(shape)


PARTIALS = Flat([('loss', 1), ('ln0_g', D), ('ln0_b', D), ('dmod_x', 6 * D), ('dmod_c', 6 * D), ('conv_w', 5 * 1536),
                 ('conv_b', 1536), ('dt_bias', 32), ('a_log', 32), ('d_skip', 32), ('ssd_norm_g', D),
                 ('gm_norm_g', D), ('gm_norm_b', D), ('w_spatial', 8 * Q * Q), ('b_spatial', 8 * Q), ('b_gate', 2 * D),
                 ('ln1_g', D), ('ln1_b', D), ('ln2_g', D), ('ln2_b', D)])

WEIGHTS = ('c_ctx', 'ln0_g', 'ln0_b', 'w_ada', 'b_ada', 'w_in', 'conv_w', 'conv_b', 'dt_bias', 'a_log', 'd_skip',
           'ssd_norm_g', 'gm_norm_g', 'gm_norm_b', 'w_spatial', 'b_spatial', 'b_gate', 'w_ssd_proj', 'w_gm_proj',
           'w_out', 'ln1_g', 'ln1_b', 'w_ff1', 'w_ff3', 'w_ff2', 'ln2_g', 'ln2_b')
BIG_NAMES = tuple(BIG)
SMALL_NAMES = tuple(n for n in WEIGHTS if n not in BIG_NAMES and n != 'w_ada')


def kernel(x, c, ctx, c_ctx, ln0_g, ln0_b, w_ada, b_ada, w_in, conv_w, conv_b, dt_bias, a_log, d_skip, ssd_norm_g, gm_norm_g, gm_norm_b, w_spatial, b_spatial, b_gate, w_ssd_proj, w_gm_proj, w_out, ln1_g, ln1_b, w_ff1, w_ff3, w_ff2, ln2_g, ln2_b, loss_target, m_c_ctx, m_ln0_g, m_ln0_b, m_w_ada, m_b_ada, m_w_in, m_conv_w, m_conv_b, m_dt_bias, m_a_log, m_d_skip, m_ssd_norm_g, m_gm_norm_g, m_gm_norm_b, m_w_spatial, m_b_spatial, m_b_gate, m_w_ssd_proj, m_w_gm_proj, m_w_out, m_ln1_g, m_ln1_b, m_w_ff1, m_w_ff3, m_w_ff2, m_ln2_g, m_ln2_b, v_c_ctx, v_ln0_g, v_ln0_b, v_w_ada, v_b_ada, v_w_in, v_conv_w, v_conv_b, v_dt_bias, v_a_log, v_d_skip, v_ssd_norm_g, v_gm_norm_g, v_gm_norm_b, v_w_spatial, v_b_spatial, v_b_gate, v_w_ssd_proj, v_w_gm_proj, v_w_out, v_ln1_g, v_ln1_b, v_w_ff1, v_w_ff3, v_w_ff2, v_ln2_g, v_ln2_b):
    wts = dict(c_ctx=c_ctx, ln0_g=ln0_g, ln0_b=ln0_b, w_ada=w_ada, b_ada=b_ada, w_in=w_in, conv_w=conv_w, conv_b=conv_b,
               dt_bias=dt_bias, a_log=a_log, d_skip=d_skip, ssd_norm_g=ssd_norm_g, gm_norm_g=gm_norm_g,
               gm_norm_b=gm_norm_b, w_spatial=w_spatial, b_spatial=b_spatial, b_gate=b_gate, w_ssd_proj=w_ssd_proj,
               w_gm_proj=w_gm_proj, w_out=w_out, ln1_g=ln1_g, ln1_b=ln1_b, w_ff1=w_ff1, w_ff3=w_ff3, w_ff2=w_ff2,
               ln2_g=ln2_g, ln2_b=ln2_b)
    ms = dict(zip(WEIGHTS, (m_c_ctx, m_ln0_g, m_ln0_b, m_w_ada, m_b_ada, m_w_in, m_conv_w, m_conv_b, m_dt_bias, m_a_log,
                            m_d_skip, m_ssd_norm_g, m_gm_norm_g, m_gm_norm_b, m_w_spatial, m_b_spatial, m_b_gate,
                            m_w_ssd_proj, m_w_gm_proj, m_w_out, m_ln1_g, m_ln1_b, m_w_ff1, m_w_ff3, m_w_ff2, m_ln2_g,
                            m_ln2_b)))
    vs = dict(zip(WEIGHTS, (v_c_ctx, v_ln0_g, v_ln0_b, v_w_ada, v_b_ada, v_w_in, v_conv_w, v_conv_b, v_dt_bias, v_a_log,
                            v_d_skip, v_ssd_norm_g, v_gm_norm_g, v_gm_norm_b, v_w_spatial, v_b_spatial, v_b_gate,
                            v_w_ssd_proj, v_w_gm_proj, v_w_out, v_ln1_g, v_ln1_b, v_w_ff1, v_w_ff3, v_w_ff2, v_ln2_g,
                            v_ln2_b)))
    px, py, pc = _place()
    shard = 2 * px + py
    dev = 2 * shard + pc
    take = lambda a, i, axis=0: lax.dynamic_index_in_dim(a, i, axis, keepdims=False)

    half = lambda n: take(wts[n][0].reshape(2, BIG[n][0] // 2, BIG[n][1]), pc).astype(BF16)

    pre = jnp.concatenate([c, jnp.pad(conv_w[0], ((0, 0), (0, D - 384))), jnp.zeros((2, D), F32)], axis=0)
    pre = allgather8("gather_cond", pre, False)
    conv_w_full = pre[0::2, 1:6, :384].transpose(1, 0, 2).reshape(5, 1536)
    a16 = jnp.concatenate([_silu(pre[:, 0, :]), _silu(c_ctx)[None], jnp.zeros((7, D), F32)], axis=0)
    mod = matmul("ada_fwd", a16, w_ada[0], 'nn', 16, 512, 1024)
    mod = mod + lax.dynamic_slice_in_dim(b_ada[0], shard * 1536, 1536)[None]
    mod = allgather8("gather_mod", mod, False)
    mod = jnp.concatenate([mod[0], mod[2], mod[4], mod[6]], axis=1)
    mod_x = take(mod, dev).reshape(6, D)
    mod_c = mod[8].reshape(6, D)

    def full(n, blocks):
        r, w = BIG[n]
        return blocks.reshape(4, r, w) if w != D else blocks.reshape(4 * r, w)

    class Exchanges:
        rest_names = BIG_NAMES[1:]

        def __init__(self):
            self.pending = []

        def w_in(self):
            blocks = allgather8("gather_w_in", half('w_in'), True)
            w = w_in_to_padded("w_in_layout", full('w_in', blocks))
            halves = [half(n) for n in self.rest_names]
            halves[0], _ = lax.optimization_barrier((halves[0], blocks))
            lands = [jax.ShapeDtypeStruct((8,) + h.shape, BF16) for h in halves]
            self.rest_refs = sequencer_exchange("gather_rest", 1, halves, lands, plan_gather)
            return w

        def rest(self):
            return {n: full(n, r[...]) for n, r in zip(self.rest_names, self.rest_refs)}

        def grads(self, group, gs):
            if group == 'in':
                gs = {'w_in': w_in_from_padded("w_in_grad_layout", gs['w_in'])}
            blocks = [g.reshape(4, 2, BIG[n][0] // 2, BIG[n][1]) for n, g in gs.items()]
            lands = [jax.ShapeDtypeStruct((8,) + b.shape[2:], BF16) for b in blocks]
            refs = sequencer_exchange("grads_" + group, 2 + len(self.pending), blocks, lands, plan_to_owner)
            self.pending.append((tuple(gs), refs))

        def finish(self):
            names, halves = [], []
            for ns, refs in self.pending:
                names += ns
                halves += [owner_sum("grads_sum_" + n, r[...]) for n, r in zip(ns, refs)]
            return {n: h.reshape(BIG[n]) for n, h in zip(names, sibling_pair("grads_halves", halves))}

    S = dict(ln0_g=ln0_g, ln0_b=ln0_b, conv_w=conv_w_full, conv_b=conv_b[0], dt_bias=dt_bias[0], a_log=a_log[0],
             d_skip=d_skip[0], ssd_norm_g=ssd_norm_g[0], gm_norm_g=gm_norm_g[0], gm_norm_b=gm_norm_b[0],
             w_spatial=w_spatial[0], b_spatial=b_spatial[0], b_gate=b_gate[0], ln1_g=ln1_g[0], ln1_b=ln1_b[0],
             ln2_g=ln2_g[0], ln2_b=ln2_b[0])
    x_all = jnp.concatenate([ctx[0], x[0]], axis=0)
    exchanges = Exchanges()
    grad_x, gsmall = core(x_all, loss_target[0], mod_x, mod_c, exchanges, S)

    parts = allgather8("gather_partials", PARTIALS.pack(gsmall), False)
    tot = sum_devices("partials_sum", parts)
    g_shards = exchanges.finish()
    g = {n: PARTIALS.get(tot, n) for n in ('ln0_g', 'ln0_b', 'conv_b', 'dt_bias', 'a_log', 'd_skip', 'ssd_norm_g',
                                           'gm_norm_g', 'gm_norm_b', 'w_spatial', 'b_spatial', 'b_gate', 'ln1_g',
                                           'ln1_b', 'ln2_g', 'ln2_b')}
    loss = PARTIALS.get(tot, 'loss', ())
    dmod_c = PARTIALS.get(tot, 'dmod_c')
    g['b_ada'] = PARTIALS.get(tot, 'dmod_x') + dmod_c
    g['conv_w'] = lax.dynamic_slice_in_dim(PARTIALS.get(tot, 'conv_w', (5, 1536)), shard * 384, 384, axis=1)
    o, size = PARTIALS.off['dmod_x']
    dmod_rows = parts[:, o // 128:(o + size) // 128].reshape(8, size)
    dm = jnp.concatenate([dmod_rows, dmod_c[None], jnp.zeros((7, 6 * D), F32)], axis=0)
    dm = lax.dynamic_slice_in_dim(dm, shard * 1536, 1536, axis=1)
    g['w_ada'] = matmul("ada_bwd_w", a16, dm, 'tn', 1024, 512, 16)
    dm_c = jnp.concatenate([dm[8:9], jnp.zeros((15, 1536), F32)], axis=0)
    dc = matmul("ada_bwd_c", dm_c, w_ada[0], 'nt', 16, 1024, 512)
    dc = allgather8("gather_dcctx", dc, False)[:, 0, :]
    dc = ((dc[0] + dc[2]) + dc[4]) + dc[6]
    sg = jax.nn.sigmoid(c_ctx)
    g['c_ctx'] = dc * (sg * (1.0 + c_ctx * (1.0 - sg)))
    for n in BIG_NAMES:
        g[n] = g_shards[n]

    delta, new_m, new_v = {}, {}, {}
    for n in BIG_NAMES + ('w_ada',):
        r, w = wts[n].shape[1:]
        if w % 128:
            T = max(t for t in range(8, 257, 8) if w % t == 0)
            d_, m_, v_ = adamw("adamw_" + n, wts[n][0].T, g[n].T, ms[n][0].T, vs[n][0].T, T)
            delta[n], new_m[n], new_v[n] = d_.T, m_.T, v_.T
        else:
            T = 352 if n == 'w_ff2' else 256
            delta[n], new_m[n], new_v[n] = adamw("adamw_" + n, wts[n][0], g[n], ms[n][0], vs[n][0], T)
    lay = Flat([(n, wts[n].size) for n in SMALL_NAMES])
    d_, m_, v_ = adamw("adamw_small", lay.pack(wts), lay.pack(g), lay.pack(ms), lay.pack(vs), lay.rows)
    for n in SMALL_NAMES:
        delta[n], new_m[n], new_v[n] = (lay.get(b, n) for b in (d_, m_, v_))

    shp = lambda d: [d[n].reshape(wts[n].shape) for n in WEIGHTS]
    return (loss, grad_x[None], *shp(g), *shp(delta), *shp(new_m), *shp(new_v))
```

```python
import functools

import jax
import jax.numpy as jnp
from jax import lax
from jax.experimental import pallas as pl
from jax.experimental.pallas import tpu as pltpu
from jax.experimental.pallas import tpu_sc as plsc

F32 = jnp.float32
BF16 = jnp.bfloat16
MESH = pl.DeviceIdType.MESH

VMEM_LIMIT_V7X = 56 * 1024 * 1024

D = 1024
LC = 256
Q = 128
NH = 16
D_FF = 2816
LN_EPS = 1e-5
ALPHA = 2.0 ** 0.25

PW = 7168
C_GATE, C_UV, C_Z, C_XBC, C_DT = 0, 2048, 4096, 5120, 6656
D_PROJ = 6688

ADAM_LR, ADAM_B1, ADAM_B2, ADAM_EPS, ADAM_WD, ADAM_STEP = 0.001, 0.9, 0.999, 1e-08, 0.01, 10


def _cp(*sem):
    return pltpu.CompilerParams(dimension_semantics=sem, vmem_limit_bytes=VMEM_LIMIT_V7X)


def _dot(a, b, ca, cb):
    return lax.dot_general(a.astype(BF16), b.astype(BF16), (((ca,), (cb,)), ((), ())),
                           preferred_element_type=F32)


@jax.custom_vjp
def mm(a, b):
    return _dot(a, b, 1, 0)


mm.defvjp(lambda a, b: (_dot(a, b, 1, 0), (a, b)),
          lambda r, g: (_dot(g, r[1], 1, 1), _dot(r[0], g, 0, 0)))


@jax.custom_vjp
def mm_nt(a, b):
    return _dot(a, b, 1, 1)


mm_nt.defvjp(lambda a, b: (_dot(a, b, 1, 1), (a, b)),
             lambda r, g: (_dot(g, r[1], 1, 0), _dot(g, r[0], 0, 0)))


@jax.custom_vjp
def mm_tn(a, b):
    return _dot(a, b, 0, 0)


mm_tn.defvjp(lambda a, b: (_dot(a, b, 0, 0), (a, b)),
             lambda r, g: (_dot(r[1], g, 1, 1), _dot(r[0], g, 1, 0)))


def _dot32(a, b):
    return lax.dot_general(a, b, (((1,), (0,)), ((), ())), precision=lax.Precision.HIGHEST,
                           preferred_element_type=F32)


def _cumsum_fn(rev):
    def tri(transpose):
        r = lax.broadcasted_iota(jnp.int32, (Q, Q), 0)
        c = lax.broadcasted_iota(jnp.int32, (Q, Q), 1)
        keep = (r >= c) if (rev == transpose) else (r <= c)
        return jnp.where(keep, 1.0, 0.0).astype(F32)

    @jax.custom_vjp
    def cums(a):
        return _dot32(tri(False), a)

    cums.defvjp(lambda a: (_dot32(tri(False), a), None), lambda _, g: (_dot32(tri(True), g),))
    return cums


def _cols(v, k):
    w = v.shape[1] // k
    return tuple(v[:, w * i:w * (i + 1)] for i in range(k))


def _splitter(k):
    @jax.custom_vjp
    def split(v):
        return _cols(v, k)

    @jax.custom_vjp
    def concat(ps):
        return jnp.concatenate(ps, axis=1)

    split.defvjp(lambda v: (_cols(v, k), None), lambda _, g: (jnp.concatenate(g, axis=1),))
    concat.defvjp(lambda ps: (jnp.concatenate(ps, axis=1), None), lambda _, g: (_cols(g, k),))
    return split, concat


split2, _ = _splitter(2)
split4, _ = _splitter(4)
split8, concat8 = _splitter(8)


def _ln(x, g, b):
    mu = jnp.mean(x, axis=-1, keepdims=True)
    xc = x - mu
    var = jnp.mean(xc * xc, axis=-1, keepdims=True)
    return xc * lax.rsqrt(var + LN_EPS) * g + b


def _silu(x):
    return x * jax.nn.sigmoid(x)


def _gelu(x):
    return 0.5 * x * (1.0 + jnp.tanh(0.7978845608028654 * (x + 0.044715 * (x * x * x))))


def _xspec(T, w, col, roff):
    return pl.BlockSpec((T, w), lambda i, col=col, roff=roff: (jnp.maximum(i + roff, 0), col))


def _pspec(p, sel):
    if sel is None:
        return pl.BlockSpec(p.shape, lambda i, n=p.ndim: (0,) * n)
    return pl.BlockSpec((1,) + p.shape[1:], lambda i, n=p.ndim: (sel(i),) + (0,) * (n - 1))


def _out_plumbing(outs, T, args, in_specs):
    shapes, specs, aliases = [], [], {}
    for k, o in enumerate(outs):
        if o[0] == 'new':
            _, rows, w, roff = o[:4]
            shapes.append(jax.ShapeDtypeStruct((rows, w), o[4] if len(o) > 4 else F32))
            specs.append(_xspec(T, w, 0, roff))
        elif o[0] == 'acc':
            shapes.append(jax.ShapeDtypeStruct(o[1], F32))
            specs.append(pl.BlockSpec(o[1], lambda i, n=len(o[1]): (0,) * n))
        elif o[0] == 'part':
            _, rows, wtot, w, col, roff, dtype = o
            shapes.append(jax.ShapeDtypeStruct((rows, wtot), dtype))
            specs.append(_xspec(T, w, col, roff))
        else:
            _, arr, w, col, roff = o
            aliases[len(args)] = k
            args.append(arr)
            in_specs.append(pl.BlockSpec(memory_space=pl.ANY))
            shapes.append(jax.ShapeDtypeStruct(arr.shape, arr.dtype))
            specs.append(_xspec(T, w, col, roff))
    return shapes, specs, aliases


def stage_fwd(name, f, T, n, xs, ps, outs):
    nx, npar = len(xs), len(ps)
    args = [x[0] for x in xs] + [p[0] for p in ps]
    in_specs = [_xspec(T, w, col, roff) for (_, w, col, roff) in xs] + [_pspec(p, sel) for (p, sel) in ps]
    n_in = len(args)
    shapes, specs, aliases = _out_plumbing(outs, T, args, in_specs)
    n_all_in = len(args)

    def body(*refs):
        i = pl.program_id(0)
        xv = [r[...] for r in refs[:nx]]
        pv = [r[...] if ps[k][1] is None else r[0] for k, r in enumerate(refs[nx:n_in])]
        res = f(*xv, *pv)
        for k, o_ref in enumerate(refs[n_all_in:]):
            if outs[k][0] == 'acc':
                @pl.when(i == 0)
                def _(o_ref=o_ref, v=res[k]):
                    o_ref[...] = v

                @pl.when(i > 0)
                def _(o_ref=o_ref, v=res[k]):
                    o_ref[...] += v
            else:
                o_ref[...] = res[k].astype(o_ref.dtype)

    return pl.pallas_call(body, name=name, grid=(n,), in_specs=in_specs, out_specs=specs, out_shape=shapes,
                          input_output_aliases=aliases, compiler_params=_cp("arbitrary"))(*args)


def stage_bwd(name, f, T, n, xs, ps, cts, dxs, dps, primal=()):
    nx, npar = len(xs), len(ps)
    args = [x[0] for x in xs] + [p[0] for p in ps]
    in_specs = [_xspec(T, w, col, roff) for (_, w, col, roff) in xs] + [_pspec(p, sel) for (p, sel) in ps]
    ct_arrs = [c for c in cts if isinstance(c, tuple)]
    for (a, w, col, roff) in ct_arrs:
        args.append(a)
        in_specs.append(_xspec(T, w, col, roff))
    n_in = len(args)
    outs, out_of = [], []
    for k, o in enumerate(dxs):
        if o is not None:
            outs.append(o)
            out_of.append(('x', k))
    for k, want in enumerate(dps):
        if want:
            p, sel = ps[k]
            outs.append(('acc', p.shape))
            out_of.append(('p', k))
    for k, shape in primal:
        outs.append(('acc', shape))
        out_of.append(('r', k))
    shapes, specs, aliases = _out_plumbing(outs, T, args, in_specs)
    for j, (kind, k) in enumerate(out_of):
        if kind == 'p' and ps[k][1] is not None:
            p, sel = ps[k]
            specs[j] = pl.BlockSpec((1,) + p.shape[1:], lambda i, n=p.ndim, sel=sel: (sel(i),) + (0,) * (n - 1))
    n_all_in = len(args)

    def body(*refs):
        i = pl.program_id(0)
        xv = [r[...] for r in refs[:nx]]
        pv = [r[...] if ps[k][1] is None else r[0] for k, r in enumerate(refs[nx:nx + npar])]
        res, vjp_fn = jax.vjp(f, *xv, *pv)
        ctv, q = [], nx + npar
        for k, c in enumerate(cts):
            if c is None:
                ctv.append(jnp.zeros_like(res[k]))
            elif isinstance(c, tuple):
                v = refs[q][...]
                if c[3] < 0:
                    v = v * (i + c[3] >= 0).astype(F32)
                ctv.append(v)
                q += 1
            else:
                ctv.append(jnp.full_like(res[k], c))
        grads = vjp_fn(tuple(ctv))
        for j, o_ref in enumerate(refs[n_all_in:]):
            kind, k = out_of[j]
            if kind == 'x':
                o_ref[...] = grads[k].astype(o_ref.dtype)
            else:
                g = res[k] if kind == 'r' else grads[nx + k]
                sel = None if kind == 'r' else ps[k][1]
                if sel is None:
                    first = i == 0
                    tgt = o_ref
                else:
                    first = jnp.logical_or(i == 0, sel(i) != sel(jnp.maximum(i - 1, 0)))
                    tgt = o_ref.at[0]

                @pl.when(first)
                def _(tgt=tgt, g=g):
                    tgt[...] = g

                @pl.when(jnp.logical_not(first))
                def _(tgt=tgt, g=g):
                    tgt[...] += g

    return pl.pallas_call(body, name=name, grid=(n,), in_specs=in_specs, out_specs=specs, out_shape=shapes,
                          input_output_aliases=aliases, compiler_params=_cp("arbitrary"))(*args)


_CONTRACT = {'nn': (1, 0), 'nt': (1, 1), 'tn': (0, 0)}


def matmul(name, a, b, mode, tm, tn, tk, out_dtype=F32, add=None):
    if mode == 'nn':
        (M, K), (_, N) = a.shape, b.shape
    elif mode == 'nt':
        (M, K), (N, _) = a.shape, b.shape
    else:
        (K, M), (_, N) = a.shape, b.shape
    assert M % tm == 0 and N % tn == 0 and K % tk == 0, (name, M, N, K, tm, tn, tk)
    a_spec = (pl.BlockSpec((tk, tm), lambda j, i, k: (k, i)) if mode == 'tn'
              else pl.BlockSpec((tm, tk), lambda j, i, k: (i, k)))
    b_spec = (pl.BlockSpec((tn, tk), lambda j, i, k: (j, k)) if mode == 'nt'
              else pl.BlockSpec((tk, tn), lambda j, i, k: (k, j)))
    o_spec = pl.BlockSpec((tm, tn), lambda j, i, k: (i, j))
    return matmul_call(name, (N // tn, M // tm, K // tk), a, a_spec, b, b_spec, (M, N), o_spec, (tm, tn), mode,
                       out_dtype, add)


def matmul_call(name, grid, a, a_spec, b, b_spec, out_shape, o_spec, tile, mode, out_dtype=F32, add=None):
    tm, tn = tile
    nk = grid[2]
    ca, cb = _CONTRACT[mode]
    args, in_specs = [a, b], [a_spec, b_spec]
    if add is not None:
        args.append(add)
        in_specs.append(o_spec)

    def body(*refs):
        a_ref, b_ref = refs[0], refs[1]
        o_ref, acc = refs[-2], refs[-1]
        k = pl.program_id(2)
        if nk == 1:
            p = _dot(a_ref[...], b_ref[...], ca, cb)
            o_ref[...] = (p + refs[2][...] if add is not None else p).astype(out_dtype)
            return

        @pl.when(k == 0)
        def _():
            acc[...] = refs[2][...] if add is not None else jnp.zeros_like(acc)

        acc[...] += _dot(a_ref[...], b_ref[...], ca, cb)

        @pl.when(k == nk - 1)
        def _():
            o_ref[...] = acc[...].astype(out_dtype)

    return pl.pallas_call(body, name=name, grid=grid, in_specs=in_specs, out_specs=o_spec,
                          out_shape=jax.ShapeDtypeStruct(out_shape, out_dtype),
                          scratch_shapes=[pltpu.VMEM((tm, tn) if nk > 1 else (8, 128), F32)],
                          compiler_params=_cp("arbitrary", "arbitrary", "arbitrary"))(*args)


NS, WS = 4, 704


def _resident(name, M, tm, rows, weight, out_shape, out_block, out_map, step, add=None):
    args = [rows[0], weight] + ([] if add is None else [add])
    in_specs = [pl.BlockSpec(rows[1], rows[2]), pl.BlockSpec(weight.shape, lambda i, n=weight.ndim: (0,) * n)]
    if add is not None:
        in_specs.append(pl.BlockSpec(out_block, out_map))
    return pl.pallas_call(step, name=name, grid=(M // tm,), in_specs=in_specs, out_specs=pl.BlockSpec(out_block, out_map),
                          out_shape=jax.ShapeDtypeStruct(out_shape, F32), compiler_params=_cp("arbitrary"))(*args)


def ffn_in_fwd(name, h, w1, w3, tm):
    M = h.shape[0]

    def step(h_ref, w1_ref, w3_ref, a1_ref, a3_ref, act_ref):
        for s in range(NS):
            a1 = _dot(h_ref[...], w1_ref[s], 1, 0)
            a3 = _dot(h_ref[...], w3_ref[s], 1, 0)
            a1_ref[s] = a1.astype(a1_ref.dtype)
            a3_ref[s] = a3.astype(a3_ref.dtype)
            act_ref[s] = (_silu(a1) * a3).astype(act_ref.dtype)

    wspec = pl.BlockSpec((NS, D, WS), lambda i: (0, 0, 0))
    ospec = pl.BlockSpec((NS, tm, WS), lambda i: (0, i, 0))
    return pl.pallas_call(
        step, name=name, grid=(M // tm,), in_specs=[pl.BlockSpec((tm, D), lambda i: (i, 0)), wspec, wspec],
        out_specs=[ospec, ospec, ospec],
        out_shape=[jax.ShapeDtypeStruct((NS, M, WS), BF16)] * 3, compiler_params=_cp("arbitrary"))(h, w1, w3)


def ffn_out_bwd_x(name, dff, w2, a1, a3, tm):
    M = dff.shape[0]

    def step(d_ref, w_ref, a1_ref, a3_ref, da1_ref, da3_ref):
        for s in range(NS):
            dact = _dot(d_ref[...], w_ref[s * WS:(s + 1) * WS, :], 1, 1)
            a1 = a1_ref[s].astype(F32)
            sig = jax.nn.sigmoid(a1)
            da3_ref[s] = (dact * (a1 * sig)).astype(da3_ref.dtype)
            da1_ref[s] = (dact * a3_ref[s].astype(F32) * (sig * (1.0 + a1 * (1.0 - sig)))).astype(da1_ref.dtype)

    aspec = pl.BlockSpec((NS, tm, WS), lambda i: (0, i, 0))
    return pl.pallas_call(
        step, name=name, grid=(M // tm,),
        in_specs=[pl.BlockSpec((tm, D), lambda i: (i, 0)), pl.BlockSpec(w2.shape, lambda i: (0, 0)), aspec, aspec],
        out_specs=[aspec, aspec],
        out_shape=[jax.ShapeDtypeStruct((NS, M, WS), BF16)] * 2, compiler_params=_cp("arbitrary"))(dff, w2, a1, a3)


def ff_in_bwd_x(name, da3, w3, tm, add=None):
    M = da3.shape[1]

    def step(*refs):
        d_ref, w_ref, o_ref = refs[0], refs[1], refs[-1]
        acc = _dot(d_ref[0], w_ref[0], 1, 1)
        for s in range(1, NS):
            acc = acc + _dot(d_ref[s], w_ref[s], 1, 1)
        o_ref[...] = acc if add is None else acc + refs[2][...]

    return _resident(name, M, tm, (da3, (NS, tm, WS), lambda i: (0, i, 0)), w3, (M, D), (tm, D), lambda i: (i, 0), step, add)


def ff_in_bwd_w(name, h, da3, tk):
    M = h.shape[0]

    def step(h_ref, d_ref, acc):
        for s in range(NS):
            acc[s] += _dot(h_ref[...], d_ref[s], 0, 0)

    return _token_sum(name, M // tk, [pl.BlockSpec((tk, D), lambda k: (k, 0)), pl.BlockSpec((NS, tk, WS), lambda k: (0, k, 0))],
                      (NS, D, WS), (NS, D, WS), step, (h, da3))


def ff_out_fwd(name, act3, w2, tm):
    M = act3.shape[1]

    def step(a_ref, w_ref, o_ref):
        acc = _dot(a_ref[0], w_ref[0:WS, :], 1, 0)
        for s in range(1, NS):
            acc = acc + _dot(a_ref[s], w_ref[s * WS:(s + 1) * WS, :], 1, 0)
        o_ref[...] = acc

    return _resident(name, M, tm, (act3, (NS, tm, WS), lambda i: (0, i, 0)), w2, (M, D), (tm, D), lambda i: (i, 0), step)


def _token_sum(name, nk, in_specs, out_shape, acc_shape, step, args):
    def body(*refs):
        o_ref, acc = refs[-2], refs[-1]
        k = pl.program_id(0)

        @pl.when(k == 0)
        def _():
            acc[...] = jnp.zeros_like(acc)

        step(*refs[:-2], acc)

        @pl.when(k == nk - 1)
        def _():
            o_ref[...] = acc[...].astype(o_ref.dtype)

    return pl.pallas_call(body, name=name, grid=(nk,), in_specs=in_specs,
                          out_specs=pl.BlockSpec(out_shape, lambda k, n=len(out_shape): (0,) * n),
                          out_shape=jax.ShapeDtypeStruct(out_shape, BF16), scratch_shapes=[pltpu.VMEM(acc_shape, F32)],
                          compiler_params=_cp("arbitrary"))(*args)


def ff_out_bwd_w(name, act3, dff, tk):
    M = dff.shape[0]

    def step(a_ref, d_ref, acc):
        for s in range(NS):
            acc[s * WS:(s + 1) * WS, :] += _dot(a_ref[s], d_ref[...], 0, 0)

    return _token_sum(name, M // tk, [pl.BlockSpec((NS, tk, WS), lambda k: (0, k, 0)), pl.BlockSpec((tk, D), lambda k: (k, 0))],
                      (NS * WS, D), (NS * WS, D), step, (act3, dff))


HALO = 8


def _windows(pad_ref, v):
    n = v.shape[0]
    edge = jnp.zeros((HALO, 128), F32)
    pad_ref[0:HALO, :] = edge
    pad_ref[HALO:HALO + n, :] = v
    pad_ref[HALO + n:2 * HALO + n, :] = edge
    return lambda k: pad_ref[HALO + k - 2:HALO + k - 2 + n, :]


def _conv_pre(tap, w_ref, b_ref):
    acc = tap(0) * w_ref[0:1, :] + b_ref[...]
    for k in range(1, 5):
        acc = acc + tap(k) * w_ref[k:k + 1, :]
    return acc


def conv_fwd(name, proj, conv_w, conv_b, R):
    segs = ((0, LC), (LC, R))

    def body(x_ref, w_ref, b_ref, o_ref, xp):
        for (s, e) in segs:
            pre = _conv_pre(_windows(xp, x_ref[s:e, :]), w_ref, b_ref)
            o_ref[s:e, :] = _silu(pre)

    return pl.pallas_call(
        body, name=name, grid=(12,),
        in_specs=[pl.BlockSpec((R, 128), lambda j: (0, C_XBC // 128 + j)),
                  pl.BlockSpec((8, 128), lambda j: (0, j)), pl.BlockSpec((1, 128), lambda j: (0, j))],
        out_specs=pl.BlockSpec((R, 128), lambda j: (0, j)),
        out_shape=jax.ShapeDtypeStruct((R, 1536), F32), scratch_shapes=[pltpu.VMEM((R - LC + 2 * HALO, 128), F32)],
        compiler_params=_cp("arbitrary"))(proj, conv_w, conv_b)


def conv_bwd(name, proj, conv_w, conv_b, d_f, d_b, d_skip, dproj, R):
    segs = ((0, LC), (LC, R))

    def body(x_ref, w_ref, b_ref, df_ref, db_ref, ds_ref, _, dx_ref, dw_ref, dbias_ref, xp, dp):
        j = pl.program_id(0)
        has_skip = (j < 8).astype(F32)
        dw = [jnp.zeros((1, 128), F32) for _ in range(5)]
        dbias = jnp.zeros((1, 128), F32)
        for (s, e) in segs:
            xw = _windows(xp, x_ref[s:e, :])
            pre = _conv_pre(xw, w_ref, b_ref)
            sig = jax.nn.sigmoid(pre)
            dy = df_ref[s:e, :] + db_ref[s:e, :]
            if s == LC:
                dy = dy + ds_ref[...] * has_skip
            dpre = dy * (sig * (1.0 + pre * (1.0 - sig)))
            dtap = _windows(dp, dpre)
            dx = dtap(4) * w_ref[0:1, :]
            for k in range(1, 5):
                dx = dx + dtap(4 - k) * w_ref[k:k + 1, :]
            for k in range(5):
                dw[k] = dw[k] + jnp.sum(dtap(2) * xw(k), axis=0, keepdims=True)
            dbias = dbias + jnp.sum(dpre, axis=0, keepdims=True)
            dx_ref[s:e, :] = dx.astype(dx_ref.dtype)
        dw_ref[...] = jnp.zeros_like(dw_ref)
        for k in range(5):
            dw_ref[k:k + 1, :] = dw[k]
        dbias_ref[...] = dbias

    pad = pltpu.VMEM((R - LC + 2 * HALO, 128), F32)
    return pl.pallas_call(
        body, name=name, grid=(12,),
        in_specs=[pl.BlockSpec((R, 128), lambda j: (0, C_XBC // 128 + j)),
                  pl.BlockSpec((8, 128), lambda j: (0, j)), pl.BlockSpec((1, 128), lambda j: (0, j)),
                  pl.BlockSpec((R, 128), lambda j: (0, j)), pl.BlockSpec((R, 128), lambda j: (0, j)),
                  pl.BlockSpec((R - LC, 128), lambda j: (0, jnp.minimum(j, 7))),
                  pl.BlockSpec(memory_space=pl.ANY)],
        out_specs=[pl.BlockSpec((R, 128), lambda j: (0, C_XBC // 128 + j)),
                   pl.BlockSpec((8, 128), lambda j: (0, j)), pl.BlockSpec((1, 128), lambda j: (0, j))],
        out_shape=[jax.ShapeDtypeStruct(dproj.shape, dproj.dtype), jax.ShapeDtypeStruct((8, 1536), F32),
                   jax.ShapeDtypeStruct((1, 1536), F32)], scratch_shapes=[pad, pad],
        input_output_aliases={6: 0}, compiler_params=_cp("arbitrary"))(proj, conv_w, conv_b, d_f, d_b, d_skip, dproj)


def _ssd_chunk(rev, dirn):
    cums = _cumsum_fn(rev)

    def f(xs, Bs, Cs, dt, alog, Hs):
        lane = lax.broadcasted_iota(jnp.int32, (1, 128), 1)
        sub = lax.broadcasted_iota(jnp.int32, (Q, 1), 0)
        r = lax.broadcasted_iota(jnp.int32, (Q, Q), 0)
        c = lax.broadcasted_iota(jnp.int32, (Q, Q), 1)
        mask = (r <= c) if rev else (r >= c)
        left = lane < 64
        a = dt * (-jnp.exp(alog))
        s = cums(a)
        sT, dtT = s.T, dt.T
        last_row = (sub == (0 if rev else Q - 1)).astype(F32)
        s_last = jnp.sum(s * last_row, axis=0, keepdims=True)
        G = [mm_nt(Cs[g], Bs[g]) for g in range(2)]
        M, es, wc, ed = [], [], [], []
        for h in range(NH):
            l = 16 * dirn + h
            oh_l = (lane == l).astype(F32)
            oh_s = (sub == l).astype(F32)
            s_col = jnp.sum(s * oh_l, axis=1, keepdims=True)
            dt_col = jnp.sum(dt * oh_l, axis=1, keepdims=True)
            s_row = jnp.sum(sT * oh_s, axis=0, keepdims=True)
            dt_row = jnp.sum(dtT * oh_s, axis=0, keepdims=True)
            sl = jnp.sum(s_last * oh_l, axis=1, keepdims=True)
            seg = jnp.where(mask, s_col - s_row, 0.0)
            lm = jnp.where(mask, jnp.exp(seg), 0.0)
            M.append(G[h // 8] * lm * dt_row)
            es.append(jnp.exp(s_col))
            wc.append(jnp.exp(sl - s_col) * dt_col)
            ed.append(jnp.exp(sl))
        Ys, Hn = [], []
        for j in range(8):
            g = j // 4
            xa = jnp.where(left, xs[j], 0.0)
            xb = jnp.where(left, 0.0, xs[j])
            yd = mm(M[2 * j], xa) + mm(M[2 * j + 1], xb)
            yo = mm(Cs[g], Hs[j]) * jnp.where(left, es[2 * j], es[2 * j + 1])
            Ys.append(yd + yo)
            st = mm_tn(Bs[g], xs[j] * jnp.where(left, wc[2 * j], wc[2 * j + 1]))
            Hn.append(Hs[j] * jnp.where(left, ed[2 * j], ed[2 * j + 1]) + st)
        return Ys, Hn

    return f


def _chunk_of(t, n, rev):
    if not rev:
        return t
    return jnp.where(t < 2, 1 - t, n + 1 - t)


def _cols128(ref, k, lead=()):
    return [ref[lead + (slice(None), slice(128 * j, 128 * (j + 1)))] for j in range(k)]


def ssd_fwd(name, xbc, dts, alog, n, dirs):
    nd = len(dirs)
    chunks = [_ssd_chunk(rev, dirn) for rev, dirn in dirs]

    def body(*refs):
        al_ref = refs[4 * nd]
        for d in range(nd):
            x_ref, b_ref, c_ref, dt_ref = refs[4 * d:4 * d + 4]
            y_ref, hs_ref = refs[4 * nd + 1 + 2 * d:4 * nd + 3 + 2 * d]
            h_scr = refs[4 * nd + 1 + 2 * nd + d]

            @pl.when(pl.program_id(0) == 0)
            def _(h_scr=h_scr):
                h_scr[...] = jnp.zeros_like(h_scr)

            hs_ref[0] = h_scr[...]
            Ys, Hn = chunks[d](_cols128(x_ref, 8), _cols128(b_ref, 2), _cols128(c_ref, 2), dt_ref[...], al_ref[...],
                               _cols128(h_scr, 8))
            for j in range(8):
                y_ref[:, 128 * j:128 * (j + 1)] = Ys[j]
                h_scr[:, 128 * j:128 * (j + 1)] = Hn[j]

    in_specs, out_specs, out_shape, args = [], [], [], []
    for (rev, _), dt in zip(dirs, dts):
        cm = lambda t, rev=rev: _chunk_of(t, n, rev)
        in_specs += [pl.BlockSpec((Q, 1024), lambda t, cm=cm: (cm(t), 0)), pl.BlockSpec((Q, 256), lambda t, cm=cm: (cm(t), 4)),
                     pl.BlockSpec((Q, 256), lambda t, cm=cm: (cm(t), 5)), pl.BlockSpec((Q, 128), lambda t, cm=cm: (cm(t), 0))]
        args += [xbc, xbc, xbc, dt]
        out_specs += [pl.BlockSpec((Q, 1024), lambda t, cm=cm: (cm(t), 0)),
                      pl.BlockSpec((1, Q, 1024), lambda t, cm=cm: (cm(t), 0, 0))]
        out_shape += [jax.ShapeDtypeStruct((n * Q, 1024), F32), jax.ShapeDtypeStruct((n, Q, 1024), F32)]
    res = pl.pallas_call(
        body, name=name, grid=(n,), in_specs=in_specs + [pl.BlockSpec((1, 128), lambda t: (0, 0))],
        out_specs=out_specs, out_shape=out_shape, scratch_shapes=[pltpu.VMEM((Q, 1024), F32)] * nd,
        compiler_params=_cp("arbitrary"))(*args, alog)
    return [res[2 * d:2 * d + 2] for d in range(nd)]


def ssd_bwd(name, xbc, dts, alog, hss, dy, n, dirs):
    nd = len(dirs)
    chunks = [_ssd_chunk(rev, dirn) for rev, dirn in dirs]

    def body(*refs):
        tt = pl.program_id(0)
        al_ref = refs[6 * nd]
        for d, (rev, _) in enumerate(dirs):
            x_ref, b_ref, c_ref, dt_ref, hs_ref, dy_ref = refs[6 * d:6 * d + 6]
            dx_ref, ddt_ref, dal_ref = refs[6 * nd + 1 + 3 * d:6 * nd + 4 + 3 * d]
            dh_scr = refs[6 * nd + 1 + 3 * nd + d]
            ch = _chunk_of(n - 1 - tt, n, rev)

            @pl.when(tt == 0)
            def _(dh_scr=dh_scr):
                dh_scr[...] = jnp.zeros_like(dh_scr)

            live = (ch >= 2).astype(F32)
            dYs = [v * live for v in _cols128(dy_ref, 8)]
            _, vjp_fn = jax.vjp(chunks[d], _cols128(x_ref, 8), _cols128(b_ref, 2), _cols128(c_ref, 2), dt_ref[...],
                                al_ref[...], _cols128(hs_ref, 8, (0,)))
            dxs, dBs, dCs, ddt, dal, dHs = vjp_fn((dYs, _cols128(dh_scr, 8)))
            for j in range(8):
                dx_ref[:, 128 * j:128 * (j + 1)] = dxs[j]
                dh_scr[:, 128 * j:128 * (j + 1)] = dHs[j]
            for g in range(2):
                dx_ref[:, 1024 + 128 * g:1024 + 128 * (g + 1)] = dBs[g]
                dx_ref[:, 1280 + 128 * g:1280 + 128 * (g + 1)] = dCs[g]
            ddt_ref[...] = ddt

            @pl.when(tt == 0)
            def _(dal_ref=dal_ref, dal=dal):
                dal_ref[...] = dal

            @pl.when(tt > 0)
            def _(dal_ref=dal_ref, dal=dal):
                dal_ref[...] += dal

    in_specs, out_specs, out_shape, args = [], [], [], []
    for (rev, _), dt, hs in zip(dirs, dts, hss):
        cm = lambda t, rev=rev: _chunk_of(n - 1 - t, n, rev)
        in_specs += [pl.BlockSpec((Q, 1024), lambda t, cm=cm: (cm(t), 0)), pl.BlockSpec((Q, 256), lambda t, cm=cm: (cm(t), 4)),
                     pl.BlockSpec((Q, 256), lambda t, cm=cm: (cm(t), 5)), pl.BlockSpec((Q, 128), lambda t, cm=cm: (cm(t), 0)),
                     pl.BlockSpec((1, Q, 1024), lambda t, cm=cm: (cm(t), 0, 0)),
                     pl.BlockSpec((Q, 1024), lambda t, cm=cm: (jnp.maximum(cm(t) - 2, 0), 0))]
        args += [xbc, xbc, xbc, dt, hs, dy]
        out_specs += [pl.BlockSpec((Q, 1536), lambda t, cm=cm: (cm(t), 0)), pl.BlockSpec((Q, 128), lambda t, cm=cm: (cm(t), 0)),
                      pl.BlockSpec((1, 128), lambda t: (0, 0))]
        out_shape += [jax.ShapeDtypeStruct((n * Q, 1536), F32), jax.ShapeDtypeStruct((n * Q, 128), F32),
                      jax.ShapeDtypeStruct((1, 128), F32)]
    res = pl.pallas_call(
        body, name=name, grid=(n,), in_specs=in_specs + [pl.BlockSpec((1, 128), lambda t: (0, 0))],
        out_specs=out_specs, out_shape=out_shape, scratch_shapes=[pltpu.VMEM((Q, 1024), F32)] * nd,
        compiler_params=_cp("arbitrary"))(*args, alog)
    return [res[3 * d:3 * d + 3] for d in range(nd)]


def f_norm0(c, x, g0, b0, sc, sh, is_ctx):
    x0 = _ln(jnp.where(is_ctx > 0.5, c, x), g0, b0)
    return x0, x0 * (1.0 + sc) + sh


def f_dt(raw, bias):
    z = split4(raw)[0] + bias
    dt = jnp.maximum(z, 0.0) + jnp.log1p(jnp.exp(-jnp.abs(z)))
    return dt, dt


def f_gated_norm(yf, yb, xs, z, dcol, g):
    h = (yf + yb + xs * dcol) * _silu(z)
    return (h * lax.rsqrt(jnp.mean(h * h, axis=-1, keepdims=True) + LN_EPS) * g,)


def f_gmlp(uv, gmg, gmb, *wb):
    ws, bs = wb[:8], wb[8:]
    u, v = split2(uv)
    vn = split8(_ln(_gelu(v), gmg, gmb))
    mixed = concat8(tuple(mm(ws[g], vn[g]) + bs[g] for g in range(8)))
    return (_gelu(u) * mixed,)


def f_merge(ps, pg, gates, bg):
    gs, gg = split2(jax.nn.sigmoid(gates + bg))
    return (gs * ps + gg * pg,)


def f_res1(x0, out, g1, lg, lb, sc, sh):
    x1 = _ln(ALPHA * x0 + g1 * out, lg, lb)
    return x1, x1 * (1.0 + sc) + sh


def f_res2_loss(x1, ff, tgt, g2, lg, lb):
    x2 = _ln(ALPHA * x1 + g2 * ff, lg, lb)
    e = x2 - tgt
    return (0.5 * jnp.sum(jnp.mean(e * e, axis=-1, keepdims=True), axis=0, keepdims=True),)


def _row_tile(M):
    return 544 if M % 544 == 0 else (512 if M % 512 == 0 else M)


def core(ctx, x, tgt, mod_x, mod_c, X, S):
    L = x.shape[0]
    R = LC + L
    n = R // Q
    T = 256
    nt, ntl = R // T, L // T
    tmR, tmL = _row_tile(R), _row_tile(L)
    tkR = 256 if R % 512 else 512
    tkL = 512 if L % 512 == 0 else 256
    row = lambda v: v.reshape(1, -1)
    mx = [row(mod_x[k]) for k in range(6)]
    mc = [row(mod_c[k]) for k in range(6)]
    sel = lambda i: jnp.minimum(i, 1)
    sc1 = jnp.stack([mc[1], mx[1]])
    sh1 = jnp.stack([mc[0], mx[0]])
    ln0 = [(row(S['ln0_g']), None), (row(S['ln0_b']), None), (sc1, sel), (sh1, sel),
           (jnp.array([1.0, 0.0], F32).reshape(2, 1, 1), sel)]
    x_n0 = [(ctx, D, 0, -nt), (x, D, 0, -1)]

    x0, xm = stage_fwd("norm0_fwd", f_norm0, T, nt, x_n0, ln0, [('new', R, D, 0), ('new', R, D, 0, BF16)])
    w_in = X.w_in()
    proj = matmul("proj_fwd", xm, w_in, 'nn', tmR, PW // 2, 1024)
    conv_w8 = jnp.pad(S['conv_w'], ((0, 3), (0, 0)))
    conv_b = row(S['conv_b'])
    xbc = conv_fwd("conv_fwd", proj, conv_w8, conv_b, R)
    dt_bias = jnp.pad(S['dt_bias'].reshape(1, 32), ((0, 0), (0, 96)))
    alog = jnp.pad(S['a_log'].reshape(1, 32), ((0, 0), (0, 96)))
    x_dt = [(proj, 512, C_DT // 512, 0)]
    dt_f, dt_b = stage_fwd("dt_fwd", f_dt, T, nt, x_dt, [(dt_bias, None)], [('new', R, 128, 0), ('new', R, 128, 0)])
    directions = [(False, 0), (True, 1)]
    (y_f, hs_f), (y_b, hs_b) = ssd_fwd("ssd_fwd", xbc, [dt_f, dt_b], alog, n, directions)
    W = X.rest()
    dcol = jnp.repeat(S['d_skip'][0] + S['d_skip'][1], 64).reshape(1, D)
    x_gn = [(y_f, D, 0, 1), (y_b, D, 0, 1), (xbc, D, 0, 1), (proj, D, C_Z // D, 1)]
    p_gn = [(dcol, None), (row(S['ssd_norm_g']), None)]
    (yn,) = stage_fwd("gnorm_fwd", f_gated_norm, T, ntl, x_gn, p_gn, [('new', L, D, 0, BF16)])
    x_gm = [(proj, 2 * D, C_UV // (2 * D), LC // Q)]
    p_gm = ([(row(S['gm_norm_g']), None), (row(S['gm_norm_b']), None)]
            + [(S['w_spatial'][g], None) for g in range(8)] + [(S['b_spatial'][g].reshape(Q, 1), None) for g in range(8)])
    (y_gm,) = stage_fwd("gmlp_fwd", f_gmlp, Q, L // Q, x_gm, p_gm, [('new', L, D, 0, BF16)])
    p_ssd = matmul("pssd_fwd", yn, W['w_ssd_proj'], 'nn', tmL, 1024, 1024)
    p_g = matmul("pgm_fwd", y_gm, W['w_gm_proj'], 'nn', tmL, 1024, 1024)
    x_mg = [(p_ssd, D, 0, 0), (p_g, D, 0, 0), (proj, 2 * D, C_GATE // (2 * D), 1)]
    p_mg = [(row(S['b_gate']), None)]
    (merged,) = stage_fwd("merge_fwd", f_merge, T, ntl, x_mg, p_mg, [('new', L, D, 0, BF16)])
    out = matmul("out_fwd", merged, W['w_out'], 'nn', tmL, 1024, 1024)
    x_r1 = [(x0, D, 0, 1), (out, D, 0, 0)]
    p_r1 = [(mx[2], None), (row(S['ln1_g']), None), (row(S['ln1_b']), None), (mx[4], None), (mx[3], None)]
    x1, hm = stage_fwd("res1_fwd", f_res1, T, ntl, x_r1, p_r1, [('new', L, D, 0), ('new', L, D, 0, BF16)])
    a1, a3, act = ffn_in_fwd("ffn_in_fwd", hm, W['w_ff1'], W['w_ff3'], T)
    ff = ff_out_fwd("ff2_fwd", act, W['w_ff2'], tmL)
    x_r2 = [(x1, D, 0, 0), (ff, D, 0, 0), (tgt, D, 0, 0)]
    p_r2 = [(mx[5], None), (row(S['ln2_g']), None), (row(S['ln2_b']), None)]

    dx1_a, dff, dg2, dl2g, dl2b, loss = stage_bwd(
        "res2_bwd", f_res2_loss, T, ntl, x_r2, p_r2, [1.0],
        [('new', L, D, 0), ('new', L, D, 0, BF16), None], [True, True, True], primal=[(0, (1, 1))])
    da1, da3 = ffn_out_bwd_x("ffn_out_bwd_x", dff, W['w_ff2'], a1, a3, T)
    gw_ff2 = ff_out_bwd_w("ff2_bwd_w", act, dff, tkL)
    dhm = ff_in_bwd_x("ff1_bwd_x", da1, W['w_ff1'], tmL)
    dhm = ff_in_bwd_x("ff3_bwd_x", da3, W['w_ff3'], tmL, add=dhm)
    gw_ff1 = ff_in_bwd_w("ff1_bwd_w", hm, da1, tkL)
    gw_ff3 = ff_in_bwd_w("ff3_bwd_w", hm, da3, tkL)
    X.grads('ffn', {'w_ff2': gw_ff2, 'w_ff1': gw_ff1, 'w_ff3': gw_ff3})
    dx0_a, dout, dg1, dl1g, dl1b, dsc2, dsh2 = stage_bwd(
        "res1_bwd", f_res1, T, ntl, x_r1, p_r1, [(dx1_a, D, 0, 0), (dhm, D, 0, 0)],
        [('new', L, D, 0), ('new', L, D, 0, BF16)], [True] * 5)
    dmerged = matmul("out_bwd_x", dout, W['w_out'], 'nt', tmL, 1024, 1024)
    gw_out = matmul("out_bwd_w", merged, dout, 'tn', 1024, 1024, tkL, BF16)
    lt, lq = -(LC // T), -(LC // Q)
    x_mg_b = [(p_ssd, D, 0, lt), (p_g, D, 0, lt), (proj, 2 * D, C_GATE // (2 * D), 0)]
    dp_ssd, dp_g, dproj, dbg = stage_bwd(
        "merge_bwd", f_merge, T, nt, x_mg_b, p_mg, [(dmerged, D, 0, lt)],
        [('new', L, D, lt, BF16), ('new', L, D, lt, BF16), ('part', R, PW, 2 * D, C_GATE // (2 * D), 0, BF16)], [True])
    dyn = matmul("pssd_bwd_x", dp_ssd, W['w_ssd_proj'], 'nt', tmL, 1024, 1024)
    gw_ssd = matmul("pssd_bwd_w", yn, dp_ssd, 'tn', 1024, 1024, tkL, BF16)
    dy_gm = matmul("pgm_bwd_x", dp_g, W['w_gm_proj'], 'nt', tmL, 1024, 1024)
    gw_gm = matmul("pgm_bwd_w", y_gm, dp_g, 'tn', 1024, 1024, tkL, BF16)
    X.grads('proj', {'w_out': gw_out, 'w_ssd_proj': gw_ssd, 'w_gm_proj': gw_gm})
    r_gm = stage_bwd("gmlp_bwd", f_gmlp, Q, n, [(proj, 2 * D, C_UV // (2 * D), 0)], p_gm, [(dy_gm, D, 0, lq)],
                     [('alias', dproj, 2 * D, C_UV // (2 * D), 0)], [True] * 18)
    dproj, dgmg, dgmb, dws, dbs = r_gm[0], r_gm[1], r_gm[2], r_gm[3:11], r_gm[11:19]
    x_gn_b = [(y_f, D, 0, 0), (y_b, D, 0, 0), (xbc, D, 0, 0), (proj, D, C_Z // D, 0)]
    dy, dskipx, dproj, ddcol, dng = stage_bwd(
        "gnorm_bwd", f_gated_norm, T, nt, x_gn_b, p_gn, [(dyn, D, 0, lt)],
        [('new', L, D, lt), None, ('new', L, D, lt), ('alias', dproj, D, C_Z // D, 0)], [True, True])
    (dxbc_f, ddt_f, dal_f), (dxbc_b, ddt_b, dal_b) = ssd_bwd("ssd_bwd", xbc, [dt_f, dt_b], alog, [hs_f, hs_b], dy, n,
                                                             directions)
    dproj, ddtb = stage_bwd("dt_bwd", f_dt, T, nt, x_dt, [(dt_bias, None)],
                            [(ddt_f, 128, 0, 0), (ddt_b, 128, 0, 0)],
                            [('alias', dproj, 512, C_DT // 512, 0)], [True])
    dproj, dcw8, dcb = conv_bwd("conv_bwd", proj, conv_w8, conv_b, dxbc_f, dxbc_b, dskipx, dproj, R)
    gw_in = matmul("proj_bwd_w", xm, dproj, 'tn', 1024, PW // 2, tkR, BF16)
    X.grads('in', {'w_in': gw_in})
    dxm = matmul("proj_bwd_x", dproj, w_in, 'nt', R // 2 if R % 16 == 0 else R, 1024, 1024)
    grad_x, dl0g, dl0b, dsc1, dsh1 = stage_bwd(
        "norm0_bwd", f_norm0, T, nt, x_n0, ln0, [(dx0_a, D, 0, -1), (dxm, D, 0, 0)],
        [None, ('new', L, D, -1)], [True] * 4 + [False])

    zero = jnp.zeros((D,), F32)
    flat = lambda v: v.reshape(-1)
    small = {
        'loss': flat(loss), 'ln0_g': flat(dl0g), 'ln0_b': flat(dl0b),
        'dmod_x': jnp.concatenate([flat(dsh1[1]), flat(dsc1[1]), flat(dg1), flat(dsh2), flat(dsc2), flat(dg2)]),
        'dmod_c': jnp.concatenate([flat(dsh1[0]), flat(dsc1[0]), zero, zero, zero, zero]),
        'conv_w': flat(dcw8[:5]), 'conv_b': flat(dcb), 'dt_bias': flat(ddtb[:, :32]),
        'a_log': flat((dal_f + dal_b)[:, :32]),
        'd_skip': flat(jnp.tile(ddcol.reshape(1, NH, 64).sum(-1), (2, 1))),
        'ssd_norm_g': flat(dng), 'gm_norm_g': flat(dgmg), 'gm_norm_b': flat(dgmb),
        'w_spatial': flat(jnp.stack(dws)), 'b_spatial': flat(jnp.stack(dbs)), 'b_gate': flat(dbg),
        'ln1_g': flat(dl1g), 'ln1_b': flat(dl1b), 'ln2_g': flat(dl2g), 'ln2_b': flat(dl2b),
    }
    return grad_x, small


def _place():
    return lax.axis_index("x"), lax.axis_index("y"), lax.axis_index("c")


def allgather8(name, blk, hbm):
    space = pl.ANY if hbm else pltpu.VMEM

    def body(x_ref, out_ref, send_sems, recv_sems, local_sem):
        x, y, c = _place()
        me, sibling = (x, y, c), (x, y, 1 - c)
        chips = [(1 - x, y), (x, 1 - y), (1 - x, 1 - y)]

        def slot(px, py, pc):
            return out_ref.at[4 * px + 2 * py + pc]

        def copy(k, block, to, src=None):
            return pltpu.make_async_remote_copy(
                src_ref=slot(*block) if src is None else src, dst_ref=slot(*block),
                send_sem=send_sems.at[k], recv_sem=recv_sems.at[k], device_id=to, device_id_type=MESH)

        mine = pltpu.make_async_copy(x_ref, slot(*me), local_sem)
        mine.start()
        first = [copy(0, me, sibling, src=x_ref)]
        first += [copy(1 + j, me, (*chip, c), src=x_ref) for j, chip in enumerate(chips)]
        for cp in first:
            cp.start()
        passed = [copy(4 + j, (*chip, c), sibling) for j, chip in enumerate(chips)]
        for j, chip in enumerate(chips):
            copy(1 + j, (*chip, c), me).wait_recv()
            passed[j].start()
        copy(0, sibling, me).wait_recv()
        for j, chip in enumerate(chips):
            copy(4 + j, (*chip, 1 - c), me).wait_recv()
        for cp in first + passed:
            cp.wait_send()
        mine.wait()

    return pl.pallas_call(
        body, name=name, out_shape=jax.ShapeDtypeStruct((8,) + blk.shape, blk.dtype),
        in_specs=[pl.BlockSpec(memory_space=space)], out_specs=pl.BlockSpec(memory_space=space),
        scratch_shapes=[pltpu.SemaphoreType.DMA((7,)), pltpu.SemaphoreType.DMA((7,)), pltpu.SemaphoreType.DMA],
        compiler_params=pltpu.CompilerParams(vmem_limit_bytes=VMEM_LIMIT_V7X))(blk)


def _peers(place):
    x, y, c = place
    return [((1 - x) if k & 4 else x, (1 - y) if k & 2 else y, (1 - c) if k & 1 else c) for k in range(1, 8)]


def _slot(p):
    return 4 * p[0] + 2 * p[1] + p[2]


def plan_gather(place, srcs, lands):
    remote = [(s, l.at[_slot(place)], to) for s, l in zip(srcs, lands) for to in _peers(place)]
    return remote, [(s, l.at[_slot(place)]) for s, l in zip(srcs, lands)]


def plan_to_owner(place, srcs, lands):
    remote = [(s.at[2 * to[0] + to[1], to[2]], l.at[_slot(place)], to) for s, l in zip(srcs, lands) for to in _peers(place)]
    x, y, c = place
    return remote, [(s.at[2 * x + y, c], l.at[_slot(place)]) for s, l in zip(srcs, lands)]


def sequencer_exchange(name, collective_id, srcs, land_shapes, plan):
    n = len(srcs)
    src_refs = [jax.new_ref(a, memory_space=pltpu.MemorySpace.HBM) for a in srcs]
    land_refs = [jax.empty_ref(s, memory_space=pltpu.MemorySpace.HBM) for s in land_shapes]

    @pl.kernel(mesh=plsc.ScalarSubcoreMesh(axis_name="sequencer", num_cores=1), name=name,
               scratch_types=(pltpu.SemaphoreType.DMA((7 * n,)), pltpu.SemaphoreType.DMA((7 * n,)),
                              pltpu.SemaphoreType.DMA((n,))),
               compiler_params=pltpu.CompilerParams(collective_id=collective_id))
    def launch(send_sems, recv_sems, local_sems):
        place = _place()
        barrier = pltpu.get_barrier_semaphore()
        for to in _peers(place):
            pl.semaphore_signal(barrier, inc=1, device_id=to, device_id_type=MESH)
        pl.semaphore_wait(barrier, 7)
        remote, local = plan(place, src_refs, land_refs)
        mine = [pltpu.make_async_copy(s, d, local_sems.at[a]) for a, (s, d) in enumerate(local)]
        for cp in mine:
            cp.start()
        cps = [pltpu.make_async_remote_copy(src_ref=s, dst_ref=d, send_sem=send_sems.at[k], recv_sem=recv_sems.at[k],
                                            device_id=to, device_id_type=MESH) for k, (s, d, to) in enumerate(remote)]
        for cp in cps:
            cp.start()
        for cp in mine:
            cp.wait()
        for cp in cps:
            cp.wait()

    launch()
    return land_refs


def sibling_pair(name, hs):
    n = len(hs)

    def body(*refs):
        ins, outs = refs[:n], refs[n:2 * n]
        send_sems, recv_sems = refs[2 * n:]
        x, y, c = _place()
        cps = [pltpu.make_async_remote_copy(src_ref=outs[a].at[c], dst_ref=outs[a].at[c], send_sem=send_sems.at[a],
                                            recv_sem=recv_sems.at[a], device_id=(x, y, 1 - c), device_id_type=MESH)
               for a in range(n)]
        for cp in cps:
            cp.start()
        for a in range(n):
            pltpu.make_async_remote_copy(src_ref=outs[a].at[1 - c], dst_ref=outs[a].at[1 - c], send_sem=send_sems.at[a],
                                         recv_sem=recv_sems.at[a], device_id=(x, y, 1 - c),
                                         device_id_type=MESH).wait_recv()
        for cp in cps:
            cp.wait_send()

    any_spec = pl.BlockSpec(memory_space=pl.ANY)
    return pl.pallas_call(
        body, name=name, out_shape=[jax.ShapeDtypeStruct(h.shape, h.dtype) for h in hs],
        in_specs=[any_spec] * n, out_specs=[any_spec] * n, input_output_aliases={a: a for a in range(n)},
        scratch_shapes=[pltpu.SemaphoreType.DMA((n,)), pltpu.SemaphoreType.DMA((n,))])(*hs)


def owner_sum(name, land):
    _, r, w = land.shape
    T = r // 2

    def body(_, l_ref, o_ref):
        acc = l_ref[0].astype(F32)
        for j in range(1, 8):
            acc = acc + l_ref[j].astype(F32)
        o_ref[...] = acc

    grid_spec = pltpu.PrefetchScalarGridSpec(
        num_scalar_prefetch=1, grid=(2,),
        in_specs=[pl.BlockSpec((8, T, w), lambda i, at: (0, i, 0))],
        out_specs=pl.BlockSpec((None, T, w), lambda i, at: (at[0], i, 0)))
    at = jnp.stack([lax.axis_index("c")]).astype(jnp.int32)
    return pl.pallas_call(body, name=name, grid_spec=grid_spec, out_shape=jax.ShapeDtypeStruct((2, r, w), F32),
                          compiler_params=_cp("arbitrary"))(at, land)


W_IN_RUNS = ((0, 2, 1296, 376), (376, 3, 0, 1672), (2048, 1, 920, 752), (2800, 2, 0, 1296), (4096, 0, 0, 1024),
             (5120, 0, 1024, 648), (5768, 1, 0, 920))


def w_in_to_padded(name, g4):
    T = 128

    def body(g_ref, o_ref):
        o_ref[:, D_PROJ:PW] = jnp.zeros((T, PW - D_PROJ), o_ref.dtype)
        for (a, s, j0, w) in W_IN_RUNS:
            o_ref[:, a:a + w] = g_ref[s, :, j0:j0 + w]

    return pl.pallas_call(body, name=name, grid=(D // T,), in_specs=[pl.BlockSpec((4, T, 1672), lambda i: (0, i, 0))],
                          out_specs=pl.BlockSpec((T, PW), lambda i: (i, 0)),
                          out_shape=jax.ShapeDtypeStruct((D, PW), g4.dtype), compiler_params=_cp("arbitrary"))(g4)


def w_in_from_padded(name, gp):
    T = 128

    def body(g_ref, o_ref):
        for (a, s, j0, w) in W_IN_RUNS:
            o_ref[s, :, j0:j0 + w] = g_ref[:, a:a + w]

    return pl.pallas_call(body, name=name, grid=(D // T,), in_specs=[pl.BlockSpec((T, PW), lambda i: (i, 0))],
                          out_specs=pl.BlockSpec((4, T, 1672), lambda i: (0, i, 0)),
                          out_shape=jax.ShapeDtypeStruct((4, D, 1672), gp.dtype), compiler_params=_cp("arbitrary"))(gp)


def sum_devices(name, g):
    def body(g_ref, o_ref):
        acc = g_ref[0]
        for k in range(1, 8):
            acc = acc + g_ref[k]
        o_ref[...] = acc

    return pl.pallas_call(body, name=name, out_shape=jax.ShapeDtypeStruct(g.shape[1:], F32),
                          compiler_params=pltpu.CompilerParams(vmem_limit_bytes=VMEM_LIMIT_V7X))(g)


def adamw(name, w, g, m, v, T):
    r, wd = w.shape
    c1 = 1.0 - ADAM_B1 ** ADAM_STEP
    c2 = 1.0 - ADAM_B2 ** ADAM_STEP

    def body(w_ref, g_ref, m_ref, v_ref, d_ref, mo_ref, vo_ref):
        gv = g_ref[...]
        mn = ADAM_B1 * m_ref[...] + (1.0 - ADAM_B1) * gv
        vn = ADAM_B2 * v_ref[...] + (1.0 - ADAM_B2) * (gv * gv)
        d_ref[...] = -ADAM_LR * ((mn / c1) / (jnp.sqrt(vn / c2) + ADAM_EPS) + ADAM_WD * w_ref[...])
        mo_ref[...] = mn
        vo_ref[...] = vn

    spec = pl.BlockSpec((T, wd), lambda i: (i, 0))
    return pl.pallas_call(body, name=name, grid=(r // T,), in_specs=[spec] * 4, out_specs=[spec] * 3,
                          out_shape=[jax.ShapeDtypeStruct((r, wd), F32)] * 3, compiler_params=_cp("arbitrary"))(w, g, m, v)


BIG = {'w_in': (1024, 1672), 'w_ssd_proj': (256, 1024), 'w_gm_proj': (256, 1024), 'w_out': (256, 1024),
       'w_ff1': (1024, 704), 'w_ff3': (1024, 704), 'w_ff2': (704, 1024)}


class Flat:
    def __init__(self, segs):
        self.off, o = {}, 0
        for name, size in segs:
            self.off[name] = (o, size)
            o += -(-size // 128) * 128
        self.rows = -(-o // 1024) * 8

    def pack(self, vals):
        parts = []
        for name, (o, size) in self.off.items():
            v = vals[name].reshape(-1).astype(F32)
            parts.append(jnp.pad(v, (0, -(-size // 128) * 128 - size)))
        buf = jnp.concatenate(parts)
        return jnp.pad(buf, (0, self.rows * 128 - buf.shape[0])).reshape(self.rows, 128)

    def get(self, buf, name, shape=None):
        o, size = self.off[name]
        v = buf[o // 128:(o + size + 127) // 128].reshape(-1)[:size]
        return v if shape is None else v.reshape(shape)


PARTIALS = Flat([('loss', 1), ('ln0_g', D), ('ln0_b', D), ('dmod_x', 6 * D), ('dmod_c', 6 * D), ('conv_w', 5 * 1536),
                 ('conv_b', 1536), ('dt_bias', 32), ('a_log', 32), ('d_skip', 32), ('ssd_norm_g', D),
                 ('gm_norm_g', D), ('gm_norm_b', D), ('w_spatial', 8 * Q * Q), ('b_spatial', 8 * Q), ('b_gate', 2 * D),
                 ('ln1_g', D), ('ln1_b', D), ('ln2_g', D), ('ln2_b', D)])

WEIGHTS = ('c_ctx', 'ln0_g', 'ln0_b', 'w_ada', 'b_ada', 'w_in', 'conv_w', 'conv_b', 'dt_bias', 'a_log', 'd_skip',
           'ssd_norm_g', 'gm_norm_g', 'gm_norm_b', 'w_spatial', 'b_spatial', 'b_gate', 'w_ssd_proj', 'w_gm_proj',
           'w_out', 'ln1_g', 'ln1_b', 'w_ff1', 'w_ff3', 'w_ff2', 'ln2_g', 'ln2_b')
BIG_NAMES = tuple(BIG)
SMALL_NAMES = tuple(n for n in WEIGHTS if n not in BIG_NAMES and n != 'w_ada')


def kernel(x, c, ctx, c_ctx, ln0_g, ln0_b, w_ada, b_ada, w_in, conv_w, conv_b, dt_bias, a_log, d_skip, ssd_norm_g, gm_norm_g, gm_norm_b, w_spatial, b_spatial, b_gate, w_ssd_proj, w_gm_proj, w_out, ln1_g, ln1_b, w_ff1, w_ff3, w_ff2, ln2_g, ln2_b, loss_target, m_c_ctx, m_ln0_g, m_ln0_b, m_w_ada, m_b_ada, m_w_in, m_conv_w, m_conv_b, m_dt_bias, m_a_log, m_d_skip, m_ssd_norm_g, m_gm_norm_g, m_gm_norm_b, m_w_spatial, m_b_spatial, m_b_gate, m_w_ssd_proj, m_w_gm_proj, m_w_out, m_ln1_g, m_ln1_b, m_w_ff1, m_w_ff3, m_w_ff2, m_ln2_g, m_ln2_b, v_c_ctx, v_ln0_g, v_ln0_b, v_w_ada, v_b_ada, v_w_in, v_conv_w, v_conv_b, v_dt_bias, v_a_log, v_d_skip, v_ssd_norm_g, v_gm_norm_g, v_gm_norm_b, v_w_spatial, v_b_spatial, v_b_gate, v_w_ssd_proj, v_w_gm_proj, v_w_out, v_ln1_g, v_ln1_b, v_w_ff1, v_w_ff3, v_w_ff2, v_ln2_g, v_ln2_b):
    wts = dict(c_ctx=c_ctx, ln0_g=ln0_g, ln0_b=ln0_b, w_ada=w_ada, b_ada=b_ada, w_in=w_in, conv_w=conv_w, conv_b=conv_b,
               dt_bias=dt_bias, a_log=a_log, d_skip=d_skip, ssd_norm_g=ssd_norm_g, gm_norm_g=gm_norm_g,
               gm_norm_b=gm_norm_b, w_spatial=w_spatial, b_spatial=b_spatial, b_gate=b_gate, w_ssd_proj=w_ssd_proj,
               w_gm_proj=w_gm_proj, w_out=w_out, ln1_g=ln1_g, ln1_b=ln1_b, w_ff1=w_ff1, w_ff3=w_ff3, w_ff2=w_ff2,
               ln2_g=ln2_g, ln2_b=ln2_b)
    ms = dict(zip(WEIGHTS, (m_c_ctx, m_ln0_g, m_ln0_b, m_w_ada, m_b_ada, m_w_in, m_conv_w, m_conv_b, m_dt_bias, m_a_log,
                            m_d_skip, m_ssd_norm_g, m_gm_norm_g, m_gm_norm_b, m_w_spatial, m_b_spatial, m_b_gate,
                            m_w_ssd_proj, m_w_gm_proj, m_w_out, m_ln1_g, m_ln1_b, m_w_ff1, m_w_ff3, m_w_ff2, m_ln2_g,
                            m_ln2_b)))
    vs = dict(zip(WEIGHTS, (v_c_ctx, v_ln0_g, v_ln0_b, v_w_ada, v_b_ada, v_w_in, v_conv_w, v_conv_b, v_dt_bias, v_a_log,
                            v_d_skip, v_ssd_norm_g, v_gm_norm_g, v_gm_norm_b, v_w_spatial, v_b_spatial, v_b_gate,
                            v_w_ssd_proj, v_w_gm_proj, v_w_out, v_ln1_g, v_ln1_b, v_w_ff1, v_w_ff3, v_w_ff2, v_ln2_g,
                            v_ln2_b)))
    px, py, pc = _place()
    shard = 2 * px + py
    dev = 2 * shard + pc
    take = lambda a, i, axis=0: lax.dynamic_index_in_dim(a, i, axis, keepdims=False)

    half = lambda n: take(wts[n][0].reshape(2, BIG[n][0] // 2, BIG[n][1]), pc).astype(BF16)

    pre = jnp.concatenate([c, jnp.pad(conv_w[0], ((0, 0), (0, D - 384))), jnp.zeros((2, D), F32)], axis=0)
    pre = allgather8("gather_cond", pre, False)
    conv_w_full = pre[0::2, 1:6, :384].transpose(1, 0, 2).reshape(5, 1536)
    a16 = jnp.concatenate([_silu(pre[:, 0, :]), _silu(c_ctx)[None], jnp.zeros((7, D), F32)], axis=0)
    mod = matmul("ada_fwd", a16, w_ada[0], 'nn', 16, 512, 1024)
    mod = mod + lax.dynamic_slice_in_dim(b_ada[0], shard * 1536, 1536)[None]
    mod = allgather8("gather_mod", mod, False)
    mod = jnp.concatenate([mod[0], mod[2], mod[4], mod[6]], axis=1)
    mod_x = take(mod, dev).reshape(6, D)
    mod_c = mod[8].reshape(6, D)

    def full(n, blocks):
        r, w = BIG[n]
        return blocks.reshape(4, r, w) if w != D else blocks.reshape(4 * r, w)

    class Exchanges:
        rest_names = BIG_NAMES[1:]

        def __init__(self):
            self.pending = []

        def w_in(self):
            blocks = allgather8("gather_w_in", half('w_in'), True)
            w = w_in_to_padded("w_in_layout", full('w_in', blocks))
            halves = [half(n) for n in self.rest_names]
            halves[0], _ = lax.optimization_barrier((halves[0], blocks))
            lands = [jax.ShapeDtypeStruct((8,) + h.shape, BF16) for h in halves]
            self.rest_refs = sequencer_exchange("gather_rest", 1, halves, lands, plan_gather)
            return w

        def rest(self):
            return {n: full(n, r[...]) for n, r in zip(self.rest_names, self.rest_refs)}

        def grads(self, group, gs):
            if group == 'in':
                gs = {'w_in': w_in_from_padded("w_in_grad_layout", gs['w_in'])}
            blocks = [g.reshape(4, 2, BIG[n][0] // 2, BIG[n][1]) for n, g in gs.items()]
            lands = [jax.ShapeDtypeStruct((8,) + b.shape[2:], BF16) for b in blocks]
            refs = sequencer_exchange("grads_" + group, 2 + len(self.pending), blocks, lands, plan_to_owner)
            self.pending.append((tuple(gs), refs))

        def finish(self):
            names, halves = [], []
            for ns, refs in self.pending:
                names += ns
                halves += [owner_sum("grads_sum_" + n, r[...]) for n, r in zip(ns, refs)]
            return {n: h.reshape(BIG[n]) for n, h in zip(names, sibling_pair("grads_halves", halves))}

    S = dict(ln0_g=ln0_g, ln0_b=ln0_b, conv_w=conv_w_full, conv_b=conv_b[0], dt_bias=dt_bias[0], a_log=a_log[0],
             d_skip=d_skip[0], ssd_norm_g=ssd_norm_g[0], gm_norm_g=gm_norm_g[0], gm_norm_b=gm_norm_b[0],
             w_spatial=w_spatial[0], b_spatial=b_spatial[0], b_gate=b_gate[0], ln1_g=ln1_g[0], ln1_b=ln1_b[0],
             ln2_g=ln2_g[0], ln2_b=ln2_b[0])
    exchanges = Exchanges()
    grad_x, gsmall = core(ctx[0], x[0], loss_target[0], mod_x, mod_c, exchanges, S)

    parts = allgather8("gather_partials", PARTIALS.pack(gsmall), False)
    tot = sum_devices("partials_sum", parts)
    g_shards = exchanges.finish()
    g = {n: PARTIALS.get(tot, n) for n in ('ln0_g', 'ln0_b', 'conv_b', 'dt_bias', 'a_log', 'd_skip', 'ssd_norm_g',
                                           'gm_norm_g', 'gm_norm_b', 'w_spatial', 'b_spatial', 'b_gate', 'ln1_g',
                                           'ln1_b', 'ln2_g', 'ln2_b')}
    loss = PARTIALS.get(tot, 'loss', ())
    dmod_c = PARTIALS.get(tot, 'dmod_c')
    g['b_ada'] = PARTIALS.get(tot, 'dmod_x') + dmod_c
    g['conv_w'] = lax.dynamic_slice_in_dim(PARTIALS.get(tot, 'conv_w', (5, 1536)), shard * 384, 384, axis=1)
    o, size = PARTIALS.off['dmod_x']
    dmod_rows = parts[:, o // 128:(o + size) // 128].reshape(8, size)
    dm = jnp.concatenate([dmod_rows, dmod_c[None], jnp.zeros((7, 6 * D), F32)], axis=0)
    dm = lax.dynamic_slice_in_dim(dm, shard * 1536, 1536, axis=1)
    g['w_ada'] = matmul("ada_bwd_w", a16, dm, 'tn', 1024, 512, 16)
    dm_c = jnp.concatenate([dm[8:9], jnp.zeros((15, 1536), F32)], axis=0)
    dc = matmul("ada_bwd_c", dm_c, w_ada[0], 'nt', 16, 1024, 512)
    dc = allgather8("gather_dcctx", dc, False)[:, 0, :]
    dc = ((dc[0] + dc[2]) + dc[4]) + dc[6]
    sg = jax.nn.sigmoid(c_ctx)
    g['c_ctx'] = dc * (sg * (1.0 + c_ctx * (1.0 - sg)))
    for n in BIG_NAMES:
        g[n] = g_shards[n]

    delta, new_m, new_v = {}, {}, {}
    for n in BIG_NAMES + ('w_ada',):
        r, w = wts[n].shape[1:]
        if w % 128:
            T = max(t for t in range(8, 257, 8) if w % t == 0)
            d_, m_, v_ = adamw("adamw_" + n, wts[n][0].T, g[n].T, ms[n][0].T, vs[n][0].T, T)
            delta[n], new_m[n], new_v[n] = d_.T, m_.T, v_.T
        else:
            T = 352 if n == 'w_ff2' else 256
            delta[n], new_m[n], new_v[n] = adamw("adamw_" + n, wts[n][0], g[n], ms[n][0], vs[n][0], T)
    lay = Flat([(n, wts[n].size) for n in SMALL_NAMES])
    d_, m_, v_ = adamw("adamw_small", lay.pack(wts), lay.pack(g), lay.pack(ms), lay.pack(vs), lay.rows)
    for n in SMALL_NAMES:
        delta[n], new_m[n], new_v[n] = (lay.get(b, n) for b in (d_, m_, v_))

    shp = lambda d: [d[n].reshape(wts[n].shape) for n in WEIGHTS]
    return (loss, grad_x[None], *shp(g), *shp(delta), *shp(new_m), *shp(new_v))
```

```python
import functools

import jax
import jax.numpy as jnp
from jax import lax
from jax.experimental import pallas as pl
from jax.experimental.pallas import tpu as pltpu
from jax.experimental.pallas import tpu_sc as plsc

F32 = jnp.float32
BF16 = jnp.bfloat16
MESH = pl.DeviceIdType.MESH

VMEM_LIMIT_V7X = 56 * 1024 * 1024

D = 1024
LC = 256
Q = 128
NH = 16
D_FF = 2816
LN_EPS = 1e-5
ALPHA = 2.0 ** 0.25

PW = 7168
C_GATE, C_UV, C_Z, C_XBC, C_DT = 0, 2048, 4096, 5120, 6656
D_PROJ = 6688

ADAM_LR, ADAM_B1, ADAM_B2, ADAM_EPS, ADAM_WD, ADAM_STEP = 0.001, 0.9, 0.999, 1e-08, 0.01, 10


def _cp(*sem):
    return pltpu.CompilerParams(dimension_semantics=sem, vmem_limit_bytes=VMEM_LIMIT_V7X)


def _dot(a, b, ca, cb):
    return lax.dot_general(a.astype(BF16), b.astype(BF16), (((ca,), (cb,)), ((), ())),
                           preferred_element_type=F32)


@jax.custom_vjp
def mm(a, b):
    return _dot(a, b, 1, 0)


mm.defvjp(lambda a, b: (_dot(a, b, 1, 0), (a, b)),
          lambda r, g: (_dot(g, r[1], 1, 1), _dot(r[0], g, 0, 0)))


@jax.custom_vjp
def mm_nt(a, b):
    return _dot(a, b, 1, 1)


mm_nt.defvjp(lambda a, b: (_dot(a, b, 1, 1), (a, b)),
             lambda r, g: (_dot(g, r[1], 1, 0), _dot(g, r[0], 0, 0)))


@jax.custom_vjp
def mm_tn(a, b):
    return _dot(a, b, 0, 0)


mm_tn.defvjp(lambda a, b: (_dot(a, b, 0, 0), (a, b)),
             lambda r, g: (_dot(r[1], g, 1, 1), _dot(r[0], g, 1, 0)))


def _dot32(a, b):
    return lax.dot_general(a, b, (((1,), (0,)), ((), ())), precision=lax.Precision.HIGHEST,
                           preferred_element_type=F32)


def _cumsum_fn(rev):
    def tri(transpose):
        r = lax.broadcasted_iota(jnp.int32, (Q, Q), 0)
        c = lax.broadcasted_iota(jnp.int32, (Q, Q), 1)
        keep = (r >= c) if (rev == transpose) else (r <= c)
        return jnp.where(keep, 1.0, 0.0).astype(F32)

    @jax.custom_vjp
    def cums(a):
        return _dot32(tri(False), a)

    cums.defvjp(lambda a: (_dot32(tri(False), a), None), lambda _, g: (_dot32(tri(True), g),))
    return cums


def _cols(v, k):
    w = v.shape[1] // k
    return tuple(v[:, w * i:w * (i + 1)] for i in range(k))


def _splitter(k):
    @jax.custom_vjp
    def split(v):
        return _cols(v, k)

    @jax.custom_vjp
    def concat(ps):
        return jnp.concatenate(ps, axis=1)

    split.defvjp(lambda v: (_cols(v, k), None), lambda _, g: (jnp.concatenate(g, axis=1),))
    concat.defvjp(lambda ps: (jnp.concatenate(ps, axis=1), None), lambda _, g: (_cols(g, k),))
    return split, concat


split2, _ = _splitter(2)
split4, _ = _splitter(4)
split8, concat8 = _splitter(8)


def _ln(x, g, b):
    mu = jnp.mean(x, axis=-1, keepdims=True)
    xc = x - mu
    var = jnp.mean(xc * xc, axis=-1, keepdims=True)
    return xc * lax.rsqrt(var + LN_EPS) * g + b


def _silu(x):
    return x * jax.nn.sigmoid(x)


def _gelu(x):
    return 0.5 * x * (1.0 + jnp.tanh(0.7978845608028654 * (x + 0.044715 * (x * x * x))))


def _xspec(T, w, col, roff):
    return pl.BlockSpec((T, w), lambda i, col=col, roff=roff: (jnp.maximum(i + roff, 0), col))


def _pspec(p, sel):
    if sel is None:
        return pl.BlockSpec(p.shape, lambda i, n=p.ndim: (0,) * n)
    return pl.BlockSpec((1,) + p.shape[1:], lambda i, n=p.ndim: (sel(i),) + (0,) * (n - 1))


def _out_plumbing(outs, T, args, in_specs):
    shapes, specs, aliases = [], [], {}
    for k, o in enumerate(outs):
        if o[0] == 'new':
            _, rows, w, roff = o[:4]
            shapes.append(jax.ShapeDtypeStruct((rows, w), o[4] if len(o) > 4 else F32))
            specs.append(_xspec(T, w, 0, roff))
        elif o[0] == 'acc':
            shapes.append(jax.ShapeDtypeStruct(o[1], F32))
            specs.append(pl.BlockSpec(o[1], lambda i, n=len(o[1]): (0,) * n))
        elif o[0] == 'part':
            _, rows, wtot, w, col, roff, dtype = o
            shapes.append(jax.ShapeDtypeStruct((rows, wtot), dtype))
            specs.append(_xspec(T, w, col, roff))
        else:
            _, arr, w, col, roff = o
            aliases[len(args)] = k
            args.append(arr)
            in_specs.append(pl.BlockSpec(memory_space=pl.ANY))
            shapes.append(jax.ShapeDtypeStruct(arr.shape, arr.dtype))
            specs.append(_xspec(T, w, col, roff))
    return shapes, specs, aliases


def stage_fwd(name, f, T, n, xs, ps, outs):
    nx, npar = len(xs), len(ps)
    args = [x[0] for x in xs] + [p[0] for p in ps]
    in_specs = [_xspec(T, w, col, roff) for (_, w, col, roff) in xs] + [_pspec(p, sel) for (p, sel) in ps]
    n_in = len(args)
    shapes, specs, aliases = _out_plumbing(outs, T, args, in_specs)
    n_all_in = len(args)

    def body(*refs):
        i = pl.program_id(0)
        xv = [r[...] for r in refs[:nx]]
        pv = [r[...] if ps[k][1] is None else r[0] for k, r in enumerate(refs[nx:n_in])]
        res = f(*xv, *pv)
        for k, o_ref in enumerate(refs[n_all_in:]):
            if outs[k][0] == 'acc':
                @pl.when(i == 0)
                def _(o_ref=o_ref, v=res[k]):
                    o_ref[...] = v

                @pl.when(i > 0)
                def _(o_ref=o_ref, v=res[k]):
                    o_ref[...] += v
            else:
                o_ref[...] = res[k].astype(o_ref.dtype)

    return pl.pallas_call(body, name=name, grid=(n,), in_specs=in_specs, out_specs=specs, out_shape=shapes,
                          input_output_aliases=aliases, compiler_params=_cp("arbitrary"))(*args)


def stage_bwd(name, f, T, n, xs, ps, cts, dxs, dps, primal=()):
    nx, npar = len(xs), len(ps)
    args = [x[0] for x in xs] + [p[0] for p in ps]
    in_specs = [_xspec(T, w, col, roff) for (_, w, col, roff) in xs] + [_pspec(p, sel) for (p, sel) in ps]
    ct_arrs = [c for c in cts if isinstance(c, tuple)]
    for (a, w, col, roff) in ct_arrs:
        args.append(a)
        in_specs.append(_xspec(T, w, col, roff))
    n_in = len(args)
    outs, out_of = [], []
    for k, o in enumerate(dxs):
        if o is not None:
            outs.append(o)
            out_of.append(('x', k))
    for k, want in enumerate(dps):
        if want:
            p, sel = ps[k]
            outs.append(('acc', p.shape))
            out_of.append(('p', k))
    for k, shape in primal:
        outs.append(('acc', shape))
        out_of.append(('r', k))
    shapes, specs, aliases = _out_plumbing(outs, T, args, in_specs)
    for j, (kind, k) in enumerate(out_of):
        if kind == 'p' and ps[k][1] is not None:
            p, sel = ps[k]
            specs[j] = pl.BlockSpec((1,) + p.shape[1:], lambda i, n=p.ndim, sel=sel: (sel(i),) + (0,) * (n - 1))
    n_all_in = len(args)

    def body(*refs):
        i = pl.program_id(0)
        xv = [r[...] for r in refs[:nx]]
        pv = [r[...] if ps[k][1] is None else r[0] for k, r in enumerate(refs[nx:nx + npar])]
        res, vjp_fn = jax.vjp(f, *xv, *pv)
        ctv, q = [], nx + npar
        for k, c in enumerate(cts):
            if c is None:
                ctv.append(jnp.zeros_like(res[k]))
            elif isinstance(c, tuple):
                v = refs[q][...].astype(res[k].dtype)
                if c[3] < 0:
                    v = v * (i + c[3] >= 0).astype(F32)
                ctv.append(v)
                q += 1
            else:
                ctv.append(jnp.full_like(res[k], c))
        grads = vjp_fn(tuple(ctv))
        for j, o_ref in enumerate(refs[n_all_in:]):
            kind, k = out_of[j]
            if kind == 'x':
                o_ref[...] = grads[k].astype(o_ref.dtype)
            else:
                g = res[k] if kind == 'r' else grads[nx + k]
                sel = None if kind == 'r' else ps[k][1]
                if sel is None:
                    first = i == 0
                    tgt = o_ref
                else:
                    first = jnp.logical_or(i == 0, sel(i) != sel(jnp.maximum(i - 1, 0)))
                    tgt = o_ref.at[0]

                @pl.when(first)
                def _(tgt=tgt, g=g):
                    tgt[...] = g

                @pl.when(jnp.logical_not(first))
                def _(tgt=tgt, g=g):
                    tgt[...] += g

    return pl.pallas_call(body, name=name, grid=(n,), in_specs=in_specs, out_specs=specs, out_shape=shapes,
                          input_output_aliases=aliases, compiler_params=_cp("arbitrary"))(*args)


_CONTRACT = {'nn': (1, 0), 'nt': (1, 1), 'tn': (0, 0)}


def matmul(name, a, b, mode, tm, tn, tk, out_dtype=F32, add=None):
    if mode == 'nn':
        (M, K), (_, N) = a.shape, b.shape
    elif mode == 'nt':
        (M, K), (N, _) = a.shape, b.shape
    else:
        (K, M), (_, N) = a.shape, b.shape
    assert M % tm == 0 and N % tn == 0 and K % tk == 0, (name, M, N, K, tm, tn, tk)
    a_spec = (pl.BlockSpec((tk, tm), lambda j, i, k: (k, i)) if mode == 'tn'
              else pl.BlockSpec((tm, tk), lambda j, i, k: (i, k)))
    b_spec = (pl.BlockSpec((tn, tk), lambda j, i, k: (j, k)) if mode == 'nt'
              else pl.BlockSpec((tk, tn), lambda j, i, k: (k, j)))
    o_spec = pl.BlockSpec((tm, tn), lambda j, i, k: (i, j))
    return matmul_call(name, (N // tn, M // tm, K // tk), a, a_spec, b, b_spec, (M, N), o_spec, (tm, tn), mode,
                       out_dtype, add)


def matmul_call(name, grid, a, a_spec, b, b_spec, out_shape, o_spec, tile, mode, out_dtype=F32, add=None):
    tm, tn = tile
    nk = grid[2]
    ca, cb = _CONTRACT[mode]
    args, in_specs = [a, b], [a_spec, b_spec]
    if add is not None:
        args.append(add)
        in_specs.append(o_spec)

    def body(*refs):
        a_ref, b_ref = refs[0], refs[1]
        o_ref, acc = refs[-2], refs[-1]
        k = pl.program_id(2)
        if nk == 1:
            p = _dot(a_ref[...], b_ref[...], ca, cb)
            o_ref[...] = (p + refs[2][...] if add is not None else p).astype(out_dtype)
            return

        @pl.when(k == 0)
        def _():
            acc[...] = refs[2][...] if add is not None else jnp.zeros_like(acc)

        acc[...] += _dot(a_ref[...], b_ref[...], ca, cb)

        @pl.when(k == nk - 1)
        def _():
            o_ref[...] = acc[...].astype(out_dtype)

    return pl.pallas_call(body, name=name, grid=grid, in_specs=in_specs, out_specs=o_spec,
                          out_shape=jax.ShapeDtypeStruct(out_shape, out_dtype),
                          scratch_shapes=[pltpu.VMEM((tm, tn) if nk > 1 else (8, 128), F32)],
                          compiler_params=_cp("arbitrary", "arbitrary", "arbitrary"))(*args)


NS, WS = 4, 704


def _resident(name, M, tm, rows, weight, out_shape, out_block, out_map, step, add=None):
    args = [rows[0], weight] + ([] if add is None else [add])
    in_specs = [pl.BlockSpec(rows[1], rows[2]), pl.BlockSpec(weight.shape, lambda i, n=weight.ndim: (0,) * n)]
    if add is not None:
        in_specs.append(pl.BlockSpec(out_block, out_map))
    return pl.pallas_call(step, name=name, grid=(M // tm,), in_specs=in_specs, out_specs=pl.BlockSpec(out_block, out_map),
                          out_shape=jax.ShapeDtypeStruct(out_shape, F32), compiler_params=_cp("arbitrary"))(*args)


def ffn_in_fwd(name, h, w1, w3, tm):
    M = h.shape[0]

    def step(h_ref, w1_ref, w3_ref, a1_ref, a3_ref, act_ref):
        for s in range(NS):
            a1 = _dot(h_ref[...], w1_ref[s], 1, 0)
            a3 = _dot(h_ref[...], w3_ref[s], 1, 0)
            a1_ref[s] = a1.astype(a1_ref.dtype)
            a3_ref[s] = a3.astype(a3_ref.dtype)
            act_ref[s] = (_silu(a1) * a3).astype(act_ref.dtype)

    wspec = pl.BlockSpec((NS, D, WS), lambda i: (0, 0, 0))
    ospec = pl.BlockSpec((NS, tm, WS), lambda i: (0, i, 0))
    return pl.pallas_call(
        step, name=name, grid=(M // tm,), in_specs=[pl.BlockSpec((tm, D), lambda i: (i, 0)), wspec, wspec],
        out_specs=[ospec, ospec, ospec],
        out_shape=[jax.ShapeDtypeStruct((NS, M, WS), BF16)] * 3, compiler_params=_cp("arbitrary"))(h, w1, w3)


def ffn_out_bwd_x(name, dff, w2, a1, a3, tm):
    M = dff.shape[0]

    def step(d_ref, w_ref, a1_ref, a3_ref, da1_ref, da3_ref):
        for s in range(NS):
            dact = _dot(d_ref[...], w_ref[s * WS:(s + 1) * WS, :], 1, 1)
            a1 = a1_ref[s].astype(F32)
            sig = jax.nn.sigmoid(a1)
            da3_ref[s] = (dact * (a1 * sig)).astype(da3_ref.dtype)
            da1_ref[s] = (dact * a3_ref[s].astype(F32) * (sig * (1.0 + a1 * (1.0 - sig)))).astype(da1_ref.dtype)

    aspec = pl.BlockSpec((NS, tm, WS), lambda i: (0, i, 0))
    return pl.pallas_call(
        step, name=name, grid=(M // tm,),
        in_specs=[pl.BlockSpec((tm, D), lambda i: (i, 0)), pl.BlockSpec(w2.shape, lambda i: (0, 0)), aspec, aspec],
        out_specs=[aspec, aspec],
        out_shape=[jax.ShapeDtypeStruct((NS, M, WS), BF16)] * 2, compiler_params=_cp("arbitrary"))(dff, w2, a1, a3)


def ff_in_bwd_x(name, da3, w3, tm, add=None):
    M = da3.shape[1]

    def step(*refs):
        d_ref, w_ref, o_ref = refs[0], refs[1], refs[-1]
        acc = _dot(d_ref[0], w_ref[0], 1, 1)
        for s in range(1, NS):
            acc = acc + _dot(d_ref[s], w_ref[s], 1, 1)
        o_ref[...] = acc if add is None else acc + refs[2][...]

    return _resident(name, M, tm, (da3, (NS, tm, WS), lambda i: (0, i, 0)), w3, (M, D), (tm, D), lambda i: (i, 0), step, add)


def ff_in_bwd_w(name, h, da3, tk):
    M = h.shape[0]

    def step(h_ref, d_ref, acc):
        for s in range(NS):
            acc[s] += _dot(h_ref[...], d_ref[s], 0, 0)

    return _token_sum(name, M // tk, [pl.BlockSpec((tk, D), lambda k: (k, 0)), pl.BlockSpec((NS, tk, WS), lambda k: (0, k, 0))],
                      (NS, D, WS), (NS, D, WS), step, (h, da3))


def ff_out_fwd(name, act3, w2, tm):
    M = act3.shape[1]

    def step(a_ref, w_ref, o_ref):
        acc = _dot(a_ref[0], w_ref[0:WS, :], 1, 0)
        for s in range(1, NS):
            acc = acc + _dot(a_ref[s], w_ref[s * WS:(s + 1) * WS, :], 1, 0)
        o_ref[...] = acc

    return _resident(name, M, tm, (act3, (NS, tm, WS), lambda i: (0, i, 0)), w2, (M, D), (tm, D), lambda i: (i, 0), step)


def _token_sum(name, nk, in_specs, out_shape, acc_shape, step, args):
    def body(*refs):
        o_ref, acc = refs[-2], refs[-1]
        k = pl.program_id(0)

        @pl.when(k == 0)
        def _():
            acc[...] = jnp.zeros_like(acc)

        step(*refs[:-2], acc)

        @pl.when(k == nk - 1)
        def _():
            o_ref[...] = acc[...].astype(o_ref.dtype)

    return pl.pallas_call(body, name=name, grid=(nk,), in_specs=in_specs,
                          out_specs=pl.BlockSpec(out_shape, lambda k, n=len(out_shape): (0,) * n),
                          out_shape=jax.ShapeDtypeStruct(out_shape, BF16), scratch_shapes=[pltpu.VMEM(acc_shape, F32)],
                          compiler_params=_cp("arbitrary"))(*args)


def ff_out_bwd_w(name, act3, dff, tk):
    M = dff.shape[0]

    def step(a_ref, d_ref, acc):
        for s in range(NS):
            acc[s * WS:(s + 1) * WS, :] += _dot(a_ref[s], d_ref[...], 0, 0)

    return _token_sum(name, M // tk, [pl.BlockSpec((NS, tk, WS), lambda k: (0, k, 0)), pl.BlockSpec((tk, D), lambda k: (k, 0))],
                      (NS * WS, D), (NS * WS, D), step, (act3, dff))


HALO = 8


def _windows(pad_ref, v):
    n = v.shape[0]
    edge = jnp.zeros((HALO, 128), F32)
    pad_ref[0:HALO, :] = edge
    pad_ref[HALO:HALO + n, :] = v
    pad_ref[HALO + n:2 * HALO + n, :] = edge
    return lambda k: pad_ref[HALO + k - 2:HALO + k - 2 + n, :]


def _conv_pre(tap, w_ref, b_ref):
    acc = tap(0) * w_ref[0:1, :] + b_ref[...]
    for k in range(1, 5):
        acc = acc + tap(k) * w_ref[k:k + 1, :]
    return acc


def conv_fwd(name, proj, conv_w, conv_b, R):
    segs = ((0, LC), (LC, R))

    def body(x_ref, w_ref, b_ref, o_ref, xp):
        for (s, e) in segs:
            pre = _conv_pre(_windows(xp, x_ref[s:e, :]), w_ref, b_ref)
            o_ref[s:e, :] = _silu(pre)

    return pl.pallas_call(
        body, name=name, grid=(12,),
        in_specs=[pl.BlockSpec((R, 128), lambda j: (0, C_XBC // 128 + j)),
                  pl.BlockSpec((8, 128), lambda j: (0, j)), pl.BlockSpec((1, 128), lambda j: (0, j))],
        out_specs=pl.BlockSpec((R, 128), lambda j: (0, j)),
        out_shape=jax.ShapeDtypeStruct((R, 1536), F32), scratch_shapes=[pltpu.VMEM((R - LC + 2 * HALO, 128), F32)],
        compiler_params=_cp("arbitrary"))(proj, conv_w, conv_b)


def conv_bwd(name, proj, conv_w, conv_b, d_f, d_b, d_skip, dproj, R):
    segs = ((0, LC), (LC, R))

    def body(x_ref, w_ref, b_ref, df_ref, db_ref, ds_ref, _, dx_ref, dw_ref, dbias_ref, xp, dp):
        j = pl.program_id(0)
        has_skip = (j < 8).astype(F32)
        dw = [jnp.zeros((1, 128), F32) for _ in range(5)]
        dbias = jnp.zeros((1, 128), F32)
        for (s, e) in segs:
            xw = _windows(xp, x_ref[s:e, :])
            pre = _conv_pre(xw, w_ref, b_ref)
            sig = jax.nn.sigmoid(pre)
            dy = df_ref[s:e, :] + db_ref[s:e, :]
            if s == LC:
                dy = dy + ds_ref[...] * has_skip
            dpre = dy * (sig * (1.0 + pre * (1.0 - sig)))
            dtap = _windows(dp, dpre)
            dx = dtap(4) * w_ref[0:1, :]
            for k in range(1, 5):
                dx = dx + dtap(4 - k) * w_ref[k:k + 1, :]
            for k in range(5):
                dw[k] = dw[k] + jnp.sum(dtap(2) * xw(k), axis=0, keepdims=True)
            dbias = dbias + jnp.sum(dpre, axis=0, keepdims=True)
            dx_ref[s:e, :] = dx.astype(dx_ref.dtype)
        dw_ref[...] = jnp.zeros_like(dw_ref)
        for k in range(5):
            dw_ref[k:k + 1, :] = dw[k]
        dbias_ref[...] = dbias

    pad = pltpu.VMEM((R - LC + 2 * HALO, 128), F32)
    return pl.pallas_call(
        body, name=name, grid=(12,),
        in_specs=[pl.BlockSpec((R, 128), lambda j: (0, C_XBC // 128 + j)),
                  pl.BlockSpec((8, 128), lambda j: (0, j)), pl.BlockSpec((1, 128), lambda j: (0, j)),
                  pl.BlockSpec((R, 128), lambda j: (0, j)), pl.BlockSpec((R, 128), lambda j: (0, j)),
                  pl.BlockSpec((R - LC, 128), lambda j: (0, jnp.minimum(j, 7))),
                  pl.BlockSpec(memory_space=pl.ANY)],
        out_specs=[pl.BlockSpec((R, 128), lambda j: (0, C_XBC // 128 + j)),
                   pl.BlockSpec((8, 128), lambda j: (0, j)), pl.BlockSpec((1, 128), lambda j: (0, j))],
        out_shape=[jax.ShapeDtypeStruct(dproj.shape, dproj.dtype), jax.ShapeDtypeStruct((8, 1536), F32),
                   jax.ShapeDtypeStruct((1, 1536), F32)], scratch_shapes=[pad, pad],
        input_output_aliases={6: 0}, compiler_params=_cp("arbitrary"))(proj, conv_w, conv_b, d_f, d_b, d_skip, dproj)


def _ssd_chunk(rev, dirn):
    cums = _cumsum_fn(rev)

    def f(xs, Bs, Cs, dt, alog, Hs):
        lane = lax.broadcasted_iota(jnp.int32, (1, 128), 1)
        sub = lax.broadcasted_iota(jnp.int32, (Q, 1), 0)
        r = lax.broadcasted_iota(jnp.int32, (Q, Q), 0)
        c = lax.broadcasted_iota(jnp.int32, (Q, Q), 1)
        mask = (r <= c) if rev else (r >= c)
        left = lane < 64
        a = dt * (-jnp.exp(alog))
        s = cums(a)
        sT, dtT = s.T, dt.T
        last_row = (sub == (0 if rev else Q - 1)).astype(F32)
        s_last = jnp.sum(s * last_row, axis=0, keepdims=True)
        G = [mm_nt(Cs[g], Bs[g]) for g in range(2)]
        M, es, wc, ed = [], [], [], []
        for h in range(NH):
            l = 16 * dirn + h
            oh_l = (lane == l).astype(F32)
            oh_s = (sub == l).astype(F32)
            s_col = jnp.sum(s * oh_l, axis=1, keepdims=True)
            dt_col = jnp.sum(dt * oh_l, axis=1, keepdims=True)
            s_row = jnp.sum(sT * oh_s, axis=0, keepdims=True)
            dt_row = jnp.sum(dtT * oh_s, axis=0, keepdims=True)
            sl = jnp.sum(s_last * oh_l, axis=1, keepdims=True)
            seg = jnp.where(mask, s_col - s_row, 0.0)
            lm = jnp.where(mask, jnp.exp(seg), 0.0)
            M.append(G[h // 8] * lm * dt_row)
            es.append(jnp.exp(s_col))
            wc.append(jnp.exp(sl - s_col) * dt_col)
            ed.append(jnp.exp(sl))
        Ys, Hn = [], []
        for j in range(8):
            g = j // 4
            xa = jnp.where(left, xs[j], 0.0)
            xb = jnp.where(left, 0.0, xs[j])
            yd = mm(M[2 * j], xa) + mm(M[2 * j + 1], xb)
            yo = mm(Cs[g], Hs[j]) * jnp.where(left, es[2 * j], es[2 * j + 1])
            Ys.append(yd + yo)
            st = mm_tn(Bs[g], xs[j] * jnp.where(left, wc[2 * j], wc[2 * j + 1]))
            Hn.append(Hs[j] * jnp.where(left, ed[2 * j], ed[2 * j + 1]) + st)
        return Ys, Hn

    return f


def _chunk_of(t, n, rev):
    if not rev:
        return t
    return jnp.where(t < 2, 1 - t, n + 1 - t)


def _cols128(ref, k, lead=()):
    return [ref[lead + (slice(None), slice(128 * j, 128 * (j + 1)))] for j in range(k)]


def ssd_fwd(name, xbc, dts, alog, n, dirs):
    nd = len(dirs)
    chunks = [_ssd_chunk(rev, dirn) for rev, dirn in dirs]

    def body(*refs):
        al_ref = refs[4 * nd]
        for d in range(nd):
            x_ref, b_ref, c_ref, dt_ref = refs[4 * d:4 * d + 4]
            y_ref, hs_ref = refs[4 * nd + 1 + 2 * d:4 * nd + 3 + 2 * d]
            h_scr = refs[4 * nd + 1 + 2 * nd + d]

            @pl.when(pl.program_id(0) == 0)
            def _(h_scr=h_scr):
                h_scr[...] = jnp.zeros_like(h_scr)

            hs_ref[0] = h_scr[...]
            Ys, Hn = chunks[d](_cols128(x_ref, 8), _cols128(b_ref, 2), _cols128(c_ref, 2), dt_ref[...], al_ref[...],
                               _cols128(h_scr, 8))
            for j in range(8):
                y_ref[:, 128 * j:128 * (j + 1)] = Ys[j]
                h_scr[:, 128 * j:128 * (j + 1)] = Hn[j]

    in_specs, out_specs, out_shape, args = [], [], [], []
    for (rev, _), dt in zip(dirs, dts):
        cm = lambda t, rev=rev: _chunk_of(t, n, rev)
        in_specs += [pl.BlockSpec((Q, 1024), lambda t, cm=cm: (cm(t), 0)), pl.BlockSpec((Q, 256), lambda t, cm=cm: (cm(t), 4)),
                     pl.BlockSpec((Q, 256), lambda t, cm=cm: (cm(t), 5)), pl.BlockSpec((Q, 128), lambda t, cm=cm: (cm(t), 0))]
        args += [xbc, xbc, xbc, dt]
        out_specs += [pl.BlockSpec((Q, 1024), lambda t, cm=cm: (cm(t), 0)),
                      pl.BlockSpec((1, Q, 1024), lambda t, cm=cm: (cm(t), 0, 0))]
        out_shape += [jax.ShapeDtypeStruct((n * Q, 1024), F32), jax.ShapeDtypeStruct((n, Q, 1024), F32)]
    res = pl.pallas_call(
        body, name=name, grid=(n,), in_specs=in_specs + [pl.BlockSpec((1, 128), lambda t: (0, 0))],
        out_specs=out_specs, out_shape=out_shape, scratch_shapes=[pltpu.VMEM((Q, 1024), F32)] * nd,
        compiler_params=_cp("arbitrary"))(*args, alog)
    return [res[2 * d:2 * d + 2] for d in range(nd)]


def ssd_bwd(name, xbc, dts, alog, hss, dy, n, dirs):
    nd = len(dirs)
    chunks = [_ssd_chunk(rev, dirn) for rev, dirn in dirs]

    def body(*refs):
        tt = pl.program_id(0)
        al_ref = refs[6 * nd]
        for d, (rev, _) in enumerate(dirs):
            x_ref, b_ref, c_ref, dt_ref, hs_ref, dy_ref = refs[6 * d:6 * d + 6]
            dx_ref, ddt_ref, dal_ref = refs[6 * nd + 1 + 3 * d:6 * nd + 4 + 3 * d]
            dh_scr = refs[6 * nd + 1 + 3 * nd + d]
            ch = _chunk_of(n - 1 - tt, n, rev)

            @pl.when(tt == 0)
            def _(dh_scr=dh_scr):
                dh_scr[...] = jnp.zeros_like(dh_scr)

            live = (ch >= 2).astype(F32)
            dYs = [v * live for v in _cols128(dy_ref, 8)]
            _, vjp_fn = jax.vjp(chunks[d], _cols128(x_ref, 8), _cols128(b_ref, 2), _cols128(c_ref, 2), dt_ref[...],
                                al_ref[...], _cols128(hs_ref, 8, (0,)))
            dxs, dBs, dCs, ddt, dal, dHs = vjp_fn((dYs, _cols128(dh_scr, 8)))
            for j in range(8):
                dx_ref[:, 128 * j:128 * (j + 1)] = dxs[j]
                dh_scr[:, 128 * j:128 * (j + 1)] = dHs[j]
            for g in range(2):
                dx_ref[:, 1024 + 128 * g:1024 + 128 * (g + 1)] = dBs[g]
                dx_ref[:, 1280 + 128 * g:1280 + 128 * (g + 1)] = dCs[g]
            ddt_ref[...] = ddt

            @pl.when(tt == 0)
            def _(dal_ref=dal_ref, dal=dal):
                dal_ref[...] = dal

            @pl.when(tt > 0)
            def _(dal_ref=dal_ref, dal=dal):
                dal_ref[...] += dal

    in_specs, out_specs, out_shape, args = [], [], [], []
    for (rev, _), dt, hs in zip(dirs, dts, hss):
        cm = lambda t, rev=rev: _chunk_of(n - 1 - t, n, rev)
        in_specs += [pl.BlockSpec((Q, 1024), lambda t, cm=cm: (cm(t), 0)), pl.BlockSpec((Q, 256), lambda t, cm=cm: (cm(t), 4)),
                     pl.BlockSpec((Q, 256), lambda t, cm=cm: (cm(t), 5)), pl.BlockSpec((Q, 128), lambda t, cm=cm: (cm(t), 0)),
                     pl.BlockSpec((1, Q, 1024), lambda t, cm=cm: (cm(t), 0, 0)),
                     pl.BlockSpec((Q, 1024), lambda t, cm=cm: (jnp.maximum(cm(t) - 2, 0), 0))]
        args += [xbc, xbc, xbc, dt, hs, dy]
        out_specs += [pl.BlockSpec((Q, 1536), lambda t, cm=cm: (cm(t), 0)), pl.BlockSpec((Q, 128), lambda t, cm=cm: (cm(t), 0)),
                      pl.BlockSpec((1, 128), lambda t: (0, 0))]
        out_shape += [jax.ShapeDtypeStruct((n * Q, 1536), F32), jax.ShapeDtypeStruct((n * Q, 128), F32),
                      jax.ShapeDtypeStruct((1, 128), F32)]
    res = pl.pallas_call(
        body, name=name, grid=(n,), in_specs=in_specs + [pl.BlockSpec((1, 128), lambda t: (0, 0))],
        out_specs=out_specs, out_shape=out_shape, scratch_shapes=[pltpu.VMEM((Q, 1024), F32)] * nd,
        compiler_params=_cp("arbitrary"))(*args, alog)
    return [res[3 * d:3 * d + 3] for d in range(nd)]


def f_norm0(c, x, g0, b0, sc, sh, is_ctx):
    x0 = _ln(jnp.where(is_ctx > 0.5, c, x), g0, b0)
    return x0, x0 * (1.0 + sc) + sh


def f_dt(raw, bias):
    z = split4(raw)[0] + bias
    dt = jnp.maximum(z, 0.0) + jnp.log1p(jnp.exp(-jnp.abs(z)))
    return dt, dt


def f_gated_norm(yf, yb, xs, z, dcol, g):
    h = (yf + yb + xs * dcol) * _silu(z)
    return (h * lax.rsqrt(jnp.mean(h * h, axis=-1, keepdims=True) + LN_EPS) * g,)


def f_gmlp(uv, gmg, gmb, *wb):
    ws, bs = wb[:8], wb[8:]
    u, v = split2(uv)
    vn = split8(_ln(_gelu(v), gmg, gmb))
    mixed = concat8(tuple(mm(ws[g], vn[g]) + bs[g] for g in range(8)))
    return (_gelu(u) * mixed,)


def f_merge(ps, pg, gates, bg):
    gs, gg = split2(jax.nn.sigmoid(gates + bg))
    return (gs * ps + gg * pg,)


def f_res1(x0, out, g1, lg, lb, sc, sh):
    x1 = _ln(ALPHA * x0 + g1 * out, lg, lb)
    return x1, x1 * (1.0 + sc) + sh


def f_res2_loss(x1, ff, tgt, g2, lg, lb):
    x2 = _ln(ALPHA * x1 + g2 * ff, lg, lb)
    e = x2 - tgt
    return (0.5 * jnp.sum(jnp.mean(e * e, axis=-1, keepdims=True), axis=0, keepdims=True),)


def _row_tile(M):
    return 544 if M % 544 == 0 else (512 if M % 512 == 0 else M)


def core(ctx, x, tgt, mod_x, mod_c, X, S):
    L = x.shape[0]
    R = LC + L
    n = R // Q
    T = 256
    nt, ntl = R // T, L // T
    tmR, tmL = _row_tile(R), _row_tile(L)
    tkR = 256 if R % 512 else 512
    tkL = 512 if L % 512 == 0 else 256
    row = lambda v: v.reshape(1, -1)
    mx = [row(mod_x[k]) for k in range(6)]
    mc = [row(mod_c[k]) for k in range(6)]
    sel = lambda i: jnp.minimum(i, 1)
    sc1 = jnp.stack([mc[1], mx[1]])
    sh1 = jnp.stack([mc[0], mx[0]])
    ln0 = [(row(S['ln0_g']), None), (row(S['ln0_b']), None), (sc1, sel), (sh1, sel),
           (jnp.array([1.0, 0.0], F32).reshape(2, 1, 1), sel)]
    x_n0 = [(ctx, D, 0, -nt), (x, D, 0, -1)]

    x0, xm = stage_fwd("norm0_fwd", f_norm0, T, nt, x_n0, ln0, [('new', R, D, 0), ('new', R, D, 0, BF16)])
    w_in = X.w_in()
    proj = matmul("proj_fwd", xm, w_in, 'nn', tmR, PW // 2, 1024)
    conv_w8 = jnp.pad(S['conv_w'], ((0, 3), (0, 0)))
    conv_b = row(S['conv_b'])
    xbc = conv_fwd("conv_fwd", proj, conv_w8, conv_b, R)
    dt_bias = jnp.pad(S['dt_bias'].reshape(1, 32), ((0, 0), (0, 96)))
    alog = jnp.pad(S['a_log'].reshape(1, 32), ((0, 0), (0, 96)))
    x_dt = [(proj, 512, C_DT // 512, 0)]
    dt_f, dt_b = stage_fwd("dt_fwd", f_dt, T, nt, x_dt, [(dt_bias, None)], [('new', R, 128, 0), ('new', R, 128, 0)])
    directions = [(False, 0), (True, 1)]
    (y_f, hs_f), (y_b, hs_b) = ssd_fwd("ssd_fwd", xbc, [dt_f, dt_b], alog, n, directions)
    W = X.rest()
    dcol = jnp.repeat(S['d_skip'][0] + S['d_skip'][1], 64).reshape(1, D)
    x_gn = [(y_f, D, 0, 1), (y_b, D, 0, 1), (xbc, D, 0, 1), (proj, D, C_Z // D, 1)]
    p_gn = [(dcol, None), (row(S['ssd_norm_g']), None)]
    (yn,) = stage_fwd("gnorm_fwd", f_gated_norm, T, ntl, x_gn, p_gn, [('new', L, D, 0, BF16)])
    x_gm = [(proj, 2 * D, C_UV // (2 * D), LC // Q)]
    p_gm = ([(row(S['gm_norm_g']), None), (row(S['gm_norm_b']), None)]
            + [(S['w_spatial'][g], None) for g in range(8)] + [(S['b_spatial'][g].reshape(Q, 1), None) for g in range(8)])
    (y_gm,) = stage_fwd("gmlp_fwd", f_gmlp, Q, L // Q, x_gm, p_gm, [('new', L, D, 0, BF16)])
    p_ssd = matmul("pssd_fwd", yn, W['w_ssd_proj'], 'nn', tmL, 1024, 1024, BF16)
    p_g = matmul("pgm_fwd", y_gm, W['w_gm_proj'], 'nn', tmL, 1024, 1024, BF16)
    x_mg = [(p_ssd, D, 0, 0), (p_g, D, 0, 0), (proj, 2 * D, C_GATE // (2 * D), 1)]
    p_mg = [(row(S['b_gate']), None)]
    (merged,) = stage_fwd("merge_fwd", f_merge, T, ntl, x_mg, p_mg, [('new', L, D, 0, BF16)])
    out = matmul("out_fwd", merged, W['w_out'], 'nn', tmL, 1024, 1024)
    x_r1 = [(x0, D, 0, 1), (out, D, 0, 0)]
    p_r1 = [(mx[2], None), (row(S['ln1_g']), None), (row(S['ln1_b']), None), (mx[4], None), (mx[3], None)]
    x1, hm = stage_fwd("res1_fwd", f_res1, T, ntl, x_r1, p_r1, [('new', L, D, 0), ('new', L, D, 0, BF16)])
    a1, a3, act = ffn_in_fwd("ffn_in_fwd", hm, W['w_ff1'], W['w_ff3'], T)
    ff = ff_out_fwd("ff2_fwd", act, W['w_ff2'], tmL)
    x_r2 = [(x1, D, 0, 0), (ff, D, 0, 0), (tgt, D, 0, 0)]
    p_r2 = [(mx[5], None), (row(S['ln2_g']), None), (row(S['ln2_b']), None)]

    dx1_a, dff, dg2, dl2g, dl2b, loss = stage_bwd(
        "res2_bwd", f_res2_loss, T, ntl, x_r2, p_r2, [1.0],
        [('new', L, D, 0), ('new', L, D, 0, BF16), None], [True, True, True], primal=[(0, (1, 1))])
    da1, da3 = ffn_out_bwd_x("ffn_out_bwd_x", dff, W['w_ff2'], a1, a3, T)
    gw_ff2 = ff_out_bwd_w("ff2_bwd_w", act, dff, tkL)
    dhm = ff_in_bwd_x("ff1_bwd_x", da1, W['w_ff1'], tmL)
    dhm = ff_in_bwd_x("ff3_bwd_x", da3, W['w_ff3'], tmL, add=dhm)
    gw_ff1 = ff_in_bwd_w("ff1_bwd_w", hm, da1, tkL)
    gw_ff3 = ff_in_bwd_w("ff3_bwd_w", hm, da3, tkL)
    X.grads('ffn', {'w_ff2': gw_ff2, 'w_ff1': gw_ff1, 'w_ff3': gw_ff3})
    dx0_a, dout, dg1, dl1g, dl1b, dsc2, dsh2 = stage_bwd(
        "res1_bwd", f_res1, T, ntl, x_r1, p_r1, [(dx1_a, D, 0, 0), (dhm, D, 0, 0)],
        [('new', L, D, 0), ('new', L, D, 0, BF16)], [True] * 5)
    dmerged = matmul("out_bwd_x", dout, W['w_out'], 'nt', tmL, 1024, 1024, BF16)
    gw_out = matmul("out_bwd_w", merged, dout, 'tn', 1024, 1024, tkL, BF16)
    lt, lq = -(LC // T), -(LC // Q)
    x_mg_b = [(p_ssd, D, 0, lt), (p_g, D, 0, lt), (proj, 2 * D, C_GATE // (2 * D), 0)]
    dp_ssd, dp_g, dproj, dbg = stage_bwd(
        "merge_bwd", f_merge, T, nt, x_mg_b, p_mg, [(dmerged, D, 0, lt)],
        [('new', L, D, lt, BF16), ('new', L, D, lt, BF16), ('part', R, PW, 2 * D, C_GATE // (2 * D), 0, BF16)], [True])
    dyn = matmul("pssd_bwd_x", dp_ssd, W['w_ssd_proj'], 'nt', tmL, 1024, 1024)
    gw_ssd = matmul("pssd_bwd_w", yn, dp_ssd, 'tn', 1024, 1024, tkL, BF16)
    dy_gm = matmul("pgm_bwd_x", dp_g, W['w_gm_proj'], 'nt', tmL, 1024, 1024, BF16)
    gw_gm = matmul("pgm_bwd_w", y_gm, dp_g, 'tn', 1024, 1024, tkL, BF16)
    X.grads('proj', {'w_out': gw_out, 'w_ssd_proj': gw_ssd, 'w_gm_proj': gw_gm})
    r_gm = stage_bwd("gmlp_bwd", f_gmlp, Q, n, [(proj, 2 * D, C_UV // (2 * D), 0)], p_gm, [(dy_gm, D, 0, lq)],
                     [('alias', dproj, 2 * D, C_UV // (2 * D), 0)], [True] * 18)
    dproj, dgmg, dgmb, dws, dbs = r_gm[0], r_gm[1], r_gm[2], r_gm[3:11], r_gm[11:19]
    x_gn_b = [(y_f, D, 0, 0), (y_b, D, 0, 0), (xbc, D, 0, 0), (proj, D, C_Z // D, 0)]
    dy, dskipx, dproj, ddcol, dng = stage_bwd(
        "gnorm_bwd", f_gated_norm, T, nt, x_gn_b, p_gn, [(dyn, D, 0, lt)],
        [('new', L, D, lt), None, ('new', L, D, lt), ('alias', dproj, D, C_Z // D, 0)], [True, True])
    (dxbc_f, ddt_f, dal_f), (dxbc_b, ddt_b, dal_b) = ssd_bwd("ssd_bwd", xbc, [dt_f, dt_b], alog, [hs_f, hs_b], dy, n,
                                                             directions)
    dproj, ddtb = stage_bwd("dt_bwd", f_dt, T, nt, x_dt, [(dt_bias, None)],
                            [(ddt_f, 128, 0, 0), (ddt_b, 128, 0, 0)],
                            [('alias', dproj, 512, C_DT // 512, 0)], [True])
    dproj, dcw8, dcb = conv_bwd("conv_bwd", proj, conv_w8, conv_b, dxbc_f, dxbc_b, dskipx, dproj, R)
    gw_in = matmul("proj_bwd_w", xm, dproj, 'tn', 1024, PW // 2, tkR, BF16)
    X.grads('in', {'w_in': gw_in})
    dxm = matmul("proj_bwd_x", dproj, w_in, 'nt', R // 2 if R % 16 == 0 else R, 1024, 1024)
    grad_x, dl0g, dl0b, dsc1, dsh1 = stage_bwd(
        "norm0_bwd", f_norm0, T, nt, x_n0, ln0, [(dx0_a, D, 0, -1), (dxm, D, 0, 0)],
        [None, ('new', L, D, -1)], [True] * 4 + [False])

    zero = jnp.zeros((D,), F32)
    flat = lambda v: v.reshape(-1)
    small = {
        'loss': flat(loss), 'ln0_g': flat(dl0g), 'ln0_b': flat(dl0b),
        'dmod_x': jnp.concatenate([flat(dsh1[1]), flat(dsc1[1]), flat(dg1), flat(dsh2), flat(dsc2), flat(dg2)]),
        'dmod_c': jnp.concatenate([flat(dsh1[0]), flat(dsc1[0]), zero, zero, zero, zero]),
        'conv_w': flat(dcw8[:5]), 'conv_b': flat(dcb), 'dt_bias': flat(ddtb[:, :32]),
        'a_log': flat((dal_f + dal_b)[:, :32]),
        'd_skip': flat(jnp.tile(ddcol.reshape(1, NH, 64).sum(-1), (2, 1))),
        'ssd_norm_g': flat(dng), 'gm_norm_g': flat(dgmg), 'gm_norm_b': flat(dgmb),
        'w_spatial': flat(jnp.stack(dws)), 'b_spatial': flat(jnp.stack(dbs)), 'b_gate': flat(dbg),
        'ln1_g': flat(dl1g), 'ln1_b': flat(dl1b), 'ln2_g': flat(dl2g), 'ln2_b': flat(dl2b),
    }
    return grad_x, small


def _place():
    return lax.axis_index("x"), lax.axis_index("y"), lax.axis_index("c")


def allgather8(name, blk, hbm):
    space = pl.ANY if hbm else pltpu.VMEM

    def body(x_ref, out_ref, send_sems, recv_sems, local_sem):
        x, y, c = _place()
        me, sibling = (x, y, c), (x, y, 1 - c)
        chips = [(1 - x, y), (x, 1 - y), (1 - x, 1 - y)]

        def slot(px, py, pc):
            return out_ref.at[4 * px + 2 * py + pc]

        def copy(k, block, to, src=None):
            return pltpu.make_async_remote_copy(
                src_ref=slot(*block) if src is None else src, dst_ref=slot(*block),
                send_sem=send_sems.at[k], recv_sem=recv_sems.at[k], device_id=to, device_id_type=MESH)

        mine = pltpu.make_async_copy(x_ref, slot(*me), local_sem)
        mine.start()
        first = [copy(0, me, sibling, src=x_ref)]
        first += [copy(1 + j, me, (*chip, c), src=x_ref) for j, chip in enumerate(chips)]
        for cp in first:
            cp.start()
        passed = [copy(4 + j, (*chip, c), sibling) for j, chip in enumerate(chips)]
        for j, chip in enumerate(chips):
            copy(1 + j, (*chip, c), me).wait_recv()
            passed[j].start()
        copy(0, sibling, me).wait_recv()
        for j, chip in enumerate(chips):
            copy(4 + j, (*chip, 1 - c), me).wait_recv()
        for cp in first + passed:
            cp.wait_send()
        mine.wait()

    return pl.pallas_call(
        body, name=name, out_shape=jax.ShapeDtypeStruct((8,) + blk.shape, blk.dtype),
        in_specs=[pl.BlockSpec(memory_space=space)], out_specs=pl.BlockSpec(memory_space=space),
        scratch_shapes=[pltpu.SemaphoreType.DMA((7,)), pltpu.SemaphoreType.DMA((7,)), pltpu.SemaphoreType.DMA],
        compiler_params=pltpu.CompilerParams(vmem_limit_bytes=VMEM_LIMIT_V7X))(blk)


def _peers(place):
    x, y, c = place
    return [((1 - x) if k & 4 else x, (1 - y) if k & 2 else y, (1 - c) if k & 1 else c) for k in range(1, 8)]


def _slot(p):
    return 4 * p[0] + 2 * p[1] + p[2]


def plan_gather(place, srcs, lands):
    remote = [(s, l.at[_slot(place)], to) for s, l in zip(srcs, lands) for to in _peers(place)]
    return remote, [(s, l.at[_slot(place)]) for s, l in zip(srcs, lands)]


def plan_to_owner(place, srcs, lands):
    remote = [(s.at[2 * to[0] + to[1], to[2]], l.at[_slot(place)], to) for s, l in zip(srcs, lands) for to in _peers(place)]
    x, y, c = place
    return remote, [(s.at[2 * x + y, c], l.at[_slot(place)]) for s, l in zip(srcs, lands)]


def sequencer_exchange(name, collective_id, srcs, land_shapes, plan):
    n = len(srcs)
    src_refs = [jax.new_ref(a, memory_space=pltpu.MemorySpace.HBM) for a in srcs]
    land_refs = [jax.empty_ref(s, memory_space=pltpu.MemorySpace.HBM) for s in land_shapes]

    @pl.kernel(mesh=plsc.ScalarSubcoreMesh(axis_name="sequencer", num_cores=1), name=name,
               scratch_types=(pltpu.SemaphoreType.DMA((7 * n,)), pltpu.SemaphoreType.DMA((7 * n,)),
                              pltpu.SemaphoreType.DMA((n,))),
               compiler_params=pltpu.CompilerParams(collective_id=collective_id))
    def launch(send_sems, recv_sems, local_sems):
        place = _place()
        barrier = pltpu.get_barrier_semaphore()
        for to in _peers(place):
            pl.semaphore_signal(barrier, inc=1, device_id=to, device_id_type=MESH)
        pl.semaphore_wait(barrier, 7)
        remote, local = plan(place, src_refs, land_refs)
        mine = [pltpu.make_async_copy(s, d, local_sems.at[a]) for a, (s, d) in enumerate(local)]
        for cp in mine:
            cp.start()
        cps = [pltpu.make_async_remote_copy(src_ref=s, dst_ref=d, send_sem=send_sems.at[k], recv_sem=recv_sems.at[k],
                                            device_id=to, device_id_type=MESH) for k, (s, d, to) in enumerate(remote)]
        for cp in cps:
            cp.start()
        for cp in mine:
            cp.wait()
        for cp in cps:
            cp.wait()

    launch()
    return land_refs


def sibling_pair(name, hs):
    n = len(hs)

    def body(*refs):
        ins, outs = refs[:n], refs[n:2 * n]
        send_sems, recv_sems = refs[2 * n:]
        x, y, c = _place()
        cps = [pltpu.make_async_remote_copy(src_ref=outs[a].at[c], dst_ref=outs[a].at[c], send_sem=send_sems.at[a],
                                            recv_sem=recv_sems.at[a], device_id=(x, y, 1 - c), device_id_type=MESH)
               for a in range(n)]
        for cp in cps:
            cp.start()
        for a in range(n):
            pltpu.make_async_remote_copy(src_ref=outs[a].at[1 - c], dst_ref=outs[a].at[1 - c], send_sem=send_sems.at[a],
                                         recv_sem=recv_sems.at[a], device_id=(x, y, 1 - c),
                                         device_id_type=MESH).wait_recv()
        for cp in cps:
            cp.wait_send()

    any_spec = pl.BlockSpec(memory_space=pl.ANY)
    return pl.pallas_call(
        body, name=name, out_shape=[jax.ShapeDtypeStruct(h.shape, h.dtype) for h in hs],
        in_specs=[any_spec] * n, out_specs=[any_spec] * n, input_output_aliases={a: a for a in range(n)},
        scratch_shapes=[pltpu.SemaphoreType.DMA((n,)), pltpu.SemaphoreType.DMA((n,))])(*hs)


def owner_sum(name, land):
    _, r, w = land.shape
    T = r // 2

    def body(_, l_ref, o_ref):
        acc = l_ref[0].astype(F32)
        for j in range(1, 8):
            acc = acc + l_ref[j].astype(F32)
        o_ref[...] = acc

    grid_spec = pltpu.PrefetchScalarGridSpec(
        num_scalar_prefetch=1, grid=(2,),
        in_specs=[pl.BlockSpec((8, T, w), lambda i, at: (0, i, 0))],
        out_specs=pl.BlockSpec((None, T, w), lambda i, at: (at[0], i, 0)))
    at = jnp.stack([lax.axis_index("c")]).astype(jnp.int32)
    return pl.pallas_call(body, name=name, grid_spec=grid_spec, out_shape=jax.ShapeDtypeStruct((2, r, w), F32),
                          compiler_params=_cp("arbitrary"))(at, land)


W_IN_RUNS = ((0, 2, 1296, 376), (376, 3, 0, 1672), (2048, 1, 920, 752), (2800, 2, 0, 1296), (4096, 0, 0, 1024),
             (5120, 0, 1024, 648), (5768, 1, 0, 920))


def w_in_to_padded(name, g4):
    T = 128

    def body(g_ref, o_ref):
        o_ref[:, D_PROJ:PW] = jnp.zeros((T, PW - D_PROJ), o_ref.dtype)
        for (a, s, j0, w) in W_IN_RUNS:
            o_ref[:, a:a + w] = g_ref[s, :, j0:j0 + w]

    return pl.pallas_call(body, name=name, grid=(D // T,), in_specs=[pl.BlockSpec((4, T, 1672), lambda i: (0, i, 0))],
                          out_specs=pl.BlockSpec((T, PW), lambda i: (i, 0)),
                          out_shape=jax.ShapeDtypeStruct((D, PW), g4.dtype), compiler_params=_cp("arbitrary"))(g4)


def w_in_from_padded(name, gp):
    T = 128

    def body(g_ref, o_ref):
        for (a, s, j0, w) in W_IN_RUNS:
            o_ref[s, :, j0:j0 + w] = g_ref[:, a:a + w]

    return pl.pallas_call(body, name=name, grid=(D // T,), in_specs=[pl.BlockSpec((T, PW), lambda i: (i, 0))],
                          out_specs=pl.BlockSpec((4, T, 1672), lambda i: (0, i, 0)),
                          out_shape=jax.ShapeDtypeStruct((4, D, 1672), gp.dtype), compiler_params=_cp("arbitrary"))(gp)


def sum_devices(name, g):
    def body(g_ref, o_ref):
        acc = g_ref[0]
        for k in range(1, 8):
            acc = acc + g_ref[k]
        o_ref[...] = acc

    return pl.pallas_call(body, name=name, out_shape=jax.ShapeDtypeStruct(g.shape[1:], F32),
                          compiler_params=pltpu.CompilerParams(vmem_limit_bytes=VMEM_LIMIT_V7X))(g)


def adamw(name, w, g, m, v, T):
    r, wd = w.shape
    c1 = 1.0 - ADAM_B1 ** ADAM_STEP
    c2 = 1.0 - ADAM_B2 ** ADAM_STEP

    def body(w_ref, g_ref, m_ref, v_ref, d_ref, mo_ref, vo_ref):
        gv = g_ref[...]
        mn = ADAM_B1 * m_ref[...] + (1.0 - ADAM_B1) * gv
        vn = ADAM_B2 * v_ref[...] + (1.0 - ADAM_B2) * (gv * gv)
        d_ref[...] = -ADAM_LR * ((mn / c1) / (jnp.sqrt(vn / c2) + ADAM_EPS) + ADAM_WD * w_ref[...])
        mo_ref[...] = mn
        vo_ref[...] = vn

    spec = pl.BlockSpec((T, wd), lambda i: (i, 0))
    return pl.pallas_call(body, name=name, grid=(r // T,), in_specs=[spec] * 4, out_specs=[spec] * 3,
                          out_shape=[jax.ShapeDtypeStruct((r, wd), F32)] * 3, compiler_params=_cp("arbitrary"))(w, g, m, v)


BIG = {'w_in': (1024, 1672), 'w_ssd_proj': (256, 1024), 'w_gm_proj': (256, 1024), 'w_out': (256, 1024),
       'w_ff1': (1024, 704), 'w_ff3': (1024, 704), 'w_ff2': (704, 1024)}


class Flat:
    def __init__(self, segs):
        self.off, o = {}, 0
        for name, size in segs:
            self.off[name] = (o, size)
            o += -(-size // 128) * 128
        self.rows = -(-o // 1024) * 8

    def pack(self, vals):
        parts = []
        for name, (o, size) in self.off.items():
            v = vals[name].reshape(-1).astype(F32)
            parts.append(jnp.pad(v, (0, -(-size // 128) * 128 - size)))
        buf = jnp.concatenate(parts)
        return jnp.pad(buf, (0, self.rows * 128 - buf.shape[0])).reshape(self.rows, 128)

    def get(self, buf, name, shape=None):
        o, size = self.off[name]
        v = buf[o // 128:(o + size + 127) // 128].reshape(-1)[:size]
        return v if shape is None else v.reshape(shape)


PARTIALS = Flat([('loss', 1), ('ln0_g', D), ('ln0_b', D), ('dmod_x', 6 * D), ('dmod_c', 6 * D), ('conv_w', 5 * 1536),
                 ('conv_b', 1536), ('dt_bias', 32), ('a_log', 32), ('d_skip', 32), ('ssd_norm_g', D),
                 ('gm_norm_g', D), ('gm_norm_b', D), ('w_spatial', 8 * Q * Q), ('b_spatial', 8 * Q), ('b_gate', 2 * D),
                 ('ln1_g', D), ('ln1_b', D), ('ln2_g', D), ('ln2_b', D)])

WEIGHTS = ('c_ctx', 'ln0_g', 'ln0_b', 'w_ada', 'b_ada', 'w_in', 'conv_w', 'conv_b', 'dt_bias', 'a_log', 'd_skip',
           'ssd_norm_g', 'gm_norm_g', 'gm_norm_b', 'w_spatial', 'b_spatial', 'b_gate', 'w_ssd_proj', 'w_gm_proj',
           'w_out', 'ln1_g', 'ln1_b', 'w_ff1', 'w_ff3', 'w_ff2', 'ln2_g', 'ln2_b')
BIG_NAMES = tuple(BIG)
SMALL_NAMES = tuple(n for n in WEIGHTS if n not in BIG_NAMES and n != 'w_ada')


def kernel(x, c, ctx, c_ctx, ln0_g, ln0_b, w_ada, b_ada, w_in, conv_w, conv_b, dt_bias, a_log, d_skip, ssd_norm_g, gm_norm_g, gm_norm_b, w_spatial, b_spatial, b_gate, w_ssd_proj, w_gm_proj, w_out, ln1_g, ln1_b, w_ff1, w_ff3, w_ff2, ln2_g, ln2_b, loss_target, m_c_ctx, m_ln0_g, m_ln0_b, m_w_ada, m_b_ada, m_w_in, m_conv_w, m_conv_b, m_dt_bias, m_a_log, m_d_skip, m_ssd_norm_g, m_gm_norm_g, m_gm_norm_b, m_w_spatial, m_b_spatial, m_b_gate, m_w_ssd_proj, m_w_gm_proj, m_w_out, m_ln1_g, m_ln1_b, m_w_ff1, m_w_ff3, m_w_ff2, m_ln2_g, m_ln2_b, v_c_ctx, v_ln0_g, v_ln0_b, v_w_ada, v_b_ada, v_w_in, v_conv_w, v_conv_b, v_dt_bias, v_a_log, v_d_skip, v_ssd_norm_g, v_gm_norm_g, v_gm_norm_b, v_w_spatial, v_b_spatial, v_b_gate, v_w_ssd_proj, v_w_gm_proj, v_w_out, v_ln1_g, v_ln1_b, v_w_ff1, v_w_ff3, v_w_ff2, v_ln2_g, v_ln2_b):
    wts = dict(c_ctx=c_ctx, ln0_g=ln0_g, ln0_b=ln0_b, w_ada=w_ada, b_ada=b_ada, w_in=w_in, conv_w=conv_w, conv_b=conv_b,
               dt_bias=dt_bias, a_log=a_log, d_skip=d_skip, ssd_norm_g=ssd_norm_g, gm_norm_g=gm_norm_g,
               gm_norm_b=gm_norm_b, w_spatial=w_spatial, b_spatial=b_spatial, b_gate=b_gate, w_ssd_proj=w_ssd_proj,
               w_gm_proj=w_gm_proj, w_out=w_out, ln1_g=ln1_g, ln1_b=ln1_b, w_ff1=w_ff1, w_ff3=w_ff3, w_ff2=w_ff2,
               ln2_g=ln2_g, ln2_b=ln2_b)
    ms = dict(zip(WEIGHTS, (m_c_ctx, m_ln0_g, m_ln0_b, m_w_ada, m_b_ada, m_w_in, m_conv_w, m_conv_b, m_dt_bias, m_a_log,
                            m_d_skip, m_ssd_norm_g, m_gm_norm_g, m_gm_norm_b, m_w_spatial, m_b_spatial, m_b_gate,
                            m_w_ssd_proj, m_w_gm_proj, m_w_out, m_ln1_g, m_ln1_b, m_w_ff1, m_w_ff3, m_w_ff2, m_ln2_g,
                            m_ln2_b)))
    vs = dict(zip(WEIGHTS, (v_c_ctx, v_ln0_g, v_ln0_b, v_w_ada, v_b_ada, v_w_in, v_conv_w, v_conv_b, v_dt_bias, v_a_log,
                            v_d_skip, v_ssd_norm_g, v_gm_norm_g, v_gm_norm_b, v_w_spatial, v_b_spatial, v_b_gate,
                            v_w_ssd_proj, v_w_gm_proj, v_w_out, v_ln1_g, v_ln1_b, v_w_ff1, v_w_ff3, v_w_ff2, v_ln2_g,
                            v_ln2_b)))
    px, py, pc = _place()
    shard = 2 * px + py
    dev = 2 * shard + pc
    take = lambda a, i, axis=0: lax.dynamic_index_in_dim(a, i, axis, keepdims=False)

    half = lambda n: take(wts[n][0].reshape(2, BIG[n][0] // 2, BIG[n][1]), pc).astype(BF16)

    pre = jnp.concatenate([c, jnp.pad(conv_w[0], ((0, 0), (0, D - 384))), jnp.zeros((2, D), F32)], axis=0)
    pre = allgather8("gather_cond", pre, False)
    conv_w_full = pre[0::2, 1:6, :384].transpose(1, 0, 2).reshape(5, 1536)
    a16 = jnp.concatenate([_silu(pre[:, 0, :]), _silu(c_ctx)[None], jnp.zeros((7, D), F32)], axis=0)
    mod = matmul("ada_fwd", a16, w_ada[0], 'nn', 16, 512, 1024)
    mod = mod + lax.dynamic_slice_in_dim(b_ada[0], shard * 1536, 1536)[None]
    mod = allgather8("gather_mod", mod, False)
    mod = jnp.concatenate([mod[0], mod[2], mod[4], mod[6]], axis=1)
    mod_x = take(mod, dev).reshape(6, D)
    mod_c = mod[8].reshape(6, D)

    def full(n, blocks):
        r, w = BIG[n]
        return blocks.reshape(4, r, w) if w != D else blocks.reshape(4 * r, w)

    class Exchanges:
        rest_names = BIG_NAMES[1:]

        def __init__(self):
            self.pending = []

        def w_in(self):
            blocks = allgather8("gather_w_in", half('w_in'), True)
            w = w_in_to_padded("w_in_layout", full('w_in', blocks))
            halves = [half(n) for n in self.rest_names]
            halves[0], _ = lax.optimization_barrier((halves[0], blocks))
            lands = [jax.ShapeDtypeStruct((8,) + h.shape, BF16) for h in halves]
            self.rest_refs = sequencer_exchange("gather_rest", 1, halves, lands, plan_gather)
            return w

        def rest(self):
            return {n: full(n, r[...]) for n, r in zip(self.rest_names, self.rest_refs)}

        def grads(self, group, gs):
            if group == 'in':
                gs = {'w_in': w_in_from_padded("w_in_grad_layout", gs['w_in'])}
            blocks = [g.reshape(4, 2, BIG[n][0] // 2, BIG[n][1]) for n, g in gs.items()]
            lands = [jax.ShapeDtypeStruct((8,) + b.shape[2:], BF16) for b in blocks]
            refs = sequencer_exchange("grads_" + group, 2 + len(self.pending), blocks, lands, plan_to_owner)
            self.pending.append((tuple(gs), refs))

        def finish(self):
            names, halves = [], []
            for ns, refs in self.pending:
                names += ns
                halves += [owner_sum("grads_sum_" + n, r[...]) for n, r in zip(ns, refs)]
            return {n: h.reshape(BIG[n]) for n, h in zip(names, sibling_pair("grads_halves", halves))}

    S = dict(ln0_g=ln0_g, ln0_b=ln0_b, conv_w=conv_w_full, conv_b=conv_b[0], dt_bias=dt_bias[0], a_log=a_log[0],
             d_skip=d_skip[0], ssd_norm_g=ssd_norm_g[0], gm_norm_g=gm_norm_g[0], gm_norm_b=gm_norm_b[0],
             w_spatial=w_spatial[0], b_spatial=b_spatial[0], b_gate=b_gate[0], ln1_g=ln1_g[0], ln1_b=ln1_b[0],
             ln2_g=ln2_g[0], ln2_b=ln2_b[0])
    exchanges = Exchanges()
    grad_x, gsmall = core(ctx[0], x[0], loss_target[0], mod_x, mod_c, exchanges, S)

    parts = allgather8("gather_partials", PARTIALS.pack(gsmall), False)
    tot = sum_devices("partials_sum", parts)
    g_shards = exchanges.finish()
    g = {n: PARTIALS.get(tot, n) for n in ('ln0_g', 'ln0_b', 'conv_b', 'dt_bias', 'a_log', 'd_skip', 'ssd_norm_g',
                                           'gm_norm_g', 'gm_norm_b', 'w_spatial', 'b_spatial', 'b_gate', 'ln1_g',
                                           'ln1_b', 'ln2_g', 'ln2_b')}
    loss = PARTIALS.get(tot, 'loss', ())
    dmod_c = PARTIALS.get(tot, 'dmod_c')
    g['b_ada'] = PARTIALS.get(tot, 'dmod_x') + dmod_c
    g['conv_w'] = lax.dynamic_slice_in_dim(PARTIALS.get(tot, 'conv_w', (5, 1536)), shard * 384, 384, axis=1)
    o, size = PARTIALS.off['dmod_x']
    dmod_rows = parts[:, o // 128:(o + size) // 128].reshape(8, size)
    dm = jnp.concatenate([dmod_rows, dmod_c[None], jnp.zeros((7, 6 * D), F32)], axis=0)
    dm = lax.dynamic_slice_in_dim(dm, shard * 1536, 1536, axis=1)
    g['w_ada'] = matmul("ada_bwd_w", a16, dm, 'tn', 1024, 512, 16)
    dm_c = jnp.concatenate([dm[8:9], jnp.zeros((15, 1536), F32)], axis=0)
    dc = matmul("ada_bwd_c", dm_c, w_ada[0], 'nt', 16, 1024, 512)
    dc = allgather8("gather_dcctx", dc, False)[:, 0, :]
    dc = ((dc[0] + dc[2]) + dc[4]) + dc[6]
    sg = jax.nn.sigmoid(c_ctx)
    g['c_ctx'] = dc * (sg * (1.0 + c_ctx * (1.0 - sg)))
    for n in BIG_NAMES:
        g[n] = g_shards[n]

    delta, new_m, new_v = {}, {}, {}
    for n in BIG_NAMES + ('w_ada',):
        r, w = wts[n].shape[1:]
        if w % 128:
            T = max(t for t in range(8, 257, 8) if w % t == 0)
            d_, m_, v_ = adamw("adamw_" + n, wts[n][0].T, g[n].T, ms[n][0].T, vs[n][0].T, T)
            delta[n], new_m[n], new_v[n] = d_.T, m_.T, v_.T
        else:
            T = 352 if n == 'w_ff2' else 256
            delta[n], new_m[n], new_v[n] = adamw("adamw_" + n, wts[n][0], g[n], ms[n][0], vs[n][0], T)
    lay = Flat([(n, wts[n].size) for n in SMALL_NAMES])
    d_, m_, v_ = adamw("adamw_small", lay.pack(wts), lay.pack(g), lay.pack(ms), lay.pack(vs), lay.rows)
    for n in SMALL_NAMES:
        delta[n], new_m[n], new_v[n] = (lay.get(b, n) for b in (d_, m_, v_))

    shp = lambda d: [d[n].reshape(wts[n].shape) for n in WEIGHTS]
    return (loss, grad_x[None], *shp(g), *shp(delta), *shp(new_m), *shp(new_v))
```

```python
import functools

import jax
import jax.numpy as jnp
from jax import lax
from jax.experimental import pallas as pl
from jax.experimental.pallas import tpu as pltpu
from jax.experimental.pallas import tpu_sc as plsc

F32 = jnp.float32
BF16 = jnp.bfloat16
MESH = pl.DeviceIdType.MESH

VMEM_LIMIT_V7X = 56 * 1024 * 1024

D = 1024
LC = 256
Q = 128
NH = 16
D_FF = 2816
LN_EPS = 1e-5
ALPHA = 2.0 ** 0.25

PW = 7168
C_GATE, C_UV, C_Z, C_XBC, C_DT = 0, 2048, 4096, 5120, 6656
D_PROJ = 6688

ADAM_LR, ADAM_B1, ADAM_B2, ADAM_EPS, ADAM_WD, ADAM_STEP = 0.001, 0.9, 0.999, 1e-08, 0.01, 10


def _cp(*sem):
    return pltpu.CompilerParams(dimension_semantics=sem, vmem_limit_bytes=VMEM_LIMIT_V7X)


def _dot(a, b, ca, cb):
    return lax.dot_general(a.astype(BF16), b.astype(BF16), (((ca,), (cb,)), ((), ())),
                           preferred_element_type=F32)


@jax.custom_vjp
def mm(a, b):
    return _dot(a, b, 1, 0)


mm.defvjp(lambda a, b: (_dot(a, b, 1, 0), (a, b)),
          lambda r, g: (_dot(g, r[1], 1, 1), _dot(r[0], g, 0, 0)))


@jax.custom_vjp
def mm_nt(a, b):
    return _dot(a, b, 1, 1)


mm_nt.defvjp(lambda a, b: (_dot(a, b, 1, 1), (a, b)),
             lambda r, g: (_dot(g, r[1], 1, 0), _dot(g, r[0], 0, 0)))


@jax.custom_vjp
def mm_tn(a, b):
    return _dot(a, b, 0, 0)


mm_tn.defvjp(lambda a, b: (_dot(a, b, 0, 0), (a, b)),
             lambda r, g: (_dot(r[1], g, 1, 1), _dot(r[0], g, 1, 0)))


def _dot32(a, b):
    return lax.dot_general(a, b, (((1,), (0,)), ((), ())), precision=lax.Precision.HIGHEST,
                           preferred_element_type=F32)


def _cumsum_fn(rev):
    def tri(transpose):
        r = lax.broadcasted_iota(jnp.int32, (Q, Q), 0)
        c = lax.broadcasted_iota(jnp.int32, (Q, Q), 1)
        keep = (r >= c) if (rev == transpose) else (r <= c)
        return jnp.where(keep, 1.0, 0.0).astype(F32)

    @jax.custom_vjp
    def cums(a):
        return _dot32(tri(False), a)

    cums.defvjp(lambda a: (_dot32(tri(False), a), None), lambda _, g: (_dot32(tri(True), g),))
    return cums


def _cols(v, k):
    w = v.shape[1] // k
    return tuple(v[:, w * i:w * (i + 1)] for i in range(k))


def _splitter(k):
    @jax.custom_vjp
    def split(v):
        return _cols(v, k)

    @jax.custom_vjp
    def concat(ps):
        return jnp.concatenate(ps, axis=1)

    split.defvjp(lambda v: (_cols(v, k), None), lambda _, g: (jnp.concatenate(g, axis=1),))
    concat.defvjp(lambda ps: (jnp.concatenate(ps, axis=1), None), lambda _, g: (_cols(g, k),))
    return split, concat


split2, _ = _splitter(2)
split4, _ = _splitter(4)
split8, concat8 = _splitter(8)


def _ln(x, g, b):
    mu = jnp.mean(x, axis=-1, keepdims=True)
    xc = x - mu
    var = jnp.mean(xc * xc, axis=-1, keepdims=True)
    return xc * lax.rsqrt(var + LN_EPS) * g + b


def _silu(x):
    return x * jax.nn.sigmoid(x)


def _gelu(x):
    return 0.5 * x * (1.0 + jnp.tanh(0.7978845608028654 * (x + 0.044715 * (x * x * x))))


def _xspec(T, w, col, roff):
    return pl.BlockSpec((T, w), lambda i, col=col, roff=roff: (jnp.maximum(i + roff, 0), col))


def _pspec(p, sel):
    if sel is None:
        return pl.BlockSpec(p.shape, lambda i, n=p.ndim: (0,) * n)
    return pl.BlockSpec((1,) + p.shape[1:], lambda i, n=p.ndim: (sel(i),) + (0,) * (n - 1))


def _out_plumbing(outs, T, args, in_specs):
    shapes, specs, aliases = [], [], {}
    for k, o in enumerate(outs):
        if o[0] == 'new':
            _, rows, w, roff = o[:4]
            shapes.append(jax.ShapeDtypeStruct((rows, w), o[4] if len(o) > 4 else F32))
            specs.append(_xspec(T, w, 0, roff))
        elif o[0] == 'acc':
            shapes.append(jax.ShapeDtypeStruct(o[1], F32))
            specs.append(pl.BlockSpec(o[1], lambda i, n=len(o[1]): (0,) * n))
        elif o[0] == 'part':
            _, rows, wtot, w, col, roff, dtype = o
            shapes.append(jax.ShapeDtypeStruct((rows, wtot), dtype))
            specs.append(_xspec(T, w, col, roff))
        else:
            _, arr, w, col, roff = o
            aliases[len(args)] = k
            args.append(arr)
            in_specs.append(pl.BlockSpec(memory_space=pl.ANY))
            shapes.append(jax.ShapeDtypeStruct(arr.shape, arr.dtype))
            specs.append(_xspec(T, w, col, roff))
    return shapes, specs, aliases


def stage_fwd(name, f, T, n, xs, ps, outs, rows=None):
    nx, npar = len(xs), len(ps)
    args = [x[0] for x in xs] + [p[0] for p in ps]
    in_specs = [_xspec(T, w, col, roff) for (_, w, col, roff) in xs] + [_pspec(p, sel) for (p, sel) in ps]
    n_in = len(args)
    shapes, specs, aliases = _out_plumbing(outs, T, args, in_specs)
    n_all_in = len(args)

    def body(*refs):
        i = pl.program_id(0)
        pv = [r[...] if ps[k][1] is None else r[0] for k, r in enumerate(refs[nx:n_in])]
        sums = {}
        for r0 in range(0, T, rows or T):
            g = slice(r0, r0 + (rows or T))
            res = f(*[r[g, :] for r in refs[:nx]], *pv)
            for k, o_ref in enumerate(refs[n_all_in:]):
                if outs[k][0] == 'acc':
                    sums[k] = res[k] if r0 == 0 else sums[k] + res[k]
                else:
                    o_ref[g, :] = res[k].astype(o_ref.dtype)
        for k, v in sums.items():
            o_ref = refs[n_all_in + k]

            @pl.when(i == 0)
            def _(o_ref=o_ref, v=v):
                o_ref[...] = v

            @pl.when(i > 0)
            def _(o_ref=o_ref, v=v):
                o_ref[...] += v

    return pl.pallas_call(body, name=name, grid=(n,), in_specs=in_specs, out_specs=specs, out_shape=shapes,
                          input_output_aliases=aliases, compiler_params=_cp("arbitrary"))(*args)


def stage_bwd(name, f, T, n, xs, ps, cts, dxs, dps, primal=(), rows=None):
    nx, npar = len(xs), len(ps)
    args = [x[0] for x in xs] + [p[0] for p in ps]
    in_specs = [_xspec(T, w, col, roff) for (_, w, col, roff) in xs] + [_pspec(p, sel) for (p, sel) in ps]
    ct_arrs = [c for c in cts if isinstance(c, tuple)]
    for (a, w, col, roff) in ct_arrs:
        args.append(a)
        in_specs.append(_xspec(T, w, col, roff))
    n_in = len(args)
    outs, out_of = [], []
    for k, o in enumerate(dxs):
        if o is not None:
            outs.append(o)
            out_of.append(('x', k))
    for k, want in enumerate(dps):
        if want:
            p, sel = ps[k]
            outs.append(('acc', p.shape))
            out_of.append(('p', k))
    for k, shape in primal:
        outs.append(('acc', shape))
        out_of.append(('r', k))
    shapes, specs, aliases = _out_plumbing(outs, T, args, in_specs)
    for j, (kind, k) in enumerate(out_of):
        if kind == 'p' and ps[k][1] is not None:
            p, sel = ps[k]
            specs[j] = pl.BlockSpec((1,) + p.shape[1:], lambda i, n=p.ndim, sel=sel: (sel(i),) + (0,) * (n - 1))
    n_all_in = len(args)

    def body(*refs):
        i = pl.program_id(0)
        pv = [r[...] if ps[k][1] is None else r[0] for k, r in enumerate(refs[nx:nx + npar])]
        sums = {}
        for r0 in range(0, T, rows or T):
            g = slice(r0, r0 + (rows or T))
            res, vjp_fn = jax.vjp(f, *[r[g, :] for r in refs[:nx]], *pv)
            ctv, q = [], nx + npar
            for k, c in enumerate(cts):
                if c is None:
                    ctv.append(jnp.zeros_like(res[k]))
                elif isinstance(c, tuple):
                    v = refs[q][g, :]
                    if c[3] < 0:
                        v = v * (i + c[3] >= 0).astype(F32)
                    ctv.append(v)
                    q += 1
                else:
                    ctv.append(jnp.full_like(res[k], c))
            grads = vjp_fn(tuple(ctv))
            for j, o_ref in enumerate(refs[n_all_in:]):
                kind, k = out_of[j]
                if kind == 'x':
                    o_ref[g, :] = grads[k].astype(o_ref.dtype)
                else:
                    v = res[k] if kind == 'r' else grads[nx + k]
                    sums[j] = v if r0 == 0 else sums[j] + v
        for j, v in sums.items():
            kind, k = out_of[j]
            o_ref = refs[n_all_in + j]
            sel = None if kind == 'r' else ps[k][1]
            if sel is None:
                first, tgt = i == 0, o_ref
            else:
                first, tgt = jnp.logical_or(i == 0, sel(i) != sel(jnp.maximum(i - 1, 0))), o_ref.at[0]

            @pl.when(first)
            def _(tgt=tgt, v=v):
                tgt[...] = v

            @pl.when(jnp.logical_not(first))
            def _(tgt=tgt, v=v):
                tgt[...] += v

    return pl.pallas_call(body, name=name, grid=(n,), in_specs=in_specs, out_specs=specs, out_shape=shapes,
                          input_output_aliases=aliases, compiler_params=_cp("arbitrary"))(*args)


_CONTRACT = {'nn': (1, 0), 'nt': (1, 1), 'tn': (0, 0)}


def matmul(name, a, b, mode, tm, tn, tk, out_dtype=F32, add=None):
    if mode == 'nn':
        (M, K), (_, N) = a.shape, b.shape
    elif mode == 'nt':
        (M, K), (N, _) = a.shape, b.shape
    else:
        (K, M), (_, N) = a.shape, b.shape
    assert M % tm == 0 and N % tn == 0 and K % tk == 0, (name, M, N, K, tm, tn, tk)
    a_spec = (pl.BlockSpec((tk, tm), lambda j, i, k: (k, i)) if mode == 'tn'
              else pl.BlockSpec((tm, tk), lambda j, i, k: (i, k)))
    b_spec = (pl.BlockSpec((tn, tk), lambda j, i, k: (j, k)) if mode == 'nt'
              else pl.BlockSpec((tk, tn), lambda j, i, k: (k, j)))
    o_spec = pl.BlockSpec((tm, tn), lambda j, i, k: (i, j))
    return matmul_call(name, (N // tn, M // tm, K // tk), a, a_spec, b, b_spec, (M, N), o_spec, (tm, tn), mode,
                       out_dtype, add)


def matmul_call(name, grid, a, a_spec, b, b_spec, out_shape, o_spec, tile, mode, out_dtype=F32, add=None):
    tm, tn = tile
    nk = grid[2]
    ca, cb = _CONTRACT[mode]
    args, in_specs = [a, b], [a_spec, b_spec]
    if add is not None:
        args.append(add)
        in_specs.append(o_spec)

    def body(*refs):
        a_ref, b_ref = refs[0], refs[1]
        o_ref, acc = refs[-2], refs[-1]
        k = pl.program_id(2)
        if nk == 1:
            p = _dot(a_ref[...], b_ref[...], ca, cb)
            o_ref[...] = (p + refs[2][...] if add is not None else p).astype(out_dtype)
            return

        @pl.when(k == 0)
        def _():
            acc[...] = refs[2][...] if add is not None else jnp.zeros_like(acc)

        acc[...] += _dot(a_ref[...], b_ref[...], ca, cb)

        @pl.when(k == nk - 1)
        def _():
            o_ref[...] = acc[...].astype(out_dtype)

    return pl.pallas_call(body, name=name, grid=grid, in_specs=in_specs, out_specs=o_spec,
                          out_shape=jax.ShapeDtypeStruct(out_shape, out_dtype),
                          scratch_shapes=[pltpu.VMEM((tm, tn) if nk > 1 else (8, 128), F32)],
                          compiler_params=_cp("arbitrary", "arbitrary", "arbitrary"))(*args)


NS, WS = 4, 704


def _resident(name, M, tm, rows, weight, out_shape, out_block, out_map, step, add=None):
    args = [rows[0], weight] + ([] if add is None else [add])
    in_specs = [pl.BlockSpec(rows[1], rows[2]), pl.BlockSpec(weight.shape, lambda i, n=weight.ndim: (0,) * n)]
    if add is not None:
        in_specs.append(pl.BlockSpec(out_block, out_map))
    return pl.pallas_call(step, name=name, grid=(M // tm,), in_specs=in_specs, out_specs=pl.BlockSpec(out_block, out_map),
                          out_shape=jax.ShapeDtypeStruct(out_shape, F32), compiler_params=_cp("arbitrary"))(*args)


def ffn_in_fwd(name, h, w1, w3, tm):
    M = h.shape[0]

    def step(h_ref, w1_ref, w3_ref, a1_ref, a3_ref, act_ref):
        for s in range(NS):
            a1 = _dot(h_ref[...], w1_ref[s], 1, 0)
            a3 = _dot(h_ref[...], w3_ref[s], 1, 0)
            a1_ref[s] = a1.astype(a1_ref.dtype)
            a3_ref[s] = a3.astype(a3_ref.dtype)
            act_ref[s] = (_silu(a1) * a3).astype(act_ref.dtype)

    wspec = pl.BlockSpec((NS, D, WS), lambda i: (0, 0, 0))
    ospec = pl.BlockSpec((NS, tm, WS), lambda i: (0, i, 0))
    return pl.pallas_call(
        step, name=name, grid=(M // tm,), in_specs=[pl.BlockSpec((tm, D), lambda i: (i, 0)), wspec, wspec],
        out_specs=[ospec, ospec, ospec],
        out_shape=[jax.ShapeDtypeStruct((NS, M, WS), BF16)] * 3, compiler_params=_cp("arbitrary"))(h, w1, w3)


def ffn_out_bwd_x(name, dff, w2, a1, a3, tm):
    M = dff.shape[0]

    def step(d_ref, w_ref, a1_ref, a3_ref, da1_ref, da3_ref):
        for s in range(NS):
            dact = _dot(d_ref[...], w_ref[s * WS:(s + 1) * WS, :], 1, 1)
            a1 = a1_ref[s].astype(F32)
            sig = jax.nn.sigmoid(a1)
            da3_ref[s] = (dact * (a1 * sig)).astype(da3_ref.dtype)
            da1_ref[s] = (dact * a3_ref[s].astype(F32) * (sig * (1.0 + a1 * (1.0 - sig)))).astype(da1_ref.dtype)

    aspec = pl.BlockSpec((NS, tm, WS), lambda i: (0, i, 0))
    return pl.pallas_call(
        step, name=name, grid=(M // tm,),
        in_specs=[pl.BlockSpec((tm, D), lambda i: (i, 0)), pl.BlockSpec(w2.shape, lambda i: (0, 0)), aspec, aspec],
        out_specs=[aspec, aspec],
        out_shape=[jax.ShapeDtypeStruct((NS, M, WS), BF16)] * 2, compiler_params=_cp("arbitrary"))(dff, w2, a1, a3)


def ff_in_bwd_x(name, da3, w3, tm, add=None):
    M = da3.shape[1]

    def step(*refs):
        d_ref, w_ref, o_ref = refs[0], refs[1], refs[-1]
        acc = _dot(d_ref[0], w_ref[0], 1, 1)
        for s in range(1, NS):
            acc = acc + _dot(d_ref[s], w_ref[s], 1, 1)
        o_ref[...] = acc if add is None else acc + refs[2][...]

    return _resident(name, M, tm, (da3, (NS, tm, WS), lambda i: (0, i, 0)), w3, (M, D), (tm, D), lambda i: (i, 0), step, add)


def ff_in_bwd_w(name, h, da3, tk):
    M = h.shape[0]

    def step(h_ref, d_ref, acc):
        for s in range(NS):
            acc[s] += _dot(h_ref[...], d_ref[s], 0, 0)

    return _token_sum(name, M // tk, [pl.BlockSpec((tk, D), lambda k: (k, 0)), pl.BlockSpec((NS, tk, WS), lambda k: (0, k, 0))],
                      (NS, D, WS), (NS, D, WS), step, (h, da3))


def ff_out_fwd(name, act3, w2, tm):
    M = act3.shape[1]

    def step(a_ref, w_ref, o_ref):
        acc = _dot(a_ref[0], w_ref[0:WS, :], 1, 0)
        for s in range(1, NS):
            acc = acc + _dot(a_ref[s], w_ref[s * WS:(s + 1) * WS, :], 1, 0)
        o_ref[...] = acc

    return _resident(name, M, tm, (act3, (NS, tm, WS), lambda i: (0, i, 0)), w2, (M, D), (tm, D), lambda i: (i, 0), step)


def _token_sum(name, nk, in_specs, out_shape, acc_shape, step, args):
    def body(*refs):
        o_ref, acc = refs[-2], refs[-1]
        k = pl.program_id(0)

        @pl.when(k == 0)
        def _():
            acc[...] = jnp.zeros_like(acc)

        step(*refs[:-2], acc)

        @pl.when(k == nk - 1)
        def _():
            o_ref[...] = acc[...].astype(o_ref.dtype)

    return pl.pallas_call(body, name=name, grid=(nk,), in_specs=in_specs,
                          out_specs=pl.BlockSpec(out_shape, lambda k, n=len(out_shape): (0,) * n),
                          out_shape=jax.ShapeDtypeStruct(out_shape, BF16), scratch_shapes=[pltpu.VMEM(acc_shape, F32)],
                          compiler_params=_cp("arbitrary"))(*args)


def ff_out_bwd_w(name, act3, dff, tk):
    M = dff.shape[0]

    def step(a_ref, d_ref, acc):
        for s in range(NS):
            acc[s * WS:(s + 1) * WS, :] += _dot(a_ref[s], d_ref[...], 0, 0)

    return _token_sum(name, M // tk, [pl.BlockSpec((NS, tk, WS), lambda k: (0, k, 0)), pl.BlockSpec((tk, D), lambda k: (k, 0))],
                      (NS * WS, D), (NS * WS, D), step, (act3, dff))


HALO = 8
CONV_ROWS = 128


def _fill(pad_ref, n, v=None):
    edge = jnp.zeros((HALO, 128), F32)
    pad_ref[0:HALO, :] = edge
    pad_ref[HALO + n:2 * HALO + n, :] = edge
    if v is not None:
        pad_ref[HALO:HALO + n, :] = v


def _tap(pad_ref, r0, k, rows=CONV_ROWS):
    return pad_ref[HALO + r0 + k - 2:HALO + r0 + k - 2 + rows, :]


def _conv_pre(pad_ref, r0, w_ref, b_ref):
    acc = _tap(pad_ref, r0, 0) * w_ref[0:1, :] + b_ref[...]
    for k in range(1, 5):
        acc = acc + _tap(pad_ref, r0, k) * w_ref[k:k + 1, :]
    return acc


def conv_fwd(name, proj, conv_w, conv_b, R):
    segs = ((0, LC), (LC, R))

    def body(x_ref, w_ref, b_ref, o_ref, xp):
        for (s, e) in segs:
            _fill(xp, e - s, x_ref[s:e, :])
            for r0 in range(0, e - s, CONV_ROWS):
                o_ref[s + r0:s + r0 + CONV_ROWS, :] = _silu(_conv_pre(xp, r0, w_ref, b_ref))

    return pl.pallas_call(
        body, name=name, grid=(12,),
        in_specs=[pl.BlockSpec((R, 128), lambda j: (0, C_XBC // 128 + j)),
                  pl.BlockSpec((8, 128), lambda j: (0, j)), pl.BlockSpec((1, 128), lambda j: (0, j))],
        out_specs=pl.BlockSpec((R, 128), lambda j: (0, j)),
        out_shape=jax.ShapeDtypeStruct((R, 1536), F32), scratch_shapes=[pltpu.VMEM((R - LC + 2 * HALO, 128), F32)],
        compiler_params=_cp("arbitrary"))(proj, conv_w, conv_b)


def conv_bwd(name, proj, conv_w, conv_b, d_f, d_b, d_skip, dproj, R):
    segs = ((0, LC), (LC, R))

    def body(x_ref, w_ref, b_ref, df_ref, db_ref, ds_ref, _, dx_ref, dw_ref, dbias_ref, xp, dp):
        j = pl.program_id(0)
        has_skip = (j < 8).astype(F32)
        dw = [jnp.zeros((8, 128), F32) for _ in range(5)]
        dbias = jnp.zeros((8, 128), F32)
        fold = lambda v: jnp.sum(v.reshape(CONV_ROWS // 8, 8, 128), axis=0)
        for (s, e) in segs:
            n = e - s
            _fill(xp, n, x_ref[s:e, :])
            _fill(dp, n)
            for r0 in range(0, n, CONV_ROWS):
                rows = slice(s + r0, s + r0 + CONV_ROWS)
                pre = _conv_pre(xp, r0, w_ref, b_ref)
                sig = jax.nn.sigmoid(pre)
                dy = df_ref[rows, :] + db_ref[rows, :]
                if s == LC:
                    dy = dy + ds_ref[r0:r0 + CONV_ROWS, :] * has_skip
                dpre = dy * (sig * (1.0 + pre * (1.0 - sig)))
                dp[HALO + r0:HALO + r0 + CONV_ROWS, :] = dpre
                dbias = dbias + fold(dpre)
            for r0 in range(0, n, CONV_ROWS):
                x = x_ref[s + r0:s + r0 + CONV_ROWS, :]
                dx = jnp.zeros_like(x)
                for k in range(5):
                    d = _tap(dp, r0, 4 - k)
                    dx = dx + d * w_ref[k:k + 1, :]
                    dw[k] = dw[k] + fold(d * x)
                dx_ref[s + r0:s + r0 + CONV_ROWS, :] = dx.astype(dx_ref.dtype)
        dw_ref[...] = jnp.zeros_like(dw_ref)
        for k in range(5):
            dw_ref[k:k + 1, :] = jnp.sum(dw[k], axis=0, keepdims=True)
        dbias_ref[...] = jnp.sum(dbias, axis=0, keepdims=True)

    pad = pltpu.VMEM((R - LC + 2 * HALO, 128), F32)
    return pl.pallas_call(
        body, name=name, grid=(12,),
        in_specs=[pl.BlockSpec((R, 128), lambda j: (0, C_XBC // 128 + j)),
                  pl.BlockSpec((8, 128), lambda j: (0, j)), pl.BlockSpec((1, 128), lambda j: (0, j)),
                  pl.BlockSpec((R, 128), lambda j: (0, j)), pl.BlockSpec((R, 128), lambda j: (0, j)),
                  pl.BlockSpec((R - LC, 128), lambda j: (0, jnp.minimum(j, 7))),
                  pl.BlockSpec(memory_space=pl.ANY)],
        out_specs=[pl.BlockSpec((R, 128), lambda j: (0, C_XBC // 128 + j)),
                   pl.BlockSpec((8, 128), lambda j: (0, j)), pl.BlockSpec((1, 128), lambda j: (0, j))],
        out_shape=[jax.ShapeDtypeStruct(dproj.shape, dproj.dtype), jax.ShapeDtypeStruct((8, 1536), F32),
                   jax.ShapeDtypeStruct((1, 1536), F32)], scratch_shapes=[pad, pad],
        input_output_aliases={6: 0}, compiler_params=_cp("arbitrary"))(proj, conv_w, conv_b, d_f, d_b, d_skip, dproj)


def _ssd_chunk(rev, dirn):
    cums = _cumsum_fn(rev)

    def f(xs, Bs, Cs, dt, alog, Hs):
        lane = lax.broadcasted_iota(jnp.int32, (1, 128), 1)
        sub = lax.broadcasted_iota(jnp.int32, (Q, 1), 0)
        r = lax.broadcasted_iota(jnp.int32, (Q, Q), 0)
        c = lax.broadcasted_iota(jnp.int32, (Q, Q), 1)
        mask = (r <= c) if rev else (r >= c)
        left = lane < 64
        a = dt * (-jnp.exp(alog))
        s = cums(a)
        sT, dtT = s.T, dt.T
        last_row = (sub == (0 if rev else Q - 1)).astype(F32)
        s_last = jnp.sum(s * last_row, axis=0, keepdims=True)
        G = [mm_nt(Cs[g], Bs[g]) for g in range(2)]
        M, es, wc, ed = [], [], [], []
        for h in range(NH):
            l = 16 * dirn + h
            oh_l = (lane == l).astype(F32)
            oh_s = (sub == l).astype(F32)
            s_col = jnp.sum(s * oh_l, axis=1, keepdims=True)
            dt_col = jnp.sum(dt * oh_l, axis=1, keepdims=True)
            s_row = jnp.sum(sT * oh_s, axis=0, keepdims=True)
            dt_row = jnp.sum(dtT * oh_s, axis=0, keepdims=True)
            sl = jnp.sum(s_last * oh_l, axis=1, keepdims=True)
            seg = jnp.where(mask, s_col - s_row, 0.0)
            lm = jnp.where(mask, jnp.exp(seg), 0.0)
            M.append(G[h // 8] * lm * dt_row)
            es.append(jnp.exp(s_col))
            wc.append(jnp.exp(sl - s_col) * dt_col)
            ed.append(jnp.exp(sl))
        Ys, Hn = [], []
        for j in range(8):
            g = j // 4
            xa = jnp.where(left, xs[j], 0.0)
            xb = jnp.where(left, 0.0, xs[j])
            yd = mm(M[2 * j], xa) + mm(M[2 * j + 1], xb)
            yo = mm(Cs[g], Hs[j]) * jnp.where(left, es[2 * j], es[2 * j + 1])
            Ys.append(yd + yo)
            st = mm_tn(Bs[g], xs[j] * jnp.where(left, wc[2 * j], wc[2 * j + 1]))
            Hn.append(Hs[j] * jnp.where(left, ed[2 * j], ed[2 * j + 1]) + st)
        return Ys, Hn

    return f


def _chunk_of(t, n, rev):
    if not rev:
        return t
    return jnp.where(t < 2, 1 - t, n + 1 - t)


def _cols128(ref, k, lead=()):
    return [ref[lead + (slice(None), slice(128 * j, 128 * (j + 1)))] for j in range(k)]


def ssd_fwd(name, xbc, dts, alog, n, dirs):
    nd = len(dirs)
    chunks = [_ssd_chunk(rev, dirn) for rev, dirn in dirs]

    def body(*refs):
        al_ref = refs[4 * nd]
        for d in range(nd):
            x_ref, b_ref, c_ref, dt_ref = refs[4 * d:4 * d + 4]
            y_ref, hs_ref = refs[4 * nd + 1 + 2 * d:4 * nd + 3 + 2 * d]
            h_scr = refs[4 * nd + 1 + 2 * nd + d]

            @pl.when(pl.program_id(0) == 0)
            def _(h_scr=h_scr):
                h_scr[...] = jnp.zeros_like(h_scr)

            hs_ref[0] = h_scr[...]
            Ys, Hn = chunks[d](_cols128(x_ref, 8), _cols128(b_ref, 2), _cols128(c_ref, 2), dt_ref[...], al_ref[...],
                               _cols128(h_scr, 8))
            for j in range(8):
                y_ref[:, 128 * j:128 * (j + 1)] = Ys[j]
                h_scr[:, 128 * j:128 * (j + 1)] = Hn[j]

    in_specs, out_specs, out_shape, args = [], [], [], []
    for (rev, _), dt in zip(dirs, dts):
        cm = lambda t, rev=rev: _chunk_of(t, n, rev)
        in_specs += [pl.BlockSpec((Q, 1024), lambda t, cm=cm: (cm(t), 0)), pl.BlockSpec((Q, 256), lambda t, cm=cm: (cm(t), 4)),
                     pl.BlockSpec((Q, 256), lambda t, cm=cm: (cm(t), 5)), pl.BlockSpec((Q, 128), lambda t, cm=cm: (cm(t), 0))]
        args += [xbc, xbc, xbc, dt]
        out_specs += [pl.BlockSpec((Q, 1024), lambda t, cm=cm: (cm(t), 0)),
                      pl.BlockSpec((1, Q, 1024), lambda t, cm=cm: (cm(t), 0, 0))]
        out_shape += [jax.ShapeDtypeStruct((n * Q, 1024), F32), jax.ShapeDtypeStruct((n, Q, 1024), F32)]
    res = pl.pallas_call(
        body, name=name, grid=(n,), in_specs=in_specs + [pl.BlockSpec((1, 128), lambda t: (0, 0))],
        out_specs=out_specs, out_shape=out_shape, scratch_shapes=[pltpu.VMEM((Q, 1024), F32)] * nd,
        compiler_params=_cp("arbitrary"))(*args, alog)
    return [res[2 * d:2 * d + 2] for d in range(nd)]


def ssd_bwd(name, xbc, dts, alog, hss, dy, n, dirs):
    nd = len(dirs)
    chunks = [_ssd_chunk(rev, dirn) for rev, dirn in dirs]

    def body(*refs):
        tt = pl.program_id(0)
        al_ref = refs[6 * nd]
        for d, (rev, _) in enumerate(dirs):
            x_ref, b_ref, c_ref, dt_ref, hs_ref, dy_ref = refs[6 * d:6 * d + 6]
            dx_ref, ddt_ref, dal_ref = refs[6 * nd + 1 + 3 * d:6 * nd + 4 + 3 * d]
            dh_scr = refs[6 * nd + 1 + 3 * nd + d]
            ch = _chunk_of(n - 1 - tt, n, rev)

            @pl.when(tt == 0)
            def _(dh_scr=dh_scr):
                dh_scr[...] = jnp.zeros_like(dh_scr)

            live = (ch >= 2).astype(F32)
            dYs = [v * live for v in _cols128(dy_ref, 8)]
            _, vjp_fn = jax.vjp(chunks[d], _cols128(x_ref, 8), _cols128(b_ref, 2), _cols128(c_ref, 2), dt_ref[...],
                                al_ref[...], _cols128(hs_ref, 8, (0,)))
            dxs, dBs, dCs, ddt, dal, dHs = vjp_fn((dYs, _cols128(dh_scr, 8)))
            for j in range(8):
                dx_ref[:, 128 * j:128 * (j + 1)] = dxs[j]
                dh_scr[:, 128 * j:128 * (j + 1)] = dHs[j]
            for g in range(2):
                dx_ref[:, 1024 + 128 * g:1024 + 128 * (g + 1)] = dBs[g]
                dx_ref[:, 1280 + 128 * g:1280 + 128 * (g + 1)] = dCs[g]
            ddt_ref[...] = ddt

            @pl.when(tt == 0)
            def _(dal_ref=dal_ref, dal=dal):
                dal_ref[...] = dal

            @pl.when(tt > 0)
            def _(dal_ref=dal_ref, dal=dal):
                dal_ref[...] += dal

    in_specs, out_specs, out_shape, args = [], [], [], []
    for (rev, _), dt, hs in zip(dirs, dts, hss):
        cm = lambda t, rev=rev: _chunk_of(n - 1 - t, n, rev)
        in_specs += [pl.BlockSpec((Q, 1024), lambda t, cm=cm: (cm(t), 0)), pl.BlockSpec((Q, 256), lambda t, cm=cm: (cm(t), 4)),
                     pl.BlockSpec((Q, 256), lambda t, cm=cm: (cm(t), 5)), pl.BlockSpec((Q, 128), lambda t, cm=cm: (cm(t), 0)),
                     pl.BlockSpec((1, Q, 1024), lambda t, cm=cm: (cm(t), 0, 0)),
                     pl.BlockSpec((Q, 1024), lambda t, cm=cm: (jnp.maximum(cm(t) - 2, 0), 0))]
        args += [xbc, xbc, xbc, dt, hs, dy]
        out_specs += [pl.BlockSpec((Q, 1536), lambda t, cm=cm: (cm(t), 0)), pl.BlockSpec((Q, 128), lambda t, cm=cm: (cm(t), 0)),
                      pl.BlockSpec((1, 128), lambda t: (0, 0))]
        out_shape += [jax.ShapeDtypeStruct((n * Q, 1536), F32), jax.ShapeDtypeStruct((n * Q, 128), F32),
                      jax.ShapeDtypeStruct((1, 128), F32)]
    res = pl.pallas_call(
        body, name=name, grid=(n,), in_specs=in_specs + [pl.BlockSpec((1, 128), lambda t: (0, 0))],
        out_specs=out_specs, out_shape=out_shape, scratch_shapes=[pltpu.VMEM((Q, 1024), F32)] * nd,
        compiler_params=_cp("arbitrary"))(*args, alog)
    return [res[3 * d:3 * d + 3] for d in range(nd)]


def f_norm0(c, x, g0, b0, sc, sh, is_ctx):
    x0 = _ln(jnp.where(is_ctx > 0.5, c, x), g0, b0)
    return x0, x0 * (1.0 + sc) + sh


def f_dt(raw, bias):
    z = split4(raw)[0] + bias
    dt = jnp.maximum(z, 0.0) + jnp.log1p(jnp.exp(-jnp.abs(z)))
    return dt, dt


def f_gated_norm(yf, yb, xs, z, dcol, g):
    h = (yf + yb + xs * dcol) * _silu(z)
    return (h * lax.rsqrt(jnp.mean(h * h, axis=-1, keepdims=True) + LN_EPS) * g,)


def f_gmlp(uv, gmg, gmb, *wb):
    ws, bs = wb[:8], wb[8:]
    u, v = split2(uv)
    vn = split8(_ln(_gelu(v), gmg, gmb))
    mixed = concat8(tuple(mm(ws[g], vn[g]) + bs[g] for g in range(8)))
    return (_gelu(u) * mixed,)


def f_merge(ps, pg, gates, bg):
    gs, gg = split2(jax.nn.sigmoid(gates + bg))
    return (gs * ps + gg * pg,)


def f_res1(x0, out, g1, lg, lb, sc, sh):
    x1 = _ln(ALPHA * x0 + g1 * out, lg, lb)
    return x1, x1 * (1.0 + sc) + sh


def f_res2_loss(x1, ff, tgt, g2, lg, lb):
    x2 = _ln(ALPHA * x1 + g2 * ff, lg, lb)
    e = x2 - tgt
    return (0.5 * jnp.sum(jnp.mean(e * e, axis=-1, keepdims=True), axis=0, keepdims=True),)


def _row_tile(M):
    return 544 if M % 544 == 0 else (512 if M % 512 == 0 else M)


def core(ctx, x, tgt, mod_x, mod_c, X, S):
    L = x.shape[0]
    R = LC + L
    n = R // Q
    T = 256
    nt, ntl = R // T, L // T
    tmR, tmL = _row_tile(R), _row_tile(L)
    tkR = 256 if R % 512 else 512
    tkL = 512 if L % 512 == 0 else 256
    row = lambda v: v.reshape(1, -1)
    mx = [row(mod_x[k]) for k in range(6)]
    mc = [row(mod_c[k]) for k in range(6)]
    sel = lambda i: jnp.minimum(i, 1)
    sc1 = jnp.stack([mc[1], mx[1]])
    sh1 = jnp.stack([mc[0], mx[0]])
    ln0 = [(row(S['ln0_g']), None), (row(S['ln0_b']), None), (sc1, sel), (sh1, sel),
           (jnp.array([1.0, 0.0], F32).reshape(2, 1, 1), sel)]
    x_n0 = [(ctx, D, 0, -nt), (x, D, 0, -1)]

    x0, xm = stage_fwd("norm0_fwd", f_norm0, T, nt, x_n0, ln0, [('new', R, D, 0), ('new', R, D, 0, BF16)])
    w_in = X.w_in()
    proj = matmul("proj_fwd", xm, w_in, 'nn', tmR, PW // 2, 1024)
    conv_w8 = jnp.pad(S['conv_w'], ((0, 3), (0, 0)))
    conv_b = row(S['conv_b'])
    xbc = conv_fwd("conv_fwd", proj, conv_w8, conv_b, R)
    dt_bias = jnp.pad(S['dt_bias'].reshape(1, 32), ((0, 0), (0, 96)))
    alog = jnp.pad(S['a_log'].reshape(1, 32), ((0, 0), (0, 96)))
    x_dt = [(proj, 512, C_DT // 512, 0)]
    dt_f, dt_b = stage_fwd("dt_fwd", f_dt, T, nt, x_dt, [(dt_bias, None)], [('new', R, 128, 0), ('new', R, 128, 0)])
    directions = [(False, 0), (True, 1)]
    (y_f, hs_f), (y_b, hs_b) = ssd_fwd("ssd_fwd", xbc, [dt_f, dt_b], alog, n, directions)
    W = X.rest()
    dcol = jnp.repeat(S['d_skip'][0] + S['d_skip'][1], 64).reshape(1, D)
    x_gn = [(y_f, D, 0, 1), (y_b, D, 0, 1), (xbc, D, 0, 1), (proj, D, C_Z // D, 1)]
    p_gn = [(dcol, None), (row(S['ssd_norm_g']), None)]
    (yn,) = stage_fwd("gnorm_fwd", f_gated_norm, T, ntl, x_gn, p_gn, [('new', L, D, 0, BF16)])
    x_gm = [(proj, 2 * D, C_UV // (2 * D), LC // Q)]
    p_gm = ([(row(S['gm_norm_g']), None), (row(S['gm_norm_b']), None)]
            + [(S['w_spatial'][g], None) for g in range(8)] + [(S['b_spatial'][g].reshape(Q, 1), None) for g in range(8)])
    (y_gm,) = stage_fwd("gmlp_fwd", f_gmlp, Q, L // Q, x_gm, p_gm, [('new', L, D, 0, BF16)])
    p_ssd = matmul("pssd_fwd", yn, W['w_ssd_proj'], 'nn', tmL, 1024, 1024)
    p_g = matmul("pgm_fwd", y_gm, W['w_gm_proj'], 'nn', tmL, 1024, 1024)
    x_mg = [(p_ssd, D, 0, 0), (p_g, D, 0, 0), (proj, 2 * D, C_GATE // (2 * D), 1)]
    p_mg = [(row(S['b_gate']), None)]
    (merged,) = stage_fwd("merge_fwd", f_merge, T, ntl, x_mg, p_mg, [('new', L, D, 0, BF16)])
    out = matmul("out_fwd", merged, W['w_out'], 'nn', tmL, 1024, 1024)
    x_r1 = [(x0, D, 0, 1), (out, D, 0, 0)]
    p_r1 = [(mx[2], None), (row(S['ln1_g']), None), (row(S['ln1_b']), None), (mx[4], None), (mx[3], None)]
    x1, hm = stage_fwd("res1_fwd", f_res1, T, ntl, x_r1, p_r1, [('new', L, D, 0), ('new', L, D, 0, BF16)])
    a1, a3, act = ffn_in_fwd("ffn_in_fwd", hm, W['w_ff1'], W['w_ff3'], T)
    ff = ff_out_fwd("ff2_fwd", act, W['w_ff2'], tmL)
    x_r2 = [(x1, D, 0, 0), (ff, D, 0, 0), (tgt, D, 0, 0)]
    p_r2 = [(mx[5], None), (row(S['ln2_g']), None), (row(S['ln2_b']), None)]

    dx1_a, dff, dg2, dl2g, dl2b, loss = stage_bwd(
        "res2_bwd", f_res2_loss, T, ntl, x_r2, p_r2, [1.0],
        [('new', L, D, 0), ('new', L, D, 0, BF16), None], [True, True, True], primal=[(0, (1, 1))])
    da1, da3 = ffn_out_bwd_x("ffn_out_bwd_x", dff, W['w_ff2'], a1, a3, T)
    gw_ff2 = ff_out_bwd_w("ff2_bwd_w", act, dff, tkL)
    dhm = ff_in_bwd_x("ff1_bwd_x", da1, W['w_ff1'], tmL)
    dhm = ff_in_bwd_x("ff3_bwd_x", da3, W['w_ff3'], tmL, add=dhm)
    gw_ff1 = ff_in_bwd_w("ff1_bwd_w", hm, da1, tkL)
    gw_ff3 = ff_in_bwd_w("ff3_bwd_w", hm, da3, tkL)
    X.grads('ffn', {'w_ff2': gw_ff2, 'w_ff1': gw_ff1, 'w_ff3': gw_ff3})
    dx0_a, dout, dg1, dl1g, dl1b, dsc2, dsh2 = stage_bwd(
        "res1_bwd", f_res1, T, ntl, x_r1, p_r1, [(dx1_a, D, 0, 0), (dhm, D, 0, 0)],
        [('new', L, D, 0), ('new', L, D, 0, BF16)], [True] * 5)
    dmerged = matmul("out_bwd_x", dout, W['w_out'], 'nt', tmL, 1024, 1024)
    gw_out = matmul("out_bwd_w", merged, dout, 'tn', 1024, 1024, tkL, BF16)
    lt, lq = -(LC // T), -(LC // Q)
    x_mg_b = [(p_ssd, D, 0, lt), (p_g, D, 0, lt), (proj, 2 * D, C_GATE // (2 * D), 0)]
    dp_ssd, dp_g, dproj, dbg = stage_bwd(
        "merge_bwd", f_merge, T, nt, x_mg_b, p_mg, [(dmerged, D, 0, lt)],
        [('new', L, D, lt, BF16), ('new', L, D, lt, BF16), ('part', R, PW, 2 * D, C_GATE // (2 * D), 0, BF16)], [True])
    dyn = matmul("pssd_bwd_x", dp_ssd, W['w_ssd_proj'], 'nt', tmL, 1024, 1024)
    gw_ssd = matmul("pssd_bwd_w", yn, dp_ssd, 'tn', 1024, 1024, tkL, BF16)
    dy_gm = matmul("pgm_bwd_x", dp_g, W['w_gm_proj'], 'nt', tmL, 1024, 1024)
    gw_gm = matmul("pgm_bwd_w", y_gm, dp_g, 'tn', 1024, 1024, tkL, BF16)
    X.grads('proj', {'w_out': gw_out, 'w_ssd_proj': gw_ssd, 'w_gm_proj': gw_gm})
    r_gm = stage_bwd("gmlp_bwd", f_gmlp, Q, n, [(proj, 2 * D, C_UV // (2 * D), 0)], p_gm, [(dy_gm, D, 0, lq)],
                     [('alias', dproj, 2 * D, C_UV // (2 * D), 0)], [True] * 18)
    dproj, dgmg, dgmb, dws, dbs = r_gm[0], r_gm[1], r_gm[2], r_gm[3:11], r_gm[11:19]
    x_gn_b = [(y_f, D, 0, 0), (y_b, D, 0, 0), (xbc, D, 0, 0), (proj, D, C_Z // D, 0)]
    dy, dskipx, dproj, ddcol, dng = stage_bwd(
        "gnorm_bwd", f_gated_norm, T, nt, x_gn_b, p_gn, [(dyn, D, 0, lt)],
        [('new', L, D, lt), None, ('new', L, D, lt), ('alias', dproj, D, C_Z // D, 0)], [True, True], rows=32)
    (dxbc_f, ddt_f, dal_f), (dxbc_b, ddt_b, dal_b) = ssd_bwd("ssd_bwd", xbc, [dt_f, dt_b], alog, [hs_f, hs_b], dy, n,
                                                             directions)
    dproj, ddtb = stage_bwd("dt_bwd", f_dt, T, nt, x_dt, [(dt_bias, None)],
                            [(ddt_f, 128, 0, 0), (ddt_b, 128, 0, 0)],
                            [('alias', dproj, 512, C_DT // 512, 0)], [True])
    dproj, dcw8, dcb = conv_bwd("conv_bwd", proj, conv_w8, conv_b, dxbc_f, dxbc_b, dskipx, dproj, R)
    gw_in = matmul("proj_bwd_w", xm, dproj, 'tn', 1024, PW // 2, tkR, BF16)
    X.grads('in', {'w_in': gw_in})
    dxm = matmul("proj_bwd_x", dproj, w_in, 'nt', R // 2 if R % 16 == 0 else R, 1024, 1024)
    grad_x, dl0g, dl0b, dsc1, dsh1 = stage_bwd(
        "norm0_bwd", f_norm0, T, nt, x_n0, ln0, [(dx0_a, D, 0, -1), (dxm, D, 0, 0)],
        [None, ('new', L, D, -1)], [True] * 4 + [False])

    zero = jnp.zeros((D,), F32)
    flat = lambda v: v.reshape(-1)
    small = {
        'loss': flat(loss), 'ln0_g': flat(dl0g), 'ln0_b': flat(dl0b),
        'dmod_x': jnp.concatenate([flat(dsh1[1]), flat(dsc1[1]), flat(dg1), flat(dsh2), flat(dsc2), flat(dg2)]),
        'dmod_c': jnp.concatenate([flat(dsh1[0]), flat(dsc1[0]), zero, zero, zero, zero]),
        'conv_w': flat(dcw8[:5]), 'conv_b': flat(dcb), 'dt_bias': flat(ddtb[:, :32]),
        'a_log': flat((dal_f + dal_b)[:, :32]),
        'd_skip': flat(jnp.tile(ddcol.reshape(1, NH, 64).sum(-1), (2, 1))),
        'ssd_norm_g': flat(dng), 'gm_norm_g': flat(dgmg), 'gm_norm_b': flat(dgmb),
        'w_spatial': flat(jnp.stack(dws)), 'b_spatial': flat(jnp.stack(dbs)), 'b_gate': flat(dbg),
        'ln1_g': flat(dl1g), 'ln1_b': flat(dl1b), 'ln2_g': flat(dl2g), 'ln2_b': flat(dl2b),
    }
    return grad_x, small


def _place():
    return lax.axis_index("x"), lax.axis_index("y"), lax.axis_index("c")


def allgather8(name, blk, hbm):
    space = pl.ANY if hbm else pltpu.VMEM

    def body(x_ref, out_ref, send_sems, recv_sems, local_sem):
        x, y, c = _place()
        me, sibling = (x, y, c), (x, y, 1 - c)
        chips = [(1 - x, y), (x, 1 - y), (1 - x, 1 - y)]

        def slot(px, py, pc):
            return out_ref.at[4 * px + 2 * py + pc]

        def copy(k, block, to, src=None):
            return pltpu.make_async_remote_copy(
                src_ref=slot(*block) if src is None else src, dst_ref=slot(*block),
                send_sem=send_sems.at[k], recv_sem=recv_sems.at[k], device_id=to, device_id_type=MESH)

        mine = pltpu.make_async_copy(x_ref, slot(*me), local_sem)
        mine.start()
        first = [copy(0, me, sibling, src=x_ref)]
        first += [copy(1 + j, me, (*chip, c), src=x_ref) for j, chip in enumerate(chips)]
        for cp in first:
            cp.start()
        passed = [copy(4 + j, (*chip, c), sibling) for j, chip in enumerate(chips)]
        for j, chip in enumerate(chips):
            copy(1 + j, (*chip, c), me).wait_recv()
            passed[j].start()
        copy(0, sibling, me).wait_recv()
        for j, chip in enumerate(chips):
            copy(4 + j, (*chip, 1 - c), me).wait_recv()
        for cp in first + passed:
            cp.wait_send()
        mine.wait()

    return pl.pallas_call(
        body, name=name, out_shape=jax.ShapeDtypeStruct((8,) + blk.shape, blk.dtype),
        in_specs=[pl.BlockSpec(memory_space=space)], out_specs=pl.BlockSpec(memory_space=space),
        scratch_shapes=[pltpu.SemaphoreType.DMA((7,)), pltpu.SemaphoreType.DMA((7,)), pltpu.SemaphoreType.DMA],
        compiler_params=pltpu.CompilerParams(vmem_limit_bytes=VMEM_LIMIT_V7X))(blk)


def _peers(place):
    x, y, c = place
    return [((1 - x) if k & 4 else x, (1 - y) if k & 2 else y, (1 - c) if k & 1 else c) for k in range(1, 8)]


def _slot(p):
    return 4 * p[0] + 2 * p[1] + p[2]


def plan_gather(place, srcs, lands):
    remote = [(s, l.at[_slot(place)], to) for s, l in zip(srcs, lands) for to in _peers(place)]
    return remote, [(s, l.at[_slot(place)]) for s, l in zip(srcs, lands)]


def plan_to_owner(place, srcs, lands):
    remote = [(s.at[2 * to[0] + to[1], to[2]], l.at[_slot(place)], to) for s, l in zip(srcs, lands) for to in _peers(place)]
    x, y, c = place
    return remote, [(s.at[2 * x + y, c], l.at[_slot(place)]) for s, l in zip(srcs, lands)]


def sequencer_exchange(name, collective_id, srcs, land_shapes, plan):
    n = len(srcs)
    src_refs = [jax.new_ref(a, memory_space=pltpu.MemorySpace.HBM) for a in srcs]
    land_refs = [jax.empty_ref(s, memory_space=pltpu.MemorySpace.HBM) for s in land_shapes]

    @pl.kernel(mesh=plsc.ScalarSubcoreMesh(axis_name="sequencer", num_cores=1), name=name,
               scratch_types=(pltpu.SemaphoreType.DMA((7 * n,)), pltpu.SemaphoreType.DMA((7 * n,)),
                              pltpu.SemaphoreType.DMA((n,))),
               compiler_params=pltpu.CompilerParams(collective_id=collective_id))
    def launch(send_sems, recv_sems, local_sems):
        place = _place()
        barrier = pltpu.get_barrier_semaphore()
        for to in _peers(place):
            pl.semaphore_signal(barrier, inc=1, device_id=to, device_id_type=MESH)
        pl.semaphore_wait(barrier, 7)
        remote, local = plan(place, src_refs, land_refs)
        mine = [pltpu.make_async_copy(s, d, local_sems.at[a]) for a, (s, d) in enumerate(local)]
        for cp in mine:
            cp.start()
        cps = [pltpu.make_async_remote_copy(src_ref=s, dst_ref=d, send_sem=send_sems.at[k], recv_sem=recv_sems.at[k],
                                            device_id=to, device_id_type=MESH) for k, (s, d, to) in enumerate(remote)]
        for cp in cps:
            cp.start()
        for cp in mine:
            cp.wait()
        for cp in cps:
            cp.wait()

    launch()
    return land_refs


def sibling_pair(name, hs):
    n = len(hs)

    def body(*refs):
        ins, outs = refs[:n], refs[n:2 * n]
        send_sems, recv_sems = refs[2 * n:]
        x, y, c = _place()
        cps = [pltpu.make_async_remote_copy(src_ref=outs[a].at[c], dst_ref=outs[a].at[c], send_sem=send_sems.at[a],
                                            recv_sem=recv_sems.at[a], device_id=(x, y, 1 - c), device_id_type=MESH)
               for a in range(n)]
        for cp in cps:
            cp.start()
        for a in range(n):
            pltpu.make_async_remote_copy(src_ref=outs[a].at[1 - c], dst_ref=outs[a].at[1 - c], send_sem=send_sems.at[a],
                                         recv_sem=recv_sems.at[a], device_id=(x, y, 1 - c),
                                         device_id_type=MESH).wait_recv()
        for cp in cps:
            cp.wait_send()

    any_spec = pl.BlockSpec(memory_space=pl.ANY)
    return pl.pallas_call(
        body, name=name, out_shape=[jax.ShapeDtypeStruct(h.shape, h.dtype) for h in hs],
        in_specs=[any_spec] * n, out_specs=[any_spec] * n, input_output_aliases={a: a for a in range(n)},
        scratch_shapes=[pltpu.SemaphoreType.DMA((n,)), pltpu.SemaphoreType.DMA((n,))])(*hs)


def owner_sum(name, land):
    _, r, w = land.shape
    T = r // 2

    def body(_, l_ref, o_ref):
        acc = l_ref[0].astype(F32)
        for j in range(1, 8):
            acc = acc + l_ref[j].astype(F32)
        o_ref[...] = acc

    grid_spec = pltpu.PrefetchScalarGridSpec(
        num_scalar_prefetch=1, grid=(2,),
        in_specs=[pl.BlockSpec((8, T, w), lambda i, at: (0, i, 0))],
        out_specs=pl.BlockSpec((None, T, w), lambda i, at: (at[0], i, 0)))
    at = jnp.stack([lax.axis_index("c")]).astype(jnp.int32)
    return pl.pallas_call(body, name=name, grid_spec=grid_spec, out_shape=jax.ShapeDtypeStruct((2, r, w), F32),
                          compiler_params=_cp("arbitrary"))(at, land)


W_IN_RUNS = ((0, 2, 1296, 376), (376, 3, 0, 1672), (2048, 1, 920, 752), (2800, 2, 0, 1296), (4096, 0, 0, 1024),
             (5120, 0, 1024, 648), (5768, 1, 0, 920))


def w_in_to_padded(name, g4):
    T = 128

    def body(g_ref, o_ref):
        o_ref[:, D_PROJ:PW] = jnp.zeros((T, PW - D_PROJ), o_ref.dtype)
        for (a, s, j0, w) in W_IN_RUNS:
            o_ref[:, a:a + w] = g_ref[s, :, j0:j0 + w]

    return pl.pallas_call(body, name=name, grid=(D // T,), in_specs=[pl.BlockSpec((4, T, 1672), lambda i: (0, i, 0))],
                          out_specs=pl.BlockSpec((T, PW), lambda i: (i, 0)),
                          out_shape=jax.ShapeDtypeStruct((D, PW), g4.dtype), compiler_params=_cp("arbitrary"))(g4)


def w_in_from_padded(name, gp):
    T = 128

    def body(g_ref, o_ref):
        for (a, s, j0, w) in W_IN_RUNS:
            o_ref[s, :, j0:j0 + w] = g_ref[:, a:a + w]

    return pl.pallas_call(body, name=name, grid=(D // T,), in_specs=[pl.BlockSpec((T, PW), lambda i: (i, 0))],
                          out_specs=pl.BlockSpec((4, T, 1672), lambda i: (0, i, 0)),
                          out_shape=jax.ShapeDtypeStruct((4, D, 1672), gp.dtype), compiler_params=_cp("arbitrary"))(gp)


def sum_devices(name, g):
    def body(g_ref, o_ref):
        acc = g_ref[0]
        for k in range(1, 8):
            acc = acc + g_ref[k]
        o_ref[...] = acc

    return pl.pallas_call(body, name=name, out_shape=jax.ShapeDtypeStruct(g.shape[1:], F32),
                          compiler_params=pltpu.CompilerParams(vmem_limit_bytes=VMEM_LIMIT_V7X))(g)


def adamw(name, w, g, m, v, T):
    r, wd = w.shape
    c1 = 1.0 - ADAM_B1 ** ADAM_STEP
    c2 = 1.0 - ADAM_B2 ** ADAM_STEP

    def body(w_ref, g_ref, m_ref, v_ref, d_ref, mo_ref, vo_ref):
        gv = g_ref[...]
        mn = ADAM_B1 * m_ref[...] + (1.0 - ADAM_B1) * gv
        vn = ADAM_B2 * v_ref[...] + (1.0 - ADAM_B2) * (gv * gv)
        d_ref[...] = -ADAM_LR * ((mn / c1) / (jnp.sqrt(vn / c2) + ADAM_EPS) + ADAM_WD * w_ref[...])
        mo_ref[...] = mn
        vo_ref[...] = vn

    spec = pl.BlockSpec((T, wd), lambda i: (i, 0))
    return pl.pallas_call(body, name=name, grid=(r // T,), in_specs=[spec] * 4, out_specs=[spec] * 3,
                          out_shape=[jax.ShapeDtypeStruct((r, wd), F32)] * 3, compiler_params=_cp("arbitrary"))(w, g, m, v)


BIG = {'w_in': (1024, 1672), 'w_ssd_proj': (256, 1024), 'w_gm_proj': (256, 1024), 'w_out': (256, 1024),
       'w_ff1': (1024, 704), 'w_ff3': (1024, 704), 'w_ff2': (704, 1024)}


class Flat:
    def __init__(self, segs):
        self.off, o = {}, 0
        for name, size in segs:
            self.off[name] = (o, size)
            o += -(-size // 128) * 128
        self.rows = -(-o // 1024) * 8

    def pack(self, vals):
        parts = []
        for name, (o, size) in self.off.items():
            v = vals[name].reshape(-1).astype(F32)
            parts.append(jnp.pad(v, (0, -(-size // 128) * 128 - size)))
        buf = jnp.concatenate(parts)
        return jnp.pad(buf, (0, self.rows * 128 - buf.shape[0])).reshape(self.rows, 128)

    def get(self, buf, name, shape=None):
        o, size = self.off[name]
        v = buf[o // 128:(o + size + 127) // 128].reshape(-1)[:size]
        return v if shape is None else v.reshape(shape)


PARTIALS = Flat([('loss', 1), ('ln0_g', D), ('ln0_b', D), ('dmod_x', 6 * D), ('dmod_c', 6 * D), ('conv_w', 5 * 1536),
                 ('conv_b', 1536), ('dt_bias', 32), ('a_log', 32), ('d_skip', 32), ('ssd_norm_g', D),
                 ('gm_norm_g', D), ('gm_norm_b', D), ('w_spatial', 8 * Q * Q), ('b_spatial', 8 * Q), ('b_gate', 2 * D),
                 ('ln1_g', D), ('ln1_b', D), ('ln2_g', D), ('ln2_b', D)])

WEIGHTS = ('c_ctx', 'ln0_g', 'ln0_b', 'w_ada', 'b_ada', 'w_in', 'conv_w', 'conv_b', 'dt_bias', 'a_log', 'd_skip',
           'ssd_norm_g', 'gm_norm_g', 'gm_norm_b', 'w_spatial', 'b_spatial', 'b_gate', 'w_ssd_proj', 'w_gm_proj',
           'w_out', 'ln1_g', 'ln1_b', 'w_ff1', 'w_ff3', 'w_ff2', 'ln2_g', 'ln2_b')
BIG_NAMES = tuple(BIG)
SMALL_NAMES = tuple(n for n in WEIGHTS if n not in BIG_NAMES and n != 'w_ada')


def kernel(x, c, ctx, c_ctx, ln0_g, ln0_b, w_ada, b_ada, w_in, conv_w, conv_b, dt_bias, a_log, d_skip, ssd_norm_g, gm_norm_g, gm_norm_b, w_spatial, b_spatial, b_gate, w_ssd_proj, w_gm_proj, w_out, ln1_g, ln1_b, w_ff1, w_ff3, w_ff2, ln2_g, ln2_b, loss_target, m_c_ctx, m_ln0_g, m_ln0_b, m_w_ada, m_b_ada, m_w_in, m_conv_w, m_conv_b, m_dt_bias, m_a_log, m_d_skip, m_ssd_norm_g, m_gm_norm_g, m_gm_norm_b, m_w_spatial, m_b_spatial, m_b_gate, m_w_ssd_proj, m_w_gm_proj, m_w_out, m_ln1_g, m_ln1_b, m_w_ff1, m_w_ff3, m_w_ff2, m_ln2_g, m_ln2_b, v_c_ctx, v_ln0_g, v_ln0_b, v_w_ada, v_b_ada, v_w_in, v_conv_w, v_conv_b, v_dt_bias, v_a_log, v_d_skip, v_ssd_norm_g, v_gm_norm_g, v_gm_norm_b, v_w_spatial, v_b_spatial, v_b_gate, v_w_ssd_proj, v_w_gm_proj, v_w_out, v_ln1_g, v_ln1_b, v_w_ff1, v_w_ff3, v_w_ff2, v_ln2_g, v_ln2_b):
    wts = dict(c_ctx=c_ctx, ln0_g=ln0_g, ln0_b=ln0_b, w_ada=w_ada, b_ada=b_ada, w_in=w_in, conv_w=conv_w, conv_b=conv_b,
               dt_bias=dt_bias, a_log=a_log, d_skip=d_skip, ssd_norm_g=ssd_norm_g, gm_norm_g=gm_norm_g,
               gm_norm_b=gm_norm_b, w_spatial=w_spatial, b_spatial=b_spatial, b_gate=b_gate, w_ssd_proj=w_ssd_proj,
               w_gm_proj=w_gm_proj, w_out=w_out, ln1_g=ln1_g, ln1_b=ln1_b, w_ff1=w_ff1, w_ff3=w_ff3, w_ff2=w_ff2,
               ln2_g=ln2_g, ln2_b=ln2_b)
    ms = dict(zip(WEIGHTS, (m_c_ctx, m_ln0_g, m_ln0_b, m_w_ada, m_b_ada, m_w_in, m_conv_w, m_conv_b, m_dt_bias, m_a_log,
                            m_d_skip, m_ssd_norm_g, m_gm_norm_g, m_gm_norm_b, m_w_spatial, m_b_spatial, m_b_gate,
                            m_w_ssd_proj, m_w_gm_proj, m_w_out, m_ln1_g, m_ln1_b, m_w_ff1, m_w_ff3, m_w_ff2, m_ln2_g,
                            m_ln2_b)))
    vs = dict(zip(WEIGHTS, (v_c_ctx, v_ln0_g, v_ln0_b, v_w_ada, v_b_ada, v_w_in, v_conv_w, v_conv_b, v_dt_bias, v_a_log,
                            v_d_skip, v_ssd_norm_g, v_gm_norm_g, v_gm_norm_b, v_w_spatial, v_b_spatial, v_b_gate,
                            v_w_ssd_proj, v_w_gm_proj, v_w_out, v_ln1_g, v_ln1_b, v_w_ff1, v_w_ff3, v_w_ff2, v_ln2_g,
                            v_ln2_b)))
    px, py, pc = _place()
    shard = 2 * px + py
    dev = 2 * shard + pc
    take = lambda a, i, axis=0: lax.dynamic_index_in_dim(a, i, axis, keepdims=False)

    half = lambda n: take(wts[n][0].reshape(2, BIG[n][0] // 2, BIG[n][1]), pc).astype(BF16)

    pre = jnp.concatenate([c, jnp.pad(conv_w[0], ((0, 0), (0, D - 384))), jnp.zeros((2, D), F32)], axis=0)
    pre = allgather8("gather_cond", pre, False)
    conv_w_full = pre[0::2, 1:6, :384].transpose(1, 0, 2).reshape(5, 1536)
    a16 = jnp.concatenate([_silu(pre[:, 0, :]), _silu(c_ctx)[None], jnp.zeros((7, D), F32)], axis=0)
    mod = matmul("ada_fwd", a16, w_ada[0], 'nn', 16, 512, 1024)
    mod = mod + lax.dynamic_slice_in_dim(b_ada[0], shard * 1536, 1536)[None]
    mod = allgather8("gather_mod", mod, False)
    mod = jnp.concatenate([mod[0], mod[2], mod[4], mod[6]], axis=1)
    mod_x = take(mod, dev).reshape(6, D)
    mod_c = mod[8].reshape(6, D)

    def full(n, blocks):
        r, w = BIG[n]
        return blocks.reshape(4, r, w) if w != D else blocks.reshape(4 * r, w)

    class Exchanges:
        rest_names = BIG_NAMES[1:]

        def __init__(self):
            self.pending = []

        def w_in(self):
            blocks = allgather8("gather_w_in", half('w_in'), True)
            w = w_in_to_padded("w_in_layout", full('w_in', blocks))
            halves = [half(n) for n in self.rest_names]
            halves[0], _ = lax.optimization_barrier((halves[0], blocks))
            lands = [jax.ShapeDtypeStruct((8,) + h.shape, BF16) for h in halves]
            self.rest_refs = sequencer_exchange("gather_rest", 1, halves, lands, plan_gather)
            return w

        def rest(self):
            return {n: full(n, r[...]) for n, r in zip(self.rest_names, self.rest_refs)}

        def grads(self, group, gs):
            if group == 'in':
                gs = {'w_in': w_in_from_padded("w_in_grad_layout", gs['w_in'])}
            blocks = [g.reshape(4, 2, BIG[n][0] // 2, BIG[n][1]) for n, g in gs.items()]
            lands = [jax.ShapeDtypeStruct((8,) + b.shape[2:], BF16) for b in blocks]
            refs = sequencer_exchange("grads_" + group, 2 + len(self.pending), blocks, lands, plan_to_owner)
            self.pending.append((tuple(gs), refs))

        def finish(self):
            names, halves = [], []
            for ns, refs in self.pending:
                names += ns
                halves += [owner_sum("grads_sum_" + n, r[...]) for n, r in zip(ns, refs)]
            return {n: h.reshape(BIG[n]) for n, h in zip(names, sibling_pair("grads_halves", halves))}

    S = dict(ln0_g=ln0_g, ln0_b=ln0_b, conv_w=conv_w_full, conv_b=conv_b[0], dt_bias=dt_bias[0], a_log=a_log[0],
             d_skip=d_skip[0], ssd_norm_g=ssd_norm_g[0], gm_norm_g=gm_norm_g[0], gm_norm_b=gm_norm_b[0],
             w_spatial=w_spatial[0], b_spatial=b_spatial[0], b_gate=b_gate[0], ln1_g=ln1_g[0], ln1_b=ln1_b[0],
             ln2_g=ln2_g[0], ln2_b=ln2_b[0])
    exchanges = Exchanges()
    grad_x, gsmall = core(ctx[0], x[0], loss_target[0], mod_x, mod_c, exchanges, S)

    parts = allgather8("gather_partials", PARTIALS.pack(gsmall), False)
    tot = sum_devices("partials_sum", parts)
    g_shards = exchanges.finish()
    g = {n: PARTIALS.get(tot, n) for n in ('ln0_g', 'ln0_b', 'conv_b', 'dt_bias', 'a_log', 'd_skip', 'ssd_norm_g',
                                           'gm_norm_g', 'gm_norm_b', 'w_spatial', 'b_spatial', 'b_gate', 'ln1_g',
                                           'ln1_b', 'ln2_g', 'ln2_b')}
    loss = PARTIALS.get(tot, 'loss', ())
    dmod_c = PARTIALS.get(tot, 'dmod_c')
    g['b_ada'] = PARTIALS.get(tot, 'dmod_x') + dmod_c
    g['conv_w'] = lax.dynamic_slice_in_dim(PARTIALS.get(tot, 'conv_w', (5, 1536)), shard * 384, 384, axis=1)
    o, size = PARTIALS.off['dmod_x']
    dmod_rows = parts[:, o // 128:(o + size) // 128].reshape(8, size)
    dm = jnp.concatenate([dmod_rows, dmod_c[None], jnp.zeros((7, 6 * D), F32)], axis=0)
    dm = lax.dynamic_slice_in_dim(dm, shard * 1536, 1536, axis=1)
    g['w_ada'] = matmul("ada_bwd_w", a16, dm, 'tn', 1024, 512, 16)
    dm_c = jnp.concatenate([dm[8:9], jnp.zeros((15, 1536), F32)], axis=0)
    dc = matmul("ada_bwd_c", dm_c, w_ada[0], 'nt', 16, 1024, 512)
    dc = allgather8("gather_dcctx", dc, False)[:, 0, :]
    dc = ((dc[0] + dc[2]) + dc[4]) + dc[6]
    sg = jax.nn.sigmoid(c_ctx)
    g['c_ctx'] = dc * (sg * (1.0 + c_ctx * (1.0 - sg)))
    for n in BIG_NAMES:
        g[n] = g_shards[n]

    delta, new_m, new_v = {}, {}, {}
    for n in BIG_NAMES + ('w_ada',):
        r, w = wts[n].shape[1:]
        if w % 128:
            T = max(t for t in range(8, 257, 8) if w % t == 0)
            d_, m_, v_ = adamw("adamw_" + n, wts[n][0].T, g[n].T, ms[n][0].T, vs[n][0].T, T)
            delta[n], new_m[n], new_v[n] = d_.T, m_.T, v_.T
        else:
            T = 352 if n == 'w_ff2' else 256
            delta[n], new_m[n], new_v[n] = adamw("adamw_" + n, wts[n][0], g[n], ms[n][0], vs[n][0], T)
    lay = Flat([(n, wts[n].size) for n in SMALL_NAMES])
    d_, m_, v_ = adamw("adamw_small", lay.pack(wts), lay.pack(g), lay.pack(ms), lay.pack(vs), lay.rows)
    for n in SMALL_NAMES:
        delta[n], new_m[n], new_v[n] = (lay.get(b, n) for b in (d_, m_, v_))

    shp = lambda d: [d[n].reshape(wts[n].shape) for n in WEIGHTS]
    return (loss, grad_x[None], *shp(g), *shp(delta), *shp(new_m), *shp(new_v))
```

```python
import functools

import jax
import jax.numpy as jnp
from jax import lax
from jax.experimental import pallas as pl
from jax.experimental.pallas import tpu as pltpu
from jax.experimental.pallas import tpu_sc as plsc

F32 = jnp.float32
BF16 = jnp.bfloat16
MESH = pl.DeviceIdType.MESH

VMEM_LIMIT_V7X = 56 * 1024 * 1024

D = 1024
LC = 256
Q = 128
NH = 16
D_FF = 2816
LN_EPS = 1e-5
ALPHA = 2.0 ** 0.25

PW = 7168
C_GATE, C_UV, C_Z, C_XBC, C_DT = 0, 2048, 4096, 5120, 6656
D_PROJ = 6688

ADAM_LR, ADAM_B1, ADAM_B2, ADAM_EPS, ADAM_WD, ADAM_STEP = 0.001, 0.9, 0.999, 1e-08, 0.01, 10


def _cp(*sem):
    return pltpu.CompilerParams(dimension_semantics=sem, vmem_limit_bytes=VMEM_LIMIT_V7X)


def _dot(a, b, ca, cb):
    return lax.dot_general(a.astype(BF16), b.astype(BF16), (((ca,), (cb,)), ((), ())),
                           preferred_element_type=F32)


@jax.custom_vjp
def mm(a, b):
    return _dot(a, b, 1, 0)


mm.defvjp(lambda a, b: (_dot(a, b, 1, 0), (a, b)),
          lambda r, g: (_dot(g, r[1], 1, 1), _dot(r[0], g, 0, 0)))


@jax.custom_vjp
def mm_nt(a, b):
    return _dot(a, b, 1, 1)


mm_nt.defvjp(lambda a, b: (_dot(a, b, 1, 1), (a, b)),
             lambda r, g: (_dot(g, r[1], 1, 0), _dot(g, r[0], 0, 0)))


@jax.custom_vjp
def mm_tn(a, b):
    return _dot(a, b, 0, 0)


mm_tn.defvjp(lambda a, b: (_dot(a, b, 0, 0), (a, b)),
             lambda r, g: (_dot(r[1], g, 1, 1), _dot(r[0], g, 1, 0)))


def _dot32(a, b):
    return lax.dot_general(a, b, (((1,), (0,)), ((), ())), precision=lax.Precision.HIGHEST,
                           preferred_element_type=F32)


def _cumsum_fn(rev):
    def tri(transpose):
        r = lax.broadcasted_iota(jnp.int32, (Q, Q), 0)
        c = lax.broadcasted_iota(jnp.int32, (Q, Q), 1)
        keep = (r >= c) if (rev == transpose) else (r <= c)
        return jnp.where(keep, 1.0, 0.0).astype(F32)

    @jax.custom_vjp
    def cums(a):
        return _dot32(tri(False), a)

    cums.defvjp(lambda a: (_dot32(tri(False), a), None), lambda _, g: (_dot32(tri(True), g),))
    return cums


def _cols(v, k):
    w = v.shape[1] // k
    return tuple(v[:, w * i:w * (i + 1)] for i in range(k))


def _splitter(k):
    @jax.custom_vjp
    def split(v):
        return _cols(v, k)

    @jax.custom_vjp
    def concat(ps):
        return jnp.concatenate(ps, axis=1)

    split.defvjp(lambda v: (_cols(v, k), None), lambda _, g: (jnp.concatenate(g, axis=1),))
    concat.defvjp(lambda ps: (jnp.concatenate(ps, axis=1), None), lambda _, g: (_cols(g, k),))
    return split, concat


split2, _ = _splitter(2)
split4, _ = _splitter(4)
split8, concat8 = _splitter(8)


def _ln(x, g, b):
    mu = jnp.mean(x, axis=-1, keepdims=True)
    xc = x - mu
    var = jnp.mean(xc * xc, axis=-1, keepdims=True)
    return xc * lax.rsqrt(var + LN_EPS) * g + b


def _silu(x):
    return x * jax.nn.sigmoid(x)


def _gelu(x):
    return 0.5 * x * (1.0 + jnp.tanh(0.7978845608028654 * (x + 0.044715 * (x * x * x))))


def _xspec(T, w, col, roff):
    return pl.BlockSpec((T, w), lambda i, col=col, roff=roff: (jnp.maximum(i + roff, 0), col))


def _pspec(p, sel):
    if sel is None:
        return pl.BlockSpec(p.shape, lambda i, n=p.ndim: (0,) * n)
    return pl.BlockSpec((1,) + p.shape[1:], lambda i, n=p.ndim: (sel(i),) + (0,) * (n - 1))


def _out_plumbing(outs, T, args, in_specs):
    shapes, specs, aliases = [], [], {}
    for k, o in enumerate(outs):
        if o[0] == 'new':
            _, rows, w, roff = o[:4]
            shapes.append(jax.ShapeDtypeStruct((rows, w), o[4] if len(o) > 4 else F32))
            specs.append(_xspec(T, w, 0, roff))
        elif o[0] == 'acc':
            shapes.append(jax.ShapeDtypeStruct(o[1], F32))
            specs.append(pl.BlockSpec(o[1], lambda i, n=len(o[1]): (0,) * n))
        elif o[0] == 'part':
            _, rows, wtot, w, col, roff, dtype = o
            shapes.append(jax.ShapeDtypeStruct((rows, wtot), dtype))
            specs.append(_xspec(T, w, col, roff))
        else:
            _, arr, w, col, roff = o
            aliases[len(args)] = k
            args.append(arr)
            in_specs.append(pl.BlockSpec(memory_space=pl.ANY))
            shapes.append(jax.ShapeDtypeStruct(arr.shape, arr.dtype))
            specs.append(_xspec(T, w, col, roff))
    return shapes, specs, aliases


def stage_fwd(name, f, T, n, xs, ps, outs, rows=None):
    nx, npar = len(xs), len(ps)
    args = [x[0] for x in xs] + [p[0] for p in ps]
    in_specs = [_xspec(T, w, col, roff) for (_, w, col, roff) in xs] + [_pspec(p, sel) for (p, sel) in ps]
    n_in = len(args)
    shapes, specs, aliases = _out_plumbing(outs, T, args, in_specs)
    n_all_in = len(args)

    def body(*refs):
        i = pl.program_id(0)
        pv = [r[...] if ps[k][1] is None else r[0] for k, r in enumerate(refs[nx:n_in])]
        sums = {}
        for r0 in range(0, T, rows or T):
            g = slice(r0, r0 + (rows or T))
            res = f(*[r[g, :] for r in refs[:nx]], *pv)
            for k, o_ref in enumerate(refs[n_all_in:]):
                if outs[k][0] == 'acc':
                    sums[k] = res[k] if r0 == 0 else sums[k] + res[k]
                else:
                    o_ref[g, :] = res[k].astype(o_ref.dtype)
        for k, v in sums.items():
            o_ref = refs[n_all_in + k]

            @pl.when(i == 0)
            def _(o_ref=o_ref, v=v):
                o_ref[...] = v

            @pl.when(i > 0)
            def _(o_ref=o_ref, v=v):
                o_ref[...] += v

    return pl.pallas_call(body, name=name, grid=(n,), in_specs=in_specs, out_specs=specs, out_shape=shapes,
                          input_output_aliases=aliases, compiler_params=_cp("arbitrary"))(*args)


def stage_bwd(name, f, T, n, xs, ps, cts, dxs, dps, primal=(), rows=None):
    nx, npar = len(xs), len(ps)
    args = [x[0] for x in xs] + [p[0] for p in ps]
    in_specs = [_xspec(T, w, col, roff) for (_, w, col, roff) in xs] + [_pspec(p, sel) for (p, sel) in ps]
    ct_arrs = [c for c in cts if isinstance(c, tuple)]
    for (a, w, col, roff) in ct_arrs:
        args.append(a)
        in_specs.append(_xspec(T, w, col, roff))
    n_in = len(args)
    outs, out_of = [], []
    for k, o in enumerate(dxs):
        if o is not None:
            outs.append(o)
            out_of.append(('x', k))
    for k, want in enumerate(dps):
        if want:
            p, sel = ps[k]
            outs.append(('acc', p.shape))
            out_of.append(('p', k))
    for k, shape in primal:
        outs.append(('acc', shape))
        out_of.append(('r', k))
    shapes, specs, aliases = _out_plumbing(outs, T, args, in_specs)
    for j, (kind, k) in enumerate(out_of):
        if kind == 'p' and ps[k][1] is not None:
            p, sel = ps[k]
            specs[j] = pl.BlockSpec((1,) + p.shape[1:], lambda i, n=p.ndim, sel=sel: (sel(i),) + (0,) * (n - 1))
    n_all_in = len(args)

    def body(*refs):
        i = pl.program_id(0)
        pv = [r[...] if ps[k][1] is None else r[0] for k, r in enumerate(refs[nx:nx + npar])]
        sums = {}
        for r0 in range(0, T, rows or T):
            g = slice(r0, r0 + (rows or T))
            res, vjp_fn = jax.vjp(f, *[r[g, :] for r in refs[:nx]], *pv)
            ctv, q = [], nx + npar
            for k, c in enumerate(cts):
                if c is None:
                    ctv.append(jnp.zeros_like(res[k]))
                elif isinstance(c, tuple):
                    v = refs[q][g, :]
                    if c[3] < 0:
                        v = v * (i + c[3] >= 0).astype(F32)
                    ctv.append(v)
                    q += 1
                else:
                    ctv.append(jnp.full_like(res[k], c))
            grads = vjp_fn(tuple(ctv))
            for j, o_ref in enumerate(refs[n_all_in:]):
                kind, k = out_of[j]
                if kind == 'x':
                    o_ref[g, :] = grads[k].astype(o_ref.dtype)
                else:
                    v = res[k] if kind == 'r' else grads[nx + k]
                    sums[j] = v if r0 == 0 else sums[j] + v
        for j, v in sums.items():
            kind, k = out_of[j]
            o_ref = refs[n_all_in + j]
            sel = None if kind == 'r' else ps[k][1]
            if sel is None:
                first, tgt = i == 0, o_ref
            else:
                first, tgt = jnp.logical_or(i == 0, sel(i) != sel(jnp.maximum(i - 1, 0))), o_ref.at[0]

            @pl.when(first)
            def _(tgt=tgt, v=v):
                tgt[...] = v

            @pl.when(jnp.logical_not(first))
            def _(tgt=tgt, v=v):
                tgt[...] += v

    return pl.pallas_call(body, name=name, grid=(n,), in_specs=in_specs, out_specs=specs, out_shape=shapes,
                          input_output_aliases=aliases, compiler_params=_cp("arbitrary"))(*args)


_CONTRACT = {'nn': (1, 0), 'nt': (1, 1), 'tn': (0, 0)}


def matmul(name, a, b, mode, tm, tn, tk, out_dtype=F32, add=None):
    if mode == 'nn':
        (M, K), (_, N) = a.shape, b.shape
    elif mode == 'nt':
        (M, K), (N, _) = a.shape, b.shape
    else:
        (K, M), (_, N) = a.shape, b.shape
    assert M % tm == 0 and N % tn == 0 and K % tk == 0, (name, M, N, K, tm, tn, tk)
    a_spec = (pl.BlockSpec((tk, tm), lambda j, i, k: (k, i)) if mode == 'tn'
              else pl.BlockSpec((tm, tk), lambda j, i, k: (i, k)))
    b_spec = (pl.BlockSpec((tn, tk), lambda j, i, k: (j, k)) if mode == 'nt'
              else pl.BlockSpec((tk, tn), lambda j, i, k: (k, j)))
    o_spec = pl.BlockSpec((tm, tn), lambda j, i, k: (i, j))
    return matmul_call(name, (N // tn, M // tm, K // tk), a, a_spec, b, b_spec, (M, N), o_spec, (tm, tn), mode,
                       out_dtype, add)


def matmul_call(name, grid, a, a_spec, b, b_spec, out_shape, o_spec, tile, mode, out_dtype=F32, add=None):
    tm, tn = tile
    nk = grid[2]
    ca, cb = _CONTRACT[mode]
    args, in_specs = [a, b], [a_spec, b_spec]
    if add is not None:
        args.append(add)
        in_specs.append(o_spec)

    def body(*refs):
        a_ref, b_ref = refs[0], refs[1]
        o_ref, acc = refs[-2], refs[-1]
        k = pl.program_id(2)
        if nk == 1:
            p = _dot(a_ref[...], b_ref[...], ca, cb)
            o_ref[...] = (p + refs[2][...] if add is not None else p).astype(out_dtype)
            return

        @pl.when(k == 0)
        def _():
            acc[...] = refs[2][...] if add is not None else jnp.zeros_like(acc)

        acc[...] += _dot(a_ref[...], b_ref[...], ca, cb)

        @pl.when(k == nk - 1)
        def _():
            o_ref[...] = acc[...].astype(out_dtype)

    return pl.pallas_call(body, name=name, grid=grid, in_specs=in_specs, out_specs=o_spec,
                          out_shape=jax.ShapeDtypeStruct(out_shape, out_dtype),
                          scratch_shapes=[pltpu.VMEM((tm, tn) if nk > 1 else (8, 128), F32)],
                          compiler_params=_cp("arbitrary", "arbitrary", "arbitrary"))(*args)


NS, WS = 4, 704


def _resident(name, M, tm, rows, weight, out_shape, out_block, out_map, step, add=None):
    args = [rows[0], weight] + ([] if add is None else [add])
    in_specs = [pl.BlockSpec(rows[1], rows[2]), pl.BlockSpec(weight.shape, lambda i, n=weight.ndim: (0,) * n)]
    if add is not None:
        in_specs.append(pl.BlockSpec(out_block, out_map))
    return pl.pallas_call(step, name=name, grid=(M // tm,), in_specs=in_specs, out_specs=pl.BlockSpec(out_block, out_map),
                          out_shape=jax.ShapeDtypeStruct(out_shape, F32), compiler_params=_cp("arbitrary"))(*args)


def ffn_in_fwd(name, h, w1, w3, tm):
    M = h.shape[0]

    def step(h_ref, w1_ref, w3_ref, a1_ref, a3_ref, act_ref):
        for s in range(NS):
            a1 = _dot(h_ref[...], w1_ref[s], 1, 0)
            a3 = _dot(h_ref[...], w3_ref[s], 1, 0)
            a1_ref[s] = a1.astype(a1_ref.dtype)
            a3_ref[s] = a3.astype(a3_ref.dtype)
            act_ref[s] = (_silu(a1) * a3).astype(act_ref.dtype)

    wspec = pl.BlockSpec((NS, D, WS), lambda i: (0, 0, 0))
    ospec = pl.BlockSpec((NS, tm, WS), lambda i: (0, i, 0))
    return pl.pallas_call(
        step, name=name, grid=(M // tm,), in_specs=[pl.BlockSpec((tm, D), lambda i: (i, 0)), wspec, wspec],
        out_specs=[ospec, ospec, ospec],
        out_shape=[jax.ShapeDtypeStruct((NS, M, WS), BF16)] * 3, compiler_params=_cp("arbitrary"))(h, w1, w3)


def ffn_out_bwd_x(name, dff, w2, a1, a3, tm):
    M = dff.shape[0]

    def step(d_ref, w_ref, a1_ref, a3_ref, da1_ref, da3_ref):
        for s in range(NS):
            dact = _dot(d_ref[...], w_ref[s * WS:(s + 1) * WS, :], 1, 1)
            a1 = a1_ref[s].astype(F32)
            sig = jax.nn.sigmoid(a1)
            da3_ref[s] = (dact * (a1 * sig)).astype(da3_ref.dtype)
            da1_ref[s] = (dact * a3_ref[s].astype(F32) * (sig * (1.0 + a1 * (1.0 - sig)))).astype(da1_ref.dtype)

    aspec = pl.BlockSpec((NS, tm, WS), lambda i: (0, i, 0))
    return pl.pallas_call(
        step, name=name, grid=(M // tm,),
        in_specs=[pl.BlockSpec((tm, D), lambda i: (i, 0)), pl.BlockSpec(w2.shape, lambda i: (0, 0)), aspec, aspec],
        out_specs=[aspec, aspec],
        out_shape=[jax.ShapeDtypeStruct((NS, M, WS), BF16)] * 2, compiler_params=_cp("arbitrary"))(dff, w2, a1, a3)


def ff_in_bwd_x(name, da3, w3, tm, add=None):
    M = da3.shape[1]

    def step(*refs):
        d_ref, w_ref, o_ref = refs[0], refs[1], refs[-1]
        acc = _dot(d_ref[0], w_ref[0], 1, 1)
        for s in range(1, NS):
            acc = acc + _dot(d_ref[s], w_ref[s], 1, 1)
        o_ref[...] = acc if add is None else acc + refs[2][...]

    return _resident(name, M, tm, (da3, (NS, tm, WS), lambda i: (0, i, 0)), w3, (M, D), (tm, D), lambda i: (i, 0), step, add)


def ff_in_bwd_w(name, h, da3, tk):
    M = h.shape[0]

    def step(h_ref, d_ref, acc):
        for s in range(NS):
            acc[s] += _dot(h_ref[...], d_ref[s], 0, 0)

    return _token_sum(name, M // tk, [pl.BlockSpec((tk, D), lambda k: (k, 0)), pl.BlockSpec((NS, tk, WS), lambda k: (0, k, 0))],
                      (NS, D, WS), (NS, D, WS), step, (h, da3))


def ff_out_fwd(name, act3, w2, tm):
    M = act3.shape[1]

    def step(a_ref, w_ref, o_ref):
        acc = _dot(a_ref[0], w_ref[0:WS, :], 1, 0)
        for s in range(1, NS):
            acc = acc + _dot(a_ref[s], w_ref[s * WS:(s + 1) * WS, :], 1, 0)
        o_ref[...] = acc

    return _resident(name, M, tm, (act3, (NS, tm, WS), lambda i: (0, i, 0)), w2, (M, D), (tm, D), lambda i: (i, 0), step)


def _token_sum(name, nk, in_specs, out_shape, acc_shape, step, args):
    def body(*refs):
        o_ref, acc = refs[-2], refs[-1]
        k = pl.program_id(0)

        @pl.when(k == 0)
        def _():
            acc[...] = jnp.zeros_like(acc)

        step(*refs[:-2], acc)

        @pl.when(k == nk - 1)
        def _():
            o_ref[...] = acc[...].astype(o_ref.dtype)

    return pl.pallas_call(body, name=name, grid=(nk,), in_specs=in_specs,
                          out_specs=pl.BlockSpec(out_shape, lambda k, n=len(out_shape): (0,) * n),
                          out_shape=jax.ShapeDtypeStruct(out_shape, BF16), scratch_shapes=[pltpu.VMEM(acc_shape, F32)],
                          compiler_params=_cp("arbitrary"))(*args)


def ff_out_bwd_w(name, act3, dff, tk):
    M = dff.shape[0]

    def step(a_ref, d_ref, acc):
        for s in range(NS):
            acc[s * WS:(s + 1) * WS, :] += _dot(a_ref[s], d_ref[...], 0, 0)

    return _token_sum(name, M // tk, [pl.BlockSpec((NS, tk, WS), lambda k: (0, k, 0)), pl.BlockSpec((tk, D), lambda k: (k, 0))],
                      (NS * WS, D), (NS * WS, D), step, (act3, dff))


HALO = 8
CONV_ROWS = 128


def _fill(pad_ref, n, v=None):
    edge = jnp.zeros((HALO, 128), F32)
    pad_ref[0:HALO, :] = edge
    pad_ref[HALO + n:2 * HALO + n, :] = edge
    if v is not None:
        pad_ref[HALO:HALO + n, :] = v


def _tap(pad_ref, r0, k, rows=CONV_ROWS):
    return pad_ref[HALO + r0 + k - 2:HALO + r0 + k - 2 + rows, :]


def _conv_pre(pad_ref, r0, w_ref, b_ref):
    acc = _tap(pad_ref, r0, 0) * w_ref[0:1, :] + b_ref[...]
    for k in range(1, 5):
        acc = acc + _tap(pad_ref, r0, k) * w_ref[k:k + 1, :]
    return acc


def conv_fwd(name, proj, conv_w, conv_b, R):
    segs = ((0, LC), (LC, R))

    def body(x_ref, w_ref, b_ref, o_ref, xp):
        for (s, e) in segs:
            _fill(xp, e - s, x_ref[s:e, :])
            for r0 in range(0, e - s, CONV_ROWS):
                o_ref[s + r0:s + r0 + CONV_ROWS, :] = _silu(_conv_pre(xp, r0, w_ref, b_ref))

    return pl.pallas_call(
        body, name=name, grid=(12,),
        in_specs=[pl.BlockSpec((R, 128), lambda j: (0, C_XBC // 128 + j)),
                  pl.BlockSpec((8, 128), lambda j: (0, j)), pl.BlockSpec((1, 128), lambda j: (0, j))],
        out_specs=pl.BlockSpec((R, 128), lambda j: (0, j)),
        out_shape=jax.ShapeDtypeStruct((R, 1536), F32), scratch_shapes=[pltpu.VMEM((R - LC + 2 * HALO, 128), F32)],
        compiler_params=_cp("arbitrary"))(proj, conv_w, conv_b)


def conv_bwd(name, proj, conv_w, conv_b, d_f, d_b, d_skip, dproj, R):
    segs = ((0, LC), (LC, R))

    def body(x_ref, w_ref, b_ref, df_ref, db_ref, ds_ref, _, dx_ref, dw_ref, dbias_ref, xp, dp):
        j = pl.program_id(0)
        has_skip = (j < 8).astype(F32)
        dw = [jnp.zeros((8, 128), F32) for _ in range(5)]
        dbias = jnp.zeros((8, 128), F32)
        fold = lambda v: jnp.sum(v.reshape(CONV_ROWS // 8, 8, 128), axis=0)
        for (s, e) in segs:
            n = e - s
            _fill(xp, n, x_ref[s:e, :])
            _fill(dp, n)
            for r0 in range(0, n, CONV_ROWS):
                rows = slice(s + r0, s + r0 + CONV_ROWS)
                pre = _conv_pre(xp, r0, w_ref, b_ref)
                sig = jax.nn.sigmoid(pre)
                dy = df_ref[rows, :] + db_ref[rows, :]
                if s == LC:
                    dy = dy + ds_ref[r0:r0 + CONV_ROWS, :] * has_skip
                dpre = dy * (sig * (1.0 + pre * (1.0 - sig)))
                dp[HALO + r0:HALO + r0 + CONV_ROWS, :] = dpre
                dbias = dbias + fold(dpre)
            for r0 in range(0, n, CONV_ROWS):
                x = x_ref[s + r0:s + r0 + CONV_ROWS, :]
                dx = jnp.zeros_like(x)
                for k in range(5):
                    d = _tap(dp, r0, 4 - k)
                    dx = dx + d * w_ref[k:k + 1, :]
                    dw[k] = dw[k] + fold(d * x)
                dx_ref[s + r0:s + r0 + CONV_ROWS, :] = dx.astype(dx_ref.dtype)
        dw_ref[...] = jnp.zeros_like(dw_ref)
        for k in range(5):
            dw_ref[k:k + 1, :] = jnp.sum(dw[k], axis=0, keepdims=True)
        dbias_ref[...] = jnp.sum(dbias, axis=0, keepdims=True)

    pad = pltpu.VMEM((R - LC + 2 * HALO, 128), F32)
    return pl.pallas_call(
        body, name=name, grid=(12,),
        in_specs=[pl.BlockSpec((R, 128), lambda j: (0, C_XBC // 128 + j)),
                  pl.BlockSpec((8, 128), lambda j: (0, j)), pl.BlockSpec((1, 128), lambda j: (0, j)),
                  pl.BlockSpec((R, 128), lambda j: (0, j)), pl.BlockSpec((R, 128), lambda j: (0, j)),
                  pl.BlockSpec((R - LC, 128), lambda j: (0, jnp.minimum(j, 7))),
                  pl.BlockSpec(memory_space=pl.ANY)],
        out_specs=[pl.BlockSpec((R, 128), lambda j: (0, C_XBC // 128 + j)),
                   pl.BlockSpec((8, 128), lambda j: (0, j)), pl.BlockSpec((1, 128), lambda j: (0, j))],
        out_shape=[jax.ShapeDtypeStruct(dproj.shape, dproj.dtype), jax.ShapeDtypeStruct((8, 1536), F32),
                   jax.ShapeDtypeStruct((1, 1536), F32)], scratch_shapes=[pad, pad],
        input_output_aliases={6: 0}, compiler_params=_cp("arbitrary"))(proj, conv_w, conv_b, d_f, d_b, d_skip, dproj)


def _ssd_chunk(rev, dirn):
    cums = _cumsum_fn(rev)

    def f(xs, Bs, Cs, dt, alog, Hs):
        lane = lax.broadcasted_iota(jnp.int32, (1, 128), 1)
        sub = lax.broadcasted_iota(jnp.int32, (Q, 1), 0)
        r = lax.broadcasted_iota(jnp.int32, (Q, Q), 0)
        c = lax.broadcasted_iota(jnp.int32, (Q, Q), 1)
        mask = (r <= c) if rev else (r >= c)
        left = lane < 64
        a = dt * (-jnp.exp(alog))
        s = cums(a)
        sT, dtT = s.T, dt.T
        last_row = (sub == (0 if rev else Q - 1)).astype(F32)
        s_last = jnp.sum(s * last_row, axis=0, keepdims=True)
        G = [mm_nt(Cs[g], Bs[g]) for g in range(2)]
        M, es, wc, ed = [], [], [], []
        for h in range(NH):
            l = 16 * dirn + h
            oh_l = (lane == l).astype(F32)
            oh_s = (sub == l).astype(F32)
            s_col = jnp.sum(s * oh_l, axis=1, keepdims=True)
            dt_col = jnp.sum(dt * oh_l, axis=1, keepdims=True)
            s_row = jnp.sum(sT * oh_s, axis=0, keepdims=True)
            dt_row = jnp.sum(dtT * oh_s, axis=0, keepdims=True)
            sl = jnp.sum(s_last * oh_l, axis=1, keepdims=True)
            seg = jnp.where(mask, s_col - s_row, 0.0)
            lm = jnp.where(mask, jnp.exp(seg), 0.0)
            M.append(G[h // 8] * lm * dt_row)
            es.append(jnp.exp(s_col))
            wc.append(jnp.exp(sl - s_col) * dt_col)
            ed.append(jnp.exp(sl))
        Ys, Hn = [], []
        for j in range(8):
            g = j // 4
            xa = jnp.where(left, xs[j], 0.0)
            xb = jnp.where(left, 0.0, xs[j])
            yd = mm(M[2 * j], xa) + mm(M[2 * j + 1], xb)
            yo = mm(Cs[g], Hs[j]) * jnp.where(left, es[2 * j], es[2 * j + 1])
            Ys.append(yd + yo)
            st = mm_tn(Bs[g], xs[j] * jnp.where(left, wc[2 * j], wc[2 * j + 1]))
            Hn.append(Hs[j] * jnp.where(left, ed[2 * j], ed[2 * j + 1]) + st)
        return Ys, Hn

    return f


def _chunk_of(t, n, rev):
    if not rev:
        return t
    return jnp.where(t < 2, 1 - t, n + 1 - t)


def _cols128(ref, k, lead=()):
    return [ref[lead + (slice(None), slice(128 * j, 128 * (j + 1)))] for j in range(k)]


def ssd_fwd(name, xbc, dts, alog, n, dirs):
    nd = len(dirs)
    chunks = [_ssd_chunk(rev, dirn) for rev, dirn in dirs]

    def body(*refs):
        al_ref = refs[4 * nd]
        for d in range(nd):
            x_ref, b_ref, c_ref, dt_ref = refs[4 * d:4 * d + 4]
            y_ref, hs_ref = refs[4 * nd + 1 + 2 * d:4 * nd + 3 + 2 * d]
            h_scr = refs[4 * nd + 1 + 2 * nd + d]

            @pl.when(pl.program_id(0) == 0)
            def _(h_scr=h_scr):
                h_scr[...] = jnp.zeros_like(h_scr)

            hs_ref[0] = h_scr[...]
            Ys, Hn = chunks[d](_cols128(x_ref, 8), _cols128(b_ref, 2), _cols128(c_ref, 2), dt_ref[...], al_ref[...],
                               _cols128(h_scr, 8))
            for j in range(8):
                y_ref[:, 128 * j:128 * (j + 1)] = Ys[j]
                h_scr[:, 128 * j:128 * (j + 1)] = Hn[j]

    in_specs, out_specs, out_shape, args = [], [], [], []
    for (rev, _), dt in zip(dirs, dts):
        cm = lambda t, rev=rev: _chunk_of(t, n, rev)
        in_specs += [pl.BlockSpec((Q, 1024), lambda t, cm=cm: (cm(t), 0)), pl.BlockSpec((Q, 256), lambda t, cm=cm: (cm(t), 4)),
                     pl.BlockSpec((Q, 256), lambda t, cm=cm: (cm(t), 5)), pl.BlockSpec((Q, 128), lambda t, cm=cm: (cm(t), 0))]
        args += [xbc, xbc, xbc, dt]
        out_specs += [pl.BlockSpec((Q, 1024), lambda t, cm=cm: (cm(t), 0)),
                      pl.BlockSpec((1, Q, 1024), lambda t, cm=cm: (cm(t), 0, 0))]
        out_shape += [jax.ShapeDtypeStruct((n * Q, 1024), F32), jax.ShapeDtypeStruct((n, Q, 1024), F32)]
    res = pl.pallas_call(
        body, name=name, grid=(n,), in_specs=in_specs + [pl.BlockSpec((1, 128), lambda t: (0, 0))],
        out_specs=out_specs, out_shape=out_shape, scratch_shapes=[pltpu.VMEM((Q, 1024), F32)] * nd,
        compiler_params=_cp("arbitrary"))(*args, alog)
    return [res[2 * d:2 * d + 2] for d in range(nd)]


def ssd_bwd(name, xbc, dts, alog, hss, dy, n, dirs):
    nd = len(dirs)
    chunks = [_ssd_chunk(rev, dirn) for rev, dirn in dirs]

    def body(*refs):
        tt = pl.program_id(0)
        al_ref = refs[6 * nd]
        for d, (rev, _) in enumerate(dirs):
            x_ref, b_ref, c_ref, dt_ref, hs_ref, dy_ref = refs[6 * d:6 * d + 6]
            dx_ref, ddt_ref, dal_ref = refs[6 * nd + 1 + 3 * d:6 * nd + 4 + 3 * d]
            dh_scr = refs[6 * nd + 1 + 3 * nd + d]
            ch = _chunk_of(n - 1 - tt, n, rev)

            @pl.when(tt == 0)
            def _(dh_scr=dh_scr):
                dh_scr[...] = jnp.zeros_like(dh_scr)

            live = (ch >= 2).astype(F32)
            dYs = [v * live for v in _cols128(dy_ref, 8)]
            _, vjp_fn = jax.vjp(chunks[d], _cols128(x_ref, 8), _cols128(b_ref, 2), _cols128(c_ref, 2), dt_ref[...],
                                al_ref[...], _cols128(hs_ref, 8, (0,)))
            dxs, dBs, dCs, ddt, dal, dHs = vjp_fn((dYs, _cols128(dh_scr, 8)))
            for j in range(8):
                dx_ref[:, 128 * j:128 * (j + 1)] = dxs[j]
                dh_scr[:, 128 * j:128 * (j + 1)] = dHs[j]
            for g in range(2):
                dx_ref[:, 1024 + 128 * g:1024 + 128 * (g + 1)] = dBs[g]
                dx_ref[:, 1280 + 128 * g:1280 + 128 * (g + 1)] = dCs[g]
            ddt_ref[...] = ddt

            @pl.when(tt == 0)
            def _(dal_ref=dal_ref, dal=dal):
                dal_ref[...] = dal

            @pl.when(tt > 0)
            def _(dal_ref=dal_ref, dal=dal):
                dal_ref[...] += dal

    in_specs, out_specs, out_shape, args = [], [], [], []
    for (rev, _), dt, hs in zip(dirs, dts, hss):
        cm = lambda t, rev=rev: _chunk_of(n - 1 - t, n, rev)
        in_specs += [pl.BlockSpec((Q, 1024), lambda t, cm=cm: (cm(t), 0)), pl.BlockSpec((Q, 256), lambda t, cm=cm: (cm(t), 4)),
                     pl.BlockSpec((Q, 256), lambda t, cm=cm: (cm(t), 5)), pl.BlockSpec((Q, 128), lambda t, cm=cm: (cm(t), 0)),
                     pl.BlockSpec((1, Q, 1024), lambda t, cm=cm: (cm(t), 0, 0)),
                     pl.BlockSpec((Q, 1024), lambda t, cm=cm: (jnp.maximum(cm(t) - 2, 0), 0))]
        args += [xbc, xbc, xbc, dt, hs, dy]
        out_specs += [pl.BlockSpec((Q, 1536), lambda t, cm=cm: (cm(t), 0)), pl.BlockSpec((Q, 128), lambda t, cm=cm: (cm(t), 0)),
                      pl.BlockSpec((1, 128), lambda t: (0, 0))]
        out_shape += [jax.ShapeDtypeStruct((n * Q, 1536), F32), jax.ShapeDtypeStruct((n * Q, 128), F32),
                      jax.ShapeDtypeStruct((1, 128), F32)]
    res = pl.pallas_call(
        body, name=name, grid=(n,), in_specs=in_specs + [pl.BlockSpec((1, 128), lambda t: (0, 0))],
        out_specs=out_specs, out_shape=out_shape, scratch_shapes=[pltpu.VMEM((Q, 1024), F32)] * nd,
        compiler_params=_cp("arbitrary"))(*args, alog)
    return [res[3 * d:3 * d + 3] for d in range(nd)]


def f_norm0(c, x, g0, b0, sc, sh, is_ctx):
    x0 = _ln(jnp.where(is_ctx > 0.5, c, x), g0, b0)
    return x0, x0 * (1.0 + sc) + sh


def f_dt(raw, bias):
    z = split4(raw)[0] + bias
    dt = jnp.maximum(z, 0.0) + jnp.log1p(jnp.exp(-jnp.abs(z)))
    return dt, dt


def f_gated_norm(yf, yb, xs, z, dcol, g):
    h = (yf + yb + xs * dcol) * _silu(z)
    return (h * lax.rsqrt(jnp.mean(h * h, axis=-1, keepdims=True) + LN_EPS) * g,)


def f_gmlp(uv, gmg, gmb, *wb):
    ws, bs = wb[:8], wb[8:]
    u, v = split2(uv)
    vn = split8(_ln(_gelu(v), gmg, gmb))
    mixed = concat8(tuple(mm(ws[g], vn[g]) + bs[g] for g in range(8)))
    return (_gelu(u) * mixed,)


def f_merge(ps, pg, gates, bg):
    gs, gg = split2(jax.nn.sigmoid(gates + bg))
    return (gs * ps + gg * pg,)


def f_res1(x0, out, g1, lg, lb, sc, sh):
    x1 = _ln(ALPHA * x0 + g1 * out, lg, lb)
    return x1, x1 * (1.0 + sc) + sh


def f_res2_loss(x1, ff, tgt, g2, lg, lb):
    x2 = _ln(ALPHA * x1 + g2 * ff, lg, lb)
    e = x2 - tgt
    return (0.5 * jnp.sum(jnp.mean(e * e, axis=-1, keepdims=True), axis=0, keepdims=True),)


def _row_tile(M):
    return 544 if M % 544 == 0 else (512 if M % 512 == 0 else M)


def core(ctx, x, tgt, mod_x, mod_c, X, S):
    L = x.shape[0]
    R = LC + L
    n = R // Q
    T = 256
    nt, ntl = R // T, L // T
    tmR, tmL = _row_tile(R), _row_tile(L)
    tmS = 1024 if L % 1024 == 0 else tmL
    tkR = 256 if R % 512 else 512
    tkL = 512 if L % 512 == 0 else 256
    row = lambda v: v.reshape(1, -1)
    mx = [row(mod_x[k]) for k in range(6)]
    mc = [row(mod_c[k]) for k in range(6)]
    sel = lambda i: jnp.minimum(i, 1)
    sc1 = jnp.stack([mc[1], mx[1]])
    sh1 = jnp.stack([mc[0], mx[0]])
    ln0 = [(row(S['ln0_g']), None), (row(S['ln0_b']), None), (sc1, sel), (sh1, sel),
           (jnp.array([1.0, 0.0], F32).reshape(2, 1, 1), sel)]
    x_n0 = [(ctx, D, 0, -nt), (x, D, 0, -1)]

    x0, xm = stage_fwd("norm0_fwd", f_norm0, T, nt, x_n0, ln0, [('new', R, D, 0), ('new', R, D, 0, BF16)])
    w_in = X.w_in()
    proj = matmul("proj_fwd", xm, w_in, 'nn', tmR, PW // 2, 1024)
    conv_w8 = jnp.pad(S['conv_w'], ((0, 3), (0, 0)))
    conv_b = row(S['conv_b'])
    xbc = conv_fwd("conv_fwd", proj, conv_w8, conv_b, R)
    dt_bias = jnp.pad(S['dt_bias'].reshape(1, 32), ((0, 0), (0, 96)))
    alog = jnp.pad(S['a_log'].reshape(1, 32), ((0, 0), (0, 96)))
    x_dt = [(proj, 512, C_DT // 512, 0)]
    dt_f, dt_b = stage_fwd("dt_fwd", f_dt, T, nt, x_dt, [(dt_bias, None)], [('new', R, 128, 0), ('new', R, 128, 0)])
    directions = [(False, 0), (True, 1)]
    (y_f, hs_f), (y_b, hs_b) = ssd_fwd("ssd_fwd", xbc, [dt_f, dt_b], alog, n, directions)
    W = X.rest()
    dcol = jnp.repeat(S['d_skip'][0] + S['d_skip'][1], 64).reshape(1, D)
    x_gn = [(y_f, D, 0, 1), (y_b, D, 0, 1), (xbc, D, 0, 1), (proj, D, C_Z // D, 1)]
    p_gn = [(dcol, None), (row(S['ssd_norm_g']), None)]
    (yn,) = stage_fwd("gnorm_fwd", f_gated_norm, T, ntl, x_gn, p_gn, [('new', L, D, 0, BF16)])
    x_gm = [(proj, 2 * D, C_UV // (2 * D), LC // Q)]
    p_gm = ([(row(S['gm_norm_g']), None), (row(S['gm_norm_b']), None)]
            + [(S['w_spatial'][g], None) for g in range(8)] + [(S['b_spatial'][g].reshape(Q, 1), None) for g in range(8)])
    (y_gm,) = stage_fwd("gmlp_fwd", f_gmlp, Q, L // Q, x_gm, p_gm, [('new', L, D, 0, BF16)])
    p_ssd = matmul("pssd_fwd", yn, W['w_ssd_proj'], 'nn', tmS, 1024, 1024)
    p_g = matmul("pgm_fwd", y_gm, W['w_gm_proj'], 'nn', tmS, 1024, 1024)
    x_mg = [(p_ssd, D, 0, 0), (p_g, D, 0, 0), (proj, 2 * D, C_GATE // (2 * D), 1)]
    p_mg = [(row(S['b_gate']), None)]
    (merged,) = stage_fwd("merge_fwd", f_merge, T, ntl, x_mg, p_mg, [('new', L, D, 0, BF16)])
    out = matmul("out_fwd", merged, W['w_out'], 'nn', tmS, 1024, 1024)
    x_r1 = [(x0, D, 0, 1), (out, D, 0, 0)]
    p_r1 = [(mx[2], None), (row(S['ln1_g']), None), (row(S['ln1_b']), None), (mx[4], None), (mx[3], None)]
    x1, hm = stage_fwd("res1_fwd", f_res1, T, ntl, x_r1, p_r1, [('new', L, D, 0), ('new', L, D, 0, BF16)])
    a1, a3, act = ffn_in_fwd("ffn_in_fwd", hm, W['w_ff1'], W['w_ff3'], tmL)
    ff = ff_out_fwd("ff2_fwd", act, W['w_ff2'], tmL)
    x_r2 = [(x1, D, 0, 0), (ff, D, 0, 0), (tgt, D, 0, 0)]
    p_r2 = [(mx[5], None), (row(S['ln2_g']), None), (row(S['ln2_b']), None)]

    dx1_a, dff, dg2, dl2g, dl2b, loss = stage_bwd(
        "res2_bwd", f_res2_loss, T, ntl, x_r2, p_r2, [1.0],
        [('new', L, D, 0), ('new', L, D, 0, BF16), None], [True, True, True], primal=[(0, (1, 1))])
    da1, da3 = ffn_out_bwd_x("ffn_out_bwd_x", dff, W['w_ff2'], a1, a3, tmL)
    gw_ff2 = ff_out_bwd_w("ff2_bwd_w", act, dff, tkL)
    dhm = ff_in_bwd_x("ff1_bwd_x", da1, W['w_ff1'], tmL)
    dhm = ff_in_bwd_x("ff3_bwd_x", da3, W['w_ff3'], tmL, add=dhm)
    gw_ff1 = ff_in_bwd_w("ff1_bwd_w", hm, da1, tkL)
    gw_ff3 = ff_in_bwd_w("ff3_bwd_w", hm, da3, tkL)
    X.grads('ffn', {'w_ff2': gw_ff2, 'w_ff1': gw_ff1, 'w_ff3': gw_ff3})
    dx0_a, dout, dg1, dl1g, dl1b, dsc2, dsh2 = stage_bwd(
        "res1_bwd", f_res1, T, ntl, x_r1, p_r1, [(dx1_a, D, 0, 0), (dhm, D, 0, 0)],
        [('new', L, D, 0), ('new', L, D, 0, BF16)], [True] * 5)
    dmerged = matmul("out_bwd_x", dout, W['w_out'], 'nt', tmS, 1024, 1024)
    gw_out = matmul("out_bwd_w", merged, dout, 'tn', 1024, 1024, tkL, BF16)
    lt, lq = -(LC // T), -(LC // Q)
    x_mg_b = [(p_ssd, D, 0, lt), (p_g, D, 0, lt), (proj, 2 * D, C_GATE // (2 * D), 0)]
    dp_ssd, dp_g, dproj, dbg = stage_bwd(
        "merge_bwd", f_merge, T, nt, x_mg_b, p_mg, [(dmerged, D, 0, lt)],
        [('new', L, D, lt, BF16), ('new', L, D, lt, BF16), ('part', R, PW, 2 * D, C_GATE // (2 * D), 0, BF16)], [True])
    dyn = matmul("pssd_bwd_x", dp_ssd, W['w_ssd_proj'], 'nt', tmS, 1024, 1024)
    gw_ssd = matmul("pssd_bwd_w", yn, dp_ssd, 'tn', 1024, 1024, tkL, BF16)
    dy_gm = matmul("pgm_bwd_x", dp_g, W['w_gm_proj'], 'nt', tmS, 1024, 1024)
    gw_gm = matmul("pgm_bwd_w", y_gm, dp_g, 'tn', 1024, 1024, tkL, BF16)
    X.grads('proj', {'w_out': gw_out, 'w_ssd_proj': gw_ssd, 'w_gm_proj': gw_gm})
    r_gm = stage_bwd("gmlp_bwd", f_gmlp, Q, n, [(proj, 2 * D, C_UV // (2 * D), 0)], p_gm, [(dy_gm, D, 0, lq)],
                     [('alias', dproj, 2 * D, C_UV // (2 * D), 0)], [True] * 18)
    dproj, dgmg, dgmb, dws, dbs = r_gm[0], r_gm[1], r_gm[2], r_gm[3:11], r_gm[11:19]
    x_gn_b = [(y_f, D, 0, 0), (y_b, D, 0, 0), (xbc, D, 0, 0), (proj, D, C_Z // D, 0)]
    dy, dskipx, dproj, ddcol, dng = stage_bwd(
        "gnorm_bwd", f_gated_norm, T, nt, x_gn_b, p_gn, [(dyn, D, 0, lt)],
        [('new', L, D, lt), None, ('new', L, D, lt), ('alias', dproj, D, C_Z // D, 0)], [True, True], rows=32)
    (dxbc_f, ddt_f, dal_f), (dxbc_b, ddt_b, dal_b) = ssd_bwd("ssd_bwd", xbc, [dt_f, dt_b], alog, [hs_f, hs_b], dy, n,
                                                             directions)
    dproj, ddtb = stage_bwd("dt_bwd", f_dt, T, nt, x_dt, [(dt_bias, None)],
                            [(ddt_f, 128, 0, 0), (ddt_b, 128, 0, 0)],
                            [('alias', dproj, 512, C_DT // 512, 0)], [True])
    dproj, dcw8, dcb = conv_bwd("conv_bwd", proj, conv_w8, conv_b, dxbc_f, dxbc_b, dskipx, dproj, R)
    gw_in = matmul("proj_bwd_w", xm, dproj, 'tn', 1024, PW // 2, tkR, BF16)
    X.grads('in', {'w_in': gw_in})
    dxm = matmul("proj_bwd_x", dproj, w_in, 'nt', R // 2 if R % 16 == 0 else R, 1024, 1024)
    grad_x, dl0g, dl0b, dsc1, dsh1 = stage_bwd(
        "norm0_bwd", f_norm0, T, nt, x_n0, ln0, [(dx0_a, D, 0, -1), (dxm, D, 0, 0)],
        [None, ('new', L, D, -1)], [True] * 4 + [False])

    zero = jnp.zeros((D,), F32)
    flat = lambda v: v.reshape(-1)
    small = {
        'loss': flat(loss), 'ln0_g': flat(dl0g), 'ln0_b': flat(dl0b),
        'dmod_x': jnp.concatenate([flat(dsh1[1]), flat(dsc1[1]), flat(dg1), flat(dsh2), flat(dsc2), flat(dg2)]),
        'dmod_c': jnp.concatenate([flat(dsh1[0]), flat(dsc1[0]), zero, zero, zero, zero]),
        'conv_w': flat(dcw8[:5]), 'conv_b': flat(dcb), 'dt_bias': flat(ddtb[:, :32]),
        'a_log': flat((dal_f + dal_b)[:, :32]),
        'd_skip': flat(jnp.tile(ddcol.reshape(1, NH, 64).sum(-1), (2, 1))),
        'ssd_norm_g': flat(dng), 'gm_norm_g': flat(dgmg), 'gm_norm_b': flat(dgmb),
        'w_spatial': flat(jnp.stack(dws)), 'b_spatial': flat(jnp.stack(dbs)), 'b_gate': flat(dbg),
        'ln1_g': flat(dl1g), 'ln1_b': flat(dl1b), 'ln2_g': flat(dl2g), 'ln2_b': flat(dl2b),
    }
    return grad_x, small


def _place():
    return lax.axis_index("x"), lax.axis_index("y"), lax.axis_index("c")


def allgather8(name, blk, hbm):
    space = pl.ANY if hbm else pltpu.VMEM

    def body(x_ref, out_ref, send_sems, recv_sems, local_sem):
        x, y, c = _place()
        me, sibling = (x, y, c), (x, y, 1 - c)
        chips = [(1 - x, y), (x, 1 - y), (1 - x, 1 - y)]

        def slot(px, py, pc):
            return out_ref.at[4 * px + 2 * py + pc]

        def copy(k, block, to, src=None):
            return pltpu.make_async_remote_copy(
                src_ref=slot(*block) if src is None else src, dst_ref=slot(*block),
                send_sem=send_sems.at[k], recv_sem=recv_sems.at[k], device_id=to, device_id_type=MESH)

        mine = pltpu.make_async_copy(x_ref, slot(*me), local_sem)
        mine.start()
        first = [copy(0, me, sibling, src=x_ref)]
        first += [copy(1 + j, me, (*chip, c), src=x_ref) for j, chip in enumerate(chips)]
        for cp in first:
            cp.start()
        passed = [copy(4 + j, (*chip, c), sibling) for j, chip in enumerate(chips)]
        for j, chip in enumerate(chips):
            copy(1 + j, (*chip, c), me).wait_recv()
            passed[j].start()
        copy(0, sibling, me).wait_recv()
        for j, chip in enumerate(chips):
            copy(4 + j, (*chip, 1 - c), me).wait_recv()
        for cp in first + passed:
            cp.wait_send()
        mine.wait()

    return pl.pallas_call(
        body, name=name, out_shape=jax.ShapeDtypeStruct((8,) + blk.shape, blk.dtype),
        in_specs=[pl.BlockSpec(memory_space=space)], out_specs=pl.BlockSpec(memory_space=space),
        scratch_shapes=[pltpu.SemaphoreType.DMA((7,)), pltpu.SemaphoreType.DMA((7,)), pltpu.SemaphoreType.DMA],
        compiler_params=pltpu.CompilerParams(vmem_limit_bytes=VMEM_LIMIT_V7X))(blk)


def _peers(place):
    x, y, c = place
    return [((1 - x) if k & 4 else x, (1 - y) if k & 2 else y, (1 - c) if k & 1 else c) for k in range(1, 8)]


def _slot(p):
    return 4 * p[0] + 2 * p[1] + p[2]


def plan_gather(place, srcs, lands):
    remote = [(s, l.at[_slot(place)], to) for s, l in zip(srcs, lands) for to in _peers(place)]
    return remote, [(s, l.at[_slot(place)]) for s, l in zip(srcs, lands)]


def plan_to_owner(place, srcs, lands):
    remote = [(s.at[2 * to[0] + to[1], to[2]], l.at[_slot(place)], to) for s, l in zip(srcs, lands) for to in _peers(place)]
    x, y, c = place
    return remote, [(s.at[2 * x + y, c], l.at[_slot(place)]) for s, l in zip(srcs, lands)]


def sequencer_exchange(name, collective_id, srcs, land_shapes, plan):
    n = len(srcs)
    src_refs = [jax.new_ref(a, memory_space=pltpu.MemorySpace.HBM) for a in srcs]
    land_refs = [jax.empty_ref(s, memory_space=pltpu.MemorySpace.HBM) for s in land_shapes]

    @pl.kernel(mesh=plsc.ScalarSubcoreMesh(axis_name="sequencer", num_cores=1), name=name,
               scratch_types=(pltpu.SemaphoreType.DMA((7 * n,)), pltpu.SemaphoreType.DMA((7 * n,)),
                              pltpu.SemaphoreType.DMA((n,))),
               compiler_params=pltpu.CompilerParams(collective_id=collective_id))
    def launch(send_sems, recv_sems, local_sems):
        place = _place()
        barrier = pltpu.get_barrier_semaphore()
        for to in _peers(place):
            pl.semaphore_signal(barrier, inc=1, device_id=to, device_id_type=MESH)
        pl.semaphore_wait(barrier, 7)
        remote, local = plan(place, src_refs, land_refs)
        mine = [pltpu.make_async_copy(s, d, local_sems.at[a]) for a, (s, d) in enumerate(local)]
        for cp in mine:
            cp.start()
        cps = [pltpu.make_async_remote_copy(src_ref=s, dst_ref=d, send_sem=send_sems.at[k], recv_sem=recv_sems.at[k],
                                            device_id=to, device_id_type=MESH) for k, (s, d, to) in enumerate(remote)]
        for cp in cps:
            cp.start()
        for cp in mine:
            cp.wait()
        for cp in cps:
            cp.wait()

    launch()
    return land_refs


def sibling_pair(name, hs):
    n = len(hs)

    def body(*refs):
        ins, outs = refs[:n], refs[n:2 * n]
        send_sems, recv_sems = refs[2 * n:]
        x, y, c = _place()
        cps = [pltpu.make_async_remote_copy(src_ref=outs[a].at[c], dst_ref=outs[a].at[c], send_sem=send_sems.at[a],
                                            recv_sem=recv_sems.at[a], device_id=(x, y, 1 - c), device_id_type=MESH)
               for a in range(n)]
        for cp in cps:
            cp.start()
        for a in range(n):
            pltpu.make_async_remote_copy(src_ref=outs[a].at[1 - c], dst_ref=outs[a].at[1 - c], send_sem=send_sems.at[a],
                                         recv_sem=recv_sems.at[a], device_id=(x, y, 1 - c),
                                         device_id_type=MESH).wait_recv()
        for cp in cps:
            cp.wait_send()

    any_spec = pl.BlockSpec(memory_space=pl.ANY)
    return pl.pallas_call(
        body, name=name, out_shape=[jax.ShapeDtypeStruct(h.shape, h.dtype) for h in hs],
        in_specs=[any_spec] * n, out_specs=[any_spec] * n, input_output_aliases={a: a for a in range(n)},
        scratch_shapes=[pltpu.SemaphoreType.DMA((n,)), pltpu.SemaphoreType.DMA((n,))])(*hs)


def owner_sum(name, land):
    _, r, w = land.shape
    T = r // 2

    def body(_, l_ref, o_ref):
        acc = l_ref[0].astype(F32)
        for j in range(1, 8):
            acc = acc + l_ref[j].astype(F32)
        o_ref[...] = acc

    grid_spec = pltpu.PrefetchScalarGridSpec(
        num_scalar_prefetch=1, grid=(2,),
        in_specs=[pl.BlockSpec((8, T, w), lambda i, at: (0, i, 0))],
        out_specs=pl.BlockSpec((None, T, w), lambda i, at: (at[0], i, 0)))
    at = jnp.stack([lax.axis_index("c")]).astype(jnp.int32)
    return pl.pallas_call(body, name=name, grid_spec=grid_spec, out_shape=jax.ShapeDtypeStruct((2, r, w), F32),
                          compiler_params=_cp("arbitrary"))(at, land)


W_IN_RUNS = ((0, 2, 1296, 376), (376, 3, 0, 1672), (2048, 1, 920, 752), (2800, 2, 0, 1296), (4096, 0, 0, 1024),
             (5120, 0, 1024, 648), (5768, 1, 0, 920))


def w_in_to_padded(name, g4):
    T = 128

    def body(g_ref, o_ref):
        o_ref[:, D_PROJ:PW] = jnp.zeros((T, PW - D_PROJ), o_ref.dtype)
        for (a, s, j0, w) in W_IN_RUNS:
            o_ref[:, a:a + w] = g_ref[s, :, j0:j0 + w]

    return pl.pallas_call(body, name=name, grid=(D // T,), in_specs=[pl.BlockSpec((4, T, 1672), lambda i: (0, i, 0))],
                          out_specs=pl.BlockSpec((T, PW), lambda i: (i, 0)),
                          out_shape=jax.ShapeDtypeStruct((D, PW), g4.dtype), compiler_params=_cp("arbitrary"))(g4)


def w_in_from_padded(name, gp):
    T = 128

    def body(g_ref, o_ref):
        for (a, s, j0, w) in W_IN_RUNS:
            o_ref[s, :, j0:j0 + w] = g_ref[:, a:a + w]

    return pl.pallas_call(body, name=name, grid=(D // T,), in_specs=[pl.BlockSpec((T, PW), lambda i: (i, 0))],
                          out_specs=pl.BlockSpec((4, T, 1672), lambda i: (0, i, 0)),
                          out_shape=jax.ShapeDtypeStruct((4, D, 1672), gp.dtype), compiler_params=_cp("arbitrary"))(gp)


def sum_devices(name, g):
    def body(g_ref, o_ref):
        acc = g_ref[0]
        for k in range(1, 8):
            acc = acc + g_ref[k]
        o_ref[...] = acc

    return pl.pallas_call(body, name=name, out_shape=jax.ShapeDtypeStruct(g.shape[1:], F32),
                          compiler_params=pltpu.CompilerParams(vmem_limit_bytes=VMEM_LIMIT_V7X))(g)


def adamw(name, w, g, m, v, T):
    r, wd = w.shape
    c1 = 1.0 - ADAM_B1 ** ADAM_STEP
    c2 = 1.0 - ADAM_B2 ** ADAM_STEP

    def body(w_ref, g_ref, m_ref, v_ref, d_ref, mo_ref, vo_ref):
        gv = g_ref[...]
        mn = ADAM_B1 * m_ref[...] + (1.0 - ADAM_B1) * gv
        vn = ADAM_B2 * v_ref[...] + (1.0 - ADAM_B2) * (gv * gv)
        d_ref[...] = -ADAM_LR * ((mn / c1) / (jnp.sqrt(vn / c2) + ADAM_EPS) + ADAM_WD * w_ref[...])
        mo_ref[...] = mn
        vo_ref[...] = vn

    spec = pl.BlockSpec((T, wd), lambda i: (i, 0))
    return pl.pallas_call(body, name=name, grid=(r // T,), in_specs=[spec] * 4, out_specs=[spec] * 3,
                          out_shape=[jax.ShapeDtypeStruct((r, wd), F32)] * 3, compiler_params=_cp("arbitrary"))(w, g, m, v)


BIG = {'w_in': (1024, 1672), 'w_ssd_proj': (256, 1024), 'w_gm_proj': (256, 1024), 'w_out': (256, 1024),
       'w_ff1': (1024, 704), 'w_ff3': (1024, 704), 'w_ff2': (704, 1024)}


class Flat:
    def __init__(self, segs):
        self.off, o = {}, 0
        for name, size in segs:
            self.off[name] = (o, size)
            o += -(-size // 128) * 128
        self.rows = -(-o // 1024) * 8

    def pack(self, vals):
        parts = []
        for name, (o, size) in self.off.items():
            v = vals[name].reshape(-1).astype(F32)
            parts.append(jnp.pad(v, (0, -(-size // 128) * 128 - size)))
        buf = jnp.concatenate(parts)
        return jnp.pad(buf, (0, self.rows * 128 - buf.shape[0])).reshape(self.rows, 128)

    def get(self, buf, name, shape=None):
        o, size = self.off[name]
        v = buf[o // 128:(o + size + 127) // 128].reshape(-1)[:size]
        return v if shape is None else v.reshape(shape)


PARTIALS = Flat([('loss', 1), ('ln0_g', D), ('ln0_b', D), ('dmod_x', 6 * D), ('dmod_c', 6 * D), ('conv_w', 5 * 1536),
                 ('conv_b', 1536), ('dt_bias', 32), ('a_log', 32), ('d_skip', 32), ('ssd_norm_g', D),
                 ('gm_norm_g', D), ('gm_norm_b', D), ('w_spatial', 8 * Q * Q), ('b_spatial', 8 * Q), ('b_gate', 2 * D),
                 ('ln1_g', D), ('ln1_b', D), ('ln2_g', D), ('ln2_b', D)])

WEIGHTS = ('c_ctx', 'ln0_g', 'ln0_b', 'w_ada', 'b_ada', 'w_in', 'conv_w', 'conv_b', 'dt_bias', 'a_log', 'd_skip',
           'ssd_norm_g', 'gm_norm_g', 'gm_norm_b', 'w_spatial', 'b_spatial', 'b_gate', 'w_ssd_proj', 'w_gm_proj',
           'w_out', 'ln1_g', 'ln1_b', 'w_ff1', 'w_ff3', 'w_ff2', 'ln2_g', 'ln2_b')
BIG_NAMES = tuple(BIG)
SMALL_NAMES = tuple(n for n in WEIGHTS if n not in BIG_NAMES and n != 'w_ada')


def kernel(x, c, ctx, c_ctx, ln0_g, ln0_b, w_ada, b_ada, w_in, conv_w, conv_b, dt_bias, a_log, d_skip, ssd_norm_g, gm_norm_g, gm_norm_b, w_spatial, b_spatial, b_gate, w_ssd_proj, w_gm_proj, w_out, ln1_g, ln1_b, w_ff1, w_ff3, w_ff2, ln2_g, ln2_b, loss_target, m_c_ctx, m_ln0_g, m_ln0_b, m_w_ada, m_b_ada, m_w_in, m_conv_w, m_conv_b, m_dt_bias, m_a_log, m_d_skip, m_ssd_norm_g, m_gm_norm_g, m_gm_norm_b, m_w_spatial, m_b_spatial, m_b_gate, m_w_ssd_proj, m_w_gm_proj, m_w_out, m_ln1_g, m_ln1_b, m_w_ff1, m_w_ff3, m_w_ff2, m_ln2_g, m_ln2_b, v_c_ctx, v_ln0_g, v_ln0_b, v_w_ada, v_b_ada, v_w_in, v_conv_w, v_conv_b, v_dt_bias, v_a_log, v_d_skip, v_ssd_norm_g, v_gm_norm_g, v_gm_norm_b, v_w_spatial, v_b_spatial, v_b_gate, v_w_ssd_proj, v_w_gm_proj, v_w_out, v_ln1_g, v_ln1_b, v_w_ff1, v_w_ff3, v_w_ff2, v_ln2_g, v_ln2_b):
    wts = dict(c_ctx=c_ctx, ln0_g=ln0_g, ln0_b=ln0_b, w_ada=w_ada, b_ada=b_ada, w_in=w_in, conv_w=conv_w, conv_b=conv_b,
               dt_bias=dt_bias, a_log=a_log, d_skip=d_skip, ssd_norm_g=ssd_norm_g, gm_norm_g=gm_norm_g,
               gm_norm_b=gm_norm_b, w_spatial=w_spatial, b_spatial=b_spatial, b_gate=b_gate, w_ssd_proj=w_ssd_proj,
               w_gm_proj=w_gm_proj, w_out=w_out, ln1_g=ln1_g, ln1_b=ln1_b, w_ff1=w_ff1, w_ff3=w_ff3, w_ff2=w_ff2,
               ln2_g=ln2_g, ln2_b=ln2_b)
    ms = dict(zip(WEIGHTS, (m_c_ctx, m_ln0_g, m_ln0_b, m_w_ada, m_b_ada, m_w_in, m_conv_w, m_conv_b, m_dt_bias, m_a_log,
                            m_d_skip, m_ssd_norm_g, m_gm_norm_g, m_gm_norm_b, m_w_spatial, m_b_spatial, m_b_gate,
                            m_w_ssd_proj, m_w_gm_proj, m_w_out, m_ln1_g, m_ln1_b, m_w_ff1, m_w_ff3, m_w_ff2, m_ln2_g,
                            m_ln2_b)))
    vs = dict(zip(WEIGHTS, (v_c_ctx, v_ln0_g, v_ln0_b, v_w_ada, v_b_ada, v_w_in, v_conv_w, v_conv_b, v_dt_bias, v_a_log,
                            v_d_skip, v_ssd_norm_g, v_gm_norm_g, v_gm_norm_b, v_w_spatial, v_b_spatial, v_b_gate,
                            v_w_ssd_proj, v_w_gm_proj, v_w_out, v_ln1_g, v_ln1_b, v_w_ff1, v_w_ff3, v_w_ff2, v_ln2_g,
                            v_ln2_b)))
    px, py, pc = _place()
    shard = 2 * px + py
    dev = 2 * shard + pc
    take = lambda a, i, axis=0: lax.dynamic_index_in_dim(a, i, axis, keepdims=False)

    half = lambda n: take(wts[n][0].reshape(2, BIG[n][0] // 2, BIG[n][1]), pc).astype(BF16)

    pre = jnp.concatenate([c, jnp.pad(conv_w[0], ((0, 0), (0, D - 384))), jnp.zeros((2, D), F32)], axis=0)
    pre = allgather8("gather_cond", pre, False)
    conv_w_full = pre[0::2, 1:6, :384].transpose(1, 0, 2).reshape(5, 1536)
    a16 = jnp.concatenate([_silu(pre[:, 0, :]), _silu(c_ctx)[None], jnp.zeros((7, D), F32)], axis=0)
    mod = matmul("ada_fwd", a16, w_ada[0], 'nn', 16, 512, 1024)
    mod = mod + lax.dynamic_slice_in_dim(b_ada[0], shard * 1536, 1536)[None]
    mod = allgather8("gather_mod", mod, False)
    mod = jnp.concatenate([mod[0], mod[2], mod[4], mod[6]], axis=1)
    mod_x = take(mod, dev).reshape(6, D)
    mod_c = mod[8].reshape(6, D)

    def full(n, blocks):
        r, w = BIG[n]
        return blocks.reshape(4, r, w) if w != D else blocks.reshape(4 * r, w)

    class Exchanges:
        rest_names = BIG_NAMES[1:]

        def __init__(self):
            self.pending = []

        def w_in(self):
            blocks = allgather8("gather_w_in", half('w_in'), True)
            w = w_in_to_padded("w_in_layout", full('w_in', blocks))
            halves = [half(n) for n in self.rest_names]
            halves[0], _ = lax.optimization_barrier((halves[0], blocks))
            lands = [jax.ShapeDtypeStruct((8,) + h.shape, BF16) for h in halves]
            self.rest_refs = sequencer_exchange("gather_rest", 1, halves, lands, plan_gather)
            return w

        def rest(self):
            return {n: full(n, r[...]) for n, r in zip(self.rest_names, self.rest_refs)}

        def grads(self, group, gs):
            if group == 'in':
                gs = {'w_in': w_in_from_padded("w_in_grad_layout", gs['w_in'])}
            blocks = [g.reshape(4, 2, BIG[n][0] // 2, BIG[n][1]) for n, g in gs.items()]
            lands = [jax.ShapeDtypeStruct((8,) + b.shape[2:], BF16) for b in blocks]
            refs = sequencer_exchange("grads_" + group, 2 + len(self.pending), blocks, lands, plan_to_owner)
            self.pending.append((tuple(gs), refs))

        def finish(self):
            names, halves = [], []
            for ns, refs in self.pending:
                names += ns
                halves += [owner_sum("grads_sum_" + n, r[...]) for n, r in zip(ns, refs)]
            return {n: h.reshape(BIG[n]) for n, h in zip(names, sibling_pair("grads_halves", halves))}

    S = dict(ln0_g=ln0_g, ln0_b=ln0_b, conv_w=conv_w_full, conv_b=conv_b[0], dt_bias=dt_bias[0], a_log=a_log[0],
             d_skip=d_skip[0], ssd_norm_g=ssd_norm_g[0], gm_norm_g=gm_norm_g[0], gm_norm_b=gm_norm_b[0],
             w_spatial=w_spatial[0], b_spatial=b_spatial[0], b_gate=b_gate[0], ln1_g=ln1_g[0], ln1_b=ln1_b[0],
             ln2_g=ln2_g[0], ln2_b=ln2_b[0])
    exchanges = Exchanges()
    grad_x, gsmall = core(ctx[0], x[0], loss_target[0], mod_x, mod_c, exchanges, S)

    parts = allgather8("gather_partials", PARTIALS.pack(gsmall), False)
    tot = sum_devices("partials_sum", parts)
    g_shards = exchanges.finish()
    g = {n: PARTIALS.get(tot, n) for n in ('ln0_g', 'ln0_b', 'conv_b', 'dt_bias', 'a_log', 'd_skip', 'ssd_norm_g',
                                           'gm_norm_g', 'gm_norm_b', 'w_spatial', 'b_spatial', 'b_gate', 'ln1_g',
                                           'ln1_b', 'ln2_g', 'ln2_b')}
    loss = PARTIALS.get(tot, 'loss', ())
    dmod_c = PARTIALS.get(tot, 'dmod_c')
    g['b_ada'] = PARTIALS.get(tot, 'dmod_x') + dmod_c
    g['conv_w'] = lax.dynamic_slice_in_dim(PARTIALS.get(tot, 'conv_w', (5, 1536)), shard * 384, 384, axis=1)
    o, size = PARTIALS.off['dmod_x']
    dmod_rows = parts[:, o // 128:(o + size) // 128].reshape(8, size)
    dm = jnp.concatenate([dmod_rows, dmod_c[None], jnp.zeros((7, 6 * D), F32)], axis=0)
    dm = lax.dynamic_slice_in_dim(dm, shard * 1536, 1536, axis=1)
    g['w_ada'] = matmul("ada_bwd_w", a16, dm, 'tn', 1024, 512, 16)
    dm_c = jnp.concatenate([dm[8:9], jnp.zeros((15, 1536), F32)], axis=0)
    dc = matmul("ada_bwd_c", dm_c, w_ada[0], 'nt', 16, 1024, 512)
    dc = allgather8("gather_dcctx", dc, False)[:, 0, :]
    dc = ((dc[0] + dc[2]) + dc[4]) + dc[6]
    sg = jax.nn.sigmoid(c_ctx)
    g['c_ctx'] = dc * (sg * (1.0 + c_ctx * (1.0 - sg)))
    for n in BIG_NAMES:
        g[n] = g_shards[n]

    delta, new_m, new_v = {}, {}, {}
    for n in BIG_NAMES + ('w_ada',):
        r, w = wts[n].shape[1:]
        if w % 128:
            T = max(t for t in range(8, 257, 8) if w % t == 0)
            d_, m_, v_ = adamw("adamw_" + n, wts[n][0].T, g[n].T, ms[n][0].T, vs[n][0].T, T)
            delta[n], new_m[n], new_v[n] = d_.T, m_.T, v_.T
        else:
            T = 352 if n == 'w_ff2' else 256
            delta[n], new_m[n], new_v[n] = adamw("adamw_" + n, wts[n][0], g[n], ms[n][0], vs[n][0], T)
    lay = Flat([(n, wts[n].size) for n in SMALL_NAMES])
    d_, m_, v_ = adamw("adamw_small", lay.pack(wts), lay.pack(g), lay.pack(ms), lay.pack(vs), lay.rows)
    for n in SMALL_NAMES:
        delta[n], new_m[n], new_v[n] = (lay.get(b, n) for b in (d_, m_, v_))

    shp = lambda d: [d[n].reshape(wts[n].shape) for n in WEIGHTS]
    return (loss, grad_x[None], *shp(g), *shp(delta), *shp(new_m), *shp(new_v))
```

```python
import functools

import jax
import jax.numpy as jnp
from jax import lax
from jax.experimental import pallas as pl
from jax.experimental.pallas import tpu as pltpu
from jax.experimental.pallas import tpu_sc as plsc

F32 = jnp.float32
BF16 = jnp.bfloat16
MESH = pl.DeviceIdType.MESH

VMEM_LIMIT_V7X = 56 * 1024 * 1024

D = 1024
LC = 256
Q = 128
NH = 16
D_FF = 2816
LN_EPS = 1e-5
ALPHA = 2.0 ** 0.25

PW = 7168
C_GATE, C_UV, C_Z, C_XBC, C_DT = 0, 2048, 4096, 5120, 6656
D_PROJ = 6688

ADAM_LR, ADAM_B1, ADAM_B2, ADAM_EPS, ADAM_WD, ADAM_STEP = 0.001, 0.9, 0.999, 1e-08, 0.01, 10


def _cp(*sem):
    return pltpu.CompilerParams(dimension_semantics=sem, vmem_limit_bytes=VMEM_LIMIT_V7X)


def _dot(a, b, ca, cb):
    return lax.dot_general(a.astype(BF16), b.astype(BF16), (((ca,), (cb,)), ((), ())),
                           preferred_element_type=F32)


@jax.custom_vjp
def mm(a, b):
    return _dot(a, b, 1, 0)


mm.defvjp(lambda a, b: (_dot(a, b, 1, 0), (a, b)),
          lambda r, g: (_dot(g, r[1], 1, 1), _dot(r[0], g, 0, 0)))


@jax.custom_vjp
def mm_nt(a, b):
    return _dot(a, b, 1, 1)


mm_nt.defvjp(lambda a, b: (_dot(a, b, 1, 1), (a, b)),
             lambda r, g: (_dot(g, r[1], 1, 0), _dot(g, r[0], 0, 0)))


@jax.custom_vjp
def mm_tn(a, b):
    return _dot(a, b, 0, 0)


mm_tn.defvjp(lambda a, b: (_dot(a, b, 0, 0), (a, b)),
             lambda r, g: (_dot(r[1], g, 1, 1), _dot(r[0], g, 1, 0)))


def _dot32(a, b):
    return lax.dot_general(a, b, (((1,), (0,)), ((), ())), precision=lax.Precision.HIGHEST,
                           preferred_element_type=F32)


def _cumsum_fn(rev):
    def tri(transpose):
        r = lax.broadcasted_iota(jnp.int32, (Q, Q), 0)
        c = lax.broadcasted_iota(jnp.int32, (Q, Q), 1)
        keep = (r >= c) if (rev == transpose) else (r <= c)
        return jnp.where(keep, 1.0, 0.0).astype(F32)

    @jax.custom_vjp
    def cums(a):
        return _dot32(tri(False), a)

    cums.defvjp(lambda a: (_dot32(tri(False), a), None), lambda _, g: (_dot32(tri(True), g),))
    return cums


def _cols(v, k):
    w = v.shape[1] // k
    return tuple(v[:, w * i:w * (i + 1)] for i in range(k))


def _splitter(k):
    @jax.custom_vjp
    def split(v):
        return _cols(v, k)

    @jax.custom_vjp
    def concat(ps):
        return jnp.concatenate(ps, axis=1)

    split.defvjp(lambda v: (_cols(v, k), None), lambda _, g: (jnp.concatenate(g, axis=1),))
    concat.defvjp(lambda ps: (jnp.concatenate(ps, axis=1), None), lambda _, g: (_cols(g, k),))
    return split, concat


split2, _ = _splitter(2)
split4, _ = _splitter(4)
split8, concat8 = _splitter(8)


def _ln(x, g, b):
    mu = jnp.mean(x, axis=-1, keepdims=True)
    xc = x - mu
    var = jnp.mean(xc * xc, axis=-1, keepdims=True)
    return xc * lax.rsqrt(var + LN_EPS) * g + b


def _silu(x):
    return x * jax.nn.sigmoid(x)


def _gelu(x):
    return 0.5 * x * (1.0 + jnp.tanh(0.7978845608028654 * (x + 0.044715 * (x * x * x))))


def _xspec(T, w, col, roff):
    return pl.BlockSpec((T, w), lambda i, col=col, roff=roff: (jnp.maximum(i + roff, 0), col))


def _pspec(p, sel):
    if sel is None:
        return pl.BlockSpec(p.shape, lambda i, n=p.ndim: (0,) * n)
    return pl.BlockSpec((1,) + p.shape[1:], lambda i, n=p.ndim: (sel(i),) + (0,) * (n - 1))


def _out_plumbing(outs, T, args, in_specs):
    shapes, specs, aliases = [], [], {}
    for k, o in enumerate(outs):
        if o[0] == 'new':
            _, rows, w, roff = o[:4]
            shapes.append(jax.ShapeDtypeStruct((rows, w), o[4] if len(o) > 4 else F32))
            specs.append(_xspec(T, w, 0, roff))
        elif o[0] == 'acc':
            shapes.append(jax.ShapeDtypeStruct(o[1], F32))
            specs.append(pl.BlockSpec(o[1], lambda i, n=len(o[1]): (0,) * n))
        elif o[0] == 'part':
            _, rows, wtot, w, col, roff, dtype = o
            shapes.append(jax.ShapeDtypeStruct((rows, wtot), dtype))
            specs.append(_xspec(T, w, col, roff))
        else:
            _, arr, w, col, roff = o
            aliases[len(args)] = k
            args.append(arr)
            in_specs.append(pl.BlockSpec(memory_space=pl.ANY))
            shapes.append(jax.ShapeDtypeStruct(arr.shape, arr.dtype))
            specs.append(_xspec(T, w, col, roff))
    return shapes, specs, aliases


def stage_fwd(name, f, T, n, xs, ps, outs, rows=None, gather=None):
    nx, npar = len(xs), len(ps)
    args = [x[0] for x in xs] + [p[0] for p in ps]
    in_specs = [_xspec(T, w, col, roff) for (_, w, col, roff) in xs] + [_pspec(p, sel) for (p, sel) in ps]
    n_in = len(args)
    shapes, specs, aliases = _out_plumbing(outs, T, args, in_specs)
    n_all_in = len(args)
    scratch = []
    if gather is not None:
        args.append(gather)
        in_specs.append(pl.BlockSpec(memory_space=pl.ANY))
        shapes.append(jax.ShapeDtypeStruct((8,) + gather.shape, gather.dtype))
        specs.append(pl.BlockSpec(memory_space=pl.ANY))
        scratch = [pltpu.SemaphoreType.DMA((7,)), pltpu.SemaphoreType.DMA((7,)), pltpu.SemaphoreType.DMA]

    def exchange(i, blk_ref, out_ref, send_sems, recv_sems, local_sem):
        x, y, c = _place()
        me, sibling = (x, y, c), (x, y, 1 - c)
        chips = [(1 - x, y), (x, 1 - y), (1 - x, 1 - y)]

        def copy(k, block, to, own=False):
            dst = out_ref.at[_slot(block)]
            return pltpu.make_async_remote_copy(src_ref=blk_ref if own else dst, dst_ref=dst, send_sem=send_sems.at[k],
                                                recv_sem=recv_sems.at[k], device_id=to, device_id_type=MESH)

        mine = pltpu.make_async_copy(blk_ref, out_ref.at[_slot(me)], local_sem)
        first = [copy(0, me, sibling, own=True)] + [copy(1 + j, me, (*ch, c), own=True) for j, ch in enumerate(chips)]
        passed = [copy(4 + j, (*ch, c), sibling) for j, ch in enumerate(chips)]

        @pl.when(i == 0)
        def _():
            mine.start()
            for cp in first:
                cp.start()

        @pl.when(i == n // 2)
        def _():
            for j, ch in enumerate(chips):
                copy(1 + j, (*ch, c), me).wait_recv()
                passed[j].start()

        @pl.when(i == n - 1)
        def _():
            copy(0, sibling, me).wait_recv()
            for j, ch in enumerate(chips):
                copy(4 + j, (*ch, 1 - c), me).wait_recv()
            for cp in first + passed:
                cp.wait_send()
            mine.wait()

    n_out = len(outs)

    def body(*refs):
        i = pl.program_id(0)
        if gather is not None:
            exchange(i, refs[n_all_in], refs[n_all_in + 1 + n_out], *refs[n_all_in + 2 + n_out:])
        out_refs = refs[len(args):len(args) + n_out]
        pv = [r[...] if ps[k][1] is None else r[0] for k, r in enumerate(refs[nx:n_in])]
        sums = {}
        for r0 in range(0, T, rows or T):
            g = slice(r0, r0 + (rows or T))
            res = f(*[r[g, :] for r in refs[:nx]], *pv)
            for k, o_ref in enumerate(out_refs):
                if outs[k][0] == 'acc':
                    sums[k] = res[k] if r0 == 0 else sums[k] + res[k]
                else:
                    o_ref[g, :] = res[k].astype(o_ref.dtype)
        for k, v in sums.items():
            o_ref = out_refs[k]

            @pl.when(i == 0)
            def _(o_ref=o_ref, v=v):
                o_ref[...] = v

            @pl.when(i > 0)
            def _(o_ref=o_ref, v=v):
                o_ref[...] += v

    return pl.pallas_call(body, name=name, grid=(n,), in_specs=in_specs, out_specs=specs, out_shape=shapes,
                          input_output_aliases=aliases, scratch_shapes=scratch,
                          compiler_params=_cp("arbitrary"))(*args)


def stage_bwd(name, f, T, n, xs, ps, cts, dxs, dps, primal=(), rows=None):
    nx, npar = len(xs), len(ps)
    args = [x[0] for x in xs] + [p[0] for p in ps]
    in_specs = [_xspec(T, w, col, roff) for (_, w, col, roff) in xs] + [_pspec(p, sel) for (p, sel) in ps]
    ct_arrs = [c for c in cts if isinstance(c, tuple)]
    for (a, w, col, roff) in ct_arrs:
        args.append(a)
        in_specs.append(_xspec(T, w, col, roff))
    n_in = len(args)
    outs, out_of = [], []
    for k, o in enumerate(dxs):
        if o is not None:
            outs.append(o)
            out_of.append(('x', k))
    for k, want in enumerate(dps):
        if want:
            p, sel = ps[k]
            outs.append(('acc', p.shape))
            out_of.append(('p', k))
    for k, shape in primal:
        outs.append(('acc', shape))
        out_of.append(('r', k))
    shapes, specs, aliases = _out_plumbing(outs, T, args, in_specs)
    for j, (kind, k) in enumerate(out_of):
        if kind == 'p' and ps[k][1] is not None:
            p, sel = ps[k]
            specs[j] = pl.BlockSpec((1,) + p.shape[1:], lambda i, n=p.ndim, sel=sel: (sel(i),) + (0,) * (n - 1))
    n_all_in = len(args)

    def body(*refs):
        i = pl.program_id(0)
        pv = [r[...] if ps[k][1] is None else r[0] for k, r in enumerate(refs[nx:nx + npar])]
        sums = {}
        for r0 in range(0, T, rows or T):
            g = slice(r0, r0 + (rows or T))
            res, vjp_fn = jax.vjp(f, *[r[g, :] for r in refs[:nx]], *pv)
            ctv, q = [], nx + npar
            for k, c in enumerate(cts):
                if c is None:
                    ctv.append(jnp.zeros_like(res[k]))
                elif isinstance(c, tuple):
                    v = refs[q][g, :]
                    if c[3] < 0:
                        v = v * (i + c[3] >= 0).astype(F32)
                    ctv.append(v)
                    q += 1
                else:
                    ctv.append(jnp.full_like(res[k], c))
            grads = vjp_fn(tuple(ctv))
            for j, o_ref in enumerate(refs[n_all_in:]):
                kind, k = out_of[j]
                if kind == 'x':
                    o_ref[g, :] = grads[k].astype(o_ref.dtype)
                else:
                    v = res[k] if kind == 'r' else grads[nx + k]
                    sums[j] = v if r0 == 0 else sums[j] + v
        for j, v in sums.items():
            kind, k = out_of[j]
            o_ref = refs[n_all_in + j]
            sel = None if kind == 'r' else ps[k][1]
            if sel is None:
                first, tgt = i == 0, o_ref
            else:
                first, tgt = jnp.logical_or(i == 0, sel(i) != sel(jnp.maximum(i - 1, 0))), o_ref.at[0]

            @pl.when(first)
            def _(tgt=tgt, v=v):
                tgt[...] = v

            @pl.when(jnp.logical_not(first))
            def _(tgt=tgt, v=v):
                tgt[...] += v

    return pl.pallas_call(body, name=name, grid=(n,), in_specs=in_specs, out_specs=specs, out_shape=shapes,
                          input_output_aliases=aliases, compiler_params=_cp("arbitrary"))(*args)


_CONTRACT = {'nn': (1, 0), 'nt': (1, 1), 'tn': (0, 0)}


def matmul(name, a, b, mode, tm, tn, tk, out_dtype=F32, add=None):
    if mode == 'nn':
        (M, K), (_, N) = a.shape, b.shape
    elif mode == 'nt':
        (M, K), (N, _) = a.shape, b.shape
    else:
        (K, M), (_, N) = a.shape, b.shape
    assert M % tm == 0 and N % tn == 0 and K % tk == 0, (name, M, N, K, tm, tn, tk)
    a_spec = (pl.BlockSpec((tk, tm), lambda j, i, k: (k, i)) if mode == 'tn'
              else pl.BlockSpec((tm, tk), lambda j, i, k: (i, k)))
    b_spec = (pl.BlockSpec((tn, tk), lambda j, i, k: (j, k)) if mode == 'nt'
              else pl.BlockSpec((tk, tn), lambda j, i, k: (k, j)))
    o_spec = pl.BlockSpec((tm, tn), lambda j, i, k: (i, j))
    return matmul_call(name, (N // tn, M // tm, K // tk), a, a_spec, b, b_spec, (M, N), o_spec, (tm, tn), mode,
                       out_dtype, add)


def matmul_call(name, grid, a, a_spec, b, b_spec, out_shape, o_spec, tile, mode, out_dtype=F32, add=None):
    tm, tn = tile
    nk = grid[2]
    ca, cb = _CONTRACT[mode]
    args, in_specs = [a, b], [a_spec, b_spec]
    if add is not None:
        args.append(add)
        in_specs.append(o_spec)

    def body(*refs):
        a_ref, b_ref = refs[0], refs[1]
        o_ref, acc = refs[-2], refs[-1]
        k = pl.program_id(2)
        if nk == 1:
            p = _dot(a_ref[...], b_ref[...], ca, cb)
            o_ref[...] = (p + refs[2][...] if add is not None else p).astype(out_dtype)
            return

        @pl.when(k == 0)
        def _():
            acc[...] = refs[2][...] if add is not None else jnp.zeros_like(acc)

        acc[...] += _dot(a_ref[...], b_ref[...], ca, cb)

        @pl.when(k == nk - 1)
        def _():
            o_ref[...] = acc[...].astype(out_dtype)

    return pl.pallas_call(body, name=name, grid=grid, in_specs=in_specs, out_specs=o_spec,
                          out_shape=jax.ShapeDtypeStruct(out_shape, out_dtype),
                          scratch_shapes=[pltpu.VMEM((tm, tn) if nk > 1 else (8, 128), F32)],
                          compiler_params=_cp("arbitrary", "arbitrary", "arbitrary"))(*args)


NS, WS = 4, 704


def _resident(name, M, tm, rows, weight, out_shape, out_block, out_map, step, add=None):
    args = [rows[0], weight] + ([] if add is None else [add])
    in_specs = [pl.BlockSpec(rows[1], rows[2]), pl.BlockSpec(weight.shape, lambda i, n=weight.ndim: (0,) * n)]
    if add is not None:
        in_specs.append(pl.BlockSpec(out_block, out_map))
    return pl.pallas_call(step, name=name, grid=(M // tm,), in_specs=in_specs, out_specs=pl.BlockSpec(out_block, out_map),
                          out_shape=jax.ShapeDtypeStruct(out_shape, F32), compiler_params=_cp("arbitrary"))(*args)


def ffn_in_fwd(name, h, w1, w3, tm):
    M = h.shape[0]

    def step(h_ref, w1_ref, w3_ref, a1_ref, a3_ref, act_ref):
        for s in range(NS):
            a1 = _dot(h_ref[...], w1_ref[s], 1, 0)
            a3 = _dot(h_ref[...], w3_ref[s], 1, 0)
            a1_ref[s] = a1.astype(a1_ref.dtype)
            a3_ref[s] = a3.astype(a3_ref.dtype)
            act_ref[s] = (_silu(a1) * a3).astype(act_ref.dtype)

    wspec = pl.BlockSpec((NS, D, WS), lambda i: (0, 0, 0))
    ospec = pl.BlockSpec((NS, tm, WS), lambda i: (0, i, 0))
    return pl.pallas_call(
        step, name=name, grid=(M // tm,), in_specs=[pl.BlockSpec((tm, D), lambda i: (i, 0)), wspec, wspec],
        out_specs=[ospec, ospec, ospec],
        out_shape=[jax.ShapeDtypeStruct((NS, M, WS), BF16)] * 3, compiler_params=_cp("arbitrary"))(h, w1, w3)


def ffn_out_bwd_x(name, dff, w2, a1, a3, tm):
    M = dff.shape[0]

    def step(d_ref, w_ref, a1_ref, a3_ref, da1_ref, da3_ref):
        for s in range(NS):
            dact = _dot(d_ref[...], w_ref[s * WS:(s + 1) * WS, :], 1, 1)
            a1 = a1_ref[s].astype(F32)
            sig = jax.nn.sigmoid(a1)
            da3_ref[s] = (dact * (a1 * sig)).astype(da3_ref.dtype)
            da1_ref[s] = (dact * a3_ref[s].astype(F32) * (sig * (1.0 + a1 * (1.0 - sig)))).astype(da1_ref.dtype)

    aspec = pl.BlockSpec((NS, tm, WS), lambda i: (0, i, 0))
    return pl.pallas_call(
        step, name=name, grid=(M // tm,),
        in_specs=[pl.BlockSpec((tm, D), lambda i: (i, 0)), pl.BlockSpec(w2.shape, lambda i: (0, 0)), aspec, aspec],
        out_specs=[aspec, aspec],
        out_shape=[jax.ShapeDtypeStruct((NS, M, WS), BF16)] * 2, compiler_params=_cp("arbitrary"))(dff, w2, a1, a3)


def ff_in_bwd_x(name, da3, w3, tm, add=None):
    M = da3.shape[1]

    def step(*refs):
        d_ref, w_ref, o_ref = refs[0], refs[1], refs[-1]
        acc = _dot(d_ref[0], w_ref[0], 1, 1)
        for s in range(1, NS):
            acc = acc + _dot(d_ref[s], w_ref[s], 1, 1)
        o_ref[...] = acc if add is None else acc + refs[2][...]

    return _resident(name, M, tm, (da3, (NS, tm, WS), lambda i: (0, i, 0)), w3, (M, D), (tm, D), lambda i: (i, 0), step, add)


def ff_in_bwd_w(name, h, da3, tk):
    M = h.shape[0]

    def step(h_ref, d_ref, acc):
        for s in range(NS):
            acc[s] += _dot(h_ref[...], d_ref[s], 0, 0)

    return _token_sum(name, M // tk, [pl.BlockSpec((tk, D), lambda k: (k, 0)), pl.BlockSpec((NS, tk, WS), lambda k: (0, k, 0))],
                      (NS, D, WS), (NS, D, WS), step, (h, da3))


def ff_out_fwd(name, act3, w2, tm):
    M = act3.shape[1]

    def step(a_ref, w_ref, o_ref):
        acc = _dot(a_ref[0], w_ref[0:WS, :], 1, 0)
        for s in range(1, NS):
            acc = acc + _dot(a_ref[s], w_ref[s * WS:(s + 1) * WS, :], 1, 0)
        o_ref[...] = acc

    return _resident(name, M, tm, (act3, (NS, tm, WS), lambda i: (0, i, 0)), w2, (M, D), (tm, D), lambda i: (i, 0), step)


def _token_sum(name, nk, in_specs, out_shape, acc_shape, step, args):
    def body(*refs):
        o_ref, acc = refs[-2], refs[-1]
        k = pl.program_id(0)

        @pl.when(k == 0)
        def _():
            acc[...] = jnp.zeros_like(acc)

        step(*refs[:-2], acc)

        @pl.when(k == nk - 1)
        def _():
            o_ref[...] = acc[...].astype(o_ref.dtype)

    return pl.pallas_call(body, name=name, grid=(nk,), in_specs=in_specs,
                          out_specs=pl.BlockSpec(out_shape, lambda k, n=len(out_shape): (0,) * n),
                          out_shape=jax.ShapeDtypeStruct(out_shape, BF16), scratch_shapes=[pltpu.VMEM(acc_shape, F32)],
                          compiler_params=_cp("arbitrary"))(*args)


def ff_out_bwd_w(name, act3, dff, tk):
    M = dff.shape[0]

    def step(a_ref, d_ref, acc):
        for s in range(NS):
            acc[s * WS:(s + 1) * WS, :] += _dot(a_ref[s], d_ref[...], 0, 0)

    return _token_sum(name, M // tk, [pl.BlockSpec((NS, tk, WS), lambda k: (0, k, 0)), pl.BlockSpec((tk, D), lambda k: (k, 0))],
                      (NS * WS, D), (NS * WS, D), step, (act3, dff))


HALO = 8
CONV_ROWS = 128


def _fill(pad_ref, n, v=None):
    edge = jnp.zeros((HALO, 128), F32)
    pad_ref[0:HALO, :] = edge
    pad_ref[HALO + n:2 * HALO + n, :] = edge
    if v is not None:
        pad_ref[HALO:HALO + n, :] = v


def _tap(pad_ref, r0, k, rows=CONV_ROWS):
    return pad_ref[HALO + r0 + k - 2:HALO + r0 + k - 2 + rows, :]


def _conv_pre(pad_ref, r0, w_ref, b_ref):
    acc = _tap(pad_ref, r0, 0) * w_ref[0:1, :] + b_ref[...]
    for k in range(1, 5):
        acc = acc + _tap(pad_ref, r0, k) * w_ref[k:k + 1, :]
    return acc


def conv_fwd(name, proj, conv_w, conv_b, R):
    segs = ((0, LC), (LC, R))

    def body(x_ref, w_ref, b_ref, o_ref, xp):
        for (s, e) in segs:
            _fill(xp, e - s, x_ref[s:e, :])
            for r0 in range(0, e - s, CONV_ROWS):
                o_ref[s + r0:s + r0 + CONV_ROWS, :] = _silu(_conv_pre(xp, r0, w_ref, b_ref))

    return pl.pallas_call(
        body, name=name, grid=(12,),
        in_specs=[pl.BlockSpec((R, 128), lambda j: (0, C_XBC // 128 + j)),
                  pl.BlockSpec((8, 128), lambda j: (0, j)), pl.BlockSpec((1, 128), lambda j: (0, j))],
        out_specs=pl.BlockSpec((R, 128), lambda j: (0, j)),
        out_shape=jax.ShapeDtypeStruct((R, 1536), F32), scratch_shapes=[pltpu.VMEM((R - LC + 2 * HALO, 128), F32)],
        compiler_params=_cp("arbitrary"))(proj, conv_w, conv_b)


def conv_bwd(name, proj, conv_w, conv_b, d_f, d_b, d_skip, dproj, R):
    segs = ((0, LC), (LC, R))

    def body(x_ref, w_ref, b_ref, df_ref, db_ref, ds_ref, _, dx_ref, dw_ref, dbias_ref, xp, dp):
        j = pl.program_id(0)
        has_skip = (j < 8).astype(F32)
        dw = [jnp.zeros((8, 128), F32) for _ in range(5)]
        dbias = jnp.zeros((8, 128), F32)
        fold = lambda v: jnp.sum(v.reshape(CONV_ROWS // 8, 8, 128), axis=0)
        for (s, e) in segs:
            n = e - s
            _fill(xp, n, x_ref[s:e, :])
            _fill(dp, n)
            for r0 in range(0, n, CONV_ROWS):
                rows = slice(s + r0, s + r0 + CONV_ROWS)
                pre = _conv_pre(xp, r0, w_ref, b_ref)
                sig = jax.nn.sigmoid(pre)
                dy = df_ref[rows, :] + db_ref[rows, :]
                if s == LC:
                    dy = dy + ds_ref[r0:r0 + CONV_ROWS, :] * has_skip
                dpre = dy * (sig * (1.0 + pre * (1.0 - sig)))
                dp[HALO + r0:HALO + r0 + CONV_ROWS, :] = dpre
                dbias = dbias + fold(dpre)
            for r0 in range(0, n, CONV_ROWS):
                x = x_ref[s + r0:s + r0 + CONV_ROWS, :]
                dx = jnp.zeros_like(x)
                for k in range(5):
                    d = _tap(dp, r0, 4 - k)
                    dx = dx + d * w_ref[k:k + 1, :]
                    dw[k] = dw[k] + fold(d * x)
                dx_ref[s + r0:s + r0 + CONV_ROWS, :] = dx.astype(dx_ref.dtype)
        dw_ref[...] = jnp.zeros_like(dw_ref)
        for k in range(5):
            dw_ref[k:k + 1, :] = jnp.sum(dw[k], axis=0, keepdims=True)
        dbias_ref[...] = jnp.sum(dbias, axis=0, keepdims=True)

    pad = pltpu.VMEM((R - LC + 2 * HALO, 128), F32)
    return pl.pallas_call(
        body, name=name, grid=(12,),
        in_specs=[pl.BlockSpec((R, 128), lambda j: (0, C_XBC // 128 + j)),
                  pl.BlockSpec((8, 128), lambda j: (0, j)), pl.BlockSpec((1, 128), lambda j: (0, j)),
                  pl.BlockSpec((R, 128), lambda j: (0, j)), pl.BlockSpec((R, 128), lambda j: (0, j)),
                  pl.BlockSpec((R - LC, 128), lambda j: (0, jnp.minimum(j, 7))),
                  pl.BlockSpec(memory_space=pl.ANY)],
        out_specs=[pl.BlockSpec((R, 128), lambda j: (0, C_XBC // 128 + j)),
                   pl.BlockSpec((8, 128), lambda j: (0, j)), pl.BlockSpec((1, 128), lambda j: (0, j))],
        out_shape=[jax.ShapeDtypeStruct(dproj.shape, dproj.dtype), jax.ShapeDtypeStruct((8, 1536), F32),
                   jax.ShapeDtypeStruct((1, 1536), F32)], scratch_shapes=[pad, pad],
        input_output_aliases={6: 0}, compiler_params=_cp("arbitrary"))(proj, conv_w, conv_b, d_f, d_b, d_skip, dproj)


def _ssd_chunk(rev, dirn):
    cums = _cumsum_fn(rev)

    def f(xs, Bs, Cs, dt, alog, Hs):
        lane = lax.broadcasted_iota(jnp.int32, (1, 128), 1)
        sub = lax.broadcasted_iota(jnp.int32, (Q, 1), 0)
        r = lax.broadcasted_iota(jnp.int32, (Q, Q), 0)
        c = lax.broadcasted_iota(jnp.int32, (Q, Q), 1)
        mask = (r <= c) if rev else (r >= c)
        left = lane < 64
        a = dt * (-jnp.exp(alog))
        s = cums(a)
        sT, dtT = s.T, dt.T
        last_row = (sub == (0 if rev else Q - 1)).astype(F32)
        s_last = jnp.sum(s * last_row, axis=0, keepdims=True)
        G = [mm_nt(Cs[g], Bs[g]) for g in range(2)]
        M, es, wc, ed = [], [], [], []
        for h in range(NH):
            l = 16 * dirn + h
            oh_l = (lane == l).astype(F32)
            oh_s = (sub == l).astype(F32)
            s_col = jnp.sum(s * oh_l, axis=1, keepdims=True)
            dt_col = jnp.sum(dt * oh_l, axis=1, keepdims=True)
            s_row = jnp.sum(sT * oh_s, axis=0, keepdims=True)
            dt_row = jnp.sum(dtT * oh_s, axis=0, keepdims=True)
            sl = jnp.sum(s_last * oh_l, axis=1, keepdims=True)
            seg = jnp.where(mask, s_col - s_row, 0.0)
            lm = jnp.where(mask, jnp.exp(seg), 0.0)
            M.append(G[h // 8] * lm * dt_row)
            es.append(jnp.exp(s_col))
            wc.append(jnp.exp(sl - s_col) * dt_col)
            ed.append(jnp.exp(sl))
        Ys, Hn = [], []
        for j in range(8):
            g = j // 4
            xa = jnp.where(left, xs[j], 0.0)
            xb = jnp.where(left, 0.0, xs[j])
            yd = mm(M[2 * j], xa) + mm(M[2 * j + 1], xb)
            yo = mm(Cs[g], Hs[j]) * jnp.where(left, es[2 * j], es[2 * j + 1])
            Ys.append(yd + yo)
            st = mm_tn(Bs[g], xs[j] * jnp.where(left, wc[2 * j], wc[2 * j + 1]))
            Hn.append(Hs[j] * jnp.where(left, ed[2 * j], ed[2 * j + 1]) + st)
        return Ys, Hn

    return f


def _chunk_of(t, n, rev):
    if not rev:
        return t
    return jnp.where(t < 2, 1 - t, n + 1 - t)


def _cols128(ref, k, lead=()):
    return [ref[lead + (slice(None), slice(128 * j, 128 * (j + 1)))] for j in range(k)]


def ssd_fwd(name, xbc, dts, alog, n, dirs):
    nd = len(dirs)
    chunks = [_ssd_chunk(rev, dirn) for rev, dirn in dirs]

    def body(*refs):
        al_ref = refs[4 * nd]
        for d in range(nd):
            x_ref, b_ref, c_ref, dt_ref = refs[4 * d:4 * d + 4]
            y_ref, hs_ref = refs[4 * nd + 1 + 2 * d:4 * nd + 3 + 2 * d]
            h_scr = refs[4 * nd + 1 + 2 * nd + d]

            @pl.when(pl.program_id(0) == 0)
            def _(h_scr=h_scr):
                h_scr[...] = jnp.zeros_like(h_scr)

            hs_ref[0] = h_scr[...]
            Ys, Hn = chunks[d](_cols128(x_ref, 8), _cols128(b_ref, 2), _cols128(c_ref, 2), dt_ref[...], al_ref[...],
                               _cols128(h_scr, 8))
            for j in range(8):
                y_ref[:, 128 * j:128 * (j + 1)] = Ys[j]
                h_scr[:, 128 * j:128 * (j + 1)] = Hn[j]

    in_specs, out_specs, out_shape, args = [], [], [], []
    for (rev, _), dt in zip(dirs, dts):
        cm = lambda t, rev=rev: _chunk_of(t, n, rev)
        in_specs += [pl.BlockSpec((Q, 1024), lambda t, cm=cm: (cm(t), 0)), pl.BlockSpec((Q, 256), lambda t, cm=cm: (cm(t), 4)),
                     pl.BlockSpec((Q, 256), lambda t, cm=cm: (cm(t), 5)), pl.BlockSpec((Q, 128), lambda t, cm=cm: (cm(t), 0))]
        args += [xbc, xbc, xbc, dt]
        out_specs += [pl.BlockSpec((Q, 1024), lambda t, cm=cm: (cm(t), 0)),
                      pl.BlockSpec((1, Q, 1024), lambda t, cm=cm: (cm(t), 0, 0))]
        out_shape += [jax.ShapeDtypeStruct((n * Q, 1024), F32), jax.ShapeDtypeStruct((n, Q, 1024), F32)]
    res = pl.pallas_call(
        body, name=name, grid=(n,), in_specs=in_specs + [pl.BlockSpec((1, 128), lambda t: (0, 0))],
        out_specs=out_specs, out_shape=out_shape, scratch_shapes=[pltpu.VMEM((Q, 1024), F32)] * nd,
        compiler_params=_cp("arbitrary"))(*args, alog)
    return [res[2 * d:2 * d + 2] for d in range(nd)]


def ssd_bwd(name, xbc, dts, alog, hss, dy, n, dirs):
    nd = len(dirs)
    chunks = [_ssd_chunk(rev, dirn) for rev, dirn in dirs]

    def body(*refs):
        tt = pl.program_id(0)
        al_ref = refs[6 * nd]
        for d, (rev, _) in enumerate(dirs):
            x_ref, b_ref, c_ref, dt_ref, hs_ref, dy_ref = refs[6 * d:6 * d + 6]
            dx_ref, ddt_ref, dal_ref = refs[6 * nd + 1 + 3 * d:6 * nd + 4 + 3 * d]
            dh_scr = refs[6 * nd + 1 + 3 * nd + d]
            ch = _chunk_of(n - 1 - tt, n, rev)

            @pl.when(tt == 0)
            def _(dh_scr=dh_scr):
                dh_scr[...] = jnp.zeros_like(dh_scr)

            live = (ch >= 2).astype(F32)
            dYs = [v * live for v in _cols128(dy_ref, 8)]
            _, vjp_fn = jax.vjp(chunks[d], _cols128(x_ref, 8), _cols128(b_ref, 2), _cols128(c_ref, 2), dt_ref[...],
                                al_ref[...], _cols128(hs_ref, 8, (0,)))
            dxs, dBs, dCs, ddt, dal, dHs = vjp_fn((dYs, _cols128(dh_scr, 8)))
            for j in range(8):
                dx_ref[:, 128 * j:128 * (j + 1)] = dxs[j]
                dh_scr[:, 128 * j:128 * (j + 1)] = dHs[j]
            for g in range(2):
                dx_ref[:, 1024 + 128 * g:1024 + 128 * (g + 1)] = dBs[g]
                dx_ref[:, 1280 + 128 * g:1280 + 128 * (g + 1)] = dCs[g]
            ddt_ref[...] = ddt

            @pl.when(tt == 0)
            def _(dal_ref=dal_ref, dal=dal):
                dal_ref[...] = dal

            @pl.when(tt > 0)
            def _(dal_ref=dal_ref, dal=dal):
                dal_ref[...] += dal

    in_specs, out_specs, out_shape, args = [], [], [], []
    for (rev, _), dt, hs in zip(dirs, dts, hss):
        cm = lambda t, rev=rev: _chunk_of(n - 1 - t, n, rev)
        in_specs += [pl.BlockSpec((Q, 1024), lambda t, cm=cm: (cm(t), 0)), pl.BlockSpec((Q, 256), lambda t, cm=cm: (cm(t), 4)),
                     pl.BlockSpec((Q, 256), lambda t, cm=cm: (cm(t), 5)), pl.BlockSpec((Q, 128), lambda t, cm=cm: (cm(t), 0)),
                     pl.BlockSpec((1, Q, 1024), lambda t, cm=cm: (cm(t), 0, 0)),
                     pl.BlockSpec((Q, 1024), lambda t, cm=cm: (jnp.maximum(cm(t) - 2, 0), 0))]
        args += [xbc, xbc, xbc, dt, hs, dy]
        out_specs += [pl.BlockSpec((Q, 1536), lambda t, cm=cm: (cm(t), 0)), pl.BlockSpec((Q, 128), lambda t, cm=cm: (cm(t), 0)),
                      pl.BlockSpec((1, 128), lambda t: (0, 0))]
        out_shape += [jax.ShapeDtypeStruct((n * Q, 1536), F32), jax.ShapeDtypeStruct((n * Q, 128), F32),
                      jax.ShapeDtypeStruct((1, 128), F32)]
    res = pl.pallas_call(
        body, name=name, grid=(n,), in_specs=in_specs + [pl.BlockSpec((1, 128), lambda t: (0, 0))],
        out_specs=out_specs, out_shape=out_shape, scratch_shapes=[pltpu.VMEM((Q, 1024), F32)] * nd,
        compiler_params=_cp("arbitrary"))(*args, alog)
    return [res[3 * d:3 * d + 3] for d in range(nd)]


def f_norm0(c, x, g0, b0, sc, sh, is_ctx):
    x0 = _ln(jnp.where(is_ctx > 0.5, c, x), g0, b0)
    return x0, x0 * (1.0 + sc) + sh


def f_dt(raw, bias):
    z = split4(raw)[0] + bias
    dt = jnp.maximum(z, 0.0) + jnp.log1p(jnp.exp(-jnp.abs(z)))
    return dt, dt


def f_gated_norm(yf, yb, xs, z, dcol, g):
    h = (yf + yb + xs * dcol) * _silu(z)
    return (h * lax.rsqrt(jnp.mean(h * h, axis=-1, keepdims=True) + LN_EPS) * g,)


def f_gmlp(uv, gmg, gmb, *wb):
    ws, bs = wb[:8], wb[8:]
    u, v = split2(uv)
    vn = split8(_ln(_gelu(v), gmg, gmb))
    mixed = concat8(tuple(mm(ws[g], vn[g]) + bs[g] for g in range(8)))
    return (_gelu(u) * mixed,)


def f_merge(ps, pg, gates, bg):
    gs, gg = split2(jax.nn.sigmoid(gates + bg))
    return (gs * ps + gg * pg,)


def f_res1(x0, out, g1, lg, lb, sc, sh):
    x1 = _ln(ALPHA * x0 + g1 * out, lg, lb)
    return x1, x1 * (1.0 + sc) + sh


def f_res2_loss(x1, ff, tgt, g2, lg, lb):
    x2 = _ln(ALPHA * x1 + g2 * ff, lg, lb)
    e = x2 - tgt
    return (0.5 * jnp.sum(jnp.mean(e * e, axis=-1, keepdims=True), axis=0, keepdims=True),)


def _row_tile(M):
    return 544 if M % 544 == 0 else (512 if M % 512 == 0 else M)


def core(ctx, x, tgt, mod_x, mod_c, X, S):
    L = x.shape[0]
    R = LC + L
    n = R // Q
    T = 256
    nt, ntl = R // T, L // T
    tmR, tmL = _row_tile(R), _row_tile(L)
    tmS = 1024 if L % 1024 == 0 else tmL
    tkR = 256 if R % 512 else 512
    tkL = 512 if L % 512 == 0 else 256
    row = lambda v: v.reshape(1, -1)
    mx = [row(mod_x[k]) for k in range(6)]
    mc = [row(mod_c[k]) for k in range(6)]
    sel = lambda i: jnp.minimum(i, 1)
    sc1 = jnp.stack([mc[1], mx[1]])
    sh1 = jnp.stack([mc[0], mx[0]])
    ln0 = [(row(S['ln0_g']), None), (row(S['ln0_b']), None), (sc1, sel), (sh1, sel),
           (jnp.array([1.0, 0.0], F32).reshape(2, 1, 1), sel)]
    x_n0 = [(ctx, D, 0, -nt), (x, D, 0, -1)]

    x0, xm, *landed = stage_fwd("norm0_fwd", f_norm0, T, nt, x_n0, ln0, [('new', R, D, 0), ('new', R, D, 0, BF16)],
                                gather=X.w_in_block())
    w_in = X.w_in(*landed)
    proj = matmul("proj_fwd", xm, w_in, 'nn', tmR, PW // 2, 1024)
    conv_w8 = jnp.pad(S['conv_w'], ((0, 3), (0, 0)))
    conv_b = row(S['conv_b'])
    xbc = conv_fwd("conv_fwd", proj, conv_w8, conv_b, R)
    dt_bias = jnp.pad(S['dt_bias'].reshape(1, 32), ((0, 0), (0, 96)))
    alog = jnp.pad(S['a_log'].reshape(1, 32), ((0, 0), (0, 96)))
    x_dt = [(proj, 512, C_DT // 512, 0)]
    dt_f, dt_b = stage_fwd("dt_fwd", f_dt, T, nt, x_dt, [(dt_bias, None)], [('new', R, 128, 0), ('new', R, 128, 0)])
    directions = [(False, 0), (True, 1)]
    (y_f, hs_f), (y_b, hs_b) = ssd_fwd("ssd_fwd", xbc, [dt_f, dt_b], alog, n, directions)
    W = X.rest()
    dcol = jnp.repeat(S['d_skip'][0] + S['d_skip'][1], 64).reshape(1, D)
    x_gn = [(y_f, D, 0, 1), (y_b, D, 0, 1), (xbc, D, 0, 1), (proj, D, C_Z // D, 1)]
    p_gn = [(dcol, None), (row(S['ssd_norm_g']), None)]
    (yn,) = stage_fwd("gnorm_fwd", f_gated_norm, T, ntl, x_gn, p_gn, [('new', L, D, 0, BF16)])
    x_gm = [(proj, 2 * D, C_UV // (2 * D), LC // Q)]
    p_gm = ([(row(S['gm_norm_g']), None), (row(S['gm_norm_b']), None)]
            + [(S['w_spatial'][g], None) for g in range(8)] + [(S['b_spatial'][g].reshape(Q, 1), None) for g in range(8)])
    (y_gm,) = stage_fwd("gmlp_fwd", f_gmlp, Q, L // Q, x_gm, p_gm, [('new', L, D, 0, BF16)])
    p_ssd = matmul("pssd_fwd", yn, W['w_ssd_proj'], 'nn', tmS, 1024, 1024)
    p_g = matmul("pgm_fwd", y_gm, W['w_gm_proj'], 'nn', tmS, 1024, 1024)
    x_mg = [(p_ssd, D, 0, 0), (p_g, D, 0, 0), (proj, 2 * D, C_GATE // (2 * D), 1)]
    p_mg = [(row(S['b_gate']), None)]
    (merged,) = stage_fwd("merge_fwd", f_merge, T, ntl, x_mg, p_mg, [('new', L, D, 0, BF16)])
    out = matmul("out_fwd", merged, W['w_out'], 'nn', tmS, 1024, 1024)
    x_r1 = [(x0, D, 0, 1), (out, D, 0, 0)]
    p_r1 = [(mx[2], None), (row(S['ln1_g']), None), (row(S['ln1_b']), None), (mx[4], None), (mx[3], None)]
    x1, hm = stage_fwd("res1_fwd", f_res1, T, ntl, x_r1, p_r1, [('new', L, D, 0), ('new', L, D, 0, BF16)])
    a1, a3, act = ffn_in_fwd("ffn_in_fwd", hm, W['w_ff1'], W['w_ff3'], tmL)
    ff = ff_out_fwd("ff2_fwd", act, W['w_ff2'], tmL)
    x_r2 = [(x1, D, 0, 0), (ff, D, 0, 0), (tgt, D, 0, 0)]
    p_r2 = [(mx[5], None), (row(S['ln2_g']), None), (row(S['ln2_b']), None)]

    dx1_a, dff, dg2, dl2g, dl2b, loss = stage_bwd(
        "res2_bwd", f_res2_loss, T, ntl, x_r2, p_r2, [1.0],
        [('new', L, D, 0), ('new', L, D, 0, BF16), None], [True, True, True], primal=[(0, (1, 1))])
    da1, da3 = ffn_out_bwd_x("ffn_out_bwd_x", dff, W['w_ff2'], a1, a3, tmL)
    gw_ff2 = ff_out_bwd_w("ff2_bwd_w", act, dff, tkL)
    dhm = ff_in_bwd_x("ff1_bwd_x", da1, W['w_ff1'], tmL)
    dhm = ff_in_bwd_x("ff3_bwd_x", da3, W['w_ff3'], tmL, add=dhm)
    gw_ff1 = ff_in_bwd_w("ff1_bwd_w", hm, da1, tkL)
    gw_ff3 = ff_in_bwd_w("ff3_bwd_w", hm, da3, tkL)
    X.grads('ffn', {'w_ff2': gw_ff2, 'w_ff1': gw_ff1, 'w_ff3': gw_ff3})
    dx0_a, dout, dg1, dl1g, dl1b, dsc2, dsh2 = stage_bwd(
        "res1_bwd", f_res1, T, ntl, x_r1, p_r1, [(dx1_a, D, 0, 0), (dhm, D, 0, 0)],
        [('new', L, D, 0), ('new', L, D, 0, BF16)], [True] * 5)
    dmerged = matmul("out_bwd_x", dout, W['w_out'], 'nt', tmS, 1024, 1024)
    gw_out = matmul("out_bwd_w", merged, dout, 'tn', 1024, 1024, tkL, BF16)
    lt, lq = -(LC // T), -(LC // Q)
    x_mg_b = [(p_ssd, D, 0, lt), (p_g, D, 0, lt), (proj, 2 * D, C_GATE // (2 * D), 0)]
    dp_ssd, dp_g, dproj, dbg = stage_bwd(
        "merge_bwd", f_merge, T, nt, x_mg_b, p_mg, [(dmerged, D, 0, lt)],
        [('new', L, D, lt, BF16), ('new', L, D, lt, BF16), ('part', R, PW, 2 * D, C_GATE // (2 * D), 0, BF16)], [True])
    dyn = matmul("pssd_bwd_x", dp_ssd, W['w_ssd_proj'], 'nt', tmS, 1024, 1024)
    gw_ssd = matmul("pssd_bwd_w", yn, dp_ssd, 'tn', 1024, 1024, tkL, BF16)
    dy_gm = matmul("pgm_bwd_x", dp_g, W['w_gm_proj'], 'nt', tmS, 1024, 1024)
    gw_gm = matmul("pgm_bwd_w", y_gm, dp_g, 'tn', 1024, 1024, tkL, BF16)
    X.grads('proj', {'w_out': gw_out, 'w_ssd_proj': gw_ssd, 'w_gm_proj': gw_gm})
    r_gm = stage_bwd("gmlp_bwd", f_gmlp, Q, n, [(proj, 2 * D, C_UV // (2 * D), 0)], p_gm, [(dy_gm, D, 0, lq)],
                     [('alias', dproj, 2 * D, C_UV // (2 * D), 0)], [True] * 18)
    dproj, dgmg, dgmb, dws, dbs = r_gm[0], r_gm[1], r_gm[2], r_gm[3:11], r_gm[11:19]
    x_gn_b = [(y_f, D, 0, 0), (y_b, D, 0, 0), (xbc, D, 0, 0), (proj, D, C_Z // D, 0)]
    dy, dskipx, dproj, ddcol, dng = stage_bwd(
        "gnorm_bwd", f_gated_norm, T, nt, x_gn_b, p_gn, [(dyn, D, 0, lt)],
        [('new', L, D, lt), None, ('new', L, D, lt), ('alias', dproj, D, C_Z // D, 0)], [True, True], rows=32)
    (dxbc_f, ddt_f, dal_f), (dxbc_b, ddt_b, dal_b) = ssd_bwd("ssd_bwd", xbc, [dt_f, dt_b], alog, [hs_f, hs_b], dy, n,
                                                             directions)
    dproj, ddtb = stage_bwd("dt_bwd", f_dt, T, nt, x_dt, [(dt_bias, None)],
                            [(ddt_f, 128, 0, 0), (ddt_b, 128, 0, 0)],
                            [('alias', dproj, 512, C_DT // 512, 0)], [True])
    dproj, dcw8, dcb = conv_bwd("conv_bwd", proj, conv_w8, conv_b, dxbc_f, dxbc_b, dskipx, dproj, R)
    gw_in = matmul("proj_bwd_w", xm, dproj, 'tn', 1024, PW // 2, tkR, BF16)
    X.grads('in', {'w_in': gw_in})
    dxm = matmul("proj_bwd_x", dproj, w_in, 'nt', R // 2 if R % 16 == 0 else R, 1024, 1024)
    grad_x, dl0g, dl0b, dsc1, dsh1 = stage_bwd(
        "norm0_bwd", f_norm0, T, nt, x_n0, ln0, [(dx0_a, D, 0, -1), (dxm, D, 0, 0)],
        [None, ('new', L, D, -1)], [True] * 4 + [False])

    zero = jnp.zeros((D,), F32)
    flat = lambda v: v.reshape(-1)
    small = {
        'loss': flat(loss), 'ln0_g': flat(dl0g), 'ln0_b': flat(dl0b),
        'dmod_x': jnp.concatenate([flat(dsh1[1]), flat(dsc1[1]), flat(dg1), flat(dsh2), flat(dsc2), flat(dg2)]),
        'dmod_c': jnp.concatenate([flat(dsh1[0]), flat(dsc1[0]), zero, zero, zero, zero]),
        'conv_w': flat(dcw8[:5]), 'conv_b': flat(dcb), 'dt_bias': flat(ddtb[:, :32]),
        'a_log': flat((dal_f + dal_b)[:, :32]),
        'd_skip': flat(jnp.tile(ddcol.reshape(1, NH, 64).sum(-1), (2, 1))),
        'ssd_norm_g': flat(dng), 'gm_norm_g': flat(dgmg), 'gm_norm_b': flat(dgmb),
        'w_spatial': flat(jnp.stack(dws)), 'b_spatial': flat(jnp.stack(dbs)), 'b_gate': flat(dbg),
        'ln1_g': flat(dl1g), 'ln1_b': flat(dl1b), 'ln2_g': flat(dl2g), 'ln2_b': flat(dl2b),
    }
    return grad_x, small


def _place():
    return lax.axis_index("x"), lax.axis_index("y"), lax.axis_index("c")


def allgather8(name, blk, hbm):
    space = pl.ANY if hbm else pltpu.VMEM

    def body(x_ref, out_ref, send_sems, recv_sems, local_sem):
        x, y, c = _place()
        me, sibling = (x, y, c), (x, y, 1 - c)
        chips = [(1 - x, y), (x, 1 - y), (1 - x, 1 - y)]

        def slot(px, py, pc):
            return out_ref.at[4 * px + 2 * py + pc]

        def copy(k, block, to, src=None):
            return pltpu.make_async_remote_copy(
                src_ref=slot(*block) if src is None else src, dst_ref=slot(*block),
                send_sem=send_sems.at[k], recv_sem=recv_sems.at[k], device_id=to, device_id_type=MESH)

        mine = pltpu.make_async_copy(x_ref, slot(*me), local_sem)
        mine.start()
        first = [copy(0, me, sibling, src=x_ref)]
        first += [copy(1 + j, me, (*chip, c), src=x_ref) for j, chip in enumerate(chips)]
        for cp in first:
            cp.start()
        passed = [copy(4 + j, (*chip, c), sibling) for j, chip in enumerate(chips)]
        for j, chip in enumerate(chips):
            copy(1 + j, (*chip, c), me).wait_recv()
            passed[j].start()
        copy(0, sibling, me).wait_recv()
        for j, chip in enumerate(chips):
            copy(4 + j, (*chip, 1 - c), me).wait_recv()
        for cp in first + passed:
            cp.wait_send()
        mine.wait()

    return pl.pallas_call(
        body, name=name, out_shape=jax.ShapeDtypeStruct((8,) + blk.shape, blk.dtype),
        in_specs=[pl.BlockSpec(memory_space=space)], out_specs=pl.BlockSpec(memory_space=space),
        scratch_shapes=[pltpu.SemaphoreType.DMA((7,)), pltpu.SemaphoreType.DMA((7,)), pltpu.SemaphoreType.DMA],
        compiler_params=pltpu.CompilerParams(vmem_limit_bytes=VMEM_LIMIT_V7X))(blk)


def _peers(place):
    x, y, c = place
    return [((1 - x) if k & 4 else x, (1 - y) if k & 2 else y, (1 - c) if k & 1 else c) for k in range(1, 8)]


def _slot(p):
    return 4 * p[0] + 2 * p[1] + p[2]


def plan_gather(place, srcs, lands):
    remote = [(s, l.at[_slot(place)], to) for s, l in zip(srcs, lands) for to in _peers(place)]
    return remote, [(s, l.at[_slot(place)]) for s, l in zip(srcs, lands)]


def plan_to_owner(place, srcs, lands):
    remote = [(s.at[2 * to[0] + to[1], to[2]], l.at[_slot(place)], to) for s, l in zip(srcs, lands) for to in _peers(place)]
    x, y, c = place
    return remote, [(s.at[2 * x + y, c], l.at[_slot(place)]) for s, l in zip(srcs, lands)]


def sequencer_exchange(name, collective_id, srcs, land_shapes, plan):
    n = len(srcs)
    src_refs = [jax.new_ref(a, memory_space=pltpu.MemorySpace.HBM) for a in srcs]
    land_refs = [jax.empty_ref(s, memory_space=pltpu.MemorySpace.HBM) for s in land_shapes]

    @pl.kernel(mesh=plsc.ScalarSubcoreMesh(axis_name="sequencer", num_cores=1), name=name,
               scratch_types=(pltpu.SemaphoreType.DMA((7 * n,)), pltpu.SemaphoreType.DMA((7 * n,)),
                              pltpu.SemaphoreType.DMA((n,))),
               compiler_params=pltpu.CompilerParams(collective_id=collective_id))
    def launch(send_sems, recv_sems, local_sems):
        place = _place()
        barrier = pltpu.get_barrier_semaphore()
        for to in _peers(place):
            pl.semaphore_signal(barrier, inc=1, device_id=to, device_id_type=MESH)
        pl.semaphore_wait(barrier, 7)
        remote, local = plan(place, src_refs, land_refs)
        mine = [pltpu.make_async_copy(s, d, local_sems.at[a]) for a, (s, d) in enumerate(local)]
        for cp in mine:
            cp.start()
        cps = [pltpu.make_async_remote_copy(src_ref=s, dst_ref=d, send_sem=send_sems.at[k], recv_sem=recv_sems.at[k],
                                            device_id=to, device_id_type=MESH) for k, (s, d, to) in enumerate(remote)]
        for cp in cps:
            cp.start()
        for cp in mine:
            cp.wait()
        for cp in cps:
            cp.wait()

    launch()
    return land_refs


def sibling_pair(name, hs):
    n = len(hs)

    def body(*refs):
        ins, outs = refs[:n], refs[n:2 * n]
        send_sems, recv_sems = refs[2 * n:]
        x, y, c = _place()
        cps = [pltpu.make_async_remote_copy(src_ref=outs[a].at[c], dst_ref=outs[a].at[c], send_sem=send_sems.at[a],
                                            recv_sem=recv_sems.at[a], device_id=(x, y, 1 - c), device_id_type=MESH)
               for a in range(n)]
        for cp in cps:
            cp.start()
        for a in range(n):
            pltpu.make_async_remote_copy(src_ref=outs[a].at[1 - c], dst_ref=outs[a].at[1 - c], send_sem=send_sems.at[a],
                                         recv_sem=recv_sems.at[a], device_id=(x, y, 1 - c),
                                         device_id_type=MESH).wait_recv()
        for cp in cps:
            cp.wait_send()

    any_spec = pl.BlockSpec(memory_space=pl.ANY)
    return pl.pallas_call(
        body, name=name, out_shape=[jax.ShapeDtypeStruct(h.shape, h.dtype) for h in hs],
        in_specs=[any_spec] * n, out_specs=[any_spec] * n, input_output_aliases={a: a for a in range(n)},
        scratch_shapes=[pltpu.SemaphoreType.DMA((n,)), pltpu.SemaphoreType.DMA((n,))])(*hs)


def owner_sum(name, land):
    _, r, w = land.shape
    T = r // 2

    def body(_, l_ref, o_ref):
        acc = l_ref[0].astype(F32)
        for j in range(1, 8):
            acc = acc + l_ref[j].astype(F32)
        o_ref[...] = acc

    grid_spec = pltpu.PrefetchScalarGridSpec(
        num_scalar_prefetch=1, grid=(2,),
        in_specs=[pl.BlockSpec((8, T, w), lambda i, at: (0, i, 0))],
        out_specs=pl.BlockSpec((None, T, w), lambda i, at: (at[0], i, 0)))
    at = jnp.stack([lax.axis_index("c")]).astype(jnp.int32)
    return pl.pallas_call(body, name=name, grid_spec=grid_spec, out_shape=jax.ShapeDtypeStruct((2, r, w), F32),
                          compiler_params=_cp("arbitrary"))(at, land)


W_IN_RUNS = ((0, 2, 1296, 376), (376, 3, 0, 1672), (2048, 1, 920, 752), (2800, 2, 0, 1296), (4096, 0, 0, 1024),
             (5120, 0, 1024, 648), (5768, 1, 0, 920))


def w_in_to_padded(name, g4):
    T = 128

    def body(g_ref, o_ref):
        o_ref[:, D_PROJ:PW] = jnp.zeros((T, PW - D_PROJ), o_ref.dtype)
        for (a, s, j0, w) in W_IN_RUNS:
            o_ref[:, a:a + w] = g_ref[s, :, j0:j0 + w]

    return pl.pallas_call(body, name=name, grid=(D // T,), in_specs=[pl.BlockSpec((4, T, 1672), lambda i: (0, i, 0))],
                          out_specs=pl.BlockSpec((T, PW), lambda i: (i, 0)),
                          out_shape=jax.ShapeDtypeStruct((D, PW), g4.dtype), compiler_params=_cp("arbitrary"))(g4)


def w_in_from_padded(name, gp):
    T = 128

    def body(g_ref, o_ref):
        for (a, s, j0, w) in W_IN_RUNS:
            o_ref[s, :, j0:j0 + w] = g_ref[:, a:a + w]

    return pl.pallas_call(body, name=name, grid=(D // T,), in_specs=[pl.BlockSpec((T, PW), lambda i: (i, 0))],
                          out_specs=pl.BlockSpec((4, T, 1672), lambda i: (0, i, 0)),
                          out_shape=jax.ShapeDtypeStruct((4, D, 1672), gp.dtype), compiler_params=_cp("arbitrary"))(gp)


def sum_devices(name, g):
    def body(g_ref, o_ref):
        acc = g_ref[0]
        for k in range(1, 8):
            acc = acc + g_ref[k]
        o_ref[...] = acc

    return pl.pallas_call(body, name=name, out_shape=jax.ShapeDtypeStruct(g.shape[1:], F32),
                          compiler_params=pltpu.CompilerParams(vmem_limit_bytes=VMEM_LIMIT_V7X))(g)


def adamw(name, w, g, m, v, T):
    r, wd = w.shape
    c1 = 1.0 - ADAM_B1 ** ADAM_STEP
    c2 = 1.0 - ADAM_B2 ** ADAM_STEP

    def body(w_ref, g_ref, m_ref, v_ref, d_ref, mo_ref, vo_ref):
        gv = g_ref[...]
        mn = ADAM_B1 * m_ref[...] + (1.0 - ADAM_B1) * gv
        vn = ADAM_B2 * v_ref[...] + (1.0 - ADAM_B2) * (gv * gv)
        d_ref[...] = -ADAM_LR * ((mn / c1) / (jnp.sqrt(vn / c2) + ADAM_EPS) + ADAM_WD * w_ref[...])
        mo_ref[...] = mn
        vo_ref[...] = vn

    spec = pl.BlockSpec((T, wd), lambda i: (i, 0))
    return pl.pallas_call(body, name=name, grid=(r // T,), in_specs=[spec] * 4, out_specs=[spec] * 3,
                          out_shape=[jax.ShapeDtypeStruct((r, wd), F32)] * 3, compiler_params=_cp("arbitrary"))(w, g, m, v)


BIG = {'w_in': (1024, 1672), 'w_ssd_proj': (256, 1024), 'w_gm_proj': (256, 1024), 'w_out': (256, 1024),
       'w_ff1': (1024, 704), 'w_ff3': (1024, 704), 'w_ff2': (704, 1024)}


class Flat:
    def __init__(self, segs):
        self.off, o = {}, 0
        for name, size in segs:
            self.off[name] = (o, size)
            o += -(-size // 128) * 128
        self.rows = -(-o // 1024) * 8

    def pack(self, vals):
        parts = []
        for name, (o, size) in self.off.items():
            v = vals[name].reshape(-1).astype(F32)
            parts.append(jnp.pad(v, (0, -(-size // 128) * 128 - size)))
        buf = jnp.concatenate(parts)
        return jnp.pad(buf, (0, self.rows * 128 - buf.shape[0])).reshape(self.rows, 128)

    def get(self, buf, name, shape=None):
        o, size = self.off[name]
        v = buf[o // 128:(o + size + 127) // 128].reshape(-1)[:size]
        return v if shape is None else v.reshape(shape)


PARTIALS = Flat([('loss', 1), ('ln0_g', D), ('ln0_b', D), ('dmod_x', 6 * D), ('dmod_c', 6 * D), ('conv_w', 5 * 1536),
                 ('conv_b', 1536), ('dt_bias', 32), ('a_log', 32), ('d_skip', 32), ('ssd_norm_g', D),
                 ('gm_norm_g', D), ('gm_norm_b', D), ('w_spatial', 8 * Q * Q), ('b_spatial', 8 * Q), ('b_gate', 2 * D),
                 ('ln1_g', D), ('ln1_b', D), ('ln2_g', D), ('ln2_b', D)])

WEIGHTS = ('c_ctx', 'ln0_g', 'ln0_b', 'w_ada', 'b_ada', 'w_in', 'conv_w', 'conv_b', 'dt_bias', 'a_log', 'd_skip',
           'ssd_norm_g', 'gm_norm_g', 'gm_norm_b', 'w_spatial', 'b_spatial', 'b_gate', 'w_ssd_proj', 'w_gm_proj',
           'w_out', 'ln1_g', 'ln1_b', 'w_ff1', 'w_ff3', 'w_ff2', 'ln2_g', 'ln2_b')
BIG_NAMES = tuple(BIG)
SMALL_NAMES = tuple(n for n in WEIGHTS if n not in BIG_NAMES and n != 'w_ada')


def kernel(x, c, ctx, c_ctx, ln0_g, ln0_b, w_ada, b_ada, w_in, conv_w, conv_b, dt_bias, a_log, d_skip, ssd_norm_g, gm_norm_g, gm_norm_b, w_spatial, b_spatial, b_gate, w_ssd_proj, w_gm_proj, w_out, ln1_g, ln1_b, w_ff1, w_ff3, w_ff2, ln2_g, ln2_b, loss_target, m_c_ctx, m_ln0_g, m_ln0_b, m_w_ada, m_b_ada, m_w_in, m_conv_w, m_conv_b, m_dt_bias, m_a_log, m_d_skip, m_ssd_norm_g, m_gm_norm_g, m_gm_norm_b, m_w_spatial, m_b_spatial, m_b_gate, m_w_ssd_proj, m_w_gm_proj, m_w_out, m_ln1_g, m_ln1_b, m_w_ff1, m_w_ff3, m_w_ff2, m_ln2_g, m_ln2_b, v_c_ctx, v_ln0_g, v_ln0_b, v_w_ada, v_b_ada, v_w_in, v_conv_w, v_conv_b, v_dt_bias, v_a_log, v_d_skip, v_ssd_norm_g, v_gm_norm_g, v_gm_norm_b, v_w_spatial, v_b_spatial, v_b_gate, v_w_ssd_proj, v_w_gm_proj, v_w_out, v_ln1_g, v_ln1_b, v_w_ff1, v_w_ff3, v_w_ff2, v_ln2_g, v_ln2_b):
    wts = dict(c_ctx=c_ctx, ln0_g=ln0_g, ln0_b=ln0_b, w_ada=w_ada, b_ada=b_ada, w_in=w_in, conv_w=conv_w, conv_b=conv_b,
               dt_bias=dt_bias, a_log=a_log, d_skip=d_skip, ssd_norm_g=ssd_norm_g, gm_norm_g=gm_norm_g,
               gm_norm_b=gm_norm_b, w_spatial=w_spatial, b_spatial=b_spatial, b_gate=b_gate, w_ssd_proj=w_ssd_proj,
               w_gm_proj=w_gm_proj, w_out=w_out, ln1_g=ln1_g, ln1_b=ln1_b, w_ff1=w_ff1, w_ff3=w_ff3, w_ff2=w_ff2,
               ln2_g=ln2_g, ln2_b=ln2_b)
    ms = dict(zip(WEIGHTS, (m_c_ctx, m_ln0_g, m_ln0_b, m_w_ada, m_b_ada, m_w_in, m_conv_w, m_conv_b, m_dt_bias, m_a_log,
                            m_d_skip, m_ssd_norm_g, m_gm_norm_g, m_gm_norm_b, m_w_spatial, m_b_spatial, m_b_gate,
                            m_w_ssd_proj, m_w_gm_proj, m_w_out, m_ln1_g, m_ln1_b, m_w_ff1, m_w_ff3, m_w_ff2, m_ln2_g,
                            m_ln2_b)))
    vs = dict(zip(WEIGHTS, (v_c_ctx, v_ln0_g, v_ln0_b, v_w_ada, v_b_ada, v_w_in, v_conv_w, v_conv_b, v_dt_bias, v_a_log,
                            v_d_skip, v_ssd_norm_g, v_gm_norm_g, v_gm_norm_b, v_w_spatial, v_b_spatial, v_b_gate,
                            v_w_ssd_proj, v_w_gm_proj, v_w_out, v_ln1_g, v_ln1_b, v_w_ff1, v_w_ff3, v_w_ff2, v_ln2_g,
                            v_ln2_b)))
    px, py, pc = _place()
    shard = 2 * px + py
    dev = 2 * shard + pc
    take = lambda a, i, axis=0: lax.dynamic_index_in_dim(a, i, axis, keepdims=False)

    half = lambda n: take(wts[n][0].reshape(2, BIG[n][0] // 2, BIG[n][1]), pc).astype(BF16)

    pre = jnp.concatenate([c, jnp.pad(conv_w[0], ((0, 0), (0, D - 384))), jnp.zeros((2, D), F32)], axis=0)
    pre = allgather8("gather_cond", pre, False)
    conv_w_full = pre[0::2, 1:6, :384].transpose(1, 0, 2).reshape(5, 1536)
    a16 = jnp.concatenate([_silu(pre[:, 0, :]), _silu(c_ctx)[None], jnp.zeros((7, D), F32)], axis=0)
    mod = matmul("ada_fwd", a16, w_ada[0], 'nn', 16, 512, 1024)
    mod = mod + lax.dynamic_slice_in_dim(b_ada[0], shard * 1536, 1536)[None]
    mod = allgather8("gather_mod", mod, False)
    mod = jnp.concatenate([mod[0], mod[2], mod[4], mod[6]], axis=1)
    mod_x = take(mod, dev).reshape(6, D)
    mod_c = mod[8].reshape(6, D)

    def full(n, blocks):
        r, w = BIG[n]
        return blocks.reshape(4, r, w) if w != D else blocks.reshape(4 * r, w)

    class Exchanges:
        rest_names = BIG_NAMES[1:]

        def __init__(self):
            self.pending = []

        def w_in_block(self):
            return half('w_in')

        def w_in(self, blocks):
            w = w_in_to_padded("w_in_layout", full('w_in', blocks))
            halves = [half(n) for n in self.rest_names]
            halves[0], _ = lax.optimization_barrier((halves[0], blocks))
            lands = [jax.ShapeDtypeStruct((8,) + h.shape, BF16) for h in halves]
            self.rest_refs = sequencer_exchange("gather_rest", 1, halves, lands, plan_gather)
            return w

        def rest(self):
            return {n: full(n, r[...]) for n, r in zip(self.rest_names, self.rest_refs)}

        def grads(self, group, gs):
            if group == 'in':
                gs = {'w_in': w_in_from_padded("w_in_grad_layout", gs['w_in'])}
            blocks = [g.reshape(4, 2, BIG[n][0] // 2, BIG[n][1]) for n, g in gs.items()]
            lands = [jax.ShapeDtypeStruct((8,) + b.shape[2:], BF16) for b in blocks]
            refs = sequencer_exchange("grads_" + group, 2 + len(self.pending), blocks, lands, plan_to_owner)
            self.pending.append((tuple(gs), refs))

        def finish(self):
            names, halves = [], []
            for ns, refs in self.pending:
                names += ns
                halves += [owner_sum("grads_sum_" + n, r[...]) for n, r in zip(ns, refs)]
            return {n: h.reshape(BIG[n]) for n, h in zip(names, sibling_pair("grads_halves", halves))}

    S = dict(ln0_g=ln0_g, ln0_b=ln0_b, conv_w=conv_w_full, conv_b=conv_b[0], dt_bias=dt_bias[0], a_log=a_log[0],
             d_skip=d_skip[0], ssd_norm_g=ssd_norm_g[0], gm_norm_g=gm_norm_g[0], gm_norm_b=gm_norm_b[0],
             w_spatial=w_spatial[0], b_spatial=b_spatial[0], b_gate=b_gate[0], ln1_g=ln1_g[0], ln1_b=ln1_b[0],
             ln2_g=ln2_g[0], ln2_b=ln2_b[0])
    exchanges = Exchanges()
    grad_x, gsmall = core(ctx[0], x[0], loss_target[0], mod_x, mod_c, exchanges, S)

    parts = allgather8("gather_partials", PARTIALS.pack(gsmall), False)
    tot = sum_devices("partials_sum", parts)
    g_shards = exchanges.finish()
    g = {n: PARTIALS.get(tot, n) for n in ('ln0_g', 'ln0_b', 'conv_b', 'dt_bias', 'a_log', 'd_skip', 'ssd_norm_g',
                                           'gm_norm_g', 'gm_norm_b', 'w_spatial', 'b_spatial', 'b_gate', 'ln1_g',
                                           'ln1_b', 'ln2_g', 'ln2_b')}
    loss = PARTIALS.get(tot, 'loss', ())
    dmod_c = PARTIALS.get(tot, 'dmod_c')
    g['b_ada'] = PARTIALS.get(tot, 'dmod_x') + dmod_c
    g['conv_w'] = lax.dynamic_slice_in_dim(PARTIALS.get(tot, 'conv_w', (5, 1536)), shard * 384, 384, axis=1)
    o, size = PARTIALS.off['dmod_x']
    dmod_rows = parts[:, o // 128:(o + size) // 128].reshape(8, size)
    dm = jnp.concatenate([dmod_rows, dmod_c[None], jnp.zeros((7, 6 * D), F32)], axis=0)
    dm = lax.dynamic_slice_in_dim(dm, shard * 1536, 1536, axis=1)
    g['w_ada'] = matmul("ada_bwd_w", a16, dm, 'tn', 1024, 512, 16)
    dm_c = jnp.concatenate([dm[8:9], jnp.zeros((15, 1536), F32)], axis=0)
    dc = matmul("ada_bwd_c", dm_c, w_ada[0], 'nt', 16, 1024, 512)
    dc = allgather8("gather_dcctx", dc, False)[:, 0, :]
    dc = ((dc[0] + dc[2]) + dc[4]) + dc[6]
    sg = jax.nn.sigmoid(c_ctx)
    g['c_ctx'] = dc * (sg * (1.0 + c_ctx * (1.0 - sg)))
    for n in BIG_NAMES:
        g[n] = g_shards[n]

    delta, new_m, new_v = {}, {}, {}
    for n in BIG_NAMES + ('w_ada',):
        r, w = wts[n].shape[1:]
        if w % 128:
            T = max(t for t in range(8, 257, 8) if w % t == 0)
            d_, m_, v_ = adamw("adamw_" + n, wts[n][0].T, g[n].T, ms[n][0].T, vs[n][0].T, T)
            delta[n], new_m[n], new_v[n] = d_.T, m_.T, v_.T
        else:
            T = 352 if n == 'w_ff2' else 256
            delta[n], new_m[n], new_v[n] = adamw("adamw_" + n, wts[n][0], g[n], ms[n][0], vs[n][0], T)
    lay = Flat([(n, wts[n].size) for n in SMALL_NAMES])
    d_, m_, v_ = adamw("adamw_small", lay.pack(wts), lay.pack(g), lay.pack(ms), lay.pack(vs), lay.rows)
    for n in SMALL_NAMES:
        delta[n], new_m[n], new_v[n] = (lay.get(b, n) for b in (d_, m_, v_))

    shp = lambda d: [d[n].reshape(wts[n].shape) for n in WEIGHTS]
    return (loss, grad_x[None], *shp(g), *shp(delta), *shp(new_m), *shp(new_v))
```

```python
import functools

import jax
import jax.numpy as jnp
from jax import lax
from jax.experimental import pallas as pl
from jax.experimental.pallas import tpu as pltpu
from jax.experimental.pallas import tpu_sc as plsc

F32 = jnp.float32
BF16 = jnp.bfloat16
MESH = pl.DeviceIdType.MESH

VMEM_LIMIT_V7X = 56 * 1024 * 1024

D = 1024
LC = 256
Q = 128
NH = 16
D_FF = 2816
LN_EPS = 1e-5
ALPHA = 2.0 ** 0.25

PW = 7168
C_GATE, C_UV, C_Z, C_XBC, C_DT = 0, 2048, 4096, 5120, 6656
D_PROJ = 6688

ADAM_LR, ADAM_B1, ADAM_B2, ADAM_EPS, ADAM_WD, ADAM_STEP = 0.001, 0.9, 0.999, 1e-08, 0.01, 10


def _cp(*sem):
    return pltpu.CompilerParams(dimension_semantics=sem, vmem_limit_bytes=VMEM_LIMIT_V7X)


def _dot(a, b, ca, cb):
    return lax.dot_general(a.astype(BF16), b.astype(BF16), (((ca,), (cb,)), ((), ())),
                           preferred_element_type=F32)


@jax.custom_vjp
def mm(a, b):
    return _dot(a, b, 1, 0)


mm.defvjp(lambda a, b: (_dot(a, b, 1, 0), (a, b)),
          lambda r, g: (_dot(g, r[1], 1, 1), _dot(r[0], g, 0, 0)))


@jax.custom_vjp
def mm_nt(a, b):
    return _dot(a, b, 1, 1)


mm_nt.defvjp(lambda a, b: (_dot(a, b, 1, 1), (a, b)),
             lambda r, g: (_dot(g, r[1], 1, 0), _dot(g, r[0], 0, 0)))


@jax.custom_vjp
def mm_tn(a, b):
    return _dot(a, b, 0, 0)


mm_tn.defvjp(lambda a, b: (_dot(a, b, 0, 0), (a, b)),
             lambda r, g: (_dot(r[1], g, 1, 1), _dot(r[0], g, 1, 0)))


def _dot32(a, b):
    return lax.dot_general(a, b, (((1,), (0,)), ((), ())), precision=lax.Precision.HIGHEST,
                           preferred_element_type=F32)


def _cumsum_fn(rev):
    def tri(transpose):
        r = lax.broadcasted_iota(jnp.int32, (Q, Q), 0)
        c = lax.broadcasted_iota(jnp.int32, (Q, Q), 1)
        keep = (r >= c) if (rev == transpose) else (r <= c)
        return jnp.where(keep, 1.0, 0.0).astype(F32)

    @jax.custom_vjp
    def cums(a):
        return _dot32(tri(False), a)

    cums.defvjp(lambda a: (_dot32(tri(False), a), None), lambda _, g: (_dot32(tri(True), g),))
    return cums


def _cols(v, k):
    w = v.shape[1] // k
    return tuple(v[:, w * i:w * (i + 1)] for i in range(k))


def _splitter(k):
    @jax.custom_vjp
    def split(v):
        return _cols(v, k)

    @jax.custom_vjp
    def concat(ps):
        return jnp.concatenate(ps, axis=1)

    split.defvjp(lambda v: (_cols(v, k), None), lambda _, g: (jnp.concatenate(g, axis=1),))
    concat.defvjp(lambda ps: (jnp.concatenate(ps, axis=1), None), lambda _, g: (_cols(g, k),))
    return split, concat


split2, _ = _splitter(2)
split4, _ = _splitter(4)
split8, concat8 = _splitter(8)


def _ln(x, g, b):
    mu = jnp.mean(x, axis=-1, keepdims=True)
    xc = x - mu
    var = jnp.mean(xc * xc, axis=-1, keepdims=True)
    return xc * lax.rsqrt(var + LN_EPS) * g + b


def _silu(x):
    return x * jax.nn.sigmoid(x)


def _gelu(x):
    return 0.5 * x * (1.0 + jnp.tanh(0.7978845608028654 * (x + 0.044715 * (x * x * x))))


def _xspec(T, w, col, roff):
    return pl.BlockSpec((T, w), lambda i, col=col, roff=roff: (jnp.maximum(i + roff, 0), col))


def _pspec(p, sel):
    if sel is None:
        return pl.BlockSpec(p.shape, lambda i, n=p.ndim: (0,) * n)
    return pl.BlockSpec((1,) + p.shape[1:], lambda i, n=p.ndim: (sel(i),) + (0,) * (n - 1))


def _out_plumbing(outs, T, args, in_specs):
    shapes, specs, aliases = [], [], {}
    for k, o in enumerate(outs):
        if o[0] == 'new':
            _, rows, w, roff = o[:4]
            shapes.append(jax.ShapeDtypeStruct((rows, w), o[4] if len(o) > 4 else F32))
            specs.append(_xspec(T, w, 0, roff))
        elif o[0] == 'acc':
            shapes.append(jax.ShapeDtypeStruct(o[1], F32))
            specs.append(pl.BlockSpec(o[1], lambda i, n=len(o[1]): (0,) * n))
        elif o[0] == 'part':
            _, rows, wtot, w, col, roff, dtype = o
            shapes.append(jax.ShapeDtypeStruct((rows, wtot), dtype))
            specs.append(_xspec(T, w, col, roff))
        else:
            _, arr, w, col, roff = o
            aliases[len(args)] = k
            args.append(arr)
            in_specs.append(pl.BlockSpec(memory_space=pl.ANY))
            shapes.append(jax.ShapeDtypeStruct(arr.shape, arr.dtype))
            specs.append(_xspec(T, w, col, roff))
    return shapes, specs, aliases


def stage_fwd(name, f, T, n, xs, ps, outs, rows=None, gather=None):
    nx, npar = len(xs), len(ps)
    args = [x[0] for x in xs] + [p[0] for p in ps]
    in_specs = [_xspec(T, w, col, roff) for (_, w, col, roff) in xs] + [_pspec(p, sel) for (p, sel) in ps]
    n_in = len(args)
    shapes, specs, aliases = _out_plumbing(outs, T, args, in_specs)
    n_all_in = len(args)
    scratch = []
    if gather is not None:
        args.append(gather)
        in_specs.append(pl.BlockSpec(memory_space=pl.ANY))
        shapes.append(jax.ShapeDtypeStruct((8,) + gather.shape, gather.dtype))
        specs.append(pl.BlockSpec(memory_space=pl.ANY))
        scratch = [pltpu.SemaphoreType.DMA((7,)), pltpu.SemaphoreType.DMA((7,)), pltpu.SemaphoreType.DMA]

    def exchange(i, blk_ref, out_ref, send_sems, recv_sems, local_sem):
        x, y, c = _place()
        me, sibling = (x, y, c), (x, y, 1 - c)
        chips = [(1 - x, y), (x, 1 - y), (1 - x, 1 - y)]

        def copy(k, block, to, own=False):
            dst = out_ref.at[_slot(block)]
            return pltpu.make_async_remote_copy(src_ref=blk_ref if own else dst, dst_ref=dst, send_sem=send_sems.at[k],
                                                recv_sem=recv_sems.at[k], device_id=to, device_id_type=MESH)

        mine = pltpu.make_async_copy(blk_ref, out_ref.at[_slot(me)], local_sem)
        first = [copy(0, me, sibling, own=True)] + [copy(1 + j, me, (*ch, c), own=True) for j, ch in enumerate(chips)]
        passed = [copy(4 + j, (*ch, c), sibling) for j, ch in enumerate(chips)]

        @pl.when(i == 0)
        def _():
            mine.start()
            for cp in first:
                cp.start()

        @pl.when(i == n // 2)
        def _():
            for j, ch in enumerate(chips):
                copy(1 + j, (*ch, c), me).wait_recv()
                passed[j].start()

        @pl.when(i == n - 1)
        def _():
            copy(0, sibling, me).wait_recv()
            for j, ch in enumerate(chips):
                copy(4 + j, (*ch, 1 - c), me).wait_recv()
            for cp in first + passed:
                cp.wait_send()
            mine.wait()

    n_out = len(outs)

    def body(*refs):
        i = pl.program_id(0)
        if gather is not None:
            exchange(i, refs[n_all_in], refs[n_all_in + 1 + n_out], *refs[n_all_in + 2 + n_out:])
        out_refs = refs[len(args):len(args) + n_out]
        pv = [r[...] if ps[k][1] is None else r[0] for k, r in enumerate(refs[nx:n_in])]
        sums = {}
        for r0 in range(0, T, rows or T):
            g = slice(r0, r0 + (rows or T))
            res = f(*[r[g, :] for r in refs[:nx]], *pv)
            for k, o_ref in enumerate(out_refs):
                if outs[k][0] == 'acc':
                    sums[k] = res[k] if r0 == 0 else sums[k] + res[k]
                else:
                    o_ref[g, :] = res[k].astype(o_ref.dtype)
        for k, v in sums.items():
            o_ref = out_refs[k]

            @pl.when(i == 0)
            def _(o_ref=o_ref, v=v):
                o_ref[...] = v

            @pl.when(i > 0)
            def _(o_ref=o_ref, v=v):
                o_ref[...] += v

    return pl.pallas_call(body, name=name, grid=(n,), in_specs=in_specs, out_specs=specs, out_shape=shapes,
                          input_output_aliases=aliases, scratch_shapes=scratch,
                          compiler_params=_cp("arbitrary"))(*args)


def stage_bwd(name, f, T, n, xs, ps, cts, dxs, dps, primal=(), rows=None):
    nx, npar = len(xs), len(ps)
    args = [x[0] for x in xs] + [p[0] for p in ps]
    in_specs = [_xspec(T, w, col, roff) for (_, w, col, roff) in xs] + [_pspec(p, sel) for (p, sel) in ps]
    ct_arrs = [c for c in cts if isinstance(c, tuple)]
    for (a, w, col, roff) in ct_arrs:
        args.append(a)
        in_specs.append(_xspec(T, w, col, roff))
    n_in = len(args)
    outs, out_of = [], []
    for k, o in enumerate(dxs):
        if o is not None:
            outs.append(o)
            out_of.append(('x', k))
    for k, want in enumerate(dps):
        if want:
            p, sel = ps[k]
            outs.append(('acc', p.shape))
            out_of.append(('p', k))
    for k, shape in primal:
        outs.append(('acc', shape))
        out_of.append(('r', k))
    shapes, specs, aliases = _out_plumbing(outs, T, args, in_specs)
    for j, (kind, k) in enumerate(out_of):
        if kind == 'p' and ps[k][1] is not None:
            p, sel = ps[k]
            specs[j] = pl.BlockSpec((1,) + p.shape[1:], lambda i, n=p.ndim, sel=sel: (sel(i),) + (0,) * (n - 1))
    n_all_in = len(args)

    def body(*refs):
        i = pl.program_id(0)
        pv = [r[...] if ps[k][1] is None else r[0] for k, r in enumerate(refs[nx:nx + npar])]
        sums = {}
        for r0 in range(0, T, rows or T):
            g = slice(r0, r0 + (rows or T))
            res, vjp_fn = jax.vjp(f, *[r[g, :] for r in refs[:nx]], *pv)
            ctv, q = [], nx + npar
            for k, c in enumerate(cts):
                if c is None:
                    ctv.append(jnp.zeros_like(res[k]))
                elif isinstance(c, tuple):
                    v = refs[q][g, :]
                    if c[3] < 0:
                        v = v * (i + c[3] >= 0).astype(F32)
                    ctv.append(v)
                    q += 1
                else:
                    ctv.append(jnp.full_like(res[k], c))
            grads = vjp_fn(tuple(ctv))
            for j, o_ref in enumerate(refs[n_all_in:]):
                kind, k = out_of[j]
                if kind == 'x':
                    o_ref[g, :] = grads[k].astype(o_ref.dtype)
                else:
                    v = res[k] if kind == 'r' else grads[nx + k]
                    sums[j] = v if r0 == 0 else sums[j] + v
        for j, v in sums.items():
            kind, k = out_of[j]
            o_ref = refs[n_all_in + j]
            sel = None if kind == 'r' else ps[k][1]
            if sel is None:
                first, tgt = i == 0, o_ref
            else:
                first, tgt = jnp.logical_or(i == 0, sel(i) != sel(jnp.maximum(i - 1, 0))), o_ref.at[0]

            @pl.when(first)
            def _(tgt=tgt, v=v):
                tgt[...] = v

            @pl.when(jnp.logical_not(first))
            def _(tgt=tgt, v=v):
                tgt[...] += v

    return pl.pallas_call(body, name=name, grid=(n,), in_specs=in_specs, out_specs=specs, out_shape=shapes,
                          input_output_aliases=aliases, compiler_params=_cp("arbitrary"))(*args)


_CONTRACT = {'nn': (1, 0), 'nt': (1, 1), 'tn': (0, 0)}


def matmul(name, a, b, mode, tm, tn, tk, out_dtype=F32, add=None):
    if mode == 'nn':
        (M, K), (_, N) = a.shape, b.shape
    elif mode == 'nt':
        (M, K), (N, _) = a.shape, b.shape
    else:
        (K, M), (_, N) = a.shape, b.shape
    assert M % tm == 0 and N % tn == 0 and K % tk == 0, (name, M, N, K, tm, tn, tk)
    a_spec = (pl.BlockSpec((tk, tm), lambda j, i, k: (k, i)) if mode == 'tn'
              else pl.BlockSpec((tm, tk), lambda j, i, k: (i, k)))
    b_spec = (pl.BlockSpec((tn, tk), lambda j, i, k: (j, k)) if mode == 'nt'
              else pl.BlockSpec((tk, tn), lambda j, i, k: (k, j)))
    o_spec = pl.BlockSpec((tm, tn), lambda j, i, k: (i, j))
    return matmul_call(name, (N // tn, M // tm, K // tk), a, a_spec, b, b_spec, (M, N), o_spec, (tm, tn), mode,
                       out_dtype, add)


def matmul_call(name, grid, a, a_spec, b, b_spec, out_shape, o_spec, tile, mode, out_dtype=F32, add=None):
    tm, tn = tile
    nk = grid[2]
    ca, cb = _CONTRACT[mode]
    args, in_specs = [a, b], [a_spec, b_spec]
    if add is not None:
        args.append(add)
        in_specs.append(o_spec)

    def body(*refs):
        a_ref, b_ref = refs[0], refs[1]
        o_ref, acc = refs[-2], refs[-1]
        k = pl.program_id(2)
        if nk == 1:
            p = _dot(a_ref[...], b_ref[...], ca, cb)
            o_ref[...] = (p + refs[2][...] if add is not None else p).astype(out_dtype)
            return

        @pl.when(k == 0)
        def _():
            acc[...] = refs[2][...] if add is not None else jnp.zeros_like(acc)

        acc[...] += _dot(a_ref[...], b_ref[...], ca, cb)

        @pl.when(k == nk - 1)
        def _():
            o_ref[...] = acc[...].astype(out_dtype)

    return pl.pallas_call(body, name=name, grid=grid, in_specs=in_specs, out_specs=o_spec,
                          out_shape=jax.ShapeDtypeStruct(out_shape, out_dtype),
                          scratch_shapes=[pltpu.VMEM((tm, tn) if nk > 1 else (8, 128), F32)],
                          compiler_params=_cp("arbitrary", "arbitrary", "arbitrary"))(*args)


NS, WS = 4, 704


def _resident(name, M, tm, rows, weight, out_shape, out_block, out_map, step):
    in_specs = [pl.BlockSpec(rows[1], rows[2]), pl.BlockSpec(weight.shape, lambda i, n=weight.ndim: (0,) * n)]
    return pl.pallas_call(step, name=name, grid=(M // tm,), in_specs=in_specs, out_specs=pl.BlockSpec(out_block, out_map),
                          out_shape=jax.ShapeDtypeStruct(out_shape, F32), compiler_params=_cp("arbitrary"))(rows[0], weight)


def ffn_in_fwd(name, h, w1, w3, tm):
    M = h.shape[0]

    def step(h_ref, w1_ref, w3_ref, a1_ref, a3_ref, act_ref):
        for s in range(NS):
            a1 = _dot(h_ref[...], w1_ref[s], 1, 0)
            a3 = _dot(h_ref[...], w3_ref[s], 1, 0)
            a1_ref[s] = a1.astype(a1_ref.dtype)
            a3_ref[s] = a3.astype(a3_ref.dtype)
            act_ref[s] = (_silu(a1) * a3).astype(act_ref.dtype)

    wspec = pl.BlockSpec((NS, D, WS), lambda i: (0, 0, 0))
    ospec = pl.BlockSpec((NS, tm, WS), lambda i: (0, i, 0))
    return pl.pallas_call(
        step, name=name, grid=(M // tm,), in_specs=[pl.BlockSpec((tm, D), lambda i: (i, 0)), wspec, wspec],
        out_specs=[ospec, ospec, ospec],
        out_shape=[jax.ShapeDtypeStruct((NS, M, WS), BF16)] * 3, compiler_params=_cp("arbitrary"))(h, w1, w3)


def ffn_out_bwd_x(name, dff, w2, a1, a3, tm):
    M = dff.shape[0]

    def step(d_ref, w_ref, a1_ref, a3_ref, da1_ref, da3_ref):
        for s in range(NS):
            dact = _dot(d_ref[...], w_ref[s * WS:(s + 1) * WS, :], 1, 1)
            a1 = a1_ref[s].astype(F32)
            sig = jax.nn.sigmoid(a1)
            da3_ref[s] = (dact * (a1 * sig)).astype(da3_ref.dtype)
            da1_ref[s] = (dact * a3_ref[s].astype(F32) * (sig * (1.0 + a1 * (1.0 - sig)))).astype(da1_ref.dtype)

    aspec = pl.BlockSpec((NS, tm, WS), lambda i: (0, i, 0))
    return pl.pallas_call(
        step, name=name, grid=(M // tm,),
        in_specs=[pl.BlockSpec((tm, D), lambda i: (i, 0)), pl.BlockSpec(w2.shape, lambda i: (0, 0)), aspec, aspec],
        out_specs=[aspec, aspec],
        out_shape=[jax.ShapeDtypeStruct((NS, M, WS), BF16)] * 2, compiler_params=_cp("arbitrary"))(dff, w2, a1, a3)


def ff_in_bwd_x(name, da1, da3, w1, w3, tm):
    M = da1.shape[1]

    def step(d1_ref, d3_ref, w1_ref, w3_ref, o_ref):
        def product(d_ref, w_ref):
            acc = _dot(d_ref[0], w_ref[0], 1, 1)
            for s in range(1, NS):
                acc = acc + _dot(d_ref[s], w_ref[s], 1, 1)
            return acc

        o_ref[...] = product(d3_ref, w3_ref) + product(d1_ref, w1_ref)

    dspec = pl.BlockSpec((NS, tm, WS), lambda i: (0, i, 0))
    wspec = pl.BlockSpec((NS, D, WS), lambda i: (0, 0, 0))
    return pl.pallas_call(step, name=name, grid=(M // tm,), in_specs=[dspec, dspec, wspec, wspec],
                          out_specs=pl.BlockSpec((tm, D), lambda i: (i, 0)),
                          out_shape=jax.ShapeDtypeStruct((M, D), F32), compiler_params=_cp("arbitrary"))(da1, da3, w1, w3)


def ff_in_bwd_w(name, h, da3, tk):
    M = h.shape[0]

    def step(h_ref, d_ref, acc):
        for s in range(NS):
            acc[s] += _dot(h_ref[...], d_ref[s], 0, 0)

    return _token_sum(name, M // tk, [pl.BlockSpec((tk, D), lambda k: (k, 0)), pl.BlockSpec((NS, tk, WS), lambda k: (0, k, 0))],
                      (NS, D, WS), (NS, D, WS), step, (h, da3))


def ff_out_fwd(name, act3, w2, tm):
    M = act3.shape[1]

    def step(a_ref, w_ref, o_ref):
        acc = _dot(a_ref[0], w_ref[0:WS, :], 1, 0)
        for s in range(1, NS):
            acc = acc + _dot(a_ref[s], w_ref[s * WS:(s + 1) * WS, :], 1, 0)
        o_ref[...] = acc

    return _resident(name, M, tm, (act3, (NS, tm, WS), lambda i: (0, i, 0)), w2, (M, D), (tm, D), lambda i: (i, 0), step)


def _token_sum(name, nk, in_specs, out_shape, acc_shape, step, args):
    def body(*refs):
        o_ref, acc = refs[-2], refs[-1]
        k = pl.program_id(0)

        @pl.when(k == 0)
        def _():
            acc[...] = jnp.zeros_like(acc)

        step(*refs[:-2], acc)

        @pl.when(k == nk - 1)
        def _():
            o_ref[...] = acc[...].astype(o_ref.dtype)

    return pl.pallas_call(body, name=name, grid=(nk,), in_specs=in_specs,
                          out_specs=pl.BlockSpec(out_shape, lambda k, n=len(out_shape): (0,) * n),
                          out_shape=jax.ShapeDtypeStruct(out_shape, BF16), scratch_shapes=[pltpu.VMEM(acc_shape, F32)],
                          compiler_params=_cp("arbitrary"))(*args)


def ff_out_bwd_w(name, act3, dff, tk):
    M = dff.shape[0]

    def step(a_ref, d_ref, acc):
        for s in range(NS):
            acc[s * WS:(s + 1) * WS, :] += _dot(a_ref[s], d_ref[...], 0, 0)

    return _token_sum(name, M // tk, [pl.BlockSpec((NS, tk, WS), lambda k: (0, k, 0)), pl.BlockSpec((tk, D), lambda k: (k, 0))],
                      (NS * WS, D), (NS * WS, D), step, (act3, dff))


HALO = 8
CONV_ROWS = 128


def _fill(pad_ref, n, v=None):
    edge = jnp.zeros((HALO, 128), F32)
    pad_ref[0:HALO, :] = edge
    pad_ref[HALO + n:2 * HALO + n, :] = edge
    if v is not None:
        pad_ref[HALO:HALO + n, :] = v


def _tap(pad_ref, r0, k, rows=CONV_ROWS):
    return pad_ref[HALO + r0 + k - 2:HALO + r0 + k - 2 + rows, :]


def _conv_pre(pad_ref, r0, w_ref, b_ref):
    acc = _tap(pad_ref, r0, 0) * w_ref[0:1, :] + b_ref[...]
    for k in range(1, 5):
        acc = acc + _tap(pad_ref, r0, k) * w_ref[k:k + 1, :]
    return acc


def conv_fwd(name, proj, conv_w, conv_b, R):
    segs = ((0, LC), (LC, R))

    def body(x_ref, w_ref, b_ref, o_ref, xp):
        for (s, e) in segs:
            _fill(xp, e - s, x_ref[s:e, :])
            for r0 in range(0, e - s, CONV_ROWS):
                o_ref[s + r0:s + r0 + CONV_ROWS, :] = _silu(_conv_pre(xp, r0, w_ref, b_ref))

    return pl.pallas_call(
        body, name=name, grid=(12,),
        in_specs=[pl.BlockSpec((R, 128), lambda j: (0, C_XBC // 128 + j)),
                  pl.BlockSpec((8, 128), lambda j: (0, j)), pl.BlockSpec((1, 128), lambda j: (0, j))],
        out_specs=pl.BlockSpec((R, 128), lambda j: (0, j)),
        out_shape=jax.ShapeDtypeStruct((R, 1536), F32), scratch_shapes=[pltpu.VMEM((R - LC + 2 * HALO, 128), F32)],
        compiler_params=_cp("arbitrary"))(proj, conv_w, conv_b)


def conv_bwd(name, proj, conv_w, conv_b, d_f, d_b, d_skip, dproj, R):
    segs = ((0, LC), (LC, R))

    def body(x_ref, w_ref, b_ref, df_ref, db_ref, ds_ref, _, dx_ref, dw_ref, dbias_ref, xp, dp):
        j = pl.program_id(0)
        has_skip = (j < 8).astype(F32)
        dw = [jnp.zeros((8, 128), F32) for _ in range(5)]
        dbias = jnp.zeros((8, 128), F32)
        fold = lambda v: jnp.sum(v.reshape(CONV_ROWS // 8, 8, 128), axis=0)
        for (s, e) in segs:
            n = e - s
            _fill(xp, n, x_ref[s:e, :])
            _fill(dp, n)
            for r0 in range(0, n, CONV_ROWS):
                rows = slice(s + r0, s + r0 + CONV_ROWS)
                pre = _conv_pre(xp, r0, w_ref, b_ref)
                sig = jax.nn.sigmoid(pre)
                dy = df_ref[rows, :] + db_ref[rows, :]
                if s == LC:
                    dy = dy + ds_ref[r0:r0 + CONV_ROWS, :] * has_skip
                dpre = dy * (sig * (1.0 + pre * (1.0 - sig)))
                dp[HALO + r0:HALO + r0 + CONV_ROWS, :] = dpre
                dbias = dbias + fold(dpre)
            for r0 in range(0, n, CONV_ROWS):
                x = x_ref[s + r0:s + r0 + CONV_ROWS, :]
                dx = jnp.zeros_like(x)
                for k in range(5):
                    d = _tap(dp, r0, 4 - k)
                    dx = dx + d * w_ref[k:k + 1, :]
                    dw[k] = dw[k] + fold(d * x)
                dx_ref[s + r0:s + r0 + CONV_ROWS, :] = dx.astype(dx_ref.dtype)
        dw_ref[...] = jnp.zeros_like(dw_ref)
        for k in range(5):
            dw_ref[k:k + 1, :] = jnp.sum(dw[k], axis=0, keepdims=True)
        dbias_ref[...] = jnp.sum(dbias, axis=0, keepdims=True)

    pad = pltpu.VMEM((R - LC + 2 * HALO, 128), F32)
    return pl.pallas_call(
        body, name=name, grid=(12,),
        in_specs=[pl.BlockSpec((R, 128), lambda j: (0, C_XBC // 128 + j)),
                  pl.BlockSpec((8, 128), lambda j: (0, j)), pl.BlockSpec((1, 128), lambda j: (0, j)),
                  pl.BlockSpec((R, 128), lambda j: (0, j)), pl.BlockSpec((R, 128), lambda j: (0, j)),
                  pl.BlockSpec((R - LC, 128), lambda j: (0, jnp.minimum(j, 7))),
                  pl.BlockSpec(memory_space=pl.ANY)],
        out_specs=[pl.BlockSpec((R, 128), lambda j: (0, C_XBC // 128 + j)),
                   pl.BlockSpec((8, 128), lambda j: (0, j)), pl.BlockSpec((1, 128), lambda j: (0, j))],
        out_shape=[jax.ShapeDtypeStruct(dproj.shape, dproj.dtype), jax.ShapeDtypeStruct((8, 1536), F32),
                   jax.ShapeDtypeStruct((1, 1536), F32)], scratch_shapes=[pad, pad],
        input_output_aliases={6: 0}, compiler_params=_cp("arbitrary"))(proj, conv_w, conv_b, d_f, d_b, d_skip, dproj)


def _ssd_chunk(rev, dirn):
    cums = _cumsum_fn(rev)

    def f(xs, Bs, Cs, dt, alog, Hs):
        lane = lax.broadcasted_iota(jnp.int32, (1, 128), 1)
        sub = lax.broadcasted_iota(jnp.int32, (Q, 1), 0)
        r = lax.broadcasted_iota(jnp.int32, (Q, Q), 0)
        c = lax.broadcasted_iota(jnp.int32, (Q, Q), 1)
        mask = (r <= c) if rev else (r >= c)
        left = lane < 64
        a = dt * (-jnp.exp(alog))
        s = cums(a)
        sT, dtT = s.T, dt.T
        last_row = (sub == (0 if rev else Q - 1)).astype(F32)
        s_last = jnp.sum(s * last_row, axis=0, keepdims=True)
        G = [mm_nt(Cs[g], Bs[g]) for g in range(2)]
        M, es, wc, ed = [], [], [], []
        for h in range(NH):
            l = 16 * dirn + h
            oh_l = (lane == l).astype(F32)
            oh_s = (sub == l).astype(F32)
            s_col = jnp.sum(s * oh_l, axis=1, keepdims=True)
            dt_col = jnp.sum(dt * oh_l, axis=1, keepdims=True)
            s_row = jnp.sum(sT * oh_s, axis=0, keepdims=True)
            dt_row = jnp.sum(dtT * oh_s, axis=0, keepdims=True)
            sl = jnp.sum(s_last * oh_l, axis=1, keepdims=True)
            seg = jnp.where(mask, s_col - s_row, 0.0)
            lm = jnp.where(mask, jnp.exp(seg), 0.0)
            M.append(G[h // 8] * lm * dt_row)
            es.append(jnp.exp(s_col))
            wc.append(jnp.exp(sl - s_col) * dt_col)
            ed.append(jnp.exp(sl))
        Ys, Hn = [], []
        for j in range(8):
            g = j // 4
            xa = jnp.where(left, xs[j], 0.0)
            xb = jnp.where(left, 0.0, xs[j])
            yd = mm(M[2 * j], xa) + mm(M[2 * j + 1], xb)
            yo = mm(Cs[g], Hs[j]) * jnp.where(left, es[2 * j], es[2 * j + 1])
            Ys.append(yd + yo)
            st = mm_tn(Bs[g], xs[j] * jnp.where(left, wc[2 * j], wc[2 * j + 1]))
            Hn.append(Hs[j] * jnp.where(left, ed[2 * j], ed[2 * j + 1]) + st)
        return Ys, Hn

    return f


def _chunk_of(t, n, rev):
    if not rev:
        return t
    return jnp.where(t < 2, 1 - t, n + 1 - t)


def _cols128(ref, k, lead=()):
    return [ref[lead + (slice(None), slice(128 * j, 128 * (j + 1)))] for j in range(k)]


def ssd_fwd(name, xbc, dts, alog, n, dirs):
    nd = len(dirs)
    chunks = [_ssd_chunk(rev, dirn) for rev, dirn in dirs]

    def body(*refs):
        al_ref = refs[4 * nd]
        for d in range(nd):
            x_ref, b_ref, c_ref, dt_ref = refs[4 * d:4 * d + 4]
            y_ref, hs_ref = refs[4 * nd + 1 + 2 * d:4 * nd + 3 + 2 * d]
            h_scr = refs[4 * nd + 1 + 2 * nd + d]

            @pl.when(pl.program_id(0) == 0)
            def _(h_scr=h_scr):
                h_scr[...] = jnp.zeros_like(h_scr)

            hs_ref[0] = h_scr[...]
            Ys, Hn = chunks[d](_cols128(x_ref, 8), _cols128(b_ref, 2), _cols128(c_ref, 2), dt_ref[...], al_ref[...],
                               _cols128(h_scr, 8))
            for j in range(8):
                y_ref[:, 128 * j:128 * (j + 1)] = Ys[j]
                h_scr[:, 128 * j:128 * (j + 1)] = Hn[j]

    in_specs, out_specs, out_shape, args = [], [], [], []
    for (rev, _), dt in zip(dirs, dts):
        cm = lambda t, rev=rev: _chunk_of(t, n, rev)
        in_specs += [pl.BlockSpec((Q, 1024), lambda t, cm=cm: (cm(t), 0)), pl.BlockSpec((Q, 256), lambda t, cm=cm: (cm(t), 4)),
                     pl.BlockSpec((Q, 256), lambda t, cm=cm: (cm(t), 5)), pl.BlockSpec((Q, 128), lambda t, cm=cm: (cm(t), 0))]
        args += [xbc, xbc, xbc, dt]
        out_specs += [pl.BlockSpec((Q, 1024), lambda t, cm=cm: (cm(t), 0)),
                      pl.BlockSpec((1, Q, 1024), lambda t, cm=cm: (cm(t), 0, 0))]
        out_shape += [jax.ShapeDtypeStruct((n * Q, 1024), F32), jax.ShapeDtypeStruct((n, Q, 1024), F32)]
    res = pl.pallas_call(
        body, name=name, grid=(n,), in_specs=in_specs + [pl.BlockSpec((1, 128), lambda t: (0, 0))],
        out_specs=out_specs, out_shape=out_shape, scratch_shapes=[pltpu.VMEM((Q, 1024), F32)] * nd,
        compiler_params=_cp("arbitrary"))(*args, alog)
    return [res[2 * d:2 * d + 2] for d in range(nd)]


def ssd_bwd(name, xbc, dts, alog, hss, dy, n, dirs):
    nd = len(dirs)
    chunks = [_ssd_chunk(rev, dirn) for rev, dirn in dirs]

    def body(*refs):
        tt = pl.program_id(0)
        al_ref = refs[6 * nd]
        for d, (rev, _) in enumerate(dirs):
            x_ref, b_ref, c_ref, dt_ref, hs_ref, dy_ref = refs[6 * d:6 * d + 6]
            dx_ref, ddt_ref, dal_ref = refs[6 * nd + 1 + 3 * d:6 * nd + 4 + 3 * d]
            dh_scr = refs[6 * nd + 1 + 3 * nd + d]
            ch = _chunk_of(n - 1 - tt, n, rev)

            @pl.when(tt == 0)
            def _(dh_scr=dh_scr):
                dh_scr[...] = jnp.zeros_like(dh_scr)

            live = (ch >= 2).astype(F32)
            dYs = [v * live for v in _cols128(dy_ref, 8)]
            _, vjp_fn = jax.vjp(chunks[d], _cols128(x_ref, 8), _cols128(b_ref, 2), _cols128(c_ref, 2), dt_ref[...],
                                al_ref[...], _cols128(hs_ref, 8, (0,)))
            dxs, dBs, dCs, ddt, dal, dHs = vjp_fn((dYs, _cols128(dh_scr, 8)))
            for j in range(8):
                dx_ref[:, 128 * j:128 * (j + 1)] = dxs[j]
                dh_scr[:, 128 * j:128 * (j + 1)] = dHs[j]
            for g in range(2):
                dx_ref[:, 1024 + 128 * g:1024 + 128 * (g + 1)] = dBs[g]
                dx_ref[:, 1280 + 128 * g:1280 + 128 * (g + 1)] = dCs[g]
            ddt_ref[...] = ddt

            @pl.when(tt == 0)
            def _(dal_ref=dal_ref, dal=dal):
                dal_ref[...] = dal

            @pl.when(tt > 0)
            def _(dal_ref=dal_ref, dal=dal):
                dal_ref[...] += dal

    in_specs, out_specs, out_shape, args = [], [], [], []
    for (rev, _), dt, hs in zip(dirs, dts, hss):
        cm = lambda t, rev=rev: _chunk_of(n - 1 - t, n, rev)
        in_specs += [pl.BlockSpec((Q, 1024), lambda t, cm=cm: (cm(t), 0)), pl.BlockSpec((Q, 256), lambda t, cm=cm: (cm(t), 4)),
                     pl.BlockSpec((Q, 256), lambda t, cm=cm: (cm(t), 5)), pl.BlockSpec((Q, 128), lambda t, cm=cm: (cm(t), 0)),
                     pl.BlockSpec((1, Q, 1024), lambda t, cm=cm: (cm(t), 0, 0)),
                     pl.BlockSpec((Q, 1024), lambda t, cm=cm: (jnp.maximum(cm(t) - 2, 0), 0))]
        args += [xbc, xbc, xbc, dt, hs, dy]
        out_specs += [pl.BlockSpec((Q, 1536), lambda t, cm=cm: (cm(t), 0)), pl.BlockSpec((Q, 128), lambda t, cm=cm: (cm(t), 0)),
                      pl.BlockSpec((1, 128), lambda t: (0, 0))]
        out_shape += [jax.ShapeDtypeStruct((n * Q, 1536), F32), jax.ShapeDtypeStruct((n * Q, 128), F32),
                      jax.ShapeDtypeStruct((1, 128), F32)]
    res = pl.pallas_call(
        body, name=name, grid=(n,), in_specs=in_specs + [pl.BlockSpec((1, 128), lambda t: (0, 0))],
        out_specs=out_specs, out_shape=out_shape, scratch_shapes=[pltpu.VMEM((Q, 1024), F32)] * nd,
        compiler_params=_cp("arbitrary"))(*args, alog)
    return [res[3 * d:3 * d + 3] for d in range(nd)]


def f_norm0(c, x, g0, b0, sc, sh, is_ctx):
    x0 = _ln(jnp.where(is_ctx > 0.5, c, x), g0, b0)
    return x0, x0 * (1.0 + sc) + sh


def f_dt(raw, bias):
    z = split4(raw)[0] + bias
    dt = jnp.maximum(z, 0.0) + jnp.log1p(jnp.exp(-jnp.abs(z)))
    return dt, dt


def f_gated_norm(yf, yb, xs, z, dcol, g):
    h = (yf + yb + xs * dcol) * _silu(z)
    return (h * lax.rsqrt(jnp.mean(h * h, axis=-1, keepdims=True) + LN_EPS) * g,)


def f_gmlp(uv, gmg, gmb, *wb):
    ws, bs = wb[:8], wb[8:]
    u, v = split2(uv)
    vn = split8(_ln(_gelu(v), gmg, gmb))
    mixed = concat8(tuple(mm(ws[g], vn[g]) + bs[g] for g in range(8)))
    return (_gelu(u) * mixed,)


def f_merge(ps, pg, gates, bg):
    gs, gg = split2(jax.nn.sigmoid(gates + bg))
    return (gs * ps + gg * pg,)


def f_res1(x0, out, g1, lg, lb, sc, sh):
    x1 = _ln(ALPHA * x0 + g1 * out, lg, lb)
    return x1, x1 * (1.0 + sc) + sh


def f_res2_loss(x1, ff, tgt, g2, lg, lb):
    x2 = _ln(ALPHA * x1 + g2 * ff, lg, lb)
    e = x2 - tgt
    return (0.5 * jnp.sum(jnp.mean(e * e, axis=-1, keepdims=True), axis=0, keepdims=True),)


def _row_tile(M):
    return 544 if M % 544 == 0 else (512 if M % 512 == 0 else M)


def core(ctx, x, tgt, mod_x, mod_c, X, S):
    L = x.shape[0]
    R = LC + L
    n = R // Q
    T = 256
    nt, ntl = R // T, L // T
    tmR, tmL = _row_tile(R), _row_tile(L)
    tmS = 1024 if L % 1024 == 0 else tmL
    tkR = 256 if R % 512 else 512
    tkL = 512 if L % 512 == 0 else 256
    row = lambda v: v.reshape(1, -1)
    mx = [row(mod_x[k]) for k in range(6)]
    mc = [row(mod_c[k]) for k in range(6)]
    sel = lambda i: jnp.minimum(i, 1)
    sc1 = jnp.stack([mc[1], mx[1]])
    sh1 = jnp.stack([mc[0], mx[0]])
    ln0 = [(row(S['ln0_g']), None), (row(S['ln0_b']), None), (sc1, sel), (sh1, sel),
           (jnp.array([1.0, 0.0], F32).reshape(2, 1, 1), sel)]
    x_n0 = [(ctx, D, 0, -nt), (x, D, 0, -1)]

    x0, xm, *landed = stage_fwd("norm0_fwd", f_norm0, T, nt, x_n0, ln0, [('new', R, D, 0), ('new', R, D, 0, BF16)],
                                gather=X.w_in_block())
    w_in = X.w_in(*landed)
    proj = matmul("proj_fwd", xm, w_in, 'nn', tmR, PW // 2, 1024)
    conv_w8 = jnp.pad(S['conv_w'], ((0, 3), (0, 0)))
    conv_b = row(S['conv_b'])
    xbc = conv_fwd("conv_fwd", proj, conv_w8, conv_b, R)
    dt_bias = jnp.pad(S['dt_bias'].reshape(1, 32), ((0, 0), (0, 96)))
    alog = jnp.pad(S['a_log'].reshape(1, 32), ((0, 0), (0, 96)))
    x_dt = [(proj, 512, C_DT // 512, 0)]
    dt_f, dt_b = stage_fwd("dt_fwd", f_dt, T, nt, x_dt, [(dt_bias, None)], [('new', R, 128, 0), ('new', R, 128, 0)])
    directions = [(False, 0), (True, 1)]
    (y_f, hs_f), (y_b, hs_b) = ssd_fwd("ssd_fwd", xbc, [dt_f, dt_b], alog, n, directions)
    W = X.rest()
    dcol = jnp.repeat(S['d_skip'][0] + S['d_skip'][1], 64).reshape(1, D)
    x_gn = [(y_f, D, 0, 1), (y_b, D, 0, 1), (xbc, D, 0, 1), (proj, D, C_Z // D, 1)]
    p_gn = [(dcol, None), (row(S['ssd_norm_g']), None)]
    (yn,) = stage_fwd("gnorm_fwd", f_gated_norm, T, ntl, x_gn, p_gn, [('new', L, D, 0, BF16)])
    x_gm = [(proj, 2 * D, C_UV // (2 * D), LC // Q)]
    p_gm = ([(row(S['gm_norm_g']), None), (row(S['gm_norm_b']), None)]
            + [(S['w_spatial'][g], None) for g in range(8)] + [(S['b_spatial'][g].reshape(Q, 1), None) for g in range(8)])
    (y_gm,) = stage_fwd("gmlp_fwd", f_gmlp, Q, L // Q, x_gm, p_gm, [('new', L, D, 0, BF16)])
    p_ssd = matmul("pssd_fwd", yn, W['w_ssd_proj'], 'nn', tmS, 1024, 1024)
    p_g = matmul("pgm_fwd", y_gm, W['w_gm_proj'], 'nn', tmS, 1024, 1024)
    x_mg = [(p_ssd, D, 0, 0), (p_g, D, 0, 0), (proj, 2 * D, C_GATE // (2 * D), 1)]
    p_mg = [(row(S['b_gate']), None)]
    (merged,) = stage_fwd("merge_fwd", f_merge, T, ntl, x_mg, p_mg, [('new', L, D, 0, BF16)])
    out = matmul("out_fwd", merged, W['w_out'], 'nn', tmS, 1024, 1024)
    x_r1 = [(x0, D, 0, 1), (out, D, 0, 0)]
    p_r1 = [(mx[2], None), (row(S['ln1_g']), None), (row(S['ln1_b']), None), (mx[4], None), (mx[3], None)]
    x1, hm = stage_fwd("res1_fwd", f_res1, T, ntl, x_r1, p_r1, [('new', L, D, 0), ('new', L, D, 0, BF16)])
    a1, a3, act = ffn_in_fwd("ffn_in_fwd", hm, W['w_ff1'], W['w_ff3'], tmL)
    ff = ff_out_fwd("ff2_fwd", act, W['w_ff2'], tmL)
    x_r2 = [(x1, D, 0, 0), (ff, D, 0, 0), (tgt, D, 0, 0)]
    p_r2 = [(mx[5], None), (row(S['ln2_g']), None), (row(S['ln2_b']), None)]

    dx1_a, dff, dg2, dl2g, dl2b, loss = stage_bwd(
        "res2_bwd", f_res2_loss, T, ntl, x_r2, p_r2, [1.0],
        [('new', L, D, 0), ('new', L, D, 0, BF16), None], [True, True, True], primal=[(0, (1, 1))])
    da1, da3 = ffn_out_bwd_x("ffn_out_bwd_x", dff, W['w_ff2'], a1, a3, tmL)
    gw_ff2 = ff_out_bwd_w("ff2_bwd_w", act, dff, tkL)
    dhm = ff_in_bwd_x("ffn_in_bwd_x", da1, da3, W['w_ff1'], W['w_ff3'], tmL)
    gw_ff1 = ff_in_bwd_w("ff1_bwd_w", hm, da1, tkL)
    gw_ff3 = ff_in_bwd_w("ff3_bwd_w", hm, da3, tkL)
    X.grads('ffn', {'w_ff2': gw_ff2, 'w_ff1': gw_ff1, 'w_ff3': gw_ff3})
    dx0_a, dout, dg1, dl1g, dl1b, dsc2, dsh2 = stage_bwd(
        "res1_bwd", f_res1, T, ntl, x_r1, p_r1, [(dx1_a, D, 0, 0), (dhm, D, 0, 0)],
        [('new', L, D, 0), ('new', L, D, 0, BF16)], [True] * 5)
    dmerged = matmul("out_bwd_x", dout, W['w_out'], 'nt', tmS, 1024, 1024)
    gw_out = matmul("out_bwd_w", merged, dout, 'tn', 1024, 1024, tkL, BF16)
    lt, lq = -(LC // T), -(LC // Q)
    x_mg_b = [(p_ssd, D, 0, lt), (p_g, D, 0, lt), (proj, 2 * D, C_GATE // (2 * D), 0)]
    dp_ssd, dp_g, dproj, dbg = stage_bwd(
        "merge_bwd", f_merge, T, nt, x_mg_b, p_mg, [(dmerged, D, 0, lt)],
        [('new', L, D, lt, BF16), ('new', L, D, lt, BF16), ('part', R, PW, 2 * D, C_GATE // (2 * D), 0, BF16)], [True])
    dyn = matmul("pssd_bwd_x", dp_ssd, W['w_ssd_proj'], 'nt', tmS, 1024, 1024)
    gw_ssd = matmul("pssd_bwd_w", yn, dp_ssd, 'tn', 1024, 1024, tkL, BF16)
    dy_gm = matmul("pgm_bwd_x", dp_g, W['w_gm_proj'], 'nt', tmS, 1024, 1024)
    gw_gm = matmul("pgm_bwd_w", y_gm, dp_g, 'tn', 1024, 1024, tkL, BF16)
    X.grads('proj', {'w_out': gw_out, 'w_ssd_proj': gw_ssd, 'w_gm_proj': gw_gm})
    r_gm = stage_bwd("gmlp_bwd", f_gmlp, Q, n, [(proj, 2 * D, C_UV // (2 * D), 0)], p_gm, [(dy_gm, D, 0, lq)],
                     [('alias', dproj, 2 * D, C_UV // (2 * D), 0)], [True] * 18)
    dproj, dgmg, dgmb, dws, dbs = r_gm[0], r_gm[1], r_gm[2], r_gm[3:11], r_gm[11:19]
    x_gn_b = [(y_f, D, 0, 0), (y_b, D, 0, 0), (xbc, D, 0, 0), (proj, D, C_Z // D, 0)]
    dy, dskipx, dproj, ddcol, dng = stage_bwd(
        "gnorm_bwd", f_gated_norm, T, nt, x_gn_b, p_gn, [(dyn, D, 0, lt)],
        [('new', L, D, lt), None, ('new', L, D, lt), ('alias', dproj, D, C_Z // D, 0)], [True, True], rows=32)
    (dxbc_f, ddt_f, dal_f), (dxbc_b, ddt_b, dal_b) = ssd_bwd("ssd_bwd", xbc, [dt_f, dt_b], alog, [hs_f, hs_b], dy, n,
                                                             directions)
    dproj, ddtb = stage_bwd("dt_bwd", f_dt, T, nt, x_dt, [(dt_bias, None)],
                            [(ddt_f, 128, 0, 0), (ddt_b, 128, 0, 0)],
                            [('alias', dproj, 512, C_DT // 512, 0)], [True])
    dproj, dcw8, dcb = conv_bwd("conv_bwd", proj, conv_w8, conv_b, dxbc_f, dxbc_b, dskipx, dproj, R)
    gw_in = matmul("proj_bwd_w", xm, dproj, 'tn', 1024, PW // 2, tkR, BF16)
    X.grads('in', {'w_in': gw_in})
    dxm = matmul("proj_bwd_x", dproj, w_in, 'nt', R // 2 if R % 16 == 0 else R, 1024, 1024)
    grad_x, dl0g, dl0b, dsc1, dsh1 = stage_bwd(
        "norm0_bwd", f_norm0, T, nt, x_n0, ln0, [(dx0_a, D, 0, -1), (dxm, D, 0, 0)],
        [None, ('new', L, D, -1)], [True] * 4 + [False])

    zero = jnp.zeros((D,), F32)
    flat = lambda v: v.reshape(-1)
    small = {
        'loss': flat(loss), 'ln0_g': flat(dl0g), 'ln0_b': flat(dl0b),
        'dmod_x': jnp.concatenate([flat(dsh1[1]), flat(dsc1[1]), flat(dg1), flat(dsh2), flat(dsc2), flat(dg2)]),
        'dmod_c': jnp.concatenate([flat(dsh1[0]), flat(dsc1[0]), zero, zero, zero, zero]),
        'conv_w': flat(dcw8[:5]), 'conv_b': flat(dcb), 'dt_bias': flat(ddtb[:, :32]),
        'a_log': flat((dal_f + dal_b)[:, :32]),
        'd_skip': flat(jnp.tile(ddcol.reshape(1, NH, 64).sum(-1), (2, 1))),
        'ssd_norm_g': flat(dng), 'gm_norm_g': flat(dgmg), 'gm_norm_b': flat(dgmb),
        'w_spatial': flat(jnp.stack(dws)), 'b_spatial': flat(jnp.stack(dbs)), 'b_gate': flat(dbg),
        'ln1_g': flat(dl1g), 'ln1_b': flat(dl1b), 'ln2_g': flat(dl2g), 'ln2_b': flat(dl2b),
    }
    return grad_x, small


def _place():
    return lax.axis_index("x"), lax.axis_index("y"), lax.axis_index("c")


def allgather8(name, blk, hbm):
    space = pl.ANY if hbm else pltpu.VMEM

    def body(x_ref, out_ref, send_sems, recv_sems, local_sem):
        x, y, c = _place()
        me, sibling = (x, y, c), (x, y, 1 - c)
        chips = [(1 - x, y), (x, 1 - y), (1 - x, 1 - y)]

        def slot(px, py, pc):
            return out_ref.at[4 * px + 2 * py + pc]

        def copy(k, block, to, src=None):
            return pltpu.make_async_remote_copy(
                src_ref=slot(*block) if src is None else src, dst_ref=slot(*block),
                send_sem=send_sems.at[k], recv_sem=recv_sems.at[k], device_id=to, device_id_type=MESH)

        mine = pltpu.make_async_copy(x_ref, slot(*me), local_sem)
        mine.start()
        first = [copy(0, me, sibling, src=x_ref)]
        first += [copy(1 + j, me, (*chip, c), src=x_ref) for j, chip in enumerate(chips)]
        for cp in first:
            cp.start()
        passed = [copy(4 + j, (*chip, c), sibling) for j, chip in enumerate(chips)]
        for j, chip in enumerate(chips):
            copy(1 + j, (*chip, c), me).wait_recv()
            passed[j].start()
        copy(0, sibling, me).wait_recv()
        for j, chip in enumerate(chips):
            copy(4 + j, (*chip, 1 - c), me).wait_recv()
        for cp in first + passed:
            cp.wait_send()
        mine.wait()

    return pl.pallas_call(
        body, name=name, out_shape=jax.ShapeDtypeStruct((8,) + blk.shape, blk.dtype),
        in_specs=[pl.BlockSpec(memory_space=space)], out_specs=pl.BlockSpec(memory_space=space),
        scratch_shapes=[pltpu.SemaphoreType.DMA((7,)), pltpu.SemaphoreType.DMA((7,)), pltpu.SemaphoreType.DMA],
        compiler_params=pltpu.CompilerParams(vmem_limit_bytes=VMEM_LIMIT_V7X))(blk)


def _peers(place):
    x, y, c = place
    return [((1 - x) if k & 4 else x, (1 - y) if k & 2 else y, (1 - c) if k & 1 else c) for k in range(1, 8)]


def _slot(p):
    return 4 * p[0] + 2 * p[1] + p[2]


def plan_gather(place, srcs, lands):
    remote = [(s, l.at[_slot(place)], to) for s, l in zip(srcs, lands) for to in _peers(place)]
    return remote, [(s, l.at[_slot(place)]) for s, l in zip(srcs, lands)]


def plan_to_owner(place, srcs, lands):
    remote = [(s.at[2 * to[0] + to[1], to[2]], l.at[_slot(place)], to) for s, l in zip(srcs, lands) for to in _peers(place)]
    x, y, c = place
    return remote, [(s.at[2 * x + y, c], l.at[_slot(place)]) for s, l in zip(srcs, lands)]


def sequencer_exchange(name, collective_id, srcs, land_shapes, plan):
    n = len(srcs)
    src_refs = [jax.new_ref(a, memory_space=pltpu.MemorySpace.HBM) for a in srcs]
    land_refs = [jax.empty_ref(s, memory_space=pltpu.MemorySpace.HBM) for s in land_shapes]

    @pl.kernel(mesh=plsc.ScalarSubcoreMesh(axis_name="sequencer", num_cores=1), name=name,
               scratch_types=(pltpu.SemaphoreType.DMA((7 * n,)), pltpu.SemaphoreType.DMA((7 * n,)),
                              pltpu.SemaphoreType.DMA((n,))),
               compiler_params=pltpu.CompilerParams(collective_id=collective_id))
    def launch(send_sems, recv_sems, local_sems):
        place = _place()
        barrier = pltpu.get_barrier_semaphore()
        for to in _peers(place):
            pl.semaphore_signal(barrier, inc=1, device_id=to, device_id_type=MESH)
        pl.semaphore_wait(barrier, 7)
        remote, local = plan(place, src_refs, land_refs)
        mine = [pltpu.make_async_copy(s, d, local_sems.at[a]) for a, (s, d) in enumerate(local)]
        for cp in mine:
            cp.start()
        cps = [pltpu.make_async_remote_copy(src_ref=s, dst_ref=d, send_sem=send_sems.at[k], recv_sem=recv_sems.at[k],
                                            device_id=to, device_id_type=MESH) for k, (s, d, to) in enumerate(remote)]
        for cp in cps:
            cp.start()
        for cp in mine:
            cp.wait()
        for cp in cps:
            cp.wait()

    launch()
    return land_refs


def sibling_pair(name, hs):
    n = len(hs)

    def body(*refs):
        ins, outs = refs[:n], refs[n:2 * n]
        send_sems, recv_sems = refs[2 * n:]
        x, y, c = _place()
        cps = [pltpu.make_async_remote_copy(src_ref=outs[a].at[c], dst_ref=outs[a].at[c], send_sem=send_sems.at[a],
                                            recv_sem=recv_sems.at[a], device_id=(x, y, 1 - c), device_id_type=MESH)
               for a in range(n)]
        for cp in cps:
            cp.start()
        for a in range(n):
            pltpu.make_async_remote_copy(src_ref=outs[a].at[1 - c], dst_ref=outs[a].at[1 - c], send_sem=send_sems.at[a],
                                         recv_sem=recv_sems.at[a], device_id=(x, y, 1 - c),
                                         device_id_type=MESH).wait_recv()
        for cp in cps:
            cp.wait_send()

    any_spec = pl.BlockSpec(memory_space=pl.ANY)
    return pl.pallas_call(
        body, name=name, out_shape=[jax.ShapeDtypeStruct(h.shape, h.dtype) for h in hs],
        in_specs=[any_spec] * n, out_specs=[any_spec] * n, input_output_aliases={a: a for a in range(n)},
        scratch_shapes=[pltpu.SemaphoreType.DMA((n,)), pltpu.SemaphoreType.DMA((n,))])(*hs)


def owner_sum(name, land):
    _, r, w = land.shape
    T = r // 2

    def body(_, l_ref, o_ref):
        acc = l_ref[0].astype(F32)
        for j in range(1, 8):
            acc = acc + l_ref[j].astype(F32)
        o_ref[...] = acc

    grid_spec = pltpu.PrefetchScalarGridSpec(
        num_scalar_prefetch=1, grid=(2,),
        in_specs=[pl.BlockSpec((8, T, w), lambda i, at: (0, i, 0))],
        out_specs=pl.BlockSpec((None, T, w), lambda i, at: (at[0], i, 0)))
    at = jnp.stack([lax.axis_index("c")]).astype(jnp.int32)
    return pl.pallas_call(body, name=name, grid_spec=grid_spec, out_shape=jax.ShapeDtypeStruct((2, r, w), F32),
                          compiler_params=_cp("arbitrary"))(at, land)


W_IN_RUNS = ((0, 2, 1296, 376), (376, 3, 0, 1672), (2048, 1, 920, 752), (2800, 2, 0, 1296), (4096, 0, 0, 1024),
             (5120, 0, 1024, 648), (5768, 1, 0, 920))


def w_in_to_padded(name, g4):
    T = 128

    def body(g_ref, o_ref):
        o_ref[:, D_PROJ:PW] = jnp.zeros((T, PW - D_PROJ), o_ref.dtype)
        for (a, s, j0, w) in W_IN_RUNS:
            o_ref[:, a:a + w] = g_ref[s, :, j0:j0 + w]

    return pl.pallas_call(body, name=name, grid=(D // T,), in_specs=[pl.BlockSpec((4, T, 1672), lambda i: (0, i, 0))],
                          out_specs=pl.BlockSpec((T, PW), lambda i: (i, 0)),
                          out_shape=jax.ShapeDtypeStruct((D, PW), g4.dtype), compiler_params=_cp("arbitrary"))(g4)


def w_in_from_padded(name, gp):
    T = 128

    def body(g_ref, o_ref):
        for (a, s, j0, w) in W_IN_RUNS:
            o_ref[s, :, j0:j0 + w] = g_ref[:, a:a + w]

    return pl.pallas_call(body, name=name, grid=(D // T,), in_specs=[pl.BlockSpec((T, PW), lambda i: (i, 0))],
                          out_specs=pl.BlockSpec((4, T, 1672), lambda i: (0, i, 0)),
                          out_shape=jax.ShapeDtypeStruct((4, D, 1672), gp.dtype), compiler_params=_cp("arbitrary"))(gp)


def sum_devices(name, g):
    def body(g_ref, o_ref):
        acc = g_ref[0]
        for k in range(1, 8):
            acc = acc + g_ref[k]
        o_ref[...] = acc

    return pl.pallas_call(body, name=name, out_shape=jax.ShapeDtypeStruct(g.shape[1:], F32),
                          compiler_params=pltpu.CompilerParams(vmem_limit_bytes=VMEM_LIMIT_V7X))(g)


def adamw(name, w, g, m, v, T):
    r, wd = w.shape
    c1 = 1.0 - ADAM_B1 ** ADAM_STEP
    c2 = 1.0 - ADAM_B2 ** ADAM_STEP

    def body(w_ref, g_ref, m_ref, v_ref, d_ref, mo_ref, vo_ref):
        gv = g_ref[...]
        mn = ADAM_B1 * m_ref[...] + (1.0 - ADAM_B1) * gv
        vn = ADAM_B2 * v_ref[...] + (1.0 - ADAM_B2) * (gv * gv)
        d_ref[...] = -ADAM_LR * ((mn / c1) / (jnp.sqrt(vn / c2) + ADAM_EPS) + ADAM_WD * w_ref[...])
        mo_ref[...] = mn
        vo_ref[...] = vn

    spec = pl.BlockSpec((T, wd), lambda i: (i, 0))
    return pl.pallas_call(body, name=name, grid=(r // T,), in_specs=[spec] * 4, out_specs=[spec] * 3,
                          out_shape=[jax.ShapeDtypeStruct((r, wd), F32)] * 3, compiler_params=_cp("arbitrary"))(w, g, m, v)


BIG = {'w_in': (1024, 1672), 'w_ssd_proj': (256, 1024), 'w_gm_proj': (256, 1024), 'w_out': (256, 1024),
       'w_ff1': (1024, 704), 'w_ff3': (1024, 704), 'w_ff2': (704, 1024)}


class Flat:
    def __init__(self, segs):
        self.off, o = {}, 0
        for name, size in segs:
            self.off[name] = (o, size)
            o += -(-size // 128) * 128
        self.rows = -(-o // 1024) * 8

    def pack(self, vals):
        parts = []
        for name, (o, size) in self.off.items():
            v = vals[name].reshape(-1).astype(F32)
            parts.append(jnp.pad(v, (0, -(-size // 128) * 128 - size)))
        buf = jnp.concatenate(parts)
        return jnp.pad(buf, (0, self.rows * 128 - buf.shape[0])).reshape(self.rows, 128)

    def get(self, buf, name, shape=None):
        o, size = self.off[name]
        v = buf[o // 128:(o + size + 127) // 128].reshape(-1)[:size]
        return v if shape is None else v.reshape(shape)


PARTIALS = Flat([('loss', 1), ('ln0_g', D), ('ln0_b', D), ('dmod_x', 6 * D), ('dmod_c', 6 * D), ('conv_w', 5 * 1536),
                 ('conv_b', 1536), ('dt_bias', 32), ('a_log', 32), ('d_skip', 32), ('ssd_norm_g', D),
                 ('gm_norm_g', D), ('gm_norm_b', D), ('w_spatial', 8 * Q * Q), ('b_spatial', 8 * Q), ('b_gate', 2 * D),
                 ('ln1_g', D), ('ln1_b', D), ('ln2_g', D), ('ln2_b', D)])

WEIGHTS = ('c_ctx', 'ln0_g', 'ln0_b', 'w_ada', 'b_ada', 'w_in', 'conv_w', 'conv_b', 'dt_bias', 'a_log', 'd_skip',
           'ssd_norm_g', 'gm_norm_g', 'gm_norm_b', 'w_spatial', 'b_spatial', 'b_gate', 'w_ssd_proj', 'w_gm_proj',
           'w_out', 'ln1_g', 'ln1_b', 'w_ff1', 'w_ff3', 'w_ff2', 'ln2_g', 'ln2_b')
BIG_NAMES = tuple(BIG)
SMALL_NAMES = tuple(n for n in WEIGHTS if n not in BIG_NAMES and n != 'w_ada')


def kernel(x, c, ctx, c_ctx, ln0_g, ln0_b, w_ada, b_ada, w_in, conv_w, conv_b, dt_bias, a_log, d_skip, ssd_norm_g, gm_norm_g, gm_norm_b, w_spatial, b_spatial, b_gate, w_ssd_proj, w_gm_proj, w_out, ln1_g, ln1_b, w_ff1, w_ff3, w_ff2, ln2_g, ln2_b, loss_target, m_c_ctx, m_ln0_g, m_ln0_b, m_w_ada, m_b_ada, m_w_in, m_conv_w, m_conv_b, m_dt_bias, m_a_log, m_d_skip, m_ssd_norm_g, m_gm_norm_g, m_gm_norm_b, m_w_spatial, m_b_spatial, m_b_gate, m_w_ssd_proj, m_w_gm_proj, m_w_out, m_ln1_g, m_ln1_b, m_w_ff1, m_w_ff3, m_w_ff2, m_ln2_g, m_ln2_b, v_c_ctx, v_ln0_g, v_ln0_b, v_w_ada, v_b_ada, v_w_in, v_conv_w, v_conv_b, v_dt_bias, v_a_log, v_d_skip, v_ssd_norm_g, v_gm_norm_g, v_gm_norm_b, v_w_spatial, v_b_spatial, v_b_gate, v_w_ssd_proj, v_w_gm_proj, v_w_out, v_ln1_g, v_ln1_b, v_w_ff1, v_w_ff3, v_w_ff2, v_ln2_g, v_ln2_b):
    wts = dict(c_ctx=c_ctx, ln0_g=ln0_g, ln0_b=ln0_b, w_ada=w_ada, b_ada=b_ada, w_in=w_in, conv_w=conv_w, conv_b=conv_b,
               dt_bias=dt_bias, a_log=a_log, d_skip=d_skip, ssd_norm_g=ssd_norm_g, gm_norm_g=gm_norm_g,
               gm_norm_b=gm_norm_b, w_spatial=w_spatial, b_spatial=b_spatial, b_gate=b_gate, w_ssd_proj=w_ssd_proj,
               w_gm_proj=w_gm_proj, w_out=w_out, ln1_g=ln1_g, ln1_b=ln1_b, w_ff1=w_ff1, w_ff3=w_ff3, w_ff2=w_ff2,
               ln2_g=ln2_g, ln2_b=ln2_b)
    ms = dict(zip(WEIGHTS, (m_c_ctx, m_ln0_g, m_ln0_b, m_w_ada, m_b_ada, m_w_in, m_conv_w, m_conv_b, m_dt_bias, m_a_log,
                            m_d_skip, m_ssd_norm_g, m_gm_norm_g, m_gm_norm_b, m_w_spatial, m_b_spatial, m_b_gate,
                            m_w_ssd_proj, m_w_gm_proj, m_w_out, m_ln1_g, m_ln1_b, m_w_ff1, m_w_ff3, m_w_ff2, m_ln2_g,
                            m_ln2_b)))
    vs = dict(zip(WEIGHTS, (v_c_ctx, v_ln0_g, v_ln0_b, v_w_ada, v_b_ada, v_w_in, v_conv_w, v_conv_b, v_dt_bias, v_a_log,
                            v_d_skip, v_ssd_norm_g, v_gm_norm_g, v_gm_norm_b, v_w_spatial, v_b_spatial, v_b_gate,
                            v_w_ssd_proj, v_w_gm_proj, v_w_out, v_ln1_g, v_ln1_b, v_w_ff1, v_w_ff3, v_w_ff2, v_ln2_g,
                            v_ln2_b)))
    px, py, pc = _place()
    shard = 2 * px + py
    dev = 2 * shard + pc
    take = lambda a, i, axis=0: lax.dynamic_index_in_dim(a, i, axis, keepdims=False)

    half = lambda n: take(wts[n][0].reshape(2, BIG[n][0] // 2, BIG[n][1]), pc).astype(BF16)

    pre = jnp.concatenate([c, jnp.pad(conv_w[0], ((0, 0), (0, D - 384))), jnp.zeros((2, D), F32)], axis=0)
    pre = allgather8("gather_cond", pre, False)
    conv_w_full = pre[0::2, 1:6, :384].transpose(1, 0, 2).reshape(5, 1536)
    a16 = jnp.concatenate([_silu(pre[:, 0, :]), _silu(c_ctx)[None], jnp.zeros((7, D), F32)], axis=0)
    mod = matmul("ada_fwd", a16, w_ada[0], 'nn', 16, 512, 1024)
    mod = mod + lax.dynamic_slice_in_dim(b_ada[0], shard * 1536, 1536)[None]
    mod = allgather8("gather_mod", mod, False)
    mod = jnp.concatenate([mod[0], mod[2], mod[4], mod[6]], axis=1)
    mod_x = take(mod, dev).reshape(6, D)
    mod_c = mod[8].reshape(6, D)

    def full(n, blocks):
        r, w = BIG[n]
        return blocks.reshape(4, r, w) if w != D else blocks.reshape(4 * r, w)

    class Exchanges:
        rest_names = BIG_NAMES[1:]

        def __init__(self):
            self.pending = []

        def w_in_block(self):
            return half('w_in')

        def w_in(self, blocks):
            w = w_in_to_padded("w_in_layout", full('w_in', blocks))
            halves = [half(n) for n in self.rest_names]
            halves[0], _ = lax.optimization_barrier((halves[0], blocks))
            lands = [jax.ShapeDtypeStruct((8,) + h.shape, BF16) for h in halves]
            self.rest_refs = sequencer_exchange("gather_rest", 1, halves, lands, plan_gather)
            return w

        def rest(self):
            return {n: full(n, r[...]) for n, r in zip(self.rest_names, self.rest_refs)}

        def grads(self, group, gs):
            if group == 'in':
                gs = {'w_in': w_in_from_padded("w_in_grad_layout", gs['w_in'])}
            blocks = [g.reshape(4, 2, BIG[n][0] // 2, BIG[n][1]) for n, g in gs.items()]
            lands = [jax.ShapeDtypeStruct((8,) + b.shape[2:], BF16) for b in blocks]
            refs = sequencer_exchange("grads_" + group, 2 + len(self.pending), blocks, lands, plan_to_owner)
            self.pending.append((tuple(gs), refs))

        def finish(self):
            names, halves = [], []
            for ns, refs in self.pending:
                names += ns
                halves += [owner_sum("grads_sum_" + n, r[...]) for n, r in zip(ns, refs)]
            return {n: h.reshape(BIG[n]) for n, h in zip(names, sibling_pair("grads_halves", halves))}

    S = dict(ln0_g=ln0_g, ln0_b=ln0_b, conv_w=conv_w_full, conv_b=conv_b[0], dt_bias=dt_bias[0], a_log=a_log[0],
             d_skip=d_skip[0], ssd_norm_g=ssd_norm_g[0], gm_norm_g=gm_norm_g[0], gm_norm_b=gm_norm_b[0],
             w_spatial=w_spatial[0], b_spatial=b_spatial[0], b_gate=b_gate[0], ln1_g=ln1_g[0], ln1_b=ln1_b[0],
             ln2_g=ln2_g[0], ln2_b=ln2_b[0])
    exchanges = Exchanges()
    grad_x, gsmall = core(ctx[0], x[0], loss_target[0], mod_x, mod_c, exchanges, S)

    parts = allgather8("gather_partials", PARTIALS.pack(gsmall), False)
    tot = sum_devices("partials_sum", parts)
    g_shards = exchanges.finish()
    g = {n: PARTIALS.get(tot, n) for n in ('ln0_g', 'ln0_b', 'conv_b', 'dt_bias', 'a_log', 'd_skip', 'ssd_norm_g',
                                           'gm_norm_g', 'gm_norm_b', 'w_spatial', 'b_spatial', 'b_gate', 'ln1_g',
                                           'ln1_b', 'ln2_g', 'ln2_b')}
    loss = PARTIALS.get(tot, 'loss', ())
    dmod_c = PARTIALS.get(tot, 'dmod_c')
    g['b_ada'] = PARTIALS.get(tot, 'dmod_x') + dmod_c
    g['conv_w'] = lax.dynamic_slice_in_dim(PARTIALS.get(tot, 'conv_w', (5, 1536)), shard * 384, 384, axis=1)
    o, size = PARTIALS.off['dmod_x']
    dmod_rows = parts[:, o // 128:(o + size) // 128].reshape(8, size)
    dm = jnp.concatenate([dmod_rows, dmod_c[None], jnp.zeros((7, 6 * D), F32)], axis=0)
    dm = lax.dynamic_slice_in_dim(dm, shard * 1536, 1536, axis=1)
    g['w_ada'] = matmul("ada_bwd_w", a16, dm, 'tn', 1024, 512, 16)
    dm_c = jnp.concatenate([dm[8:9], jnp.zeros((15, 1536), F32)], axis=0)
    dc = matmul("ada_bwd_c", dm_c, w_ada[0], 'nt', 16, 1024, 512)
    dc = allgather8("gather_dcctx", dc, False)[:, 0, :]
    dc = ((dc[0] + dc[2]) + dc[4]) + dc[6]
    sg = jax.nn.sigmoid(c_ctx)
    g['c_ctx'] = dc * (sg * (1.0 + c_ctx * (1.0 - sg)))
    for n in BIG_NAMES:
        g[n] = g_shards[n]

    delta, new_m, new_v = {}, {}, {}
    for n in BIG_NAMES + ('w_ada',):
        r, w = wts[n].shape[1:]
        if w % 128:
            T = max(t for t in range(8, 257, 8) if w % t == 0)
            d_, m_, v_ = adamw("adamw_" + n, wts[n][0].T, g[n].T, ms[n][0].T, vs[n][0].T, T)
            delta[n], new_m[n], new_v[n] = d_.T, m_.T, v_.T
        else:
            T = 352 if n == 'w_ff2' else 256
            delta[n], new_m[n], new_v[n] = adamw("adamw_" + n, wts[n][0], g[n], ms[n][0], vs[n][0], T)
    lay = Flat([(n, wts[n].size) for n in SMALL_NAMES])
    d_, m_, v_ = adamw("adamw_small", lay.pack(wts), lay.pack(g), lay.pack(ms), lay.pack(vs), lay.rows)
    for n in SMALL_NAMES:
        delta[n], new_m[n], new_v[n] = (lay.get(b, n) for b in (d_, m_, v_))

    shp = lambda d: [d[n].reshape(wts[n].shape) for n in WEIGHTS]
    return (loss, grad_x[None], *shp(g), *shp(delta), *shp(new_m), *shp(new_v))
```

```python
import functools

import jax
import jax.numpy as jnp
from jax import lax
from jax.experimental import pallas as pl
from jax.experimental.pallas import tpu as pltpu
from jax.experimental.pallas import tpu_sc as plsc

F32 = jnp.float32
BF16 = jnp.bfloat16
MESH = pl.DeviceIdType.MESH

VMEM_LIMIT_V7X = 56 * 1024 * 1024

D = 1024
LC = 256
Q = 128
NH = 16
D_FF = 2816
LN_EPS = 1e-5
ALPHA = 2.0 ** 0.25

PW = 7168
C_GATE, C_UV, C_Z, C_XBC, C_DT = 0, 2048, 4096, 5120, 6656
D_PROJ = 6688

ADAM_LR, ADAM_B1, ADAM_B2, ADAM_EPS, ADAM_WD, ADAM_STEP = 0.001, 0.9, 0.999, 1e-08, 0.01, 10


def _cp(*sem):
    return pltpu.CompilerParams(dimension_semantics=sem, vmem_limit_bytes=VMEM_LIMIT_V7X)


def _dot(a, b, ca, cb):
    return lax.dot_general(a.astype(BF16), b.astype(BF16), (((ca,), (cb,)), ((), ())),
                           preferred_element_type=F32)


@jax.custom_vjp
def mm(a, b):
    return _dot(a, b, 1, 0)


mm.defvjp(lambda a, b: (_dot(a, b, 1, 0), (a, b)),
          lambda r, g: (_dot(g, r[1], 1, 1), _dot(r[0], g, 0, 0)))


@jax.custom_vjp
def mm_nt(a, b):
    return _dot(a, b, 1, 1)


mm_nt.defvjp(lambda a, b: (_dot(a, b, 1, 1), (a, b)),
             lambda r, g: (_dot(g, r[1], 1, 0), _dot(g, r[0], 0, 0)))


@jax.custom_vjp
def mm_tn(a, b):
    return _dot(a, b, 0, 0)


mm_tn.defvjp(lambda a, b: (_dot(a, b, 0, 0), (a, b)),
             lambda r, g: (_dot(r[1], g, 1, 1), _dot(r[0], g, 1, 0)))


def _dot32(a, b):
    return lax.dot_general(a, b, (((1,), (0,)), ((), ())), precision=lax.Precision.HIGHEST,
                           preferred_element_type=F32)


def _cumsum_fn(rev):
    def tri(transpose):
        r = lax.broadcasted_iota(jnp.int32, (Q, Q), 0)
        c = lax.broadcasted_iota(jnp.int32, (Q, Q), 1)
        keep = (r >= c) if (rev == transpose) else (r <= c)
        return jnp.where(keep, 1.0, 0.0).astype(F32)

    @jax.custom_vjp
    def cums(a):
        return _dot32(tri(False), a)

    cums.defvjp(lambda a: (_dot32(tri(False), a), None), lambda _, g: (_dot32(tri(True), g),))
    return cums


def _cols(v, k):
    w = v.shape[1] // k
    return tuple(v[:, w * i:w * (i + 1)] for i in range(k))


def _splitter(k):
    @jax.custom_vjp
    def split(v):
        return _cols(v, k)

    @jax.custom_vjp
    def concat(ps):
        return jnp.concatenate(ps, axis=1)

    split.defvjp(lambda v: (_cols(v, k), None), lambda _, g: (jnp.concatenate(g, axis=1),))
    concat.defvjp(lambda ps: (jnp.concatenate(ps, axis=1), None), lambda _, g: (_cols(g, k),))
    return split, concat


split2, _ = _splitter(2)
split4, _ = _splitter(4)
split8, concat8 = _splitter(8)


def _ln(x, g, b):
    mu = jnp.mean(x, axis=-1, keepdims=True)
    xc = x - mu
    var = jnp.mean(xc * xc, axis=-1, keepdims=True)
    return xc * lax.rsqrt(var + LN_EPS) * g + b


def _silu(x):
    return x * jax.nn.sigmoid(x)


def _gelu(x):
    return 0.5 * x * (1.0 + jnp.tanh(0.7978845608028654 * (x + 0.044715 * (x * x * x))))


def _xspec(T, w, col, roff):
    return pl.BlockSpec((T, w), lambda i, col=col, roff=roff: (jnp.maximum(i + roff, 0), col))


def _pspec(p, sel):
    if sel is None:
        return pl.BlockSpec(p.shape, lambda i, n=p.ndim: (0,) * n)
    return pl.BlockSpec((1,) + p.shape[1:], lambda i, n=p.ndim: (sel(i),) + (0,) * (n - 1))


def _out_plumbing(outs, T, args, in_specs):
    shapes, specs, aliases = [], [], {}
    for k, o in enumerate(outs):
        if o[0] == 'new':
            _, rows, w, roff = o[:4]
            shapes.append(jax.ShapeDtypeStruct((rows, w), o[4] if len(o) > 4 else F32))
            specs.append(_xspec(T, w, 0, roff))
        elif o[0] == 'acc':
            shapes.append(jax.ShapeDtypeStruct(o[1], F32))
            specs.append(pl.BlockSpec(o[1], lambda i, n=len(o[1]): (0,) * n))
        elif o[0] == 'part':
            _, rows, wtot, w, col, roff, dtype = o
            shapes.append(jax.ShapeDtypeStruct((rows, wtot), dtype))
            specs.append(_xspec(T, w, col, roff))
        else:
            _, arr, w, col, roff = o
            aliases[len(args)] = k
            args.append(arr)
            in_specs.append(pl.BlockSpec(memory_space=pl.ANY))
            shapes.append(jax.ShapeDtypeStruct(arr.shape, arr.dtype))
            specs.append(_xspec(T, w, col, roff))
    return shapes, specs, aliases


def stage_fwd(name, f, T, n, xs, ps, outs, rows=None, gather=None):
    nx, npar = len(xs), len(ps)
    args = [x[0] for x in xs] + [p[0] for p in ps]
    in_specs = [_xspec(T, w, col, roff) for (_, w, col, roff) in xs] + [_pspec(p, sel) for (p, sel) in ps]
    n_in = len(args)
    shapes, specs, aliases = _out_plumbing(outs, T, args, in_specs)
    n_all_in = len(args)
    scratch = []
    if gather is not None:
        args.append(gather)
        in_specs.append(pl.BlockSpec(memory_space=pl.ANY))
        shapes.append(jax.ShapeDtypeStruct((8,) + gather.shape, gather.dtype))
        specs.append(pl.BlockSpec(memory_space=pl.ANY))
        scratch = [pltpu.SemaphoreType.DMA((7,)), pltpu.SemaphoreType.DMA((7,)), pltpu.SemaphoreType.DMA]

    def exchange(i, blk_ref, out_ref, send_sems, recv_sems, local_sem):
        x, y, c = _place()
        me, sibling = (x, y, c), (x, y, 1 - c)
        chips = [(1 - x, y), (x, 1 - y), (1 - x, 1 - y)]

        def copy(k, block, to, own=False):
            dst = out_ref.at[_slot(block)]
            return pltpu.make_async_remote_copy(src_ref=blk_ref if own else dst, dst_ref=dst, send_sem=send_sems.at[k],
                                                recv_sem=recv_sems.at[k], device_id=to, device_id_type=MESH)

        mine = pltpu.make_async_copy(blk_ref, out_ref.at[_slot(me)], local_sem)
        first = [copy(0, me, sibling, own=True)] + [copy(1 + j, me, (*ch, c), own=True) for j, ch in enumerate(chips)]
        passed = [copy(4 + j, (*ch, c), sibling) for j, ch in enumerate(chips)]

        @pl.when(i == 0)
        def _():
            mine.start()
            for cp in first:
                cp.start()

        @pl.when(i == n // 2)
        def _():
            for j, ch in enumerate(chips):
                copy(1 + j, (*ch, c), me).wait_recv()
                passed[j].start()

        @pl.when(i == n - 1)
        def _():
            copy(0, sibling, me).wait_recv()
            for j, ch in enumerate(chips):
                copy(4 + j, (*ch, 1 - c), me).wait_recv()
            for cp in first + passed:
                cp.wait_send()
            mine.wait()

    n_out = len(outs)

    def body(*refs):
        i = pl.program_id(0)
        if gather is not None:
            exchange(i, refs[n_all_in], refs[n_all_in + 1 + n_out], *refs[n_all_in + 2 + n_out:])
        out_refs = refs[len(args):len(args) + n_out]
        pv = [r[...] if ps[k][1] is None else r[0] for k, r in enumerate(refs[nx:n_in])]
        sums = {}
        for r0 in range(0, T, rows or T):
            g = slice(r0, r0 + (rows or T))
            res = f(*[r[g, :] for r in refs[:nx]], *pv)
            for k, o_ref in enumerate(out_refs):
                if outs[k][0] == 'acc':
                    sums[k] = res[k] if r0 == 0 else sums[k] + res[k]
                else:
                    o_ref[g, :] = res[k].astype(o_ref.dtype)
        for k, v in sums.items():
            o_ref = out_refs[k]

            @pl.when(i == 0)
            def _(o_ref=o_ref, v=v):
                o_ref[...] = v

            @pl.when(i > 0)
            def _(o_ref=o_ref, v=v):
                o_ref[...] += v

    return pl.pallas_call(body, name=name, grid=(n,), in_specs=in_specs, out_specs=specs, out_shape=shapes,
                          input_output_aliases=aliases, scratch_shapes=scratch,
                          compiler_params=_cp("arbitrary"))(*args)


def stage_bwd(name, f, T, n, xs, ps, cts, dxs, dps, primal=(), rows=None):
    nx, npar = len(xs), len(ps)
    args = [x[0] for x in xs] + [p[0] for p in ps]
    in_specs = [_xspec(T, w, col, roff) for (_, w, col, roff) in xs] + [_pspec(p, sel) for (p, sel) in ps]
    ct_arrs = [c for c in cts if isinstance(c, tuple)]
    for (a, w, col, roff) in ct_arrs:
        args.append(a)
        in_specs.append(_xspec(T, w, col, roff))
    n_in = len(args)
    outs, out_of = [], []
    for k, o in enumerate(dxs):
        if o is not None:
            outs.append(o)
            out_of.append(('x', k))
    for k, want in enumerate(dps):
        if want:
            p, sel = ps[k]
            outs.append(('acc', p.shape))
            out_of.append(('p', k))
    for k, shape in primal:
        outs.append(('acc', shape))
        out_of.append(('r', k))
    shapes, specs, aliases = _out_plumbing(outs, T, args, in_specs)
    for j, (kind, k) in enumerate(out_of):
        if kind == 'p' and ps[k][1] is not None:
            p, sel = ps[k]
            specs[j] = pl.BlockSpec((1,) + p.shape[1:], lambda i, n=p.ndim, sel=sel: (sel(i),) + (0,) * (n - 1))
    n_all_in = len(args)

    def body(*refs):
        i = pl.program_id(0)
        pv = [r[...] if ps[k][1] is None else r[0] for k, r in enumerate(refs[nx:nx + npar])]
        sums = {}
        for r0 in range(0, T, rows or T):
            g = slice(r0, r0 + (rows or T))
            res, vjp_fn = jax.vjp(f, *[r[g, :] for r in refs[:nx]], *pv)
            ctv, q = [], nx + npar
            for k, c in enumerate(cts):
                if c is None:
                    ctv.append(jnp.zeros_like(res[k]))
                elif isinstance(c, tuple):
                    v = refs[q][g, :]
                    if c[3] < 0:
                        v = v * (i + c[3] >= 0).astype(F32)
                    ctv.append(v)
                    q += 1
                else:
                    ctv.append(jnp.full_like(res[k], c))
            grads = vjp_fn(tuple(ctv))
            for j, o_ref in enumerate(refs[n_all_in:]):
                kind, k = out_of[j]
                if kind == 'x':
                    o_ref[g, :] = grads[k].astype(o_ref.dtype)
                else:
                    v = res[k] if kind == 'r' else grads[nx + k]
                    sums[j] = v if r0 == 0 else sums[j] + v
        for j, v in sums.items():
            kind, k = out_of[j]
            o_ref = refs[n_all_in + j]
            sel = None if kind == 'r' else ps[k][1]
            if sel is None:
                first, tgt = i == 0, o_ref
            else:
                first, tgt = jnp.logical_or(i == 0, sel(i) != sel(jnp.maximum(i - 1, 0))), o_ref.at[0]

            @pl.when(first)
            def _(tgt=tgt, v=v):
                tgt[...] = v

            @pl.when(jnp.logical_not(first))
            def _(tgt=tgt, v=v):
                tgt[...] += v

    return pl.pallas_call(body, name=name, grid=(n,), in_specs=in_specs, out_specs=specs, out_shape=shapes,
                          input_output_aliases=aliases, compiler_params=_cp("arbitrary"))(*args)


_CONTRACT = {'nn': (1, 0), 'nt': (1, 1), 'tn': (0, 0)}


def matmul(name, a, b, mode, tm, tn, tk, out_dtype=F32, add=None):
    if mode == 'nn':
        (M, K), (_, N) = a.shape, b.shape
    elif mode == 'nt':
        (M, K), (N, _) = a.shape, b.shape
    else:
        (K, M), (_, N) = a.shape, b.shape
    assert M % tm == 0 and N % tn == 0 and K % tk == 0, (name, M, N, K, tm, tn, tk)
    a_spec = (pl.BlockSpec((tk, tm), lambda j, i, k: (k, i)) if mode == 'tn'
              else pl.BlockSpec((tm, tk), lambda j, i, k: (i, k)))
    b_spec = (pl.BlockSpec((tn, tk), lambda j, i, k: (j, k)) if mode == 'nt'
              else pl.BlockSpec((tk, tn), lambda j, i, k: (k, j)))
    o_spec = pl.BlockSpec((tm, tn), lambda j, i, k: (i, j))
    return matmul_call(name, (N // tn, M // tm, K // tk), a, a_spec, b, b_spec, (M, N), o_spec, (tm, tn), mode,
                       out_dtype, add)


def matmul_call(name, grid, a, a_spec, b, b_spec, out_shape, o_spec, tile, mode, out_dtype=F32, add=None):
    tm, tn = tile
    nk = grid[2]
    ca, cb = _CONTRACT[mode]
    args, in_specs = [a, b], [a_spec, b_spec]
    if add is not None:
        args.append(add)
        in_specs.append(o_spec)

    def body(*refs):
        a_ref, b_ref = refs[0], refs[1]
        o_ref, acc = refs[-2], refs[-1]
        k = pl.program_id(2)
        if nk == 1:
            p = _dot(a_ref[...], b_ref[...], ca, cb)
            o_ref[...] = (p + refs[2][...] if add is not None else p).astype(out_dtype)
            return

        @pl.when(k == 0)
        def _():
            acc[...] = refs[2][...] if add is not None else jnp.zeros_like(acc)

        acc[...] += _dot(a_ref[...], b_ref[...], ca, cb)

        @pl.when(k == nk - 1)
        def _():
            o_ref[...] = acc[...].astype(out_dtype)

    return pl.pallas_call(body, name=name, grid=grid, in_specs=in_specs, out_specs=o_spec,
                          out_shape=jax.ShapeDtypeStruct(out_shape, out_dtype),
                          scratch_shapes=[pltpu.VMEM((tm, tn) if nk > 1 else (8, 128), F32)],
                          compiler_params=_cp("arbitrary", "arbitrary", "arbitrary"))(*args)


NS, WS = 4, 704


def _resident(name, M, tm, rows, weight, out_shape, out_block, out_map, step):
    in_specs = [pl.BlockSpec(rows[1], rows[2]), pl.BlockSpec(weight.shape, lambda i, n=weight.ndim: (0,) * n)]
    return pl.pallas_call(step, name=name, grid=(M // tm,), in_specs=in_specs, out_specs=pl.BlockSpec(out_block, out_map),
                          out_shape=jax.ShapeDtypeStruct(out_shape, F32), compiler_params=_cp("arbitrary"))(rows[0], weight)


def ffn_in_fwd(name, h, w1, w3, tm):
    M = h.shape[0]

    def step(h_ref, w1_ref, w3_ref, a1_ref, a3_ref, act_ref):
        for s in range(NS):
            a1 = _dot(h_ref[...], w1_ref[s], 1, 0)
            a3 = _dot(h_ref[...], w3_ref[s], 1, 0)
            a1_ref[s] = a1.astype(a1_ref.dtype)
            a3_ref[s] = a3.astype(a3_ref.dtype)
            act_ref[s] = (_silu(a1) * a3).astype(act_ref.dtype)

    wspec = pl.BlockSpec((NS, D, WS), lambda i: (0, 0, 0))
    ospec = pl.BlockSpec((NS, tm, WS), lambda i: (0, i, 0))
    return pl.pallas_call(
        step, name=name, grid=(M // tm,), in_specs=[pl.BlockSpec((tm, D), lambda i: (i, 0)), wspec, wspec],
        out_specs=[ospec, ospec, ospec],
        out_shape=[jax.ShapeDtypeStruct((NS, M, WS), BF16)] * 3, compiler_params=_cp("arbitrary"))(h, w1, w3)


def ffn_out_bwd_x(name, dff, w2, a1, a3, tm):
    M = dff.shape[0]

    def step(d_ref, w_ref, a1_ref, a3_ref, da1_ref, da3_ref):
        for s in range(NS):
            dact = _dot(d_ref[...], w_ref[s * WS:(s + 1) * WS, :], 1, 1)
            a1 = a1_ref[s].astype(F32)
            sig = jax.nn.sigmoid(a1)
            da3_ref[s] = (dact * (a1 * sig)).astype(da3_ref.dtype)
            da1_ref[s] = (dact * a3_ref[s].astype(F32) * (sig * (1.0 + a1 * (1.0 - sig)))).astype(da1_ref.dtype)

    aspec = pl.BlockSpec((NS, tm, WS), lambda i: (0, i, 0))
    return pl.pallas_call(
        step, name=name, grid=(M // tm,),
        in_specs=[pl.BlockSpec((tm, D), lambda i: (i, 0)), pl.BlockSpec(w2.shape, lambda i: (0, 0)), aspec, aspec],
        out_specs=[aspec, aspec],
        out_shape=[jax.ShapeDtypeStruct((NS, M, WS), BF16)] * 2, compiler_params=_cp("arbitrary"))(dff, w2, a1, a3)


def ff_in_bwd_x(name, da1, da3, w1, w3, tm):
    M = da1.shape[1]

    def step(d1_ref, d3_ref, w1_ref, w3_ref, o_ref):
        def product(d_ref, w_ref):
            acc = _dot(d_ref[0], w_ref[0], 1, 1)
            for s in range(1, NS):
                acc = acc + _dot(d_ref[s], w_ref[s], 1, 1)
            return acc

        o_ref[...] = product(d3_ref, w3_ref) + product(d1_ref, w1_ref)

    dspec = pl.BlockSpec((NS, tm, WS), lambda i: (0, i, 0))
    wspec = pl.BlockSpec((NS, D, WS), lambda i: (0, 0, 0))
    return pl.pallas_call(step, name=name, grid=(M // tm,), in_specs=[dspec, dspec, wspec, wspec],
                          out_specs=pl.BlockSpec((tm, D), lambda i: (i, 0)),
                          out_shape=jax.ShapeDtypeStruct((M, D), F32), compiler_params=_cp("arbitrary"))(da1, da3, w1, w3)


def ff_in_bwd_w(name, h, da3, tk):
    M = h.shape[0]

    def step(h_ref, d_ref, acc):
        for s in range(NS):
            acc[s] += _dot(h_ref[...], d_ref[s], 0, 0)

    return _token_sum(name, M // tk, [pl.BlockSpec((tk, D), lambda k: (k, 0)), pl.BlockSpec((NS, tk, WS), lambda k: (0, k, 0))],
                      (NS, D, WS), (NS, D, WS), step, (h, da3))


def ff_out_fwd(name, act3, w2, tm):
    M = act3.shape[1]

    def step(a_ref, w_ref, o_ref):
        acc = _dot(a_ref[0], w_ref[0:WS, :], 1, 0)
        for s in range(1, NS):
            acc = acc + _dot(a_ref[s], w_ref[s * WS:(s + 1) * WS, :], 1, 0)
        o_ref[...] = acc

    return _resident(name, M, tm, (act3, (NS, tm, WS), lambda i: (0, i, 0)), w2, (M, D), (tm, D), lambda i: (i, 0), step)


def _token_sum(name, nk, in_specs, out_shape, acc_shape, step, args):
    def body(*refs):
        o_ref, acc = refs[-2], refs[-1]
        k = pl.program_id(0)

        @pl.when(k == 0)
        def _():
            acc[...] = jnp.zeros_like(acc)

        step(*refs[:-2], acc)

        @pl.when(k == nk - 1)
        def _():
            o_ref[...] = acc[...].astype(o_ref.dtype)

    return pl.pallas_call(body, name=name, grid=(nk,), in_specs=in_specs,
                          out_specs=pl.BlockSpec(out_shape, lambda k, n=len(out_shape): (0,) * n),
                          out_shape=jax.ShapeDtypeStruct(out_shape, BF16), scratch_shapes=[pltpu.VMEM(acc_shape, F32)],
                          compiler_params=_cp("arbitrary"))(*args)


def ff_out_bwd_w(name, act3, dff, tk):
    M = dff.shape[0]

    def step(a_ref, d_ref, acc):
        for s in range(NS):
            acc[s * WS:(s + 1) * WS, :] += _dot(a_ref[s], d_ref[...], 0, 0)

    return _token_sum(name, M // tk, [pl.BlockSpec((NS, tk, WS), lambda k: (0, k, 0)), pl.BlockSpec((tk, D), lambda k: (k, 0))],
                      (NS * WS, D), (NS * WS, D), step, (act3, dff))


HALO = 8
CONV_ROWS = 128


def _fill(pad_ref, n, v=None):
    edge = jnp.zeros((HALO, 128), F32)
    pad_ref[0:HALO, :] = edge
    pad_ref[HALO + n:2 * HALO + n, :] = edge
    if v is not None:
        pad_ref[HALO:HALO + n, :] = v


def _tap(pad_ref, r0, k, rows=CONV_ROWS):
    return pad_ref[HALO + r0 + k - 2:HALO + r0 + k - 2 + rows, :]


def _conv_pre(pad_ref, r0, w_ref, b_ref):
    acc = _tap(pad_ref, r0, 0) * w_ref[0:1, :] + b_ref[...]
    for k in range(1, 5):
        acc = acc + _tap(pad_ref, r0, k) * w_ref[k:k + 1, :]
    return acc


def conv_fwd(name, proj, conv_w, conv_b, R):
    segs = ((0, LC), (LC, R))

    def body(x_ref, w_ref, b_ref, o_ref, xp):
        for (s, e) in segs:
            _fill(xp, e - s, x_ref[s:e, :])
            for r0 in range(0, e - s, CONV_ROWS):
                o_ref[s + r0:s + r0 + CONV_ROWS, :] = _silu(_conv_pre(xp, r0, w_ref, b_ref))

    return pl.pallas_call(
        body, name=name, grid=(12,),
        in_specs=[pl.BlockSpec((R, 128), lambda j: (0, C_XBC // 128 + j)),
                  pl.BlockSpec((8, 128), lambda j: (0, j)), pl.BlockSpec((1, 128), lambda j: (0, j))],
        out_specs=pl.BlockSpec((R, 128), lambda j: (0, j)),
        out_shape=jax.ShapeDtypeStruct((R, 1536), F32), scratch_shapes=[pltpu.VMEM((R - LC + 2 * HALO, 128), F32)],
        compiler_params=_cp("arbitrary"))(proj, conv_w, conv_b)


def conv_bwd(name, proj, conv_w, conv_b, d_f, d_b, d_skip, dproj, R):
    segs = ((0, LC), (LC, R))

    def body(x_ref, w_ref, b_ref, df_ref, db_ref, ds_ref, _, dx_ref, dw_ref, dbias_ref, xp, dp):
        j = pl.program_id(0)
        has_skip = (j < 8).astype(F32)
        dw = [jnp.zeros((8, 128), F32) for _ in range(5)]
        dbias = jnp.zeros((8, 128), F32)
        fold = lambda v: jnp.sum(v.reshape(CONV_ROWS // 8, 8, 128), axis=0)
        for (s, e) in segs:
            n = e - s
            _fill(xp, n, x_ref[s:e, :])
            _fill(dp, n)
            for r0 in range(0, n, CONV_ROWS):
                rows = slice(s + r0, s + r0 + CONV_ROWS)
                pre = _conv_pre(xp, r0, w_ref, b_ref)
                sig = jax.nn.sigmoid(pre)
                dy = df_ref[rows, :] + db_ref[rows, :]
                if s == LC:
                    dy = dy + ds_ref[r0:r0 + CONV_ROWS, :] * has_skip
                dpre = dy * (sig * (1.0 + pre * (1.0 - sig)))
                dp[HALO + r0:HALO + r0 + CONV_ROWS, :] = dpre
                dbias = dbias + fold(dpre)
            for r0 in range(0, n, CONV_ROWS):
                x = x_ref[s + r0:s + r0 + CONV_ROWS, :]
                dx = jnp.zeros_like(x)
                for k in range(5):
                    d = _tap(dp, r0, 4 - k)
                    dx = dx + d * w_ref[k:k + 1, :]
                    dw[k] = dw[k] + fold(d * x)
                dx_ref[s + r0:s + r0 + CONV_ROWS, :] = dx.astype(dx_ref.dtype)
        dw_ref[...] = jnp.zeros_like(dw_ref)
        for k in range(5):
            dw_ref[k:k + 1, :] = jnp.sum(dw[k], axis=0, keepdims=True)
        dbias_ref[...] = jnp.sum(dbias, axis=0, keepdims=True)

    pad = pltpu.VMEM((R - LC + 2 * HALO, 128), F32)
    return pl.pallas_call(
        body, name=name, grid=(12,),
        in_specs=[pl.BlockSpec((R, 128), lambda j: (0, C_XBC // 128 + j)),
                  pl.BlockSpec((8, 128), lambda j: (0, j)), pl.BlockSpec((1, 128), lambda j: (0, j)),
                  pl.BlockSpec((R, 128), lambda j: (0, j)), pl.BlockSpec((R, 128), lambda j: (0, j)),
                  pl.BlockSpec((R - LC, 128), lambda j: (0, jnp.minimum(j, 7))),
                  pl.BlockSpec(memory_space=pl.ANY)],
        out_specs=[pl.BlockSpec((R, 128), lambda j: (0, C_XBC // 128 + j)),
                   pl.BlockSpec((8, 128), lambda j: (0, j)), pl.BlockSpec((1, 128), lambda j: (0, j))],
        out_shape=[jax.ShapeDtypeStruct(dproj.shape, dproj.dtype), jax.ShapeDtypeStruct((8, 1536), F32),
                   jax.ShapeDtypeStruct((1, 1536), F32)], scratch_shapes=[pad, pad],
        input_output_aliases={6: 0}, compiler_params=_cp("arbitrary"))(proj, conv_w, conv_b, d_f, d_b, d_skip, dproj)


def _ssd_chunk(rev, dirn):
    cums = _cumsum_fn(rev)

    def f(xs, Bs, Cs, dt, alog, Hs):
        lane = lax.broadcasted_iota(jnp.int32, (1, 128), 1)
        sub = lax.broadcasted_iota(jnp.int32, (Q, 1), 0)
        r = lax.broadcasted_iota(jnp.int32, (Q, Q), 0)
        c = lax.broadcasted_iota(jnp.int32, (Q, Q), 1)
        mask = (r <= c) if rev else (r >= c)
        left = lane < 64
        a = dt * (-jnp.exp(alog))
        s = cums(a)
        sT, dtT = s.T, dt.T
        last_row = (sub == (0 if rev else Q - 1)).astype(F32)
        s_last = jnp.sum(s * last_row, axis=0, keepdims=True)
        G = [mm_nt(Cs[g], Bs[g]) for g in range(2)]
        M, es, wc, ed = [], [], [], []
        for h in range(NH):
            l = 16 * dirn + h
            oh_l = (lane == l).astype(F32)
            oh_s = (sub == l).astype(F32)
            s_col = jnp.sum(s * oh_l, axis=1, keepdims=True)
            dt_col = jnp.sum(dt * oh_l, axis=1, keepdims=True)
            s_row = jnp.sum(sT * oh_s, axis=0, keepdims=True)
            dt_row = jnp.sum(dtT * oh_s, axis=0, keepdims=True)
            sl = jnp.sum(s_last * oh_l, axis=1, keepdims=True)
            seg = jnp.where(mask, s_col - s_row, 0.0)
            lm = jnp.where(mask, jnp.exp(seg), 0.0)
            M.append(G[h // 8] * lm * dt_row)
            es.append(jnp.exp(s_col))
            wc.append(jnp.exp(sl - s_col) * dt_col)
            ed.append(jnp.exp(sl))
        Ys, Hn = [], []
        for j in range(8):
            g = j // 4
            xa = jnp.where(left, xs[j], 0.0)
            xb = jnp.where(left, 0.0, xs[j])
            yd = mm(M[2 * j], xa) + mm(M[2 * j + 1], xb)
            yo = mm(Cs[g], Hs[j]) * jnp.where(left, es[2 * j], es[2 * j + 1])
            Ys.append(yd + yo)
            st = mm_tn(Bs[g], xs[j] * jnp.where(left, wc[2 * j], wc[2 * j + 1]))
            Hn.append(Hs[j] * jnp.where(left, ed[2 * j], ed[2 * j + 1]) + st)
        return Ys, Hn

    return f


def _chunk_of(t, n, rev):
    if not rev:
        return t
    return jnp.where(t < 2, 1 - t, n + 1 - t)


def _cols128(ref, k, lead=()):
    return [ref[lead + (slice(None), slice(128 * j, 128 * (j + 1)))] for j in range(k)]


def ssd_fwd(name, xbc, dts, alog, n, dirs):
    nd = len(dirs)
    chunks = [_ssd_chunk(rev, dirn) for rev, dirn in dirs]

    def body(*refs):
        al_ref = refs[4 * nd]
        for d in range(nd):
            x_ref, b_ref, c_ref, dt_ref = refs[4 * d:4 * d + 4]
            y_ref, hs_ref = refs[4 * nd + 1 + 2 * d:4 * nd + 3 + 2 * d]
            h_scr = refs[4 * nd + 1 + 2 * nd + d]

            @pl.when(pl.program_id(0) == 0)
            def _(h_scr=h_scr):
                h_scr[...] = jnp.zeros_like(h_scr)

            hs_ref[0] = h_scr[...]
            Ys, Hn = chunks[d](_cols128(x_ref, 8), _cols128(b_ref, 2), _cols128(c_ref, 2), dt_ref[...], al_ref[...],
                               _cols128(h_scr, 8))
            for j in range(8):
                y_ref[:, 128 * j:128 * (j + 1)] = Ys[j]
                h_scr[:, 128 * j:128 * (j + 1)] = Hn[j]

    in_specs, out_specs, out_shape, args = [], [], [], []
    for (rev, _), dt in zip(dirs, dts):
        cm = lambda t, rev=rev: _chunk_of(t, n, rev)
        in_specs += [pl.BlockSpec((Q, 1024), lambda t, cm=cm: (cm(t), 0)), pl.BlockSpec((Q, 256), lambda t, cm=cm: (cm(t), 4)),
                     pl.BlockSpec((Q, 256), lambda t, cm=cm: (cm(t), 5)), pl.BlockSpec((Q, 128), lambda t, cm=cm: (cm(t), 0))]
        args += [xbc, xbc, xbc, dt]
        out_specs += [pl.BlockSpec((Q, 1024), lambda t, cm=cm: (cm(t), 0)),
                      pl.BlockSpec((1, Q, 1024), lambda t, cm=cm: (cm(t), 0, 0))]
        out_shape += [jax.ShapeDtypeStruct((n * Q, 1024), F32), jax.ShapeDtypeStruct((n, Q, 1024), F32)]
    res = pl.pallas_call(
        body, name=name, grid=(n,), in_specs=in_specs + [pl.BlockSpec((1, 128), lambda t: (0, 0))],
        out_specs=out_specs, out_shape=out_shape, scratch_shapes=[pltpu.VMEM((Q, 1024), F32)] * nd,
        compiler_params=_cp("arbitrary"))(*args, alog)
    return [res[2 * d:2 * d + 2] for d in range(nd)]


def ssd_bwd(name, xbc, dts, alog, hss, dy, n, dirs):
    nd = len(dirs)
    chunks = [_ssd_chunk(rev, dirn) for rev, dirn in dirs]

    def body(*refs):
        tt = pl.program_id(0)
        al_ref = refs[6 * nd]
        for d, (rev, _) in enumerate(dirs):
            x_ref, b_ref, c_ref, dt_ref, hs_ref, dy_ref = refs[6 * d:6 * d + 6]
            dx_ref, ddt_ref, dal_ref = refs[6 * nd + 1 + 3 * d:6 * nd + 4 + 3 * d]
            dh_scr = refs[6 * nd + 1 + 3 * nd + d]
            ch = _chunk_of(n - 1 - tt, n, rev)

            @pl.when(tt == 0)
            def _(dh_scr=dh_scr):
                dh_scr[...] = jnp.zeros_like(dh_scr)

            live = (ch >= 2).astype(F32)
            dYs = [v * live for v in _cols128(dy_ref, 8)]
            _, vjp_fn = jax.vjp(chunks[d], _cols128(x_ref, 8), _cols128(b_ref, 2), _cols128(c_ref, 2), dt_ref[...],
                                al_ref[...], _cols128(hs_ref, 8, (0,)))
            dxs, dBs, dCs, ddt, dal, dHs = vjp_fn((dYs, _cols128(dh_scr, 8)))
            for j in range(8):
                dx_ref[:, 128 * j:128 * (j + 1)] = dxs[j]
                dh_scr[:, 128 * j:128 * (j + 1)] = dHs[j]
            for g in range(2):
                dx_ref[:, 1024 + 128 * g:1024 + 128 * (g + 1)] = dBs[g]
                dx_ref[:, 1280 + 128 * g:1280 + 128 * (g + 1)] = dCs[g]
            ddt_ref[...] = ddt

            @pl.when(tt == 0)
            def _(dal_ref=dal_ref, dal=dal):
                dal_ref[...] = dal

            @pl.when(tt > 0)
            def _(dal_ref=dal_ref, dal=dal):
                dal_ref[...] += dal

    in_specs, out_specs, out_shape, args = [], [], [], []
    for (rev, _), dt, hs in zip(dirs, dts, hss):
        cm = lambda t, rev=rev: _chunk_of(n - 1 - t, n, rev)
        in_specs += [pl.BlockSpec((Q, 1024), lambda t, cm=cm: (cm(t), 0)), pl.BlockSpec((Q, 256), lambda t, cm=cm: (cm(t), 4)),
                     pl.BlockSpec((Q, 256), lambda t, cm=cm: (cm(t), 5)), pl.BlockSpec((Q, 128), lambda t, cm=cm: (cm(t), 0)),
                     pl.BlockSpec((1, Q, 1024), lambda t, cm=cm: (cm(t), 0, 0)),
                     pl.BlockSpec((Q, 1024), lambda t, cm=cm: (jnp.maximum(cm(t) - 2, 0), 0))]
        args += [xbc, xbc, xbc, dt, hs, dy]
        out_specs += [pl.BlockSpec((Q, 1536), lambda t, cm=cm: (cm(t), 0)), pl.BlockSpec((Q, 128), lambda t, cm=cm: (cm(t), 0)),
                      pl.BlockSpec((1, 128), lambda t: (0, 0))]
        out_shape += [jax.ShapeDtypeStruct((n * Q, 1536), F32), jax.ShapeDtypeStruct((n * Q, 128), F32),
                      jax.ShapeDtypeStruct((1, 128), F32)]
    res = pl.pallas_call(
        body, name=name, grid=(n,), in_specs=in_specs + [pl.BlockSpec((1, 128), lambda t: (0, 0))],
        out_specs=out_specs, out_shape=out_shape, scratch_shapes=[pltpu.VMEM((Q, 1024), F32)] * nd,
        compiler_params=_cp("arbitrary"))(*args, alog)
    return [res[3 * d:3 * d + 3] for d in range(nd)]


def f_norm0(c, x, g0, b0, sc, sh, is_ctx):
    x0 = _ln(jnp.where(is_ctx > 0.5, c, x), g0, b0)
    return x0, x0 * (1.0 + sc) + sh


def f_dt(raw, bias):
    z = split4(raw)[0] + bias
    dt = jnp.maximum(z, 0.0) + jnp.log1p(jnp.exp(-jnp.abs(z)))
    return dt, dt


def f_gated_norm(yf, yb, xs, z, dcol, g):
    h = (yf + yb + xs * dcol) * _silu(z)
    return (h * lax.rsqrt(jnp.mean(h * h, axis=-1, keepdims=True) + LN_EPS) * g,)


def f_gmlp(uv, gmg, gmb, *wb):
    ws, bs = wb[:8], wb[8:]
    u, v = split2(uv)
    vn = split8(_ln(_gelu(v), gmg, gmb))
    mixed = concat8(tuple(mm(ws[g], vn[g]) + bs[g] for g in range(8)))
    return (_gelu(u) * mixed,)


def f_merge(ps, pg, gates, bg):
    gs, gg = split2(jax.nn.sigmoid(gates + bg))
    return (gs * ps + gg * pg,)


def f_project_merge(yn, y_gm, gates, w_ssd, w_gm, bg):
    ps, pg = mm(yn, w_ssd), mm(y_gm, w_gm)
    return f_merge(ps, pg, gates, bg) + (ps, pg)


def f_res1(x0, out, g1, lg, lb, sc, sh):
    x1 = _ln(ALPHA * x0 + g1 * out, lg, lb)
    return x1, x1 * (1.0 + sc) + sh


def f_res2_loss(x1, ff, tgt, g2, lg, lb):
    x2 = _ln(ALPHA * x1 + g2 * ff, lg, lb)
    e = x2 - tgt
    return (0.5 * jnp.sum(jnp.mean(e * e, axis=-1, keepdims=True), axis=0, keepdims=True),)


def _row_tile(M):
    return 544 if M % 544 == 0 else (512 if M % 512 == 0 else M)


def core(ctx, x, tgt, mod_x, mod_c, X, S):
    L = x.shape[0]
    R = LC + L
    n = R // Q
    T = 256
    nt, ntl = R // T, L // T
    tmR, tmL = _row_tile(R), _row_tile(L)
    tmS = 1024 if L % 1024 == 0 else tmL
    tkR = 256 if R % 512 else 512
    tkL = 512 if L % 512 == 0 else 256
    row = lambda v: v.reshape(1, -1)
    mx = [row(mod_x[k]) for k in range(6)]
    mc = [row(mod_c[k]) for k in range(6)]
    sel = lambda i: jnp.minimum(i, 1)
    sc1 = jnp.stack([mc[1], mx[1]])
    sh1 = jnp.stack([mc[0], mx[0]])
    ln0 = [(row(S['ln0_g']), None), (row(S['ln0_b']), None), (sc1, sel), (sh1, sel),
           (jnp.array([1.0, 0.0], F32).reshape(2, 1, 1), sel)]
    x_n0 = [(ctx, D, 0, -nt), (x, D, 0, -1)]

    x0, xm, *landed = stage_fwd("norm0_fwd", f_norm0, T, nt, x_n0, ln0, [('new', R, D, 0), ('new', R, D, 0, BF16)],
                                gather=X.w_in_block())
    w_in = X.w_in(*landed)
    proj = matmul("proj_fwd", xm, w_in, 'nn', tmR, PW // 2, 1024)
    conv_w8 = jnp.pad(S['conv_w'], ((0, 3), (0, 0)))
    conv_b = row(S['conv_b'])
    xbc = conv_fwd("conv_fwd", proj, conv_w8, conv_b, R)
    dt_bias = jnp.pad(S['dt_bias'].reshape(1, 32), ((0, 0), (0, 96)))
    alog = jnp.pad(S['a_log'].reshape(1, 32), ((0, 0), (0, 96)))
    x_dt = [(proj, 512, C_DT // 512, 0)]
    dt_f, dt_b = stage_fwd("dt_fwd", f_dt, T, nt, x_dt, [(dt_bias, None)], [('new', R, 128, 0), ('new', R, 128, 0)])
    directions = [(False, 0), (True, 1)]
    (y_f, hs_f), (y_b, hs_b) = ssd_fwd("ssd_fwd", xbc, [dt_f, dt_b], alog, n, directions)
    W = X.rest()
    dcol = jnp.repeat(S['d_skip'][0] + S['d_skip'][1], 64).reshape(1, D)
    x_gn = [(y_f, D, 0, 1), (y_b, D, 0, 1), (xbc, D, 0, 1), (proj, D, C_Z // D, 1)]
    p_gn = [(dcol, None), (row(S['ssd_norm_g']), None)]
    (yn,) = stage_fwd("gnorm_fwd", f_gated_norm, T, ntl, x_gn, p_gn, [('new', L, D, 0, BF16)])
    x_gm = [(proj, 2 * D, C_UV // (2 * D), LC // Q)]
    p_gm = ([(row(S['gm_norm_g']), None), (row(S['gm_norm_b']), None)]
            + [(S['w_spatial'][g], None) for g in range(8)] + [(S['b_spatial'][g].reshape(Q, 1), None) for g in range(8)])
    (y_gm,) = stage_fwd("gmlp_fwd", f_gmlp, Q, L // Q, x_gm, p_gm, [('new', L, D, 0, BF16)])
    x_pm = [(yn, D, 0, 0), (y_gm, D, 0, 0), (proj, 2 * D, C_GATE // (2 * D), 1)]
    p_mg = [(row(S['b_gate']), None)]
    p_pm = [(W['w_ssd_proj'], None), (W['w_gm_proj'], None)] + p_mg
    merged, p_ssd, p_g = stage_fwd("merge_fwd", f_project_merge, T, ntl, x_pm, p_pm,
                                   [('new', L, D, 0, BF16), ('new', L, D, 0), ('new', L, D, 0)])
    out = matmul("out_fwd", merged, W['w_out'], 'nn', tmS, 1024, 1024)
    x_r1 = [(x0, D, 0, 1), (out, D, 0, 0)]
    p_r1 = [(mx[2], None), (row(S['ln1_g']), None), (row(S['ln1_b']), None), (mx[4], None), (mx[3], None)]
    x1, hm = stage_fwd("res1_fwd", f_res1, T, ntl, x_r1, p_r1, [('new', L, D, 0), ('new', L, D, 0, BF16)])
    a1, a3, act = ffn_in_fwd("ffn_in_fwd", hm, W['w_ff1'], W['w_ff3'], tmL)
    ff = ff_out_fwd("ff2_fwd", act, W['w_ff2'], tmL)
    x_r2 = [(x1, D, 0, 0), (ff, D, 0, 0), (tgt, D, 0, 0)]
    p_r2 = [(mx[5], None), (row(S['ln2_g']), None), (row(S['ln2_b']), None)]

    dx1_a, dff, dg2, dl2g, dl2b, loss = stage_bwd(
        "res2_bwd", f_res2_loss, T, ntl, x_r2, p_r2, [1.0],
        [('new', L, D, 0), ('new', L, D, 0, BF16), None], [True, True, True], primal=[(0, (1, 1))])
    da1, da3 = ffn_out_bwd_x("ffn_out_bwd_x", dff, W['w_ff2'], a1, a3, tmL)
    gw_ff2 = ff_out_bwd_w("ff2_bwd_w", act, dff, tkL)
    dhm = ff_in_bwd_x("ffn_in_bwd_x", da1, da3, W['w_ff1'], W['w_ff3'], tmL)
    gw_ff1 = ff_in_bwd_w("ff1_bwd_w", hm, da1, tkL)
    gw_ff3 = ff_in_bwd_w("ff3_bwd_w", hm, da3, tkL)
    X.grads('ffn', {'w_ff2': gw_ff2, 'w_ff1': gw_ff1, 'w_ff3': gw_ff3})
    dx0_a, dout, dg1, dl1g, dl1b, dsc2, dsh2 = stage_bwd(
        "res1_bwd", f_res1, T, ntl, x_r1, p_r1, [(dx1_a, D, 0, 0), (dhm, D, 0, 0)],
        [('new', L, D, 0), ('new', L, D, 0, BF16)], [True] * 5)
    dmerged = matmul("out_bwd_x", dout, W['w_out'], 'nt', tmS, 1024, 1024)
    gw_out = matmul("out_bwd_w", merged, dout, 'tn', 1024, 1024, tkL, BF16)
    lt, lq = -(LC // T), -(LC // Q)
    x_mg_b = [(p_ssd, D, 0, lt), (p_g, D, 0, lt), (proj, 2 * D, C_GATE // (2 * D), 0)]
    dp_ssd, dp_g, dproj, dbg = stage_bwd(
        "merge_bwd", f_merge, T, nt, x_mg_b, p_mg, [(dmerged, D, 0, lt)],
        [('new', L, D, lt, BF16), ('new', L, D, lt, BF16), ('part', R, PW, 2 * D, C_GATE // (2 * D), 0, BF16)], [True])
    dyn = matmul("pssd_bwd_x", dp_ssd, W['w_ssd_proj'], 'nt', tmS, 1024, 1024)
    gw_ssd = matmul("pssd_bwd_w", yn, dp_ssd, 'tn', 1024, 1024, tkL, BF16)
    dy_gm = matmul("pgm_bwd_x", dp_g, W['w_gm_proj'], 'nt', tmS, 1024, 1024)
    gw_gm = matmul("pgm_bwd_w", y_gm, dp_g, 'tn', 1024, 1024, tkL, BF16)
    X.grads('proj', {'w_out': gw_out, 'w_ssd_proj': gw_ssd, 'w_gm_proj': gw_gm})
    r_gm = stage_bwd("gmlp_bwd", f_gmlp, Q, n, [(proj, 2 * D, C_UV // (2 * D), 0)], p_gm, [(dy_gm, D, 0, lq)],
                     [('alias', dproj, 2 * D, C_UV // (2 * D), 0)], [True] * 18)
    dproj, dgmg, dgmb, dws, dbs = r_gm[0], r_gm[1], r_gm[2], r_gm[3:11], r_gm[11:19]
    x_gn_b = [(y_f, D, 0, 0), (y_b, D, 0, 0), (xbc, D, 0, 0), (proj, D, C_Z // D, 0)]
    dy, dskipx, dproj, ddcol, dng = stage_bwd(
        "gnorm_bwd", f_gated_norm, T, nt, x_gn_b, p_gn, [(dyn, D, 0, lt)],
        [('new', L, D, lt), None, ('new', L, D, lt), ('alias', dproj, D, C_Z // D, 0)], [True, True], rows=32)
    (dxbc_f, ddt_f, dal_f), (dxbc_b, ddt_b, dal_b) = ssd_bwd("ssd_bwd", xbc, [dt_f, dt_b], alog, [hs_f, hs_b], dy, n,
                                                             directions)
    dproj, ddtb = stage_bwd("dt_bwd", f_dt, T, nt, x_dt, [(dt_bias, None)],
                            [(ddt_f, 128, 0, 0), (ddt_b, 128, 0, 0)],
                            [('alias', dproj, 512, C_DT // 512, 0)], [True])
    dproj, dcw8, dcb = conv_bwd("conv_bwd", proj, conv_w8, conv_b, dxbc_f, dxbc_b, dskipx, dproj, R)
    gw_in = matmul("proj_bwd_w", xm, dproj, 'tn', 1024, PW // 2, tkR, BF16)
    X.grads('in', {'w_in': gw_in})
    dxm = matmul("proj_bwd_x", dproj, w_in, 'nt', R // 2 if R % 16 == 0 else R, 1024, 1024)
    grad_x, dl0g, dl0b, dsc1, dsh1 = stage_bwd(
        "norm0_bwd", f_norm0, T, nt, x_n0, ln0, [(dx0_a, D, 0, -1), (dxm, D, 0, 0)],
        [None, ('new', L, D, -1)], [True] * 4 + [False])

    zero = jnp.zeros((D,), F32)
    flat = lambda v: v.reshape(-1)
    small = {
        'loss': flat(loss), 'ln0_g': flat(dl0g), 'ln0_b': flat(dl0b),
        'dmod_x': jnp.concatenate([flat(dsh1[1]), flat(dsc1[1]), flat(dg1), flat(dsh2), flat(dsc2), flat(dg2)]),
        'dmod_c': jnp.concatenate([flat(dsh1[0]), flat(dsc1[0]), zero, zero, zero, zero]),
        'conv_w': flat(dcw8[:5]), 'conv_b': flat(dcb), 'dt_bias': flat(ddtb[:, :32]),
        'a_log': flat((dal_f + dal_b)[:, :32]),
        'd_skip': flat(jnp.tile(ddcol.reshape(1, NH, 64).sum(-1), (2, 1))),
        'ssd_norm_g': flat(dng), 'gm_norm_g': flat(dgmg), 'gm_norm_b': flat(dgmb),
        'w_spatial': flat(jnp.stack(dws)), 'b_spatial': flat(jnp.stack(dbs)), 'b_gate': flat(dbg),
        'ln1_g': flat(dl1g), 'ln1_b': flat(dl1b), 'ln2_g': flat(dl2g), 'ln2_b': flat(dl2b),
    }
    return grad_x, small


def _place():
    return lax.axis_index("x"), lax.axis_index("y"), lax.axis_index("c")


def allgather8(name, blk, hbm):
    space = pl.ANY if hbm else pltpu.VMEM

    def body(x_ref, out_ref, send_sems, recv_sems, local_sem):
        x, y, c = _place()
        me, sibling = (x, y, c), (x, y, 1 - c)
        chips = [(1 - x, y), (x, 1 - y), (1 - x, 1 - y)]

        def slot(px, py, pc):
            return out_ref.at[4 * px + 2 * py + pc]

        def copy(k, block, to, src=None):
            return pltpu.make_async_remote_copy(
                src_ref=slot(*block) if src is None else src, dst_ref=slot(*block),
                send_sem=send_sems.at[k], recv_sem=recv_sems.at[k], device_id=to, device_id_type=MESH)

        mine = pltpu.make_async_copy(x_ref, slot(*me), local_sem)
        mine.start()
        first = [copy(0, me, sibling, src=x_ref)]
        first += [copy(1 + j, me, (*chip, c), src=x_ref) for j, chip in enumerate(chips)]
        for cp in first:
            cp.start()
        passed = [copy(4 + j, (*chip, c), sibling) for j, chip in enumerate(chips)]
        for j, chip in enumerate(chips):
            copy(1 + j, (*chip, c), me).wait_recv()
            passed[j].start()
        copy(0, sibling, me).wait_recv()
        for j, chip in enumerate(chips):
            copy(4 + j, (*chip, 1 - c), me).wait_recv()
        for cp in first + passed:
            cp.wait_send()
        mine.wait()

    return pl.pallas_call(
        body, name=name, out_shape=jax.ShapeDtypeStruct((8,) + blk.shape, blk.dtype),
        in_specs=[pl.BlockSpec(memory_space=space)], out_specs=pl.BlockSpec(memory_space=space),
        scratch_shapes=[pltpu.SemaphoreType.DMA((7,)), pltpu.SemaphoreType.DMA((7,)), pltpu.SemaphoreType.DMA],
        compiler_params=pltpu.CompilerParams(vmem_limit_bytes=VMEM_LIMIT_V7X))(blk)


def _peers(place):
    x, y, c = place
    return [((1 - x) if k & 4 else x, (1 - y) if k & 2 else y, (1 - c) if k & 1 else c) for k in range(1, 8)]


def _slot(p):
    return 4 * p[0] + 2 * p[1] + p[2]


def plan_gather(place, srcs, lands):
    remote = [(s, l.at[_slot(place)], to) for s, l in zip(srcs, lands) for to in _peers(place)]
    return remote, [(s, l.at[_slot(place)]) for s, l in zip(srcs, lands)]


def plan_to_owner(place, srcs, lands):
    remote = [(s.at[2 * to[0] + to[1], to[2]], l.at[_slot(place)], to) for s, l in zip(srcs, lands) for to in _peers(place)]
    x, y, c = place
    return remote, [(s.at[2 * x + y, c], l.at[_slot(place)]) for s, l in zip(srcs, lands)]


def sequencer_exchange(name, collective_id, srcs, land_shapes, plan):
    n = len(srcs)
    src_refs = [jax.new_ref(a, memory_space=pltpu.MemorySpace.HBM) for a in srcs]
    land_refs = [jax.empty_ref(s, memory_space=pltpu.MemorySpace.HBM) for s in land_shapes]

    @pl.kernel(mesh=plsc.ScalarSubcoreMesh(axis_name="sequencer", num_cores=1), name=name,
               scratch_types=(pltpu.SemaphoreType.DMA((7 * n,)), pltpu.SemaphoreType.DMA((7 * n,)),
                              pltpu.SemaphoreType.DMA((n,))),
               compiler_params=pltpu.CompilerParams(collective_id=collective_id))
    def launch(send_sems, recv_sems, local_sems):
        place = _place()
        barrier = pltpu.get_barrier_semaphore()
        for to in _peers(place):
            pl.semaphore_signal(barrier, inc=1, device_id=to, device_id_type=MESH)
        pl.semaphore_wait(barrier, 7)
        remote, local = plan(place, src_refs, land_refs)
        mine = [pltpu.make_async_copy(s, d, local_sems.at[a]) for a, (s, d) in enumerate(local)]
        for cp in mine:
            cp.start()
        cps = [pltpu.make_async_remote_copy(src_ref=s, dst_ref=d, send_sem=send_sems.at[k], recv_sem=recv_sems.at[k],
                                            device_id=to, device_id_type=MESH) for k, (s, d, to) in enumerate(remote)]
        for cp in cps:
            cp.start()
        for cp in mine:
            cp.wait()
        for cp in cps:
            cp.wait()

    launch()
    return land_refs


def sibling_pair(name, hs):
    n = len(hs)

    def body(*refs):
        ins, outs = refs[:n], refs[n:2 * n]
        send_sems, recv_sems = refs[2 * n:]
        x, y, c = _place()
        cps = [pltpu.make_async_remote_copy(src_ref=outs[a].at[c], dst_ref=outs[a].at[c], send_sem=send_sems.at[a],
                                            recv_sem=recv_sems.at[a], device_id=(x, y, 1 - c), device_id_type=MESH)
               for a in range(n)]
        for cp in cps:
            cp.start()
        for a in range(n):
            pltpu.make_async_remote_copy(src_ref=outs[a].at[1 - c], dst_ref=outs[a].at[1 - c], send_sem=send_sems.at[a],
                                         recv_sem=recv_sems.at[a], device_id=(x, y, 1 - c),
                                         device_id_type=MESH).wait_recv()
        for cp in cps:
            cp.wait_send()

    any_spec = pl.BlockSpec(memory_space=pl.ANY)
    return pl.pallas_call(
        body, name=name, out_shape=[jax.ShapeDtypeStruct(h.shape, h.dtype) for h in hs],
        in_specs=[any_spec] * n, out_specs=[any_spec] * n, input_output_aliases={a: a for a in range(n)},
        scratch_shapes=[pltpu.SemaphoreType.DMA((n,)), pltpu.SemaphoreType.DMA((n,))])(*hs)


def owner_sum(name, land):
    _, r, w = land.shape
    T = r // 2

    def body(_, l_ref, o_ref):
        acc = l_ref[0].astype(F32)
        for j in range(1, 8):
            acc = acc + l_ref[j].astype(F32)
        o_ref[...] = acc

    grid_spec = pltpu.PrefetchScalarGridSpec(
        num_scalar_prefetch=1, grid=(2,),
        in_specs=[pl.BlockSpec((8, T, w), lambda i, at: (0, i, 0))],
        out_specs=pl.BlockSpec((None, T, w), lambda i, at: (at[0], i, 0)))
    at = jnp.stack([lax.axis_index("c")]).astype(jnp.int32)
    return pl.pallas_call(body, name=name, grid_spec=grid_spec, out_shape=jax.ShapeDtypeStruct((2, r, w), F32),
                          compiler_params=_cp("arbitrary"))(at, land)


W_IN_RUNS = ((0, 2, 1296, 376), (376, 3, 0, 1672), (2048, 1, 920, 752), (2800, 2, 0, 1296), (4096, 0, 0, 1024),
             (5120, 0, 1024, 648), (5768, 1, 0, 920))


def w_in_to_padded(name, g4):
    T = 128

    def body(g_ref, o_ref):
        o_ref[:, D_PROJ:PW] = jnp.zeros((T, PW - D_PROJ), o_ref.dtype)
        for (a, s, j0, w) in W_IN_RUNS:
            o_ref[:, a:a + w] = g_ref[s, :, j0:j0 + w]

    return pl.pallas_call(body, name=name, grid=(D // T,), in_specs=[pl.BlockSpec((4, T, 1672), lambda i: (0, i, 0))],
                          out_specs=pl.BlockSpec((T, PW), lambda i: (i, 0)),
                          out_shape=jax.ShapeDtypeStruct((D, PW), g4.dtype), compiler_params=_cp("arbitrary"))(g4)


def w_in_from_padded(name, gp):
    T = 128

    def body(g_ref, o_ref):
        for (a, s, j0, w) in W_IN_RUNS:
            o_ref[s, :, j0:j0 + w] = g_ref[:, a:a + w]

    return pl.pallas_call(body, name=name, grid=(D // T,), in_specs=[pl.BlockSpec((T, PW), lambda i: (i, 0))],
                          out_specs=pl.BlockSpec((4, T, 1672), lambda i: (0, i, 0)),
                          out_shape=jax.ShapeDtypeStruct((4, D, 1672), gp.dtype), compiler_params=_cp("arbitrary"))(gp)


def sum_devices(name, g):
    def body(g_ref, o_ref):
        acc = g_ref[0]
        for k in range(1, 8):
            acc = acc + g_ref[k]
        o_ref[...] = acc

    return pl.pallas_call(body, name=name, out_shape=jax.ShapeDtypeStruct(g.shape[1:], F32),
                          compiler_params=pltpu.CompilerParams(vmem_limit_bytes=VMEM_LIMIT_V7X))(g)


def adamw(name, w, g, m, v, T):
    r, wd = w.shape
    c1 = 1.0 - ADAM_B1 ** ADAM_STEP
    c2 = 1.0 - ADAM_B2 ** ADAM_STEP

    def body(w_ref, g_ref, m_ref, v_ref, d_ref, mo_ref, vo_ref):
        gv = g_ref[...]
        mn = ADAM_B1 * m_ref[...] + (1.0 - ADAM_B1) * gv
        vn = ADAM_B2 * v_ref[...] + (1.0 - ADAM_B2) * (gv * gv)
        d_ref[...] = -ADAM_LR * ((mn / c1) / (jnp.sqrt(vn / c2) + ADAM_EPS) + ADAM_WD * w_ref[...])
        mo_ref[...] = mn
        vo_ref[...] = vn

    spec = pl.BlockSpec((T, wd), lambda i: (i, 0))
    return pl.pallas_call(body, name=name, grid=(r // T,), in_specs=[spec] * 4, out_specs=[spec] * 3,
                          out_shape=[jax.ShapeDtypeStruct((r, wd), F32)] * 3, compiler_params=_cp("arbitrary"))(w, g, m, v)


BIG = {'w_in': (1024, 1672), 'w_ssd_proj': (256, 1024), 'w_gm_proj': (256, 1024), 'w_out': (256, 1024),
       'w_ff1': (1024, 704), 'w_ff3': (1024, 704), 'w_ff2': (704, 1024)}


class Flat:
    def __init__(self, segs):
        self.off, o = {}, 0
        for name, size in segs:
            self.off[name] = (o, size)
            o += -(-size // 128) * 128
        self.rows = -(-o // 1024) * 8

    def pack(self, vals):
        parts = []
        for name, (o, size) in self.off.items():
            v = vals[name].reshape(-1).astype(F32)
            parts.append(jnp.pad(v, (0, -(-size // 128) * 128 - size)))
        buf = jnp.concatenate(parts)
        return jnp.pad(buf, (0, self.rows * 128 - buf.shape[0])).reshape(self.rows, 128)

    def get(self, buf, name, shape=None):
        o, size = self.off[name]
        v = buf[o // 128:(o + size + 127) // 128].reshape(-1)[:size]
        return v if shape is None else v.reshape(shape)


PARTIALS = Flat([('loss', 1), ('ln0_g', D), ('ln0_b', D), ('dmod_x', 6 * D), ('dmod_c', 6 * D), ('conv_w', 5 * 1536),
                 ('conv_b', 1536), ('dt_bias', 32), ('a_log', 32), ('d_skip', 32), ('ssd_norm_g', D),
                 ('gm_norm_g', D), ('gm_norm_b', D), ('w_spatial', 8 * Q * Q), ('b_spatial', 8 * Q), ('b_gate', 2 * D),
                 ('ln1_g', D), ('ln1_b', D), ('ln2_g', D), ('ln2_b', D)])

WEIGHTS = ('c_ctx', 'ln0_g', 'ln0_b', 'w_ada', 'b_ada', 'w_in', 'conv_w', 'conv_b', 'dt_bias', 'a_log', 'd_skip',
           'ssd_norm_g', 'gm_norm_g', 'gm_norm_b', 'w_spatial', 'b_spatial', 'b_gate', 'w_ssd_proj', 'w_gm_proj',
           'w_out', 'ln1_g', 'ln1_b', 'w_ff1', 'w_ff3', 'w_ff2', 'ln2_g', 'ln2_b')
BIG_NAMES = tuple(BIG)
SMALL_NAMES = tuple(n for n in WEIGHTS if n not in BIG_NAMES and n != 'w_ada')


def kernel(x, c, ctx, c_ctx, ln0_g, ln0_b, w_ada, b_ada, w_in, conv_w, conv_b, dt_bias, a_log, d_skip, ssd_norm_g, gm_norm_g, gm_norm_b, w_spatial, b_spatial, b_gate, w_ssd_proj, w_gm_proj, w_out, ln1_g, ln1_b, w_ff1, w_ff3, w_ff2, ln2_g, ln2_b, loss_target, m_c_ctx, m_ln0_g, m_ln0_b, m_w_ada, m_b_ada, m_w_in, m_conv_w, m_conv_b, m_dt_bias, m_a_log, m_d_skip, m_ssd_norm_g, m_gm_norm_g, m_gm_norm_b, m_w_spatial, m_b_spatial, m_b_gate, m_w_ssd_proj, m_w_gm_proj, m_w_out, m_ln1_g, m_ln1_b, m_w_ff1, m_w_ff3, m_w_ff2, m_ln2_g, m_ln2_b, v_c_ctx, v_ln0_g, v_ln0_b, v_w_ada, v_b_ada, v_w_in, v_conv_w, v_conv_b, v_dt_bias, v_a_log, v_d_skip, v_ssd_norm_g, v_gm_norm_g, v_gm_norm_b, v_w_spatial, v_b_spatial, v_b_gate, v_w_ssd_proj, v_w_gm_proj, v_w_out, v_ln1_g, v_ln1_b, v_w_ff1, v_w_ff3, v_w_ff2, v_ln2_g, v_ln2_b):
    wts = dict(c_ctx=c_ctx, ln0_g=ln0_g, ln0_b=ln0_b, w_ada=w_ada, b_ada=b_ada, w_in=w_in, conv_w=conv_w, conv_b=conv_b,
               dt_bias=dt_bias, a_log=a_log, d_skip=d_skip, ssd_norm_g=ssd_norm_g, gm_norm_g=gm_norm_g,
               gm_norm_b=gm_norm_b, w_spatial=w_spatial, b_spatial=b_spatial, b_gate=b_gate, w_ssd_proj=w_ssd_proj,
               w_gm_proj=w_gm_proj, w_out=w_out, ln1_g=ln1_g, ln1_b=ln1_b, w_ff1=w_ff1, w_ff3=w_ff3, w_ff2=w_ff2,
               ln2_g=ln2_g, ln2_b=ln2_b)
    ms = dict(zip(WEIGHTS, (m_c_ctx, m_ln0_g, m_ln0_b, m_w_ada, m_b_ada, m_w_in, m_conv_w, m_conv_b, m_dt_bias, m_a_log,
                            m_d_skip, m_ssd_norm_g, m_gm_norm_g, m_gm_norm_b, m_w_spatial, m_b_spatial, m_b_gate,
                            m_w_ssd_proj, m_w_gm_proj, m_w_out, m_ln1_g, m_ln1_b, m_w_ff1, m_w_ff3, m_w_ff2, m_ln2_g,
                            m_ln2_b)))
    vs = dict(zip(WEIGHTS, (v_c_ctx, v_ln0_g, v_ln0_b, v_w_ada, v_b_ada, v_w_in, v_conv_w, v_conv_b, v_dt_bias, v_a_log,
                            v_d_skip, v_ssd_norm_g, v_gm_norm_g, v_gm_norm_b, v_w_spatial, v_b_spatial, v_b_gate,
                            v_w_ssd_proj, v_w_gm_proj, v_w_out, v_ln1_g, v_ln1_b, v_w_ff1, v_w_ff3, v_w_ff2, v_ln2_g,
                            v_ln2_b)))
    px, py, pc = _place()
    shard = 2 * px + py
    dev = 2 * shard + pc
    take = lambda a, i, axis=0: lax.dynamic_index_in_dim(a, i, axis, keepdims=False)

    half = lambda n: take(wts[n][0].reshape(2, BIG[n][0] // 2, BIG[n][1]), pc).astype(BF16)

    pre = jnp.concatenate([c, jnp.pad(conv_w[0], ((0, 0), (0, D - 384))), jnp.zeros((2, D), F32)], axis=0)
    pre = allgather8("gather_cond", pre, False)
    conv_w_full = pre[0::2, 1:6, :384].transpose(1, 0, 2).reshape(5, 1536)
    a16 = jnp.concatenate([_silu(pre[:, 0, :]), _silu(c_ctx)[None], jnp.zeros((7, D), F32)], axis=0)
    mod = matmul("ada_fwd", a16, w_ada[0], 'nn', 16, 512, 1024)
    mod = mod + lax.dynamic_slice_in_dim(b_ada[0], shard * 1536, 1536)[None]
    mod = allgather8("gather_mod", mod, False)
    mod = jnp.concatenate([mod[0], mod[2], mod[4], mod[6]], axis=1)
    mod_x = take(mod, dev).reshape(6, D)
    mod_c = mod[8].reshape(6, D)

    def full(n, blocks):
        r, w = BIG[n]
        return blocks.reshape(4, r, w) if w != D else blocks.reshape(4 * r, w)

    class Exchanges:
        rest_names = BIG_NAMES[1:]

        def __init__(self):
            self.pending = []

        def w_in_block(self):
            return half('w_in')

        def w_in(self, blocks):
            w = w_in_to_padded("w_in_layout", full('w_in', blocks))
            halves = [half(n) for n in self.rest_names]
            halves[0], _ = lax.optimization_barrier((halves[0], blocks))
            lands = [jax.ShapeDtypeStruct((8,) + h.shape, BF16) for h in halves]
            self.rest_refs = sequencer_exchange("gather_rest", 1, halves, lands, plan_gather)
            return w

        def rest(self):
            return {n: full(n, r[...]) for n, r in zip(self.rest_names, self.rest_refs)}

        def grads(self, group, gs):
            if group == 'in':
                gs = {'w_in': w_in_from_padded("w_in_grad_layout", gs['w_in'])}
            blocks = [g.reshape(4, 2, BIG[n][0] // 2, BIG[n][1]) for n, g in gs.items()]
            lands = [jax.ShapeDtypeStruct((8,) + b.shape[2:], BF16) for b in blocks]
            refs = sequencer_exchange("grads_" + group, 2 + len(self.pending), blocks, lands, plan_to_owner)
            self.pending.append((tuple(gs), refs))

        def finish(self):
            names, halves = [], []
            for ns, refs in self.pending:
                names += ns
                halves += [owner_sum("grads_sum_" + n, r[...]) for n, r in zip(ns, refs)]
            return {n: h.reshape(BIG[n]) for n, h in zip(names, sibling_pair("grads_halves", halves))}

    S = dict(ln0_g=ln0_g, ln0_b=ln0_b, conv_w=conv_w_full, conv_b=conv_b[0], dt_bias=dt_bias[0], a_log=a_log[0],
             d_skip=d_skip[0], ssd_norm_g=ssd_norm_g[0], gm_norm_g=gm_norm_g[0], gm_norm_b=gm_norm_b[0],
             w_spatial=w_spatial[0], b_spatial=b_spatial[0], b_gate=b_gate[0], ln1_g=ln1_g[0], ln1_b=ln1_b[0],
             ln2_g=ln2_g[0], ln2_b=ln2_b[0])
    exchanges = Exchanges()
    grad_x, gsmall = core(ctx[0], x[0], loss_target[0], mod_x, mod_c, exchanges, S)

    parts = allgather8("gather_partials", PARTIALS.pack(gsmall), False)
    tot = sum_devices("partials_sum", parts)
    g_shards = exchanges.finish()
    g = {n: PARTIALS.get(tot, n) for n in ('ln0_g', 'ln0_b', 'conv_b', 'dt_bias', 'a_log', 'd_skip', 'ssd_norm_g',
                                           'gm_norm_g', 'gm_norm_b', 'w_spatial', 'b_spatial', 'b_gate', 'ln1_g',
                                           'ln1_b', 'ln2_g', 'ln2_b')}
    loss = PARTIALS.get(tot, 'loss', ())
    dmod_c = PARTIALS.get(tot, 'dmod_c')
    g['b_ada'] = PARTIALS.get(tot, 'dmod_x') + dmod_c
    g['conv_w'] = lax.dynamic_slice_in_dim(PARTIALS.get(tot, 'conv_w', (5, 1536)), shard * 384, 384, axis=1)
    o, size = PARTIALS.off['dmod_x']
    dmod_rows = parts[:, o // 128:(o + size) // 128].reshape(8, size)
    dm = jnp.concatenate([dmod_rows, dmod_c[None], jnp.zeros((7, 6 * D), F32)], axis=0)
    dm = lax.dynamic_slice_in_dim(dm, shard * 1536, 1536, axis=1)
    g['w_ada'] = matmul("ada_bwd_w", a16, dm, 'tn', 1024, 512, 16)
    dm_c = jnp.concatenate([dm[8:9], jnp.zeros((15, 1536), F32)], axis=0)
    dc = matmul("ada_bwd_c", dm_c, w_ada[0], 'nt', 16, 1024, 512)
    dc = allgather8("gather_dcctx", dc, False)[:, 0, :]
    dc = ((dc[0] + dc[2]) + dc[4]) + dc[6]
    sg = jax.nn.sigmoid(c_ctx)
    g['c_ctx'] = dc * (sg * (1.0 + c_ctx * (1.0 - sg)))
    for n in BIG_NAMES:
        g[n] = g_shards[n]

    delta, new_m, new_v = {}, {}, {}
    for n in BIG_NAMES + ('w_ada',):
        r, w = wts[n].shape[1:]
        if w % 128:
            T = max(t for t in range(8, 257, 8) if w % t == 0)
            d_, m_, v_ = adamw("adamw_" + n, wts[n][0].T, g[n].T, ms[n][0].T, vs[n][0].T, T)
            delta[n], new_m[n], new_v[n] = d_.T, m_.T, v_.T
        else:
            T = 352 if n == 'w_ff2' else 256
            delta[n], new_m[n], new_v[n] = adamw("adamw_" + n, wts[n][0], g[n], ms[n][0], vs[n][0], T)
    lay = Flat([(n, wts[n].size) for n in SMALL_NAMES])
    d_, m_, v_ = adamw("adamw_small", lay.pack(wts), lay.pack(g), lay.pack(ms), lay.pack(vs), lay.rows)
    for n in SMALL_NAMES:
        delta[n], new_m[n], new_v[n] = (lay.get(b, n) for b in (d_, m_, v_))

    shp = lambda d: [d[n].reshape(wts[n].shape) for n in WEIGHTS]
    return (loss, grad_x[None], *shp(g), *shp(delta), *shp(new_m), *shp(new_v))
```

```python
import functools

import jax
import jax.numpy as jnp
from jax import lax
from jax.experimental import pallas as pl
from jax.experimental.pallas import tpu as pltpu
from jax.experimental.pallas import tpu_sc as plsc

F32 = jnp.float32
BF16 = jnp.bfloat16
MESH = pl.DeviceIdType.MESH

VMEM_LIMIT_V7X = 56 * 1024 * 1024

D = 1024
LC = 256
Q = 128
NH = 16
D_FF = 2816
LN_EPS = 1e-5
ALPHA = 2.0 ** 0.25

PW = 7168
C_GATE, C_UV, C_Z, C_XBC, C_DT = 0, 2048, 4096, 5120, 6656
D_PROJ = 6688

ADAM_LR, ADAM_B1, ADAM_B2, ADAM_EPS, ADAM_WD, ADAM_STEP = 0.001, 0.9, 0.999, 1e-08, 0.01, 10


def _cp(*sem):
    return pltpu.CompilerParams(dimension_semantics=sem, vmem_limit_bytes=VMEM_LIMIT_V7X)


def _dot(a, b, ca, cb):
    return lax.dot_general(a.astype(BF16), b.astype(BF16), (((ca,), (cb,)), ((), ())),
                           preferred_element_type=F32)


@jax.custom_vjp
def mm(a, b):
    return _dot(a, b, 1, 0)


mm.defvjp(lambda a, b: (_dot(a, b, 1, 0), (a, b)),
          lambda r, g: (_dot(g, r[1], 1, 1), _dot(r[0], g, 0, 0)))


@jax.custom_vjp
def mm_nt(a, b):
    return _dot(a, b, 1, 1)


mm_nt.defvjp(lambda a, b: (_dot(a, b, 1, 1), (a, b)),
             lambda r, g: (_dot(g, r[1], 1, 0), _dot(g, r[0], 0, 0)))


@jax.custom_vjp
def mm_tn(a, b):
    return _dot(a, b, 0, 0)


mm_tn.defvjp(lambda a, b: (_dot(a, b, 0, 0), (a, b)),
             lambda r, g: (_dot(r[1], g, 1, 1), _dot(r[0], g, 1, 0)))


def _dot32(a, b):
    return lax.dot_general(a, b, (((1,), (0,)), ((), ())), precision=lax.Precision.HIGHEST,
                           preferred_element_type=F32)


def _cumsum_fn(rev):
    def tri(transpose):
        r = lax.broadcasted_iota(jnp.int32, (Q, Q), 0)
        c = lax.broadcasted_iota(jnp.int32, (Q, Q), 1)
        keep = (r >= c) if (rev == transpose) else (r <= c)
        return jnp.where(keep, 1.0, 0.0).astype(F32)

    @jax.custom_vjp
    def cums(a):
        return _dot32(tri(False), a)

    cums.defvjp(lambda a: (_dot32(tri(False), a), None), lambda _, g: (_dot32(tri(True), g),))
    return cums


def _cols(v, k):
    w = v.shape[1] // k
    return tuple(v[:, w * i:w * (i + 1)] for i in range(k))


def _splitter(k):
    @jax.custom_vjp
    def split(v):
        return _cols(v, k)

    @jax.custom_vjp
    def concat(ps):
        return jnp.concatenate(ps, axis=1)

    split.defvjp(lambda v: (_cols(v, k), None), lambda _, g: (jnp.concatenate(g, axis=1),))
    concat.defvjp(lambda ps: (jnp.concatenate(ps, axis=1), None), lambda _, g: (_cols(g, k),))
    return split, concat


split2, _ = _splitter(2)
split4, _ = _splitter(4)
split8, concat8 = _splitter(8)


def _ln(x, g, b):
    mu = jnp.mean(x, axis=-1, keepdims=True)
    xc = x - mu
    var = jnp.mean(xc * xc, axis=-1, keepdims=True)
    return xc * lax.rsqrt(var + LN_EPS) * g + b


def _silu(x):
    return x * jax.nn.sigmoid(x)


def _gelu(x):
    return 0.5 * x * (1.0 + jnp.tanh(0.7978845608028654 * (x + 0.044715 * (x * x * x))))


def _xspec(T, w, col, roff):
    return pl.BlockSpec((T, w), lambda i, col=col, roff=roff: (jnp.maximum(i + roff, 0), col))


def _pspec(p, sel):
    if sel is None:
        return pl.BlockSpec(p.shape, lambda i, n=p.ndim: (0,) * n)
    return pl.BlockSpec((1,) + p.shape[1:], lambda i, n=p.ndim: (sel(i),) + (0,) * (n - 1))


def _out_plumbing(outs, T, args, in_specs):
    shapes, specs, aliases = [], [], {}
    for k, o in enumerate(outs):
        if o[0] == 'new':
            _, rows, w, roff = o[:4]
            shapes.append(jax.ShapeDtypeStruct((rows, w), o[4] if len(o) > 4 else F32))
            specs.append(_xspec(T, w, 0, roff))
        elif o[0] == 'acc':
            shapes.append(jax.ShapeDtypeStruct(o[1], F32))
            specs.append(pl.BlockSpec(o[1], lambda i, n=len(o[1]): (0,) * n))
        elif o[0] == 'part':
            _, rows, wtot, w, col, roff, dtype = o
            shapes.append(jax.ShapeDtypeStruct((rows, wtot), dtype))
            specs.append(_xspec(T, w, col, roff))
        else:
            _, arr, w, col, roff = o
            aliases[len(args)] = k
            args.append(arr)
            in_specs.append(pl.BlockSpec(memory_space=pl.ANY))
            shapes.append(jax.ShapeDtypeStruct(arr.shape, arr.dtype))
            specs.append(_xspec(T, w, col, roff))
    return shapes, specs, aliases


def stage_fwd(name, f, T, n, xs, ps, outs, rows=None, gather=None):
    nx, npar = len(xs), len(ps)
    args = [x[0] for x in xs] + [p[0] for p in ps]
    in_specs = [_xspec(T, w, col, roff) for (_, w, col, roff) in xs] + [_pspec(p, sel) for (p, sel) in ps]
    n_in = len(args)
    shapes, specs, aliases = _out_plumbing(outs, T, args, in_specs)
    n_all_in = len(args)
    scratch = []
    if gather is not None:
        args.append(gather)
        in_specs.append(pl.BlockSpec(memory_space=pl.ANY))
        shapes.append(jax.ShapeDtypeStruct((8,) + gather.shape, gather.dtype))
        specs.append(pl.BlockSpec(memory_space=pl.ANY))
        scratch = [pltpu.SemaphoreType.DMA((7,)), pltpu.SemaphoreType.DMA((7,)), pltpu.SemaphoreType.DMA]

    def exchange(i, blk_ref, out_ref, send_sems, recv_sems, local_sem):
        x, y, c = _place()
        me, sibling = (x, y, c), (x, y, 1 - c)
        chips = [(1 - x, y), (x, 1 - y), (1 - x, 1 - y)]

        def copy(k, block, to, own=False):
            dst = out_ref.at[_slot(block)]
            return pltpu.make_async_remote_copy(src_ref=blk_ref if own else dst, dst_ref=dst, send_sem=send_sems.at[k],
                                                recv_sem=recv_sems.at[k], device_id=to, device_id_type=MESH)

        mine = pltpu.make_async_copy(blk_ref, out_ref.at[_slot(me)], local_sem)
        first = [copy(0, me, sibling, own=True)] + [copy(1 + j, me, (*ch, c), own=True) for j, ch in enumerate(chips)]
        passed = [copy(4 + j, (*ch, c), sibling) for j, ch in enumerate(chips)]

        @pl.when(i == 0)
        def _():
            mine.start()
            for cp in first:
                cp.start()

        @pl.when(i == n // 2)
        def _():
            for j, ch in enumerate(chips):
                copy(1 + j, (*ch, c), me).wait_recv()
                passed[j].start()

        @pl.when(i == n - 1)
        def _():
            copy(0, sibling, me).wait_recv()
            for j, ch in enumerate(chips):
                copy(4 + j, (*ch, 1 - c), me).wait_recv()
            for cp in first + passed:
                cp.wait_send()
            mine.wait()

    n_out = len(outs)

    def body(*refs):
        i = pl.program_id(0)
        if gather is not None:
            exchange(i, refs[n_all_in], refs[n_all_in + 1 + n_out], *refs[n_all_in + 2 + n_out:])
        out_refs = refs[len(args):len(args) + n_out]
        pv = [r[...] if ps[k][1] is None else r[0] for k, r in enumerate(refs[nx:n_in])]
        sums = {}
        for r0 in range(0, T, rows or T):
            g = slice(r0, r0 + (rows or T))
            res = f(*[r[g, :] for r in refs[:nx]], *pv)
            for k, o_ref in enumerate(out_refs):
                if outs[k][0] == 'acc':
                    sums[k] = res[k] if r0 == 0 else sums[k] + res[k]
                else:
                    o_ref[g, :] = res[k].astype(o_ref.dtype)
        for k, v in sums.items():
            o_ref = out_refs[k]

            @pl.when(i == 0)
            def _(o_ref=o_ref, v=v):
                o_ref[...] = v

            @pl.when(i > 0)
            def _(o_ref=o_ref, v=v):
                o_ref[...] += v

    return pl.pallas_call(body, name=name, grid=(n,), in_specs=in_specs, out_specs=specs, out_shape=shapes,
                          input_output_aliases=aliases, scratch_shapes=scratch,
                          compiler_params=_cp("arbitrary"))(*args)


def stage_bwd(name, f, T, n, xs, ps, cts, dxs, dps, primal=(), rows=None):
    nx, npar = len(xs), len(ps)
    args = [x[0] for x in xs] + [p[0] for p in ps]
    in_specs = [_xspec(T, w, col, roff) for (_, w, col, roff) in xs] + [_pspec(p, sel) for (p, sel) in ps]
    ct_arrs = [c for c in cts if isinstance(c, tuple)]
    for (a, w, col, roff) in ct_arrs:
        args.append(a)
        in_specs.append(_xspec(T, w, col, roff))
    n_in = len(args)
    outs, out_of = [], []
    for k, o in enumerate(dxs):
        if o is not None:
            outs.append(o)
            out_of.append(('x', k))
    for k, want in enumerate(dps):
        if want:
            p, sel = ps[k]
            outs.append(('acc', p.shape))
            out_of.append(('p', k))
    for k, shape in primal:
        outs.append(('acc', shape))
        out_of.append(('r', k))
    shapes, specs, aliases = _out_plumbing(outs, T, args, in_specs)
    for j, (kind, k) in enumerate(out_of):
        if kind == 'p' and ps[k][1] is not None:
            p, sel = ps[k]
            specs[j] = pl.BlockSpec((1,) + p.shape[1:], lambda i, n=p.ndim, sel=sel: (sel(i),) + (0,) * (n - 1))
    n_all_in = len(args)

    def body(*refs):
        i = pl.program_id(0)
        pv = [r[...] if ps[k][1] is None else r[0] for k, r in enumerate(refs[nx:nx + npar])]
        sums = {}
        for r0 in range(0, T, rows or T):
            g = slice(r0, r0 + (rows or T))
            res, vjp_fn = jax.vjp(f, *[r[g, :] for r in refs[:nx]], *pv)
            ctv, q = [], nx + npar
            for k, c in enumerate(cts):
                if c is None:
                    ctv.append(jnp.zeros_like(res[k]))
                elif isinstance(c, tuple):
                    v = refs[q][g, :]
                    if c[3] < 0:
                        v = v * (i + c[3] >= 0).astype(F32)
                    ctv.append(v)
                    q += 1
                else:
                    ctv.append(jnp.full_like(res[k], c))
            grads = vjp_fn(tuple(ctv))
            for j, o_ref in enumerate(refs[n_all_in:]):
                kind, k = out_of[j]
                if kind == 'x':
                    o_ref[g, :] = grads[k].astype(o_ref.dtype)
                else:
                    v = res[k] if kind == 'r' else grads[nx + k]
                    sums[j] = v if r0 == 0 else sums[j] + v
        for j, v in sums.items():
            kind, k = out_of[j]
            o_ref = refs[n_all_in + j]
            sel = None if kind == 'r' else ps[k][1]
            if sel is None:
                first, tgt = i == 0, o_ref
            else:
                first, tgt = jnp.logical_or(i == 0, sel(i) != sel(jnp.maximum(i - 1, 0))), o_ref.at[0]

            @pl.when(first)
            def _(tgt=tgt, v=v):
                tgt[...] = v

            @pl.when(jnp.logical_not(first))
            def _(tgt=tgt, v=v):
                tgt[...] += v

    return pl.pallas_call(body, name=name, grid=(n,), in_specs=in_specs, out_specs=specs, out_shape=shapes,
                          input_output_aliases=aliases, compiler_params=_cp("arbitrary"))(*args)


_CONTRACT = {'nn': (1, 0), 'nt': (1, 1), 'tn': (0, 0)}


def matmul(name, a, b, mode, tm, tn, tk, out_dtype=F32, add=None):
    if mode == 'nn':
        (M, K), (_, N) = a.shape, b.shape
    elif mode == 'nt':
        (M, K), (N, _) = a.shape, b.shape
    else:
        (K, M), (_, N) = a.shape, b.shape
    assert M % tm == 0 and N % tn == 0 and K % tk == 0, (name, M, N, K, tm, tn, tk)
    a_spec = (pl.BlockSpec((tk, tm), lambda j, i, k: (k, i)) if mode == 'tn'
              else pl.BlockSpec((tm, tk), lambda j, i, k: (i, k)))
    b_spec = (pl.BlockSpec((tn, tk), lambda j, i, k: (j, k)) if mode == 'nt'
              else pl.BlockSpec((tk, tn), lambda j, i, k: (k, j)))
    o_spec = pl.BlockSpec((tm, tn), lambda j, i, k: (i, j))
    return matmul_call(name, (N // tn, M // tm, K // tk), a, a_spec, b, b_spec, (M, N), o_spec, (tm, tn), mode,
                       out_dtype, add)


def matmul_call(name, grid, a, a_spec, b, b_spec, out_shape, o_spec, tile, mode, out_dtype=F32, add=None):
    tm, tn = tile
    nk = grid[2]
    ca, cb = _CONTRACT[mode]
    args, in_specs = [a, b], [a_spec, b_spec]
    if add is not None:
        args.append(add)
        in_specs.append(o_spec)

    def body(*refs):
        a_ref, b_ref = refs[0], refs[1]
        o_ref, acc = refs[-2], refs[-1]
        k = pl.program_id(2)
        if nk == 1:
            p = _dot(a_ref[...], b_ref[...], ca, cb)
            o_ref[...] = (p + refs[2][...] if add is not None else p).astype(out_dtype)
            return

        @pl.when(k == 0)
        def _():
            acc[...] = refs[2][...] if add is not None else jnp.zeros_like(acc)

        acc[...] += _dot(a_ref[...], b_ref[...], ca, cb)

        @pl.when(k == nk - 1)
        def _():
            o_ref[...] = acc[...].astype(out_dtype)

    return pl.pallas_call(body, name=name, grid=grid, in_specs=in_specs, out_specs=o_spec,
                          out_shape=jax.ShapeDtypeStruct(out_shape, out_dtype),
                          scratch_shapes=[pltpu.VMEM((tm, tn) if nk > 1 else (8, 128), F32)],
                          compiler_params=_cp("arbitrary", "arbitrary", "arbitrary"))(*args)


NS, WS = 4, 704


def _resident(name, M, tm, rows, weight, out_shape, out_block, out_map, step):
    in_specs = [pl.BlockSpec(rows[1], rows[2]), pl.BlockSpec(weight.shape, lambda i, n=weight.ndim: (0,) * n)]
    return pl.pallas_call(step, name=name, grid=(M // tm,), in_specs=in_specs, out_specs=pl.BlockSpec(out_block, out_map),
                          out_shape=jax.ShapeDtypeStruct(out_shape, F32), compiler_params=_cp("arbitrary"))(rows[0], weight)


def ffn_in_fwd(name, h, w1, w3, tm):
    M = h.shape[0]

    def step(h_ref, w1_ref, w3_ref, a1_ref, a3_ref, act_ref):
        for s in range(NS):
            a1 = _dot(h_ref[...], w1_ref[s], 1, 0)
            a3 = _dot(h_ref[...], w3_ref[s], 1, 0)
            a1_ref[s] = a1.astype(a1_ref.dtype)
            a3_ref[s] = a3.astype(a3_ref.dtype)
            act_ref[s] = (_silu(a1) * a3).astype(act_ref.dtype)

    wspec = pl.BlockSpec((NS, D, WS), lambda i: (0, 0, 0))
    ospec = pl.BlockSpec((NS, tm, WS), lambda i: (0, i, 0))
    return pl.pallas_call(
        step, name=name, grid=(M // tm,), in_specs=[pl.BlockSpec((tm, D), lambda i: (i, 0)), wspec, wspec],
        out_specs=[ospec, ospec, ospec],
        out_shape=[jax.ShapeDtypeStruct((NS, M, WS), BF16)] * 3, compiler_params=_cp("arbitrary"))(h, w1, w3)


def ffn_out_bwd_x(name, dff, w2, a1, a3, tm):
    M = dff.shape[0]

    def step(d_ref, w_ref, a1_ref, a3_ref, da1_ref, da3_ref):
        for s in range(NS):
            dact = _dot(d_ref[...], w_ref[s * WS:(s + 1) * WS, :], 1, 1)
            a1 = a1_ref[s].astype(F32)
            sig = jax.nn.sigmoid(a1)
            da3_ref[s] = (dact * (a1 * sig)).astype(da3_ref.dtype)
            da1_ref[s] = (dact * a3_ref[s].astype(F32) * (sig * (1.0 + a1 * (1.0 - sig)))).astype(da1_ref.dtype)

    aspec = pl.BlockSpec((NS, tm, WS), lambda i: (0, i, 0))
    return pl.pallas_call(
        step, name=name, grid=(M // tm,),
        in_specs=[pl.BlockSpec((tm, D), lambda i: (i, 0)), pl.BlockSpec(w2.shape, lambda i: (0, 0)), aspec, aspec],
        out_specs=[aspec, aspec],
        out_shape=[jax.ShapeDtypeStruct((NS, M, WS), BF16)] * 2, compiler_params=_cp("arbitrary"))(dff, w2, a1, a3)


def ff_in_bwd_x(name, da1, da3, w1, w3, tm):
    M = da1.shape[1]

    def step(d1_ref, d3_ref, w1_ref, w3_ref, o_ref):
        def product(d_ref, w_ref):
            acc = _dot(d_ref[0], w_ref[0], 1, 1)
            for s in range(1, NS):
                acc = acc + _dot(d_ref[s], w_ref[s], 1, 1)
            return acc

        o_ref[...] = product(d3_ref, w3_ref) + product(d1_ref, w1_ref)

    dspec = pl.BlockSpec((NS, tm, WS), lambda i: (0, i, 0))
    wspec = pl.BlockSpec((NS, D, WS), lambda i: (0, 0, 0))
    return pl.pallas_call(step, name=name, grid=(M // tm,), in_specs=[dspec, dspec, wspec, wspec],
                          out_specs=pl.BlockSpec((tm, D), lambda i: (i, 0)),
                          out_shape=jax.ShapeDtypeStruct((M, D), F32), compiler_params=_cp("arbitrary"))(da1, da3, w1, w3)


def ff_in_bwd_w(name, h, da3, tk):
    M = h.shape[0]

    def step(h_ref, d_ref, acc):
        for s in range(NS):
            acc[s] += _dot(h_ref[...], d_ref[s], 0, 0)

    return _token_sum(name, M // tk, [pl.BlockSpec((tk, D), lambda k: (k, 0)), pl.BlockSpec((NS, tk, WS), lambda k: (0, k, 0))],
                      (NS, D, WS), (NS, D, WS), step, (h, da3))


def ff_out_fwd(name, act3, w2, tm):
    M = act3.shape[1]

    def step(a_ref, w_ref, o_ref):
        acc = _dot(a_ref[0], w_ref[0:WS, :], 1, 0)
        for s in range(1, NS):
            acc = acc + _dot(a_ref[s], w_ref[s * WS:(s + 1) * WS, :], 1, 0)
        o_ref[...] = acc

    return _resident(name, M, tm, (act3, (NS, tm, WS), lambda i: (0, i, 0)), w2, (M, D), (tm, D), lambda i: (i, 0), step)


def _token_sum(name, nk, in_specs, out_shape, acc_shape, step, args):
    def body(*refs):
        o_ref, acc = refs[-2], refs[-1]
        k = pl.program_id(0)

        @pl.when(k == 0)
        def _():
            acc[...] = jnp.zeros_like(acc)

        step(*refs[:-2], acc)

        @pl.when(k == nk - 1)
        def _():
            o_ref[...] = acc[...].astype(o_ref.dtype)

    return pl.pallas_call(body, name=name, grid=(nk,), in_specs=in_specs,
                          out_specs=pl.BlockSpec(out_shape, lambda k, n=len(out_shape): (0,) * n),
                          out_shape=jax.ShapeDtypeStruct(out_shape, BF16), scratch_shapes=[pltpu.VMEM(acc_shape, F32)],
                          compiler_params=_cp("arbitrary"))(*args)


def ff_out_bwd_w(name, act3, dff, tk):
    M = dff.shape[0]

    def step(a_ref, d_ref, acc):
        for s in range(NS):
            acc[s * WS:(s + 1) * WS, :] += _dot(a_ref[s], d_ref[...], 0, 0)

    return _token_sum(name, M // tk, [pl.BlockSpec((NS, tk, WS), lambda k: (0, k, 0)), pl.BlockSpec((tk, D), lambda k: (k, 0))],
                      (NS * WS, D), (NS * WS, D), step, (act3, dff))


HALO = 8
CONV_ROWS = 128


def _fill(pad_ref, n, v=None):
    edge = jnp.zeros((HALO, 128), F32)
    pad_ref[0:HALO, :] = edge
    pad_ref[HALO + n:2 * HALO + n, :] = edge
    if v is not None:
        pad_ref[HALO:HALO + n, :] = v


def _tap(pad_ref, r0, k, rows=CONV_ROWS):
    return pad_ref[HALO + r0 + k - 2:HALO + r0 + k - 2 + rows, :]


def _conv_pre(pad_ref, r0, w_ref, b_ref):
    acc = _tap(pad_ref, r0, 0) * w_ref[0:1, :] + b_ref[...]
    for k in range(1, 5):
        acc = acc + _tap(pad_ref, r0, k) * w_ref[k:k + 1, :]
    return acc


def conv_fwd(name, proj, conv_w, conv_b, R):
    segs = ((0, LC), (LC, R))

    def body(x_ref, w_ref, b_ref, o_ref, xp):
        for (s, e) in segs:
            _fill(xp, e - s, x_ref[s:e, :])
            for r0 in range(0, e - s, CONV_ROWS):
                o_ref[s + r0:s + r0 + CONV_ROWS, :] = _silu(_conv_pre(xp, r0, w_ref, b_ref))

    return pl.pallas_call(
        body, name=name, grid=(12,),
        in_specs=[pl.BlockSpec((R, 128), lambda j: (0, C_XBC // 128 + j)),
                  pl.BlockSpec((8, 128), lambda j: (0, j)), pl.BlockSpec((1, 128), lambda j: (0, j))],
        out_specs=pl.BlockSpec((R, 128), lambda j: (0, j)),
        out_shape=jax.ShapeDtypeStruct((R, 1536), F32), scratch_shapes=[pltpu.VMEM((R - LC + 2 * HALO, 128), F32)],
        compiler_params=_cp("arbitrary"))(proj, conv_w, conv_b)


def conv_bwd(name, proj, conv_w, conv_b, d_f, d_b, d_skip, dproj, R):
    segs = ((0, LC), (LC, R))

    def body(x_ref, w_ref, b_ref, df_ref, db_ref, ds_ref, _, dx_ref, dw_ref, dbias_ref, xp, dp):
        j = pl.program_id(0)
        has_skip = (j < 8).astype(F32)
        dw = [jnp.zeros((8, 128), F32) for _ in range(5)]
        dbias = jnp.zeros((8, 128), F32)
        fold = lambda v: jnp.sum(v.reshape(CONV_ROWS // 8, 8, 128), axis=0)
        for (s, e) in segs:
            n = e - s
            _fill(xp, n, x_ref[s:e, :])
            _fill(dp, n)
            for r0 in range(0, n, CONV_ROWS):
                rows = slice(s + r0, s + r0 + CONV_ROWS)
                pre = _conv_pre(xp, r0, w_ref, b_ref)
                sig = jax.nn.sigmoid(pre)
                dy = df_ref[rows, :] + db_ref[rows, :]
                if s == LC:
                    dy = dy + ds_ref[r0:r0 + CONV_ROWS, :] * has_skip
                dpre = dy * (sig * (1.0 + pre * (1.0 - sig)))
                dp[HALO + r0:HALO + r0 + CONV_ROWS, :] = dpre
                dbias = dbias + fold(dpre)
            for r0 in range(0, n, CONV_ROWS):
                x = x_ref[s + r0:s + r0 + CONV_ROWS, :]
                dx = jnp.zeros_like(x)
                for k in range(5):
                    d = _tap(dp, r0, 4 - k)
                    dx = dx + d * w_ref[k:k + 1, :]
                    dw[k] = dw[k] + fold(d * x)
                dx_ref[s + r0:s + r0 + CONV_ROWS, :] = dx.astype(dx_ref.dtype)
        dw_ref[...] = jnp.zeros_like(dw_ref)
        for k in range(5):
            dw_ref[k:k + 1, :] = jnp.sum(dw[k], axis=0, keepdims=True)
        dbias_ref[...] = jnp.sum(dbias, axis=0, keepdims=True)

    pad = pltpu.VMEM((R - LC + 2 * HALO, 128), F32)
    return pl.pallas_call(
        body, name=name, grid=(12,),
        in_specs=[pl.BlockSpec((R, 128), lambda j: (0, C_XBC // 128 + j)),
                  pl.BlockSpec((8, 128), lambda j: (0, j)), pl.BlockSpec((1, 128), lambda j: (0, j)),
                  pl.BlockSpec((R, 128), lambda j: (0, j)), pl.BlockSpec((R, 128), lambda j: (0, j)),
                  pl.BlockSpec((R - LC, 128), lambda j: (0, jnp.minimum(j, 7))),
                  pl.BlockSpec(memory_space=pl.ANY)],
        out_specs=[pl.BlockSpec((R, 128), lambda j: (0, C_XBC // 128 + j)),
                   pl.BlockSpec((8, 128), lambda j: (0, j)), pl.BlockSpec((1, 128), lambda j: (0, j))],
        out_shape=[jax.ShapeDtypeStruct(dproj.shape, dproj.dtype), jax.ShapeDtypeStruct((8, 1536), F32),
                   jax.ShapeDtypeStruct((1, 1536), F32)], scratch_shapes=[pad, pad],
        input_output_aliases={6: 0}, compiler_params=_cp("arbitrary"))(proj, conv_w, conv_b, d_f, d_b, d_skip, dproj)


def _ssd_chunk(rev, dirn):
    cums = _cumsum_fn(rev)

    def f(xs, Bs, Cs, dt, alog, Hs):
        lane = lax.broadcasted_iota(jnp.int32, (1, 128), 1)
        sub = lax.broadcasted_iota(jnp.int32, (Q, 1), 0)
        r = lax.broadcasted_iota(jnp.int32, (Q, Q), 0)
        c = lax.broadcasted_iota(jnp.int32, (Q, Q), 1)
        mask = (r <= c) if rev else (r >= c)
        left = lane < 64
        a = dt * (-jnp.exp(alog))
        s = cums(a)
        sT, dtT = s.T, dt.T
        last_row = (sub == (0 if rev else Q - 1)).astype(F32)
        s_last = jnp.sum(s * last_row, axis=0, keepdims=True)
        G = [mm_nt(Cs[g], Bs[g]) for g in range(2)]
        M, es, wc, ed = [], [], [], []
        for h in range(NH):
            l = 16 * dirn + h
            oh_l = (lane == l).astype(F32)
            oh_s = (sub == l).astype(F32)
            s_col = jnp.sum(s * oh_l, axis=1, keepdims=True)
            dt_col = jnp.sum(dt * oh_l, axis=1, keepdims=True)
            s_row = jnp.sum(sT * oh_s, axis=0, keepdims=True)
            dt_row = jnp.sum(dtT * oh_s, axis=0, keepdims=True)
            sl = jnp.sum(s_last * oh_l, axis=1, keepdims=True)
            seg = jnp.where(mask, s_col - s_row, 0.0)
            lm = jnp.where(mask, jnp.exp(seg), 0.0)
            M.append(G[h // 8] * lm * dt_row)
            es.append(jnp.exp(s_col))
            wc.append(jnp.exp(sl - s_col) * dt_col)
            ed.append(jnp.exp(sl))
        Ys, Hn = [], []
        for j in range(8):
            g = j // 4
            xa = jnp.where(left, xs[j], 0.0)
            xb = jnp.where(left, 0.0, xs[j])
            yd = mm(M[2 * j], xa) + mm(M[2 * j + 1], xb)
            yo = mm(Cs[g], Hs[j]) * jnp.where(left, es[2 * j], es[2 * j + 1])
            Ys.append(yd + yo)
            st = mm_tn(Bs[g], xs[j] * jnp.where(left, wc[2 * j], wc[2 * j + 1]))
            Hn.append(Hs[j] * jnp.where(left, ed[2 * j], ed[2 * j + 1]) + st)
        return Ys, Hn

    return f


def _chunk_of(t, n, rev):
    if not rev:
        return t
    return jnp.where(t < 2, 1 - t, n + 1 - t)


def _cols128(ref, k, lead=()):
    return [ref[lead + (slice(None), slice(128 * j, 128 * (j + 1)))] for j in range(k)]


def ssd_fwd(name, xbc, dts, alog, n, dirs):
    nd = len(dirs)
    chunks = [_ssd_chunk(rev, dirn) for rev, dirn in dirs]

    def body(*refs):
        al_ref = refs[4 * nd]
        for d in range(nd):
            x_ref, b_ref, c_ref, dt_ref = refs[4 * d:4 * d + 4]
            y_ref, hs_ref = refs[4 * nd + 1 + 2 * d:4 * nd + 3 + 2 * d]
            h_scr = refs[4 * nd + 1 + 2 * nd + d]

            @pl.when(pl.program_id(0) == 0)
            def _(h_scr=h_scr):
                h_scr[...] = jnp.zeros_like(h_scr)

            hs_ref[0] = h_scr[...]
            Ys, Hn = chunks[d](_cols128(x_ref, 8), _cols128(b_ref, 2), _cols128(c_ref, 2), dt_ref[...], al_ref[...],
                               _cols128(h_scr, 8))
            for j in range(8):
                y_ref[:, 128 * j:128 * (j + 1)] = Ys[j]
                h_scr[:, 128 * j:128 * (j + 1)] = Hn[j]

    in_specs, out_specs, out_shape, args = [], [], [], []
    for (rev, _), dt in zip(dirs, dts):
        cm = lambda t, rev=rev: _chunk_of(t, n, rev)
        in_specs += [pl.BlockSpec((Q, 1024), lambda t, cm=cm: (cm(t), 0)), pl.BlockSpec((Q, 256), lambda t, cm=cm: (cm(t), 4)),
                     pl.BlockSpec((Q, 256), lambda t, cm=cm: (cm(t), 5)), pl.BlockSpec((Q, 128), lambda t, cm=cm: (cm(t), 0))]
        args += [xbc, xbc, xbc, dt]
        out_specs += [pl.BlockSpec((Q, 1024), lambda t, cm=cm: (cm(t), 0)),
                      pl.BlockSpec((1, Q, 1024), lambda t, cm=cm: (cm(t), 0, 0))]
        out_shape += [jax.ShapeDtypeStruct((n * Q, 1024), F32), jax.ShapeDtypeStruct((n, Q, 1024), F32)]
    res = pl.pallas_call(
        body, name=name, grid=(n,), in_specs=in_specs + [pl.BlockSpec((1, 128), lambda t: (0, 0))],
        out_specs=out_specs, out_shape=out_shape, scratch_shapes=[pltpu.VMEM((Q, 1024), F32)] * nd,
        compiler_params=_cp("arbitrary"))(*args, alog)
    return [res[2 * d:2 * d + 2] for d in range(nd)]


def ssd_bwd(name, xbc, dts, alog, hss, dy, n, dirs):
    nd = len(dirs)
    chunks = [_ssd_chunk(rev, dirn) for rev, dirn in dirs]

    def body(*refs):
        tt = pl.program_id(0)
        al_ref = refs[6 * nd]
        for d, (rev, _) in enumerate(dirs):
            x_ref, b_ref, c_ref, dt_ref, hs_ref, dy_ref = refs[6 * d:6 * d + 6]
            dx_ref, ddt_ref, dal_ref = refs[6 * nd + 1 + 3 * d:6 * nd + 4 + 3 * d]
            dh_scr = refs[6 * nd + 1 + 3 * nd + d]
            ch = _chunk_of(n - 1 - tt, n, rev)

            @pl.when(tt == 0)
            def _(dh_scr=dh_scr):
                dh_scr[...] = jnp.zeros_like(dh_scr)

            live = (ch >= 2).astype(F32)
            dYs = [v * live for v in _cols128(dy_ref, 8)]
            _, vjp_fn = jax.vjp(chunks[d], _cols128(x_ref, 8), _cols128(b_ref, 2), _cols128(c_ref, 2), dt_ref[...],
                                al_ref[...], _cols128(hs_ref, 8, (0,)))
            dxs, dBs, dCs, ddt, dal, dHs = vjp_fn((dYs, _cols128(dh_scr, 8)))
            for j in range(8):
                dx_ref[:, 128 * j:128 * (j + 1)] = dxs[j]
                dh_scr[:, 128 * j:128 * (j + 1)] = dHs[j]
            for g in range(2):
                dx_ref[:, 1024 + 128 * g:1024 + 128 * (g + 1)] = dBs[g]
                dx_ref[:, 1280 + 128 * g:1280 + 128 * (g + 1)] = dCs[g]
            ddt_ref[...] = ddt

            @pl.when(tt == 0)
            def _(dal_ref=dal_ref, dal=dal):
                dal_ref[...] = dal

            @pl.when(tt > 0)
            def _(dal_ref=dal_ref, dal=dal):
                dal_ref[...] += dal

    in_specs, out_specs, out_shape, args = [], [], [], []
    for (rev, _), dt, hs in zip(dirs, dts, hss):
        cm = lambda t, rev=rev: _chunk_of(n - 1 - t, n, rev)
        in_specs += [pl.BlockSpec((Q, 1024), lambda t, cm=cm: (cm(t), 0)), pl.BlockSpec((Q, 256), lambda t, cm=cm: (cm(t), 4)),
                     pl.BlockSpec((Q, 256), lambda t, cm=cm: (cm(t), 5)), pl.BlockSpec((Q, 128), lambda t, cm=cm: (cm(t), 0)),
                     pl.BlockSpec((1, Q, 1024), lambda t, cm=cm: (cm(t), 0, 0)),
                     pl.BlockSpec((Q, 1024), lambda t, cm=cm: (jnp.maximum(cm(t) - 2, 0), 0))]
        args += [xbc, xbc, xbc, dt, hs, dy]
        out_specs += [pl.BlockSpec((Q, 1536), lambda t, cm=cm: (cm(t), 0)), pl.BlockSpec((Q, 128), lambda t, cm=cm: (cm(t), 0)),
                      pl.BlockSpec((1, 128), lambda t: (0, 0))]
        out_shape += [jax.ShapeDtypeStruct((n * Q, 1536), F32), jax.ShapeDtypeStruct((n * Q, 128), F32),
                      jax.ShapeDtypeStruct((1, 128), F32)]
    res = pl.pallas_call(
        body, name=name, grid=(n,), in_specs=in_specs + [pl.BlockSpec((1, 128), lambda t: (0, 0))],
        out_specs=out_specs, out_shape=out_shape, scratch_shapes=[pltpu.VMEM((Q, 1024), F32)] * nd,
        compiler_params=_cp("arbitrary"))(*args, alog)
    return [res[3 * d:3 * d + 3] for d in range(nd)]


def f_norm0(c, x, g0, b0, sc, sh, is_ctx):
    x0 = _ln(jnp.where(is_ctx > 0.5, c, x), g0, b0)
    return x0, x0 * (1.0 + sc) + sh


def f_dt(raw, bias):
    z = split4(raw)[0] + bias
    dt = jnp.maximum(z, 0.0) + jnp.log1p(jnp.exp(-jnp.abs(z)))
    return dt, dt


def f_gated_norm(yf, yb, xs, z, dcol, g):
    h = (yf + yb + xs * dcol) * _silu(z)
    return (h * lax.rsqrt(jnp.mean(h * h, axis=-1, keepdims=True) + LN_EPS) * g,)


def f_gmlp(uv, gmg, gmb, *wb):
    ws, bs = wb[:8], wb[8:]
    u, v = split2(uv)
    vn = split8(_ln(_gelu(v), gmg, gmb))
    mixed = concat8(tuple(mm(ws[g], vn[g]) + bs[g] for g in range(8)))
    return (_gelu(u) * mixed,)


def f_merge(ps, pg, gates, bg):
    gs, gg = split2(jax.nn.sigmoid(gates + bg))
    return (gs * ps + gg * pg,)


def f_merge_out(ps, pg, gates, bg, w_out):
    return (mm(f_merge(ps, pg, gates, bg)[0], w_out),)


def f_project_merge(yn, y_gm, gates, w_ssd, w_gm, bg):
    ps, pg = mm(yn, w_ssd), mm(y_gm, w_gm)
    return f_merge(ps, pg, gates, bg) + (ps, pg)


def f_res1(x0, out, g1, lg, lb, sc, sh):
    x1 = _ln(ALPHA * x0 + g1 * out, lg, lb)
    return x1, x1 * (1.0 + sc) + sh


def f_res2_loss(x1, ff, tgt, g2, lg, lb):
    x2 = _ln(ALPHA * x1 + g2 * ff, lg, lb)
    e = x2 - tgt
    return (0.5 * jnp.sum(jnp.mean(e * e, axis=-1, keepdims=True), axis=0, keepdims=True),)


def _row_tile(M):
    return 544 if M % 544 == 0 else (512 if M % 512 == 0 else M)


def core(ctx, x, tgt, mod_x, mod_c, X, S):
    L = x.shape[0]
    R = LC + L
    n = R // Q
    T = 256
    nt, ntl = R // T, L // T
    tmR, tmL = _row_tile(R), _row_tile(L)
    tmS = 1024 if L % 1024 == 0 else tmL
    tkR = 256 if R % 512 else 512
    tkL = 512 if L % 512 == 0 else 256
    row = lambda v: v.reshape(1, -1)
    mx = [row(mod_x[k]) for k in range(6)]
    mc = [row(mod_c[k]) for k in range(6)]
    sel = lambda i: jnp.minimum(i, 1)
    sc1 = jnp.stack([mc[1], mx[1]])
    sh1 = jnp.stack([mc[0], mx[0]])
    ln0 = [(row(S['ln0_g']), None), (row(S['ln0_b']), None), (sc1, sel), (sh1, sel),
           (jnp.array([1.0, 0.0], F32).reshape(2, 1, 1), sel)]
    x_n0 = [(ctx, D, 0, -nt), (x, D, 0, -1)]

    x0, xm, *landed = stage_fwd("norm0_fwd", f_norm0, T, nt, x_n0, ln0, [('new', R, D, 0), ('new', R, D, 0, BF16)],
                                gather=X.w_in_block())
    w_in = X.w_in(*landed)
    proj = matmul("proj_fwd", xm, w_in, 'nn', tmR, PW // 2, 1024)
    conv_w8 = jnp.pad(S['conv_w'], ((0, 3), (0, 0)))
    conv_b = row(S['conv_b'])
    xbc = conv_fwd("conv_fwd", proj, conv_w8, conv_b, R)
    dt_bias = jnp.pad(S['dt_bias'].reshape(1, 32), ((0, 0), (0, 96)))
    alog = jnp.pad(S['a_log'].reshape(1, 32), ((0, 0), (0, 96)))
    x_dt = [(proj, 512, C_DT // 512, 0)]
    dt_f, dt_b = stage_fwd("dt_fwd", f_dt, T, nt, x_dt, [(dt_bias, None)], [('new', R, 128, 0), ('new', R, 128, 0)])
    directions = [(False, 0), (True, 1)]
    (y_f, hs_f), (y_b, hs_b) = ssd_fwd("ssd_fwd", xbc, [dt_f, dt_b], alog, n, directions)
    W = X.rest()
    dcol = jnp.repeat(S['d_skip'][0] + S['d_skip'][1], 64).reshape(1, D)
    x_gn = [(y_f, D, 0, 1), (y_b, D, 0, 1), (xbc, D, 0, 1), (proj, D, C_Z // D, 1)]
    p_gn = [(dcol, None), (row(S['ssd_norm_g']), None)]
    (yn,) = stage_fwd("gnorm_fwd", f_gated_norm, T, ntl, x_gn, p_gn, [('new', L, D, 0, BF16)])
    x_gm = [(proj, 2 * D, C_UV // (2 * D), LC // Q)]
    p_gm = ([(row(S['gm_norm_g']), None), (row(S['gm_norm_b']), None)]
            + [(S['w_spatial'][g], None) for g in range(8)] + [(S['b_spatial'][g].reshape(Q, 1), None) for g in range(8)])
    (y_gm,) = stage_fwd("gmlp_fwd", f_gmlp, Q, L // Q, x_gm, p_gm, [('new', L, D, 0, BF16)])
    x_pm = [(yn, D, 0, 0), (y_gm, D, 0, 0), (proj, 2 * D, C_GATE // (2 * D), 1)]
    p_mg = [(row(S['b_gate']), None)]
    p_pm = [(W['w_ssd_proj'], None), (W['w_gm_proj'], None)] + p_mg
    merged, p_ssd, p_g = stage_fwd("merge_fwd", f_project_merge, T, ntl, x_pm, p_pm,
                                   [('new', L, D, 0, BF16), ('new', L, D, 0), ('new', L, D, 0)])
    out = matmul("out_fwd", merged, W['w_out'], 'nn', tmS, 1024, 1024)
    x_r1 = [(x0, D, 0, 1), (out, D, 0, 0)]
    p_r1 = [(mx[2], None), (row(S['ln1_g']), None), (row(S['ln1_b']), None), (mx[4], None), (mx[3], None)]
    x1, hm = stage_fwd("res1_fwd", f_res1, T, ntl, x_r1, p_r1, [('new', L, D, 0), ('new', L, D, 0, BF16)])
    a1, a3, act = ffn_in_fwd("ffn_in_fwd", hm, W['w_ff1'], W['w_ff3'], tmL)
    ff = ff_out_fwd("ff2_fwd", act, W['w_ff2'], tmL)
    x_r2 = [(x1, D, 0, 0), (ff, D, 0, 0), (tgt, D, 0, 0)]
    p_r2 = [(mx[5], None), (row(S['ln2_g']), None), (row(S['ln2_b']), None)]

    dx1_a, dff, dg2, dl2g, dl2b, loss = stage_bwd(
        "res2_bwd", f_res2_loss, T, ntl, x_r2, p_r2, [1.0],
        [('new', L, D, 0), ('new', L, D, 0, BF16), None], [True, True, True], primal=[(0, (1, 1))])
    da1, da3 = ffn_out_bwd_x("ffn_out_bwd_x", dff, W['w_ff2'], a1, a3, tmL)
    gw_ff2 = ff_out_bwd_w("ff2_bwd_w", act, dff, tkL)
    dhm = ff_in_bwd_x("ffn_in_bwd_x", da1, da3, W['w_ff1'], W['w_ff3'], tmL)
    gw_ff1 = ff_in_bwd_w("ff1_bwd_w", hm, da1, tkL)
    gw_ff3 = ff_in_bwd_w("ff3_bwd_w", hm, da3, tkL)
    X.grads('ffn', {'w_ff2': gw_ff2, 'w_ff1': gw_ff1, 'w_ff3': gw_ff3})
    dx0_a, dout, dg1, dl1g, dl1b, dsc2, dsh2 = stage_bwd(
        "res1_bwd", f_res1, T, ntl, x_r1, p_r1, [(dx1_a, D, 0, 0), (dhm, D, 0, 0)],
        [('new', L, D, 0), ('new', L, D, 0, BF16)], [True] * 5)
    lt, lq = -(LC // T), -(LC // Q)
    x_mg_b = [(p_ssd, D, 0, lt), (p_g, D, 0, lt), (proj, 2 * D, C_GATE // (2 * D), 0)]
    dp_ssd, dp_g, dproj, dbg, gw_out = stage_bwd(
        "merge_bwd", f_merge_out, T, nt, x_mg_b, p_mg + [(W['w_out'], None)], [(dout, D, 0, lt)],
        [('new', L, D, lt, BF16), ('new', L, D, lt, BF16), ('part', R, PW, 2 * D, C_GATE // (2 * D), 0, BF16)],
        [True, True])
    gw_out = gw_out.astype(BF16)
    dyn = matmul("pssd_bwd_x", dp_ssd, W['w_ssd_proj'], 'nt', tmS, 1024, 1024)
    gw_ssd = matmul("pssd_bwd_w", yn, dp_ssd, 'tn', 1024, 1024, tkL, BF16)
    dy_gm = matmul("pgm_bwd_x", dp_g, W['w_gm_proj'], 'nt', tmS, 1024, 1024)
    gw_gm = matmul("pgm_bwd_w", y_gm, dp_g, 'tn', 1024, 1024, tkL, BF16)
    X.grads('proj', {'w_out': gw_out, 'w_ssd_proj': gw_ssd, 'w_gm_proj': gw_gm})
    r_gm = stage_bwd("gmlp_bwd", f_gmlp, Q, n, [(proj, 2 * D, C_UV // (2 * D), 0)], p_gm, [(dy_gm, D, 0, lq)],
                     [('alias', dproj, 2 * D, C_UV // (2 * D), 0)], [True] * 18)
    dproj, dgmg, dgmb, dws, dbs = r_gm[0], r_gm[1], r_gm[2], r_gm[3:11], r_gm[11:19]
    x_gn_b = [(y_f, D, 0, 0), (y_b, D, 0, 0), (xbc, D, 0, 0), (proj, D, C_Z // D, 0)]
    dy, dskipx, dproj, ddcol, dng = stage_bwd(
        "gnorm_bwd", f_gated_norm, T, nt, x_gn_b, p_gn, [(dyn, D, 0, lt)],
        [('new', L, D, lt), None, ('new', L, D, lt), ('alias', dproj, D, C_Z // D, 0)], [True, True], rows=32)
    (dxbc_f, ddt_f, dal_f), (dxbc_b, ddt_b, dal_b) = ssd_bwd("ssd_bwd", xbc, [dt_f, dt_b], alog, [hs_f, hs_b], dy, n,
                                                             directions)
    dproj, ddtb = stage_bwd("dt_bwd", f_dt, T, nt, x_dt, [(dt_bias, None)],
                            [(ddt_f, 128, 0, 0), (ddt_b, 128, 0, 0)],
                            [('alias', dproj, 512, C_DT // 512, 0)], [True])
    dproj, dcw8, dcb = conv_bwd("conv_bwd", proj, conv_w8, conv_b, dxbc_f, dxbc_b, dskipx, dproj, R)
    gw_in = matmul("proj_bwd_w", xm, dproj, 'tn', 1024, PW // 2, tkR, BF16)
    X.grads('in', {'w_in': gw_in})
    dxm = matmul("proj_bwd_x", dproj, w_in, 'nt', R // 2 if R % 16 == 0 else R, 1024, 1024)
    grad_x, dl0g, dl0b, dsc1, dsh1 = stage_bwd(
        "norm0_bwd", f_norm0, T, nt, x_n0, ln0, [(dx0_a, D, 0, -1), (dxm, D, 0, 0)],
        [None, ('new', L, D, -1)], [True] * 4 + [False])

    zero = jnp.zeros((D,), F32)
    flat = lambda v: v.reshape(-1)
    small = {
        'loss': flat(loss), 'ln0_g': flat(dl0g), 'ln0_b': flat(dl0b),
        'dmod_x': jnp.concatenate([flat(dsh1[1]), flat(dsc1[1]), flat(dg1), flat(dsh2), flat(dsc2), flat(dg2)]),
        'dmod_c': jnp.concatenate([flat(dsh1[0]), flat(dsc1[0]), zero, zero, zero, zero]),
        'conv_w': flat(dcw8[:5]), 'conv_b': flat(dcb), 'dt_bias': flat(ddtb[:, :32]),
        'a_log': flat((dal_f + dal_b)[:, :32]),
        'd_skip': flat(jnp.tile(ddcol.reshape(1, NH, 64).sum(-1), (2, 1))),
        'ssd_norm_g': flat(dng), 'gm_norm_g': flat(dgmg), 'gm_norm_b': flat(dgmb),
        'w_spatial': flat(jnp.stack(dws)), 'b_spatial': flat(jnp.stack(dbs)), 'b_gate': flat(dbg),
        'ln1_g': flat(dl1g), 'ln1_b': flat(dl1b), 'ln2_g': flat(dl2g), 'ln2_b': flat(dl2b),
    }
    return grad_x, small


def _place():
    return lax.axis_index("x"), lax.axis_index("y"), lax.axis_index("c")


def allgather8(name, blk, hbm):
    space = pl.ANY if hbm else pltpu.VMEM

    def body(x_ref, out_ref, send_sems, recv_sems, local_sem):
        x, y, c = _place()
        me, sibling = (x, y, c), (x, y, 1 - c)
        chips = [(1 - x, y), (x, 1 - y), (1 - x, 1 - y)]

        def slot(px, py, pc):
            return out_ref.at[4 * px + 2 * py + pc]

        def copy(k, block, to, src=None):
            return pltpu.make_async_remote_copy(
                src_ref=slot(*block) if src is None else src, dst_ref=slot(*block),
                send_sem=send_sems.at[k], recv_sem=recv_sems.at[k], device_id=to, device_id_type=MESH)

        mine = pltpu.make_async_copy(x_ref, slot(*me), local_sem)
        mine.start()
        first = [copy(0, me, sibling, src=x_ref)]
        first += [copy(1 + j, me, (*chip, c), src=x_ref) for j, chip in enumerate(chips)]
        for cp in first:
            cp.start()
        passed = [copy(4 + j, (*chip, c), sibling) for j, chip in enumerate(chips)]
        for j, chip in enumerate(chips):
            copy(1 + j, (*chip, c), me).wait_recv()
            passed[j].start()
        copy(0, sibling, me).wait_recv()
        for j, chip in enumerate(chips):
            copy(4 + j, (*chip, 1 - c), me).wait_recv()
        for cp in first + passed:
            cp.wait_send()
        mine.wait()

    return pl.pallas_call(
        body, name=name, out_shape=jax.ShapeDtypeStruct((8,) + blk.shape, blk.dtype),
        in_specs=[pl.BlockSpec(memory_space=space)], out_specs=pl.BlockSpec(memory_space=space),
        scratch_shapes=[pltpu.SemaphoreType.DMA((7,)), pltpu.SemaphoreType.DMA((7,)), pltpu.SemaphoreType.DMA],
        compiler_params=pltpu.CompilerParams(vmem_limit_bytes=VMEM_LIMIT_V7X))(blk)


def _peers(place):
    x, y, c = place
    return [((1 - x) if k & 4 else x, (1 - y) if k & 2 else y, (1 - c) if k & 1 else c) for k in range(1, 8)]


def _slot(p):
    return 4 * p[0] + 2 * p[1] + p[2]


def plan_gather(place, srcs, lands):
    remote = [(s, l.at[_slot(place)], to) for s, l in zip(srcs, lands) for to in _peers(place)]
    return remote, [(s, l.at[_slot(place)]) for s, l in zip(srcs, lands)]


def plan_to_owner(place, srcs, lands):
    remote = [(s.at[2 * to[0] + to[1], to[2]], l.at[_slot(place)], to) for s, l in zip(srcs, lands) for to in _peers(place)]
    x, y, c = place
    return remote, [(s.at[2 * x + y, c], l.at[_slot(place)]) for s, l in zip(srcs, lands)]


def sequencer_exchange(name, collective_id, srcs, land_shapes, plan):
    n = len(srcs)
    src_refs = [jax.new_ref(a, memory_space=pltpu.MemorySpace.HBM) for a in srcs]
    land_refs = [jax.empty_ref(s, memory_space=pltpu.MemorySpace.HBM) for s in land_shapes]

    @pl.kernel(mesh=plsc.ScalarSubcoreMesh(axis_name="sequencer", num_cores=1), name=name,
               scratch_types=(pltpu.SemaphoreType.DMA((7 * n,)), pltpu.SemaphoreType.DMA((7 * n,)),
                              pltpu.SemaphoreType.DMA((n,))),
               compiler_params=pltpu.CompilerParams(collective_id=collective_id))
    def launch(send_sems, recv_sems, local_sems):
        place = _place()
        barrier = pltpu.get_barrier_semaphore()
        for to in _peers(place):
            pl.semaphore_signal(barrier, inc=1, device_id=to, device_id_type=MESH)
        pl.semaphore_wait(barrier, 7)
        remote, local = plan(place, src_refs, land_refs)
        mine = [pltpu.make_async_copy(s, d, local_sems.at[a]) for a, (s, d) in enumerate(local)]
        for cp in mine:
            cp.start()
        cps = [pltpu.make_async_remote_copy(src_ref=s, dst_ref=d, send_sem=send_sems.at[k], recv_sem=recv_sems.at[k],
                                            device_id=to, device_id_type=MESH) for k, (s, d, to) in enumerate(remote)]
        for cp in cps:
            cp.start()
        for cp in mine:
            cp.wait()
        for cp in cps:
            cp.wait()

    launch()
    return land_refs


def sibling_pair(name, hs):
    n = len(hs)

    def body(*refs):
        ins, outs = refs[:n], refs[n:2 * n]
        send_sems, recv_sems = refs[2 * n:]
        x, y, c = _place()
        cps = [pltpu.make_async_remote_copy(src_ref=outs[a].at[c], dst_ref=outs[a].at[c], send_sem=send_sems.at[a],
                                            recv_sem=recv_sems.at[a], device_id=(x, y, 1 - c), device_id_type=MESH)
               for a in range(n)]
        for cp in cps:
            cp.start()
        for a in range(n):
            pltpu.make_async_remote_copy(src_ref=outs[a].at[1 - c], dst_ref=outs[a].at[1 - c], send_sem=send_sems.at[a],
                                         recv_sem=recv_sems.at[a], device_id=(x, y, 1 - c),
                                         device_id_type=MESH).wait_recv()
        for cp in cps:
            cp.wait_send()

    any_spec = pl.BlockSpec(memory_space=pl.ANY)
    return pl.pallas_call(
        body, name=name, out_shape=[jax.ShapeDtypeStruct(h.shape, h.dtype) for h in hs],
        in_specs=[any_spec] * n, out_specs=[any_spec] * n, input_output_aliases={a: a for a in range(n)},
        scratch_shapes=[pltpu.SemaphoreType.DMA((n,)), pltpu.SemaphoreType.DMA((n,))])(*hs)


def owner_sum(name, land):
    _, r, w = land.shape
    T = r // 2

    def body(_, l_ref, o_ref):
        acc = l_ref[0].astype(F32)
        for j in range(1, 8):
            acc = acc + l_ref[j].astype(F32)
        o_ref[...] = acc

    grid_spec = pltpu.PrefetchScalarGridSpec(
        num_scalar_prefetch=1, grid=(2,),
        in_specs=[pl.BlockSpec((8, T, w), lambda i, at: (0, i, 0))],
        out_specs=pl.BlockSpec((None, T, w), lambda i, at: (at[0], i, 0)))
    at = jnp.stack([lax.axis_index("c")]).astype(jnp.int32)
    return pl.pallas_call(body, name=name, grid_spec=grid_spec, out_shape=jax.ShapeDtypeStruct((2, r, w), F32),
                          compiler_params=_cp("arbitrary"))(at, land)


W_IN_RUNS = ((0, 2, 1296, 376), (376, 3, 0, 1672), (2048, 1, 920, 752), (2800, 2, 0, 1296), (4096, 0, 0, 1024),
             (5120, 0, 1024, 648), (5768, 1, 0, 920))


def w_in_to_padded(name, g4):
    T = 128

    def body(g_ref, o_ref):
        o_ref[:, D_PROJ:PW] = jnp.zeros((T, PW - D_PROJ), o_ref.dtype)
        for (a, s, j0, w) in W_IN_RUNS:
            o_ref[:, a:a + w] = g_ref[s, :, j0:j0 + w]

    return pl.pallas_call(body, name=name, grid=(D // T,), in_specs=[pl.BlockSpec((4, T, 1672), lambda i: (0, i, 0))],
                          out_specs=pl.BlockSpec((T, PW), lambda i: (i, 0)),
                          out_shape=jax.ShapeDtypeStruct((D, PW), g4.dtype), compiler_params=_cp("arbitrary"))(g4)


def w_in_from_padded(name, gp):
    T = 128

    def body(g_ref, o_ref):
        for (a, s, j0, w) in W_IN_RUNS:
            o_ref[s, :, j0:j0 + w] = g_ref[:, a:a + w]

    return pl.pallas_call(body, name=name, grid=(D // T,), in_specs=[pl.BlockSpec((T, PW), lambda i: (i, 0))],
                          out_specs=pl.BlockSpec((4, T, 1672), lambda i: (0, i, 0)),
                          out_shape=jax.ShapeDtypeStruct((4, D, 1672), gp.dtype), compiler_params=_cp("arbitrary"))(gp)


def sum_devices(name, g):
    def body(g_ref, o_ref):
        acc = g_ref[0]
        for k in range(1, 8):
            acc = acc + g_ref[k]
        o_ref[...] = acc

    return pl.pallas_call(body, name=name, out_shape=jax.ShapeDtypeStruct(g.shape[1:], F32),
                          compiler_params=pltpu.CompilerParams(vmem_limit_bytes=VMEM_LIMIT_V7X))(g)


def adamw(name, w, g, m, v, T):
    r, wd = w.shape
    c1 = 1.0 - ADAM_B1 ** ADAM_STEP
    c2 = 1.0 - ADAM_B2 ** ADAM_STEP

    def body(w_ref, g_ref, m_ref, v_ref, d_ref, mo_ref, vo_ref):
        gv = g_ref[...]
        mn = ADAM_B1 * m_ref[...] + (1.0 - ADAM_B1) * gv
        vn = ADAM_B2 * v_ref[...] + (1.0 - ADAM_B2) * (gv * gv)
        d_ref[...] = -ADAM_LR * ((mn / c1) / (jnp.sqrt(vn / c2) + ADAM_EPS) + ADAM_WD * w_ref[...])
        mo_ref[...] = mn
        vo_ref[...] = vn

    spec = pl.BlockSpec((T, wd), lambda i: (i, 0))
    return pl.pallas_call(body, name=name, grid=(r // T,), in_specs=[spec] * 4, out_specs=[spec] * 3,
                          out_shape=[jax.ShapeDtypeStruct((r, wd), F32)] * 3, compiler_params=_cp("arbitrary"))(w, g, m, v)


BIG = {'w_in': (1024, 1672), 'w_ssd_proj': (256, 1024), 'w_gm_proj': (256, 1024), 'w_out': (256, 1024),
       'w_ff1': (1024, 704), 'w_ff3': (1024, 704), 'w_ff2': (704, 1024)}


class Flat:
    def __init__(self, segs):
        self.off, o = {}, 0
        for name, size in segs:
            self.off[name] = (o, size)
            o += -(-size // 128) * 128
        self.rows = -(-o // 1024) * 8

    def pack(self, vals):
        parts = []
        for name, (o, size) in self.off.items():
            v = vals[name].reshape(-1).astype(F32)
            parts.append(jnp.pad(v, (0, -(-size // 128) * 128 - size)))
        buf = jnp.concatenate(parts)
        return jnp.pad(buf, (0, self.rows * 128 - buf.shape[0])).reshape(self.rows, 128)

    def get(self, buf, name, shape=None):
        o, size = self.off[name]
        v = buf[o // 128:(o + size + 127) // 128].reshape(-1)[:size]
        return v if shape is None else v.reshape(shape)


PARTIALS = Flat([('loss', 1), ('ln0_g', D), ('ln0_b', D), ('dmod_x', 6 * D), ('dmod_c', 6 * D), ('conv_w', 5 * 1536),
                 ('conv_b', 1536), ('dt_bias', 32), ('a_log', 32), ('d_skip', 32), ('ssd_norm_g', D),
                 ('gm_norm_g', D), ('gm_norm_b', D), ('w_spatial', 8 * Q * Q), ('b_spatial', 8 * Q), ('b_gate', 2 * D),
                 ('ln1_g', D), ('ln1_b', D), ('ln2_g', D), ('ln2_b', D)])

WEIGHTS = ('c_ctx', 'ln0_g', 'ln0_b', 'w_ada', 'b_ada', 'w_in', 'conv_w', 'conv_b', 'dt_bias', 'a_log', 'd_skip',
           'ssd_norm_g', 'gm_norm_g', 'gm_norm_b', 'w_spatial', 'b_spatial', 'b_gate', 'w_ssd_proj', 'w_gm_proj',
           'w_out', 'ln1_g', 'ln1_b', 'w_ff1', 'w_ff3', 'w_ff2', 'ln2_g', 'ln2_b')
BIG_NAMES = tuple(BIG)
SMALL_NAMES = tuple(n for n in WEIGHTS if n not in BIG_NAMES and n != 'w_ada')


def kernel(x, c, ctx, c_ctx, ln0_g, ln0_b, w_ada, b_ada, w_in, conv_w, conv_b, dt_bias, a_log, d_skip, ssd_norm_g, gm_norm_g, gm_norm_b, w_spatial, b_spatial, b_gate, w_ssd_proj, w_gm_proj, w_out, ln1_g, ln1_b, w_ff1, w_ff3, w_ff2, ln2_g, ln2_b, loss_target, m_c_ctx, m_ln0_g, m_ln0_b, m_w_ada, m_b_ada, m_w_in, m_conv_w, m_conv_b, m_dt_bias, m_a_log, m_d_skip, m_ssd_norm_g, m_gm_norm_g, m_gm_norm_b, m_w_spatial, m_b_spatial, m_b_gate, m_w_ssd_proj, m_w_gm_proj, m_w_out, m_ln1_g, m_ln1_b, m_w_ff1, m_w_ff3, m_w_ff2, m_ln2_g, m_ln2_b, v_c_ctx, v_ln0_g, v_ln0_b, v_w_ada, v_b_ada, v_w_in, v_conv_w, v_conv_b, v_dt_bias, v_a_log, v_d_skip, v_ssd_norm_g, v_gm_norm_g, v_gm_norm_b, v_w_spatial, v_b_spatial, v_b_gate, v_w_ssd_proj, v_w_gm_proj, v_w_out, v_ln1_g, v_ln1_b, v_w_ff1, v_w_ff3, v_w_ff2, v_ln2_g, v_ln2_b):
    wts = dict(c_ctx=c_ctx, ln0_g=ln0_g, ln0_b=ln0_b, w_ada=w_ada, b_ada=b_ada, w_in=w_in, conv_w=conv_w, conv_b=conv_b,
               dt_bias=dt_bias, a_log=a_log, d_skip=d_skip, ssd_norm_g=ssd_norm_g, gm_norm_g=gm_norm_g,
               gm_norm_b=gm_norm_b, w_spatial=w_spatial, b_spatial=b_spatial, b_gate=b_gate, w_ssd_proj=w_ssd_proj,
               w_gm_proj=w_gm_proj, w_out=w_out, ln1_g=ln1_g, ln1_b=ln1_b, w_ff1=w_ff1, w_ff3=w_ff3, w_ff2=w_ff2,
               ln2_g=ln2_g, ln2_b=ln2_b)
    ms = dict(zip(WEIGHTS, (m_c_ctx, m_ln0_g, m_ln0_b, m_w_ada, m_b_ada, m_w_in, m_conv_w, m_conv_b, m_dt_bias, m_a_log,
                            m_d_skip, m_ssd_norm_g, m_gm_norm_g, m_gm_norm_b, m_w_spatial, m_b_spatial, m_b_gate,
                            m_w_ssd_proj, m_w_gm_proj, m_w_out, m_ln1_g, m_ln1_b, m_w_ff1, m_w_ff3, m_w_ff2, m_ln2_g,
                            m_ln2_b)))
    vs = dict(zip(WEIGHTS, (v_c_ctx, v_ln0_g, v_ln0_b, v_w_ada, v_b_ada, v_w_in, v_conv_w, v_conv_b, v_dt_bias, v_a_log,
                            v_d_skip, v_ssd_norm_g, v_gm_norm_g, v_gm_norm_b, v_w_spatial, v_b_spatial, v_b_gate,
                            v_w_ssd_proj, v_w_gm_proj, v_w_out, v_ln1_g, v_ln1_b, v_w_ff1, v_w_ff3, v_w_ff2, v_ln2_g,
                            v_ln2_b)))
    px, py, pc = _place()
    shard = 2 * px + py
    dev = 2 * shard + pc
    take = lambda a, i, axis=0: lax.dynamic_index_in_dim(a, i, axis, keepdims=False)

    half = lambda n: take(wts[n][0].reshape(2, BIG[n][0] // 2, BIG[n][1]), pc).astype(BF16)

    pre = jnp.concatenate([c, jnp.pad(conv_w[0], ((0, 0), (0, D - 384))), jnp.zeros((2, D), F32)], axis=0)
    pre = allgather8("gather_cond", pre, False)
    conv_w_full = pre[0::2, 1:6, :384].transpose(1, 0, 2).reshape(5, 1536)
    a16 = jnp.concatenate([_silu(pre[:, 0, :]), _silu(c_ctx)[None], jnp.zeros((7, D), F32)], axis=0)
    mod = matmul("ada_fwd", a16, w_ada[0], 'nn', 16, 512, 1024)
    mod = mod + lax.dynamic_slice_in_dim(b_ada[0], shard * 1536, 1536)[None]
    mod = allgather8("gather_mod", mod, False)
    mod = jnp.concatenate([mod[0], mod[2], mod[4], mod[6]], axis=1)
    mod_x = take(mod, dev).reshape(6, D)
    mod_c = mod[8].reshape(6, D)

    def full(n, blocks):
        r, w = BIG[n]
        return blocks.reshape(4, r, w) if w != D else blocks.reshape(4 * r, w)

    class Exchanges:
        rest_names = BIG_NAMES[1:]

        def __init__(self):
            self.pending = []

        def w_in_block(self):
            return half('w_in')

        def w_in(self, blocks):
            w = w_in_to_padded("w_in_layout", full('w_in', blocks))
            halves = [half(n) for n in self.rest_names]
            halves[0], _ = lax.optimization_barrier((halves[0], blocks))
            lands = [jax.ShapeDtypeStruct((8,) + h.shape, BF16) for h in halves]
            self.rest_refs = sequencer_exchange("gather_rest", 1, halves, lands, plan_gather)
            return w

        def rest(self):
            return {n: full(n, r[...]) for n, r in zip(self.rest_names, self.rest_refs)}

        def grads(self, group, gs):
            if group == 'in':
                gs = {'w_in': w_in_from_padded("w_in_grad_layout", gs['w_in'])}
            blocks = [g.reshape(4, 2, BIG[n][0] // 2, BIG[n][1]) for n, g in gs.items()]
            lands = [jax.ShapeDtypeStruct((8,) + b.shape[2:], BF16) for b in blocks]
            refs = sequencer_exchange("grads_" + group, 2 + len(self.pending), blocks, lands, plan_to_owner)
            self.pending.append((tuple(gs), refs))

        def finish(self):
            names, halves = [], []
            for ns, refs in self.pending:
                names += ns
                halves += [owner_sum("grads_sum_" + n, r[...]) for n, r in zip(ns, refs)]
            return {n: h.reshape(BIG[n]) for n, h in zip(names, sibling_pair("grads_halves", halves))}

    S = dict(ln0_g=ln0_g, ln0_b=ln0_b, conv_w=conv_w_full, conv_b=conv_b[0], dt_bias=dt_bias[0], a_log=a_log[0],
             d_skip=d_skip[0], ssd_norm_g=ssd_norm_g[0], gm_norm_g=gm_norm_g[0], gm_norm_b=gm_norm_b[0],
             w_spatial=w_spatial[0], b_spatial=b_spatial[0], b_gate=b_gate[0], ln1_g=ln1_g[0], ln1_b=ln1_b[0],
             ln2_g=ln2_g[0], ln2_b=ln2_b[0])
    exchanges = Exchanges()
    grad_x, gsmall = core(ctx[0], x[0], loss_target[0], mod_x, mod_c, exchanges, S)

    parts = allgather8("gather_partials", PARTIALS.pack(gsmall), False)
    tot = sum_devices("partials_sum", parts)
    g_shards = exchanges.finish()
    g = {n: PARTIALS.get(tot, n) for n in ('ln0_g', 'ln0_b', 'conv_b', 'dt_bias', 'a_log', 'd_skip', 'ssd_norm_g',
                                           'gm_norm_g', 'gm_norm_b', 'w_spatial', 'b_spatial', 'b_gate', 'ln1_g',
                                           'ln1_b', 'ln2_g', 'ln2_b')}
    loss = PARTIALS.get(tot, 'loss', ())
    dmod_c = PARTIALS.get(tot, 'dmod_c')
    g['b_ada'] = PARTIALS.get(tot, 'dmod_x') + dmod_c
    g['conv_w'] = lax.dynamic_slice_in_dim(PARTIALS.get(tot, 'conv_w', (5, 1536)), shard * 384, 384, axis=1)
    o, size = PARTIALS.off['dmod_x']
    dmod_rows = parts[:, o // 128:(o + size) // 128].reshape(8, size)
    dm = jnp.concatenate([dmod_rows, dmod_c[None], jnp.zeros((7, 6 * D), F32)], axis=0)
    dm = lax.dynamic_slice_in_dim(dm, shard * 1536, 1536, axis=1)
    g['w_ada'] = matmul("ada_bwd_w", a16, dm, 'tn', 1024, 512, 16)
    dm_c = jnp.concatenate([dm[8:9], jnp.zeros((15, 1536), F32)], axis=0)
    dc = matmul("ada_bwd_c", dm_c, w_ada[0], 'nt', 16, 1024, 512)
    dc = allgather8("gather_dcctx", dc, False)[:, 0, :]
    dc = ((dc[0] + dc[2]) + dc[4]) + dc[6]
    sg = jax.nn.sigmoid(c_ctx)
    g['c_ctx'] = dc * (sg * (1.0 + c_ctx * (1.0 - sg)))
    for n in BIG_NAMES:
        g[n] = g_shards[n]

    delta, new_m, new_v = {}, {}, {}
    for n in BIG_NAMES + ('w_ada',):
        r, w = wts[n].shape[1:]
        if w % 128:
            T = max(t for t in range(8, 257, 8) if w % t == 0)
            d_, m_, v_ = adamw("adamw_" + n, wts[n][0].T, g[n].T, ms[n][0].T, vs[n][0].T, T)
            delta[n], new_m[n], new_v[n] = d_.T, m_.T, v_.T
        else:
            T = 352 if n == 'w_ff2' else 256
            delta[n], new_m[n], new_v[n] = adamw("adamw_" + n, wts[n][0], g[n], ms[n][0], vs[n][0], T)
    lay = Flat([(n, wts[n].size) for n in SMALL_NAMES])
    d_, m_, v_ = adamw("adamw_small", lay.pack(wts), lay.pack(g), lay.pack(ms), lay.pack(vs), lay.rows)
    for n in SMALL_NAMES:
        delta[n], new_m[n], new_v[n] = (lay.get(b, n) for b in (d_, m_, v_))

    shp = lambda d: [d[n].reshape(wts[n].shape) for n in WEIGHTS]
    return (loss, grad_x[None], *shp(g), *shp(delta), *shp(new_m), *shp(new_v))
```

```python
import functools

import jax
import jax.numpy as jnp
from jax import lax
from jax.experimental import pallas as pl
from jax.experimental.pallas import tpu as pltpu
from jax.experimental.pallas import tpu_sc as plsc

F32 = jnp.float32
BF16 = jnp.bfloat16
MESH = pl.DeviceIdType.MESH

VMEM_LIMIT_V7X = 56 * 1024 * 1024

D = 1024
LC = 256
Q = 128
NH = 16
D_FF = 2816
LN_EPS = 1e-5
ALPHA = 2.0 ** 0.25

PW = 7168
C_GATE, C_UV, C_Z, C_XBC, C_DT = 0, 2048, 4096, 5120, 6656
D_PROJ = 6688

ADAM_LR, ADAM_B1, ADAM_B2, ADAM_EPS, ADAM_WD, ADAM_STEP = 0.001, 0.9, 0.999, 1e-08, 0.01, 10


def _cp(*sem):
    return pltpu.CompilerParams(dimension_semantics=sem, vmem_limit_bytes=VMEM_LIMIT_V7X)


def _dot(a, b, ca, cb):
    return lax.dot_general(a.astype(BF16), b.astype(BF16), (((ca,), (cb,)), ((), ())),
                           preferred_element_type=F32)


@jax.custom_vjp
def mm(a, b):
    return _dot(a, b, 1, 0)


mm.defvjp(lambda a, b: (_dot(a, b, 1, 0), (a, b)),
          lambda r, g: (_dot(g, r[1], 1, 1), _dot(r[0], g, 0, 0)))


@jax.custom_vjp
def mm_nt(a, b):
    return _dot(a, b, 1, 1)


mm_nt.defvjp(lambda a, b: (_dot(a, b, 1, 1), (a, b)),
             lambda r, g: (_dot(g, r[1], 1, 0), _dot(g, r[0], 0, 0)))


@jax.custom_vjp
def mm_tn(a, b):
    return _dot(a, b, 0, 0)


mm_tn.defvjp(lambda a, b: (_dot(a, b, 0, 0), (a, b)),
             lambda r, g: (_dot(r[1], g, 1, 1), _dot(r[0], g, 1, 0)))


def _dot32(a, b):
    return lax.dot_general(a, b, (((1,), (0,)), ((), ())), precision=lax.Precision.HIGHEST,
                           preferred_element_type=F32)


def _cumsum_fn(rev):
    def tri(transpose):
        r = lax.broadcasted_iota(jnp.int32, (Q, Q), 0)
        c = lax.broadcasted_iota(jnp.int32, (Q, Q), 1)
        keep = (r >= c) if (rev == transpose) else (r <= c)
        return jnp.where(keep, 1.0, 0.0).astype(F32)

    @jax.custom_vjp
    def cums(a):
        return _dot32(tri(False), a)

    cums.defvjp(lambda a: (_dot32(tri(False), a), None), lambda _, g: (_dot32(tri(True), g),))
    return cums


def _cols(v, k):
    w = v.shape[1] // k
    return tuple(v[:, w * i:w * (i + 1)] for i in range(k))


def _splitter(k):
    @jax.custom_vjp
    def split(v):
        return _cols(v, k)

    @jax.custom_vjp
    def concat(ps):
        return jnp.concatenate(ps, axis=1)

    split.defvjp(lambda v: (_cols(v, k), None), lambda _, g: (jnp.concatenate(g, axis=1),))
    concat.defvjp(lambda ps: (jnp.concatenate(ps, axis=1), None), lambda _, g: (_cols(g, k),))
    return split, concat


split2, _ = _splitter(2)
split4, _ = _splitter(4)
split8, concat8 = _splitter(8)


def _ln(x, g, b):
    mu = jnp.mean(x, axis=-1, keepdims=True)
    xc = x - mu
    var = jnp.mean(xc * xc, axis=-1, keepdims=True)
    return xc * lax.rsqrt(var + LN_EPS) * g + b


def _silu(x):
    return x * jax.nn.sigmoid(x)


def _gelu(x):
    return 0.5 * x * (1.0 + jnp.tanh(0.7978845608028654 * (x + 0.044715 * (x * x * x))))


def _xspec(T, w, col, roff):
    return pl.BlockSpec((T, w), lambda i, col=col, roff=roff: (jnp.maximum(i + roff, 0), col))


def _pspec(p, sel):
    if sel is None:
        return pl.BlockSpec(p.shape, lambda i, n=p.ndim: (0,) * n)
    return pl.BlockSpec((1,) + p.shape[1:], lambda i, n=p.ndim: (sel(i),) + (0,) * (n - 1))


def _out_plumbing(outs, T, args, in_specs):
    shapes, specs, aliases = [], [], {}
    for k, o in enumerate(outs):
        if o[0] == 'new':
            _, rows, w, roff = o[:4]
            shapes.append(jax.ShapeDtypeStruct((rows, w), o[4] if len(o) > 4 else F32))
            specs.append(_xspec(T, w, 0, roff))
        elif o[0] == 'acc':
            shapes.append(jax.ShapeDtypeStruct(o[1], F32))
            specs.append(pl.BlockSpec(o[1], lambda i, n=len(o[1]): (0,) * n))
        elif o[0] == 'part':
            _, rows, wtot, w, col, roff, dtype = o
            shapes.append(jax.ShapeDtypeStruct((rows, wtot), dtype))
            specs.append(_xspec(T, w, col, roff))
        else:
            _, arr, w, col, roff = o
            aliases[len(args)] = k
            args.append(arr)
            in_specs.append(pl.BlockSpec(memory_space=pl.ANY))
            shapes.append(jax.ShapeDtypeStruct(arr.shape, arr.dtype))
            specs.append(_xspec(T, w, col, roff))
    return shapes, specs, aliases


def stage_fwd(name, f, T, n, xs, ps, outs, rows=None, gather=None):
    nx, npar = len(xs), len(ps)
    args = [x[0] for x in xs] + [p[0] for p in ps]
    in_specs = [_xspec(T, w, col, roff) for (_, w, col, roff) in xs] + [_pspec(p, sel) for (p, sel) in ps]
    n_in = len(args)
    shapes, specs, aliases = _out_plumbing(outs, T, args, in_specs)
    n_all_in = len(args)
    scratch = []
    if gather is not None:
        args.append(gather)
        in_specs.append(pl.BlockSpec(memory_space=pl.ANY))
        shapes.append(jax.ShapeDtypeStruct((8,) + gather.shape, gather.dtype))
        specs.append(pl.BlockSpec(memory_space=pl.ANY))
        scratch = [pltpu.SemaphoreType.DMA((7,)), pltpu.SemaphoreType.DMA((7,)), pltpu.SemaphoreType.DMA]

    def exchange(i, blk_ref, out_ref, send_sems, recv_sems, local_sem):
        x, y, c = _place()
        me, sibling = (x, y, c), (x, y, 1 - c)
        chips = [(1 - x, y), (x, 1 - y), (1 - x, 1 - y)]

        def copy(k, block, to, own=False):
            dst = out_ref.at[_slot(block)]
            return pltpu.make_async_remote_copy(src_ref=blk_ref if own else dst, dst_ref=dst, send_sem=send_sems.at[k],
                                                recv_sem=recv_sems.at[k], device_id=to, device_id_type=MESH)

        mine = pltpu.make_async_copy(blk_ref, out_ref.at[_slot(me)], local_sem)
        first = [copy(0, me, sibling, own=True)] + [copy(1 + j, me, (*ch, c), own=True) for j, ch in enumerate(chips)]
        passed = [copy(4 + j, (*ch, c), sibling) for j, ch in enumerate(chips)]

        @pl.when(i == 0)
        def _():
            mine.start()
            for cp in first:
                cp.start()

        @pl.when(i == n // 2)
        def _():
            for j, ch in enumerate(chips):
                copy(1 + j, (*ch, c), me).wait_recv()
                passed[j].start()

        @pl.when(i == n - 1)
        def _():
            copy(0, sibling, me).wait_recv()
            for j, ch in enumerate(chips):
                copy(4 + j, (*ch, 1 - c), me).wait_recv()
            for cp in first + passed:
                cp.wait_send()
            mine.wait()

    n_out = len(outs)

    def body(*refs):
        i = pl.program_id(0)
        if gather is not None:
            exchange(i, refs[n_all_in], refs[n_all_in + 1 + n_out], *refs[n_all_in + 2 + n_out:])
        out_refs = refs[len(args):len(args) + n_out]
        pv = [r[...] if ps[k][1] is None else r[0] for k, r in enumerate(refs[nx:n_in])]
        sums = {}
        for r0 in range(0, T, rows or T):
            g = slice(r0, r0 + (rows or T))
            res = f(*[r[g, :] for r in refs[:nx]], *pv)
            for k, o_ref in enumerate(out_refs):
                if outs[k][0] == 'acc':
                    sums[k] = res[k] if r0 == 0 else sums[k] + res[k]
                else:
                    o_ref[g, :] = res[k].astype(o_ref.dtype)
        for k, v in sums.items():
            o_ref = out_refs[k]

            @pl.when(i == 0)
            def _(o_ref=o_ref, v=v):
                o_ref[...] = v

            @pl.when(i > 0)
            def _(o_ref=o_ref, v=v):
                o_ref[...] += v

    return pl.pallas_call(body, name=name, grid=(n,), in_specs=in_specs, out_specs=specs, out_shape=shapes,
                          input_output_aliases=aliases, scratch_shapes=scratch,
                          compiler_params=_cp("arbitrary"))(*args)


def stage_bwd(name, f, T, n, xs, ps, cts, dxs, dps, primal=(), rows=None):
    nx, npar = len(xs), len(ps)
    args = [x[0] for x in xs] + [p[0] for p in ps]
    in_specs = [_xspec(T, w, col, roff) for (_, w, col, roff) in xs] + [_pspec(p, sel) for (p, sel) in ps]
    ct_arrs = [c for c in cts if isinstance(c, tuple)]
    for (a, w, col, roff) in ct_arrs:
        args.append(a)
        in_specs.append(_xspec(T, w, col, roff))
    n_in = len(args)
    outs, out_of = [], []
    for k, o in enumerate(dxs):
        if o is not None:
            outs.append(o)
            out_of.append(('x', k))
    for k, want in enumerate(dps):
        if want:
            p, sel = ps[k]
            outs.append(('acc', p.shape))
            out_of.append(('p', k))
    for k, shape in primal:
        outs.append(('acc', shape))
        out_of.append(('r', k))
    shapes, specs, aliases = _out_plumbing(outs, T, args, in_specs)
    for j, (kind, k) in enumerate(out_of):
        if kind == 'p' and ps[k][1] is not None:
            p, sel = ps[k]
            specs[j] = pl.BlockSpec((1,) + p.shape[1:], lambda i, n=p.ndim, sel=sel: (sel(i),) + (0,) * (n - 1))
    n_all_in = len(args)

    def body(*refs):
        i = pl.program_id(0)
        pv = [r[...] if ps[k][1] is None else r[0] for k, r in enumerate(refs[nx:nx + npar])]
        sums = {}
        for r0 in range(0, T, rows or T):
            g = slice(r0, r0 + (rows or T))
            res, vjp_fn = jax.vjp(f, *[r[g, :] for r in refs[:nx]], *pv)
            ctv, q = [], nx + npar
            for k, c in enumerate(cts):
                if c is None:
                    ctv.append(jnp.zeros_like(res[k]))
                elif isinstance(c, tuple):
                    v = refs[q][g, :]
                    if c[3] < 0:
                        v = v * (i + c[3] >= 0).astype(F32)
                    ctv.append(v)
                    q += 1
                else:
                    ctv.append(jnp.full_like(res[k], c))
            grads = vjp_fn(tuple(ctv))
            for j, o_ref in enumerate(refs[n_all_in:]):
                kind, k = out_of[j]
                if kind == 'x':
                    o_ref[g, :] = grads[k].astype(o_ref.dtype)
                else:
                    v = res[k] if kind == 'r' else grads[nx + k]
                    sums[j] = v if r0 == 0 else sums[j] + v
        for j, v in sums.items():
            kind, k = out_of[j]
            o_ref = refs[n_all_in + j]
            sel = None if kind == 'r' else ps[k][1]
            if sel is None:
                first, tgt = i == 0, o_ref
            else:
                first, tgt = jnp.logical_or(i == 0, sel(i) != sel(jnp.maximum(i - 1, 0))), o_ref.at[0]

            @pl.when(first)
            def _(tgt=tgt, v=v):
                tgt[...] = v

            @pl.when(jnp.logical_not(first))
            def _(tgt=tgt, v=v):
                tgt[...] += v

    return pl.pallas_call(body, name=name, grid=(n,), in_specs=in_specs, out_specs=specs, out_shape=shapes,
                          input_output_aliases=aliases, compiler_params=_cp("arbitrary"))(*args)


_CONTRACT = {'nn': (1, 0), 'nt': (1, 1), 'tn': (0, 0)}


def matmul(name, a, b, mode, tm, tn, tk, out_dtype=F32, add=None):
    if mode == 'nn':
        (M, K), (_, N) = a.shape, b.shape
    elif mode == 'nt':
        (M, K), (N, _) = a.shape, b.shape
    else:
        (K, M), (_, N) = a.shape, b.shape
    assert M % tm == 0 and N % tn == 0 and K % tk == 0, (name, M, N, K, tm, tn, tk)
    a_spec = (pl.BlockSpec((tk, tm), lambda j, i, k: (k, i)) if mode == 'tn'
              else pl.BlockSpec((tm, tk), lambda j, i, k: (i, k)))
    b_spec = (pl.BlockSpec((tn, tk), lambda j, i, k: (j, k)) if mode == 'nt'
              else pl.BlockSpec((tk, tn), lambda j, i, k: (k, j)))
    o_spec = pl.BlockSpec((tm, tn), lambda j, i, k: (i, j))
    return matmul_call(name, (N // tn, M // tm, K // tk), a, a_spec, b, b_spec, (M, N), o_spec, (tm, tn), mode,
                       out_dtype, add)


def matmul_call(name, grid, a, a_spec, b, b_spec, out_shape, o_spec, tile, mode, out_dtype=F32, add=None):
    tm, tn = tile
    nk = grid[2]
    ca, cb = _CONTRACT[mode]
    args, in_specs = [a, b], [a_spec, b_spec]
    if add is not None:
        args.append(add)
        in_specs.append(o_spec)

    def body(*refs):
        a_ref, b_ref = refs[0], refs[1]
        o_ref, acc = refs[-2], refs[-1]
        k = pl.program_id(2)
        if nk == 1:
            p = _dot(a_ref[...], b_ref[...], ca, cb)
            o_ref[...] = (p + refs[2][...] if add is not None else p).astype(out_dtype)
            return

        @pl.when(k == 0)
        def _():
            acc[...] = refs[2][...] if add is not None else jnp.zeros_like(acc)

        acc[...] += _dot(a_ref[...], b_ref[...], ca, cb)

        @pl.when(k == nk - 1)
        def _():
            o_ref[...] = acc[...].astype(out_dtype)

    return pl.pallas_call(body, name=name, grid=grid, in_specs=in_specs, out_specs=o_spec,
                          out_shape=jax.ShapeDtypeStruct(out_shape, out_dtype),
                          scratch_shapes=[pltpu.VMEM((tm, tn) if nk > 1 else (8, 128), F32)],
                          compiler_params=_cp("arbitrary", "arbitrary", "arbitrary"))(*args)


NS, WS = 4, 704


def _resident(name, M, tm, rows, weight, out_shape, out_block, out_map, step):
    in_specs = [pl.BlockSpec(rows[1], rows[2]), pl.BlockSpec(weight.shape, lambda i, n=weight.ndim: (0,) * n)]
    return pl.pallas_call(step, name=name, grid=(M // tm,), in_specs=in_specs, out_specs=pl.BlockSpec(out_block, out_map),
                          out_shape=jax.ShapeDtypeStruct(out_shape, F32), compiler_params=_cp("arbitrary"))(rows[0], weight)


def ffn_in_fwd(name, h, w1, w3, tm):
    M = h.shape[0]

    def step(h_ref, w1_ref, w3_ref, a1_ref, a3_ref, act_ref):
        for s in range(NS):
            a1 = _dot(h_ref[...], w1_ref[s], 1, 0)
            a3 = _dot(h_ref[...], w3_ref[s], 1, 0)
            a1_ref[s] = a1.astype(a1_ref.dtype)
            a3_ref[s] = a3.astype(a3_ref.dtype)
            act_ref[s] = (_silu(a1) * a3).astype(act_ref.dtype)

    wspec = pl.BlockSpec((NS, D, WS), lambda i: (0, 0, 0))
    ospec = pl.BlockSpec((NS, tm, WS), lambda i: (0, i, 0))
    return pl.pallas_call(
        step, name=name, grid=(M // tm,), in_specs=[pl.BlockSpec((tm, D), lambda i: (i, 0)), wspec, wspec],
        out_specs=[ospec, ospec, ospec],
        out_shape=[jax.ShapeDtypeStruct((NS, M, WS), BF16)] * 3, compiler_params=_cp("arbitrary"))(h, w1, w3)


def ffn_out_bwd_x(name, dff, w2, a1, a3, tm):
    M = dff.shape[0]

    def step(d_ref, w_ref, a1_ref, a3_ref, da1_ref, da3_ref):
        for s in range(NS):
            dact = _dot(d_ref[...], w_ref[s * WS:(s + 1) * WS, :], 1, 1)
            a1 = a1_ref[s].astype(F32)
            sig = jax.nn.sigmoid(a1)
            da3_ref[s] = (dact * (a1 * sig)).astype(da3_ref.dtype)
            da1_ref[s] = (dact * a3_ref[s].astype(F32) * (sig * (1.0 + a1 * (1.0 - sig)))).astype(da1_ref.dtype)

    aspec = pl.BlockSpec((NS, tm, WS), lambda i: (0, i, 0))
    return pl.pallas_call(
        step, name=name, grid=(M // tm,),
        in_specs=[pl.BlockSpec((tm, D), lambda i: (i, 0)), pl.BlockSpec(w2.shape, lambda i: (0, 0)), aspec, aspec],
        out_specs=[aspec, aspec],
        out_shape=[jax.ShapeDtypeStruct((NS, M, WS), BF16)] * 2, compiler_params=_cp("arbitrary"))(dff, w2, a1, a3)


def ff_in_bwd_x(name, da1, da3, w1, w3, tm):
    M = da1.shape[1]

    def step(d1_ref, d3_ref, w1_ref, w3_ref, o_ref):
        def product(d_ref, w_ref):
            acc = _dot(d_ref[0], w_ref[0], 1, 1)
            for s in range(1, NS):
                acc = acc + _dot(d_ref[s], w_ref[s], 1, 1)
            return acc

        o_ref[...] = product(d3_ref, w3_ref) + product(d1_ref, w1_ref)

    dspec = pl.BlockSpec((NS, tm, WS), lambda i: (0, i, 0))
    wspec = pl.BlockSpec((NS, D, WS), lambda i: (0, 0, 0))
    return pl.pallas_call(step, name=name, grid=(M // tm,), in_specs=[dspec, dspec, wspec, wspec],
                          out_specs=pl.BlockSpec((tm, D), lambda i: (i, 0)),
                          out_shape=jax.ShapeDtypeStruct((M, D), F32), compiler_params=_cp("arbitrary"))(da1, da3, w1, w3)


def ff_in_bwd_w(name, h, da3, tk):
    M = h.shape[0]

    def step(h_ref, d_ref, acc):
        for s in range(NS):
            acc[s] += _dot(h_ref[...], d_ref[s], 0, 0)

    return _token_sum(name, M // tk, [pl.BlockSpec((tk, D), lambda k: (k, 0)), pl.BlockSpec((NS, tk, WS), lambda k: (0, k, 0))],
                      (NS, D, WS), (NS, D, WS), step, (h, da3))


def ff_out_fwd(name, act3, w2, tm):
    M = act3.shape[1]

    def step(a_ref, w_ref, o_ref):
        acc = _dot(a_ref[0], w_ref[0:WS, :], 1, 0)
        for s in range(1, NS):
            acc = acc + _dot(a_ref[s], w_ref[s * WS:(s + 1) * WS, :], 1, 0)
        o_ref[...] = acc

    return _resident(name, M, tm, (act3, (NS, tm, WS), lambda i: (0, i, 0)), w2, (M, D), (tm, D), lambda i: (i, 0), step)


def _token_sum(name, nk, in_specs, out_shape, acc_shape, step, args):
    def body(*refs):
        o_ref, acc = refs[-2], refs[-1]
        k = pl.program_id(0)

        @pl.when(k == 0)
        def _():
            acc[...] = jnp.zeros_like(acc)

        step(*refs[:-2], acc)

        @pl.when(k == nk - 1)
        def _():
            o_ref[...] = acc[...].astype(o_ref.dtype)

    return pl.pallas_call(body, name=name, grid=(nk,), in_specs=in_specs,
                          out_specs=pl.BlockSpec(out_shape, lambda k, n=len(out_shape): (0,) * n),
                          out_shape=jax.ShapeDtypeStruct(out_shape, BF16), scratch_shapes=[pltpu.VMEM(acc_shape, F32)],
                          compiler_params=_cp("arbitrary"))(*args)


def ff_out_bwd_w(name, act3, dff, tk):
    M = dff.shape[0]

    def step(a_ref, d_ref, acc):
        for s in range(NS):
            acc[s * WS:(s + 1) * WS, :] += _dot(a_ref[s], d_ref[...], 0, 0)

    return _token_sum(name, M // tk, [pl.BlockSpec((NS, tk, WS), lambda k: (0, k, 0)), pl.BlockSpec((tk, D), lambda k: (k, 0))],
                      (NS * WS, D), (NS * WS, D), step, (act3, dff))


HALO = 8
CONV_ROWS = 128


def _fill(pad_ref, n, v=None):
    edge = jnp.zeros((HALO, 128), F32)
    pad_ref[0:HALO, :] = edge
    pad_ref[HALO + n:2 * HALO + n, :] = edge
    if v is not None:
        pad_ref[HALO:HALO + n, :] = v


def _tap(pad_ref, r0, k, rows=CONV_ROWS):
    return pad_ref[HALO + r0 + k - 2:HALO + r0 + k - 2 + rows, :]


def _conv_pre(pad_ref, r0, w_ref, b_ref):
    acc = _tap(pad_ref, r0, 0) * w_ref[0:1, :] + b_ref[...]
    for k in range(1, 5):
        acc = acc + _tap(pad_ref, r0, k) * w_ref[k:k + 1, :]
    return acc


def conv_fwd(name, proj, conv_w, conv_b, R):
    segs = ((0, LC), (LC, R))

    def body(x_ref, w_ref, b_ref, o_ref, xp):
        for (s, e) in segs:
            _fill(xp, e - s, x_ref[s:e, :])
            for r0 in range(0, e - s, CONV_ROWS):
                o_ref[s + r0:s + r0 + CONV_ROWS, :] = _silu(_conv_pre(xp, r0, w_ref, b_ref))

    return pl.pallas_call(
        body, name=name, grid=(12,),
        in_specs=[pl.BlockSpec((R, 128), lambda j: (0, C_XBC // 128 + j)),
                  pl.BlockSpec((8, 128), lambda j: (0, j)), pl.BlockSpec((1, 128), lambda j: (0, j))],
        out_specs=pl.BlockSpec((R, 128), lambda j: (0, j)),
        out_shape=jax.ShapeDtypeStruct((R, 1536), F32), scratch_shapes=[pltpu.VMEM((R - LC + 2 * HALO, 128), F32)],
        compiler_params=_cp("arbitrary"))(proj, conv_w, conv_b)


def conv_bwd(name, proj, conv_w, conv_b, d_f, d_b, d_skip, dproj, R):
    segs = ((0, LC), (LC, R))

    def body(x_ref, w_ref, b_ref, df_ref, db_ref, ds_ref, _, dx_ref, dw_ref, dbias_ref, xp, dp):
        j = pl.program_id(0)
        has_skip = (j < 8).astype(F32)
        dw = [jnp.zeros((8, 128), F32) for _ in range(5)]
        dbias = jnp.zeros((8, 128), F32)
        fold = lambda v: jnp.sum(v.reshape(CONV_ROWS // 8, 8, 128), axis=0)
        for (s, e) in segs:
            n = e - s
            _fill(xp, n, x_ref[s:e, :])
            _fill(dp, n)
            for r0 in range(0, n, CONV_ROWS):
                rows = slice(s + r0, s + r0 + CONV_ROWS)
                pre = _conv_pre(xp, r0, w_ref, b_ref)
                sig = jax.nn.sigmoid(pre)
                dy = df_ref[rows, :] + db_ref[rows, :]
                if s == LC:
                    dy = dy + ds_ref[r0:r0 + CONV_ROWS, :] * has_skip
                dpre = dy * (sig * (1.0 + pre * (1.0 - sig)))
                dp[HALO + r0:HALO + r0 + CONV_ROWS, :] = dpre
                dbias = dbias + fold(dpre)
            for r0 in range(0, n, CONV_ROWS):
                x = x_ref[s + r0:s + r0 + CONV_ROWS, :]
                dx = jnp.zeros_like(x)
                for k in range(5):
                    d = _tap(dp, r0, 4 - k)
                    dx = dx + d * w_ref[k:k + 1, :]
                    dw[k] = dw[k] + fold(d * x)
                dx_ref[s + r0:s + r0 + CONV_ROWS, :] = dx.astype(dx_ref.dtype)
        dw_ref[...] = jnp.zeros_like(dw_ref)
        for k in range(5):
            dw_ref[k:k + 1, :] = jnp.sum(dw[k], axis=0, keepdims=True)
        dbias_ref[...] = jnp.sum(dbias, axis=0, keepdims=True)

    pad = pltpu.VMEM((R - LC + 2 * HALO, 128), F32)
    return pl.pallas_call(
        body, name=name, grid=(12,),
        in_specs=[pl.BlockSpec((R, 128), lambda j: (0, C_XBC // 128 + j)),
                  pl.BlockSpec((8, 128), lambda j: (0, j)), pl.BlockSpec((1, 128), lambda j: (0, j)),
                  pl.BlockSpec((R, 128), lambda j: (0, j)), pl.BlockSpec((R, 128), lambda j: (0, j)),
                  pl.BlockSpec((R - LC, 128), lambda j: (0, jnp.minimum(j, 7))),
                  pl.BlockSpec(memory_space=pl.ANY)],
        out_specs=[pl.BlockSpec((R, 128), lambda j: (0, C_XBC // 128 + j)),
                   pl.BlockSpec((8, 128), lambda j: (0, j)), pl.BlockSpec((1, 128), lambda j: (0, j))],
        out_shape=[jax.ShapeDtypeStruct(dproj.shape, dproj.dtype), jax.ShapeDtypeStruct((8, 1536), F32),
                   jax.ShapeDtypeStruct((1, 1536), F32)], scratch_shapes=[pad, pad],
        input_output_aliases={6: 0}, compiler_params=_cp("arbitrary"))(proj, conv_w, conv_b, d_f, d_b, d_skip, dproj)


def _ssd_chunk(rev, dirn):
    cums = _cumsum_fn(rev)

    def f(xs, Bs, Cs, dt, alog, Hs):
        lane = lax.broadcasted_iota(jnp.int32, (1, 128), 1)
        sub = lax.broadcasted_iota(jnp.int32, (Q, 1), 0)
        r = lax.broadcasted_iota(jnp.int32, (Q, Q), 0)
        c = lax.broadcasted_iota(jnp.int32, (Q, Q), 1)
        mask = (r <= c) if rev else (r >= c)
        left = lane < 64
        a = dt * (-jnp.exp(alog))
        s = cums(a)
        sT, dtT = s.T, dt.T
        last_row = (sub == (0 if rev else Q - 1)).astype(F32)
        s_last = jnp.sum(s * last_row, axis=0, keepdims=True)
        G = [mm_nt(Cs[g], Bs[g]) for g in range(2)]
        M, es, wc, ed = [], [], [], []
        for h in range(NH):
            l = 16 * dirn + h
            oh_l = (lane == l).astype(F32)
            oh_s = (sub == l).astype(F32)
            s_col = jnp.sum(s * oh_l, axis=1, keepdims=True)
            dt_col = jnp.sum(dt * oh_l, axis=1, keepdims=True)
            s_row = jnp.sum(sT * oh_s, axis=0, keepdims=True)
            dt_row = jnp.sum(dtT * oh_s, axis=0, keepdims=True)
            sl = jnp.sum(s_last * oh_l, axis=1, keepdims=True)
            seg = jnp.where(mask, s_col - s_row, 0.0)
            lm = jnp.where(mask, jnp.exp(seg), 0.0)
            M.append(G[h // 8] * lm * dt_row)
            es.append(jnp.exp(s_col))
            wc.append(jnp.exp(sl - s_col) * dt_col)
            ed.append(jnp.exp(sl))
        Ys, Hn = [], []
        for j in range(8):
            g = j // 4
            xa = jnp.where(left, xs[j], 0.0)
            xb = jnp.where(left, 0.0, xs[j])
            yd = mm(M[2 * j], xa) + mm(M[2 * j + 1], xb)
            yo = mm(Cs[g], Hs[j]) * jnp.where(left, es[2 * j], es[2 * j + 1])
            Ys.append(yd + yo)
            st = mm_tn(Bs[g], xs[j] * jnp.where(left, wc[2 * j], wc[2 * j + 1]))
            Hn.append(Hs[j] * jnp.where(left, ed[2 * j], ed[2 * j + 1]) + st)
        return Ys, Hn

    return f


def _chunk_of(t, n, rev):
    if not rev:
        return t
    return jnp.where(t < 2, 1 - t, n + 1 - t)


def _cols128(ref, k, lead=()):
    return [ref[lead + (slice(None), slice(128 * j, 128 * (j + 1)))] for j in range(k)]


def ssd_fwd(name, xbc, dts, alog, n, dirs):
    nd = len(dirs)
    chunks = [_ssd_chunk(rev, dirn) for rev, dirn in dirs]

    def body(*refs):
        al_ref = refs[4 * nd]
        for d in range(nd):
            x_ref, b_ref, c_ref, dt_ref = refs[4 * d:4 * d + 4]
            y_ref, hs_ref = refs[4 * nd + 1 + 2 * d:4 * nd + 3 + 2 * d]
            h_scr = refs[4 * nd + 1 + 2 * nd + d]

            @pl.when(pl.program_id(0) == 0)
            def _(h_scr=h_scr):
                h_scr[...] = jnp.zeros_like(h_scr)

            hs_ref[0] = h_scr[...]
            Ys, Hn = chunks[d](_cols128(x_ref, 8), _cols128(b_ref, 2), _cols128(c_ref, 2), dt_ref[...], al_ref[...],
                               _cols128(h_scr, 8))
            for j in range(8):
                y_ref[:, 128 * j:128 * (j + 1)] = Ys[j]
                h_scr[:, 128 * j:128 * (j + 1)] = Hn[j]

    in_specs, out_specs, out_shape, args = [], [], [], []
    for (rev, _), dt in zip(dirs, dts):
        cm = lambda t, rev=rev: _chunk_of(t, n, rev)
        in_specs += [pl.BlockSpec((Q, 1024), lambda t, cm=cm: (cm(t), 0)), pl.BlockSpec((Q, 256), lambda t, cm=cm: (cm(t), 4)),
                     pl.BlockSpec((Q, 256), lambda t, cm=cm: (cm(t), 5)), pl.BlockSpec((Q, 128), lambda t, cm=cm: (cm(t), 0))]
        args += [xbc, xbc, xbc, dt]
        out_specs += [pl.BlockSpec((Q, 1024), lambda t, cm=cm: (cm(t), 0)),
                      pl.BlockSpec((1, Q, 1024), lambda t, cm=cm: (cm(t), 0, 0))]
        out_shape += [jax.ShapeDtypeStruct((n * Q, 1024), F32), jax.ShapeDtypeStruct((n, Q, 1024), F32)]
    res = pl.pallas_call(
        body, name=name, grid=(n,), in_specs=in_specs + [pl.BlockSpec((1, 128), lambda t: (0, 0))],
        out_specs=out_specs, out_shape=out_shape, scratch_shapes=[pltpu.VMEM((Q, 1024), F32)] * nd,
        compiler_params=_cp("arbitrary"))(*args, alog)
    return [res[2 * d:2 * d + 2] for d in range(nd)]


def ssd_bwd(name, xbc, dts, alog, hss, dy, n, dirs):
    nd = len(dirs)
    chunks = [_ssd_chunk(rev, dirn) for rev, dirn in dirs]

    def body(*refs):
        tt = pl.program_id(0)
        al_ref = refs[6 * nd]
        for d, (rev, _) in enumerate(dirs):
            x_ref, b_ref, c_ref, dt_ref, hs_ref, dy_ref = refs[6 * d:6 * d + 6]
            dx_ref, ddt_ref, dal_ref = refs[6 * nd + 1 + 3 * d:6 * nd + 4 + 3 * d]
            dh_scr = refs[6 * nd + 1 + 3 * nd + d]
            ch = _chunk_of(n - 1 - tt, n, rev)

            @pl.when(tt == 0)
            def _(dh_scr=dh_scr):
                dh_scr[...] = jnp.zeros_like(dh_scr)

            live = (ch >= 2).astype(F32)
            dYs = [v * live for v in _cols128(dy_ref, 8)]
            _, vjp_fn = jax.vjp(chunks[d], _cols128(x_ref, 8), _cols128(b_ref, 2), _cols128(c_ref, 2), dt_ref[...],
                                al_ref[...], _cols128(hs_ref, 8, (0,)))
            dxs, dBs, dCs, ddt, dal, dHs = vjp_fn((dYs, _cols128(dh_scr, 8)))
            for j in range(8):
                dx_ref[:, 128 * j:128 * (j + 1)] = dxs[j]
                dh_scr[:, 128 * j:128 * (j + 1)] = dHs[j]
            for g in range(2):
                dx_ref[:, 1024 + 128 * g:1024 + 128 * (g + 1)] = dBs[g]
                dx_ref[:, 1280 + 128 * g:1280 + 128 * (g + 1)] = dCs[g]
            ddt_ref[...] = ddt

            @pl.when(tt == 0)
            def _(dal_ref=dal_ref, dal=dal):
                dal_ref[...] = dal

            @pl.when(tt > 0)
            def _(dal_ref=dal_ref, dal=dal):
                dal_ref[...] += dal

    in_specs, out_specs, out_shape, args = [], [], [], []
    for (rev, _), dt, hs in zip(dirs, dts, hss):
        cm = lambda t, rev=rev: _chunk_of(n - 1 - t, n, rev)
        in_specs += [pl.BlockSpec((Q, 1024), lambda t, cm=cm: (cm(t), 0)), pl.BlockSpec((Q, 256), lambda t, cm=cm: (cm(t), 4)),
                     pl.BlockSpec((Q, 256), lambda t, cm=cm: (cm(t), 5)), pl.BlockSpec((Q, 128), lambda t, cm=cm: (cm(t), 0)),
                     pl.BlockSpec((1, Q, 1024), lambda t, cm=cm: (cm(t), 0, 0)),
                     pl.BlockSpec((Q, 1024), lambda t, cm=cm: (jnp.maximum(cm(t) - 2, 0), 0))]
        args += [xbc, xbc, xbc, dt, hs, dy]
        out_specs += [pl.BlockSpec((Q, 1536), lambda t, cm=cm: (cm(t), 0)), pl.BlockSpec((Q, 128), lambda t, cm=cm: (cm(t), 0)),
                      pl.BlockSpec((1, 128), lambda t: (0, 0))]
        out_shape += [jax.ShapeDtypeStruct((n * Q, 1536), F32), jax.ShapeDtypeStruct((n * Q, 128), F32),
                      jax.ShapeDtypeStruct((1, 128), F32)]
    res = pl.pallas_call(
        body, name=name, grid=(n,), in_specs=in_specs + [pl.BlockSpec((1, 128), lambda t: (0, 0))],
        out_specs=out_specs, out_shape=out_shape, scratch_shapes=[pltpu.VMEM((Q, 1024), F32)] * nd,
        compiler_params=_cp("arbitrary"))(*args, alog)
    return [res[3 * d:3 * d + 3] for d in range(nd)]


def f_norm0(c, x, g0, b0, sc, sh, is_ctx):
    x0 = _ln(jnp.where(is_ctx > 0.5, c, x), g0, b0)
    return x0, x0 * (1.0 + sc) + sh


def f_dt(raw, bias):
    z = split4(raw)[0] + bias
    dt = jnp.maximum(z, 0.0) + jnp.log1p(jnp.exp(-jnp.abs(z)))
    return dt, dt


def f_gated_norm(yf, yb, xs, z, dcol, g):
    h = (yf + yb + xs * dcol) * _silu(z)
    return (h * lax.rsqrt(jnp.mean(h * h, axis=-1, keepdims=True) + LN_EPS) * g,)


def f_gmlp(uv, gmg, gmb, *wb):
    ws, bs = wb[:8], wb[8:]
    u, v = split2(uv)
    vn = split8(_ln(_gelu(v), gmg, gmb))
    mixed = concat8(tuple(mm(ws[g], vn[g]) + bs[g] for g in range(8)))
    return (_gelu(u) * mixed,)


def f_merge(ps, pg, gates, bg):
    gs, gg = split2(jax.nn.sigmoid(gates + bg))
    return (gs * ps + gg * pg,)


def f_merge_out(ps, pg, gates, bg, w_out):
    return (mm(f_merge(ps, pg, gates, bg)[0], w_out),)


def f_project_merge(yn, y_gm, gates, w_ssd, w_gm, bg):
    ps, pg = mm(yn, w_ssd), mm(y_gm, w_gm)
    return f_merge(ps, pg, gates, bg) + (ps, pg)


def f_res1(x0, out, g1, lg, lb, sc, sh):
    x1 = _ln(ALPHA * x0 + g1 * out, lg, lb)
    return x1, x1 * (1.0 + sc) + sh


def f_out_res1(x0, merged, w_out, *p):
    out = mm(merged, w_out)
    return f_res1(x0, out, *p) + (out,)


def f_res2_loss(x1, ff, tgt, g2, lg, lb):
    x2 = _ln(ALPHA * x1 + g2 * ff, lg, lb)
    e = x2 - tgt
    return (0.5 * jnp.sum(jnp.mean(e * e, axis=-1, keepdims=True), axis=0, keepdims=True),)


def _row_tile(M):
    return 544 if M % 544 == 0 else (512 if M % 512 == 0 else M)


def core(ctx, x, tgt, mod_x, mod_c, X, S):
    L = x.shape[0]
    R = LC + L
    n = R // Q
    T = 256
    nt, ntl = R // T, L // T
    tmR, tmL = _row_tile(R), _row_tile(L)
    tmS = 1024 if L % 1024 == 0 else tmL
    tkR = 256 if R % 512 else 512
    tkL = 512 if L % 512 == 0 else 256
    row = lambda v: v.reshape(1, -1)
    mx = [row(mod_x[k]) for k in range(6)]
    mc = [row(mod_c[k]) for k in range(6)]
    sel = lambda i: jnp.minimum(i, 1)
    sc1 = jnp.stack([mc[1], mx[1]])
    sh1 = jnp.stack([mc[0], mx[0]])
    ln0 = [(row(S['ln0_g']), None), (row(S['ln0_b']), None), (sc1, sel), (sh1, sel),
           (jnp.array([1.0, 0.0], F32).reshape(2, 1, 1), sel)]
    x_n0 = [(ctx, D, 0, -nt), (x, D, 0, -1)]

    x0, xm, *landed = stage_fwd("norm0_fwd", f_norm0, T, nt, x_n0, ln0, [('new', R, D, 0), ('new', R, D, 0, BF16)],
                                gather=X.w_in_block())
    w_in = X.w_in(*landed)
    proj = matmul("proj_fwd", xm, w_in, 'nn', tmR, PW // 2, 1024)
    conv_w8 = jnp.pad(S['conv_w'], ((0, 3), (0, 0)))
    conv_b = row(S['conv_b'])
    xbc = conv_fwd("conv_fwd", proj, conv_w8, conv_b, R)
    dt_bias = jnp.pad(S['dt_bias'].reshape(1, 32), ((0, 0), (0, 96)))
    alog = jnp.pad(S['a_log'].reshape(1, 32), ((0, 0), (0, 96)))
    x_dt = [(proj, 512, C_DT // 512, 0)]
    dt_f, dt_b = stage_fwd("dt_fwd", f_dt, T, nt, x_dt, [(dt_bias, None)], [('new', R, 128, 0), ('new', R, 128, 0)])
    directions = [(False, 0), (True, 1)]
    (y_f, hs_f), (y_b, hs_b) = ssd_fwd("ssd_fwd", xbc, [dt_f, dt_b], alog, n, directions)
    W = X.rest()
    dcol = jnp.repeat(S['d_skip'][0] + S['d_skip'][1], 64).reshape(1, D)
    x_gn = [(y_f, D, 0, 1), (y_b, D, 0, 1), (xbc, D, 0, 1), (proj, D, C_Z // D, 1)]
    p_gn = [(dcol, None), (row(S['ssd_norm_g']), None)]
    (yn,) = stage_fwd("gnorm_fwd", f_gated_norm, T, ntl, x_gn, p_gn, [('new', L, D, 0, BF16)])
    x_gm = [(proj, 2 * D, C_UV // (2 * D), LC // Q)]
    p_gm = ([(row(S['gm_norm_g']), None), (row(S['gm_norm_b']), None)]
            + [(S['w_spatial'][g], None) for g in range(8)] + [(S['b_spatial'][g].reshape(Q, 1), None) for g in range(8)])
    (y_gm,) = stage_fwd("gmlp_fwd", f_gmlp, Q, L // Q, x_gm, p_gm, [('new', L, D, 0, BF16)])
    x_pm = [(yn, D, 0, 0), (y_gm, D, 0, 0), (proj, 2 * D, C_GATE // (2 * D), 1)]
    p_mg = [(row(S['b_gate']), None)]
    p_pm = [(W['w_ssd_proj'], None), (W['w_gm_proj'], None)] + p_mg
    merged, p_ssd, p_g = stage_fwd("merge_fwd", f_project_merge, T, ntl, x_pm, p_pm,
                                   [('new', L, D, 0, BF16), ('new', L, D, 0), ('new', L, D, 0)])
    p_r1 = [(mx[2], None), (row(S['ln1_g']), None), (row(S['ln1_b']), None), (mx[4], None), (mx[3], None)]
    x1, hm, out = stage_fwd("res1_fwd", f_out_res1, T, ntl, [(x0, D, 0, 1), (merged, D, 0, 0)],
                            [(W['w_out'], None)] + p_r1, [('new', L, D, 0), ('new', L, D, 0, BF16), ('new', L, D, 0)])
    x_r1 = [(x0, D, 0, 1), (out, D, 0, 0)]
    a1, a3, act = ffn_in_fwd("ffn_in_fwd", hm, W['w_ff1'], W['w_ff3'], tmL)
    ff = ff_out_fwd("ff2_fwd", act, W['w_ff2'], tmL)
    x_r2 = [(x1, D, 0, 0), (ff, D, 0, 0), (tgt, D, 0, 0)]
    p_r2 = [(mx[5], None), (row(S['ln2_g']), None), (row(S['ln2_b']), None)]

    dx1_a, dff, dg2, dl2g, dl2b, loss = stage_bwd(
        "res2_bwd", f_res2_loss, T, ntl, x_r2, p_r2, [1.0],
        [('new', L, D, 0), ('new', L, D, 0, BF16), None], [True, True, True], primal=[(0, (1, 1))])
    da1, da3 = ffn_out_bwd_x("ffn_out_bwd_x", dff, W['w_ff2'], a1, a3, tmL)
    gw_ff2 = ff_out_bwd_w("ff2_bwd_w", act, dff, tkL)
    dhm = ff_in_bwd_x("ffn_in_bwd_x", da1, da3, W['w_ff1'], W['w_ff3'], tmL)
    gw_ff1 = ff_in_bwd_w("ff1_bwd_w", hm, da1, tkL)
    gw_ff3 = ff_in_bwd_w("ff3_bwd_w", hm, da3, tkL)
    X.grads('ffn', {'w_ff2': gw_ff2, 'w_ff1': gw_ff1, 'w_ff3': gw_ff3})
    dx0_a, dout, dg1, dl1g, dl1b, dsc2, dsh2 = stage_bwd(
        "res1_bwd", f_res1, T, ntl, x_r1, p_r1, [(dx1_a, D, 0, 0), (dhm, D, 0, 0)],
        [('new', L, D, 0), ('new', L, D, 0, BF16)], [True] * 5)
    lt, lq = -(LC // T), -(LC // Q)
    x_mg_b = [(p_ssd, D, 0, lt), (p_g, D, 0, lt), (proj, 2 * D, C_GATE // (2 * D), 0)]
    dp_ssd, dp_g, dproj, dbg, gw_out = stage_bwd(
        "merge_bwd", f_merge_out, T, nt, x_mg_b, p_mg + [(W['w_out'], None)], [(dout, D, 0, lt)],
        [('new', L, D, lt, BF16), ('new', L, D, lt, BF16), ('part', R, PW, 2 * D, C_GATE // (2 * D), 0, BF16)],
        [True, True])
    gw_out = gw_out.astype(BF16)
    dyn = matmul("pssd_bwd_x", dp_ssd, W['w_ssd_proj'], 'nt', tmS, 1024, 1024)
    gw_ssd = matmul("pssd_bwd_w", yn, dp_ssd, 'tn', 1024, 1024, tkL, BF16)
    dy_gm = matmul("pgm_bwd_x", dp_g, W['w_gm_proj'], 'nt', tmS, 1024, 1024)
    gw_gm = matmul("pgm_bwd_w", y_gm, dp_g, 'tn', 1024, 1024, tkL, BF16)
    X.grads('proj', {'w_out': gw_out, 'w_ssd_proj': gw_ssd, 'w_gm_proj': gw_gm})
    r_gm = stage_bwd("gmlp_bwd", f_gmlp, Q, n, [(proj, 2 * D, C_UV // (2 * D), 0)], p_gm, [(dy_gm, D, 0, lq)],
                     [('alias', dproj, 2 * D, C_UV // (2 * D), 0)], [True] * 18)
    dproj, dgmg, dgmb, dws, dbs = r_gm[0], r_gm[1], r_gm[2], r_gm[3:11], r_gm[11:19]
    x_gn_b = [(y_f, D, 0, 0), (y_b, D, 0, 0), (xbc, D, 0, 0), (proj, D, C_Z // D, 0)]
    dy, dskipx, dproj, ddcol, dng = stage_bwd(
        "gnorm_bwd", f_gated_norm, T, nt, x_gn_b, p_gn, [(dyn, D, 0, lt)],
        [('new', L, D, lt), None, ('new', L, D, lt), ('alias', dproj, D, C_Z // D, 0)], [True, True], rows=32)
    (dxbc_f, ddt_f, dal_f), (dxbc_b, ddt_b, dal_b) = ssd_bwd("ssd_bwd", xbc, [dt_f, dt_b], alog, [hs_f, hs_b], dy, n,
                                                             directions)
    dproj, ddtb = stage_bwd("dt_bwd", f_dt, T, nt, x_dt, [(dt_bias, None)],
                            [(ddt_f, 128, 0, 0), (ddt_b, 128, 0, 0)],
                            [('alias', dproj, 512, C_DT // 512, 0)], [True])
    dproj, dcw8, dcb = conv_bwd("conv_bwd", proj, conv_w8, conv_b, dxbc_f, dxbc_b, dskipx, dproj, R)
    gw_in = matmul("proj_bwd_w", xm, dproj, 'tn', 1024, PW // 2, tkR, BF16)
    X.grads('in', {'w_in': gw_in})
    dxm = matmul("proj_bwd_x", dproj, w_in, 'nt', R // 2 if R % 16 == 0 else R, 1024, 1024)
    grad_x, dl0g, dl0b, dsc1, dsh1 = stage_bwd(
        "norm0_bwd", f_norm0, T, nt, x_n0, ln0, [(dx0_a, D, 0, -1), (dxm, D, 0, 0)],
        [None, ('new', L, D, -1)], [True] * 4 + [False])

    zero = jnp.zeros((D,), F32)
    flat = lambda v: v.reshape(-1)
    small = {
        'loss': flat(loss), 'ln0_g': flat(dl0g), 'ln0_b': flat(dl0b),
        'dmod_x': jnp.concatenate([flat(dsh1[1]), flat(dsc1[1]), flat(dg1), flat(dsh2), flat(dsc2), flat(dg2)]),
        'dmod_c': jnp.concatenate([flat(dsh1[0]), flat(dsc1[0]), zero, zero, zero, zero]),
        'conv_w': flat(dcw8[:5]), 'conv_b': flat(dcb), 'dt_bias': flat(ddtb[:, :32]),
        'a_log': flat((dal_f + dal_b)[:, :32]),
        'd_skip': flat(jnp.tile(ddcol.reshape(1, NH, 64).sum(-1), (2, 1))),
        'ssd_norm_g': flat(dng), 'gm_norm_g': flat(dgmg), 'gm_norm_b': flat(dgmb),
        'w_spatial': flat(jnp.stack(dws)), 'b_spatial': flat(jnp.stack(dbs)), 'b_gate': flat(dbg),
        'ln1_g': flat(dl1g), 'ln1_b': flat(dl1b), 'ln2_g': flat(dl2g), 'ln2_b': flat(dl2b),
    }
    return grad_x, small


def _place():
    return lax.axis_index("x"), lax.axis_index("y"), lax.axis_index("c")


def allgather8(name, blk, hbm):
    space = pl.ANY if hbm else pltpu.VMEM

    def body(x_ref, out_ref, send_sems, recv_sems, local_sem):
        x, y, c = _place()
        me, sibling = (x, y, c), (x, y, 1 - c)
        chips = [(1 - x, y), (x, 1 - y), (1 - x, 1 - y)]

        def slot(px, py, pc):
            return out_ref.at[4 * px + 2 * py + pc]

        def copy(k, block, to, src=None):
            return pltpu.make_async_remote_copy(
                src_ref=slot(*block) if src is None else src, dst_ref=slot(*block),
                send_sem=send_sems.at[k], recv_sem=recv_sems.at[k], device_id=to, device_id_type=MESH)

        mine = pltpu.make_async_copy(x_ref, slot(*me), local_sem)
        mine.start()
        first = [copy(0, me, sibling, src=x_ref)]
        first += [copy(1 + j, me, (*chip, c), src=x_ref) for j, chip in enumerate(chips)]
        for cp in first:
            cp.start()
        passed = [copy(4 + j, (*chip, c), sibling) for j, chip in enumerate(chips)]
        for j, chip in enumerate(chips):
            copy(1 + j, (*chip, c), me).wait_recv()
            passed[j].start()
        copy(0, sibling, me).wait_recv()
        for j, chip in enumerate(chips):
            copy(4 + j, (*chip, 1 - c), me).wait_recv()
        for cp in first + passed:
            cp.wait_send()
        mine.wait()

    return pl.pallas_call(
        body, name=name, out_shape=jax.ShapeDtypeStruct((8,) + blk.shape, blk.dtype),
        in_specs=[pl.BlockSpec(memory_space=space)], out_specs=pl.BlockSpec(memory_space=space),
        scratch_shapes=[pltpu.SemaphoreType.DMA((7,)), pltpu.SemaphoreType.DMA((7,)), pltpu.SemaphoreType.DMA],
        compiler_params=pltpu.CompilerParams(vmem_limit_bytes=VMEM_LIMIT_V7X))(blk)


def _peers(place):
    x, y, c = place
    return [((1 - x) if k & 4 else x, (1 - y) if k & 2 else y, (1 - c) if k & 1 else c) for k in range(1, 8)]


def _slot(p):
    return 4 * p[0] + 2 * p[1] + p[2]


def plan_gather(place, srcs, lands):
    remote = [(s, l.at[_slot(place)], to) for s, l in zip(srcs, lands) for to in _peers(place)]
    return remote, [(s, l.at[_slot(place)]) for s, l in zip(srcs, lands)]


def plan_to_owner(place, srcs, lands):
    remote = [(s.at[2 * to[0] + to[1], to[2]], l.at[_slot(place)], to) for s, l in zip(srcs, lands) for to in _peers(place)]
    x, y, c = place
    return remote, [(s.at[2 * x + y, c], l.at[_slot(place)]) for s, l in zip(srcs, lands)]


def sequencer_exchange(name, collective_id, srcs, land_shapes, plan):
    n = len(srcs)
    src_refs = [jax.new_ref(a, memory_space=pltpu.MemorySpace.HBM) for a in srcs]
    land_refs = [jax.empty_ref(s, memory_space=pltpu.MemorySpace.HBM) for s in land_shapes]

    @pl.kernel(mesh=plsc.ScalarSubcoreMesh(axis_name="sequencer", num_cores=1), name=name,
               scratch_types=(pltpu.SemaphoreType.DMA((7 * n,)), pltpu.SemaphoreType.DMA((7 * n,)),
                              pltpu.SemaphoreType.DMA((n,))),
               compiler_params=pltpu.CompilerParams(collective_id=collective_id))
    def launch(send_sems, recv_sems, local_sems):
        place = _place()
        barrier = pltpu.get_barrier_semaphore()
        for to in _peers(place):
            pl.semaphore_signal(barrier, inc=1, device_id=to, device_id_type=MESH)
        pl.semaphore_wait(barrier, 7)
        remote, local = plan(place, src_refs, land_refs)
        mine = [pltpu.make_async_copy(s, d, local_sems.at[a]) for a, (s, d) in enumerate(local)]
        for cp in mine:
            cp.start()
        cps = [pltpu.make_async_remote_copy(src_ref=s, dst_ref=d, send_sem=send_sems.at[k], recv_sem=recv_sems.at[k],
                                            device_id=to, device_id_type=MESH) for k, (s, d, to) in enumerate(remote)]
        for cp in cps:
            cp.start()
        for cp in mine:
            cp.wait()
        for cp in cps:
            cp.wait()

    launch()
    return land_refs


def sibling_pair(name, hs):
    n = len(hs)

    def body(*refs):
        ins, outs = refs[:n], refs[n:2 * n]
        send_sems, recv_sems = refs[2 * n:]
        x, y, c = _place()
        cps = [pltpu.make_async_remote_copy(src_ref=outs[a].at[c], dst_ref=outs[a].at[c], send_sem=send_sems.at[a],
                                            recv_sem=recv_sems.at[a], device_id=(x, y, 1 - c), device_id_type=MESH)
               for a in range(n)]
        for cp in cps:
            cp.start()
        for a in range(n):
            pltpu.make_async_remote_copy(src_ref=outs[a].at[1 - c], dst_ref=outs[a].at[1 - c], send_sem=send_sems.at[a],
                                         recv_sem=recv_sems.at[a], device_id=(x, y, 1 - c),
                                         device_id_type=MESH).wait_recv()
        for cp in cps:
            cp.wait_send()

    any_spec = pl.BlockSpec(memory_space=pl.ANY)
    return pl.pallas_call(
        body, name=name, out_shape=[jax.ShapeDtypeStruct(h.shape, h.dtype) for h in hs],
        in_specs=[any_spec] * n, out_specs=[any_spec] * n, input_output_aliases={a: a for a in range(n)},
        scratch_shapes=[pltpu.SemaphoreType.DMA((n,)), pltpu.SemaphoreType.DMA((n,))])(*hs)


def owner_sum(name, land):
    _, r, w = land.shape
    T = r // 2

    def body(_, l_ref, o_ref):
        acc = l_ref[0].astype(F32)
        for j in range(1, 8):
            acc = acc + l_ref[j].astype(F32)
        o_ref[...] = acc

    grid_spec = pltpu.PrefetchScalarGridSpec(
        num_scalar_prefetch=1, grid=(2,),
        in_specs=[pl.BlockSpec((8, T, w), lambda i, at: (0, i, 0))],
        out_specs=pl.BlockSpec((None, T, w), lambda i, at: (at[0], i, 0)))
    at = jnp.stack([lax.axis_index("c")]).astype(jnp.int32)
    return pl.pallas_call(body, name=name, grid_spec=grid_spec, out_shape=jax.ShapeDtypeStruct((2, r, w), F32),
                          compiler_params=_cp("arbitrary"))(at, land)


W_IN_RUNS = ((0, 2, 1296, 376), (376, 3, 0, 1672), (2048, 1, 920, 752), (2800, 2, 0, 1296), (4096, 0, 0, 1024),
             (5120, 0, 1024, 648), (5768, 1, 0, 920))


def w_in_to_padded(name, g4):
    T = 128

    def body(g_ref, o_ref):
        o_ref[:, D_PROJ:PW] = jnp.zeros((T, PW - D_PROJ), o_ref.dtype)
        for (a, s, j0, w) in W_IN_RUNS:
            o_ref[:, a:a + w] = g_ref[s, :, j0:j0 + w]

    return pl.pallas_call(body, name=name, grid=(D // T,), in_specs=[pl.BlockSpec((4, T, 1672), lambda i: (0, i, 0))],
                          out_specs=pl.BlockSpec((T, PW), lambda i: (i, 0)),
                          out_shape=jax.ShapeDtypeStruct((D, PW), g4.dtype), compiler_params=_cp("arbitrary"))(g4)


def w_in_from_padded(name, gp):
    T = 128

    def body(g_ref, o_ref):
        for (a, s, j0, w) in W_IN_RUNS:
            o_ref[s, :, j0:j0 + w] = g_ref[:, a:a + w]

    return pl.pallas_call(body, name=name, grid=(D // T,), in_specs=[pl.BlockSpec((T, PW), lambda i: (i, 0))],
                          out_specs=pl.BlockSpec((4, T, 1672), lambda i: (0, i, 0)),
                          out_shape=jax.ShapeDtypeStruct((4, D, 1672), gp.dtype), compiler_params=_cp("arbitrary"))(gp)


def sum_devices(name, g):
    def body(g_ref, o_ref):
        acc = g_ref[0]
        for k in range(1, 8):
            acc = acc + g_ref[k]
        o_ref[...] = acc

    return pl.pallas_call(body, name=name, out_shape=jax.ShapeDtypeStruct(g.shape[1:], F32),
                          compiler_params=pltpu.CompilerParams(vmem_limit_bytes=VMEM_LIMIT_V7X))(g)


def adamw(name, w, g, m, v, T):
    r, wd = w.shape
    c1 = 1.0 - ADAM_B1 ** ADAM_STEP
    c2 = 1.0 - ADAM_B2 ** ADAM_STEP

    def body(w_ref, g_ref, m_ref, v_ref, d_ref, mo_ref, vo_ref):
        gv = g_ref[...]
        mn = ADAM_B1 * m_ref[...] + (1.0 - ADAM_B1) * gv
        vn = ADAM_B2 * v_ref[...] + (1.0 - ADAM_B2) * (gv * gv)
        d_ref[...] = -ADAM_LR * ((mn / c1) / (jnp.sqrt(vn / c2) + ADAM_EPS) + ADAM_WD * w_ref[...])
        mo_ref[...] = mn
        vo_ref[...] = vn

    spec = pl.BlockSpec((T, wd), lambda i: (i, 0))
    return pl.pallas_call(body, name=name, grid=(r // T,), in_specs=[spec] * 4, out_specs=[spec] * 3,
                          out_shape=[jax.ShapeDtypeStruct((r, wd), F32)] * 3, compiler_params=_cp("arbitrary"))(w, g, m, v)


BIG = {'w_in': (1024, 1672), 'w_ssd_proj': (256, 1024), 'w_gm_proj': (256, 1024), 'w_out': (256, 1024),
       'w_ff1': (1024, 704), 'w_ff3': (1024, 704), 'w_ff2': (704, 1024)}


class Flat:
    def __init__(self, segs):
        self.off, o = {}, 0
        for name, size in segs:
            self.off[name] = (o, size)
            o += -(-size // 128) * 128
        self.rows = -(-o // 1024) * 8

    def pack(self, vals):
        parts = []
        for name, (o, size) in self.off.items():
            v = vals[name].reshape(-1).astype(F32)
            parts.append(jnp.pad(v, (0, -(-size // 128) * 128 - size)))
        buf = jnp.concatenate(parts)
        return jnp.pad(buf, (0, self.rows * 128 - buf.shape[0])).reshape(self.rows, 128)

    def get(self, buf, name, shape=None):
        o, size = self.off[name]
        v = buf[o // 128:(o + size + 127) // 128].reshape(-1)[:size]
        return v if shape is None else v.reshape(shape)


PARTIALS = Flat([('loss', 1), ('ln0_g', D), ('ln0_b', D), ('dmod_x', 6 * D), ('dmod_c', 6 * D), ('conv_w', 5 * 1536),
                 ('conv_b', 1536), ('dt_bias', 32), ('a_log', 32), ('d_skip', 32), ('ssd_norm_g', D),
                 ('gm_norm_g', D), ('gm_norm_b', D), ('w_spatial', 8 * Q * Q), ('b_spatial', 8 * Q), ('b_gate', 2 * D),
                 ('ln1_g', D), ('ln1_b', D), ('ln2_g', D), ('ln2_b', D)])

WEIGHTS = ('c_ctx', 'ln0_g', 'ln0_b', 'w_ada', 'b_ada', 'w_in', 'conv_w', 'conv_b', 'dt_bias', 'a_log', 'd_skip',
           'ssd_norm_g', 'gm_norm_g', 'gm_norm_b', 'w_spatial', 'b_spatial', 'b_gate', 'w_ssd_proj', 'w_gm_proj',
           'w_out', 'ln1_g', 'ln1_b', 'w_ff1', 'w_ff3', 'w_ff2', 'ln2_g', 'ln2_b')
BIG_NAMES = tuple(BIG)
SMALL_NAMES = tuple(n for n in WEIGHTS if n not in BIG_NAMES and n != 'w_ada')


def kernel(x, c, ctx, c_ctx, ln0_g, ln0_b, w_ada, b_ada, w_in, conv_w, conv_b, dt_bias, a_log, d_skip, ssd_norm_g, gm_norm_g, gm_norm_b, w_spatial, b_spatial, b_gate, w_ssd_proj, w_gm_proj, w_out, ln1_g, ln1_b, w_ff1, w_ff3, w_ff2, ln2_g, ln2_b, loss_target, m_c_ctx, m_ln0_g, m_ln0_b, m_w_ada, m_b_ada, m_w_in, m_conv_w, m_conv_b, m_dt_bias, m_a_log, m_d_skip, m_ssd_norm_g, m_gm_norm_g, m_gm_norm_b, m_w_spatial, m_b_spatial, m_b_gate, m_w_ssd_proj, m_w_gm_proj, m_w_out, m_ln1_g, m_ln1_b, m_w_ff1, m_w_ff3, m_w_ff2, m_ln2_g, m_ln2_b, v_c_ctx, v_ln0_g, v_ln0_b, v_w_ada, v_b_ada, v_w_in, v_conv_w, v_conv_b, v_dt_bias, v_a_log, v_d_skip, v_ssd_norm_g, v_gm_norm_g, v_gm_norm_b, v_w_spatial, v_b_spatial, v_b_gate, v_w_ssd_proj, v_w_gm_proj, v_w_out, v_ln1_g, v_ln1_b, v_w_ff1, v_w_ff3, v_w_ff2, v_ln2_g, v_ln2_b):
    wts = dict(c_ctx=c_ctx, ln0_g=ln0_g, ln0_b=ln0_b, w_ada=w_ada, b_ada=b_ada, w_in=w_in, conv_w=conv_w, conv_b=conv_b,
               dt_bias=dt_bias, a_log=a_log, d_skip=d_skip, ssd_norm_g=ssd_norm_g, gm_norm_g=gm_norm_g,
               gm_norm_b=gm_norm_b, w_spatial=w_spatial, b_spatial=b_spatial, b_gate=b_gate, w_ssd_proj=w_ssd_proj,
               w_gm_proj=w_gm_proj, w_out=w_out, ln1_g=ln1_g, ln1_b=ln1_b, w_ff1=w_ff1, w_ff3=w_ff3, w_ff2=w_ff2,
               ln2_g=ln2_g, ln2_b=ln2_b)
    ms = dict(zip(WEIGHTS, (m_c_ctx, m_ln0_g, m_ln0_b, m_w_ada, m_b_ada, m_w_in, m_conv_w, m_conv_b, m_dt_bias, m_a_log,
                            m_d_skip, m_ssd_norm_g, m_gm_norm_g, m_gm_norm_b, m_w_spatial, m_b_spatial, m_b_gate,
                            m_w_ssd_proj, m_w_gm_proj, m_w_out, m_ln1_g, m_ln1_b, m_w_ff1, m_w_ff3, m_w_ff2, m_ln2_g,
                            m_ln2_b)))
    vs = dict(zip(WEIGHTS, (v_c_ctx, v_ln0_g, v_ln0_b, v_w_ada, v_b_ada, v_w_in, v_conv_w, v_conv_b, v_dt_bias, v_a_log,
                            v_d_skip, v_ssd_norm_g, v_gm_norm_g, v_gm_norm_b, v_w_spatial, v_b_spatial, v_b_gate,
                            v_w_ssd_proj, v_w_gm_proj, v_w_out, v_ln1_g, v_ln1_b, v_w_ff1, v_w_ff3, v_w_ff2, v_ln2_g,
                            v_ln2_b)))
    px, py, pc = _place()
    shard = 2 * px + py
    dev = 2 * shard + pc
    take = lambda a, i, axis=0: lax.dynamic_index_in_dim(a, i, axis, keepdims=False)

    half = lambda n: take(wts[n][0].reshape(2, BIG[n][0] // 2, BIG[n][1]), pc).astype(BF16)

    pre = jnp.concatenate([c, jnp.pad(conv_w[0], ((0, 0), (0, D - 384))), jnp.zeros((2, D), F32)], axis=0)
    pre = allgather8("gather_cond", pre, False)
    conv_w_full = pre[0::2, 1:6, :384].transpose(1, 0, 2).reshape(5, 1536)
    a16 = jnp.concatenate([_silu(pre[:, 0, :]), _silu(c_ctx)[None], jnp.zeros((7, D), F32)], axis=0)
    mod = matmul("ada_fwd", a16, w_ada[0], 'nn', 16, 512, 1024)
    mod = mod + lax.dynamic_slice_in_dim(b_ada[0], shard * 1536, 1536)[None]
    mod = allgather8("gather_mod", mod, False)
    mod = jnp.concatenate([mod[0], mod[2], mod[4], mod[6]], axis=1)
    mod_x = take(mod, dev).reshape(6, D)
    mod_c = mod[8].reshape(6, D)

    def full(n, blocks):
        r, w = BIG[n]
        return blocks.reshape(4, r, w) if w != D else blocks.reshape(4 * r, w)

    class Exchanges:
        rest_names = BIG_NAMES[1:]

        def __init__(self):
            self.pending = []

        def w_in_block(self):
            return half('w_in')

        def w_in(self, blocks):
            w = w_in_to_padded("w_in_layout", full('w_in', blocks))
            halves = [half(n) for n in self.rest_names]
            halves[0], _ = lax.optimization_barrier((halves[0], blocks))
            lands = [jax.ShapeDtypeStruct((8,) + h.shape, BF16) for h in halves]
            self.rest_refs = sequencer_exchange("gather_rest", 1, halves, lands, plan_gather)
            return w

        def rest(self):
            return {n: full(n, r[...]) for n, r in zip(self.rest_names, self.rest_refs)}

        def grads(self, group, gs):
            if group == 'in':
                gs = {'w_in': w_in_from_padded("w_in_grad_layout", gs['w_in'])}
            blocks = [g.reshape(4, 2, BIG[n][0] // 2, BIG[n][1]) for n, g in gs.items()]
            lands = [jax.ShapeDtypeStruct((8,) + b.shape[2:], BF16) for b in blocks]
            refs = sequencer_exchange("grads_" + group, 2 + len(self.pending), blocks, lands, plan_to_owner)
            self.pending.append((tuple(gs), refs))

        def finish(self):
            names, halves = [], []
            for ns, refs in self.pending:
                names += ns
                halves += [owner_sum("grads_sum_" + n, r[...]) for n, r in zip(ns, refs)]
            return {n: h.reshape(BIG[n]) for n, h in zip(names, sibling_pair("grads_halves", halves))}

    S = dict(ln0_g=ln0_g, ln0_b=ln0_b, conv_w=conv_w_full, conv_b=conv_b[0], dt_bias=dt_bias[0], a_log=a_log[0],
             d_skip=d_skip[0], ssd_norm_g=ssd_norm_g[0], gm_norm_g=gm_norm_g[0], gm_norm_b=gm_norm_b[0],
             w_spatial=w_spatial[0], b_spatial=b_spatial[0], b_gate=b_gate[0], ln1_g=ln1_g[0], ln1_b=ln1_b[0],
             ln2_g=ln2_g[0], ln2_b=ln2_b[0])
    exchanges = Exchanges()
    grad_x, gsmall = core(ctx[0], x[0], loss_target[0], mod_x, mod_c, exchanges, S)

    parts = allgather8("gather_partials", PARTIALS.pack(gsmall), False)
    tot = sum_devices("partials_sum", parts)
    g_shards = exchanges.finish()
    g = {n: PARTIALS.get(tot, n) for n in ('ln0_g', 'ln0_b', 'conv_b', 'dt_bias', 'a_log', 'd_skip', 'ssd_norm_g',
                                           'gm_norm_g', 'gm_norm_b', 'w_spatial', 'b_spatial', 'b_gate', 'ln1_g',
                                           'ln1_b', 'ln2_g', 'ln2_b')}
    loss = PARTIALS.get(tot, 'loss', ())
    dmod_c = PARTIALS.get(tot, 'dmod_c')
    g['b_ada'] = PARTIALS.get(tot, 'dmod_x') + dmod_c
    g['conv_w'] = lax.dynamic_slice_in_dim(PARTIALS.get(tot, 'conv_w', (5, 1536)), shard * 384, 384, axis=1)
    o, size = PARTIALS.off['dmod_x']
    dmod_rows = parts[:, o // 128:(o + size) // 128].reshape(8, size)
    dm = jnp.concatenate([dmod_rows, dmod_c[None], jnp.zeros((7, 6 * D), F32)], axis=0)
    dm = lax.dynamic_slice_in_dim(dm, shard * 1536, 1536, axis=1)
    g['w_ada'] = matmul("ada_bwd_w", a16, dm, 'tn', 1024, 512, 16)
    dm_c = jnp.concatenate([dm[8:9], jnp.zeros((15, 1536), F32)], axis=0)
    dc = matmul("ada_bwd_c", dm_c, w_ada[0], 'nt', 16, 1024, 512)
    dc = allgather8("gather_dcctx", dc, False)[:, 0, :]
    dc = ((dc[0] + dc[2]) + dc[4]) + dc[6]
    sg = jax.nn.sigmoid(c_ctx)
    g['c_ctx'] = dc * (sg * (1.0 + c_ctx * (1.0 - sg)))
    for n in BIG_NAMES:
        g[n] = g_shards[n]

    delta, new_m, new_v = {}, {}, {}
    for n in BIG_NAMES + ('w_ada',):
        r, w = wts[n].shape[1:]
        if w % 128:
            T = max(t for t in range(8, 257, 8) if w % t == 0)
            d_, m_, v_ = adamw("adamw_" + n, wts[n][0].T, g[n].T, ms[n][0].T, vs[n][0].T, T)
            delta[n], new_m[n], new_v[n] = d_.T, m_.T, v_.T
        else:
            T = 352 if n == 'w_ff2' else 256
            delta[n], new_m[n], new_v[n] = adamw("adamw_" + n, wts[n][0], g[n], ms[n][0], vs[n][0], T)
    lay = Flat([(n, wts[n].size) for n in SMALL_NAMES])
    d_, m_, v_ = adamw("adamw_small", lay.pack(wts), lay.pack(g), lay.pack(ms), lay.pack(vs), lay.rows)
    for n in SMALL_NAMES:
        delta[n], new_m[n], new_v[n] = (lay.get(b, n) for b in (d_, m_, v_))

    shp = lambda d: [d[n].reshape(wts[n].shape) for n in WEIGHTS]
    return (loss, grad_x[None], *shp(g), *shp(delta), *shp(new_m), *shp(new_v))
```
